```python
import math
import jax
import jax.numpy as jnp
from jax import lax
import numpy as np

D_MODEL = 1024
BATCH = 2
SEQ = 8192
DEPTH = 1
DEC_BATCH = 32
DEC_SEQ = 1
PAST_LEN = 16384
PAGE_SIZE = 128

HEAD_DIM_A = 64
N_HEADS_A = 8
DILATED_GROUPS = ((128, 1), (512, 4), (2048, 16))
N_GROUPS_A = len(DILATED_GROUPS)
D_GROUP_A = N_HEADS_A * HEAD_DIM_A
D_A = N_GROUPS_A * D_GROUP_A
BAND_BLOCK = 128
ROPE_THETA = 10000.0

HEAD_DIM_B = 64
N_HEADS_B = D_MODEL // HEAD_DIM_B
D_B = N_HEADS_B * HEAD_DIM_B
DECAY_LORA = 64
AAA_LORA = 64
GATE_LORA = 160
D_SHIFT_B = 3 * D_B + DECAY_LORA + AAA_LORA + GATE_LORA
SPLIT_B = (D_B, 2 * D_B, 3 * D_B, 3 * D_B + DECAY_LORA, 3 * D_B + DECAY_LORA + AAA_LORA)
LN_X_EPS = 64e-5

D_IN = 3 * D_A + D_SHIFT_B + 2 * D_MODEL

N_EXPERTS = 64
TOP_K = 8
N_EXPERT_GROUPS = 8
TOPK_GROUPS = 4
D_EXPERT = 256
ROUTED_SCALE = 2.5
EXPERT_BLOCK = 128
NORM_EPS = 1e-6

kernel_name = 'hybrid_dilated_rwkv7_moe_step'


def rms_norm(x, gain):
    xf = x.astype(jnp.float32)
    y = xf * lax.rsqrt(jnp.mean(xf * xf, axis=-1, keepdims=True) + NORM_EPS)
    return (y * gain.astype(jnp.float32)).astype(x.dtype)


def rope(x, pos):
    half = x.shape[-1] // 2
    inv_freq = ROPE_THETA ** (-jnp.arange(half, dtype=jnp.float32) / half)
    ang = pos.astype(jnp.float32)[:, None] * inv_freq[None, :]
    cos = jnp.cos(ang)[None, :, None, :]
    sin = jnp.sin(ang)[None, :, None, :]
    xf = x.astype(jnp.float32)
    x1, x2 = xf[..., :half], xf[..., half:]
    return jnp.concatenate([x1 * cos - x2 * sin, x2 * cos + x1 * sin], axis=-1).astype(x.dtype)


def band_attention(q, k, v, span):
    n, L, H, hd = q.shape
    nb = -(-L // BAND_BLOCK)
    Lp = nb * BAND_BLOCK
    pad = ((0, 0), (0, Lp - L), (0, 0), (0, 0))
    blocks = lambda t: jnp.pad(t, pad).reshape(n, nb, BAND_BLOCK, H, hd).astype(jnp.float32)
    qb, kb, vb = blocks(q), blocks(k), blocks(v)

    def with_prev(t):
        prev = jnp.pad(t[:, :-1], ((0, 0), (1, 0), (0, 0), (0, 0), (0, 0)))
        return jnp.concatenate([prev, t], axis=2)

    kband, vband = with_prev(kb), with_prev(vb)
    s = jnp.einsum('nbqhd,nbkhd->nbhqk', qb, kband) * (hd ** -0.5)
    qi = np.arange(BAND_BLOCK)[:, None]
    ki = np.arange(2 * BAND_BLOCK)[None, :]
    dist = qi + BAND_BLOCK - ki
    key_pos = np.arange(nb)[:, None, None] * BAND_BLOCK - BAND_BLOCK + ki[None]
    mask = (dist >= 0)[None] & (dist <= span)[None] & (key_pos >= 0)
    s = jnp.where(mask[None, :, None], s, -jnp.inf)
    lse = jax.nn.logsumexp(s, axis=-1)
    p = jnp.exp(s - lse[..., None])
    o = jnp.einsum('nbhqk,nbkhd->nbqhd', p, vband).reshape(n, Lp, H, hd)[:, :L]
    lse = lse.transpose(0, 1, 3, 2).reshape(n, Lp, H)[:, :L]
    return o, lse


def dilated_attention_prompt(q, k, v, window, dilation):
    b, S, H, hd = q.shape
    L = S // dilation

    def by_residue(t):
        return t.reshape(b, L, dilation, H, hd).transpose(0, 2, 1, 3, 4).reshape(b * dilation, L, H, hd)

    o, lse = band_attention(by_residue(q), by_residue(k), by_residue(v), window // dilation)
    o = o.reshape(b, dilation, L, H, hd).transpose(0, 2, 1, 3, 4).reshape(b, S, H, hd)
    lse = lse.reshape(b, dilation, L, H).transpose(0, 2, 1, 3).reshape(b, S, H)
    return o, lse


def dilated_attention_sample(q, k_new, v_new, buf, window, dilation):
    Wb, T, hd = buf.shape[1], q.shape[1], q.shape[-1]
    k_all = jnp.concatenate([buf[:, :, 0], k_new], axis=1)
    v_all = jnp.concatenate([buf[:, :, 1], v_new], axis=1)
    idx = Wb + np.arange(T)[:, None] - np.arange(window // dilation + 1)[None, :] * dilation
    valid = idx >= 0
    idx = np.maximum(idx, 0)
    kg = k_all[:, idx].astype(jnp.float32)
    vg = v_all[:, idx].astype(jnp.float32)
    s = jnp.einsum('bthd,btnhd->bthn', q.astype(jnp.float32), kg) * (hd ** -0.5)
    s = jnp.where(valid[None, :, None, :], s, -jnp.inf)
    lse = jax.nn.logsumexp(s, axis=-1)
    p = jnp.exp(s - lse[..., None])
    return jnp.einsum('bthn,btnhd->bthd', p, vg), lse


def combine_groups(outs, lses):
    w = jax.nn.softmax(jnp.stack(lses, axis=0), axis=0)
    return jnp.sum(w[..., None] * jnp.stack(outs, axis=0), axis=0)


def rwkv7_mixer(feat, prev_row, wkv0, p):
    b, t, _ = feat.shape
    prev = jnp.concatenate([prev_row[:, None].astype(feat.dtype), feat[:, :-1]], axis=1)
    xs = feat + p['mu_b'] * (prev - feat)
    r, k, v, xw, xa, xg = jnp.split(xs, SPLIT_B, axis=-1)
    w_log = -jax.nn.softplus(-(p['w0_b'] + jnp.tanh(xw) @ p['w_w2_b'])) - 0.5
    a = jax.nn.sigmoid(p['a0_b'] + xa @ p['w_a2_b'])
    g = jax.nn.sigmoid(xg) @ p['w_g2_b']
    heads = lambda z: z.reshape(b, t, N_HEADS_B, HEAD_DIM_B).astype(jnp.float32)
    decay = jnp.exp(-jnp.exp(heads(w_log)))
    kk = heads(k * p['k_k_b'])
    kk = kk / jnp.maximum(jnp.sqrt(jnp.sum(kk * kk, axis=-1, keepdims=True)), 1e-12)
    k_h = heads(k * (1 + (a - 1) * p['k_a_b']))
    r_h, v_h, a_h = heads(r), heads(v), heads(a)
    aa, bb = -kk, kk * a_h

    def step(S, inp):
        r_t, w_t, k_t, v_t, a_t, b_t = inp
        sa = jnp.einsum('bhij,bhj->bhi', S, a_t)
        S = S * w_t[:, :, None, :] + sa[..., None] * b_t[:, :, None, :] + v_t[..., None] * k_t[:, :, None, :]
        return S, jnp.einsum('bhij,bhj->bhi', S, r_t)

    seq_first = lambda z: jnp.moveaxis(z, 1, 0)
    S_fin, y = lax.scan(step, wkv0.astype(jnp.float32),
                        (seq_first(r_h), seq_first(decay), seq_first(k_h), seq_first(v_h), seq_first(aa), seq_first(bb)))
    y = jnp.moveaxis(y, 0, 1)
    mean = jnp.mean(y, axis=-1, keepdims=True)
    var = jnp.mean(jnp.square(y - mean), axis=-1, keepdims=True)
    y = ((y - mean) * lax.rsqrt(var + LN_X_EPS)).reshape(b, t, D_B)
    y = y * p['ln_x_w_b'].astype(jnp.float32) + p['ln_x_b_b'].astype(jnp.float32)
    bonus = jnp.sum(r_h * k_h * p['r_k_b'].astype(jnp.float32), axis=-1, keepdims=True) * v_h
    y = y + bonus.reshape(b, t, D_B)
    return (y.astype(feat.dtype) * g), S_fin.astype(feat.dtype), feat[:, -1]


def route(h, w_router, router_bias):
    n = h.shape[0]
    scores = jax.nn.sigmoid(h.astype(jnp.float32) @ w_router.astype(jnp.float32))
    choice = scores + router_bias.astype(jnp.float32)
    grp = choice.reshape(n, N_EXPERT_GROUPS, N_EXPERTS // N_EXPERT_GROUPS)
    grp_score = jnp.sum(lax.top_k(grp, 2)[0], axis=-1)
    _, top_groups = lax.top_k(grp_score, TOPK_GROUPS)
    group_mask = jnp.any(jax.nn.one_hot(top_groups, N_EXPERT_GROUPS, dtype=jnp.bool_), axis=1)
    expert_mask = jnp.repeat(group_mask, N_EXPERTS // N_EXPERT_GROUPS, axis=-1)
    _, idx = lax.top_k(jnp.where(expert_mask, choice, -jnp.inf), TOP_K)
    w = jnp.take_along_axis(scores, idx, axis=-1)
    w = w / jnp.sum(w, axis=-1, keepdims=True) * ROUTED_SCALE
    return idx, w


def routed_experts(h, idx, wts, w_gate, w_up, w_down):
    n, d = h.shape
    A = n * TOP_K
    flat_e = idx.reshape(A)
    order = jnp.argsort(flat_e)
    e_sorted = flat_e[order]
    counts = jnp.zeros((N_EXPERTS,), jnp.int32).at[flat_e].add(1)
    start = jnp.cumsum(counts) - counts
    padded = (counts + EXPERT_BLOCK - 1) // EXPERT_BLOCK * EXPERT_BLOCK
    pend = jnp.cumsum(padded)
    pstart = pend - padded
    dest = pstart[e_sorted] + (jnp.arange(A, dtype=jnp.int32) - start[e_sorted])
    n_blocks = -(-A // EXPERT_BLOCK) + N_EXPERTS
    slot_tok = jnp.full((n_blocks * EXPERT_BLOCK,), n, jnp.int32).at[dest].set((order // TOP_K).astype(jnp.int32))
    slot_w = jnp.zeros((n_blocks * EXPERT_BLOCK,), h.dtype).at[dest].set(wts.reshape(A)[order].astype(h.dtype))
    block_start = jnp.arange(n_blocks, dtype=jnp.int32) * EXPERT_BLOCK
    block_e = jnp.minimum(jnp.searchsorted(pend, block_start, side='right'), N_EXPERTS - 1)
    h_pad = jnp.concatenate([h, jnp.zeros((1, d), h.dtype)], axis=0)

    def one_block(args):
        tok, e = args
        xb = h_pad[tok]
        return (jax.nn.silu(xb @ w_gate[e]) * (xb @ w_up[e])) @ w_down[e]

    yb = lax.map(one_block, (slot_tok.reshape(n_blocks, EXPERT_BLOCK), block_e))
    y = jnp.zeros((n + 1, d), h.dtype).at[slot_tok].add(yb.reshape(-1, d) * slot_w[:, None])
    return y[:n]


def moe_ffn(h, p):
    idx, wts = route(h, p['w_router'], p['router_bias'])
    shared = (jax.nn.silu(h @ p['w_s_gate']) * (h @ p['w_s_up'])) @ p['w_s_down']
    return shared + routed_experts(h, idx, wts, p['w_e_gate'], p['w_e_up'], p['w_e_down'])


def layer_forward(x, c, pos, a_bufs, wkv0, shift0, p):
    b, t, d = x.shape
    mod = jax.nn.silu(c) @ p['w_ada'] + p['b_ada']
    shift1, scale1, gate1, shift2, scale2, gate2 = jnp.split(mod[:, None, :], 6, axis=-1)

    h = rms_norm(x, p['norm_pre_mix']) * (1 + scale1) + shift1
    proj = h @ p['w_in']
    qkv = proj[..., :3 * D_A].reshape(b, t, 3, N_GROUPS_A, N_HEADS_A, HEAD_DIM_A)
    feat_b = proj[..., 3 * D_A:3 * D_A + D_SHIFT_B]
    gate_a, gate_b = jnp.split(proj[..., 3 * D_A + D_SHIFT_B:], 2, axis=-1)

    outs, lses, new_a = [], [], []
    for gi, (window, dilation) in enumerate(DILATED_GROUPS):
        q = rope(qkv[:, :, 0, gi], pos)
        k = rope(qkv[:, :, 1, gi], pos)
        v = qkv[:, :, 2, gi]
        if a_bufs is None:
            o, l = dilated_attention_prompt(q, k, v, window, dilation)
            keep = min(window, t)
            new_a.append(jnp.stack([k[:, t - keep:], v[:, t - keep:]], axis=2))
        else:
            o, l = dilated_attention_sample(q, k, v, a_bufs[gi], window, dilation)
            new_a.append(jnp.stack([k, v], axis=2))
        outs.append(o)
        lses.append(l)
    o_a = combine_groups(outs, lses).astype(x.dtype).reshape(b, t, D_GROUP_A)
    o_b, new_wkv, new_shift = rwkv7_mixer(feat_b, shift0, wkv0, p)

    merged = jax.nn.sigmoid(gate_a) * (o_a @ p['w_a_out']) + jax.nn.sigmoid(gate_b) * (o_b @ p['w_b_out'])
    x = x + gate1 * rms_norm(merged @ p['w_out'], p['norm_post_mix'])

    h2 = rms_norm(x, p['norm_pre_ffn']) * (1 + scale2) + shift2
    y = moe_ffn(h2.reshape(b * t, d), p).reshape(b, t, d)
    x = x + gate2 * rms_norm(y, p['norm_post_ffn'])
    return x, tuple(new_a), new_wkv, new_shift


def setup_inputs(seed: int = 0) -> dict:
    key = jax.random.key(seed)
    ks = iter(jax.random.split(key, 64))
    f32 = jnp.float32
    L, D, E, F = DEPTH, D_MODEL, N_EXPERTS, D_EXPERT
    nrm = lambda shape, scale: jax.random.normal(next(ks), shape, f32) * scale
    unif = lambda shape, lo, hi: jax.random.uniform(next(ks), shape, f32, lo, hi)
    win = [min(w, PAST_LEN) for w, _ in DILATED_GROUPS]
    return {
        'x_prompt': nrm((BATCH, SEQ, D), 1.0),
        'x_sample': nrm((DEC_BATCH, DEC_SEQ, D), 1.0),
        'c_prompt': nrm((BATCH, D), 1.0),
        'c_sample': nrm((DEC_BATCH, D), 1.0),
        'cache_a1_kv': nrm((L, DEC_BATCH, win[0], 2, N_HEADS_A, HEAD_DIM_A), 1.0),
        'cache_a2_kv': nrm((L, DEC_BATCH, win[1], 2, N_HEADS_A, HEAD_DIM_A), 1.0),
        'cache_a3_kv': nrm((L, DEC_BATCH, win[2], 2, N_HEADS_A, HEAD_DIM_A), 1.0),
        'state_b_wkv': nrm((L, DEC_BATCH, N_HEADS_B, HEAD_DIM_B, HEAD_DIM_B), 0.3),
        'state_b_shift': nrm((L, DEC_BATCH, D_SHIFT_B), 1.0),
        'w_ada': nrm((L, D, 6 * D), 0.5 * D ** -0.5),
        'b_ada': nrm((L, 6 * D), 0.02),
        'norm_pre_mix': 1.0 + nrm((L, D), 0.1),
        'norm_post_mix': 1.0 + nrm((L, D), 0.1),
        'norm_pre_ffn': 1.0 + nrm((L, D), 0.1),
        'norm_post_ffn': 1.0 + nrm((L, D), 0.1),
        'w_in': nrm((L, D, D_IN), D ** -0.5),
        'w_a_out': nrm((L, D_GROUP_A, D), D_GROUP_A ** -0.5),
        'mu_b': unif((L, D_SHIFT_B), 0.0, 1.0),
        'w0_b': unif((L, D_B), -6.0, -1.0),
        'w_w2_b': nrm((L, DECAY_LORA, D_B), 0.1),
        'a0_b': nrm((L, D_B), 0.5),
        'w_a2_b': nrm((L, AAA_LORA, D_B), AAA_LORA ** -0.5),
        'w_g2_b': nrm((L, GATE_LORA, D_B), GATE_LORA ** -0.5),
        'k_k_b': 0.85 + nrm((L, D_B), 0.05),
        'k_a_b': 1.0 + nrm((L, D_B), 0.05),
        'r_k_b': nrm((L, N_HEADS_B, HEAD_DIM_B), 0.1),
        'ln_x_w_b': 1.0 + nrm((L, D_B), 0.1),
        'ln_x_b_b': nrm((L, D_B), 0.02),
        'w_b_out': nrm((L, D_B, D), D_B ** -0.5),
        'w_out': nrm((L, D, D), D ** -0.5),
        'w_router': nrm((L, D, E), D ** -0.5),
        'router_bias': nrm((L, E), 0.01),
        'w_e_gate': nrm((L, E, D, F), D ** -0.5),
        'w_e_up': nrm((L, E, D, F), D ** -0.5),
        'w_e_down': nrm((L, E, F, D), F ** -0.5),
        'w_s_gate': nrm((L, D, F), D ** -0.5),
        'w_s_up': nrm((L, D, F), D ** -0.5),
        'w_s_down': nrm((L, F, D), F ** -0.5),
    }


def reference(x_prompt, x_sample, c_prompt, c_sample, cache_a1_kv, cache_a2_kv, cache_a3_kv,
              state_b_wkv, state_b_shift, w_ada, b_ada, norm_pre_mix, norm_post_mix, norm_pre_ffn,
              norm_post_ffn, w_in, w_a_out, mu_b, w0_b, w_w2_b, a0_b, w_a2_b, w_g2_b, k_k_b, k_a_b,
              r_k_b, ln_x_w_b, ln_x_b_b, w_b_out, w_out, w_router, router_bias, w_e_gate, w_e_up,
              w_e_down, w_s_gate, w_s_up, w_s_down):
    pos_prompt = jnp.arange(SEQ, dtype=jnp.int32)
    pos_sample = PAST_LEN + jnp.arange(DEC_SEQ, dtype=jnp.int32)
    y_prompt, y_sample = x_prompt, x_sample
    prompt_states, sample_states = [], []
    for l in range(DEPTH):
        p = {'w_ada': w_ada[l], 'b_ada': b_ada[l], 'norm_pre_mix': norm_pre_mix[l],
             'norm_post_mix': norm_post_mix[l], 'norm_pre_ffn': norm_pre_ffn[l],
             'norm_post_ffn': norm_post_ffn[l], 'w_in': w_in[l], 'w_a_out': w_a_out[l],
             'mu_b': mu_b[l], 'w0_b': w0_b[l], 'w_w2_b': w_w2_b[l], 'a0_b': a0_b[l],
             'w_a2_b': w_a2_b[l], 'w_g2_b': w_g2_b[l], 'k_k_b': k_k_b[l], 'k_a_b': k_a_b[l],
             'r_k_b': r_k_b[l], 'ln_x_w_b': ln_x_w_b[l], 'ln_x_b_b': ln_x_b_b[l],
             'w_b_out': w_b_out[l], 'w_out': w_out[l], 'w_router': w_router[l],
             'router_bias': router_bias[l], 'w_e_gate': w_e_gate[l], 'w_e_up': w_e_up[l],
             'w_e_down': w_e_down[l], 'w_s_gate': w_s_gate[l], 'w_s_up': w_s_up[l],
             'w_s_down': w_s_down[l]}
        zero_wkv = jnp.zeros((BATCH, N_HEADS_B, HEAD_DIM_B, HEAD_DIM_B), x_prompt.dtype)
        zero_shift = jnp.zeros((BATCH, D_SHIFT_B), x_prompt.dtype)
        y_prompt, na_p, nwkv_p, nsh_p = layer_forward(y_prompt, c_prompt, pos_prompt, None, zero_wkv, zero_shift, p)
        y_sample, na_s, nwkv_s, nsh_s = layer_forward(
            y_sample, c_sample, pos_sample, (cache_a1_kv[l], cache_a2_kv[l], cache_a3_kv[l]),
            state_b_wkv[l], state_b_shift[l], p)
        prompt_states.append((na_p[0], na_p[1], na_p[2], nwkv_p, nsh_p))
        sample_states.append((na_s[0], na_s[1], na_s[2], nwkv_s, nsh_s))
    a1_p, a2_p, a3_p, wkv_p, shift_p = [jnp.stack(z, axis=0) for z in zip(*prompt_states)]
    a1_s, a2_s, a3_s, wkv_s, shift_s = [jnp.stack(z, axis=0) for z in zip(*sample_states)]
    return (y_prompt, y_sample, a1_p, a2_p, a3_p, wkv_p, shift_p, a1_s, a2_s, a3_s, wkv_s, shift_s)
```

```python
import functools
import math

import jax
import jax.numpy as jnp
import numpy as np
from jax import lax
from jax.experimental import pallas as pl
from jax.experimental.pallas import tpu as pltpu

F32 = jnp.float32
BF16 = jnp.bfloat16

D_MODEL = 1024
BATCH = 2
SEQ = 8192
DEPTH = 1
DEC_BATCH = 32
DEC_SEQ = 1
PAST_LEN = 16384

HEAD_DIM_A = 64
N_HEADS_A = 8
DILATED_GROUPS = ((128, 1), (512, 4), (2048, 16))
N_GROUPS_A = 3
D_GROUP_A = N_HEADS_A * HEAD_DIM_A
D_A = N_GROUPS_A * D_GROUP_A
D_QKV = 3 * D_A
BAND_BLOCK = 128
ROPE_THETA = 10000.0

HEAD_DIM_B = 64
N_HEADS_B = 16
D_B = 1024
DECAY_LORA = 64
AAA_LORA = 64
GATE_LORA = 160
D_SHIFT_B = 3 * D_B + DECAY_LORA + AAA_LORA + GATE_LORA
LN_X_EPS = 64e-5

N_EXPERTS = 64
TOP_K = 8
N_EXPERT_GROUPS = 8
TOPK_GROUPS = 4
D_EXPERT = 256
ROUTED_SCALE = 2.5
NORM_EPS = 1e-6

LANES = 128
WKV_CHUNK = 64
VMEM_LIMIT = 56 * 1024 * 1024


def _cparams(sem):
    return pltpu.CompilerParams(dimension_semantics=sem, vmem_limit_bytes=VMEM_LIMIT)


def _dot(a, b):
    return jnp.dot(a, b, preferred_element_type=F32)


def _dot_nt(a, b):
    return lax.dot_general(a, b, (((1,), (1,)), ((), ())), preferred_element_type=F32)


def _dot_tn(a, b):
    return lax.dot_general(a, b, (((0,), (0,)), ((), ())), preferred_element_type=F32)


def _rms(x, gain):
    return x * lax.rsqrt(jnp.mean(x * x, axis=-1, keepdims=True) + NORM_EPS) * gain


def _sigmoid(x):
    return 1.0 / (1.0 + jnp.exp(-x))


def _silu(x):
    return x * _sigmoid(x)


def _softplus(x):
    return jnp.maximum(x, 0.0) + jnp.log(1.0 + jnp.exp(-jnp.abs(x)))


def _mod_body(c_ref, w_ref, b_ref, o_ref):
    s = _silu(c_ref[...]).astype(BF16)
    o_ref[...] = _dot(s, w_ref[...].astype(BF16)) + b_ref[...]


def _mod_call(c_all, w_ada, b_ada):
    rows = c_all.shape[0]
    tn = 1536
    return pl.pallas_call(
        _mod_body,
        out_shape=jax.ShapeDtypeStruct((rows, 6 * D_MODEL), F32),
        grid=(6 * D_MODEL // tn,),
        in_specs=[pl.BlockSpec((rows, D_MODEL), lambda j: (0, 0)),
                  pl.BlockSpec((D_MODEL, tn), lambda j: (0, j)),
                  pl.BlockSpec((1, tn), lambda j: (0, j))],
        out_specs=pl.BlockSpec((rows, tn), lambda j: (0, j)),
        compiler_params=_cparams(("arbitrary",)),
        name="mod",
    )(c_all, w_ada, b_ada.reshape(1, -1))


def _inproj_body(x_ref, g_ref, sc_ref, sh_ref, cos_ref, sin_ref, wq_ref, wf_ref, wg_ref,
                 qkv_ref, feat_ref, gate_ref, tail_ref):
    x = x_ref[0]
    h = _rms(x, g_ref[...]) * (1.0 + sc_ref[0]) + sh_ref[0]
    hb = h.astype(BF16)
    p = _dot(hb, wq_ref[...])
    cos = cos_ref[...]
    sin = sin_ref[...]
    lane = lax.broadcasted_iota(jnp.int32, cos.shape, 1)
    first_half = (lane % HEAD_DIM_A) < (HEAD_DIM_A // 2)
    n_rot = 2 * D_A // LANES
    for c in range(n_rot):
        xc = p[:, c * LANES:(c + 1) * LANES]
        partner = jnp.where(first_half, pltpu.roll(xc, LANES - HEAD_DIM_A // 2, 1),
                            pltpu.roll(xc, HEAD_DIM_A // 2, 1))
        rc = xc * cos + partner * sin
        if c < D_A // LANES:
            rc = rc * (HEAD_DIM_A ** -0.5)
        qkv_ref[0, :, c * LANES:(c + 1) * LANES] = rc.astype(BF16)
        if c >= D_A // LANES:
            tail_ref[0, :, (c - D_A // LANES) * LANES:(c - D_A // LANES + 1) * LANES] = rc
    v = p[:, 2 * D_A:]
    qkv_ref[0, :, 2 * D_A:] = v.astype(BF16)
    tail_ref[0, :, D_A:] = v
    feat_ref[0] = _dot(hb, wf_ref[...])
    gate_ref[0] = _sigmoid(_dot(hb, wg_ref[...])).astype(BF16)


def _inproj_call(x, gain, scale, shift, cos_t, sin_t, wq, wf, wg, tm, tail_rows, mod_per_row):
    nb, t, _ = x.shape
    nt = t // tm
    tail_first = (t - tail_rows) // tm
    if mod_per_row:
        mod_spec = pl.BlockSpec((1, tm, D_MODEL), lambda b, i: (b, i, 0))
    else:
        mod_spec = pl.BlockSpec((1, 1, D_MODEL), lambda b, i: (b, 0, 0))
    resident = lambda shp: pl.BlockSpec(shp, lambda b, i: (0, 0), pipeline_mode=pl.Buffered(1))
    return pl.pallas_call(
        _inproj_body,
        out_shape=(jax.ShapeDtypeStruct((nb, t, D_QKV), BF16),
                   jax.ShapeDtypeStruct((nb, t, D_SHIFT_B), F32),
                   jax.ShapeDtypeStruct((nb, t, 2 * D_MODEL), BF16),
                   jax.ShapeDtypeStruct((nb, tail_rows, 2 * D_A), F32)),
        grid=(nb, nt),
        in_specs=[pl.BlockSpec((1, tm, D_MODEL), lambda b, i: (b, i, 0)),
                  pl.BlockSpec((1, D_MODEL), lambda b, i: (0, 0)),
                  mod_spec, mod_spec,
                  pl.BlockSpec((tm, LANES), lambda b, i: (i, 0)),
                  pl.BlockSpec((tm, LANES), lambda b, i: (i, 0)),
                  resident((D_MODEL, D_QKV)), resident((D_MODEL, D_SHIFT_B)),
                  resident((D_MODEL, 2 * D_MODEL))],
        out_specs=(pl.BlockSpec((1, tm, D_QKV), lambda b, i: (b, i, 0)),
                   pl.BlockSpec((1, tm, D_SHIFT_B), lambda b, i: (b, i, 0)),
                   pl.BlockSpec((1, tm, 2 * D_MODEL), lambda b, i: (b, i, 0)),
                   pl.BlockSpec((1, tm, 2 * D_A), lambda b, i: (b, jnp.maximum(i - tail_first, 0), 0))),
        compiler_params=_cparams(("arbitrary", "arbitrary")),
        name="inproj",
    )(x, gain.reshape(1, -1), scale, shift, cos_t, sin_t, wq, wf, wg)


def _attn_body(q_ref, kc_ref, kp_ref, vc_ref, vp_ref, o_ref, lse_ref):
    mb = pl.program_id(2)
    q = q_ref[0]
    k = jnp.concatenate([kp_ref[0], kc_ref[0]], axis=0)
    v = jnp.concatenate([vp_ref[0], vc_ref[0]], axis=0)
    qi = lax.broadcasted_iota(jnp.int32, (BAND_BLOCK, 2 * BAND_BLOCK), 0)
    ki = lax.broadcasted_iota(jnp.int32, (BAND_BLOCK, 2 * BAND_BLOCK), 1)
    dist = qi + BAND_BLOCK - ki
    mask = (dist >= 0) & (dist <= BAND_BLOCK) & ((ki >= BAND_BLOCK) | (mb > 0))
    lane_q = lax.broadcasted_iota(jnp.int32, (BAND_BLOCK, LANES), 1)
    lane_k = lax.broadcasted_iota(jnp.int32, (2 * BAND_BLOCK, LANES), 1)
    for hp in range(N_HEADS_A // 2):
        sl = slice(hp * LANES, (hp + 1) * LANES)
        qp, kp, vp = q[:, sl], k[:, sl], v[:, sl]
        o_pair = jnp.zeros((BAND_BLOCK, LANES), F32)
        lse_pair = jnp.zeros((BAND_BLOCK, LANES), F32)
        for sub in range(2):
            mq = (lane_q >= HEAD_DIM_A) if sub else (lane_q < HEAD_DIM_A)
            mk = (lane_k >= HEAD_DIM_A) if sub else (lane_k < HEAD_DIM_A)
            s = _dot_nt(jnp.where(mq, qp, jnp.zeros_like(qp)), kp)
            s = jnp.where(mask, s, -jnp.inf)
            mx = jnp.max(s, axis=1, keepdims=True)
            p = jnp.exp(s - mx)
            l = jnp.sum(p, axis=1, keepdims=True)
            pv = _dot(p.astype(BF16), jnp.where(mk, vp, jnp.zeros_like(vp)))
            o_pair = o_pair + pv / l
            lse_pair = jnp.where(mq, mx + jnp.log(l), lse_pair)
        o_ref[0, :, sl] = o_pair.astype(BF16)
        lse_ref[0, :, sl] = lse_pair


def _attn_call(qkv, gi, dil):
    b, s, _ = qkv.shape
    l = s // dil
    nb = l // BAND_BLOCK
    qkv_v = qkv.reshape(b, l, dil * D_QKV)
    per = D_QKV // D_GROUP_A
    blk = (1, BAND_BLOCK, D_GROUP_A)
    o, lse = pl.pallas_call(
        _attn_body,
        out_shape=(jax.ShapeDtypeStruct((b, l, dil * D_GROUP_A), BF16),
                   jax.ShapeDtypeStruct((b, l, dil * D_GROUP_A), F32)),
        grid=(b, dil, nb),
        in_specs=[pl.BlockSpec(blk, lambda bb, r, m: (bb, m, r * per + gi)),
                  pl.BlockSpec(blk, lambda bb, r, m: (bb, m, r * per + 3 + gi)),
                  pl.BlockSpec(blk, lambda bb, r, m: (bb, jnp.maximum(m - 1, 0), r * per + 3 + gi)),
                  pl.BlockSpec(blk, lambda bb, r, m: (bb, m, r * per + 6 + gi)),
                  pl.BlockSpec(blk, lambda bb, r, m: (bb, jnp.maximum(m - 1, 0), r * per + 6 + gi))],
        out_specs=(pl.BlockSpec(blk, lambda bb, r, m: (bb, m, r)),
                   pl.BlockSpec(blk, lambda bb, r, m: (bb, m, r))),
        compiler_params=_cparams(("arbitrary", "arbitrary", "arbitrary")),
        name=f"attn{gi}",
    )(qkv_v, qkv_v, qkv_v, qkv_v, qkv_v)
    return o.reshape(b, s, D_GROUP_A), lse.reshape(b, s, D_GROUP_A)


def _sattn_body(qkv_ref, b1_ref, b2_ref, b3_ref, o_ref):
    qkv = qkv_ref[0].astype(F32)
    bufs = (b1_ref, b2_ref, b3_ref)
    outs, lses = [], []
    for g in range(N_GROUPS_A):
        q = qkv[:, g * D_GROUP_A:(g + 1) * D_GROUP_A]
        kn = qkv[:, D_A + g * D_GROUP_A:D_A + (g + 1) * D_GROUP_A]
        vn = qkv[:, 2 * D_A + g * D_GROUP_A:2 * D_A + (g + 1) * D_GROUP_A]
        buf = bufs[g][0]
        kb, vb = buf[:, :D_GROUP_A], buf[:, D_GROUP_A:]
        prod = kb * q
        pnew = kn * q
        o_h, lse_h = [], []
        for h in range(N_HEADS_A):
            sl = slice(h * HEAD_DIM_A, (h + 1) * HEAD_DIM_A)
            s = jnp.sum(prod[:, sl], axis=1, keepdims=True)
            sn = jnp.sum(pnew[:, sl], axis=1, keepdims=True)
            m = jnp.maximum(jnp.max(s, axis=0, keepdims=True), sn)
            p = jnp.exp(s - m)
            pn = jnp.exp(sn - m)
            l = jnp.sum(p, axis=0, keepdims=True) + pn
            o = (jnp.sum(p * vb[:, sl], axis=0, keepdims=True) + pn * vn[:, sl]) / l
            o_h.append(o)
            lse_h.append(jnp.broadcast_to(m + jnp.log(l), (1, HEAD_DIM_A)))
        outs.append(jnp.concatenate(o_h, axis=1))
        lses.append(jnp.concatenate(lse_h, axis=1))
    mx = jnp.maximum(jnp.maximum(lses[0], lses[1]), lses[2])
    es = [jnp.exp(z - mx) for z in lses]
    o_ref[0] = (es[0] * outs[0] + es[1] * outs[1] + es[2] * outs[2]) / (es[0] + es[1] + es[2])


def _sattn_call(qkv_s, c1, c2, c3):
    n = qkv_s.shape[0]
    row = 2 * D_GROUP_A
    views = []
    for c, (_, dil) in zip((c1, c2, c3), DILATED_GROUPS):
        wb = c.shape[1]
        views.append(c.reshape(n, wb // dil, dil * row))
    buf_spec = pl.BlockSpec((1, BAND_BLOCK, row), lambda b: (b, 0, 0))
    return pl.pallas_call(
        _sattn_body,
        out_shape=jax.ShapeDtypeStruct((n, 1, D_GROUP_A), F32),
        grid=(n,),
        in_specs=[pl.BlockSpec((1, 1, D_QKV), lambda b: (b, 0, 0)), buf_spec, buf_spec, buf_spec],
        out_specs=pl.BlockSpec((1, 1, D_GROUP_A), lambda b: (b, 0, 0)),
        compiler_params=_cparams(("arbitrary",)),
        name="sattn",
    )(qkv_s.reshape(n, 1, D_QKV), *views)


def _rwkv_features(xs, w0, ww2, a0, wa2, wg2, k_a):
    r = xs[:, :D_B]
    k = xs[:, D_B:2 * D_B]
    v = xs[:, 2 * D_B:3 * D_B]
    xw = xs[:, 3 * D_B:3 * D_B + DECAY_LORA]
    xa = xs[:, 3 * D_B + DECAY_LORA:3 * D_B + DECAY_LORA + AAA_LORA]
    xg = xs[:, 3 * D_B + DECAY_LORA + AAA_LORA:]
    w_log = -_softplus(-(w0 + _dot(jnp.tanh(xw).astype(BF16), ww2.astype(BF16)))) - 0.5
    a = _sigmoid(a0 + _dot(xa.astype(BF16), wa2.astype(BF16)))
    g = _dot(_sigmoid(xg).astype(BF16), wg2.astype(BF16))
    k_h = k * (1.0 + (a - 1.0) * k_a)
    return r, k, v, w_log, a, g, k_h


def _head_norm(kk_h):
    nrm = jnp.sqrt(jnp.sum(kk_h * kk_h, axis=-1, keepdims=True))
    return kk_h / jnp.maximum(nrm, 1e-12)


def _wkv_finish_head(y, r_h, k_h, v_h, g_h, rk_h, lnw_h, lnb_h):
    mean = jnp.mean(y, axis=-1, keepdims=True)
    var = jnp.mean(jnp.square(y - mean), axis=-1, keepdims=True)
    yn = (y - mean) * lax.rsqrt(var + LN_X_EPS) * lnw_h + lnb_h
    bonus = jnp.sum(r_h * k_h * rk_h, axis=-1, keepdims=True) * v_h
    return (yn + bonus) * g_h


def _wkv_body(f_ref, fp_ref, mu_ref, w0_ref, ww2_ref, a0_ref, wa2_ref, wg2_ref, kk_ref, ka_ref,
              rk_ref, lnw_ref, lnb_ref, o_ref, st_ref, s_ref):
    c = pl.program_id(1)
    C = WKV_CHUNK

    @pl.when(c == 0)
    def _():
        s_ref[...] = jnp.zeros_like(s_ref)

    f = f_ref[0]
    prev_last = jnp.where(c == 0, 0.0, fp_ref[0][7:8, :])
    row = lax.broadcasted_iota(jnp.int32, f.shape, 0)
    prev = jnp.where(row == 0, prev_last, pltpu.roll(f, 1, 0))
    xs = f + mu_ref[...] * (prev - f)
    r, k, v, w_log, a, g, k_h = _rwkv_features(xs, w0_ref[...], ww2_ref[...], a0_ref[...],
                                               wa2_ref[...], wg2_ref[...], ka_ref[...])
    lw = -jnp.exp(w_log)
    kk = k * kk_ref[...]

    ti = lax.broadcasted_iota(jnp.int32, (C, C), 0)
    si = lax.broadcasted_iota(jnp.int32, (C, C), 1)
    tri_incl = (ti >= si).astype(BF16)
    l1 = lw.astype(BF16)
    r1 = lw - l1.astype(F32)
    l2 = r1.astype(BF16)
    l3 = (r1 - l2.astype(F32)).astype(BF16)
    cum = _dot(tri_incl, l1) + _dot(tri_incl, l2) + _dot(tri_incl, l3)
    rho = cum[C // 2 - 1:C // 2, :]
    ep = jnp.exp(cum - rho)
    em = jnp.exp(rho - cum)
    e_a = ep * jnp.exp(-lw)
    r_hat = r * ep
    k_hat = k_h * em
    e_r = jnp.exp(rho)
    e_c = jnp.exp(cum[C - 1:C, :] - rho)

    strict = ti > si
    incl = ti >= si
    eye = (ti == si).astype(F32)
    rk = rk_ref[...]
    lnw = lnw_ref[...]
    lnb = lnb_ref[...]
    outs = []
    for h in range(N_HEADS_B):
        sl = slice(h * HEAD_DIM_B, (h + 1) * HEAD_DIM_B)
        kkn = _head_norm(kk[:, sl])
        a_hat = -kkn * e_a[:, sl]
        b_hat = kkn * a[:, sl] * em[:, sl]
        kh, rh, vh = k_hat[:, sl], r_hat[:, sl], v[:, sl]
        ar = jnp.concatenate([a_hat, rh], axis=0).astype(BF16)
        bk = jnp.concatenate([b_hat, kh], axis=0).astype(BF16)
        p = _dot_nt(ar, bk)
        l_ab = jnp.where(strict, p[:C, :C], 0.0)
        l_ak = jnp.where(strict, p[:C, C:], 0.0)
        p_rb = jnp.where(incl, p[C:, :C], 0.0).astype(BF16)
        p_rk = jnp.where(incl, p[C:, C:], 0.0).astype(BF16)
        xp = l_ab
        tinv = eye + l_ab
        for _ in range(int(math.log2(C)) - 1):
            xb = xp.astype(BF16)
            xp = _dot(xb, xb)
            tinv = tinv + _dot(tinv.astype(BF16), xp.astype(BF16))
        tb = tinv.astype(BF16)
        vb = vh.astype(BF16)
        lv = _dot(l_ak.astype(BF16), vb)
        a_bar = _dot(tb, a_hat.astype(BF16))
        u_v = _dot(tb, lv.astype(BF16))
        a_bar_b = a_bar.astype(BF16)
        u_v_b = u_v.astype(BF16)
        r_bar = rh + _dot(p_rb, a_bar_b)
        y_v = _dot(p_rb, u_v_b) + _dot(p_rk, vb)
        ab = _dot_tn(a_bar_b, b_hat.astype(BF16))
        n_t = _dot_tn(jnp.concatenate([u_v_b, vb], axis=0), bk)
        s0 = s_ref[h]
        sr = s0 * e_r[:, sl]
        srb = sr.astype(BF16)
        y = _dot_nt((r_bar * e_r[:, sl]).astype(BF16), s0.astype(BF16)) + y_v
        s_new = (sr + _dot(srb, ab.astype(BF16)) + n_t) * e_c[:, sl]
        s_ref[h] = s_new
        outs.append(_wkv_finish_head(y, r[:, sl], k_h[:, sl], vh, g[:, sl], rk[:, sl], lnw[:, sl], lnb[:, sl]))
    o_ref[0] = jnp.concatenate(outs, axis=1)

    @pl.when(c == pl.num_programs(1) - 1)
    def _():
        st_ref[0] = s_ref[...]


def _wkv_call(feat, p):
    b, t, _ = feat.shape
    C = WKV_CHUNK
    nc = t // C
    row = lambda n: pl.BlockSpec((1, n), lambda bb, c: (0, 0))
    mat = lambda m, n: pl.BlockSpec((m, n), lambda bb, c: (0, 0))
    return pl.pallas_call(
        _wkv_body,
        out_shape=(jax.ShapeDtypeStruct((b, t, D_B), F32),
                   jax.ShapeDtypeStruct((b, N_HEADS_B, HEAD_DIM_B, HEAD_DIM_B), F32)),
        grid=(b, nc),
        in_specs=[pl.BlockSpec((1, C, D_SHIFT_B), lambda bb, c: (bb, c, 0)),
                  pl.BlockSpec((1, 8, D_SHIFT_B), lambda bb, c: (bb, jnp.maximum(c * (C // 8) - 1, 0), 0)),
                  row(D_SHIFT_B), row(D_B), mat(DECAY_LORA, D_B), row(D_B), mat(AAA_LORA, D_B),
                  mat(GATE_LORA, D_B), row(D_B), row(D_B), row(D_B), row(D_B), row(D_B)],
        out_specs=(pl.BlockSpec((1, C, D_B), lambda bb, c: (bb, c, 0)),
                   pl.BlockSpec((1, N_HEADS_B, HEAD_DIM_B, HEAD_DIM_B), lambda bb, c: (bb, 0, 0, 0))),
        scratch_shapes=[pltpu.VMEM((N_HEADS_B, HEAD_DIM_B, HEAD_DIM_B), F32)],
        compiler_params=_cparams(("arbitrary", "arbitrary")),
        name="wkv",
    )(feat, feat, p['mu_b'], p['w0_b'], p['w_w2_b'], p['a0_b'], p['w_a2_b'], p['w_g2_b'],
      p['k_k_b'], p['k_a_b'], p['r_k_b'], p['ln_x_w_b'], p['ln_x_b_b'])


def _swkv_prep_body(f_ref, sh_ref, mu_ref, w0_ref, ww2_ref, a0_ref, wa2_ref, wg2_ref, kk_ref, ka_ref,
                    r_ref, w_ref, k_ref, v_ref, aa_ref, bb_ref, g_ref):
    f = f_ref[...]
    xs = f + mu_ref[...] * (sh_ref[...] - f)
    r, k, v, w_log, a, g, k_h = _rwkv_features(xs, w0_ref[...], ww2_ref[...], a0_ref[...],
                                               wa2_ref[...], wg2_ref[...], ka_ref[...])
    kk = k * kk_ref[...]
    kkn = jnp.concatenate([_head_norm(kk[:, h * HEAD_DIM_B:(h + 1) * HEAD_DIM_B]) for h in range(N_HEADS_B)],
                          axis=1)
    r_ref[...] = r
    w_ref[...] = jnp.exp(-jnp.exp(w_log))
    k_ref[...] = k_h
    v_ref[...] = v
    aa_ref[...] = -kkn
    bb_ref[...] = kkn * a
    g_ref[...] = g


def _swkv_prep_call(feat_s, shift0, p):
    n = feat_s.shape[0]
    full = lambda a: pl.BlockSpec(a.shape, lambda: tuple(0 for _ in a.shape))
    args = (feat_s, shift0, p['mu_b'], p['w0_b'], p['w_w2_b'], p['a0_b'], p['w_a2_b'], p['w_g2_b'],
            p['k_k_b'], p['k_a_b'])
    return pl.pallas_call(
        _swkv_prep_body,
        out_shape=tuple(jax.ShapeDtypeStruct((n, D_B), F32) for _ in range(7)),
        in_specs=[full(a) for a in args],
        out_specs=tuple(pl.BlockSpec((n, D_B), lambda: (0, 0)) for _ in range(7)),
        compiler_params=pltpu.CompilerParams(vmem_limit_bytes=VMEM_LIMIT),
        name="swkv_prep",
    )(*args)


def _swkv_step_body(s_ref, a_ref, w_ref, b_ref, k_ref, r_ref, v_ref, so_ref, y_ref):
    s = s_ref[...]
    sa = jnp.sum(s * a_ref[...], axis=-1, keepdims=True)
    s2 = s * w_ref[...] + sa * b_ref[...] + v_ref[...] * k_ref[...]
    so_ref[...] = s2
    y_ref[...] = jnp.sum(s2 * r_ref[...], axis=-1, keepdims=True)


def _swkv_step_call(s0, aa, w, bb, k, r, v_col):
    nh = s0.shape[0]
    th = 64
    rowspec = pl.BlockSpec((th, 1, HEAD_DIM_B), lambda i: (i, 0, 0))
    colspec = pl.BlockSpec((th, HEAD_DIM_B, 1), lambda i: (i, 0, 0))
    stspec = pl.BlockSpec((th, HEAD_DIM_B, HEAD_DIM_B), lambda i: (i, 0, 0))
    return pl.pallas_call(
        _swkv_step_body,
        out_shape=(jax.ShapeDtypeStruct((nh, HEAD_DIM_B, HEAD_DIM_B), F32),
                   jax.ShapeDtypeStruct((nh, HEAD_DIM_B, 1), F32)),
        grid=(nh // th,),
        in_specs=[stspec, rowspec, rowspec, rowspec, rowspec, rowspec, colspec],
        out_specs=(stspec, colspec),
        compiler_params=_cparams(("arbitrary",)),
        name="swkv_step",
    )(s0, aa, w, bb, k, r, v_col)


def _swkv_fin_body(y_ref, r_ref, k_ref, v_ref, g_ref, rk_ref, lnw_ref, lnb_ref, o_ref):
    y, r, k, v, g = y_ref[...], r_ref[...], k_ref[...], v_ref[...], g_ref[...]
    rk, lnw, lnb = rk_ref[...], lnw_ref[...], lnb_ref[...]
    outs = []
    for h in range(N_HEADS_B):
        sl = slice(h * HEAD_DIM_B, (h + 1) * HEAD_DIM_B)
        outs.append(_wkv_finish_head(y[:, sl], r[:, sl], k[:, sl], v[:, sl], g[:, sl],
                                     rk[:, sl], lnw[:, sl], lnb[:, sl]))
    o_ref[...] = jnp.concatenate(outs, axis=1)


def _swkv_fin_call(y, r, k, v, g, p):
    n = y.shape[0]
    args = (y, r, k, v, g, p['r_k_b'], p['ln_x_w_b'], p['ln_x_b_b'])
    full = lambda a: pl.BlockSpec(a.shape, lambda: (0, 0))
    return pl.pallas_call(
        _swkv_fin_body,
        out_shape=jax.ShapeDtypeStruct((n, D_B), F32),
        in_specs=[full(a) for a in args],
        out_specs=pl.BlockSpec((n, D_B), lambda: (0, 0)),
        name="swkv_fin",
    )(*args)


def _route_t(scores, bias_col):
    n = scores.shape[1]
    gsz = N_EXPERTS // N_EXPERT_GROUPS
    choice = scores + bias_col
    ninf = -jnp.inf
    sid = lax.broadcasted_iota(jnp.int32, (gsz, n), 0)
    gs = []
    for gidx in range(N_EXPERT_GROUPS):
        blk = choice[gidx * gsz:(gidx + 1) * gsz, :]
        m1 = jnp.max(blk, axis=0, keepdims=True)
        first = jnp.min(jnp.where(blk == m1, sid, gsz), axis=0, keepdims=True)
        m2 = jnp.max(jnp.where(sid == first, ninf, blk), axis=0, keepdims=True)
        gs.append(m1 + m2)
    cur = jnp.concatenate(gs, axis=0)
    gid = lax.broadcasted_iota(jnp.int32, (N_EXPERT_GROUPS, n), 0)
    gmask = jnp.zeros((N_EXPERT_GROUPS, n), F32)
    for _ in range(TOPK_GROUPS):
        m = jnp.max(cur, axis=0, keepdims=True)
        first = jnp.min(jnp.where(cur == m, gid, N_EXPERT_GROUPS), axis=0, keepdims=True)
        sel = gid == first
        gmask = jnp.where(sel, 1.0, gmask)
        cur = jnp.where(sel, ninf, cur)
    emask = jnp.concatenate([jnp.broadcast_to(gmask[gidx:gidx + 1, :], (gsz, n))
                             for gidx in range(N_EXPERT_GROUPS)], axis=0)
    cur = jnp.where(emask > 0.5, choice, ninf)
    eid = lax.broadcasted_iota(jnp.int32, (N_EXPERTS, n), 0)
    selm = jnp.zeros((N_EXPERTS, n), F32)
    for _ in range(TOP_K):
        m = jnp.max(cur, axis=0, keepdims=True)
        first = jnp.min(jnp.where(cur == m, eid, N_EXPERTS), axis=0, keepdims=True)
        sel = eid == first
        selm = jnp.where(sel, 1.0, selm)
        cur = jnp.where(sel, ninf, cur)
    w = jnp.where(selm > 0.5, scores, 0.0)
    return w / jnp.sum(w, axis=0, keepdims=True) * ROUTED_SCALE


def _post_body(o1_ref, o2_ref, o3_ref, l1_ref, l2_ref, l3_ref, ob_ref, gt_ref, x_ref,
               g1_ref, sc2_ref, sh2_ref, npost_ref, npre_ref, wa_ref, wb_ref, wo_ref, wrt_ref, rb_ref,
               x1_ref, h2_ref, cf_ref, *, combine):
    if combine:
        l1, l2, l3 = l1_ref[0], l2_ref[0], l3_ref[0]
        mx = jnp.maximum(jnp.maximum(l1, l2), l3)
        e1, e2, e3 = jnp.exp(l1 - mx), jnp.exp(l2 - mx), jnp.exp(l3 - mx)
        o_a = (e1 * o1_ref[0].astype(F32) + e2 * o2_ref[0].astype(F32) + e3 * o3_ref[0].astype(F32)) \
            / (e1 + e2 + e3)
    else:
        o_a = o1_ref[0]
    gt = gt_ref[0].astype(F32)
    za = _dot(o_a.astype(BF16), wa_ref[...])
    zb = _dot(ob_ref[0].astype(BF16), wb_ref[...])
    merged = gt[:, :D_MODEL] * za + gt[:, D_MODEL:] * zb
    z = _dot(merged.astype(BF16), wo_ref[...])
    x1 = x_ref[0] + g1_ref[0] * _rms(z, npost_ref[...])
    x1_ref[0] = x1
    h2 = _rms(x1, npre_ref[...]) * (1.0 + sc2_ref[0]) + sh2_ref[0]
    h2_ref[0] = h2.astype(BF16)
    tm = h2.shape[0]
    tp = -(-tm // LANES) * LANES
    if tp != tm:
        h2 = jnp.concatenate([h2, jnp.zeros((tp - tm, D_MODEL), F32)], axis=0)
    logits_t = lax.dot_general(wrt_ref[...], h2, (((1,), (1,)), ((), ())),
                               precision=lax.Precision.HIGHEST, preferred_element_type=F32)
    w = _route_t(_sigmoid(logits_t[:N_EXPERTS, :]), rb_ref[...])
    w_t = jnp.concatenate([w, jnp.zeros((LANES - N_EXPERTS, tp), F32)], axis=0).T
    cf_ref[0] = w_t[:tm, :]


def _post_call(o_parts, lse_parts, ob, gates, x, gate1, scale2, shift2, p, wa, wb, wo, wrt, rb, tm, mod_per_row):
    nb, t, _ = x.shape
    nt = t // tm
    combine = lse_parts is not None
    rowblk = lambda width: pl.BlockSpec((1, tm, width), lambda b, i: (b, i, 0))
    if mod_per_row:
        mod_spec = rowblk(D_MODEL)
    else:
        mod_spec = pl.BlockSpec((1, 1, D_MODEL), lambda b, i: (b, 0, 0))
    const = lambda shp: pl.BlockSpec(shp, lambda b, i: (0, 0))
    if combine:
        o_args = list(o_parts) + list(lse_parts)
    else:
        o_args = [o_parts[0]] * 6
    return pl.pallas_call(
        functools.partial(_post_body, combine=combine),
        out_shape=(jax.ShapeDtypeStruct((nb, t, D_MODEL), F32),
                   jax.ShapeDtypeStruct((nb, t, D_MODEL), BF16),
                   jax.ShapeDtypeStruct((nb, t, LANES), F32)),
        grid=(nb, nt),
        in_specs=[rowblk(D_GROUP_A)] * 6 + [rowblk(D_B), rowblk(2 * D_MODEL), rowblk(D_MODEL),
                  mod_spec, mod_spec, mod_spec, const((1, D_MODEL)), const((1, D_MODEL)),
                  const((D_GROUP_A, D_MODEL)), const((D_B, D_MODEL)), const((D_MODEL, D_MODEL)),
                  const((LANES, D_MODEL)), const((N_EXPERTS, 1))],
        out_specs=(rowblk(D_MODEL), rowblk(D_MODEL), rowblk(LANES)),
        compiler_params=_cparams(("arbitrary", "arbitrary")),
        name="post",
    )(*o_args, ob, gates, x, gate1, scale2, shift2, p['norm_post_mix'].reshape(1, -1),
      p['norm_pre_ffn'].reshape(1, -1), wa, wb, wo, wrt, rb)


def _moe_body(h_ref, c_ref, wg_ref, wu_ref, wd_ref, sg_ref, su_ref, sd_ref, y_ref, acc_ref):
    e = pl.program_id(1)
    h = h_ref[...]

    @pl.when(e == 0)
    def _():
        act = _silu(_dot(h, sg_ref[...])) * _dot(h, su_ref[...])
        acc_ref[...] = _dot(act.astype(BF16), sd_ref[...])

    lane = lax.broadcasted_iota(jnp.int32, c_ref.shape, 1)
    ce = jnp.sum(jnp.where(lane == e, c_ref[...], 0.0), axis=1, keepdims=True)
    act = _silu(_dot(h, wg_ref[0])) * _dot(h, wu_ref[0]) * ce
    acc_ref[...] += _dot(act.astype(BF16), wd_ref[0])

    @pl.when(e == pl.num_programs(1) - 1)
    def _():
        y_ref[...] = acc_ref[...]


def _moe_call(h2, coef, weg, weu, wed, wsg, wsu, wsd, tm):
    n = h2.shape[0]
    const = lambda shp: pl.BlockSpec(shp, lambda i, e: (0, 0))
    return pl.pallas_call(
        _moe_body,
        out_shape=jax.ShapeDtypeStruct((n, D_MODEL), F32),
        grid=(n // tm, N_EXPERTS),
        in_specs=[pl.BlockSpec((tm, D_MODEL), lambda i, e: (i, 0)),
                  pl.BlockSpec((tm, LANES), lambda i, e: (i, 0)),
                  pl.BlockSpec((1, D_MODEL, D_EXPERT), lambda i, e: (e, 0, 0)),
                  pl.BlockSpec((1, D_MODEL, D_EXPERT), lambda i, e: (e, 0, 0)),
                  pl.BlockSpec((1, D_EXPERT, D_MODEL), lambda i, e: (e, 0, 0)),
                  const((D_MODEL, D_EXPERT)), const((D_MODEL, D_EXPERT)), const((D_EXPERT, D_MODEL))],
        out_specs=pl.BlockSpec((tm, D_MODEL), lambda i, e: (i, 0)),
        scratch_shapes=[pltpu.VMEM((tm, D_MODEL), F32)],
        compiler_params=_cparams(("arbitrary", "arbitrary")),
        name="moe",
    )(h2, coef, weg, weu, wed, wsg, wsu, wsd)


def _final_body(x1_ref, y_ref, g2_ref, n_ref, o_ref):
    o_ref[0] = x1_ref[0] + g2_ref[0] * _rms(y_ref[0], n_ref[...])


def _final_call(x1, y, gate2, gain, tm, mod_per_row):
    nb, t, _ = x1.shape
    rowblk = pl.BlockSpec((1, tm, D_MODEL), lambda b, i: (b, i, 0))
    mod_spec = rowblk if mod_per_row else pl.BlockSpec((1, 1, D_MODEL), lambda b, i: (b, 0, 0))
    return pl.pallas_call(
        _final_body,
        out_shape=jax.ShapeDtypeStruct((nb, t, D_MODEL), F32),
        grid=(nb, t // tm),
        in_specs=[rowblk, rowblk, mod_spec, pl.BlockSpec((1, D_MODEL), lambda b, i: (0, 0))],
        out_specs=rowblk,
        compiler_params=_cparams(("arbitrary", "arbitrary")),
        name="final",
    )(x1, y, gate2, gain.reshape(1, -1))


def _rope_tables(pos):
    half = HEAD_DIM_A // 2
    inv_freq = ROPE_THETA ** (-jnp.arange(half, dtype=F32) / half)
    ang = pos.astype(F32)[:, None] * inv_freq[None, :]
    cos = jnp.cos(ang)
    sin = jnp.sin(ang)
    reps = LANES // HEAD_DIM_A
    cos_t = jnp.tile(jnp.concatenate([cos, cos], axis=1), (1, reps))
    sin_t = jnp.tile(jnp.concatenate([-sin, sin], axis=1), (1, reps))
    return cos_t, sin_t


def _cache_from_tail(tail, keep):
    outs = []
    n, rows, _ = tail.shape
    for gi, kp in enumerate(keep):
        k = tail[:, rows - kp:, gi * D_GROUP_A:(gi + 1) * D_GROUP_A]
        v = tail[:, rows - kp:, D_A + gi * D_GROUP_A:D_A + (gi + 1) * D_GROUP_A]
        outs.append(jnp.stack([k, v], axis=2).reshape(n, kp, 2, N_HEADS_A, HEAD_DIM_A))
    return outs


def kernel(x_prompt, x_sample, c_prompt, c_sample, cache_a1_kv, cache_a2_kv, cache_a3_kv, state_b_wkv, state_b_shift, w_ada, b_ada, norm_pre_mix, norm_post_mix, norm_pre_ffn, norm_post_ffn, w_in, w_a_out, mu_b, w0_b, w_w2_b, a0_b, w_a2_b, w_g2_b, k_k_b, k_a_b, r_k_b, ln_x_w_b, ln_x_b_b, w_b_out, w_out, w_router, router_bias, w_e_gate, w_e_up, w_e_down, w_s_gate, w_s_up, w_s_down):
    assert DEPTH == 1
    l = 0
    nd = DEC_BATCH
    row = lambda a: a.reshape(1, -1)
    p = {'mu_b': row(mu_b[l]), 'w0_b': row(w0_b[l]), 'w_w2_b': w_w2_b[l], 'a0_b': row(a0_b[l]),
         'w_a2_b': w_a2_b[l], 'w_g2_b': w_g2_b[l], 'k_k_b': row(k_k_b[l]), 'k_a_b': row(k_a_b[l]),
         'r_k_b': row(r_k_b[l]), 'ln_x_w_b': row(ln_x_w_b[l]), 'ln_x_b_b': row(ln_x_b_b[l]),
         'norm_post_mix': norm_post_mix[l], 'norm_pre_ffn': norm_pre_ffn[l]}

    wq = w_in[l][:, :D_QKV].astype(BF16)
    wf = w_in[l][:, D_QKV:D_QKV + D_SHIFT_B].astype(BF16)
    wg = w_in[l][:, D_QKV + D_SHIFT_B:].astype(BF16)
    wa = w_a_out[l].astype(BF16)
    wb = w_b_out[l].astype(BF16)
    wo = w_out[l].astype(BF16)
    wrt = jnp.concatenate([w_router[l].T, jnp.zeros((LANES - N_EXPERTS, D_MODEL), F32)], axis=0)
    rb = router_bias[l].reshape(N_EXPERTS, 1)
    weg, weu, wed = w_e_gate[l].astype(BF16), w_e_up[l].astype(BF16), w_e_down[l].astype(BF16)
    wsg, wsu, wsd = w_s_gate[l].astype(BF16), w_s_up[l].astype(BF16), w_s_down[l].astype(BF16)

    n_c = BATCH + nd
    c_all = jnp.concatenate([c_prompt, c_sample, jnp.zeros((-n_c % 8, D_MODEL), F32)], axis=0)
    mod = _mod_call(c_all, w_ada[l], b_ada[l])
    mod_p = [m.reshape(BATCH, 1, D_MODEL) for m in jnp.split(mod[:BATCH], 6, axis=-1)]
    mod_s = [m.reshape(1, nd, D_MODEL) for m in jnp.split(mod[BATCH:n_c], 6, axis=-1)]

    cos_p, sin_p = _rope_tables(jnp.arange(SEQ, dtype=jnp.int32))
    cos_s, sin_s = _rope_tables(jnp.full((nd,), PAST_LEN, jnp.int32))

    keep_p = [min(w, SEQ) for w, _ in DILATED_GROUPS]
    tail_rows = max(keep_p)

    qkv_p, feat_p, gates_p, tail_p = _inproj_call(
        x_prompt, norm_pre_mix[l], mod_p[1], mod_p[0], cos_p, sin_p, wq, wf, wg,
        tm=256, tail_rows=tail_rows, mod_per_row=False)
    o_parts, lse_parts = [], []
    for gi, (_, dil) in enumerate(DILATED_GROUPS):
        o, lse = _attn_call(qkv_p, gi, dil)
        o_parts.append(o)
        lse_parts.append(lse)
    ob_p, wkv_p = _wkv_call(feat_p, p)
    x1_p, h2_p, cf_p = _post_call(o_parts, lse_parts, ob_p, gates_p, x_prompt, mod_p[2], mod_p[4], mod_p[3],
                                  p, wa, wb, wo, wrt, rb, tm=256, mod_per_row=False)

    xs3 = x_sample.reshape(1, nd, D_MODEL)
    qkv_s, feat_s, gates_s, tail_s = _inproj_call(
        xs3, norm_pre_mix[l], mod_s[1], mod_s[0], cos_s, sin_s, wq, wf, wg,
        tm=nd, tail_rows=nd, mod_per_row=True)
    oa_s = _sattn_call(qkv_s[0], cache_a1_kv[l], cache_a2_kv[l], cache_a3_kv[l])
    r_s, w_s, k_s, v_s, aa_s, bb_s, g_s = _swkv_prep_call(feat_s[0], state_b_shift[l], p)
    nh = nd * N_HEADS_B
    as_row = lambda a: a.reshape(nh, 1, HEAD_DIM_B)
    s_new, y_col = _swkv_step_call(state_b_wkv[l].reshape(nh, HEAD_DIM_B, HEAD_DIM_B), as_row(aa_s), as_row(w_s),
                                   as_row(bb_s), as_row(k_s), as_row(r_s), v_s.reshape(nh, HEAD_DIM_B, 1))
    ob_s = _swkv_fin_call(y_col.reshape(nd, D_B), r_s, k_s, v_s, g_s, p)
    x1_s, h2_s, cf_s = _post_call([oa_s.reshape(1, nd, D_GROUP_A)], None, ob_s.reshape(1, nd, D_B), gates_s, xs3,
                                  mod_s[2], mod_s[4], mod_s[3], p, wa, wb, wo, wrt, rb, tm=nd, mod_per_row=True)

    n_p = BATCH * SEQ
    tm_moe = 512
    n_all = -(-(n_p + nd) // tm_moe) * tm_moe
    pad = n_all - n_p - nd
    h2_all = jnp.concatenate([h2_p.reshape(n_p, D_MODEL), h2_s[0], jnp.zeros((pad, D_MODEL), BF16)], axis=0)
    cf_all = jnp.concatenate([cf_p.reshape(n_p, LANES), cf_s[0], jnp.zeros((pad, LANES), F32)], axis=0)
    y_all = _moe_call(h2_all, cf_all, weg, weu, wed, wsg, wsu, wsd, tm_moe)
    y_prompt = _final_call(x1_p, y_all[:n_p].reshape(BATCH, SEQ, D_MODEL), mod_p[5], norm_post_ffn[l],
                           tm=512, mod_per_row=False)
    y_sample = _final_call(x1_s, y_all[n_p:n_p + nd].reshape(1, nd, D_MODEL), mod_s[5], norm_post_ffn[l],
                           tm=nd, mod_per_row=True)

    a_p = [z[None] for z in _cache_from_tail(tail_p, keep_p)]
    a_s = [z.reshape(1, nd, DEC_SEQ, 2, N_HEADS_A, HEAD_DIM_A)
           for z in _cache_from_tail(tail_s.reshape(nd, 1, 2 * D_A), [DEC_SEQ] * N_GROUPS_A)]
    shift_p = feat_p[:, -1][None]
    shift_s = feat_s[0][None]
    return (y_prompt, y_sample.reshape(nd, DEC_SEQ, D_MODEL), a_p[0], a_p[1], a_p[2], wkv_p[None], shift_p,
            a_s[0], a_s[1], a_s[2], s_new.reshape(1, nd, N_HEADS_B, HEAD_DIM_B, HEAD_DIM_B), shift_s)
```

```python
import functools
import math

import jax
import jax.numpy as jnp
from jax import lax
from jax.experimental import pallas as pl
from jax.experimental.pallas import tpu as pltpu

F32 = jnp.float32
BF16 = jnp.bfloat16
I32 = jnp.int32

D_MODEL = 1024
BATCH = 2
SEQ = 8192
DEPTH = 1
DEC_BATCH = 32
DEC_SEQ = 1
PAST_LEN = 16384

HEAD_DIM_A = 64
N_HEADS_A = 8
DILATED_GROUPS = ((128, 1), (512, 4), (2048, 16))
N_GROUPS_A = 3
D_GROUP_A = N_HEADS_A * HEAD_DIM_A
D_A = N_GROUPS_A * D_GROUP_A
D_QKV = 3 * D_A
BAND_BLOCK = 128
ROPE_THETA = 10000.0

HEAD_DIM_B = 64
N_HEADS_B = 16
D_B = 1024
DECAY_LORA = 64
AAA_LORA = 64
GATE_LORA = 160
D_SHIFT_B = 3 * D_B + DECAY_LORA + AAA_LORA + GATE_LORA
LN_X_EPS = 64e-5

N_EXPERTS = 64
TOP_K = 8
N_EXPERT_GROUPS = 8
TOPK_GROUPS = 4
D_EXPERT = 256
ROUTED_SCALE = 2.5
EXPERT_BLOCK = 128
NORM_EPS = 1e-6

LANES = 128
WKV_CHUNK = 64
MOE_TILE = 256
VMEM_LIMIT = 56 * 1024 * 1024
HALF = D_MODEL // 2


def _cparams(sem):
    return pltpu.CompilerParams(dimension_semantics=sem, vmem_limit_bytes=VMEM_LIMIT)


def _dot(a, b):
    return jnp.dot(a, b, preferred_element_type=F32)


def _dot_nt(a, b):
    return lax.dot_general(a, b, (((1,), (1,)), ((), ())), preferred_element_type=F32)


def _dot_tn(a, b):
    return lax.dot_general(a, b, (((0,), (0,)), ((), ())), preferred_element_type=F32)


def _dot_exact(a, b):
    return lax.dot_general(a, b, (((1,), (0,)), ((), ())), precision=lax.Precision.HIGHEST,
                           preferred_element_type=F32)


def _rms(x, gain):
    return x * lax.rsqrt(jnp.mean(x * x, axis=-1, keepdims=True) + NORM_EPS) * gain


def _sigmoid(x):
    return 1.0 / (1.0 + jnp.exp(-x))


def _silu(x):
    return x * _sigmoid(x)


def _softplus(x):
    return jnp.maximum(x, 0.0) + jnp.log(1.0 + jnp.exp(-jnp.abs(x)))


def _pack_pairs(x):
    lo = lax.bitcast_convert_type(x[:, :HALF].astype(BF16).astype(F32), I32)
    hi = lax.bitcast_convert_type(x[:, HALF:].astype(BF16).astype(F32), I32)
    return lax.shift_right_logical(lo, 16) | (hi & jnp.int32(-65536))


def _unpack_pairs(w):
    lo = lax.bitcast_convert_type(w << 16, F32)
    hi = lax.bitcast_convert_type(w & jnp.int32(-65536), F32)
    return jnp.concatenate([lo, hi], axis=1).astype(BF16)


def _mod_body(c_ref, w_ref, b_ref, o_ref):
    s = _silu(c_ref[...]).astype(BF16)
    o_ref[...] = _dot(s, w_ref[...].astype(BF16)) + b_ref[...]


def _mod_call(c_all, w_ada, b_ada):
    rows = c_all.shape[0]
    tn = 1536
    return pl.pallas_call(
        _mod_body,
        out_shape=jax.ShapeDtypeStruct((rows, 6 * D_MODEL), F32),
        grid=(6 * D_MODEL // tn,),
        in_specs=[pl.BlockSpec((rows, D_MODEL), lambda j: (0, 0)),
                  pl.BlockSpec((D_MODEL, tn), lambda j: (0, j)),
                  pl.BlockSpec((1, tn), lambda j: (0, j))],
        out_specs=pl.BlockSpec((rows, tn), lambda j: (0, j)),
        compiler_params=_cparams(("arbitrary",)),
        name="mod",
    )(c_all, w_ada, b_ada.reshape(1, -1))


def _inproj_body(x_ref, g_ref, sc_ref, sh_ref, cos_ref, sin_ref, wq_ref, wf_ref, wg_ref,
                 q0_ref, q1_ref, q2_ref, feat_ref, gate_ref, tail_ref, p_ref, *, dils):
    x = x_ref[0]
    tm = x.shape[0]
    h = _rms(x, g_ref[...]) * (1.0 + sc_ref[0]) + sh_ref[0]
    hb = h.astype(BF16)
    p = _dot(hb, wq_ref[...])
    cos = cos_ref[...]
    sin = sin_ref[...]
    lane = lax.broadcasted_iota(I32, cos.shape, 1)
    first_half = (lane % HEAD_DIM_A) < (HEAD_DIM_A // 2)
    for c in range(2 * D_A // LANES):
        xc = p[:, c * LANES:(c + 1) * LANES]
        partner = jnp.where(first_half, pltpu.roll(xc, LANES - HEAD_DIM_A // 2, 1),
                            pltpu.roll(xc, HEAD_DIM_A // 2, 1))
        rc = xc * cos + partner * sin
        if c < D_A // LANES:
            rc = rc * (HEAD_DIM_A ** -0.5)
        p_ref[c] = rc
        if c >= D_A // LANES:
            tail_ref[0, :, (c - D_A // LANES) * LANES:(c - D_A // LANES + 1) * LANES] = rc
    for c in range(2 * D_A // LANES, D_QKV // LANES):
        p_ref[c] = p[:, c * LANES:(c + 1) * LANES]
    tail_ref[0, :, D_A:] = p[:, 2 * D_A:]
    per_group = D_GROUP_A // LANES
    for gi, (out_ref, dil) in enumerate(zip((q0_ref, q1_ref, q2_ref), dils)):
        for which in range(3):
            for j in range(per_group):
                c = (which * D_A + gi * D_GROUP_A) // LANES + j
                dst = slice(which * D_GROUP_A + j * LANES, which * D_GROUP_A + (j + 1) * LANES)
                if dil == 1:
                    out_ref[0, 0, :, dst] = p_ref[c].astype(BF16)
                else:
                    for r in range(dil):
                        out_ref[0, r, :, dst] = p_ref[c, pl.ds(r, tm // dil, stride=dil), :].astype(BF16)
    feat_ref[0] = _dot(hb, wf_ref[...])
    gate_ref[0] = _sigmoid(_dot(hb, wg_ref[...])).astype(BF16)


def _inproj_call(x, gain, scale, shift, cos_t, sin_t, wq, wf, wg, tm, tail_rows, mod_per_row, dils):
    nb, t, _ = x.shape
    nt = t // tm
    tail_first = (t - tail_rows) // tm
    if mod_per_row:
        mod_spec = pl.BlockSpec((1, tm, D_MODEL), lambda b, i: (b, i, 0))
    else:
        mod_spec = pl.BlockSpec((1, 1, D_MODEL), lambda b, i: (b, 0, 0))
    resident = lambda shp: pl.BlockSpec(shp, lambda b, i: (0, 0), pipeline_mode=pl.Buffered(1))
    q_shapes = tuple(jax.ShapeDtypeStruct((nb, d, t // d, 3 * D_GROUP_A), BF16) for d in dils)
    q_specs = tuple(pl.BlockSpec((1, d, tm // d, 3 * D_GROUP_A), lambda b, i: (b, 0, i, 0)) for d in dils)
    return pl.pallas_call(
        functools.partial(_inproj_body, dils=dils),
        out_shape=q_shapes + (jax.ShapeDtypeStruct((nb, t, D_SHIFT_B), F32),
                              jax.ShapeDtypeStruct((nb, t, 2 * D_MODEL), BF16),
                              jax.ShapeDtypeStruct((nb, tail_rows, 2 * D_A), F32)),
        grid=(nb, nt),
        in_specs=[pl.BlockSpec((1, tm, D_MODEL), lambda b, i: (b, i, 0)),
                  pl.BlockSpec((1, D_MODEL), lambda b, i: (0, 0)),
                  mod_spec, mod_spec,
                  pl.BlockSpec((tm, LANES), lambda b, i: (i, 0)),
                  pl.BlockSpec((tm, LANES), lambda b, i: (i, 0)),
                  resident((D_MODEL, D_QKV)), resident((D_MODEL, D_SHIFT_B)),
                  resident((D_MODEL, 2 * D_MODEL))],
        out_specs=q_specs + (pl.BlockSpec((1, tm, D_SHIFT_B), lambda b, i: (b, i, 0)),
                             pl.BlockSpec((1, tm, 2 * D_MODEL), lambda b, i: (b, i, 0)),
                             pl.BlockSpec((1, tm, 2 * D_A), lambda b, i: (b, jnp.maximum(i - tail_first, 0), 0))),
        scratch_shapes=[pltpu.VMEM((D_QKV // LANES, tm, LANES), F32)],
        compiler_params=_cparams(("arbitrary", "arbitrary")),
        name="inproj",
    )(x, gain.reshape(1, -1), scale, shift, cos_t, sin_t, wq, wf, wg)


def _attn_body(q_ref, kc_ref, kp_ref, vc_ref, vp_ref, o_ref, lse_ref):
    mb = pl.program_id(2)
    q = q_ref[0, 0]
    k = jnp.concatenate([kp_ref[0, 0], kc_ref[0, 0]], axis=0)
    v = jnp.concatenate([vp_ref[0, 0], vc_ref[0, 0]], axis=0)
    qi = lax.broadcasted_iota(I32, (BAND_BLOCK, 2 * BAND_BLOCK), 0)
    ki = lax.broadcasted_iota(I32, (BAND_BLOCK, 2 * BAND_BLOCK), 1)
    dist = qi + BAND_BLOCK - ki
    mask = (dist >= 0) & (dist <= BAND_BLOCK) & ((ki >= BAND_BLOCK) | (mb > 0))
    lane_q = lax.broadcasted_iota(I32, (BAND_BLOCK, LANES), 1)
    lane_k = lax.broadcasted_iota(I32, (2 * BAND_BLOCK, LANES), 1)
    for hp in range(N_HEADS_A // 2):
        sl = slice(hp * LANES, (hp + 1) * LANES)
        qp, kp, vp = q[:, sl], k[:, sl], v[:, sl]
        o_pair = jnp.zeros((BAND_BLOCK, LANES), F32)
        lse_pair = jnp.zeros((BAND_BLOCK, LANES), F32)
        for sub in range(2):
            mq = (lane_q >= HEAD_DIM_A) if sub else (lane_q < HEAD_DIM_A)
            mk = (lane_k >= HEAD_DIM_A) if sub else (lane_k < HEAD_DIM_A)
            s = _dot_nt(jnp.where(mq, qp, jnp.zeros_like(qp)), kp)
            s = jnp.where(mask, s, -jnp.inf)
            mx = jnp.max(s, axis=1, keepdims=True)
            p = jnp.exp(s - mx)
            l = jnp.sum(p, axis=1, keepdims=True)
            pv = _dot(p.astype(BF16), jnp.where(mk, vp, jnp.zeros_like(vp)))
            o_pair = o_pair + pv / l
            lse_pair = jnp.where(mq, mx + jnp.log(l), lse_pair)
        o_ref[0, 0, :, sl] = o_pair.astype(BF16)
        lse_ref[0, 0, :, sl] = lse_pair


def _attn_call(qkv_g, gi):
    b, dil, l, _ = qkv_g.shape
    nb = l // BAND_BLOCK
    blk = (1, 1, BAND_BLOCK, D_GROUP_A)
    cur = lambda which: pl.BlockSpec(blk, lambda bb, r, m: (bb, r, m, which))
    prev = lambda which: pl.BlockSpec(blk, lambda bb, r, m: (bb, r, jnp.maximum(m - 1, 0), which))
    return pl.pallas_call(
        _attn_body,
        out_shape=(jax.ShapeDtypeStruct((b, dil, l, D_GROUP_A), BF16),
                   jax.ShapeDtypeStruct((b, dil, l, D_GROUP_A), F32)),
        grid=(b, dil, nb),
        in_specs=[cur(0), cur(1), prev(1), cur(2), prev(2)],
        out_specs=(pl.BlockSpec(blk, lambda bb, r, m: (bb, r, m, 0)),
                   pl.BlockSpec(blk, lambda bb, r, m: (bb, r, m, 0))),
        compiler_params=_cparams(("arbitrary", "arbitrary", "arbitrary")),
        name=f"attn{gi}",
    )(qkv_g, qkv_g, qkv_g, qkv_g, qkv_g)


def _sattn_body(qkv_ref, b1_ref, b2_ref, b3_ref, o_ref):
    outs, lses = [], []
    for g, buf_ref in enumerate((b1_ref, b2_ref, b3_ref)):
        q = qkv_ref[0, g]
        kn = qkv_ref[0, N_GROUPS_A + g]
        vn = qkv_ref[0, 2 * N_GROUPS_A + g]
        kb = buf_ref[0, :, 0, 0]
        vb = buf_ref[0, :, 0, 1]
        s = jnp.sum(kb * q[None], axis=-1, keepdims=True)
        sn = jnp.sum(kn * q, axis=-1, keepdims=True)
        m = jnp.maximum(jnp.max(s, axis=0), sn)
        p = jnp.exp(s - m[None])
        pn = jnp.exp(sn - m)
        l = jnp.sum(p, axis=0) + pn
        outs.append((jnp.sum(p * vb, axis=0) + pn * vn) / l)
        lses.append(m + jnp.log(l))
    mx = jnp.maximum(jnp.maximum(lses[0], lses[1]), lses[2])
    es = [jnp.exp(z - mx) for z in lses]
    o_ref[0] = (es[0] * outs[0] + es[1] * outs[1] + es[2] * outs[2]) / (es[0] + es[1] + es[2])


def _sattn_call(qkv_s, c1, c2, c3):
    n = qkv_s.shape[0]
    views, specs = [], []
    for c, (_, dil) in zip((c1, c2, c3), DILATED_GROUPS):
        wb = c.shape[1]
        views.append(c.reshape(n, wb // dil, dil, 2, N_HEADS_A, HEAD_DIM_A))
        specs.append(pl.BlockSpec((1, wb // dil, 1, 2, N_HEADS_A, HEAD_DIM_A), lambda b: (b, 0, 0, 0, 0, 0)))
    return pl.pallas_call(
        _sattn_body,
        out_shape=jax.ShapeDtypeStruct((n, N_HEADS_A, HEAD_DIM_A), F32),
        grid=(n,),
        in_specs=[pl.BlockSpec((1, 3 * N_GROUPS_A, N_HEADS_A, HEAD_DIM_A), lambda b: (b, 0, 0, 0))] + specs,
        out_specs=pl.BlockSpec((1, N_HEADS_A, HEAD_DIM_A), lambda b: (b, 0, 0)),
        compiler_params=_cparams(("arbitrary",)),
        name="sattn",
    )(qkv_s, *views)


def _rwkv_features(xs, w0, ww2, a0, wa2, wg2, k_a):
    r = xs[:, :D_B]
    k = xs[:, D_B:2 * D_B]
    v = xs[:, 2 * D_B:3 * D_B]
    xw = xs[:, 3 * D_B:3 * D_B + DECAY_LORA]
    xa = xs[:, 3 * D_B + DECAY_LORA:3 * D_B + DECAY_LORA + AAA_LORA]
    xg = xs[:, 3 * D_B + DECAY_LORA + AAA_LORA:]
    w_log = -_softplus(-(w0 + _dot(jnp.tanh(xw).astype(BF16), ww2.astype(BF16)))) - 0.5
    a = _sigmoid(a0 + _dot(xa.astype(BF16), wa2.astype(BF16)))
    g = _dot(_sigmoid(xg).astype(BF16), wg2.astype(BF16))
    k_h = k * (1.0 + (a - 1.0) * k_a)
    return r, k, v, w_log, a, g, k_h


def _head_norm(kk_h):
    nrm = jnp.sqrt(jnp.sum(kk_h * kk_h, axis=-1, keepdims=True))
    return kk_h / jnp.maximum(nrm, 1e-12)


def _wkv_finish_head(y, r_h, k_h, v_h, g_h, rk_h, lnw_h, lnb_h):
    mean = jnp.mean(y, axis=-1, keepdims=True)
    var = jnp.mean(jnp.square(y - mean), axis=-1, keepdims=True)
    yn = (y - mean) * lax.rsqrt(var + LN_X_EPS) * lnw_h + lnb_h
    bonus = jnp.sum(r_h * k_h * rk_h, axis=-1, keepdims=True) * v_h
    return (yn + bonus) * g_h


def _wkv_body(f_ref, fp_ref, mu_ref, w0_ref, ww2_ref, a0_ref, wa2_ref, wg2_ref, kk_ref, ka_ref,
              rk_ref, lnw_ref, lnb_ref, o_ref, st_ref, s_ref):
    c = pl.program_id(1)
    C = WKV_CHUNK

    @pl.when(c == 0)
    def _():
        s_ref[...] = jnp.zeros_like(s_ref)

    f = f_ref[0]
    prev_last = jnp.where(c == 0, 0.0, fp_ref[0][7:8, :])
    row = lax.broadcasted_iota(I32, f.shape, 0)
    prev = jnp.where(row == 0, prev_last, pltpu.roll(f, 1, 0))
    xs = f + mu_ref[...] * (prev - f)
    r, k, v, w_log, a, g, k_h = _rwkv_features(xs, w0_ref[...], ww2_ref[...], a0_ref[...],
                                               wa2_ref[...], wg2_ref[...], ka_ref[...])
    lw = -jnp.exp(w_log)
    kk = k * kk_ref[...]

    ti = lax.broadcasted_iota(I32, (C, C), 0)
    si = lax.broadcasted_iota(I32, (C, C), 1)
    tri_incl = (ti >= si).astype(BF16)
    l1 = lw.astype(BF16)
    r1 = lw - l1.astype(F32)
    l2 = r1.astype(BF16)
    l3 = (r1 - l2.astype(F32)).astype(BF16)
    cum = _dot(tri_incl, l1) + _dot(tri_incl, l2) + _dot(tri_incl, l3)
    rho = cum[C // 2 - 1:C // 2, :]
    ep = jnp.exp(cum - rho)
    em = jnp.exp(rho - cum)
    e_a = ep * jnp.exp(-lw)
    r_hat = r * ep
    k_hat = k_h * em
    e_r = jnp.exp(rho)
    e_c = jnp.exp(cum[C - 1:C, :] - rho)

    strict = ti > si
    incl = ti >= si
    eye = (ti == si).astype(F32)
    rk = rk_ref[...]
    lnw = lnw_ref[...]
    lnb = lnb_ref[...]
    heads = range(N_HEADS_B)
    sls = [slice(h * HEAD_DIM_B, (h + 1) * HEAD_DIM_B) for h in heads]
    kkn = [_head_norm(kk[:, sl]) for sl in sls]
    a_hat = [-kkn[h] * e_a[:, sls[h]] for h in heads]
    b_hat = [kkn[h] * a[:, sls[h]] * em[:, sls[h]] for h in heads]
    a_hat_b = [z.astype(BF16) for z in a_hat]
    b_hat_b = [z.astype(BF16) for z in b_hat]
    rh = [r_hat[:, sl] for sl in sls]
    vb = [v[:, sl].astype(BF16) for sl in sls]
    bk = [jnp.concatenate([b_hat_b[h], k_hat[:, sls[h]].astype(BF16)], axis=0) for h in heads]
    p = [_dot_nt(jnp.concatenate([a_hat_b[h], rh[h].astype(BF16)], axis=0), bk[h]) for h in heads]
    l_ab = [jnp.where(strict, z[:C, :C], 0.0) for z in p]
    l_ak = [jnp.where(strict, z[:C, C:], 0.0).astype(BF16) for z in p]
    p_rb = [jnp.where(incl, z[C:, :C], 0.0).astype(BF16) for z in p]
    p_rk = [jnp.where(incl, z[C:, C:], 0.0).astype(BF16) for z in p]
    xb = [z.astype(BF16) for z in l_ab]
    tinv = [eye + z for z in l_ab]
    for _ in range(int(math.log2(C)) - 1):
        xb = [_dot(z, z).astype(BF16) for z in xb]
        tinv = [tinv[h] + _dot(tinv[h].astype(BF16), xb[h]) for h in heads]
    tb = [z.astype(BF16) for z in tinv]
    lv = [_dot(l_ak[h], vb[h]).astype(BF16) for h in heads]
    a_bar = [_dot(tb[h], a_hat_b[h]).astype(BF16) for h in heads]
    u_v = [_dot(tb[h], lv[h]).astype(BF16) for h in heads]
    r_bar = [rh[h] + _dot(p_rb[h], a_bar[h]) for h in heads]
    y_v = [_dot(p_rb[h], u_v[h]) + _dot(p_rk[h], vb[h]) for h in heads]
    ab = [_dot_tn(a_bar[h], b_hat_b[h]).astype(BF16) for h in heads]
    n_t = [_dot_tn(jnp.concatenate([u_v[h], vb[h]], axis=0), bk[h]) for h in heads]
    s0 = [s_ref[h] for h in heads]
    sr = [s0[h] * e_r[:, sls[h]] for h in heads]
    y = [_dot_nt((r_bar[h] * e_r[:, sls[h]]).astype(BF16), s0[h].astype(BF16)) + y_v[h] for h in heads]
    s_new = [(sr[h] + _dot(sr[h].astype(BF16), ab[h]) + n_t[h]) * e_c[:, sls[h]] for h in heads]
    for h in heads:
        s_ref[h] = s_new[h]
    outs = [_wkv_finish_head(y[h], r[:, sls[h]], k_h[:, sls[h]], v[:, sls[h]], g[:, sls[h]],
                             rk[:, sls[h]], lnw[:, sls[h]], lnb[:, sls[h]]) for h in heads]
    o_ref[0] = jnp.concatenate(outs, axis=1)

    @pl.when(c == pl.num_programs(1) - 1)
    def _():
        st_ref[0] = s_ref[...]


def _wkv_call(feat, p):
    b, t, _ = feat.shape
    C = WKV_CHUNK
    nc = t // C
    row = lambda n: pl.BlockSpec((1, n), lambda bb, c: (0, 0))
    mat = lambda m, n: pl.BlockSpec((m, n), lambda bb, c: (0, 0))
    return pl.pallas_call(
        _wkv_body,
        out_shape=(jax.ShapeDtypeStruct((b, t, D_B), F32),
                   jax.ShapeDtypeStruct((b, N_HEADS_B, HEAD_DIM_B, HEAD_DIM_B), F32)),
        grid=(b, nc),
        in_specs=[pl.BlockSpec((1, C, D_SHIFT_B), lambda bb, c: (bb, c, 0)),
                  pl.BlockSpec((1, 8, D_SHIFT_B), lambda bb, c: (bb, jnp.maximum(c * (C // 8) - 1, 0), 0)),
                  row(D_SHIFT_B), row(D_B), mat(DECAY_LORA, D_B), row(D_B), mat(AAA_LORA, D_B),
                  mat(GATE_LORA, D_B), row(D_B), row(D_B), row(D_B), row(D_B), row(D_B)],
        out_specs=(pl.BlockSpec((1, C, D_B), lambda bb, c: (bb, c, 0)),
                   pl.BlockSpec((1, N_HEADS_B, HEAD_DIM_B, HEAD_DIM_B), lambda bb, c: (bb, 0, 0, 0))),
        scratch_shapes=[pltpu.VMEM((N_HEADS_B, HEAD_DIM_B, HEAD_DIM_B), F32)],
        compiler_params=_cparams(("arbitrary", "arbitrary")),
        name="wkv",
    )(feat, feat, p['mu_b'], p['w0_b'], p['w_w2_b'], p['a0_b'], p['w_a2_b'], p['w_g2_b'],
      p['k_k_b'], p['k_a_b'], p['r_k_b'], p['ln_x_w_b'], p['ln_x_b_b'])


def _swkv_prep_body(f_ref, sh_ref, mu_ref, w0_ref, ww2_ref, a0_ref, wa2_ref, wg2_ref, kk_ref, ka_ref,
                    r_ref, w_ref, k_ref, v_ref, aa_ref, bb_ref, g_ref):
    f = f_ref[...]
    xs = f + mu_ref[...] * (sh_ref[...] - f)
    r, k, v, w_log, a, g, k_h = _rwkv_features(xs, w0_ref[...], ww2_ref[...], a0_ref[...],
                                               wa2_ref[...], wg2_ref[...], ka_ref[...])
    kk = k * kk_ref[...]
    kkn = jnp.concatenate([_head_norm(kk[:, h * HEAD_DIM_B:(h + 1) * HEAD_DIM_B]) for h in range(N_HEADS_B)],
                          axis=1)
    r_ref[...] = r
    w_ref[...] = jnp.exp(-jnp.exp(w_log))
    k_ref[...] = k_h
    v_ref[...] = v
    aa_ref[...] = -kkn
    bb_ref[...] = kkn * a
    g_ref[...] = g


def _swkv_prep_call(feat_s, shift0, p):
    n = feat_s.shape[0]
    full = lambda a: pl.BlockSpec(a.shape, lambda: tuple(0 for _ in a.shape))
    args = (feat_s, shift0, p['mu_b'], p['w0_b'], p['w_w2_b'], p['a0_b'], p['w_a2_b'], p['w_g2_b'],
            p['k_k_b'], p['k_a_b'])
    return pl.pallas_call(
        _swkv_prep_body,
        out_shape=tuple(jax.ShapeDtypeStruct((n, D_B), F32) for _ in range(7)),
        in_specs=[full(a) for a in args],
        out_specs=tuple(pl.BlockSpec((n, D_B), lambda: (0, 0)) for _ in range(7)),
        compiler_params=pltpu.CompilerParams(vmem_limit_bytes=VMEM_LIMIT),
        name="swkv_prep",
    )(*args)


def _swkv_step_body(s_ref, a_ref, w_ref, b_ref, k_ref, r_ref, v_ref, so_ref, y_ref):
    s = s_ref[...]
    sa = jnp.sum(s * a_ref[...], axis=-1, keepdims=True)
    s2 = s * w_ref[...] + sa * b_ref[...] + v_ref[...] * k_ref[...]
    so_ref[...] = s2
    y_ref[...] = jnp.sum(s2 * r_ref[...], axis=-1, keepdims=True)


def _swkv_step_call(s0, aa, w, bb, k, r, v_col):
    nh = s0.shape[0]
    th = 64
    rowspec = pl.BlockSpec((th, 1, HEAD_DIM_B), lambda i: (i, 0, 0))
    colspec = pl.BlockSpec((th, HEAD_DIM_B, 1), lambda i: (i, 0, 0))
    stspec = pl.BlockSpec((th, HEAD_DIM_B, HEAD_DIM_B), lambda i: (i, 0, 0))
    return pl.pallas_call(
        _swkv_step_body,
        out_shape=(jax.ShapeDtypeStruct((nh, HEAD_DIM_B, HEAD_DIM_B), F32),
                   jax.ShapeDtypeStruct((nh, HEAD_DIM_B, 1), F32)),
        grid=(nh // th,),
        in_specs=[stspec, rowspec, rowspec, rowspec, rowspec, rowspec, colspec],
        out_specs=(stspec, colspec),
        compiler_params=_cparams(("arbitrary",)),
        name="swkv_step",
    )(s0, aa, w, bb, k, r, v_col)


def _swkv_fin_body(y_ref, r_ref, k_ref, v_ref, g_ref, rk_ref, lnw_ref, lnb_ref, o_ref):
    y, r, k, v, g = y_ref[...], r_ref[...], k_ref[...], v_ref[...], g_ref[...]
    rk, lnw, lnb = rk_ref[...], lnw_ref[...], lnb_ref[...]
    outs = []
    for h in range(N_HEADS_B):
        sl = slice(h * HEAD_DIM_B, (h + 1) * HEAD_DIM_B)
        outs.append(_wkv_finish_head(y[:, sl], r[:, sl], k[:, sl], v[:, sl], g[:, sl],
                                     rk[:, sl], lnw[:, sl], lnb[:, sl]))
    o_ref[...] = jnp.concatenate(outs, axis=1)


def _swkv_fin_call(y, r, k, v, g, p):
    n = y.shape[0]
    args = (y, r, k, v, g, p['r_k_b'], p['ln_x_w_b'], p['ln_x_b_b'])
    full = lambda a: pl.BlockSpec(a.shape, lambda: (0, 0))
    return pl.pallas_call(
        _swkv_fin_body,
        out_shape=jax.ShapeDtypeStruct((n, D_B), F32),
        in_specs=[full(a) for a in args],
        out_specs=pl.BlockSpec((n, D_B), lambda: (0, 0)),
        name="swkv_fin",
    )(*args)


def _route_t(scores, bias_col):
    n = scores.shape[1]
    gsz = N_EXPERTS // N_EXPERT_GROUPS
    choice = scores + bias_col
    ninf = -jnp.inf
    sid = lax.broadcasted_iota(I32, (gsz, n), 0)
    gs = []
    for gidx in range(N_EXPERT_GROUPS):
        blk = choice[gidx * gsz:(gidx + 1) * gsz, :]
        m1 = jnp.max(blk, axis=0, keepdims=True)
        first = jnp.min(jnp.where(blk == m1, sid, gsz), axis=0, keepdims=True)
        m2 = jnp.max(jnp.where(sid == first, ninf, blk), axis=0, keepdims=True)
        gs.append(m1 + m2)
    cur = jnp.concatenate(gs, axis=0)
    gid = lax.broadcasted_iota(I32, (N_EXPERT_GROUPS, n), 0)
    gmask = jnp.zeros((N_EXPERT_GROUPS, n), F32)
    for _ in range(TOPK_GROUPS):
        m = jnp.max(cur, axis=0, keepdims=True)
        first = jnp.min(jnp.where(cur == m, gid, N_EXPERT_GROUPS), axis=0, keepdims=True)
        sel = gid == first
        gmask = jnp.where(sel, 1.0, gmask)
        cur = jnp.where(sel, ninf, cur)
    emask = jnp.concatenate([jnp.broadcast_to(gmask[gidx:gidx + 1, :], (gsz, n))
                             for gidx in range(N_EXPERT_GROUPS)], axis=0)
    cur = jnp.where(emask > 0.5, choice, ninf)
    eid = lax.broadcasted_iota(I32, (N_EXPERTS, n), 0)
    selm = jnp.zeros((N_EXPERTS, n), F32)
    for _ in range(TOP_K):
        m = jnp.max(cur, axis=0, keepdims=True)
        first = jnp.min(jnp.where(cur == m, eid, N_EXPERTS), axis=0, keepdims=True)
        sel = eid == first
        selm = jnp.where(sel, 1.0, selm)
        cur = jnp.where(sel, ninf, cur)
    w = jnp.where(selm > 0.5, scores, 0.0)
    w = w / jnp.sum(w, axis=0, keepdims=True) * ROUTED_SCALE
    return jnp.where(selm > 0.5, w, -1.0)


def _unpermute(blk_ref, scr_ref, dil, tm):
    if dil == 1:
        return blk_ref[0, 0].astype(F32)
    n_chunks = scr_ref.shape[0]
    for r in range(dil):
        rows = blk_ref[0, r].astype(F32)
        for j in range(n_chunks):
            scr_ref[j, pl.ds(r, tm // dil, stride=dil), :] = rows[:, j * LANES:(j + 1) * LANES]
    return jnp.concatenate([scr_ref[j] for j in range(n_chunks)], axis=1)


def _post_body(*refs, combine, dils):
    if combine:
        o_refs, l_refs, rest = refs[:3], refs[3:6], refs[6:]
    else:
        o_refs, rest = refs[:1], refs[1:]
    (ob_ref, gt_ref, x_ref, g1_ref, sc2_ref, sh2_ref, npost_ref, npre_ref, wa_ref, wb_ref, wo_ref,
     wrt_ref, rb_ref, x1_ref, hp_ref, wt_ref) = rest[:16]
    scr = rest[16:]
    tm = x_ref.shape[1]
    if combine:
        os_, ls_ = [], []
        si = 0
        for gi, dil in enumerate(dils):
            os_.append(_unpermute(o_refs[gi], scr[si] if dil > 1 else None, dil, tm))
            ls_.append(_unpermute(l_refs[gi], scr[si + 1] if dil > 1 else None, dil, tm))
            si += 2 if dil > 1 else 0
        mx = jnp.maximum(jnp.maximum(ls_[0], ls_[1]), ls_[2])
        es = [jnp.exp(z - mx) for z in ls_]
        o_a = (es[0] * os_[0] + es[1] * os_[1] + es[2] * os_[2]) / (es[0] + es[1] + es[2])
    else:
        o_a = o_refs[0][0]
    gt = gt_ref[0].astype(F32)
    za = _dot(o_a.astype(BF16), wa_ref[...])
    zb = _dot(ob_ref[0].astype(BF16), wb_ref[...])
    merged = gt[:, :D_MODEL] * za + gt[:, D_MODEL:] * zb
    z = _dot(merged.astype(BF16), wo_ref[...])
    x1 = x_ref[0] + g1_ref[0] * _rms(z, npost_ref[...])
    x1_ref[0] = x1
    h2 = _rms(x1, npre_ref[...]) * (1.0 + sc2_ref[0]) + sh2_ref[0]
    hp_ref[0] = _pack_pairs(h2)
    tp = -(-tm // LANES) * LANES
    if tp != tm:
        h2 = jnp.concatenate([h2, jnp.zeros((tp - tm, D_MODEL), F32)], axis=0)
    logits_t = lax.dot_general(wrt_ref[...], h2, (((1,), (1,)), ((), ())),
                               precision=lax.Precision.HIGHEST, preferred_element_type=F32)
    w = _route_t(_sigmoid(logits_t[:N_EXPERTS, :]), rb_ref[...])
    wt_ref[...] = w[:, :tm]


def _post_call(o_parts, lse_parts, ob, gates, x, gate1, scale2, shift2, p, wa, wb, wo, wrt, rb, tm, mod_per_row):
    nb, t, _ = x.shape
    nt = t // tm
    combine = lse_parts is not None
    rowblk = lambda width: pl.BlockSpec((1, tm, width), lambda b, i: (b, i, 0))
    if mod_per_row:
        mod_spec = rowblk(D_MODEL)
    else:
        mod_spec = pl.BlockSpec((1, 1, D_MODEL), lambda b, i: (b, 0, 0))
    const = lambda shp: pl.BlockSpec(shp, lambda b, i: (0, 0))
    scratch = []
    if combine:
        dils = tuple(o.shape[1] for o in o_parts)
        o_args = list(o_parts) + list(lse_parts)
        o_specs = [pl.BlockSpec((1, d, tm // d, D_GROUP_A), lambda b, i: (b, 0, i, 0)) for d in dils] * 2
        for d in dils:
            if d > 1:
                scratch += [pltpu.VMEM((D_GROUP_A // LANES, tm, LANES), F32)] * 2
    else:
        dils = ()
        o_args = [o_parts[0]]
        o_specs = [rowblk(D_GROUP_A)]
    return pl.pallas_call(
        functools.partial(_post_body, combine=combine, dils=dils),
        out_shape=(jax.ShapeDtypeStruct((nb, t, D_MODEL), F32),
                   jax.ShapeDtypeStruct((nb, t, HALF), I32),
                   jax.ShapeDtypeStruct((N_EXPERTS, nb * t), F32)),
        grid=(nb, nt),
        in_specs=o_specs + [rowblk(D_B), rowblk(2 * D_MODEL), rowblk(D_MODEL),
                            mod_spec, mod_spec, mod_spec, const((1, D_MODEL)), const((1, D_MODEL)),
                            const((D_GROUP_A, D_MODEL)), const((D_B, D_MODEL)), const((D_MODEL, D_MODEL)),
                            const((LANES, D_MODEL)), const((N_EXPERTS, 1))],
        out_specs=(rowblk(D_MODEL), rowblk(HALF),
                   pl.BlockSpec((N_EXPERTS, tm), lambda b, i: (0, b * nt + i))),
        scratch_shapes=scratch,
        compiler_params=_cparams(("arbitrary", "arbitrary")),
        name="post",
    )(*o_args, ob, gates, x, gate1, scale2, shift2, p['norm_post_mix'].reshape(1, -1),
      p['norm_pre_ffn'].reshape(1, -1), wa, wb, wo, wrt, rb)


def _rank_body(w_ref, dest_ref, w8_ref, tab_ref, etab_ref, cnt_ref, pst_ref, run_ref, *, n_real, n_slots):
    ph = pl.program_id(0)
    i = pl.program_id(1)
    T = MOE_TILE
    w = w_ref[...]
    sel = (w >= 0.0).astype(F32)
    cnt_tile = jnp.broadcast_to(jnp.sum(sel, axis=1, keepdims=True), (N_EXPERTS, LANES))
    ei = lax.broadcasted_iota(I32, (N_EXPERTS, N_EXPERTS), 0)
    ej = lax.broadcasted_iota(I32, (N_EXPERTS, N_EXPERTS), 1)

    @pl.when((ph == 0) & (i == 0))
    def _():
        cnt_ref[...] = jnp.zeros_like(cnt_ref)

    @pl.when(ph == 0)
    def _():
        cnt_ref[...] += cnt_tile

    @pl.when((ph == 1) & (i == 0))
    def _():
        cnt = cnt_ref[...]
        padded = jnp.floor((cnt + (EXPERT_BLOCK - 1)) / EXPERT_BLOCK) * EXPERT_BLOCK
        pstart = _dot_exact((ej < ei).astype(F32), padded)
        pst_ref[...] = pstart
        run_ref[...] = jnp.zeros_like(run_ref)
        pend = pstart + padded
        vend = pstart + cnt
        esub = lax.broadcasted_iota(I32, (N_EXPERTS, LANES), 0)
        lane = lax.broadcasted_iota(I32, (1, LANES), 1)
        tab_ref[...] = jnp.zeros_like(tab_ref)
        for c in range(tab_ref.shape[1] // LANES):
            bs = ((c * LANES + lane) * EXPERT_BLOCK).astype(F32)
            be = jnp.minimum(jnp.sum((pend <= bs).astype(F32), axis=0, keepdims=True), N_EXPERTS - 1.0)
            tab_ref[0:1, c * LANES:(c + 1) * LANES] = be.astype(I32)
            tab_ref[1:2, c * LANES:(c + 1) * LANES] = (pend[N_EXPERTS - 1:, :] / EXPERT_BLOCK).astype(I32)
        on_diag = esub == lax.broadcasted_iota(I32, (N_EXPERTS, LANES), 1)
        etab_ref[...] = jnp.zeros_like(etab_ref)
        lo = jnp.sum(jnp.where(on_diag, vend, 0.0), axis=0, keepdims=True)
        hi = jnp.sum(jnp.where(on_diag, pend, 0.0), axis=0, keepdims=True)
        etab_ref[0:1, :] = jnp.where(lane == N_EXPERTS, pend[N_EXPERTS - 1:, :], lo).astype(I32)
        etab_ref[1:2, :] = jnp.where(lane == N_EXPERTS, float(n_slots), hi).astype(I32)

    @pl.when(ph == 1)
    def _():
        ti = lax.broadcasted_iota(I32, (T, T), 0)
        tj = lax.broadcasted_iota(I32, (T, T), 1)
        selb = sel.astype(BF16)
        rank = _dot(selb, (ti < tj).astype(BF16))
        ordn = _dot((ej < ei).astype(BF16), selb)
        dest_e = pst_ref[:, :1] + run_ref[:, :1] + rank
        run_ref[...] += cnt_tile
        tok = i * T + lax.broadcasted_iota(I32, (1, T), 1)
        dks, wks = [], []
        for k in range(TOP_K):
            m = (sel > 0.5) & (ordn == float(k))
            dk = jnp.sum(jnp.where(m, dest_e, 0.0), axis=0, keepdims=True)
            wk = jnp.sum(jnp.where(m, w, 0.0), axis=0, keepdims=True)
            dks.append(jnp.where(tok < n_real, dk, 0.0))
            wks.append(jnp.where(tok < n_real, wk, 0.0))
        dest_ref[...] = jnp.concatenate(dks, axis=0).astype(I32)
        w8_ref[...] = jnp.concatenate(wks, axis=0)


def _rank_call(w_t, n_real, n_blocks, n_blocks_pad):
    n = w_t.shape[1]
    nt = n // MOE_TILE
    return pl.pallas_call(
        functools.partial(_rank_body, n_real=n_real, n_slots=n_blocks * EXPERT_BLOCK),
        out_shape=(jax.ShapeDtypeStruct((TOP_K, n), I32),
                   jax.ShapeDtypeStruct((TOP_K, n), F32),
                   jax.ShapeDtypeStruct((8, n_blocks_pad), I32),
                   jax.ShapeDtypeStruct((8, LANES), I32)),
        grid=(2, nt),
        in_specs=[pl.BlockSpec((N_EXPERTS, MOE_TILE), lambda ph, i: (0, i))],
        out_specs=(pl.BlockSpec((TOP_K, MOE_TILE), lambda ph, i: (0, i * ph)),
                   pl.BlockSpec((TOP_K, MOE_TILE), lambda ph, i: (0, i * ph)),
                   pl.BlockSpec((8, n_blocks_pad), lambda ph, i: (0, 0)),
                   pl.BlockSpec((8, LANES), lambda ph, i: (0, 0))),
        scratch_shapes=[pltpu.VMEM((N_EXPERTS, LANES), F32)] * 3,
        compiler_params=_cparams(("arbitrary", "arbitrary")),
        name="rank",
    )(w_t)


def _dispatch_body(dest_ref, etab_ref, x_hbm, z_hbm, xs_hbm, sem, zsem, *, n_real):
    i = pl.program_id(0)
    T = MOE_TILE
    n_tok = jnp.clip(n_real - i * T, 0, T)

    def row_copy(src_row, dst_row):
        return pltpu.make_async_copy(x_hbm.at[pl.ds(src_row, 1)], xs_hbm.at[pl.ds(dst_row, 1)], sem)

    def zero_copy(dst_row):
        return pltpu.make_async_copy(z_hbm, xs_hbm.at[pl.ds(dst_row, 1)], zsem)

    def issue(t, carry):
        for k in range(TOP_K):
            row_copy(i * T + t, dest_ref[k, t]).start()
        return carry

    lax.fori_loop(0, n_tok, issue, 0)

    @pl.when(i == 0)
    def _():
        def per_expert(e, total):
            lo = etab_ref[0, e]
            hi = etab_ref[1, e]

            def fill(s, carry):
                zero_copy(s).start()
                return carry

            lax.fori_loop(lo, hi, fill, 0)
            return total + (hi - lo)

        total = lax.fori_loop(0, N_EXPERTS + 1, per_expert, 0)

        def drain(j, carry):
            zero_copy(0).wait()
            return carry

        lax.fori_loop(0, total, drain, 0)

    @pl.when(n_tok == T)
    def _():
        pltpu.make_async_copy(x_hbm.at[pl.ds(0, T * TOP_K)], xs_hbm.at[pl.ds(0, T * TOP_K)], sem).wait()

    @pl.when(n_tok < T)
    def _():
        def drain(j, carry):
            row_copy(0, 0).wait()
            return carry

        lax.fori_loop(0, n_tok * TOP_K, drain, 0)


def _dispatch_call(dest8, etab, hp_all, n_real, n_slots):
    n = hp_all.shape[0]
    zrow = jnp.zeros((1, HALF), I32)
    return pl.pallas_call(
        functools.partial(_dispatch_body, n_real=n_real),
        out_shape=jax.ShapeDtypeStruct((n_slots, HALF), I32),
        grid=(n // MOE_TILE,),
        in_specs=[pl.BlockSpec((TOP_K, MOE_TILE), lambda i: (0, i), memory_space=pltpu.SMEM),
                  pl.BlockSpec((8, LANES), lambda i: (0, 0), memory_space=pltpu.SMEM),
                  pl.BlockSpec(memory_space=pl.ANY),
                  pl.BlockSpec(memory_space=pl.ANY)],
        out_specs=pl.BlockSpec(memory_space=pl.ANY),
        scratch_shapes=[pltpu.SemaphoreType.DMA, pltpu.SemaphoreType.DMA],
        compiler_params=_cparams(("arbitrary",)),
        name="dispatch",
    )(dest8, etab, hp_all, zrow)


def _ffn_body(be_ref, xs_ref, wg_ref, wu_ref, wd_ref, ys_ref, wgb, wub, wdb):
    j = pl.program_id(0)

    @pl.when((j == 0) | (be_ref[j] != be_ref[jnp.maximum(j - 1, 0)]))
    def _():
        wgb[...] = wg_ref[0].astype(BF16)
        wub[...] = wu_ref[0].astype(BF16)
        wdb[...] = wd_ref[0].astype(BF16)

    x = _unpack_pairs(xs_ref[...])
    act = _silu(_dot(x, wgb[...])) * _dot(x, wub[...])
    ys_ref[...] = _dot(act.astype(BF16), wdb[...])


def _ffn_call(blk_e, xs, w_gate, w_up, w_down, n_blocks):
    grid_spec = pltpu.PrefetchScalarGridSpec(
        num_scalar_prefetch=1,
        grid=(n_blocks,),
        in_specs=[pl.BlockSpec((EXPERT_BLOCK, HALF), lambda j, be: (j, 0)),
                  pl.BlockSpec((1, D_MODEL, D_EXPERT), lambda j, be: (be[j], 0, 0)),
                  pl.BlockSpec((1, D_MODEL, D_EXPERT), lambda j, be: (be[j], 0, 0)),
                  pl.BlockSpec((1, D_EXPERT, D_MODEL), lambda j, be: (be[j], 0, 0))],
        out_specs=pl.BlockSpec((EXPERT_BLOCK, D_MODEL), lambda j, be: (j, 0)),
        scratch_shapes=[pltpu.VMEM((D_MODEL, D_EXPERT), BF16), pltpu.VMEM((D_MODEL, D_EXPERT), BF16),
                        pltpu.VMEM((D_EXPERT, D_MODEL), BF16)])
    return pl.pallas_call(
        _ffn_body,
        out_shape=jax.ShapeDtypeStruct((n_blocks * EXPERT_BLOCK, D_MODEL), F32),
        grid_spec=grid_spec,
        compiler_params=_cparams(("arbitrary",)),
        name="ffn",
    )(blk_e, xs, w_gate, w_up, w_down)


def _combine_body(dest_ref, w8_ref, hp_ref, sg_ref, su_ref, sd_ref, ys_hbm, y_ref, buf, sem):
    T = MOE_TILE

    def issue(t, carry):
        for k in range(TOP_K):
            pltpu.make_async_copy(ys_hbm.at[pl.ds(dest_ref[k, t], 1)], buf.at[pl.ds(k * T + t, 1)], sem).start()
        return carry

    lax.fori_loop(0, T, issue, 0)
    x = _unpack_pairs(hp_ref[...])
    y = _dot((_silu(_dot(x, sg_ref[...])) * _dot(x, su_ref[...])).astype(BF16), sd_ref[...])
    w_t = jnp.concatenate([w8_ref[...], jnp.zeros((LANES - TOP_K, T), F32)], axis=0).T
    pltpu.make_async_copy(ys_hbm.at[pl.ds(0, T * TOP_K)], buf, sem).wait()
    for k in range(TOP_K):
        y = y + w_t[:, k:k + 1] * buf[k * T:(k + 1) * T, :]
    y_ref[...] = y


def _combine_call(dest8, w8, hp_all, wsg, wsu, wsd, ys):
    n = hp_all.shape[0]
    T = MOE_TILE
    const = lambda shp: pl.BlockSpec(shp, lambda i: (0, 0))
    return pl.pallas_call(
        _combine_body,
        out_shape=jax.ShapeDtypeStruct((n, D_MODEL), F32),
        grid=(n // T,),
        in_specs=[pl.BlockSpec((TOP_K, T), lambda i: (0, i), memory_space=pltpu.SMEM),
                  pl.BlockSpec((TOP_K, T), lambda i: (0, i)),
                  pl.BlockSpec((T, HALF), lambda i: (i, 0)),
                  const((D_MODEL, D_EXPERT)), const((D_MODEL, D_EXPERT)), const((D_EXPERT, D_MODEL)),
                  pl.BlockSpec(memory_space=pl.ANY)],
        out_specs=pl.BlockSpec((T, D_MODEL), lambda i: (i, 0)),
        scratch_shapes=[pltpu.VMEM((T * TOP_K, D_MODEL), F32), pltpu.SemaphoreType.DMA],
        compiler_params=_cparams(("arbitrary",)),
        name="combine",
    )(dest8, w8, hp_all, wsg, wsu, wsd, ys)


def _final_body(x1_ref, y_ref, g2_ref, n_ref, o_ref):
    o_ref[0] = x1_ref[0] + g2_ref[0] * _rms(y_ref[0], n_ref[...])


def _final_call(x1, y, gate2, gain, tm, mod_per_row):
    nb, t, _ = x1.shape
    rowblk = pl.BlockSpec((1, tm, D_MODEL), lambda b, i: (b, i, 0))
    mod_spec = rowblk if mod_per_row else pl.BlockSpec((1, 1, D_MODEL), lambda b, i: (b, 0, 0))
    return pl.pallas_call(
        _final_body,
        out_shape=jax.ShapeDtypeStruct((nb, t, D_MODEL), F32),
        grid=(nb, t // tm),
        in_specs=[rowblk, rowblk, mod_spec, pl.BlockSpec((1, D_MODEL), lambda b, i: (0, 0))],
        out_specs=rowblk,
        compiler_params=_cparams(("arbitrary", "arbitrary")),
        name="final",
    )(x1, y, gate2, gain.reshape(1, -1))


def _rope_tables(pos):
    half = HEAD_DIM_A // 2
    inv_freq = ROPE_THETA ** (-jnp.arange(half, dtype=F32) / half)
    ang = pos.astype(F32)[:, None] * inv_freq[None, :]
    cos = jnp.cos(ang)
    sin = jnp.sin(ang)
    reps = LANES // HEAD_DIM_A
    cos_t = jnp.tile(jnp.concatenate([cos, cos], axis=1), (1, reps))
    sin_t = jnp.tile(jnp.concatenate([-sin, sin], axis=1), (1, reps))
    return cos_t, sin_t


def _cache_from_tail(tail, keep):
    outs = []
    n, rows, _ = tail.shape
    for gi, kp in enumerate(keep):
        k = tail[:, rows - kp:, gi * D_GROUP_A:(gi + 1) * D_GROUP_A]
        v = tail[:, rows - kp:, D_A + gi * D_GROUP_A:D_A + (gi + 1) * D_GROUP_A]
        outs.append(jnp.stack([k, v], axis=2).reshape(n, kp, 2, N_HEADS_A, HEAD_DIM_A))
    return outs


def kernel(x_prompt, x_sample, c_prompt, c_sample, cache_a1_kv, cache_a2_kv, cache_a3_kv, state_b_wkv, state_b_shift, w_ada, b_ada, norm_pre_mix, norm_post_mix, norm_pre_ffn, norm_post_ffn, w_in, w_a_out, mu_b, w0_b, w_w2_b, a0_b, w_a2_b, w_g2_b, k_k_b, k_a_b, r_k_b, ln_x_w_b, ln_x_b_b, w_b_out, w_out, w_router, router_bias, w_e_gate, w_e_up, w_e_down, w_s_gate, w_s_up, w_s_down):
    assert DEPTH == 1
    l = 0
    nd = DEC_BATCH
    row = lambda a: a.reshape(1, -1)
    p = {'mu_b': row(mu_b[l]), 'w0_b': row(w0_b[l]), 'w_w2_b': w_w2_b[l], 'a0_b': row(a0_b[l]),
         'w_a2_b': w_a2_b[l], 'w_g2_b': w_g2_b[l], 'k_k_b': row(k_k_b[l]), 'k_a_b': row(k_a_b[l]),
         'r_k_b': row(r_k_b[l]), 'ln_x_w_b': row(ln_x_w_b[l]), 'ln_x_b_b': row(ln_x_b_b[l]),
         'norm_post_mix': norm_post_mix[l], 'norm_pre_ffn': norm_pre_ffn[l]}

    wq = w_in[l][:, :D_QKV].astype(BF16)
    wf = w_in[l][:, D_QKV:D_QKV + D_SHIFT_B].astype(BF16)
    wg = w_in[l][:, D_QKV + D_SHIFT_B:].astype(BF16)
    wa = w_a_out[l].astype(BF16)
    wb = w_b_out[l].astype(BF16)
    wo = w_out[l].astype(BF16)
    wrt = jnp.concatenate([w_router[l].T, jnp.zeros((LANES - N_EXPERTS, D_MODEL), F32)], axis=0)
    rb = router_bias[l].reshape(N_EXPERTS, 1)
    wsg, wsu, wsd = w_s_gate[l].astype(BF16), w_s_up[l].astype(BF16), w_s_down[l].astype(BF16)

    n_c = BATCH + nd
    c_all = jnp.concatenate([c_prompt, c_sample, jnp.zeros((-n_c % 8, D_MODEL), F32)], axis=0)
    mod = _mod_call(c_all, w_ada[l], b_ada[l])
    mod_p = [m.reshape(BATCH, 1, D_MODEL) for m in jnp.split(mod[:BATCH], 6, axis=-1)]
    mod_s = [m.reshape(1, nd, D_MODEL) for m in jnp.split(mod[BATCH:n_c], 6, axis=-1)]

    cos_p, sin_p = _rope_tables(jnp.arange(SEQ, dtype=I32))
    cos_s, sin_s = _rope_tables(jnp.full((nd,), PAST_LEN, I32))

    keep_p = [min(w, SEQ) for w, _ in DILATED_GROUPS]
    tail_rows = max(keep_p)
    dils = tuple(d for _, d in DILATED_GROUPS)

    q0, q1, q2, feat_p, gates_p, tail_p = _inproj_call(
        x_prompt, norm_pre_mix[l], mod_p[1], mod_p[0], cos_p, sin_p, wq, wf, wg,
        tm=256, tail_rows=tail_rows, mod_per_row=False, dils=dils)
    o_parts, lse_parts = [], []
    for gi, qg in enumerate((q0, q1, q2)):
        o, lse = _attn_call(qg, gi)
        o_parts.append(o)
        lse_parts.append(lse)
    ob_p, wkv_p = _wkv_call(feat_p, p)
    x1_p, hp_p, wt_p = _post_call(o_parts, lse_parts, ob_p, gates_p, x_prompt, mod_p[2], mod_p[4], mod_p[3],
                                  p, wa, wb, wo, wrt, rb, tm=256, mod_per_row=False)

    xs3 = x_sample.reshape(1, nd, D_MODEL)
    s0, s1, s2, feat_s, gates_s, tail_s = _inproj_call(
        xs3, norm_pre_mix[l], mod_s[1], mod_s[0], cos_s, sin_s, wq, wf, wg,
        tm=nd, tail_rows=nd, mod_per_row=True, dils=(1, 1, 1))
    qkv_s = jnp.stack([z.reshape(nd, 3, N_HEADS_A, HEAD_DIM_A) for z in (s0, s1, s2)], axis=2)
    qkv_s = qkv_s.reshape(nd, 3 * N_GROUPS_A, N_HEADS_A, HEAD_DIM_A).astype(F32)
    oa_s = _sattn_call(qkv_s, cache_a1_kv[l], cache_a2_kv[l], cache_a3_kv[l])
    r_s, w_s, k_s, v_s, aa_s, bb_s, g_s = _swkv_prep_call(feat_s[0], state_b_shift[l], p)
    nh = nd * N_HEADS_B
    as_row = lambda a: a.reshape(nh, 1, HEAD_DIM_B)
    s_new, y_col = _swkv_step_call(state_b_wkv[l].reshape(nh, HEAD_DIM_B, HEAD_DIM_B), as_row(aa_s), as_row(w_s),
                                   as_row(bb_s), as_row(k_s), as_row(r_s), v_s.reshape(nh, HEAD_DIM_B, 1))
    ob_s = _swkv_fin_call(y_col.reshape(nd, D_B), r_s, k_s, v_s, g_s, p)
    x1_s, hp_s, wt_s = _post_call([oa_s.reshape(1, nd, D_GROUP_A)], None, ob_s.reshape(1, nd, D_B), gates_s, xs3,
                                  mod_s[2], mod_s[4], mod_s[3], p, wa, wb, wo, wrt, rb, tm=nd, mod_per_row=True)

    n_p = BATCH * SEQ
    n_real = n_p + nd
    n_all = -(-n_real // MOE_TILE) * MOE_TILE
    pad = n_all - n_real
    n_blocks = -(-(n_real * TOP_K) // EXPERT_BLOCK) + N_EXPERTS
    n_blocks_pad = -(-n_blocks // LANES) * LANES
    hp_all = jnp.concatenate([hp_p.reshape(n_p, HALF), hp_s[0], jnp.zeros((pad, HALF), I32)], axis=0)
    wt_all = jnp.concatenate([wt_p, wt_s, jnp.full((N_EXPERTS, pad), -1.0, F32)], axis=1)
    dest8, w8, tab, etab = _rank_call(wt_all, n_real, n_blocks, n_blocks_pad)
    xs = _dispatch_call(dest8, etab, hp_all, n_real, n_blocks * EXPERT_BLOCK)
    ys = _ffn_call(tab[0], xs, w_e_gate[l], w_e_up[l], w_e_down[l], n_blocks)
    y_all = _combine_call(dest8, w8, hp_all, wsg, wsu, wsd, ys)
    y_prompt = _final_call(x1_p, y_all[:n_p].reshape(BATCH, SEQ, D_MODEL), mod_p[5], norm_post_ffn[l],
                           tm=512, mod_per_row=False)
    y_sample = _final_call(x1_s, y_all[n_p:n_real].reshape(1, nd, D_MODEL), mod_s[5], norm_post_ffn[l],
                           tm=nd, mod_per_row=True)

    a_p = [z[None] for z in _cache_from_tail(tail_p, keep_p)]
    a_s = [z.reshape(1, nd, DEC_SEQ, 2, N_HEADS_A, HEAD_DIM_A)
           for z in _cache_from_tail(tail_s.reshape(nd, 1, 2 * D_A), [DEC_SEQ] * N_GROUPS_A)]
    shift_p = feat_p[:, -1][None]
    shift_s = feat_s[0][None]
    return (y_prompt, y_sample.reshape(nd, DEC_SEQ, D_MODEL), a_p[0], a_p[1], a_p[2], wkv_p[None], shift_p,
            a_s[0], a_s[1], a_s[2], s_new.reshape(1, nd, N_HEADS_B, HEAD_DIM_B, HEAD_DIM_B), shift_s)
```

```python
import functools
import math

import jax
import jax.numpy as jnp
from jax import lax
from jax.experimental import pallas as pl
from jax.experimental.pallas import tpu as pltpu

F32 = jnp.float32
BF16 = jnp.bfloat16
I32 = jnp.int32

D_MODEL = 1024
BATCH = 2
SEQ = 8192
DEPTH = 1
DEC_BATCH = 32
DEC_SEQ = 1
PAST_LEN = 16384

HEAD_DIM_A = 64
N_HEADS_A = 8
DILATED_GROUPS = ((128, 1), (512, 4), (2048, 16))
N_GROUPS_A = 3
D_GROUP_A = N_HEADS_A * HEAD_DIM_A
D_A = N_GROUPS_A * D_GROUP_A
D_QKV = 3 * D_A
BAND_BLOCK = 128
ROPE_THETA = 10000.0

HEAD_DIM_B = 64
N_HEADS_B = 16
D_B = 1024
DECAY_LORA = 64
AAA_LORA = 64
GATE_LORA = 160
D_SHIFT_B = 3 * D_B + DECAY_LORA + AAA_LORA + GATE_LORA
LN_X_EPS = 64e-5

N_EXPERTS = 64
TOP_K = 8
N_EXPERT_GROUPS = 8
TOPK_GROUPS = 4
D_EXPERT = 256
ROUTED_SCALE = 2.5
EXPERT_BLOCK = 512
NORM_EPS = 1e-6

LANES = 128
WKV_CHUNK = 64
MOE_TILE = 256
VMEM_LIMIT = 56 * 1024 * 1024
ROW_TILE_SUBLANES = D_MODEL // LANES
ZERO_ROWS = 256


def _cparams(sem):
    return pltpu.CompilerParams(dimension_semantics=sem, vmem_limit_bytes=VMEM_LIMIT)


def _dot(a, b):
    return jnp.dot(a, b, preferred_element_type=F32)


def _dot_nt(a, b):
    return lax.dot_general(a, b, (((1,), (1,)), ((), ())), preferred_element_type=F32)


def _dot_tn(a, b):
    return lax.dot_general(a, b, (((0,), (0,)), ((), ())), preferred_element_type=F32)


def _dot_exact(a, b):
    return lax.dot_general(a, b, (((1,), (0,)), ((), ())), precision=lax.Precision.HIGHEST,
                           preferred_element_type=F32)


def _rms(x, gain):
    return x * lax.rsqrt(jnp.mean(x * x, axis=-1, keepdims=True) + NORM_EPS) * gain


def _sigmoid(x):
    return 1.0 / (1.0 + jnp.exp(-x))


def _silu(x):
    return x * _sigmoid(x)


def _softplus(x):
    return jnp.maximum(x, 0.0) + jnp.log(1.0 + jnp.exp(-jnp.abs(x)))


def _mod_body(c_ref, w_ref, b_ref, o_ref):
    s = _silu(c_ref[...]).astype(BF16)
    o_ref[...] = _dot(s, w_ref[...].astype(BF16)) + b_ref[...]


def _mod_call(c_all, w_ada, b_ada):
    rows = c_all.shape[0]
    tn = 1536
    return pl.pallas_call(
        _mod_body,
        out_shape=jax.ShapeDtypeStruct((rows, 6 * D_MODEL), F32),
        grid=(6 * D_MODEL // tn,),
        in_specs=[pl.BlockSpec((rows, D_MODEL), lambda j: (0, 0)),
                  pl.BlockSpec((D_MODEL, tn), lambda j: (0, j)),
                  pl.BlockSpec((1, tn), lambda j: (0, j))],
        out_specs=pl.BlockSpec((rows, tn), lambda j: (0, j)),
        compiler_params=_cparams(("arbitrary",)),
        name="mod",
    )(c_all, w_ada, b_ada.reshape(1, -1))


def _inproj_body(x_ref, g_ref, sc_ref, sh_ref, cos_ref, sin_ref, wq_ref, wf_ref, wg_ref,
                 q0_ref, q1_ref, q2_ref, feat_ref, gate_ref, tail_ref, p_ref, *, dils):
    x = x_ref[0]
    tm = x.shape[0]
    h = _rms(x, g_ref[...]) * (1.0 + sc_ref[0]) + sh_ref[0]
    hb = h.astype(BF16)
    p = _dot(hb, wq_ref[...])
    cos = cos_ref[...]
    sin = sin_ref[...]
    lane = lax.broadcasted_iota(I32, cos.shape, 1)
    first_half = (lane % HEAD_DIM_A) < (HEAD_DIM_A // 2)
    for c in range(2 * D_A // LANES):
        xc = p[:, c * LANES:(c + 1) * LANES]
        partner = jnp.where(first_half, pltpu.roll(xc, LANES - HEAD_DIM_A // 2, 1),
                            pltpu.roll(xc, HEAD_DIM_A // 2, 1))
        rc = xc * cos + partner * sin
        if c < D_A // LANES:
            rc = rc * (HEAD_DIM_A ** -0.5)
        p_ref[c] = rc
        if c >= D_A // LANES:
            tail_ref[0, :, (c - D_A // LANES) * LANES:(c - D_A // LANES + 1) * LANES] = rc
    for c in range(2 * D_A // LANES, D_QKV // LANES):
        p_ref[c] = p[:, c * LANES:(c + 1) * LANES]
    tail_ref[0, :, D_A:] = p[:, 2 * D_A:]
    per_group = D_GROUP_A // LANES
    for gi, (out_ref, dil) in enumerate(zip((q0_ref, q1_ref, q2_ref), dils)):
        for which in range(3):
            for j in range(per_group):
                c = (which * D_A + gi * D_GROUP_A) // LANES + j
                dst = slice(which * D_GROUP_A + j * LANES, which * D_GROUP_A + (j + 1) * LANES)
                if dil == 1:
                    out_ref[0, 0, :, dst] = p_ref[c].astype(BF16)
                else:
                    for r in range(dil):
                        out_ref[0, r, :, dst] = p_ref[c, pl.ds(r, tm // dil, stride=dil), :].astype(BF16)
    feat_ref[0] = _dot(hb, wf_ref[...])
    gate_ref[0] = _sigmoid(_dot(hb, wg_ref[...])).astype(BF16)


def _inproj_call(x, gain, scale, shift, cos_t, sin_t, wq, wf, wg, tm, tail_rows, mod_per_row, dils):
    nb, t, _ = x.shape
    nt = t // tm
    tail_first = (t - tail_rows) // tm
    if mod_per_row:
        mod_spec = pl.BlockSpec((1, tm, D_MODEL), lambda b, i: (b, i, 0))
    else:
        mod_spec = pl.BlockSpec((1, 1, D_MODEL), lambda b, i: (b, 0, 0))
    resident = lambda shp: pl.BlockSpec(shp, lambda b, i: (0, 0), pipeline_mode=pl.Buffered(1))
    q_shapes = tuple(jax.ShapeDtypeStruct((nb, d, t // d, 3 * D_GROUP_A), BF16) for d in dils)
    q_specs = tuple(pl.BlockSpec((1, d, tm // d, 3 * D_GROUP_A), lambda b, i: (b, 0, i, 0)) for d in dils)
    return pl.pallas_call(
        functools.partial(_inproj_body, dils=dils),
        out_shape=q_shapes + (jax.ShapeDtypeStruct((nb, t, D_SHIFT_B), F32),
                              jax.ShapeDtypeStruct((nb, t, 2 * D_MODEL), BF16),
                              jax.ShapeDtypeStruct((nb, tail_rows, 2 * D_A), F32)),
        grid=(nb, nt),
        in_specs=[pl.BlockSpec((1, tm, D_MODEL), lambda b, i: (b, i, 0)),
                  pl.BlockSpec((1, D_MODEL), lambda b, i: (0, 0)),
                  mod_spec, mod_spec,
                  pl.BlockSpec((tm, LANES), lambda b, i: (i, 0)),
                  pl.BlockSpec((tm, LANES), lambda b, i: (i, 0)),
                  resident((D_MODEL, D_QKV)), resident((D_MODEL, D_SHIFT_B)),
                  resident((D_MODEL, 2 * D_MODEL))],
        out_specs=q_specs + (pl.BlockSpec((1, tm, D_SHIFT_B), lambda b, i: (b, i, 0)),
                             pl.BlockSpec((1, tm, 2 * D_MODEL), lambda b, i: (b, i, 0)),
                             pl.BlockSpec((1, tm, 2 * D_A), lambda b, i: (b, jnp.maximum(i - tail_first, 0), 0))),
        scratch_shapes=[pltpu.VMEM((D_QKV // LANES, tm, LANES), F32)],
        compiler_params=_cparams(("arbitrary", "arbitrary")),
        name="inproj",
    )(x, gain.reshape(1, -1), scale, shift, cos_t, sin_t, wq, wf, wg)


def _attn_body(q_ref, kc_ref, kp_ref, vc_ref, vp_ref, o_ref, lse_ref):
    mb = pl.program_id(2)
    q = q_ref[0, 0]
    k = jnp.concatenate([kp_ref[0, 0], kc_ref[0, 0]], axis=0)
    v = jnp.concatenate([vp_ref[0, 0], vc_ref[0, 0]], axis=0)
    qi = lax.broadcasted_iota(I32, (BAND_BLOCK, 2 * BAND_BLOCK), 0)
    ki = lax.broadcasted_iota(I32, (BAND_BLOCK, 2 * BAND_BLOCK), 1)
    dist = qi + BAND_BLOCK - ki
    mask = (dist >= 0) & (dist <= BAND_BLOCK) & ((ki >= BAND_BLOCK) | (mb > 0))
    lane_q = lax.broadcasted_iota(I32, (BAND_BLOCK, LANES), 1)
    lane_k = lax.broadcasted_iota(I32, (2 * BAND_BLOCK, LANES), 1)
    for hp in range(N_HEADS_A // 2):
        sl = slice(hp * LANES, (hp + 1) * LANES)
        qp, kp, vp = q[:, sl], k[:, sl], v[:, sl]
        o_pair = jnp.zeros((BAND_BLOCK, LANES), F32)
        lse_pair = jnp.zeros((BAND_BLOCK, LANES), F32)
        for sub in range(2):
            mq = (lane_q >= HEAD_DIM_A) if sub else (lane_q < HEAD_DIM_A)
            mk = (lane_k >= HEAD_DIM_A) if sub else (lane_k < HEAD_DIM_A)
            s = _dot_nt(jnp.where(mq, qp, jnp.zeros_like(qp)), kp)
            s = jnp.where(mask, s, -jnp.inf)
            mx = jnp.max(s, axis=1, keepdims=True)
            p = jnp.exp(s - mx)
            l = jnp.sum(p, axis=1, keepdims=True)
            pv = _dot(p.astype(BF16), jnp.where(mk, vp, jnp.zeros_like(vp)))
            o_pair = o_pair + pv / l
            lse_pair = jnp.where(mq, mx + jnp.log(l), lse_pair)
        o_ref[0, 0, :, sl] = o_pair.astype(BF16)
        lse_ref[0, 0, :, sl] = lse_pair


def _attn_call(qkv_g, gi):
    b, dil, l, _ = qkv_g.shape
    nb = l // BAND_BLOCK
    blk = (1, 1, BAND_BLOCK, D_GROUP_A)
    cur = lambda which: pl.BlockSpec(blk, lambda bb, r, m: (bb, r, m, which))
    prev = lambda which: pl.BlockSpec(blk, lambda bb, r, m: (bb, r, jnp.maximum(m - 1, 0), which))
    return pl.pallas_call(
        _attn_body,
        out_shape=(jax.ShapeDtypeStruct((b, dil, l, D_GROUP_A), BF16),
                   jax.ShapeDtypeStruct((b, dil, l, D_GROUP_A), F32)),
        grid=(b, dil, nb),
        in_specs=[cur(0), cur(1), prev(1), cur(2), prev(2)],
        out_specs=(pl.BlockSpec(blk, lambda bb, r, m: (bb, r, m, 0)),
                   pl.BlockSpec(blk, lambda bb, r, m: (bb, r, m, 0))),
        compiler_params=_cparams(("arbitrary", "arbitrary", "arbitrary")),
        name=f"attn{gi}",
    )(qkv_g, qkv_g, qkv_g, qkv_g, qkv_g)


def _sattn_body(qkv_ref, b1_ref, b2_ref, b3_ref, o_ref):
    outs, lses = [], []
    for g, (buf_ref, (_, dil)) in enumerate(zip((b1_ref, b2_ref, b3_ref), DILATED_GROUPS)):
        q = qkv_ref[0, g]
        kn = qkv_ref[0, N_GROUPS_A + g]
        vn = qkv_ref[0, 2 * N_GROUPS_A + g]
        kb = buf_ref[0, 0]
        vb = buf_ref[0, 1]
        wb = kb.shape[-1]
        pos = lax.broadcasted_iota(I32, (1, 1, wb), 2)
        s = jnp.sum(kb * q, axis=1, keepdims=True)
        s = jnp.where(pos % dil == 0, s, -jnp.inf)
        sn = jnp.sum(kn * q, axis=1, keepdims=True)
        m = jnp.maximum(jnp.max(s, axis=2, keepdims=True), sn)
        p = jnp.exp(s - m)
        pn = jnp.exp(sn - m)
        l = jnp.sum(p, axis=2, keepdims=True) + pn
        outs.append((jnp.sum(p * vb, axis=2, keepdims=True) + pn * vn) / l)
        lses.append(m + jnp.log(l))
    mx = jnp.maximum(jnp.maximum(lses[0], lses[1]), lses[2])
    es = [jnp.exp(z - mx) for z in lses]
    o_ref[0] = (es[0] * outs[0] + es[1] * outs[1] + es[2] * outs[2]) / (es[0] + es[1] + es[2])


def _sattn_call(qkv_s, c1, c2, c3):
    n = qkv_s.shape[0]
    views, specs = [], []
    for c in (c1, c2, c3):
        wb = c.shape[1]
        views.append(jnp.transpose(c, (0, 2, 3, 4, 1)))
        specs.append(pl.BlockSpec((1, 2, N_HEADS_A, HEAD_DIM_A, wb), lambda b: (b, 0, 0, 0, 0)))
    return pl.pallas_call(
        _sattn_body,
        out_shape=jax.ShapeDtypeStruct((n, N_HEADS_A, HEAD_DIM_A, 1), F32),
        grid=(n,),
        in_specs=[pl.BlockSpec((1, 3 * N_GROUPS_A, N_HEADS_A, HEAD_DIM_A, 1), lambda b: (b, 0, 0, 0, 0))] + specs,
        out_specs=pl.BlockSpec((1, N_HEADS_A, HEAD_DIM_A, 1), lambda b: (b, 0, 0, 0)),
        compiler_params=_cparams(("arbitrary",)),
        name="sattn",
    )(qkv_s, *views)


def _rwkv_features(xs, w0, ww2, a0, wa2, wg2, k_a):
    r = xs[:, :D_B]
    k = xs[:, D_B:2 * D_B]
    v = xs[:, 2 * D_B:3 * D_B]
    xw = xs[:, 3 * D_B:3 * D_B + DECAY_LORA]
    xa = xs[:, 3 * D_B + DECAY_LORA:3 * D_B + DECAY_LORA + AAA_LORA]
    xg = xs[:, 3 * D_B + DECAY_LORA + AAA_LORA:]
    w_log = -_softplus(-(w0 + _dot(jnp.tanh(xw).astype(BF16), ww2.astype(BF16)))) - 0.5
    a = _sigmoid(a0 + _dot(xa.astype(BF16), wa2.astype(BF16)))
    g = _dot(_sigmoid(xg).astype(BF16), wg2.astype(BF16))
    k_h = k * (1.0 + (a - 1.0) * k_a)
    return r, k, v, w_log, a, g, k_h


def _head_norm(kk_h):
    nrm = jnp.sqrt(jnp.sum(kk_h * kk_h, axis=-1, keepdims=True))
    return kk_h / jnp.maximum(nrm, 1e-12)


def _wkv_finish_head(y, r_h, k_h, v_h, g_h, rk_h, lnw_h, lnb_h):
    mean = jnp.mean(y, axis=-1, keepdims=True)
    var = jnp.mean(jnp.square(y - mean), axis=-1, keepdims=True)
    yn = (y - mean) * lax.rsqrt(var + LN_X_EPS) * lnw_h + lnb_h
    bonus = jnp.sum(r_h * k_h * rk_h, axis=-1, keepdims=True) * v_h
    return (yn + bonus) * g_h


def _wkv_body(f_ref, fp_ref, mu_ref, w0_ref, ww2_ref, a0_ref, wa2_ref, wg2_ref, kk_ref, ka_ref,
              rk_ref, lnw_ref, lnb_ref, o_ref, st_ref, s_ref):
    c = pl.program_id(1)
    C = WKV_CHUNK

    @pl.when(c == 0)
    def _():
        s_ref[...] = jnp.zeros_like(s_ref)

    f = f_ref[0]
    prev_last = jnp.where(c == 0, 0.0, fp_ref[0][7:8, :])
    row = lax.broadcasted_iota(I32, f.shape, 0)
    prev = jnp.where(row == 0, prev_last, pltpu.roll(f, 1, 0))
    xs = f + mu_ref[...] * (prev - f)
    r, k, v, w_log, a, g, k_h = _rwkv_features(xs, w0_ref[...], ww2_ref[...], a0_ref[...],
                                               wa2_ref[...], wg2_ref[...], ka_ref[...])
    lw = -jnp.exp(w_log)
    kk = k * kk_ref[...]

    ti = lax.broadcasted_iota(I32, (C, C), 0)
    si = lax.broadcasted_iota(I32, (C, C), 1)
    tri_incl = (ti >= si).astype(BF16)
    l1 = lw.astype(BF16)
    r1 = lw - l1.astype(F32)
    l2 = r1.astype(BF16)
    l3 = (r1 - l2.astype(F32)).astype(BF16)
    cum = _dot(tri_incl, l1) + _dot(tri_incl, l2) + _dot(tri_incl, l3)
    rho = cum[C // 2 - 1:C // 2, :]
    ep = jnp.exp(cum - rho)
    em = jnp.exp(rho - cum)
    e_a = ep * jnp.exp(-lw)
    r_hat = r * ep
    k_hat = k_h * em
    e_r = jnp.exp(rho)
    e_c = jnp.exp(cum[C - 1:C, :] - rho)

    strict = ti > si
    incl = ti >= si
    eye = (ti == si).astype(F32)
    rk = rk_ref[...]
    lnw = lnw_ref[...]
    lnb = lnb_ref[...]
    heads = range(N_HEADS_B)
    sls = [slice(h * HEAD_DIM_B, (h + 1) * HEAD_DIM_B) for h in heads]
    kkn = [_head_norm(kk[:, sl]) for sl in sls]
    a_hat = [-kkn[h] * e_a[:, sls[h]] for h in heads]
    b_hat = [kkn[h] * a[:, sls[h]] * em[:, sls[h]] for h in heads]
    a_hat_b = [z.astype(BF16) for z in a_hat]
    b_hat_b = [z.astype(BF16) for z in b_hat]
    rh = [r_hat[:, sl] for sl in sls]
    vb = [v[:, sl].astype(BF16) for sl in sls]
    bk = [jnp.concatenate([b_hat_b[h], k_hat[:, sls[h]].astype(BF16)], axis=0) for h in heads]
    p = [_dot_nt(jnp.concatenate([a_hat_b[h], rh[h].astype(BF16)], axis=0), bk[h]) for h in heads]
    l_ab = [jnp.where(strict, z[:C, :C], 0.0) for z in p]
    l_ak = [jnp.where(strict, z[:C, C:], 0.0).astype(BF16) for z in p]
    p_rb = [jnp.where(incl, z[C:, :C], 0.0).astype(BF16) for z in p]
    p_rk = [jnp.where(incl, z[C:, C:], 0.0).astype(BF16) for z in p]
    xb = [z.astype(BF16) for z in l_ab]
    tinv = [eye + z for z in l_ab]
    for _ in range(int(math.log2(C)) - 1):
        xb = [_dot(z, z).astype(BF16) for z in xb]
        tinv = [tinv[h] + _dot(tinv[h].astype(BF16), xb[h]) for h in heads]
    tb = [z.astype(BF16) for z in tinv]
    lv = [_dot(l_ak[h], vb[h]).astype(BF16) for h in heads]
    a_bar = [_dot(tb[h], a_hat_b[h]).astype(BF16) for h in heads]
    u_v = [_dot(tb[h], lv[h]).astype(BF16) for h in heads]
    r_bar = [rh[h] + _dot(p_rb[h], a_bar[h]) for h in heads]
    y_v = [_dot(p_rb[h], u_v[h]) + _dot(p_rk[h], vb[h]) for h in heads]
    ab = [_dot_tn(a_bar[h], b_hat_b[h]).astype(BF16) for h in heads]
    n_t = [_dot_tn(jnp.concatenate([u_v[h], vb[h]], axis=0), bk[h]) for h in heads]
    s0 = [s_ref[h] for h in heads]
    sr = [s0[h] * e_r[:, sls[h]] for h in heads]
    y = [_dot_nt((r_bar[h] * e_r[:, sls[h]]).astype(BF16), s0[h].astype(BF16)) + y_v[h] for h in heads]
    s_new = [(sr[h] + _dot(sr[h].astype(BF16), ab[h]) + n_t[h]) * e_c[:, sls[h]] for h in heads]
    for h in heads:
        s_ref[h] = s_new[h]
    outs = [_wkv_finish_head(y[h], r[:, sls[h]], k_h[:, sls[h]], v[:, sls[h]], g[:, sls[h]],
                             rk[:, sls[h]], lnw[:, sls[h]], lnb[:, sls[h]]) for h in heads]
    o_ref[0] = jnp.concatenate(outs, axis=1)

    @pl.when(c == pl.num_programs(1) - 1)
    def _():
        st_ref[0] = s_ref[...]


def _wkv_call(feat, p):
    b, t, _ = feat.shape
    C = WKV_CHUNK
    nc = t // C
    row = lambda n: pl.BlockSpec((1, n), lambda bb, c: (0, 0))
    mat = lambda m, n: pl.BlockSpec((m, n), lambda bb, c: (0, 0))
    return pl.pallas_call(
        _wkv_body,
        out_shape=(jax.ShapeDtypeStruct((b, t, D_B), F32),
                   jax.ShapeDtypeStruct((b, N_HEADS_B, HEAD_DIM_B, HEAD_DIM_B), F32)),
        grid=(b, nc),
        in_specs=[pl.BlockSpec((1, C, D_SHIFT_B), lambda bb, c: (bb, c, 0)),
                  pl.BlockSpec((1, 8, D_SHIFT_B), lambda bb, c: (bb, jnp.maximum(c * (C // 8) - 1, 0), 0)),
                  row(D_SHIFT_B), row(D_B), mat(DECAY_LORA, D_B), row(D_B), mat(AAA_LORA, D_B),
                  mat(GATE_LORA, D_B), row(D_B), row(D_B), row(D_B), row(D_B), row(D_B)],
        out_specs=(pl.BlockSpec((1, C, D_B), lambda bb, c: (bb, c, 0)),
                   pl.BlockSpec((1, N_HEADS_B, HEAD_DIM_B, HEAD_DIM_B), lambda bb, c: (bb, 0, 0, 0))),
        scratch_shapes=[pltpu.VMEM((N_HEADS_B, HEAD_DIM_B, HEAD_DIM_B), F32)],
        compiler_params=_cparams(("arbitrary", "arbitrary")),
        name="wkv",
    )(feat, feat, p['mu_b'], p['w0_b'], p['w_w2_b'], p['a0_b'], p['w_a2_b'], p['w_g2_b'],
      p['k_k_b'], p['k_a_b'], p['r_k_b'], p['ln_x_w_b'], p['ln_x_b_b'])


def _swkv_prep_body(f_ref, sh_ref, mu_ref, w0_ref, ww2_ref, a0_ref, wa2_ref, wg2_ref, kk_ref, ka_ref,
                    r_ref, w_ref, k_ref, v_ref, aa_ref, bb_ref, g_ref):
    f = f_ref[...]
    xs = f + mu_ref[...] * (sh_ref[...] - f)
    r, k, v, w_log, a, g, k_h = _rwkv_features(xs, w0_ref[...], ww2_ref[...], a0_ref[...],
                                               wa2_ref[...], wg2_ref[...], ka_ref[...])
    kk = k * kk_ref[...]
    kkn = jnp.concatenate([_head_norm(kk[:, h * HEAD_DIM_B:(h + 1) * HEAD_DIM_B]) for h in range(N_HEADS_B)],
                          axis=1)
    r_ref[...] = r
    w_ref[...] = jnp.exp(-jnp.exp(w_log))
    k_ref[...] = k_h
    v_ref[...] = v
    aa_ref[...] = -kkn
    bb_ref[...] = kkn * a
    g_ref[...] = g


def _swkv_prep_call(feat_s, shift0, p):
    n = feat_s.shape[0]
    full = lambda a: pl.BlockSpec(a.shape, lambda: tuple(0 for _ in a.shape))
    args = (feat_s, shift0, p['mu_b'], p['w0_b'], p['w_w2_b'], p['a0_b'], p['w_a2_b'], p['w_g2_b'],
            p['k_k_b'], p['k_a_b'])
    return pl.pallas_call(
        _swkv_prep_body,
        out_shape=tuple(jax.ShapeDtypeStruct((n, D_B), F32) for _ in range(7)),
        in_specs=[full(a) for a in args],
        out_specs=tuple(pl.BlockSpec((n, D_B), lambda: (0, 0)) for _ in range(7)),
        compiler_params=pltpu.CompilerParams(vmem_limit_bytes=VMEM_LIMIT),
        name="swkv_prep",
    )(*args)


def _swkv_step_body(s_ref, a_ref, w_ref, b_ref, k_ref, r_ref, v_ref, so_ref, y_ref):
    s = s_ref[...]
    sa = jnp.sum(s * a_ref[...], axis=-1, keepdims=True)
    s2 = s * w_ref[...] + sa * b_ref[...] + v_ref[...] * k_ref[...]
    so_ref[...] = s2
    y_ref[...] = jnp.sum(s2 * r_ref[...], axis=-1, keepdims=True)


def _swkv_step_call(s0, aa, w, bb, k, r, v_col):
    nh = s0.shape[0]
    th = 64
    rowspec = pl.BlockSpec((th, 1, HEAD_DIM_B), lambda i: (i, 0, 0))
    colspec = pl.BlockSpec((th, HEAD_DIM_B, 1), lambda i: (i, 0, 0))
    stspec = pl.BlockSpec((th, HEAD_DIM_B, HEAD_DIM_B), lambda i: (i, 0, 0))
    return pl.pallas_call(
        _swkv_step_body,
        out_shape=(jax.ShapeDtypeStruct((nh, HEAD_DIM_B, HEAD_DIM_B), F32),
                   jax.ShapeDtypeStruct((nh, HEAD_DIM_B, 1), F32)),
        grid=(nh // th,),
        in_specs=[stspec, rowspec, rowspec, rowspec, rowspec, rowspec, colspec],
        out_specs=(stspec, colspec),
        compiler_params=_cparams(("arbitrary",)),
        name="swkv_step",
    )(s0, aa, w, bb, k, r, v_col)


def _swkv_fin_body(y_ref, r_ref, k_ref, v_ref, g_ref, rk_ref, lnw_ref, lnb_ref, o_ref):
    y, r, k, v, g = y_ref[...], r_ref[...], k_ref[...], v_ref[...], g_ref[...]
    rk, lnw, lnb = rk_ref[...], lnw_ref[...], lnb_ref[...]
    outs = []
    for h in range(N_HEADS_B):
        sl = slice(h * HEAD_DIM_B, (h + 1) * HEAD_DIM_B)
        outs.append(_wkv_finish_head(y[:, sl], r[:, sl], k[:, sl], v[:, sl], g[:, sl],
                                     rk[:, sl], lnw[:, sl], lnb[:, sl]))
    o_ref[...] = jnp.concatenate(outs, axis=1)


def _swkv_fin_call(y, r, k, v, g, p):
    n = y.shape[0]
    args = (y, r, k, v, g, p['r_k_b'], p['ln_x_w_b'], p['ln_x_b_b'])
    full = lambda a: pl.BlockSpec(a.shape, lambda: (0, 0))
    return pl.pallas_call(
        _swkv_fin_body,
        out_shape=jax.ShapeDtypeStruct((n, D_B), F32),
        in_specs=[full(a) for a in args],
        out_specs=pl.BlockSpec((n, D_B), lambda: (0, 0)),
        name="swkv_fin",
    )(*args)


def _route_t(scores, bias_col):
    n = scores.shape[1]
    gsz = N_EXPERTS // N_EXPERT_GROUPS
    choice = scores + bias_col
    ninf = -jnp.inf
    sid = lax.broadcasted_iota(I32, (gsz, n), 0)
    gs = []
    for gidx in range(N_EXPERT_GROUPS):
        blk = choice[gidx * gsz:(gidx + 1) * gsz, :]
        m1 = jnp.max(blk, axis=0, keepdims=True)
        first = jnp.min(jnp.where(blk == m1, sid, gsz), axis=0, keepdims=True)
        m2 = jnp.max(jnp.where(sid == first, ninf, blk), axis=0, keepdims=True)
        gs.append(m1 + m2)
    cur = jnp.concatenate(gs, axis=0)
    gid = lax.broadcasted_iota(I32, (N_EXPERT_GROUPS, n), 0)
    gmask = jnp.zeros((N_EXPERT_GROUPS, n), F32)
    for _ in range(TOPK_GROUPS):
        m = jnp.max(cur, axis=0, keepdims=True)
        first = jnp.min(jnp.where(cur == m, gid, N_EXPERT_GROUPS), axis=0, keepdims=True)
        sel = gid == first
        gmask = jnp.where(sel, 1.0, gmask)
        cur = jnp.where(sel, ninf, cur)
    emask = jnp.concatenate([jnp.broadcast_to(gmask[gidx:gidx + 1, :], (gsz, n))
                             for gidx in range(N_EXPERT_GROUPS)], axis=0)
    cur = jnp.where(emask > 0.5, choice, ninf)
    eid = lax.broadcasted_iota(I32, (N_EXPERTS, n), 0)
    selm = jnp.zeros((N_EXPERTS, n), F32)
    for _ in range(TOP_K):
        m = jnp.max(cur, axis=0, keepdims=True)
        first = jnp.min(jnp.where(cur == m, eid, N_EXPERTS), axis=0, keepdims=True)
        sel = eid == first
        selm = jnp.where(sel, 1.0, selm)
        cur = jnp.where(sel, ninf, cur)
    w = jnp.where(selm > 0.5, scores, 0.0)
    w = w / jnp.sum(w, axis=0, keepdims=True) * ROUTED_SCALE
    return jnp.where(selm > 0.5, w, -1.0)


def _unpermute(blk_ref, scr_ref, dil, tm):
    if dil == 1:
        return blk_ref[0, 0].astype(F32)
    n_chunks = scr_ref.shape[0]
    for r in range(dil):
        rows = blk_ref[0, r].astype(F32)
        for j in range(n_chunks):
            scr_ref[j, pl.ds(r, tm // dil, stride=dil), :] = rows[:, j * LANES:(j + 1) * LANES]
    return jnp.concatenate([scr_ref[j] for j in range(n_chunks)], axis=1)


def _post_body(*refs, combine, dils):
    if combine:
        o_refs, l_refs, rest = refs[:3], refs[3:6], refs[6:]
    else:
        o_refs, rest = refs[:1], refs[1:]
    (ob_ref, gt_ref, x_ref, g1_ref, sc2_ref, sh2_ref, npost_ref, npre_ref, wa_ref, wb_ref, wo_ref,
     wrt_ref, rb_ref, x1_ref, hp_ref, wt_ref) = rest[:16]
    scr = rest[16:]
    tm = x_ref.shape[1]
    if combine:
        os_, ls_ = [], []
        si = 0
        for gi, dil in enumerate(dils):
            os_.append(_unpermute(o_refs[gi], scr[si] if dil > 1 else None, dil, tm))
            ls_.append(_unpermute(l_refs[gi], scr[si + 1] if dil > 1 else None, dil, tm))
            si += 2 if dil > 1 else 0
        mx = jnp.maximum(jnp.maximum(ls_[0], ls_[1]), ls_[2])
        es = [jnp.exp(z - mx) for z in ls_]
        o_a = (es[0] * os_[0] + es[1] * os_[1] + es[2] * os_[2]) / (es[0] + es[1] + es[2])
    else:
        o_a = o_refs[0][0]
    gt = gt_ref[0].astype(F32)
    za = _dot(o_a.astype(BF16), wa_ref[...])
    zb = _dot(ob_ref[0].astype(BF16), wb_ref[...])
    merged = gt[:, :D_MODEL] * za + gt[:, D_MODEL:] * zb
    z = _dot(merged.astype(BF16), wo_ref[...])
    x1 = x_ref[0] + g1_ref[0] * _rms(z, npost_ref[...])
    x1_ref[0] = x1
    h2 = _rms(x1, npre_ref[...]) * (1.0 + sc2_ref[0]) + sh2_ref[0]
    for s in range(ROW_TILE_SUBLANES):
        hp_ref[0, pl.ds(s, tm, stride=ROW_TILE_SUBLANES), :] = h2[:, s * LANES:(s + 1) * LANES]
    tp =-(-tm // LANES) * LANES
    if tp != tm:
        h2 = jnp.concatenate([h2, jnp.zeros((tp - tm, D_MODEL), F32)], axis=0)
    logits_t = lax.dot_general(wrt_ref[...], h2, (((1,), (1,)), ((), ())),
                               precision=lax.Precision.HIGHEST, preferred_element_type=F32)
    w = _route_t(_sigmoid(logits_t[:N_EXPERTS, :]), rb_ref[...])
    wt_ref[...] = w[:, :tm]


def _post_call(o_parts, lse_parts, ob, gates, x, gate1, scale2, shift2, p, wa, wb, wo, wrt, rb, tm, mod_per_row):
    nb, t, _ = x.shape
    nt = t // tm
    combine = lse_parts is not None
    rowblk = lambda width: pl.BlockSpec((1, tm, width), lambda b, i: (b, i, 0))
    if mod_per_row:
        mod_spec = rowblk(D_MODEL)
    else:
        mod_spec = pl.BlockSpec((1, 1, D_MODEL), lambda b, i: (b, 0, 0))
    const = lambda shp: pl.BlockSpec(shp, lambda b, i: (0, 0))
    scratch = []
    if combine:
        dils = tuple(o.shape[1] for o in o_parts)
        o_args = list(o_parts) + list(lse_parts)
        o_specs = [pl.BlockSpec((1, d, tm // d, D_GROUP_A), lambda b, i: (b, 0, i, 0)) for d in dils] * 2
        for d in dils:
            if d > 1:
                scratch += [pltpu.VMEM((D_GROUP_A // LANES, tm, LANES), F32)] * 2
    else:
        dils = ()
        o_args = [o_parts[0]]
        o_specs = [rowblk(D_GROUP_A)]
    return pl.pallas_call(
        functools.partial(_post_body, combine=combine, dils=dils),
        out_shape=(jax.ShapeDtypeStruct((nb, t, D_MODEL), F32),
                   jax.ShapeDtypeStruct((nb, t * ROW_TILE_SUBLANES, LANES), F32),
                   jax.ShapeDtypeStruct((N_EXPERTS, nb * t), F32)),
        grid=(nb, nt),
        in_specs=o_specs + [rowblk(D_B), rowblk(2 * D_MODEL), rowblk(D_MODEL),
                            mod_spec, mod_spec, mod_spec, const((1, D_MODEL)), const((1, D_MODEL)),
                            const((D_GROUP_A, D_MODEL)), const((D_B, D_MODEL)), const((D_MODEL, D_MODEL)),
                            const((LANES, D_MODEL)), const((N_EXPERTS, 1))],
        out_specs=(rowblk(D_MODEL),
                   pl.BlockSpec((1, tm * ROW_TILE_SUBLANES, LANES), lambda b, i: (b, i, 0)),
                   pl.BlockSpec((N_EXPERTS, tm), lambda b, i: (0, b * nt + i))),
        scratch_shapes=scratch,
        compiler_params=_cparams(("arbitrary", "arbitrary")),
        name="post",
    )(*o_args, ob, gates, x, gate1, scale2, shift2, p['norm_post_mix'].reshape(1, -1),
      p['norm_pre_ffn'].reshape(1, -1), wa, wb, wo, wrt, rb)


def _rank_body(w_ref, dest_ref, w8_ref, tab_ref, etab_ref, cnt_ref, pst_ref, run_ref, *, n_real, n_slots):
    ph = pl.program_id(0)
    i = pl.program_id(1)
    T = MOE_TILE
    w = w_ref[...]
    sel = (w >= 0.0).astype(F32)
    cnt_tile = jnp.broadcast_to(jnp.sum(sel, axis=1, keepdims=True), (N_EXPERTS, LANES))
    ei = lax.broadcasted_iota(I32, (N_EXPERTS, N_EXPERTS), 0)
    ej = lax.broadcasted_iota(I32, (N_EXPERTS, N_EXPERTS), 1)

    @pl.when((ph == 0) & (i == 0))
    def _():
        cnt_ref[...] = jnp.zeros_like(cnt_ref)

    @pl.when(ph == 0)
    def _():
        cnt_ref[...] += cnt_tile

    @pl.when((ph == 1) & (i == 0))
    def _():
        cnt = cnt_ref[...]
        padded = jnp.floor((cnt + (EXPERT_BLOCK - 1)) / EXPERT_BLOCK) * EXPERT_BLOCK
        pstart = _dot_exact((ej < ei).astype(F32), padded)
        pst_ref[...] = pstart
        run_ref[...] = jnp.zeros_like(run_ref)
        pend = pstart + padded
        vend = pstart + cnt
        esub = lax.broadcasted_iota(I32, (N_EXPERTS, LANES), 0)
        lane = lax.broadcasted_iota(I32, (1, LANES), 1)
        tab_ref[...] = jnp.zeros_like(tab_ref)
        for c in range(tab_ref.shape[1] // LANES):
            bs = ((c * LANES + lane) * EXPERT_BLOCK).astype(F32)
            be = jnp.minimum(jnp.sum((pend <= bs).astype(F32), axis=0, keepdims=True), N_EXPERTS - 1.0)
            tab_ref[0:1, c * LANES:(c + 1) * LANES] = be.astype(I32)
            tab_ref[1:2, c * LANES:(c + 1) * LANES] = (pend[N_EXPERTS - 1:, :] / EXPERT_BLOCK).astype(I32)
        on_diag = esub == lax.broadcasted_iota(I32, (N_EXPERTS, LANES), 1)
        etab_ref[...] = jnp.zeros_like(etab_ref)
        lo = jnp.sum(jnp.where(on_diag, vend, 0.0), axis=0, keepdims=True)
        hi = jnp.sum(jnp.where(on_diag, pend, 0.0), axis=0, keepdims=True)
        etab_ref[0:1, :] = jnp.where(lane == N_EXPERTS, pend[N_EXPERTS - 1:, :], lo).astype(I32)
        etab_ref[1:2, :] = jnp.where(lane == N_EXPERTS, float(n_slots), hi).astype(I32)

    @pl.when(ph == 1)
    def _():
        ti = lax.broadcasted_iota(I32, (T, T), 0)
        tj = lax.broadcasted_iota(I32, (T, T), 1)
        selb = sel.astype(BF16)
        rank = _dot(selb, (ti < tj).astype(BF16))
        ordn = _dot((ej < ei).astype(BF16), selb)
        dest_e = pst_ref[:, :1] + run_ref[:, :1] + rank
        run_ref[...] += cnt_tile
        tok = i * T + lax.broadcasted_iota(I32, (1, T), 1)
        dks, wks = [], []
        for k in range(TOP_K):
            m = (sel > 0.5) & (ordn == float(k))
            dk = jnp.sum(jnp.where(m, dest_e, 0.0), axis=0, keepdims=True)
            wk = jnp.sum(jnp.where(m, w, 0.0), axis=0, keepdims=True)
            dks.append(jnp.where(tok < n_real, dk, 0.0))
            wks.append(jnp.where(tok < n_real, wk, 0.0))
        dest_ref[...] = jnp.concatenate(dks, axis=0).astype(I32)
        w8_ref[...] = jnp.concatenate(wks, axis=0)


def _rank_call(w_t, n_real, n_blocks, n_blocks_pad):
    n = w_t.shape[1]
    nt = n // MOE_TILE
    return pl.pallas_call(
        functools.partial(_rank_body, n_real=n_real, n_slots=n_blocks * EXPERT_BLOCK),
        out_shape=(jax.ShapeDtypeStruct((TOP_K, n), I32),
                   jax.ShapeDtypeStruct((TOP_K, n), F32),
                   jax.ShapeDtypeStruct((8, n_blocks_pad), I32),
                   jax.ShapeDtypeStruct((8, LANES), I32)),
        grid=(2, nt),
        in_specs=[pl.BlockSpec((N_EXPERTS, MOE_TILE), lambda ph, i: (0, i))],
        out_specs=(pl.BlockSpec((TOP_K, MOE_TILE), lambda ph, i: (0, i * ph)),
                   pl.BlockSpec((TOP_K, MOE_TILE), lambda ph, i: (0, i * ph)),
                   pl.BlockSpec((8, n_blocks_pad), lambda ph, i: (0, 0)),
                   pl.BlockSpec((8, LANES), lambda ph, i: (0, 0))),
        scratch_shapes=[pltpu.VMEM((N_EXPERTS, LANES), F32)] * 3,
        compiler_params=_cparams(("arbitrary", "arbitrary")),
        name="rank",
    )(w_t)


def _tile_rows(ref, row, n):
    return ref.at[pl.ds(pl.multiple_of(row * ROW_TILE_SUBLANES, ROW_TILE_SUBLANES), n * ROW_TILE_SUBLANES)]


def _zero_fill(etab_ref, zbuf, xs_hbm, zsem, wait):
    def go(src, dst):
        cp = pltpu.make_async_copy(src, dst, zsem)
        if wait:
            cp.wait()
        else:
            cp.start()

    def per_range(e, carry):
        lo = etab_ref[0, e]
        n = etab_ref[1, e] - lo
        n_full = n // ZERO_ROWS

        def full(j, c):
            go(zbuf, _tile_rows(xs_hbm, lo + j * ZERO_ROWS, ZERO_ROWS))
            return c

        lax.fori_loop(0, n_full, full, 0)
        pos = lo + n_full * ZERO_ROWS
        rem = n - n_full * ZERO_ROWS
        size = ZERO_ROWS // 2
        while size >= 1:
            bit = rem & size

            @pl.when(bit != 0)
            def _(size=size, pos=pos):
                go(_tile_rows(zbuf, 0, size), _tile_rows(xs_hbm, pos, size))

            pos = pos + bit
            size //= 2
        return carry

    lax.fori_loop(0, N_EXPERTS + 1, per_range, 0)


def _dispatch_body(dest_ref, etab_ref, x_ref, xs_hbm, zbuf, sem, zsem, *, n_real):
    i = pl.program_id(0)
    T = MOE_TILE
    n_tok = jnp.clip(n_real - i * T, 0, T)

    def row_copy(t, dst_row):
        return pltpu.make_async_copy(_tile_rows(x_ref, t, 1), _tile_rows(xs_hbm, dst_row, 1), sem)

    def issue(t, carry):
        for k in range(TOP_K):
            row_copy(t, dest_ref[k, t]).start(priority=k % 2)
        return carry

    lax.fori_loop(0, n_tok, issue, 0)

    @pl.when(i == 0)
    def _():
        zbuf[...] = jnp.zeros_like(zbuf)
        _zero_fill(etab_ref, zbuf, xs_hbm, zsem, wait=False)
        _zero_fill(etab_ref, zbuf, xs_hbm, zsem, wait=True)

    @pl.when(n_tok == T)
    def _():
        pltpu.make_async_copy(_tile_rows(xs_hbm, 0, T * TOP_K), _tile_rows(xs_hbm, 0, T * TOP_K), sem).wait()

    @pl.when(n_tok < T)
    def _():
        def drain(j, carry):
            row_copy(0, 0).wait()
            return carry

        lax.fori_loop(0, n_tok * TOP_K, drain, 0)


def _dispatch_call(dest8, etab, hp_all, n_real, n_slots):
    n = hp_all.shape[0] // ROW_TILE_SUBLANES
    return pl.pallas_call(
        functools.partial(_dispatch_body, n_real=n_real),
        out_shape=jax.ShapeDtypeStruct((n_slots * ROW_TILE_SUBLANES, LANES), F32),
        grid=(n // MOE_TILE,),
        in_specs=[pl.BlockSpec((TOP_K, MOE_TILE), lambda i: (0, i), memory_space=pltpu.SMEM),
                  pl.BlockSpec((8, LANES), lambda i: (0, 0), memory_space=pltpu.SMEM),
                  pl.BlockSpec((MOE_TILE * ROW_TILE_SUBLANES, LANES), lambda i: (i, 0))],
        out_specs=pl.BlockSpec(memory_space=pl.ANY),
        scratch_shapes=[pltpu.VMEM((ZERO_ROWS * ROW_TILE_SUBLANES, LANES), F32),
                        pltpu.SemaphoreType.DMA, pltpu.SemaphoreType.DMA],
        compiler_params=_cparams(("arbitrary",)),
        name="dispatch",
    )(dest8, etab, hp_all)


def _rows_from_tiles(ref, lo, n):
    return jnp.concatenate([ref[pl.ds(lo * ROW_TILE_SUBLANES + s, n, stride=ROW_TILE_SUBLANES), :]
                            for s in range(ROW_TILE_SUBLANES)], axis=1)


def _ffn_body(be_ref, xs_ref, wg_ref, wu_ref, wd_ref, ys_ref, wgb, wub, wdb):
    j = pl.program_id(0)

    @pl.when((j == 0) | (be_ref[j] != be_ref[jnp.maximum(j - 1, 0)]))
    def _():
        wgb[...] = wg_ref[0].astype(BF16)
        wub[...] = wu_ref[0].astype(BF16)
        wdb[...] = wd_ref[0].astype(BF16)

    x = _rows_from_tiles(xs_ref, 0, EXPERT_BLOCK).astype(BF16)
    act = _silu(_dot(x, wgb[...])) * _dot(x, wub[...])
    y = _dot(act.astype(BF16), wdb[...])
    for s in range(ROW_TILE_SUBLANES):
        ys_ref[pl.ds(s, EXPERT_BLOCK, stride=ROW_TILE_SUBLANES), :] = y[:, s * LANES:(s + 1) * LANES]


def _ffn_call(blk_e, xs, w_gate, w_up, w_down, n_blocks):
    tile_blk = pl.BlockSpec((EXPERT_BLOCK * ROW_TILE_SUBLANES, LANES), lambda j, be: (j, 0))
    grid_spec = pltpu.PrefetchScalarGridSpec(
        num_scalar_prefetch=1,
        grid=(n_blocks,),
        in_specs=[tile_blk,
                  pl.BlockSpec((1, D_MODEL, D_EXPERT), lambda j, be: (be[j], 0, 0)),
                  pl.BlockSpec((1, D_MODEL, D_EXPERT), lambda j, be: (be[j], 0, 0)),
                  pl.BlockSpec((1, D_EXPERT, D_MODEL), lambda j, be: (be[j], 0, 0))],
        out_specs=tile_blk,
        scratch_shapes=[pltpu.VMEM((D_MODEL, D_EXPERT), BF16), pltpu.VMEM((D_MODEL, D_EXPERT), BF16),
                        pltpu.VMEM((D_EXPERT, D_MODEL), BF16)])
    return pl.pallas_call(
        _ffn_body,
        out_shape=jax.ShapeDtypeStruct((n_blocks * EXPERT_BLOCK * ROW_TILE_SUBLANES, LANES), F32),
        grid_spec=grid_spec,
        compiler_params=_cparams(("arbitrary",)),
        name="ffn",
    )(blk_e, xs, w_gate, w_up, w_down)


def _combine_body(dest_ref, w8_ref, hp_ref, sg_ref, su_ref, sd_ref, ys_hbm, y_ref, buf, sem):
    T = MOE_TILE

    def issue(t, carry):
        for k in range(TOP_K):
            pltpu.make_async_copy(_tile_rows(ys_hbm, dest_ref[k, t], 1), _tile_rows(buf, k * T + t, 1),
                                  sem).start(priority=k % 2)
        return carry

    lax.fori_loop(0, T, issue, 0)
    x = _rows_from_tiles(hp_ref, 0, T).astype(BF16)
    y = _dot((_silu(_dot(x, sg_ref[...])) * _dot(x, su_ref[...])).astype(BF16), sd_ref[...])
    w_t = jnp.concatenate([w8_ref[...], jnp.zeros((LANES - TOP_K, T), F32)], axis=0).T
    pltpu.make_async_copy(_tile_rows(ys_hbm, 0, T * TOP_K), buf, sem).wait()
    for k in range(TOP_K):
        y = y + w_t[:, k:k + 1] * _rows_from_tiles(buf, k * T, T)
    y_ref[...] = y


def _combine_call(dest8, w8, hp_all, wsg, wsu, wsd, ys):
    n = hp_all.shape[0] // ROW_TILE_SUBLANES
    T = MOE_TILE
    const = lambda shp: pl.BlockSpec(shp, lambda i: (0, 0))
    return pl.pallas_call(
        _combine_body,
        out_shape=jax.ShapeDtypeStruct((n, D_MODEL), F32),
        grid=(n // T,),
        in_specs=[pl.BlockSpec((TOP_K, T), lambda i: (0, i), memory_space=pltpu.SMEM),
                  pl.BlockSpec((TOP_K, T), lambda i: (0, i)),
                  pl.BlockSpec((T * ROW_TILE_SUBLANES, LANES), lambda i: (i, 0)),
                  const((D_MODEL, D_EXPERT)), const((D_MODEL, D_EXPERT)), const((D_EXPERT, D_MODEL)),
                  pl.BlockSpec(memory_space=pl.ANY)],
        out_specs=pl.BlockSpec((T, D_MODEL), lambda i: (i, 0)),
        scratch_shapes=[pltpu.VMEM((T * TOP_K * ROW_TILE_SUBLANES, LANES), F32), pltpu.SemaphoreType.DMA],
        compiler_params=_cparams(("arbitrary",)),
        name="combine",
    )(dest8, w8, hp_all, wsg, wsu, wsd, ys)


def _final_body(x1_ref, y_ref, g2_ref, n_ref, o_ref):
    o_ref[0] = x1_ref[0] + g2_ref[0] * _rms(y_ref[0], n_ref[...])


def _final_call(x1, y, gate2, gain, tm, mod_per_row):
    nb, t, _ = x1.shape
    rowblk = pl.BlockSpec((1, tm, D_MODEL), lambda b, i: (b, i, 0))
    mod_spec = rowblk if mod_per_row else pl.BlockSpec((1, 1, D_MODEL), lambda b, i: (b, 0, 0))
    return pl.pallas_call(
        _final_body,
        out_shape=jax.ShapeDtypeStruct((nb, t, D_MODEL), F32),
        grid=(nb, t // tm),
        in_specs=[rowblk, rowblk, mod_spec, pl.BlockSpec((1, D_MODEL), lambda b, i: (0, 0))],
        out_specs=rowblk,
        compiler_params=_cparams(("arbitrary", "arbitrary")),
        name="final",
    )(x1, y, gate2, gain.reshape(1, -1))


def _rope_tables(pos):
    half = HEAD_DIM_A // 2
    inv_freq = ROPE_THETA ** (-jnp.arange(half, dtype=F32) / half)
    ang = pos.astype(F32)[:, None] * inv_freq[None, :]
    cos = jnp.cos(ang)
    sin = jnp.sin(ang)
    reps = LANES // HEAD_DIM_A
    cos_t = jnp.tile(jnp.concatenate([cos, cos], axis=1), (1, reps))
    sin_t = jnp.tile(jnp.concatenate([-sin, sin], axis=1), (1, reps))
    return cos_t, sin_t


def _cache_from_tail(tail, keep):
    outs = []
    n, rows, _ = tail.shape
    for gi, kp in enumerate(keep):
        k = tail[:, rows - kp:, gi * D_GROUP_A:(gi + 1) * D_GROUP_A]
        v = tail[:, rows - kp:, D_A + gi * D_GROUP_A:D_A + (gi + 1) * D_GROUP_A]
        outs.append(jnp.stack([k, v], axis=2).reshape(n, kp, 2, N_HEADS_A, HEAD_DIM_A))
    return outs


def kernel(x_prompt, x_sample, c_prompt, c_sample, cache_a1_kv, cache_a2_kv, cache_a3_kv, state_b_wkv, state_b_shift, w_ada, b_ada, norm_pre_mix, norm_post_mix, norm_pre_ffn, norm_post_ffn, w_in, w_a_out, mu_b, w0_b, w_w2_b, a0_b, w_a2_b, w_g2_b, k_k_b, k_a_b, r_k_b, ln_x_w_b, ln_x_b_b, w_b_out, w_out, w_router, router_bias, w_e_gate, w_e_up, w_e_down, w_s_gate, w_s_up, w_s_down):
    assert DEPTH == 1
    l = 0
    nd = DEC_BATCH
    row = lambda a: a.reshape(1, -1)
    p = {'mu_b': row(mu_b[l]), 'w0_b': row(w0_b[l]), 'w_w2_b': w_w2_b[l], 'a0_b': row(a0_b[l]),
         'w_a2_b': w_a2_b[l], 'w_g2_b': w_g2_b[l], 'k_k_b': row(k_k_b[l]), 'k_a_b': row(k_a_b[l]),
         'r_k_b': row(r_k_b[l]), 'ln_x_w_b': row(ln_x_w_b[l]), 'ln_x_b_b': row(ln_x_b_b[l]),
         'norm_post_mix': norm_post_mix[l], 'norm_pre_ffn': norm_pre_ffn[l]}

    wq = w_in[l][:, :D_QKV].astype(BF16)
    wf = w_in[l][:, D_QKV:D_QKV + D_SHIFT_B].astype(BF16)
    wg = w_in[l][:, D_QKV + D_SHIFT_B:].astype(BF16)
    wa = w_a_out[l].astype(BF16)
    wb = w_b_out[l].astype(BF16)
    wo = w_out[l].astype(BF16)
    wrt = jnp.concatenate([w_router[l].T, jnp.zeros((LANES - N_EXPERTS, D_MODEL), F32)], axis=0)
    rb = router_bias[l].reshape(N_EXPERTS, 1)
    wsg, wsu, wsd = w_s_gate[l].astype(BF16), w_s_up[l].astype(BF16), w_s_down[l].astype(BF16)

    n_c = BATCH + nd
    c_all = jnp.concatenate([c_prompt, c_sample, jnp.zeros((-n_c % 8, D_MODEL), F32)], axis=0)
    mod = _mod_call(c_all, w_ada[l], b_ada[l])
    mod_p = [m.reshape(BATCH, 1, D_MODEL) for m in jnp.split(mod[:BATCH], 6, axis=-1)]
    mod_s = [m.reshape(1, nd, D_MODEL) for m in jnp.split(mod[BATCH:n_c], 6, axis=-1)]

    cos_p, sin_p = _rope_tables(jnp.arange(SEQ, dtype=I32))
    cos_s, sin_s = _rope_tables(jnp.full((nd,), PAST_LEN, I32))

    keep_p = [min(w, SEQ) for w, _ in DILATED_GROUPS]
    tail_rows = max(keep_p)
    dils = tuple(d for _, d in DILATED_GROUPS)

    q0, q1, q2, feat_p, gates_p, tail_p = _inproj_call(
        x_prompt, norm_pre_mix[l], mod_p[1], mod_p[0], cos_p, sin_p, wq, wf, wg,
        tm=256, tail_rows=tail_rows, mod_per_row=False, dils=dils)
    o_parts, lse_parts = [], []
    for gi, qg in enumerate((q0, q1, q2)):
        o, lse = _attn_call(qg, gi)
        o_parts.append(o)
        lse_parts.append(lse)
    ob_p, wkv_p = _wkv_call(feat_p, p)
    x1_p, hp_p, wt_p = _post_call(o_parts, lse_parts, ob_p, gates_p, x_prompt, mod_p[2], mod_p[4], mod_p[3],
                                  p, wa, wb, wo, wrt, rb, tm=256, mod_per_row=False)

    xs3 = x_sample.reshape(1, nd, D_MODEL)
    s0, s1, s2, feat_s, gates_s, tail_s = _inproj_call(
        xs3, norm_pre_mix[l], mod_s[1], mod_s[0], cos_s, sin_s, wq, wf, wg,
        tm=nd, tail_rows=nd, mod_per_row=True, dils=(1, 1, 1))
    qkv_s = jnp.stack([z.reshape(nd, 3, N_HEADS_A, HEAD_DIM_A) for z in (s0, s1, s2)], axis=2)
    qkv_s = qkv_s.reshape(nd, 3 * N_GROUPS_A, N_HEADS_A, HEAD_DIM_A, 1).astype(F32)
    oa_s = _sattn_call(qkv_s, cache_a1_kv[l], cache_a2_kv[l], cache_a3_kv[l])
    r_s, w_s, k_s, v_s, aa_s, bb_s, g_s = _swkv_prep_call(feat_s[0], state_b_shift[l], p)
    nh = nd * N_HEADS_B
    as_row = lambda a: a.reshape(nh, 1, HEAD_DIM_B)
    s_new, y_col = _swkv_step_call(state_b_wkv[l].reshape(nh, HEAD_DIM_B, HEAD_DIM_B), as_row(aa_s), as_row(w_s),
                                   as_row(bb_s), as_row(k_s), as_row(r_s), v_s.reshape(nh, HEAD_DIM_B, 1))
    ob_s = _swkv_fin_call(y_col.reshape(nd, D_B), r_s, k_s, v_s, g_s, p)
    x1_s, hp_s, wt_s = _post_call([oa_s.reshape(1, nd, D_GROUP_A)], None, ob_s.reshape(1, nd, D_B), gates_s, xs3,
                                  mod_s[2], mod_s[4], mod_s[3], p, wa, wb, wo, wrt, rb, tm=nd, mod_per_row=True)

    n_p = BATCH * SEQ
    n_real = n_p + nd
    n_all = -(-n_real // MOE_TILE) * MOE_TILE
    pad = n_all - n_real
    n_blocks = -(-(n_real * TOP_K) // EXPERT_BLOCK) + N_EXPERTS
    n_blocks_pad = -(-n_blocks // LANES) * LANES
    hp_all = jnp.concatenate([hp_p.reshape(n_p * ROW_TILE_SUBLANES, LANES), hp_s[0],
                              jnp.zeros((pad * ROW_TILE_SUBLANES, LANES), F32)], axis=0)
    wt_all = jnp.concatenate([wt_p, wt_s, jnp.full((N_EXPERTS, pad), -1.0, F32)], axis=1)
    dest8, w8, tab, etab = _rank_call(wt_all, n_real, n_blocks, n_blocks_pad)
    xs = _dispatch_call(dest8, etab, hp_all, n_real, n_blocks * EXPERT_BLOCK)
    ys = _ffn_call(tab[0], xs, w_e_gate[l], w_e_up[l], w_e_down[l], n_blocks)
    y_all = _combine_call(dest8, w8, hp_all, wsg, wsu, wsd, ys)
    y_prompt = _final_call(x1_p, y_all[:n_p].reshape(BATCH, SEQ, D_MODEL), mod_p[5], norm_post_ffn[l],
                           tm=512, mod_per_row=False)
    y_sample = _final_call(x1_s, y_all[n_p:n_real].reshape(1, nd, D_MODEL), mod_s[5], norm_post_ffn[l],
                           tm=nd, mod_per_row=True)

    a_p = [z[None] for z in _cache_from_tail(tail_p, keep_p)]
    a_s = [z.reshape(1, nd, DEC_SEQ, 2, N_HEADS_A, HEAD_DIM_A)
           for z in _cache_from_tail(tail_s.reshape(nd, 1, 2 * D_A), [DEC_SEQ] * N_GROUPS_A)]
    shift_p = feat_p[:, -1][None]
    shift_s = feat_s[0][None]
    return (y_prompt, y_sample.reshape(nd, DEC_SEQ, D_MODEL), a_p[0], a_p[1], a_p[2], wkv_p[None], shift_p,
            a_s[0], a_s[1], a_s[2], s_new.reshape(1, nd, N_HEADS_B, HEAD_DIM_B, HEAD_DIM_B), shift_s)
```

```python
import functools
import math

import jax
import jax.numpy as jnp
from jax import lax
from jax.experimental import pallas as pl
from jax.experimental.pallas import tpu as pltpu

F32 = jnp.float32
BF16 = jnp.bfloat16
I32 = jnp.int32

D_MODEL = 1024
BATCH = 2
SEQ = 8192
DEPTH = 1
DEC_BATCH = 32
DEC_SEQ = 1
PAST_LEN = 16384

HEAD_DIM_A = 64
N_HEADS_A = 8
DILATED_GROUPS = ((128, 1), (512, 4), (2048, 16))
N_GROUPS_A = 3
D_GROUP_A = N_HEADS_A * HEAD_DIM_A
D_A = N_GROUPS_A * D_GROUP_A
D_QKV = 3 * D_A
BAND_BLOCK = 128
ROPE_THETA = 10000.0

HEAD_DIM_B = 64
N_HEADS_B = 16
D_B = 1024
DECAY_LORA = 64
AAA_LORA = 64
GATE_LORA = 160
D_SHIFT_B = 3 * D_B + DECAY_LORA + AAA_LORA + GATE_LORA
LN_X_EPS = 64e-5

N_EXPERTS = 64
TOP_K = 8
N_EXPERT_GROUPS = 8
TOPK_GROUPS = 4
D_EXPERT = 256
ROUTED_SCALE = 2.5
EXPERT_BLOCK = 512
NORM_EPS = 1e-6

LANES = 128
WKV_CHUNK = 64
MOE_TILE = 256
VMEM_LIMIT = 56 * 1024 * 1024
ROW_TILE_SUBLANES = D_MODEL // LANES
ZERO_ROWS = 256


def _cparams(sem):
    return pltpu.CompilerParams(dimension_semantics=sem, vmem_limit_bytes=VMEM_LIMIT)


def _dot(a, b):
    return jnp.dot(a, b, preferred_element_type=F32)


def _dot_nt(a, b):
    return lax.dot_general(a, b, (((1,), (1,)), ((), ())), preferred_element_type=F32)


def _dot_tn(a, b):
    return lax.dot_general(a, b, (((0,), (0,)), ((), ())), preferred_element_type=F32)


def _dot_exact(a, b):
    return lax.dot_general(a, b, (((1,), (0,)), ((), ())), precision=lax.Precision.HIGHEST,
                           preferred_element_type=F32)


def _rms(x, gain):
    return x * lax.rsqrt(jnp.mean(x * x, axis=-1, keepdims=True) + NORM_EPS) * gain


def _sigmoid(x):
    return 1.0 / (1.0 + jnp.exp(-x))


def _silu(x):
    return x * _sigmoid(x)


def _softplus(x):
    return jnp.maximum(x, 0.0) + jnp.log(1.0 + jnp.exp(-jnp.abs(x)))


def _mod_body(c_ref, w_ref, b_ref, o_ref):
    s = _silu(c_ref[...]).astype(BF16)
    o_ref[...] = _dot(s, w_ref[...].astype(BF16)) + b_ref[...]


def _mod_call(c_all, w_ada, b_ada):
    rows = c_all.shape[0]
    tn = 1536
    return pl.pallas_call(
        _mod_body,
        out_shape=jax.ShapeDtypeStruct((rows, 6 * D_MODEL), F32),
        grid=(6 * D_MODEL // tn,),
        in_specs=[pl.BlockSpec((rows, D_MODEL), lambda j: (0, 0)),
                  pl.BlockSpec((D_MODEL, tn), lambda j: (0, j)),
                  pl.BlockSpec((1, tn), lambda j: (0, j))],
        out_specs=pl.BlockSpec((rows, tn), lambda j: (0, j)),
        compiler_params=_cparams(("arbitrary",)),
        name="mod",
    )(c_all, w_ada, b_ada.reshape(1, -1))


def _inproj_body(x_ref, g_ref, sc_ref, sh_ref, cos_ref, sin_ref, wq_ref, wf_ref, wg_ref,
                 q0_ref, q1_ref, q2_ref, feat_ref, gate_ref, tail_ref, p_ref, *, dils):
    x = x_ref[0]
    tm = x.shape[0]
    h = _rms(x, g_ref[...]) * (1.0 + sc_ref[0]) + sh_ref[0]
    hb = h.astype(BF16)
    p = _dot(hb, wq_ref[...])
    cos = cos_ref[...]
    sin = sin_ref[...]
    lane = lax.broadcasted_iota(I32, cos.shape, 1)
    first_half = (lane % HEAD_DIM_A) < (HEAD_DIM_A // 2)
    for c in range(2 * D_A // LANES):
        xc = p[:, c * LANES:(c + 1) * LANES]
        partner = jnp.where(first_half, pltpu.roll(xc, LANES - HEAD_DIM_A // 2, 1),
                            pltpu.roll(xc, HEAD_DIM_A // 2, 1))
        rc = xc * cos + partner * sin
        if c < D_A // LANES:
            rc = rc * (HEAD_DIM_A ** -0.5)
        p_ref[c] = rc
        if c >= D_A // LANES:
            tail_ref[0, :, (c - D_A // LANES) * LANES:(c - D_A // LANES + 1) * LANES] = rc
    for c in range(2 * D_A // LANES, D_QKV // LANES):
        p_ref[c] = p[:, c * LANES:(c + 1) * LANES]
    tail_ref[0, :, D_A:] = p[:, 2 * D_A:]
    per_group = D_GROUP_A // LANES
    for gi, (out_ref, dil) in enumerate(zip((q0_ref, q1_ref, q2_ref), dils)):
        for which in range(3):
            for j in range(per_group):
                c = (which * D_A + gi * D_GROUP_A) // LANES + j
                dst = slice(which * D_GROUP_A + j * LANES, which * D_GROUP_A + (j + 1) * LANES)
                if dil == 1:
                    out_ref[0, 0, :, dst] = p_ref[c].astype(BF16)
                else:
                    for r in range(dil):
                        out_ref[0, r, :, dst] = p_ref[c, pl.ds(r, tm // dil, stride=dil), :].astype(BF16)
    feat_ref[0] = _dot(hb, wf_ref[...])
    gate_ref[0] = _sigmoid(_dot(hb, wg_ref[...])).astype(BF16)


def _inproj_call(x, gain, scale, shift, cos_t, sin_t, wq, wf, wg, tm, tail_rows, mod_per_row, dils):
    nb, t, _ = x.shape
    nt = t // tm
    tail_first = (t - tail_rows) // tm
    if mod_per_row:
        mod_spec = pl.BlockSpec((1, tm, D_MODEL), lambda b, i: (b, i, 0))
    else:
        mod_spec = pl.BlockSpec((1, 1, D_MODEL), lambda b, i: (b, 0, 0))
    resident = lambda shp: pl.BlockSpec(shp, lambda b, i: (0, 0), pipeline_mode=pl.Buffered(1))
    q_shapes = tuple(jax.ShapeDtypeStruct((nb, d, t // d, 3 * D_GROUP_A), BF16) for d in dils)
    q_specs = tuple(pl.BlockSpec((1, d, tm // d, 3 * D_GROUP_A), lambda b, i: (b, 0, i, 0)) for d in dils)
    return pl.pallas_call(
        functools.partial(_inproj_body, dils=dils),
        out_shape=q_shapes + (jax.ShapeDtypeStruct((nb, t, D_SHIFT_B), F32),
                              jax.ShapeDtypeStruct((nb, t, 2 * D_MODEL), BF16),
                              jax.ShapeDtypeStruct((nb, tail_rows, 2 * D_A), F32)),
        grid=(nb, nt),
        in_specs=[pl.BlockSpec((1, tm, D_MODEL), lambda b, i: (b, i, 0)),
                  pl.BlockSpec((1, D_MODEL), lambda b, i: (0, 0)),
                  mod_spec, mod_spec,
                  pl.BlockSpec((tm, LANES), lambda b, i: (i, 0)),
                  pl.BlockSpec((tm, LANES), lambda b, i: (i, 0)),
                  resident((D_MODEL, D_QKV)), resident((D_MODEL, D_SHIFT_B)),
                  resident((D_MODEL, 2 * D_MODEL))],
        out_specs=q_specs + (pl.BlockSpec((1, tm, D_SHIFT_B), lambda b, i: (b, i, 0)),
                             pl.BlockSpec((1, tm, 2 * D_MODEL), lambda b, i: (b, i, 0)),
                             pl.BlockSpec((1, tm, 2 * D_A), lambda b, i: (b, jnp.maximum(i - tail_first, 0), 0))),
        scratch_shapes=[pltpu.VMEM((D_QKV // LANES, tm, LANES), F32)],
        compiler_params=_cparams(("arbitrary", "arbitrary")),
        name="inproj",
    )(x, gain.reshape(1, -1), scale, shift, cos_t, sin_t, wq, wf, wg)


def _attn_body(q_ref, kc_ref, kp_ref, vc_ref, vp_ref, o_ref, lse_ref):
    mb = pl.program_id(2)
    q = q_ref[0, 0]
    k = jnp.concatenate([kp_ref[0, 0], kc_ref[0, 0]], axis=0)
    v = jnp.concatenate([vp_ref[0, 0], vc_ref[0, 0]], axis=0)
    qi = lax.broadcasted_iota(I32, (BAND_BLOCK, 2 * BAND_BLOCK), 0)
    ki = lax.broadcasted_iota(I32, (BAND_BLOCK, 2 * BAND_BLOCK), 1)
    dist = qi + BAND_BLOCK - ki
    mask = (dist >= 0) & (dist <= BAND_BLOCK) & ((ki >= BAND_BLOCK) | (mb > 0))
    lane_q = lax.broadcasted_iota(I32, (BAND_BLOCK, LANES), 1)
    lane_k = lax.broadcasted_iota(I32, (2 * BAND_BLOCK, LANES), 1)
    for hp in range(N_HEADS_A // 2):
        sl = slice(hp * LANES, (hp + 1) * LANES)
        qp, kp, vp = q[:, sl], k[:, sl], v[:, sl]
        o_pair = jnp.zeros((BAND_BLOCK, LANES), F32)
        lse_pair = jnp.zeros((BAND_BLOCK, LANES), F32)
        for sub in range(2):
            mq = (lane_q >= HEAD_DIM_A) if sub else (lane_q < HEAD_DIM_A)
            mk = (lane_k >= HEAD_DIM_A) if sub else (lane_k < HEAD_DIM_A)
            s = _dot_nt(jnp.where(mq, qp, jnp.zeros_like(qp)), kp)
            s = jnp.where(mask, s, -jnp.inf)
            mx = jnp.max(s, axis=1, keepdims=True)
            p = jnp.exp(s - mx)
            l = jnp.sum(p, axis=1, keepdims=True)
            pv = _dot(p.astype(BF16), jnp.where(mk, vp, jnp.zeros_like(vp)))
            o_pair = o_pair + pv / l
            lse_pair = jnp.where(mq, mx + jnp.log(l), lse_pair)
        o_ref[0, 0, :, sl] = o_pair.astype(BF16)
        lse_ref[0, 0, :, sl] = lse_pair


def _attn_call(qkv_g, gi):
    b, dil, l, _ = qkv_g.shape
    nb = l // BAND_BLOCK
    blk = (1, 1, BAND_BLOCK, D_GROUP_A)
    cur = lambda which: pl.BlockSpec(blk, lambda bb, r, m: (bb, r, m, which))
    prev = lambda which: pl.BlockSpec(blk, lambda bb, r, m: (bb, r, jnp.maximum(m - 1, 0), which))
    return pl.pallas_call(
        _attn_body,
        out_shape=(jax.ShapeDtypeStruct((b, dil, l, D_GROUP_A), BF16),
                   jax.ShapeDtypeStruct((b, dil, l, D_GROUP_A), F32)),
        grid=(b, dil, nb),
        in_specs=[cur(0), cur(1), prev(1), cur(2), prev(2)],
        out_specs=(pl.BlockSpec(blk, lambda bb, r, m: (bb, r, m, 0)),
                   pl.BlockSpec(blk, lambda bb, r, m: (bb, r, m, 0))),
        compiler_params=_cparams(("arbitrary", "arbitrary", "arbitrary")),
        name=f"attn{gi}",
    )(qkv_g, qkv_g, qkv_g, qkv_g, qkv_g)


def _sattn_body(qkv_ref, b1_ref, b2_ref, b3_ref, o_ref):
    outs, lses = [], []
    for g, (buf_ref, (_, dil)) in enumerate(zip((b1_ref, b2_ref, b3_ref), DILATED_GROUPS)):
        q = qkv_ref[0, g]
        kn = qkv_ref[0, N_GROUPS_A + g]
        vn = qkv_ref[0, 2 * N_GROUPS_A + g]
        kb = buf_ref[0, 0]
        vb = buf_ref[0, 1]
        wb = kb.shape[-1]
        pos = lax.broadcasted_iota(I32, (1, 1, wb), 2)
        s = jnp.sum(kb * q, axis=1, keepdims=True)
        s = jnp.where(pos % dil == 0, s, -jnp.inf)
        sn = jnp.sum(kn * q, axis=1, keepdims=True)
        m = jnp.maximum(jnp.max(s, axis=2, keepdims=True), sn)
        p = jnp.exp(s - m)
        pn = jnp.exp(sn - m)
        l = jnp.sum(p, axis=2, keepdims=True) + pn
        outs.append((jnp.sum(p * vb, axis=2, keepdims=True) + pn * vn) / l)
        lses.append(m + jnp.log(l))
    mx = jnp.maximum(jnp.maximum(lses[0], lses[1]), lses[2])
    es = [jnp.exp(z - mx) for z in lses]
    o_ref[0] = (es[0] * outs[0] + es[1] * outs[1] + es[2] * outs[2]) / (es[0] + es[1] + es[2])


def _sattn_call(qkv_s, c1, c2, c3):
    n = qkv_s.shape[0]
    views, specs = [], []
    for c in (c1, c2, c3):
        wb = c.shape[1]
        views.append(jnp.transpose(c, (0, 2, 3, 4, 1)))
        specs.append(pl.BlockSpec((1, 2, N_HEADS_A, HEAD_DIM_A, wb), lambda b: (b, 0, 0, 0, 0)))
    return pl.pallas_call(
        _sattn_body,
        out_shape=jax.ShapeDtypeStruct((n, N_HEADS_A, HEAD_DIM_A, 1), F32),
        grid=(n,),
        in_specs=[pl.BlockSpec((1, 3 * N_GROUPS_A, N_HEADS_A, HEAD_DIM_A, 1), lambda b: (b, 0, 0, 0, 0))] + specs,
        out_specs=pl.BlockSpec((1, N_HEADS_A, HEAD_DIM_A, 1), lambda b: (b, 0, 0, 0)),
        compiler_params=_cparams(("arbitrary",)),
        name="sattn",
    )(qkv_s, *views)


def _rwkv_features(xs, w0, ww2, a0, wa2, wg2, k_a):
    r = xs[:, :D_B]
    k = xs[:, D_B:2 * D_B]
    v = xs[:, 2 * D_B:3 * D_B]
    xw = xs[:, 3 * D_B:3 * D_B + DECAY_LORA]
    xa = xs[:, 3 * D_B + DECAY_LORA:3 * D_B + DECAY_LORA + AAA_LORA]
    xg = xs[:, 3 * D_B + DECAY_LORA + AAA_LORA:]
    w_log = -_softplus(-(w0 + _dot(jnp.tanh(xw).astype(BF16), ww2.astype(BF16)))) - 0.5
    a = _sigmoid(a0 + _dot(xa.astype(BF16), wa2.astype(BF16)))
    g = _dot(_sigmoid(xg).astype(BF16), wg2.astype(BF16))
    k_h = k * (1.0 + (a - 1.0) * k_a)
    return r, k, v, w_log, a, g, k_h


def _head_norm(kk_h):
    nrm = jnp.sqrt(jnp.sum(kk_h * kk_h, axis=-1, keepdims=True))
    return kk_h / jnp.maximum(nrm, 1e-12)


def _wkv_finish_head(y, r_h, k_h, v_h, g_h, rk_h, lnw_h, lnb_h):
    mean = jnp.mean(y, axis=-1, keepdims=True)
    var = jnp.mean(jnp.square(y - mean), axis=-1, keepdims=True)
    yn = (y - mean) * lax.rsqrt(var + LN_X_EPS) * lnw_h + lnb_h
    bonus = jnp.sum(r_h * k_h * rk_h, axis=-1, keepdims=True) * v_h
    return (yn + bonus) * g_h


def _wkv_body(f_ref, fp_ref, mu_ref, w0_ref, ww2_ref, a0_ref, wa2_ref, wg2_ref, kk_ref, ka_ref,
              rk_ref, lnw_ref, lnb_ref, o_ref, st_ref, s_ref):
    c = pl.program_id(0)
    C = WKV_CHUNK
    nb = f_ref.shape[0]

    @pl.when(c == 0)
    def _():
        s_ref[...] = jnp.zeros_like(s_ref)

    f = jnp.concatenate([f_ref[b] for b in range(nb)], axis=0)
    row = lax.broadcasted_iota(I32, f.shape, 0)
    prev = pltpu.roll(f, 1, 0)
    for b in range(nb):
        prev = jnp.where(row == b * C, jnp.where(c == 0, 0.0, fp_ref[b][7:8, :]), prev)
    xs = f + mu_ref[...] * (prev - f)
    r, k, v, w_log, a, g, k_h = _rwkv_features(xs, w0_ref[...], ww2_ref[...], a0_ref[...],
                                               wa2_ref[...], wg2_ref[...], ka_ref[...])
    lw = -jnp.exp(w_log)
    kk = k * kk_ref[...]
    jh = lax.broadcasted_iota(I32, (D_B, LANES), 0) // HEAD_DIM_B
    ind = (jh == lax.broadcasted_iota(I32, (D_B, LANES), 1)).astype(BF16)
    ind_t = (lax.broadcasted_iota(I32, (LANES, D_B), 0)
             == lax.broadcasted_iota(I32, (LANES, D_B), 1) // HEAD_DIM_B).astype(BF16)

    def head_sum(z):
        hi = z.astype(BF16)
        lo = (z - hi.astype(F32)).astype(BF16)
        s = _dot(hi, ind) + _dot(lo, ind)
        shi = s.astype(BF16)
        slo = (s - shi.astype(F32)).astype(BF16)
        return _dot(shi, ind_t) + _dot(slo, ind_t)

    kkn = kk / jnp.maximum(jnp.sqrt(head_sum(kk * kk)), 1e-12)

    tr = lax.broadcasted_iota(I32, (nb * C, nb * C), 0)
    sr_ = lax.broadcasted_iota(I32, (nb * C, nb * C), 1)
    tri_incl = ((tr >= sr_) & (tr // C == sr_ // C)).astype(BF16)
    l1 = lw.astype(BF16)
    r1 = lw - l1.astype(F32)
    l2 = r1.astype(BF16)
    l3 = (r1 - l2.astype(F32)).astype(BF16)
    cum = _dot(tri_incl, l1) + _dot(tri_incl, l2) + _dot(tri_incl, l3)
    rhos = [cum[b * C + C // 2 - 1:b * C + C // 2, :] for b in range(nb)]
    rho = jnp.concatenate([jnp.broadcast_to(z, (C, D_B)) for z in rhos], axis=0)
    ep = jnp.exp(cum - rho)
    em = jnp.exp(rho - cum)
    e_a = ep * jnp.exp(-lw)
    r_hat = r * ep
    k_hat = k_h * em
    e_rs = [jnp.exp(z) for z in rhos]
    e_cs = [jnp.exp(cum[b * C + C - 1:b * C + C, :] - rhos[b]) for b in range(nb)]

    ti = lax.broadcasted_iota(I32, (C, C), 0)
    si = lax.broadcasted_iota(I32, (C, C), 1)
    strict = ti > si
    incl = ti >= si
    eye = (ti == si).astype(F32)
    rk = rk_ref[...]
    lnw = lnw_ref[...]
    lnb = lnb_ref[...]
    items = [(b, h) for b in range(nb) for h in range(N_HEADS_B)]
    heads = range(len(items))
    lanes = [slice(h * HEAD_DIM_B, (h + 1) * HEAD_DIM_B) for _, h in items]
    cut = lambda z, i: z[items[i][0] * C:(items[i][0] + 1) * C, lanes[i]]
    e_r = [e_rs[b][:, lanes[i]] for i, (b, _) in enumerate(items)]
    e_c = [e_cs[b][:, lanes[i]] for i, (b, _) in enumerate(items)]
    a_hat_full = (-kkn * e_a).astype(BF16)
    b_hat_full = (kkn * a * em).astype(BF16)
    a_hat_b = [cut(a_hat_full, h) for h in heads]
    b_hat_b = [cut(b_hat_full, h) for h in heads]
    rh = [cut(r_hat, h) for h in heads]
    vb = [cut(v, h).astype(BF16) for h in heads]
    bk = [jnp.concatenate([b_hat_b[h], cut(k_hat, h).astype(BF16)], axis=0) for h in heads]
    p = [_dot_nt(jnp.concatenate([a_hat_b[h], rh[h].astype(BF16)], axis=0), bk[h]) for h in heads]
    l_ab = [jnp.where(strict, z[:C, :C], 0.0) for z in p]
    l_ak = [jnp.where(strict, z[:C, C:], 0.0).astype(BF16) for z in p]
    p_rb = [jnp.where(incl, z[C:, :C], 0.0).astype(BF16) for z in p]
    p_rk = [jnp.where(incl, z[C:, C:], 0.0).astype(BF16) for z in p]
    xb = [z.astype(BF16) for z in l_ab]
    tinv = [eye + z for z in l_ab]
    for _ in range(int(math.log2(C)) - 1):
        xb = [_dot(z, z).astype(BF16) for z in xb]
        tinv = [tinv[h] + _dot(tinv[h].astype(BF16), xb[h]) for h in heads]
    tb = [z.astype(BF16) for z in tinv]
    lv = [_dot(l_ak[h], vb[h]).astype(BF16) for h in heads]
    a_bar = [_dot(tb[h], a_hat_b[h]).astype(BF16) for h in heads]
    u_v = [_dot(tb[h], lv[h]).astype(BF16) for h in heads]
    r_bar = [rh[h] + _dot(p_rb[h], a_bar[h]) for h in heads]
    y_v = [_dot(p_rb[h], u_v[h]) + _dot(p_rk[h], vb[h]) for h in heads]
    ab = [_dot_tn(a_bar[h], b_hat_b[h]).astype(BF16) for h in heads]
    n_t = [_dot_tn(jnp.concatenate([u_v[h], vb[h]], axis=0), bk[h]) for h in heads]
    s0 = [s_ref[b, h] for b, h in items]
    sr = [s0[h] * e_r[h] for h in heads]
    y = [_dot_nt((r_bar[h] * e_r[h]).astype(BF16), s0[h].astype(BF16)) + y_v[h] for h in heads]
    s_new = [(sr[h] + _dot(sr[h].astype(BF16), ab[h]) + n_t[h]) * e_c[h] for h in heads]
    for i, (b, h) in enumerate(items):
        s_ref[b, h] = s_new[i]
    y_full = jnp.concatenate([jnp.concatenate(y[b * N_HEADS_B:(b + 1) * N_HEADS_B], axis=1) for b in range(nb)],
                             axis=0)
    inv_hd = 1.0 / HEAD_DIM_B
    dev = y_full - head_sum(y_full) * inv_hd
    yn = dev * lax.rsqrt(head_sum(dev * dev) * inv_hd + LN_X_EPS) * lnw + lnb
    out = (yn + head_sum(r * k_h * rk) * v) * g
    for b in range(nb):
        o_ref[b] = out[b * C:(b + 1) * C, :]

    @pl.when(c == pl.num_programs(0) - 1)
    def _():
        st_ref[...] = s_ref[...]


def _wkv_call(feat, p):
    b, t, _ = feat.shape
    C = WKV_CHUNK
    nc = t // C
    row = lambda n: pl.BlockSpec((1, n), lambda c: (0, 0))
    mat = lambda m, n: pl.BlockSpec((m, n), lambda c: (0, 0))
    return pl.pallas_call(
        _wkv_body,
        out_shape=(jax.ShapeDtypeStruct((b, t, D_B), F32),
                   jax.ShapeDtypeStruct((b, N_HEADS_B, HEAD_DIM_B, HEAD_DIM_B), F32)),
        grid=(nc,),
        in_specs=[pl.BlockSpec((b, C, D_SHIFT_B), lambda c: (0, c, 0)),
                  pl.BlockSpec((b, 8, D_SHIFT_B), lambda c: (0, jnp.maximum(c * (C // 8) - 1, 0), 0)),
                  row(D_SHIFT_B), row(D_B), mat(DECAY_LORA, D_B), row(D_B), mat(AAA_LORA, D_B),
                  mat(GATE_LORA, D_B), row(D_B), row(D_B), row(D_B), row(D_B), row(D_B)],
        out_specs=(pl.BlockSpec((b, C, D_B), lambda c: (0, c, 0)),
                   pl.BlockSpec((b, N_HEADS_B, HEAD_DIM_B, HEAD_DIM_B), lambda c: (0, 0, 0, 0))),
        scratch_shapes=[pltpu.VMEM((b, N_HEADS_B, HEAD_DIM_B, HEAD_DIM_B), F32)],
        compiler_params=_cparams(("arbitrary",)),
        name="wkv",
    )(feat, feat, p['mu_b'], p['w0_b'], p['w_w2_b'], p['a0_b'], p['w_a2_b'], p['w_g2_b'],
      p['k_k_b'], p['k_a_b'], p['r_k_b'], p['ln_x_w_b'], p['ln_x_b_b'])


def _swkv_prep_body(f_ref, sh_ref, mu_ref, w0_ref, ww2_ref, a0_ref, wa2_ref, wg2_ref, kk_ref, ka_ref,
                    r_ref, w_ref, k_ref, v_ref, aa_ref, bb_ref, g_ref):
    f = f_ref[...]
    xs = f + mu_ref[...] * (sh_ref[...] - f)
    r, k, v, w_log, a, g, k_h = _rwkv_features(xs, w0_ref[...], ww2_ref[...], a0_ref[...],
                                               wa2_ref[...], wg2_ref[...], ka_ref[...])
    kk = k * kk_ref[...]
    kkn = jnp.concatenate([_head_norm(kk[:, h * HEAD_DIM_B:(h + 1) * HEAD_DIM_B]) for h in range(N_HEADS_B)],
                          axis=1)
    r_ref[...] = r
    w_ref[...] = jnp.exp(-jnp.exp(w_log))
    k_ref[...] = k_h
    v_ref[...] = v
    aa_ref[...] = -kkn
    bb_ref[...] = kkn * a
    g_ref[...] = g


def _swkv_prep_call(feat_s, shift0, p):
    n = feat_s.shape[0]
    full = lambda a: pl.BlockSpec(a.shape, lambda: tuple(0 for _ in a.shape))
    args = (feat_s, shift0, p['mu_b'], p['w0_b'], p['w_w2_b'], p['a0_b'], p['w_a2_b'], p['w_g2_b'],
            p['k_k_b'], p['k_a_b'])
    return pl.pallas_call(
        _swkv_prep_body,
        out_shape=tuple(jax.ShapeDtypeStruct((n, D_B), F32) for _ in range(7)),
        in_specs=[full(a) for a in args],
        out_specs=tuple(pl.BlockSpec((n, D_B), lambda: (0, 0)) for _ in range(7)),
        compiler_params=pltpu.CompilerParams(vmem_limit_bytes=VMEM_LIMIT),
        name="swkv_prep",
    )(*args)


def _swkv_step_body(s_ref, a_ref, w_ref, b_ref, k_ref, r_ref, v_ref, so_ref, y_ref):
    s = s_ref[...]
    sa = jnp.sum(s * a_ref[...], axis=-1, keepdims=True)
    s2 = s * w_ref[...] + sa * b_ref[...] + v_ref[...] * k_ref[...]
    so_ref[...] = s2
    y_ref[...] = jnp.sum(s2 * r_ref[...], axis=-1, keepdims=True)


def _swkv_step_call(s0, aa, w, bb, k, r, v_col):
    nh = s0.shape[0]
    th = 64
    rowspec = pl.BlockSpec((th, 1, HEAD_DIM_B), lambda i: (i, 0, 0))
    colspec = pl.BlockSpec((th, HEAD_DIM_B, 1), lambda i: (i, 0, 0))
    stspec = pl.BlockSpec((th, HEAD_DIM_B, HEAD_DIM_B), lambda i: (i, 0, 0))
    return pl.pallas_call(
        _swkv_step_body,
        out_shape=(jax.ShapeDtypeStruct((nh, HEAD_DIM_B, HEAD_DIM_B), F32),
                   jax.ShapeDtypeStruct((nh, HEAD_DIM_B, 1), F32)),
        grid=(nh // th,),
        in_specs=[stspec, rowspec, rowspec, rowspec, rowspec, rowspec, colspec],
        out_specs=(stspec, colspec),
        compiler_params=_cparams(("arbitrary",)),
        name="swkv_step",
    )(s0, aa, w, bb, k, r, v_col)


def _swkv_fin_body(y_ref, r_ref, k_ref, v_ref, g_ref, rk_ref, lnw_ref, lnb_ref, o_ref):
    y, r, k, v, g = y_ref[...], r_ref[...], k_ref[...], v_ref[...], g_ref[...]
    rk, lnw, lnb = rk_ref[...], lnw_ref[...], lnb_ref[...]
    outs = []
    for h in range(N_HEADS_B):
        sl = slice(h * HEAD_DIM_B, (h + 1) * HEAD_DIM_B)
        outs.append(_wkv_finish_head(y[:, sl], r[:, sl], k[:, sl], v[:, sl], g[:, sl],
                                     rk[:, sl], lnw[:, sl], lnb[:, sl]))
    o_ref[...] = jnp.concatenate(outs, axis=1)


def _swkv_fin_call(y, r, k, v, g, p):
    n = y.shape[0]
    args = (y, r, k, v, g, p['r_k_b'], p['ln_x_w_b'], p['ln_x_b_b'])
    full = lambda a: pl.BlockSpec(a.shape, lambda: (0, 0))
    return pl.pallas_call(
        _swkv_fin_body,
        out_shape=jax.ShapeDtypeStruct((n, D_B), F32),
        in_specs=[full(a) for a in args],
        out_specs=pl.BlockSpec((n, D_B), lambda: (0, 0)),
        name="swkv_fin",
    )(*args)


def _route_t(scores, bias_col):
    n = scores.shape[1]
    gsz = N_EXPERTS // N_EXPERT_GROUPS
    choice = scores + bias_col
    ninf = -jnp.inf
    sid = lax.broadcasted_iota(I32, (gsz, n), 0)
    gs = []
    for gidx in range(N_EXPERT_GROUPS):
        blk = choice[gidx * gsz:(gidx + 1) * gsz, :]
        m1 = jnp.max(blk, axis=0, keepdims=True)
        first = jnp.min(jnp.where(blk == m1, sid, gsz), axis=0, keepdims=True)
        m2 = jnp.max(jnp.where(sid == first, ninf, blk), axis=0, keepdims=True)
        gs.append(m1 + m2)
    cur = jnp.concatenate(gs, axis=0)
    gid = lax.broadcasted_iota(I32, (N_EXPERT_GROUPS, n), 0)
    gmask = jnp.zeros((N_EXPERT_GROUPS, n), F32)
    for _ in range(TOPK_GROUPS):
        m = jnp.max(cur, axis=0, keepdims=True)
        first = jnp.min(jnp.where(cur == m, gid, N_EXPERT_GROUPS), axis=0, keepdims=True)
        sel = gid == first
        gmask = jnp.where(sel, 1.0, gmask)
        cur = jnp.where(sel, ninf, cur)
    emask = jnp.concatenate([jnp.broadcast_to(gmask[gidx:gidx + 1, :], (gsz, n))
                             for gidx in range(N_EXPERT_GROUPS)], axis=0)
    cur = jnp.where(emask > 0.5, choice, ninf)
    eid = lax.broadcasted_iota(I32, (N_EXPERTS, n), 0)
    selm = jnp.zeros((N_EXPERTS, n), F32)
    for _ in range(TOP_K):
        m = jnp.max(cur, axis=0, keepdims=True)
        first = jnp.min(jnp.where(cur == m, eid, N_EXPERTS), axis=0, keepdims=True)
        sel = eid == first
        selm = jnp.where(sel, 1.0, selm)
        cur = jnp.where(sel, ninf, cur)
    w = jnp.where(selm > 0.5, scores, 0.0)
    w = w / jnp.sum(w, axis=0, keepdims=True) * ROUTED_SCALE
    return jnp.where(selm > 0.5, w, -1.0)


def _unpermute(blk_ref, scr_ref, dil, tm):
    if dil == 1:
        return blk_ref[0, 0].astype(F32)
    n_chunks = scr_ref.shape[0]
    for r in range(dil):
        rows = blk_ref[0, r].astype(F32)
        for j in range(n_chunks):
            scr_ref[j, pl.ds(r, tm // dil, stride=dil), :] = rows[:, j * LANES:(j + 1) * LANES]
    return jnp.concatenate([scr_ref[j] for j in range(n_chunks)], axis=1)


def _post_body(*refs, combine, dils):
    if combine:
        o_refs, l_refs, rest = refs[:3], refs[3:6], refs[6:]
    else:
        o_refs, rest = refs[:1], refs[1:]
    (ob_ref, gt_ref, x_ref, g1_ref, sc2_ref, sh2_ref, npost_ref, npre_ref, wa_ref, wb_ref, wo_ref,
     wrt_ref, rb_ref, x1_ref, hp_ref, wt_ref) = rest[:16]
    scr = rest[16:]
    tm = x_ref.shape[1]
    if combine:
        os_, ls_ = [], []
        si = 0
        for gi, dil in enumerate(dils):
            os_.append(_unpermute(o_refs[gi], scr[si] if dil > 1 else None, dil, tm))
            ls_.append(_unpermute(l_refs[gi], scr[si + 1] if dil > 1 else None, dil, tm))
            si += 2 if dil > 1 else 0
        mx = jnp.maximum(jnp.maximum(ls_[0], ls_[1]), ls_[2])
        es = [jnp.exp(z - mx) for z in ls_]
        o_a = (es[0] * os_[0] + es[1] * os_[1] + es[2] * os_[2]) / (es[0] + es[1] + es[2])
    else:
        o_a = o_refs[0][0]
    gt = gt_ref[0].astype(F32)
    za = _dot(o_a.astype(BF16), wa_ref[...])
    zb = _dot(ob_ref[0].astype(BF16), wb_ref[...])
    merged = gt[:, :D_MODEL] * za + gt[:, D_MODEL:] * zb
    z = _dot(merged.astype(BF16), wo_ref[...])
    x1 = x_ref[0] + g1_ref[0] * _rms(z, npost_ref[...])
    x1_ref[0] = x1
    h2 = _rms(x1, npre_ref[...]) * (1.0 + sc2_ref[0]) + sh2_ref[0]
    for s in range(ROW_TILE_SUBLANES):
        hp_ref[0, pl.ds(s, tm, stride=ROW_TILE_SUBLANES), :] = h2[:, s * LANES:(s + 1) * LANES]
    tp =-(-tm // LANES) * LANES
    if tp != tm:
        h2 = jnp.concatenate([h2, jnp.zeros((tp - tm, D_MODEL), F32)], axis=0)
    logits_t = lax.dot_general(wrt_ref[...], h2, (((1,), (1,)), ((), ())),
                               precision=lax.Precision.HIGHEST, preferred_element_type=F32)
    w = _route_t(_sigmoid(logits_t[:N_EXPERTS, :]), rb_ref[...])
    wt_ref[...] = w[:, :tm]


def _post_call(o_parts, lse_parts, ob, gates, x, gate1, scale2, shift2, p, wa, wb, wo, wrt, rb, tm, mod_per_row):
    nb, t, _ = x.shape
    nt = t // tm
    combine = lse_parts is not None
    rowblk = lambda width: pl.BlockSpec((1, tm, width), lambda b, i: (b, i, 0))
    if mod_per_row:
        mod_spec = rowblk(D_MODEL)
    else:
        mod_spec = pl.BlockSpec((1, 1, D_MODEL), lambda b, i: (b, 0, 0))
    const = lambda shp: pl.BlockSpec(shp, lambda b, i: (0, 0))
    scratch = []
    if combine:
        dils = tuple(o.shape[1] for o in o_parts)
        o_args = list(o_parts) + list(lse_parts)
        o_specs = [pl.BlockSpec((1, d, tm // d, D_GROUP_A), lambda b, i: (b, 0, i, 0)) for d in dils] * 2
        for d in dils:
            if d > 1:
                scratch += [pltpu.VMEM((D_GROUP_A // LANES, tm, LANES), F32)] * 2
    else:
        dils = ()
        o_args = [o_parts[0]]
        o_specs = [rowblk(D_GROUP_A)]
    return pl.pallas_call(
        functools.partial(_post_body, combine=combine, dils=dils),
        out_shape=(jax.ShapeDtypeStruct((nb, t, D_MODEL), F32),
                   jax.ShapeDtypeStruct((nb, t * ROW_TILE_SUBLANES, LANES), F32),
                   jax.ShapeDtypeStruct((N_EXPERTS, nb * t), F32)),
        grid=(nb, nt),
        in_specs=o_specs + [rowblk(D_B), rowblk(2 * D_MODEL), rowblk(D_MODEL),
                            mod_spec, mod_spec, mod_spec, const((1, D_MODEL)), const((1, D_MODEL)),
                            const((D_GROUP_A, D_MODEL)), const((D_B, D_MODEL)), const((D_MODEL, D_MODEL)),
                            const((LANES, D_MODEL)), const((N_EXPERTS, 1))],
        out_specs=(rowblk(D_MODEL),
                   pl.BlockSpec((1, tm * ROW_TILE_SUBLANES, LANES), lambda b, i: (b, i, 0)),
                   pl.BlockSpec((N_EXPERTS, tm), lambda b, i: (0, b * nt + i))),
        scratch_shapes=scratch,
        compiler_params=_cparams(("arbitrary", "arbitrary")),
        name="post",
    )(*o_args, ob, gates, x, gate1, scale2, shift2, p['norm_post_mix'].reshape(1, -1),
      p['norm_pre_ffn'].reshape(1, -1), wa, wb, wo, wrt, rb)


def _rank_body(w_ref, dest_ref, w8_ref, tab_ref, etab_ref, cnt_ref, pst_ref, run_ref, *, n_real, n_slots):
    ph = pl.program_id(0)
    i = pl.program_id(1)
    T = MOE_TILE
    w = w_ref[...]
    sel = (w >= 0.0).astype(F32)
    cnt_tile = jnp.broadcast_to(jnp.sum(sel, axis=1, keepdims=True), (N_EXPERTS, LANES))
    ei = lax.broadcasted_iota(I32, (N_EXPERTS, N_EXPERTS), 0)
    ej = lax.broadcasted_iota(I32, (N_EXPERTS, N_EXPERTS), 1)

    @pl.when((ph == 0) & (i == 0))
    def _():
        cnt_ref[...] = jnp.zeros_like(cnt_ref)

    @pl.when(ph == 0)
    def _():
        cnt_ref[...] += cnt_tile

    @pl.when((ph == 1) & (i == 0))
    def _():
        cnt = cnt_ref[...]
        padded = jnp.floor((cnt + (EXPERT_BLOCK - 1)) / EXPERT_BLOCK) * EXPERT_BLOCK
        pstart = _dot_exact((ej < ei).astype(F32), padded)
        pst_ref[...] = pstart
        run_ref[...] = jnp.zeros_like(run_ref)
        pend = pstart + padded
        vend = pstart + cnt
        esub = lax.broadcasted_iota(I32, (N_EXPERTS, LANES), 0)
        lane = lax.broadcasted_iota(I32, (1, LANES), 1)
        tab_ref[...] = jnp.zeros_like(tab_ref)
        for c in range(tab_ref.shape[1] // LANES):
            bs = ((c * LANES + lane) * EXPERT_BLOCK).astype(F32)
            be = jnp.minimum(jnp.sum((pend <= bs).astype(F32), axis=0, keepdims=True), N_EXPERTS - 1.0)
            tab_ref[0:1, c * LANES:(c + 1) * LANES] = be.astype(I32)
            tab_ref[1:2, c * LANES:(c + 1) * LANES] = (pend[N_EXPERTS - 1:, :] / EXPERT_BLOCK).astype(I32)
        on_diag = esub == lax.broadcasted_iota(I32, (N_EXPERTS, LANES), 1)
        etab_ref[...] = jnp.zeros_like(etab_ref)
        lo = jnp.sum(jnp.where(on_diag, vend, 0.0), axis=0, keepdims=True)
        hi = jnp.sum(jnp.where(on_diag, pend, 0.0), axis=0, keepdims=True)
        etab_ref[0:1, :] = jnp.where(lane == N_EXPERTS, pend[N_EXPERTS - 1:, :], lo).astype(I32)
        etab_ref[1:2, :] = jnp.where(lane == N_EXPERTS, float(n_slots), hi).astype(I32)

    @pl.when(ph == 1)
    def _():
        ti = lax.broadcasted_iota(I32, (T, T), 0)
        tj = lax.broadcasted_iota(I32, (T, T), 1)
        selb = sel.astype(BF16)
        rank = _dot(selb, (ti < tj).astype(BF16))
        ordn = _dot((ej < ei).astype(BF16), selb)
        dest_e = pst_ref[:, :1] + run_ref[:, :1] + rank
        run_ref[...] += cnt_tile
        tok = i * T + lax.broadcasted_iota(I32, (1, T), 1)
        dks, wks = [], []
        for k in range(TOP_K):
            m = (sel > 0.5) & (ordn == float(k))
            dk = jnp.sum(jnp.where(m, dest_e, 0.0), axis=0, keepdims=True)
            wk = jnp.sum(jnp.where(m, w, 0.0), axis=0, keepdims=True)
            dks.append(jnp.where(tok < n_real, dk, 0.0))
            wks.append(jnp.where(tok < n_real, wk, 0.0))
        dest_ref[...] = jnp.concatenate(dks, axis=0).astype(I32)
        w8_ref[...] = jnp.concatenate(wks, axis=0)


def _rank_call(w_t, n_real, n_blocks, n_blocks_pad):
    n = w_t.shape[1]
    nt = n // MOE_TILE
    return pl.pallas_call(
        functools.partial(_rank_body, n_real=n_real, n_slots=n_blocks * EXPERT_BLOCK),
        out_shape=(jax.ShapeDtypeStruct((TOP_K, n), I32),
                   jax.ShapeDtypeStruct((TOP_K, n), F32),
                   jax.ShapeDtypeStruct((8, n_blocks_pad), I32),
                   jax.ShapeDtypeStruct((8, LANES), I32)),
        grid=(2, nt),
        in_specs=[pl.BlockSpec((N_EXPERTS, MOE_TILE), lambda ph, i: (0, i))],
        out_specs=(pl.BlockSpec((TOP_K, MOE_TILE), lambda ph, i: (0, i * ph)),
                   pl.BlockSpec((TOP_K, MOE_TILE), lambda ph, i: (0, i * ph)),
                   pl.BlockSpec((8, n_blocks_pad), lambda ph, i: (0, 0)),
                   pl.BlockSpec((8, LANES), lambda ph, i: (0, 0))),
        scratch_shapes=[pltpu.VMEM((N_EXPERTS, LANES), F32)] * 3,
        compiler_params=_cparams(("arbitrary", "arbitrary")),
        name="rank",
    )(w_t)


def _tile_rows(ref, row, n):
    return ref.at[pl.ds(pl.multiple_of(row * ROW_TILE_SUBLANES, ROW_TILE_SUBLANES), n * ROW_TILE_SUBLANES)]


def _zero_fill(etab_ref, zbuf, xs_hbm, zsem, wait):
    def go(src, dst):
        cp = pltpu.make_async_copy(src, dst, zsem)
        if wait:
            cp.wait()
        else:
            cp.start()

    def per_range(e, carry):
        lo = etab_ref[0, e]
        n = etab_ref[1, e] - lo
        n_full = n // ZERO_ROWS

        def full(j, c):
            go(zbuf, _tile_rows(xs_hbm, lo + j * ZERO_ROWS, ZERO_ROWS))
            return c

        lax.fori_loop(0, n_full, full, 0)
        pos = lo + n_full * ZERO_ROWS
        rem = n - n_full * ZERO_ROWS
        size = ZERO_ROWS // 2
        while size >= 1:
            bit = rem & size

            @pl.when(bit != 0)
            def _(size=size, pos=pos):
                go(_tile_rows(zbuf, 0, size), _tile_rows(xs_hbm, pos, size))

            pos = pos + bit
            size //= 2
        return carry

    lax.fori_loop(0, N_EXPERTS + 1, per_range, 0)


def _dispatch_body(dest_ref, etab_ref, xa_ref, xb_ref, xs_hbm, zbuf, sem, zsem, *, n_real, n_full):
    i = pl.program_id(0)
    T = MOE_TILE
    n_tok = jnp.clip(n_real - i * T, 0, T)

    def issue_from(x_ref):
        def issue(t, carry):
            for k in range(TOP_K):
                pltpu.make_async_copy(_tile_rows(x_ref, t, 1), _tile_rows(xs_hbm, dest_ref[k, t], 1),
                                      sem).start(priority=k % 2)
            return carry

        lax.fori_loop(0, n_tok, issue, 0)

    @pl.when(i < n_full)
    def _():
        issue_from(xa_ref)

    @pl.when(i >= n_full)
    def _():
        issue_from(xb_ref)

    @pl.when(i == 0)
    def _():
        zbuf[...] = jnp.zeros_like(zbuf)
        _zero_fill(etab_ref, zbuf, xs_hbm, zsem, wait=False)
        _zero_fill(etab_ref, zbuf, xs_hbm, zsem, wait=True)

    @pl.when(n_tok == T)
    def _():
        pltpu.make_async_copy(_tile_rows(xs_hbm, 0, T * TOP_K), _tile_rows(xs_hbm, 0, T * TOP_K), sem).wait()

    @pl.when(n_tok < T)
    def _():
        def drain(j, carry):
            pltpu.make_async_copy(_tile_rows(xs_hbm, 0, 1), _tile_rows(xs_hbm, 0, 1), sem).wait()
            return carry

        lax.fori_loop(0, n_tok * TOP_K, drain, 0)


def _dispatch_call(dest8, etab, hp_a, hp_b, n_real, n_slots):
    tile_rows = MOE_TILE * ROW_TILE_SUBLANES
    n_full = hp_a.shape[0] // tile_rows
    return pl.pallas_call(
        functools.partial(_dispatch_body, n_real=n_real, n_full=n_full),
        out_shape=jax.ShapeDtypeStruct((n_slots * ROW_TILE_SUBLANES, LANES), F32),
        grid=(n_full + 1,),
        in_specs=[pl.BlockSpec((TOP_K, MOE_TILE), lambda i: (0, i), memory_space=pltpu.SMEM),
                  pl.BlockSpec((8, LANES), lambda i: (0, 0), memory_space=pltpu.SMEM),
                  pl.BlockSpec((tile_rows, LANES), lambda i: (jnp.minimum(i, n_full - 1), 0)),
                  pl.BlockSpec((tile_rows, LANES), lambda i: (0, 0))],
        out_specs=pl.BlockSpec(memory_space=pl.ANY),
        scratch_shapes=[pltpu.VMEM((ZERO_ROWS * ROW_TILE_SUBLANES, LANES), F32),
                        pltpu.SemaphoreType.DMA, pltpu.SemaphoreType.DMA],
        compiler_params=_cparams(("arbitrary",)),
        name="dispatch",
    )(dest8, etab, hp_a, hp_b)


def _rows_from_tiles(ref, lo, n):
    return jnp.concatenate([ref[pl.ds(lo * ROW_TILE_SUBLANES + s, n, stride=ROW_TILE_SUBLANES), :]
                            for s in range(ROW_TILE_SUBLANES)], axis=1)


def _ffn_body(be_ref, xs_ref, wg_ref, wu_ref, wd_ref, ys_ref, wgb, wub, wdb):
    j = pl.program_id(0)

    @pl.when((j == 0) | (be_ref[j] != be_ref[jnp.maximum(j - 1, 0)]))
    def _():
        wgb[...] = wg_ref[0].astype(BF16)
        wub[...] = wu_ref[0].astype(BF16)
        wdb[...] = wd_ref[0].astype(BF16)

    x = _rows_from_tiles(xs_ref, 0, EXPERT_BLOCK).astype(BF16)
    act = _silu(_dot(x, wgb[...])) * _dot(x, wub[...])
    y = _dot(act.astype(BF16), wdb[...])
    for s in range(ROW_TILE_SUBLANES):
        ys_ref[pl.ds(s, EXPERT_BLOCK, stride=ROW_TILE_SUBLANES), :] = y[:, s * LANES:(s + 1) * LANES]


def _ffn_call(blk_e, xs, w_gate, w_up, w_down, n_blocks):
    tile_blk = pl.BlockSpec((EXPERT_BLOCK * ROW_TILE_SUBLANES, LANES), lambda j, be: (j, 0))
    grid_spec = pltpu.PrefetchScalarGridSpec(
        num_scalar_prefetch=1,
        grid=(n_blocks,),
        in_specs=[tile_blk,
                  pl.BlockSpec((1, D_MODEL, D_EXPERT), lambda j, be: (be[j], 0, 0)),
                  pl.BlockSpec((1, D_MODEL, D_EXPERT), lambda j, be: (be[j], 0, 0)),
                  pl.BlockSpec((1, D_EXPERT, D_MODEL), lambda j, be: (be[j], 0, 0))],
        out_specs=tile_blk,
        scratch_shapes=[pltpu.VMEM((D_MODEL, D_EXPERT), BF16), pltpu.VMEM((D_MODEL, D_EXPERT), BF16),
                        pltpu.VMEM((D_EXPERT, D_MODEL), BF16)])
    return pl.pallas_call(
        _ffn_body,
        out_shape=jax.ShapeDtypeStruct((n_blocks * EXPERT_BLOCK * ROW_TILE_SUBLANES, LANES), F32),
        grid_spec=grid_spec,
        compiler_params=_cparams(("arbitrary",)),
        name="ffn",
    )(blk_e, xs, w_gate, w_up, w_down)


def _combine_body(dest_ref, dnext_ref, w8_ref, xa_ref, xb_ref, sg_ref, su_ref, sd_ref, ys_hbm, y_ref, buf, sem,
                  *, n_full):
    i = pl.program_id(0)
    T = MOE_TILE

    def gather(d_ref, slot):
        def issue(t, carry):
            for k in range(TOP_K):
                pltpu.make_async_copy(_tile_rows(ys_hbm, d_ref[k, t], 1), _tile_rows(buf.at[slot], k * T + t, 1),
                                      sem.at[slot]).start(priority=k % 2)
            return carry

        lax.fori_loop(0, T, issue, 0)

    def step(slot):
        @pl.when(i + 1 < pl.num_programs(0))
        def _():
            gather(dnext_ref, 1 - slot)

        x = jnp.where(i < n_full, _rows_from_tiles(xa_ref, 0, T), _rows_from_tiles(xb_ref, 0, T)).astype(BF16)
        y = _dot((_silu(_dot(x, sg_ref[...])) * _dot(x, su_ref[...])).astype(BF16), sd_ref[...])
        w_t = jnp.concatenate([w8_ref[...], jnp.zeros((LANES - TOP_K, T), F32)], axis=0).T
        pltpu.make_async_copy(_tile_rows(ys_hbm, 0, T * TOP_K), buf.at[slot], sem.at[slot]).wait()
        for k in range(TOP_K):
            y = y + w_t[:, k:k + 1] * _rows_from_tiles(buf.at[slot], k * T, T)
        y_ref[...] = y

    @pl.when(i == 0)
    def _():
        gather(dest_ref, 0)

    @pl.when(i % 2 == 0)
    def _():
        step(0)

    @pl.when(i % 2 == 1)
    def _():
        step(1)


def _combine_call(dest8, w8, hp_a, hp_b, wsg, wsu, wsd, ys):
    T = MOE_TILE
    tile_rows = T * ROW_TILE_SUBLANES
    n_full = hp_a.shape[0] // tile_rows
    n_tiles = n_full + 1
    const = lambda shp: pl.BlockSpec(shp, lambda i: (0, 0))
    return pl.pallas_call(
        functools.partial(_combine_body, n_full=n_full),
        out_shape=jax.ShapeDtypeStruct((n_tiles * T, D_MODEL), F32),
        grid=(n_tiles,),
        in_specs=[pl.BlockSpec((TOP_K, T), lambda i: (0, i), memory_space=pltpu.SMEM),
                  pl.BlockSpec((TOP_K, T), lambda i: (0, jnp.minimum(i + 1, n_tiles - 1)), memory_space=pltpu.SMEM),
                  pl.BlockSpec((TOP_K, T), lambda i: (0, i)),
                  pl.BlockSpec((tile_rows, LANES), lambda i: (jnp.minimum(i, n_full - 1), 0)),
                  pl.BlockSpec((tile_rows, LANES), lambda i: (0, 0)),
                  const((D_MODEL, D_EXPERT)), const((D_MODEL, D_EXPERT)), const((D_EXPERT, D_MODEL)),
                  pl.BlockSpec(memory_space=pl.ANY)],
        out_specs=pl.BlockSpec((T, D_MODEL), lambda i: (i, 0)),
        scratch_shapes=[pltpu.VMEM((2, TOP_K * tile_rows, LANES), F32), pltpu.SemaphoreType.DMA((2,))],
        compiler_params=_cparams(("arbitrary",)),
        name="combine",
    )(dest8, dest8, w8, hp_a, hp_b, wsg, wsu, wsd, ys)


def _final_body(x1_ref, y_ref, g2_ref, n_ref, o_ref):
    o_ref[0] = x1_ref[0] + g2_ref[0] * _rms(y_ref[...], n_ref[...])


def _final_call(x1, y_all, row0, gate2, gain, tm, mod_per_row):
    nb, t, _ = x1.shape
    nt = t // tm
    blk0 = row0 // tm
    rowblk = pl.BlockSpec((1, tm, D_MODEL), lambda b, i: (b, i, 0))
    mod_spec = rowblk if mod_per_row else pl.BlockSpec((1, 1, D_MODEL), lambda b, i: (b, 0, 0))
    return pl.pallas_call(
        _final_body,
        out_shape=jax.ShapeDtypeStruct((nb, t, D_MODEL), F32),
        grid=(nb, nt),
        in_specs=[rowblk, pl.BlockSpec((tm, D_MODEL), lambda b, i: (blk0 + b * nt + i, 0)), mod_spec,
                  pl.BlockSpec((1, D_MODEL), lambda b, i: (0, 0))],
        out_specs=rowblk,
        compiler_params=_cparams(("arbitrary", "arbitrary")),
        name="final",
    )(x1, y_all, gate2, gain.reshape(1, -1))


def _rope_tables(pos):
    half = HEAD_DIM_A // 2
    inv_freq = ROPE_THETA ** (-jnp.arange(half, dtype=F32) / half)
    ang = pos.astype(F32)[:, None] * inv_freq[None, :]
    cos = jnp.cos(ang)
    sin = jnp.sin(ang)
    reps = LANES // HEAD_DIM_A
    cos_t = jnp.tile(jnp.concatenate([cos, cos], axis=1), (1, reps))
    sin_t = jnp.tile(jnp.concatenate([-sin, sin], axis=1), (1, reps))
    return cos_t, sin_t


def _cache_from_tail(tail, keep):
    outs = []
    n, rows, _ = tail.shape
    for gi, kp in enumerate(keep):
        k = tail[:, rows - kp:, gi * D_GROUP_A:(gi + 1) * D_GROUP_A]
        v = tail[:, rows - kp:, D_A + gi * D_GROUP_A:D_A + (gi + 1) * D_GROUP_A]
        outs.append(jnp.stack([k, v], axis=2).reshape(n, kp, 2, N_HEADS_A, HEAD_DIM_A))
    return outs


def kernel(x_prompt, x_sample, c_prompt, c_sample, cache_a1_kv, cache_a2_kv, cache_a3_kv, state_b_wkv, state_b_shift, w_ada, b_ada, norm_pre_mix, norm_post_mix, norm_pre_ffn, norm_post_ffn, w_in, w_a_out, mu_b, w0_b, w_w2_b, a0_b, w_a2_b, w_g2_b, k_k_b, k_a_b, r_k_b, ln_x_w_b, ln_x_b_b, w_b_out, w_out, w_router, router_bias, w_e_gate, w_e_up, w_e_down, w_s_gate, w_s_up, w_s_down):
    assert DEPTH == 1
    l = 0
    nd = DEC_BATCH
    row = lambda a: a.reshape(1, -1)
    p = {'mu_b': row(mu_b[l]), 'w0_b': row(w0_b[l]), 'w_w2_b': w_w2_b[l], 'a0_b': row(a0_b[l]),
         'w_a2_b': w_a2_b[l], 'w_g2_b': w_g2_b[l], 'k_k_b': row(k_k_b[l]), 'k_a_b': row(k_a_b[l]),
         'r_k_b': row(r_k_b[l]), 'ln_x_w_b': row(ln_x_w_b[l]), 'ln_x_b_b': row(ln_x_b_b[l]),
         'norm_post_mix': norm_post_mix[l], 'norm_pre_ffn': norm_pre_ffn[l]}

    wq = w_in[l][:, :D_QKV].astype(BF16)
    wf = w_in[l][:, D_QKV:D_QKV + D_SHIFT_B].astype(BF16)
    wg = w_in[l][:, D_QKV + D_SHIFT_B:].astype(BF16)
    wa = w_a_out[l].astype(BF16)
    wb = w_b_out[l].astype(BF16)
    wo = w_out[l].astype(BF16)
    wrt = jnp.concatenate([w_router[l].T, jnp.zeros((LANES - N_EXPERTS, D_MODEL), F32)], axis=0)
    rb = router_bias[l].reshape(N_EXPERTS, 1)
    wsg, wsu, wsd = w_s_gate[l].astype(BF16), w_s_up[l].astype(BF16), w_s_down[l].astype(BF16)

    n_c = BATCH + nd
    c_all = jnp.concatenate([c_prompt, c_sample, jnp.zeros((-n_c % 8, D_MODEL), F32)], axis=0)
    mod = _mod_call(c_all, w_ada[l], b_ada[l])
    mod_p = [m.reshape(BATCH, 1, D_MODEL) for m in jnp.split(mod[:BATCH], 6, axis=-1)]
    mod_s = [m.reshape(1, nd, D_MODEL) for m in jnp.split(mod[BATCH:n_c], 6, axis=-1)]

    cos_p, sin_p = _rope_tables(jnp.arange(SEQ, dtype=I32))
    cos_s, sin_s = _rope_tables(jnp.full((nd,), PAST_LEN, I32))

    keep_p = [min(w, SEQ) for w, _ in DILATED_GROUPS]
    tail_rows = max(keep_p)
    dils = tuple(d for _, d in DILATED_GROUPS)

    q0, q1, q2, feat_p, gates_p, tail_p = _inproj_call(
        x_prompt, norm_pre_mix[l], mod_p[1], mod_p[0], cos_p, sin_p, wq, wf, wg,
        tm=256, tail_rows=tail_rows, mod_per_row=False, dils=dils)
    o_parts, lse_parts = [], []
    for gi, qg in enumerate((q0, q1, q2)):
        o, lse = _attn_call(qg, gi)
        o_parts.append(o)
        lse_parts.append(lse)
    ob_p, wkv_p = _wkv_call(feat_p, p)
    x1_p, hp_p, wt_p = _post_call(o_parts, lse_parts, ob_p, gates_p, x_prompt, mod_p[2], mod_p[4], mod_p[3],
                                  p, wa, wb, wo, wrt, rb, tm=512, mod_per_row=False)

    xs3 = x_sample.reshape(1, nd, D_MODEL)
    s0, s1, s2, feat_s, gates_s, tail_s = _inproj_call(
        xs3, norm_pre_mix[l], mod_s[1], mod_s[0], cos_s, sin_s, wq, wf, wg,
        tm=nd, tail_rows=nd, mod_per_row=True, dils=(1, 1, 1))
    qkv_s = jnp.stack([z.reshape(nd, 3, N_HEADS_A, HEAD_DIM_A) for z in (s0, s1, s2)], axis=2)
    qkv_s = qkv_s.reshape(nd, 3 * N_GROUPS_A, N_HEADS_A, HEAD_DIM_A, 1).astype(F32)
    oa_s = _sattn_call(qkv_s, cache_a1_kv[l], cache_a2_kv[l], cache_a3_kv[l])
    r_s, w_s, k_s, v_s, aa_s, bb_s, g_s = _swkv_prep_call(feat_s[0], state_b_shift[l], p)
    nh = nd * N_HEADS_B
    as_row = lambda a: a.reshape(nh, 1, HEAD_DIM_B)
    s_new, y_col = _swkv_step_call(state_b_wkv[l].reshape(nh, HEAD_DIM_B, HEAD_DIM_B), as_row(aa_s), as_row(w_s),
                                   as_row(bb_s), as_row(k_s), as_row(r_s), v_s.reshape(nh, HEAD_DIM_B, 1))
    ob_s = _swkv_fin_call(y_col.reshape(nd, D_B), r_s, k_s, v_s, g_s, p)
    x1_s, hp_s, wt_s = _post_call([oa_s.reshape(1, nd, D_GROUP_A)], None, ob_s.reshape(1, nd, D_B), gates_s, xs3,
                                  mod_s[2], mod_s[4], mod_s[3], p, wa, wb, wo, wrt, rb, tm=nd, mod_per_row=True)

    n_p = BATCH * SEQ
    n_real = n_p + nd
    n_all = -(-n_real // MOE_TILE) * MOE_TILE
    pad = n_all - n_real
    n_blocks = -(-(n_real * TOP_K) // EXPERT_BLOCK) + N_EXPERTS
    n_blocks_pad = -(-n_blocks // LANES) * LANES
    assert n_p % MOE_TILE == 0 and nd <= MOE_TILE
    hp_a = hp_p.reshape(n_p * ROW_TILE_SUBLANES, LANES)
    hp_b = jnp.concatenate([hp_s[0], jnp.zeros((pad * ROW_TILE_SUBLANES, LANES), F32)], axis=0)
    wt_all = jnp.concatenate([wt_p, wt_s, jnp.full((N_EXPERTS, pad), -1.0, F32)], axis=1)
    dest8, w8, tab, etab = _rank_call(wt_all, n_real, n_blocks, n_blocks_pad)
    xs = _dispatch_call(dest8, etab, hp_a, hp_b, n_real, n_blocks * EXPERT_BLOCK)
    ys = _ffn_call(tab[0], xs, w_e_gate[l], w_e_up[l], w_e_down[l], n_blocks)
    y_all = _combine_call(dest8, w8, hp_a, hp_b, wsg, wsu, wsd, ys)
    y_prompt = _final_call(x1_p, y_all, 0, mod_p[5], norm_post_ffn[l], tm=512, mod_per_row=False)
    y_sample = _final_call(x1_s, y_all, n_p, mod_s[5], norm_post_ffn[l], tm=nd, mod_per_row=True)

    a_p = [z[None] for z in _cache_from_tail(tail_p, keep_p)]
    a_s = [z.reshape(1, nd, DEC_SEQ, 2, N_HEADS_A, HEAD_DIM_A)
           for z in _cache_from_tail(tail_s.reshape(nd, 1, 2 * D_A), [DEC_SEQ] * N_GROUPS_A)]
    shift_p = feat_p[:, -1][None]
    shift_s = feat_s[0][None]
    return (y_prompt, y_sample.reshape(nd, DEC_SEQ, D_MODEL), a_p[0], a_p[1], a_p[2], wkv_p[None], shift_p,
            a_s[0], a_s[1], a_s[2], s_new.reshape(1, nd, N_HEADS_B, HEAD_DIM_B, HEAD_DIM_B), shift_s)
```

```python
import functools
import math

import jax
import jax.numpy as jnp
from jax import lax
from jax.experimental import pallas as pl
from jax.experimental.pallas import tpu as pltpu

F32 = jnp.float32
BF16 = jnp.bfloat16
I32 = jnp.int32

D_MODEL = 1024
BATCH = 2
SEQ = 8192
DEPTH = 1
DEC_BATCH = 32
DEC_SEQ = 1
PAST_LEN = 16384

HEAD_DIM_A = 64
N_HEADS_A = 8
DILATED_GROUPS = ((128, 1), (512, 4), (2048, 16))
N_GROUPS_A = 3
D_GROUP_A = N_HEADS_A * HEAD_DIM_A
D_A = N_GROUPS_A * D_GROUP_A
D_QKV = 3 * D_A
BAND_BLOCK = 128
ROPE_THETA = 10000.0

HEAD_DIM_B = 64
N_HEADS_B = 16
D_B = 1024
DECAY_LORA = 64
AAA_LORA = 64
GATE_LORA = 160
D_SHIFT_B = 3 * D_B + DECAY_LORA + AAA_LORA + GATE_LORA
LN_X_EPS = 64e-5

N_EXPERTS = 64
TOP_K = 8
N_EXPERT_GROUPS = 8
TOPK_GROUPS = 4
D_EXPERT = 256
ROUTED_SCALE = 2.5
EXPERT_BLOCK = 512
NORM_EPS = 1e-6

LANES = 128
WKV_CHUNK = 64
MOE_TILE = 256
VMEM_LIMIT = 56 * 1024 * 1024
ROW_TILE_SUBLANES = D_MODEL // (2 * LANES)
ZERO_ROWS = 256


def _cparams(sem):
    return pltpu.CompilerParams(dimension_semantics=sem, vmem_limit_bytes=VMEM_LIMIT)


def _dot(a, b):
    return jnp.dot(a, b, preferred_element_type=F32)


def _dot_nt(a, b):
    return lax.dot_general(a, b, (((1,), (1,)), ((), ())), preferred_element_type=F32)


def _dot_tn(a, b):
    return lax.dot_general(a, b, (((0,), (0,)), ((), ())), preferred_element_type=F32)


def _dot_exact(a, b):
    return lax.dot_general(a, b, (((1,), (0,)), ((), ())), precision=lax.Precision.HIGHEST,
                           preferred_element_type=F32)


def _rms(x, gain):
    return x * lax.rsqrt(jnp.mean(x * x, axis=-1, keepdims=True) + NORM_EPS) * gain


def _sigmoid(x):
    return 1.0 / (1.0 + jnp.exp(-x))


def _silu(x):
    return x * _sigmoid(x)


def _softplus(x):
    return jnp.maximum(x, 0.0) + jnp.log(1.0 + jnp.exp(-jnp.abs(x)))


def _pack_pairs(x):
    half = D_MODEL // 2
    lo = lax.bitcast_convert_type(x[:, :half].astype(BF16).astype(F32), I32)
    hi = lax.bitcast_convert_type(x[:, half:].astype(BF16).astype(F32), I32)
    return lax.shift_right_logical(lo, 16) | (hi & jnp.int32(-65536))


def _unpack_pairs(w):
    lo = lax.bitcast_convert_type(w << 16, F32)
    hi = lax.bitcast_convert_type(w & jnp.int32(-65536), F32)
    return jnp.concatenate([lo, hi], axis=1)


def _mod_body(c_ref, w_ref, b_ref, o_ref):
    s = _silu(c_ref[...]).astype(BF16)
    o_ref[...] = _dot(s, w_ref[...].astype(BF16)) + b_ref[...]


def _mod_call(c_all, w_ada, b_ada):
    rows = c_all.shape[0]
    tn = 1536
    return pl.pallas_call(
        _mod_body,
        out_shape=jax.ShapeDtypeStruct((rows, 6 * D_MODEL), F32),
        grid=(6 * D_MODEL // tn,),
        in_specs=[pl.BlockSpec((rows, D_MODEL), lambda j: (0, 0)),
                  pl.BlockSpec((D_MODEL, tn), lambda j: (0, j)),
                  pl.BlockSpec((1, tn), lambda j: (0, j))],
        out_specs=pl.BlockSpec((rows, tn), lambda j: (0, j)),
        compiler_params=_cparams(("arbitrary",)),
        name="mod",
    )(c_all, w_ada, b_ada.reshape(1, -1))


def _inproj_body(x_ref, g_ref, sc_ref, sh_ref, cos_ref, sin_ref, wq_ref, wf_ref, wg_ref,
                 q0_ref, q1_ref, q2_ref, feat_ref, gate_ref, tail_ref, p_ref, *, dils):
    x = x_ref[0]
    tm = x.shape[0]
    h = _rms(x, g_ref[...]) * (1.0 + sc_ref[0]) + sh_ref[0]
    hb = h.astype(BF16)
    p = _dot(hb, wq_ref[...])
    cos = cos_ref[...]
    sin = sin_ref[...]
    lane = lax.broadcasted_iota(I32, cos.shape, 1)
    first_half = (lane % HEAD_DIM_A) < (HEAD_DIM_A // 2)
    for c in range(2 * D_A // LANES):
        xc = p[:, c * LANES:(c + 1) * LANES]
        partner = jnp.where(first_half, pltpu.roll(xc, LANES - HEAD_DIM_A // 2, 1),
                            pltpu.roll(xc, HEAD_DIM_A // 2, 1))
        rc = xc * cos + partner * sin
        if c < D_A // LANES:
            rc = rc * (HEAD_DIM_A ** -0.5)
        p_ref[c] = rc
        if c >= D_A // LANES:
            tail_ref[0, :, (c - D_A // LANES) * LANES:(c - D_A // LANES + 1) * LANES] = rc
    for c in range(2 * D_A // LANES, D_QKV // LANES):
        p_ref[c] = p[:, c * LANES:(c + 1) * LANES]
    tail_ref[0, :, D_A:] = p[:, 2 * D_A:]
    per_group = D_GROUP_A // LANES
    for gi, (out_ref, dil) in enumerate(zip((q0_ref, q1_ref, q2_ref), dils)):
        for which in range(3):
            for j in range(per_group):
                c = (which * D_A + gi * D_GROUP_A) // LANES + j
                dst = slice(which * D_GROUP_A + j * LANES, which * D_GROUP_A + (j + 1) * LANES)
                if dil == 1:
                    out_ref[0, 0, :, dst] = p_ref[c].astype(BF16)
                else:
                    for r in range(dil):
                        out_ref[0, r, :, dst] = p_ref[c, pl.ds(r, tm // dil, stride=dil), :].astype(BF16)
    feat_ref[0] = _dot(hb, wf_ref[...])
    gate_ref[0] = _sigmoid(_dot(hb, wg_ref[...])).astype(BF16)


def _inproj_call(x, gain, scale, shift, cos_t, sin_t, wq, wf, wg, tm, tail_rows, mod_per_row, dils):
    nb, t, _ = x.shape
    nt = t // tm
    tail_first = (t - tail_rows) // tm
    if mod_per_row:
        mod_spec = pl.BlockSpec((1, tm, D_MODEL), lambda b, i: (b, i, 0))
    else:
        mod_spec = pl.BlockSpec((1, 1, D_MODEL), lambda b, i: (b, 0, 0))
    resident = lambda shp: pl.BlockSpec(shp, lambda b, i: (0, 0), pipeline_mode=pl.Buffered(1))
    q_shapes = tuple(jax.ShapeDtypeStruct((nb, d, t // d, 3 * D_GROUP_A), BF16) for d in dils)
    q_specs = tuple(pl.BlockSpec((1, d, tm // d, 3 * D_GROUP_A), lambda b, i: (b, 0, i, 0)) for d in dils)
    return pl.pallas_call(
        functools.partial(_inproj_body, dils=dils),
        out_shape=q_shapes + (jax.ShapeDtypeStruct((nb, t, D_SHIFT_B), F32),
                              jax.ShapeDtypeStruct((nb, t, 2 * D_MODEL), BF16),
                              jax.ShapeDtypeStruct((nb, tail_rows, 2 * D_A), F32)),
        grid=(nb, nt),
        in_specs=[pl.BlockSpec((1, tm, D_MODEL), lambda b, i: (b, i, 0)),
                  pl.BlockSpec((1, D_MODEL), lambda b, i: (0, 0)),
                  mod_spec, mod_spec,
                  pl.BlockSpec((tm, LANES), lambda b, i: (i, 0)),
                  pl.BlockSpec((tm, LANES), lambda b, i: (i, 0)),
                  resident((D_MODEL, D_QKV)), resident((D_MODEL, D_SHIFT_B)),
                  resident((D_MODEL, 2 * D_MODEL))],
        out_specs=q_specs + (pl.BlockSpec((1, tm, D_SHIFT_B), lambda b, i: (b, i, 0)),
                             pl.BlockSpec((1, tm, 2 * D_MODEL), lambda b, i: (b, i, 0)),
                             pl.BlockSpec((1, tm, 2 * D_A), lambda b, i: (b, jnp.maximum(i - tail_first, 0), 0))),
        scratch_shapes=[pltpu.VMEM((D_QKV // LANES, tm, LANES), F32)],
        compiler_params=_cparams(("arbitrary", "arbitrary")),
        name="inproj",
    )(x, gain.reshape(1, -1), scale, shift, cos_t, sin_t, wq, wf, wg)


def _attn_body(q_ref, kc_ref, kp_ref, vc_ref, vp_ref, o_ref, lse_ref):
    mb = pl.program_id(2)
    nq = q_ref.shape[2] // BAND_BLOCK
    q = q_ref[0, 0]
    k = jnp.concatenate([kp_ref[0, 0], kc_ref[0, 0]], axis=0)
    v = jnp.concatenate([vp_ref[0, 0], vc_ref[0, 0]], axis=0)
    qi = lax.broadcasted_iota(I32, (BAND_BLOCK, 2 * BAND_BLOCK), 0)
    ki = lax.broadcasted_iota(I32, (BAND_BLOCK, 2 * BAND_BLOCK), 1)
    dist = qi + BAND_BLOCK - ki
    band = (dist >= 0) & (dist <= BAND_BLOCK)
    masks = [band & ((ki >= BAND_BLOCK) | (mb > 0))] + [band] * (nq - 1)
    lane_q = lax.broadcasted_iota(I32, (BAND_BLOCK, LANES), 1)
    lane_k = lax.broadcasted_iota(I32, (2 * BAND_BLOCK, LANES), 1)
    for hp in range(N_HEADS_A // 2):
        sl = slice(hp * LANES, (hp + 1) * LANES)
        chains = [(j, sub) for j in range(nq) for sub in range(2)]
        qs = [q[j * BAND_BLOCK:(j + 1) * BAND_BLOCK, sl] for j in range(nq)]
        ks = [k[j * BAND_BLOCK:(j + 2) * BAND_BLOCK, sl] for j in range(nq)]
        vs = [v[j * BAND_BLOCK:(j + 2) * BAND_BLOCK, sl] for j in range(nq)]
        mqs = [lane_q < HEAD_DIM_A, lane_q >= HEAD_DIM_A]
        mks = [lane_k < HEAD_DIM_A, lane_k >= HEAD_DIM_A]
        s = [jnp.where(masks[j], _dot_nt(jnp.where(mqs[sub], qs[j], jnp.zeros_like(qs[j])), ks[j]), -jnp.inf)
             for j, sub in chains]
        mx = [jnp.max(z, axis=1, keepdims=True) for z in s]
        p = [jnp.exp(z - m) for z, m in zip(s, mx)]
        l = [jnp.sum(z, axis=1, keepdims=True) for z in p]
        pv = [_dot(p[c].astype(BF16), jnp.where(mks[sub], vs[j], jnp.zeros_like(vs[j])))
              for c, (j, sub) in enumerate(chains)]
        for j in range(nq):
            c0, c1 = 2 * j, 2 * j + 1
            o_pair = pv[c0] / l[c0] + pv[c1] / l[c1]
            lse_pair = jnp.where(mqs[0], mx[c0] + jnp.log(l[c0]), mx[c1] + jnp.log(l[c1]))
            o_ref[0, 0, j * BAND_BLOCK:(j + 1) * BAND_BLOCK, sl] = o_pair.astype(BF16)
            lse_ref[0, 0, j * BAND_BLOCK:(j + 1) * BAND_BLOCK, sl] = lse_pair


def _attn_call(qkv_g, gi):
    b, dil, l, _ = qkv_g.shape
    nq = 2
    nb = l // (nq * BAND_BLOCK)
    blk = (1, 1, nq * BAND_BLOCK, D_GROUP_A)
    cur = lambda which: pl.BlockSpec(blk, lambda bb, r, m: (bb, r, m, which))
    prev = lambda which: pl.BlockSpec((1, 1, BAND_BLOCK, D_GROUP_A),
                                      lambda bb, r, m: (bb, r, jnp.maximum(nq * m - 1, 0), which))
    return pl.pallas_call(
        _attn_body,
        out_shape=(jax.ShapeDtypeStruct((b, dil, l, D_GROUP_A), BF16),
                   jax.ShapeDtypeStruct((b, dil, l, D_GROUP_A), F32)),
        grid=(b, dil, nb),
        in_specs=[cur(0), cur(1), prev(1), cur(2), prev(2)],
        out_specs=(pl.BlockSpec(blk, lambda bb, r, m: (bb, r, m, 0)),
                   pl.BlockSpec(blk, lambda bb, r, m: (bb, r, m, 0))),
        compiler_params=_cparams(("arbitrary", "arbitrary", "arbitrary")),
        name=f"attn{gi}",
    )(qkv_g, qkv_g, qkv_g, qkv_g, qkv_g)


def _sattn_body(qkv_ref, b1_ref, b2_ref, b3_ref, o_ref):
    outs, lses = [], []
    for g, (buf_ref, (_, dil)) in enumerate(zip((b1_ref, b2_ref, b3_ref), DILATED_GROUPS)):
        q = qkv_ref[0, g]
        kn = qkv_ref[0, N_GROUPS_A + g]
        vn = qkv_ref[0, 2 * N_GROUPS_A + g]
        kb = buf_ref[0, 0]
        vb = buf_ref[0, 1]
        wb = kb.shape[-1]
        pos = lax.broadcasted_iota(I32, (1, 1, wb), 2)
        s = jnp.sum(kb * q, axis=1, keepdims=True)
        s = jnp.where(pos % dil == 0, s, -jnp.inf)
        sn = jnp.sum(kn * q, axis=1, keepdims=True)
        m = jnp.maximum(jnp.max(s, axis=2, keepdims=True), sn)
        p = jnp.exp(s - m)
        pn = jnp.exp(sn - m)
        l = jnp.sum(p, axis=2, keepdims=True) + pn
        outs.append((jnp.sum(p * vb, axis=2, keepdims=True) + pn * vn) / l)
        lses.append(m + jnp.log(l))
    mx = jnp.maximum(jnp.maximum(lses[0], lses[1]), lses[2])
    es = [jnp.exp(z - mx) for z in lses]
    o_ref[0] = (es[0] * outs[0] + es[1] * outs[1] + es[2] * outs[2]) / (es[0] + es[1] + es[2])


def _sattn_call(qkv_s, c1, c2, c3):
    n = qkv_s.shape[0]
    views, specs = [], []
    for c in (c1, c2, c3):
        wb = c.shape[1]
        views.append(jnp.transpose(c, (0, 2, 3, 4, 1)))
        specs.append(pl.BlockSpec((1, 2, N_HEADS_A, HEAD_DIM_A, wb), lambda b: (b, 0, 0, 0, 0)))
    return pl.pallas_call(
        _sattn_body,
        out_shape=jax.ShapeDtypeStruct((n, N_HEADS_A, HEAD_DIM_A, 1), F32),
        grid=(n,),
        in_specs=[pl.BlockSpec((1, 3 * N_GROUPS_A, N_HEADS_A, HEAD_DIM_A, 1), lambda b: (b, 0, 0, 0, 0))] + specs,
        out_specs=pl.BlockSpec((1, N_HEADS_A, HEAD_DIM_A, 1), lambda b: (b, 0, 0, 0)),
        compiler_params=_cparams(("arbitrary",)),
        name="sattn",
    )(qkv_s, *views)


def _rwkv_features(xs, w0, ww2, a0, wa2, wg2, k_a):
    r = xs[:, :D_B]
    k = xs[:, D_B:2 * D_B]
    v = xs[:, 2 * D_B:3 * D_B]
    xw = xs[:, 3 * D_B:3 * D_B + DECAY_LORA]
    xa = xs[:, 3 * D_B + DECAY_LORA:3 * D_B + DECAY_LORA + AAA_LORA]
    xg = xs[:, 3 * D_B + DECAY_LORA + AAA_LORA:]
    w_log = -_softplus(-(w0 + _dot(jnp.tanh(xw).astype(BF16), ww2.astype(BF16)))) - 0.5
    a = _sigmoid(a0 + _dot(xa.astype(BF16), wa2.astype(BF16)))
    g = _dot(_sigmoid(xg).astype(BF16), wg2.astype(BF16))
    k_h = k * (1.0 + (a - 1.0) * k_a)
    return r, k, v, w_log, a, g, k_h


def _head_norm(kk_h):
    nrm = jnp.sqrt(jnp.sum(kk_h * kk_h, axis=-1, keepdims=True))
    return kk_h / jnp.maximum(nrm, 1e-12)


def _wkv_finish_head(y, r_h, k_h, v_h, g_h, rk_h, lnw_h, lnb_h):
    mean = jnp.mean(y, axis=-1, keepdims=True)
    var = jnp.mean(jnp.square(y - mean), axis=-1, keepdims=True)
    yn = (y - mean) * lax.rsqrt(var + LN_X_EPS) * lnw_h + lnb_h
    bonus = jnp.sum(r_h * k_h * rk_h, axis=-1, keepdims=True) * v_h
    return (yn + bonus) * g_h


def _wkv_body(f_ref, fp_ref, mu_ref, w0_ref, ww2_ref, a0_ref, wa2_ref, wg2_ref, kk_ref, ka_ref,
              rk_ref, lnw_ref, lnb_ref, o_ref, st_ref, s_ref):
    c = pl.program_id(0)
    C = WKV_CHUNK
    nb = f_ref.shape[0]

    @pl.when(c == 0)
    def _():
        s_ref[...] = jnp.zeros_like(s_ref)

    f = jnp.concatenate([f_ref[b] for b in range(nb)], axis=0)
    row = lax.broadcasted_iota(I32, f.shape, 0)
    prev = pltpu.roll(f, 1, 0)
    for b in range(nb):
        prev = jnp.where(row == b * C, jnp.where(c == 0, 0.0, fp_ref[b][7:8, :]), prev)
    xs = f + mu_ref[...] * (prev - f)
    r, k, v, w_log, a, g, k_h = _rwkv_features(xs, w0_ref[...], ww2_ref[...], a0_ref[...],
                                               wa2_ref[...], wg2_ref[...], ka_ref[...])
    lw = -jnp.exp(w_log)
    kk = k * kk_ref[...]
    jh = lax.broadcasted_iota(I32, (D_B, LANES), 0) // HEAD_DIM_B
    ind = (jh == lax.broadcasted_iota(I32, (D_B, LANES), 1)).astype(BF16)
    ind_t = (lax.broadcasted_iota(I32, (LANES, D_B), 0)
             == lax.broadcasted_iota(I32, (LANES, D_B), 1) // HEAD_DIM_B).astype(BF16)

    def head_sum(z):
        hi = z.astype(BF16)
        lo = (z - hi.astype(F32)).astype(BF16)
        s = _dot(hi, ind) + _dot(lo, ind)
        shi = s.astype(BF16)
        slo = (s - shi.astype(F32)).astype(BF16)
        return _dot(shi, ind_t) + _dot(slo, ind_t)

    kkn = kk / jnp.maximum(jnp.sqrt(head_sum(kk * kk)), 1e-12)

    tr = lax.broadcasted_iota(I32, (nb * C, nb * C), 0)
    sr_ = lax.broadcasted_iota(I32, (nb * C, nb * C), 1)
    tri_incl = ((tr >= sr_) & (tr // C == sr_ // C)).astype(BF16)
    l1 = lw.astype(BF16)
    r1 = lw - l1.astype(F32)
    l2 = r1.astype(BF16)
    l3 = (r1 - l2.astype(F32)).astype(BF16)
    cum = _dot(tri_incl, l1) + _dot(tri_incl, l2) + _dot(tri_incl, l3)
    rhos = [cum[b * C + C // 2 - 1:b * C + C // 2, :] for b in range(nb)]
    rho = jnp.concatenate([jnp.broadcast_to(z, (C, D_B)) for z in rhos], axis=0)
    ep = jnp.exp(cum - rho)
    em = jnp.exp(rho - cum)
    e_a = ep * jnp.exp(-lw)
    r_hat = r * ep
    k_hat = k_h * em
    e_rs = [jnp.exp(z) for z in rhos]
    e_cs = [jnp.exp(cum[b * C + C - 1:b * C + C, :] - rhos[b]) for b in range(nb)]

    ti = lax.broadcasted_iota(I32, (C, C), 0)
    si = lax.broadcasted_iota(I32, (C, C), 1)
    strict = ti > si
    incl = ti >= si
    eye = (ti == si).astype(F32)
    rk = rk_ref[...]
    lnw = lnw_ref[...]
    lnb = lnb_ref[...]
    items = [(b, h) for b in range(nb) for h in range(N_HEADS_B)]
    heads = range(len(items))
    lanes = [slice(h * HEAD_DIM_B, (h + 1) * HEAD_DIM_B) for _, h in items]
    cut = lambda z, i: z[items[i][0] * C:(items[i][0] + 1) * C, lanes[i]]
    e_r = [e_rs[b][:, lanes[i]] for i, (b, _) in enumerate(items)]
    e_c = [e_cs[b][:, lanes[i]] for i, (b, _) in enumerate(items)]
    a_hat_full = (-kkn * e_a).astype(BF16)
    b_hat_full = (kkn * a * em).astype(BF16)
    a_hat_b = [cut(a_hat_full, h) for h in heads]
    b_hat_b = [cut(b_hat_full, h) for h in heads]
    rh = [cut(r_hat, h) for h in heads]
    vb = [cut(v, h).astype(BF16) for h in heads]
    bk = [jnp.concatenate([b_hat_b[h], cut(k_hat, h).astype(BF16)], axis=0) for h in heads]
    p = [_dot_nt(jnp.concatenate([a_hat_b[h], rh[h].astype(BF16)], axis=0), bk[h]) for h in heads]
    l_ab = [jnp.where(strict, z[:C, :C], 0.0) for z in p]
    l_ak = [jnp.where(strict, z[:C, C:], 0.0).astype(BF16) for z in p]
    p_rb = [jnp.where(incl, z[C:, :C], 0.0).astype(BF16) for z in p]
    p_rk = [jnp.where(incl, z[C:, C:], 0.0).astype(BF16) for z in p]
    xb = [z.astype(BF16) for z in l_ab]
    tinv = [eye + z for z in l_ab]
    for _ in range(int(math.log2(C)) - 1):
        xb = [_dot(z, z).astype(BF16) for z in xb]
        tinv = [tinv[h] + _dot(tinv[h].astype(BF16), xb[h]) for h in heads]
    tb = [z.astype(BF16) for z in tinv]
    lv = [_dot(l_ak[h], vb[h]).astype(BF16) for h in heads]
    a_bar = [_dot(tb[h], a_hat_b[h]).astype(BF16) for h in heads]
    u_v = [_dot(tb[h], lv[h]).astype(BF16) for h in heads]
    r_bar = [rh[h] + _dot(p_rb[h], a_bar[h]) for h in heads]
    y_v = [_dot(p_rb[h], u_v[h]) + _dot(p_rk[h], vb[h]) for h in heads]
    ab = [_dot_tn(a_bar[h], b_hat_b[h]).astype(BF16) for h in heads]
    n_t = [_dot_tn(jnp.concatenate([u_v[h], vb[h]], axis=0), bk[h]) for h in heads]
    s0 = [s_ref[b, h] for b, h in items]
    sr = [s0[h] * e_r[h] for h in heads]
    y = [_dot_nt((r_bar[h] * e_r[h]).astype(BF16), s0[h].astype(BF16)) + y_v[h] for h in heads]
    s_new = [(sr[h] + _dot(sr[h].astype(BF16), ab[h]) + n_t[h]) * e_c[h] for h in heads]
    for i, (b, h) in enumerate(items):
        s_ref[b, h] = s_new[i]
    y_full = jnp.concatenate([jnp.concatenate(y[b * N_HEADS_B:(b + 1) * N_HEADS_B], axis=1) for b in range(nb)],
                             axis=0)
    inv_hd = 1.0 / HEAD_DIM_B
    dev = y_full - head_sum(y_full) * inv_hd
    yn = dev * lax.rsqrt(head_sum(dev * dev) * inv_hd + LN_X_EPS) * lnw + lnb
    out = (yn + head_sum(r * k_h * rk) * v) * g
    for b in range(nb):
        o_ref[b] = out[b * C:(b + 1) * C, :]

    @pl.when(c == pl.num_programs(0) - 1)
    def _():
        st_ref[...] = s_ref[...]


def _wkv_call(feat, p):
    b, t, _ = feat.shape
    C = WKV_CHUNK
    nc = t // C
    row = lambda n: pl.BlockSpec((1, n), lambda c: (0, 0))
    mat = lambda m, n: pl.BlockSpec((m, n), lambda c: (0, 0))
    return pl.pallas_call(
        _wkv_body,
        out_shape=(jax.ShapeDtypeStruct((b, t, D_B), F32),
                   jax.ShapeDtypeStruct((b, N_HEADS_B, HEAD_DIM_B, HEAD_DIM_B), F32)),
        grid=(nc,),
        in_specs=[pl.BlockSpec((b, C, D_SHIFT_B), lambda c: (0, c, 0)),
                  pl.BlockSpec((b, 8, D_SHIFT_B), lambda c: (0, jnp.maximum(c * (C // 8) - 1, 0), 0)),
                  row(D_SHIFT_B), row(D_B), mat(DECAY_LORA, D_B), row(D_B), mat(AAA_LORA, D_B),
                  mat(GATE_LORA, D_B), row(D_B), row(D_B), row(D_B), row(D_B), row(D_B)],
        out_specs=(pl.BlockSpec((b, C, D_B), lambda c: (0, c, 0)),
                   pl.BlockSpec((b, N_HEADS_B, HEAD_DIM_B, HEAD_DIM_B), lambda c: (0, 0, 0, 0))),
        scratch_shapes=[pltpu.VMEM((b, N_HEADS_B, HEAD_DIM_B, HEAD_DIM_B), F32)],
        compiler_params=_cparams(("arbitrary",)),
        name="wkv",
    )(feat, feat, p['mu_b'], p['w0_b'], p['w_w2_b'], p['a0_b'], p['w_a2_b'], p['w_g2_b'],
      p['k_k_b'], p['k_a_b'], p['r_k_b'], p['ln_x_w_b'], p['ln_x_b_b'])


def _swkv_prep_body(f_ref, sh_ref, mu_ref, w0_ref, ww2_ref, a0_ref, wa2_ref, wg2_ref, kk_ref, ka_ref,
                    r_ref, w_ref, k_ref, v_ref, aa_ref, bb_ref, g_ref):
    f = f_ref[...]
    xs = f + mu_ref[...] * (sh_ref[...] - f)
    r, k, v, w_log, a, g, k_h = _rwkv_features(xs, w0_ref[...], ww2_ref[...], a0_ref[...],
                                               wa2_ref[...], wg2_ref[...], ka_ref[...])
    kk = k * kk_ref[...]
    kkn = jnp.concatenate([_head_norm(kk[:, h * HEAD_DIM_B:(h + 1) * HEAD_DIM_B]) for h in range(N_HEADS_B)],
                          axis=1)
    r_ref[...] = r
    w_ref[...] = jnp.exp(-jnp.exp(w_log))
    k_ref[...] = k_h
    v_ref[...] = v
    aa_ref[...] = -kkn
    bb_ref[...] = kkn * a
    g_ref[...] = g


def _swkv_prep_call(feat_s, shift0, p):
    n = feat_s.shape[0]
    full = lambda a: pl.BlockSpec(a.shape, lambda: tuple(0 for _ in a.shape))
    args = (feat_s, shift0, p['mu_b'], p['w0_b'], p['w_w2_b'], p['a0_b'], p['w_a2_b'], p['w_g2_b'],
            p['k_k_b'], p['k_a_b'])
    return pl.pallas_call(
        _swkv_prep_body,
        out_shape=tuple(jax.ShapeDtypeStruct((n, D_B), F32) for _ in range(7)),
        in_specs=[full(a) for a in args],
        out_specs=tuple(pl.BlockSpec((n, D_B), lambda: (0, 0)) for _ in range(7)),
        compiler_params=pltpu.CompilerParams(vmem_limit_bytes=VMEM_LIMIT),
        name="swkv_prep",
    )(*args)


def _swkv_step_body(s_ref, a_ref, w_ref, b_ref, k_ref, r_ref, v_ref, so_ref, y_ref):
    s = s_ref[...]
    sa = jnp.sum(s * a_ref[...], axis=-1, keepdims=True)
    s2 = s * w_ref[...] + sa * b_ref[...] + v_ref[...] * k_ref[...]
    so_ref[...] = s2
    y_ref[...] = jnp.sum(s2 * r_ref[...], axis=-1, keepdims=True)


def _swkv_step_call(s0, aa, w, bb, k, r, v_col):
    nh = s0.shape[0]
    th = 64
    rowspec = pl.BlockSpec((th, 1, HEAD_DIM_B), lambda i: (i, 0, 0))
    colspec = pl.BlockSpec((th, HEAD_DIM_B, 1), lambda i: (i, 0, 0))
    stspec = pl.BlockSpec((th, HEAD_DIM_B, HEAD_DIM_B), lambda i: (i, 0, 0))
    return pl.pallas_call(
        _swkv_step_body,
        out_shape=(jax.ShapeDtypeStruct((nh, HEAD_DIM_B, HEAD_DIM_B), F32),
                   jax.ShapeDtypeStruct((nh, HEAD_DIM_B, 1), F32)),
        grid=(nh // th,),
        in_specs=[stspec, rowspec, rowspec, rowspec, rowspec, rowspec, colspec],
        out_specs=(stspec, colspec),
        compiler_params=_cparams(("arbitrary",)),
        name="swkv_step",
    )(s0, aa, w, bb, k, r, v_col)


def _swkv_fin_body(y_ref, r_ref, k_ref, v_ref, g_ref, rk_ref, lnw_ref, lnb_ref, o_ref):
    y, r, k, v, g = y_ref[...], r_ref[...], k_ref[...], v_ref[...], g_ref[...]
    rk, lnw, lnb = rk_ref[...], lnw_ref[...], lnb_ref[...]
    outs = []
    for h in range(N_HEADS_B):
        sl = slice(h * HEAD_DIM_B, (h + 1) * HEAD_DIM_B)
        outs.append(_wkv_finish_head(y[:, sl], r[:, sl], k[:, sl], v[:, sl], g[:, sl],
                                     rk[:, sl], lnw[:, sl], lnb[:, sl]))
    o_ref[...] = jnp.concatenate(outs, axis=1)


def _swkv_fin_call(y, r, k, v, g, p):
    n = y.shape[0]
    args = (y, r, k, v, g, p['r_k_b'], p['ln_x_w_b'], p['ln_x_b_b'])
    full = lambda a: pl.BlockSpec(a.shape, lambda: (0, 0))
    return pl.pallas_call(
        _swkv_fin_body,
        out_shape=jax.ShapeDtypeStruct((n, D_B), F32),
        in_specs=[full(a) for a in args],
        out_specs=pl.BlockSpec((n, D_B), lambda: (0, 0)),
        name="swkv_fin",
    )(*args)


def _route_t(scores, bias_col):
    n = scores.shape[1]
    gsz = N_EXPERTS // N_EXPERT_GROUPS
    choice = scores + bias_col
    ninf = -jnp.inf
    sid = lax.broadcasted_iota(I32, (gsz, n), 0)
    gs = []
    for gidx in range(N_EXPERT_GROUPS):
        blk = choice[gidx * gsz:(gidx + 1) * gsz, :]
        m1 = jnp.max(blk, axis=0, keepdims=True)
        first = jnp.min(jnp.where(blk == m1, sid, gsz), axis=0, keepdims=True)
        m2 = jnp.max(jnp.where(sid == first, ninf, blk), axis=0, keepdims=True)
        gs.append(m1 + m2)
    cur = jnp.concatenate(gs, axis=0)
    gid = lax.broadcasted_iota(I32, (N_EXPERT_GROUPS, n), 0)
    gmask = jnp.zeros((N_EXPERT_GROUPS, n), F32)
    for _ in range(TOPK_GROUPS):
        m = jnp.max(cur, axis=0, keepdims=True)
        first = jnp.min(jnp.where(cur == m, gid, N_EXPERT_GROUPS), axis=0, keepdims=True)
        sel = gid == first
        gmask = jnp.where(sel, 1.0, gmask)
        cur = jnp.where(sel, ninf, cur)
    emask = jnp.concatenate([jnp.broadcast_to(gmask[gidx:gidx + 1, :], (gsz, n))
                             for gidx in range(N_EXPERT_GROUPS)], axis=0)
    cur = jnp.where(emask > 0.5, choice, ninf)
    eid = lax.broadcasted_iota(I32, (N_EXPERTS, n), 0)
    selm = jnp.zeros((N_EXPERTS, n), F32)
    for _ in range(TOP_K):
        m = jnp.max(cur, axis=0, keepdims=True)
        first = jnp.min(jnp.where(cur == m, eid, N_EXPERTS), axis=0, keepdims=True)
        sel = eid == first
        selm = jnp.where(sel, 1.0, selm)
        cur = jnp.where(sel, ninf, cur)
    w = jnp.where(selm > 0.5, scores, 0.0)
    w = w / jnp.sum(w, axis=0, keepdims=True) * ROUTED_SCALE
    return jnp.where(selm > 0.5, w, -1.0)


def _unpermute(blk_ref, scr_ref, dil, tm):
    if dil == 1:
        return blk_ref[0, 0].astype(F32)
    n_chunks = scr_ref.shape[0]
    for r in range(dil):
        rows = blk_ref[0, r].astype(F32)
        for j in range(n_chunks):
            scr_ref[j, pl.ds(r, tm // dil, stride=dil), :] = rows[:, j * LANES:(j + 1) * LANES]
    return jnp.concatenate([scr_ref[j] for j in range(n_chunks)], axis=1)


def _post_body(*refs, combine, dils):
    if combine:
        o_refs, l_refs, rest = refs[:3], refs[3:6], refs[6:]
    else:
        o_refs, rest = refs[:1], refs[1:]
    (ob_ref, gt_ref, x_ref, g1_ref, sc2_ref, sh2_ref, npost_ref, npre_ref, wa_ref, wb_ref, wo_ref,
     wrt_ref, rb_ref, x1_ref, hp_ref, wt_ref) = rest[:16]
    scr = rest[16:]
    tm = x_ref.shape[1]
    if combine:
        os_, ls_ = [], []
        si = 0
        for gi, dil in enumerate(dils):
            os_.append(_unpermute(o_refs[gi], scr[si] if dil > 1 else None, dil, tm))
            ls_.append(_unpermute(l_refs[gi], scr[si + 1] if dil > 1 else None, dil, tm))
            si += 2 if dil > 1 else 0
        mx = jnp.maximum(jnp.maximum(ls_[0], ls_[1]), ls_[2])
        es = [jnp.exp(z - mx) for z in ls_]
        o_a = (es[0] * os_[0] + es[1] * os_[1] + es[2] * os_[2]) / (es[0] + es[1] + es[2])
    else:
        o_a = o_refs[0][0]
    gt = gt_ref[0].astype(F32)
    za = _dot(o_a.astype(BF16), wa_ref[...])
    zb = _dot(ob_ref[0].astype(BF16), wb_ref[...])
    merged = gt[:, :D_MODEL] * za + gt[:, D_MODEL:] * zb
    z = _dot(merged.astype(BF16), wo_ref[...])
    x1 = x_ref[0] + g1_ref[0] * _rms(z, npost_ref[...])
    x1_ref[0] = x1
    h2 = _rms(x1, npre_ref[...]) * (1.0 + sc2_ref[0]) + sh2_ref[0]
    packed = _pack_pairs(h2)
    for s in range(ROW_TILE_SUBLANES):
        hp_ref[0, pl.ds(s, tm, stride=ROW_TILE_SUBLANES), :] = packed[:, s * LANES:(s + 1) * LANES]
    tp =-(-tm // LANES) * LANES
    if tp != tm:
        h2 = jnp.concatenate([h2, jnp.zeros((tp - tm, D_MODEL), F32)], axis=0)
    logits_t = lax.dot_general(wrt_ref[...], h2, (((1,), (1,)), ((), ())),
                               precision=lax.Precision.HIGHEST, preferred_element_type=F32)
    w = _route_t(_sigmoid(logits_t[:N_EXPERTS, :]), rb_ref[...])
    wt_ref[...] = w[:, :tm]


def _post_call(o_parts, lse_parts, ob, gates, x, gate1, scale2, shift2, p, wa, wb, wo, wrt, rb, tm, mod_per_row):
    nb, t, _ = x.shape
    nt = t // tm
    combine = lse_parts is not None
    rowblk = lambda width: pl.BlockSpec((1, tm, width), lambda b, i: (b, i, 0))
    if mod_per_row:
        mod_spec = rowblk(D_MODEL)
    else:
        mod_spec = pl.BlockSpec((1, 1, D_MODEL), lambda b, i: (b, 0, 0))
    const = lambda shp: pl.BlockSpec(shp, lambda b, i: (0, 0))
    scratch = []
    if combine:
        dils = tuple(o.shape[1] for o in o_parts)
        o_args = list(o_parts) + list(lse_parts)
        o_specs = [pl.BlockSpec((1, d, tm // d, D_GROUP_A), lambda b, i: (b, 0, i, 0)) for d in dils] * 2
        for d in dils:
            if d > 1:
                scratch += [pltpu.VMEM((D_GROUP_A // LANES, tm, LANES), F32)] * 2
    else:
        dils = ()
        o_args = [o_parts[0]]
        o_specs = [rowblk(D_GROUP_A)]
    return pl.pallas_call(
        functools.partial(_post_body, combine=combine, dils=dils),
        out_shape=(jax.ShapeDtypeStruct((nb, t, D_MODEL), F32),
                   jax.ShapeDtypeStruct((nb, t * ROW_TILE_SUBLANES, LANES), I32),
                   jax.ShapeDtypeStruct((N_EXPERTS, nb * t), F32)),
        grid=(nb, nt),
        in_specs=o_specs + [rowblk(D_B), rowblk(2 * D_MODEL), rowblk(D_MODEL),
                            mod_spec, mod_spec, mod_spec, const((1, D_MODEL)), const((1, D_MODEL)),
                            const((D_GROUP_A, D_MODEL)), const((D_B, D_MODEL)), const((D_MODEL, D_MODEL)),
                            const((LANES, D_MODEL)), const((N_EXPERTS, 1))],
        out_specs=(rowblk(D_MODEL),
                   pl.BlockSpec((1, tm * ROW_TILE_SUBLANES, LANES), lambda b, i: (b, i, 0)),
                   pl.BlockSpec((N_EXPERTS, tm), lambda b, i: (0, b * nt + i))),
        scratch_shapes=scratch,
        compiler_params=_cparams(("arbitrary", "arbitrary")),
        name="post",
    )(*o_args, ob, gates, x, gate1, scale2, shift2, p['norm_post_mix'].reshape(1, -1),
      p['norm_pre_ffn'].reshape(1, -1), wa, wb, wo, wrt, rb)


def _rank_body(w_ref, dest_ref, w8_ref, tab_ref, etab_ref, cnt_ref, pst_ref, run_ref, *, n_real, n_slots):
    ph = pl.program_id(0)
    i = pl.program_id(1)
    T = MOE_TILE
    w = w_ref[...]
    sel = (w >= 0.0).astype(F32)
    cnt_tile = jnp.broadcast_to(jnp.sum(sel, axis=1, keepdims=True), (N_EXPERTS, LANES))
    ei = lax.broadcasted_iota(I32, (N_EXPERTS, N_EXPERTS), 0)
    ej = lax.broadcasted_iota(I32, (N_EXPERTS, N_EXPERTS), 1)

    @pl.when((ph == 0) & (i == 0))
    def _():
        cnt_ref[...] = jnp.zeros_like(cnt_ref)

    @pl.when(ph == 0)
    def _():
        cnt_ref[...] += cnt_tile

    @pl.when((ph == 1) & (i == 0))
    def _():
        cnt = cnt_ref[...]
        padded = jnp.floor((cnt + (EXPERT_BLOCK - 1)) / EXPERT_BLOCK) * EXPERT_BLOCK
        pstart = _dot_exact((ej < ei).astype(F32), padded)
        pst_ref[...] = pstart
        run_ref[...] = jnp.zeros_like(run_ref)
        pend = pstart + padded
        vend = pstart + cnt
        esub = lax.broadcasted_iota(I32, (N_EXPERTS, LANES), 0)
        lane = lax.broadcasted_iota(I32, (1, LANES), 1)
        tab_ref[...] = jnp.zeros_like(tab_ref)
        for c in range(tab_ref.shape[1] // LANES):
            bs = ((c * LANES + lane) * EXPERT_BLOCK).astype(F32)
            be = jnp.minimum(jnp.sum((pend <= bs).astype(F32), axis=0, keepdims=True), N_EXPERTS - 1.0)
            tab_ref[0:1, c * LANES:(c + 1) * LANES] = be.astype(I32)
            tab_ref[1:2, c * LANES:(c + 1) * LANES] = (pend[N_EXPERTS - 1:, :] / EXPERT_BLOCK).astype(I32)
        on_diag = esub == lax.broadcasted_iota(I32, (N_EXPERTS, LANES), 1)
        etab_ref[...] = jnp.zeros_like(etab_ref)
        lo = jnp.sum(jnp.where(on_diag, vend, 0.0), axis=0, keepdims=True)
        hi = jnp.sum(jnp.where(on_diag, pend, 0.0), axis=0, keepdims=True)
        etab_ref[0:1, :] = jnp.where(lane == N_EXPERTS, pend[N_EXPERTS - 1:, :], lo).astype(I32)
        etab_ref[1:2, :] = jnp.where(lane == N_EXPERTS, float(n_slots), hi).astype(I32)

    @pl.when(ph == 1)
    def _():
        ti = lax.broadcasted_iota(I32, (T, T), 0)
        tj = lax.broadcasted_iota(I32, (T, T), 1)
        selb = sel.astype(BF16)
        rank = _dot(selb, (ti < tj).astype(BF16))
        ordn = _dot((ej < ei).astype(BF16), selb)
        dest_e = pst_ref[:, :1] + run_ref[:, :1] + rank
        run_ref[...] += cnt_tile
        tok = i * T + lax.broadcasted_iota(I32, (1, T), 1)
        dks, wks = [], []
        for k in range(TOP_K):
            m = (sel > 0.5) & (ordn == float(k))
            dk = jnp.sum(jnp.where(m, dest_e, 0.0), axis=0, keepdims=True)
            wk = jnp.sum(jnp.where(m, w, 0.0), axis=0, keepdims=True)
            dks.append(jnp.where(tok < n_real, dk, 0.0))
            wks.append(jnp.where(tok < n_real, wk, 0.0))
        dest_ref[...] = jnp.concatenate(dks, axis=0).astype(I32)
        w8_ref[...] = jnp.concatenate(wks, axis=0)


def _rank_call(w_t, n_real, n_blocks, n_blocks_pad):
    n = w_t.shape[1]
    nt = n // MOE_TILE
    return pl.pallas_call(
        functools.partial(_rank_body, n_real=n_real, n_slots=n_blocks * EXPERT_BLOCK),
        out_shape=(jax.ShapeDtypeStruct((TOP_K, n), I32),
                   jax.ShapeDtypeStruct((TOP_K, n), F32),
                   jax.ShapeDtypeStruct((8, n_blocks_pad), I32),
                   jax.ShapeDtypeStruct((8, LANES), I32)),
        grid=(2, nt),
        in_specs=[pl.BlockSpec((N_EXPERTS, MOE_TILE), lambda ph, i: (0, i))],
        out_specs=(pl.BlockSpec((TOP_K, MOE_TILE), lambda ph, i: (0, i * ph)),
                   pl.BlockSpec((TOP_K, MOE_TILE), lambda ph, i: (0, i * ph)),
                   pl.BlockSpec((8, n_blocks_pad), lambda ph, i: (0, 0)),
                   pl.BlockSpec((8, LANES), lambda ph, i: (0, 0))),
        scratch_shapes=[pltpu.VMEM((N_EXPERTS, LANES), F32)] * 3,
        compiler_params=_cparams(("arbitrary", "arbitrary")),
        name="rank",
    )(w_t)


def _tile_rows(ref, row, n):
    return ref.at[pl.ds(pl.multiple_of(row * ROW_TILE_SUBLANES, ROW_TILE_SUBLANES), n * ROW_TILE_SUBLANES)]


def _zero_fill(etab_ref, zbuf, xs_hbm, zsem, wait):
    def go(src, dst):
        cp = pltpu.make_async_copy(src, dst, zsem)
        if wait:
            cp.wait()
        else:
            cp.start()

    def per_range(e, carry):
        lo = etab_ref[0, e]
        n = etab_ref[1, e] - lo
        n_full = n // ZERO_ROWS

        def full(j, c):
            go(zbuf, _tile_rows(xs_hbm, lo + j * ZERO_ROWS, ZERO_ROWS))
            return c

        lax.fori_loop(0, n_full, full, 0)
        pos = lo + n_full * ZERO_ROWS
        rem = n - n_full * ZERO_ROWS
        size = ZERO_ROWS // 2
        while size >= 1:
            bit = rem & size

            @pl.when(bit != 0)
            def _(size=size, pos=pos):
                go(_tile_rows(zbuf, 0, size), _tile_rows(xs_hbm, pos, size))

            pos = pos + bit
            size //= 2
        return carry

    lax.fori_loop(0, N_EXPERTS + 1, per_range, 0)


def _dispatch_body(dest_ref, etab_ref, xa_ref, xb_ref, xs_hbm, zbuf, sem, zsem, *, n_real, n_full):
    i = pl.program_id(0)
    T = MOE_TILE
    n_tok = jnp.clip(n_real - i * T, 0, T)

    def issue_from(x_ref):
        def issue(t, carry):
            for k in range(TOP_K):
                pltpu.make_async_copy(_tile_rows(x_ref, t, 1), _tile_rows(xs_hbm, dest_ref[k * T + t], 1),
                                      sem).start(priority=k % 2)
            return carry

        lax.fori_loop(0, n_tok, issue, 0)

    @pl.when(i < n_full)
    def _():
        issue_from(xa_ref)

    @pl.when(i >= n_full)
    def _():
        issue_from(xb_ref)

    @pl.when(i == 0)
    def _():
        zbuf[...] = jnp.zeros_like(zbuf)
        _zero_fill(etab_ref, zbuf, xs_hbm, zsem, wait=False)
        _zero_fill(etab_ref, zbuf, xs_hbm, zsem, wait=True)

    @pl.when(n_tok == T)
    def _():
        pltpu.make_async_copy(_tile_rows(xs_hbm, 0, T * TOP_K), _tile_rows(xs_hbm, 0, T * TOP_K), sem).wait()

    @pl.when(n_tok < T)
    def _():
        def drain(j, carry):
            pltpu.make_async_copy(_tile_rows(xs_hbm, 0, 1), _tile_rows(xs_hbm, 0, 1), sem).wait()
            return carry

        lax.fori_loop(0, n_tok * TOP_K, drain, 0)


def _dispatch_call(dest, etab, hp_a, hp_b, n_real, n_slots):
    tile_rows = MOE_TILE * ROW_TILE_SUBLANES
    n_full = hp_a.shape[0] // tile_rows
    return pl.pallas_call(
        functools.partial(_dispatch_body, n_real=n_real, n_full=n_full),
        out_shape=jax.ShapeDtypeStruct((n_slots * ROW_TILE_SUBLANES, LANES), I32),
        grid=(n_full + 1,),
        in_specs=[pl.BlockSpec((TOP_K * MOE_TILE,), lambda i: (i,), memory_space=pltpu.SMEM),
                  pl.BlockSpec((8, LANES), lambda i: (0, 0), memory_space=pltpu.SMEM),
                  pl.BlockSpec((tile_rows, LANES), lambda i: (jnp.minimum(i, n_full - 1), 0)),
                  pl.BlockSpec((tile_rows, LANES), lambda i: (0, 0))],
        out_specs=pl.BlockSpec(memory_space=pl.ANY),
        scratch_shapes=[pltpu.VMEM((ZERO_ROWS * ROW_TILE_SUBLANES, LANES), I32),
                        pltpu.SemaphoreType.DMA, pltpu.SemaphoreType.DMA],
        compiler_params=_cparams(("arbitrary",)),
        name="dispatch",
    )(dest, etab, hp_a, hp_b)


def _rows_from_tiles(ref, lo, n):
    return jnp.concatenate([ref[pl.ds(lo * ROW_TILE_SUBLANES + s, n, stride=ROW_TILE_SUBLANES), :]
                            for s in range(ROW_TILE_SUBLANES)], axis=1)


def _ffn_body(be_ref, xs_ref, wg_ref, wu_ref, wd_ref, ys_ref, wgb, wub, wdb):
    j = pl.program_id(0)

    @pl.when((j == 0) | (be_ref[j] != be_ref[jnp.maximum(j - 1, 0)]))
    def _():
        wgb[...] = wg_ref[0].astype(BF16)
        wub[...] = wu_ref[0].astype(BF16)
        wdb[...] = wd_ref[0].astype(BF16)

    x = _unpack_pairs(_rows_from_tiles(xs_ref, 0, EXPERT_BLOCK)).astype(BF16)
    act = _silu(_dot(x, wgb[...])) * _dot(x, wub[...])
    y = _dot(act.astype(BF16), wdb[...])
    packed = _pack_pairs(y)
    for s in range(ROW_TILE_SUBLANES):
        ys_ref[pl.ds(s, EXPERT_BLOCK, stride=ROW_TILE_SUBLANES), :] = packed[:, s * LANES:(s + 1) * LANES]


def _ffn_call(blk_e, xs, w_gate, w_up, w_down, n_blocks):
    tile_blk = pl.BlockSpec((EXPERT_BLOCK * ROW_TILE_SUBLANES, LANES), lambda j, be: (j, 0))
    grid_spec = pltpu.PrefetchScalarGridSpec(
        num_scalar_prefetch=1,
        grid=(n_blocks,),
        in_specs=[tile_blk,
                  pl.BlockSpec((1, D_MODEL, D_EXPERT), lambda j, be: (be[j], 0, 0)),
                  pl.BlockSpec((1, D_MODEL, D_EXPERT), lambda j, be: (be[j], 0, 0)),
                  pl.BlockSpec((1, D_EXPERT, D_MODEL), lambda j, be: (be[j], 0, 0))],
        out_specs=tile_blk,
        scratch_shapes=[pltpu.VMEM((D_MODEL, D_EXPERT), BF16), pltpu.VMEM((D_MODEL, D_EXPERT), BF16),
                        pltpu.VMEM((D_EXPERT, D_MODEL), BF16)])
    return pl.pallas_call(
        _ffn_body,
        out_shape=jax.ShapeDtypeStruct((n_blocks * EXPERT_BLOCK * ROW_TILE_SUBLANES, LANES), I32),
        grid_spec=grid_spec,
        compiler_params=_cparams(("arbitrary",)),
        name="ffn",
    )(blk_e, xs, w_gate, w_up, w_down)


def _combine_body(dest_ref, dnext_ref, w8_ref, xa_ref, xb_ref, sg_ref, su_ref, sd_ref, ys_hbm, y_ref, buf, sem,
                  *, n_full):
    i = pl.program_id(0)
    T = MOE_TILE

    def gather(d_ref, slot):
        def issue(t, carry):
            for k in range(TOP_K):
                pltpu.make_async_copy(_tile_rows(ys_hbm, d_ref[k * T + t], 1), _tile_rows(buf.at[slot], k * T + t, 1),
                                      sem.at[slot]).start(priority=k % 2)
            return carry

        lax.fori_loop(0, T, issue, 0, unroll=2)

    def step(slot):
        @pl.when(i + 1 < pl.num_programs(0))
        def _():
            gather(dnext_ref, 1 - slot)

        x = _unpack_pairs(jnp.where(i < n_full, _rows_from_tiles(xa_ref, 0, T),
                                    _rows_from_tiles(xb_ref, 0, T))).astype(BF16)
        y = _dot((_silu(_dot(x, sg_ref[...])) * _dot(x, su_ref[...])).astype(BF16), sd_ref[...])
        w_t = jnp.concatenate([w8_ref[...], jnp.zeros((LANES - TOP_K, T), F32)], axis=0).T
        pltpu.make_async_copy(_tile_rows(ys_hbm, 0, T * TOP_K), buf.at[slot], sem.at[slot]).wait()
        for k in range(TOP_K):
            y = y + w_t[:, k:k + 1] * _unpack_pairs(_rows_from_tiles(buf.at[slot], k * T, T))
        y_ref[...] = y

    @pl.when(i == 0)
    def _():
        gather(dest_ref, 0)

    @pl.when(i % 2 == 0)
    def _():
        step(0)

    @pl.when(i % 2 == 1)
    def _():
        step(1)


def _combine_call(dest, w8, hp_a, hp_b, wsg, wsu, wsd, ys):
    T = MOE_TILE
    tile_rows = T * ROW_TILE_SUBLANES
    n_full = hp_a.shape[0] // tile_rows
    n_tiles = n_full + 1
    const = lambda shp: pl.BlockSpec(shp, lambda i: (0, 0))
    return pl.pallas_call(
        functools.partial(_combine_body, n_full=n_full),
        out_shape=jax.ShapeDtypeStruct((n_tiles * T, D_MODEL), F32),
        grid=(n_tiles,),
        in_specs=[pl.BlockSpec((TOP_K * T,), lambda i: (i,), memory_space=pltpu.SMEM),
                  pl.BlockSpec((TOP_K * T,), lambda i: (jnp.minimum(i + 1, n_tiles - 1),), memory_space=pltpu.SMEM),
                  pl.BlockSpec((TOP_K, T), lambda i: (0, i)),
                  pl.BlockSpec((tile_rows, LANES), lambda i: (jnp.minimum(i, n_full - 1), 0)),
                  pl.BlockSpec((tile_rows, LANES), lambda i: (0, 0)),
                  const((D_MODEL, D_EXPERT)), const((D_MODEL, D_EXPERT)), const((D_EXPERT, D_MODEL)),
                  pl.BlockSpec(memory_space=pl.ANY)],
        out_specs=pl.BlockSpec((T, D_MODEL), lambda i: (i, 0)),
        scratch_shapes=[pltpu.VMEM((2, TOP_K * tile_rows, LANES), I32), pltpu.SemaphoreType.DMA((2,))],
        compiler_params=_cparams(("arbitrary",)),
        name="combine",
    )(dest, dest, w8, hp_a, hp_b, wsg, wsu, wsd, ys)


def _final_body(x1_ref, y_ref, g2_ref, n_ref, o_ref):
    o_ref[0] = x1_ref[0] + g2_ref[0] * _rms(y_ref[...], n_ref[...])


def _final_call(x1, y_all, row0, gate2, gain, tm, mod_per_row):
    nb, t, _ = x1.shape
    nt = t // tm
    blk0 = row0 // tm
    rowblk = pl.BlockSpec((1, tm, D_MODEL), lambda b, i: (b, i, 0))
    mod_spec = rowblk if mod_per_row else pl.BlockSpec((1, 1, D_MODEL), lambda b, i: (b, 0, 0))
    return pl.pallas_call(
        _final_body,
        out_shape=jax.ShapeDtypeStruct((nb, t, D_MODEL), F32),
        grid=(nb, nt),
        in_specs=[rowblk, pl.BlockSpec((tm, D_MODEL), lambda b, i: (blk0 + b * nt + i, 0)), mod_spec,
                  pl.BlockSpec((1, D_MODEL), lambda b, i: (0, 0))],
        out_specs=rowblk,
        compiler_params=_cparams(("arbitrary", "arbitrary")),
        name="final",
    )(x1, y_all, gate2, gain.reshape(1, -1))


def _rope_tables(pos):
    half = HEAD_DIM_A // 2
    inv_freq = ROPE_THETA ** (-jnp.arange(half, dtype=F32) / half)
    ang = pos.astype(F32)[:, None] * inv_freq[None, :]
    cos = jnp.cos(ang)
    sin = jnp.sin(ang)
    reps = LANES // HEAD_DIM_A
    cos_t = jnp.tile(jnp.concatenate([cos, cos], axis=1), (1, reps))
    sin_t = jnp.tile(jnp.concatenate([-sin, sin], axis=1), (1, reps))
    return cos_t, sin_t


def _cache_from_tail(tail, keep):
    outs = []
    n, rows, _ = tail.shape
    for gi, kp in enumerate(keep):
        k = tail[:, rows - kp:, gi * D_GROUP_A:(gi + 1) * D_GROUP_A]
        v = tail[:, rows - kp:, D_A + gi * D_GROUP_A:D_A + (gi + 1) * D_GROUP_A]
        outs.append(jnp.stack([k, v], axis=2).reshape(n, kp, 2, N_HEADS_A, HEAD_DIM_A))
    return outs


def kernel(x_prompt, x_sample, c_prompt, c_sample, cache_a1_kv, cache_a2_kv, cache_a3_kv, state_b_wkv, state_b_shift, w_ada, b_ada, norm_pre_mix, norm_post_mix, norm_pre_ffn, norm_post_ffn, w_in, w_a_out, mu_b, w0_b, w_w2_b, a0_b, w_a2_b, w_g2_b, k_k_b, k_a_b, r_k_b, ln_x_w_b, ln_x_b_b, w_b_out, w_out, w_router, router_bias, w_e_gate, w_e_up, w_e_down, w_s_gate, w_s_up, w_s_down):
    assert DEPTH == 1
    l = 0
    nd = DEC_BATCH
    row = lambda a: a.reshape(1, -1)
    p = {'mu_b': row(mu_b[l]), 'w0_b': row(w0_b[l]), 'w_w2_b': w_w2_b[l], 'a0_b': row(a0_b[l]),
         'w_a2_b': w_a2_b[l], 'w_g2_b': w_g2_b[l], 'k_k_b': row(k_k_b[l]), 'k_a_b': row(k_a_b[l]),
         'r_k_b': row(r_k_b[l]), 'ln_x_w_b': row(ln_x_w_b[l]), 'ln_x_b_b': row(ln_x_b_b[l]),
         'norm_post_mix': norm_post_mix[l], 'norm_pre_ffn': norm_pre_ffn[l]}

    wq = w_in[l][:, :D_QKV].astype(BF16)
    wf = w_in[l][:, D_QKV:D_QKV + D_SHIFT_B].astype(BF16)
    wg = w_in[l][:, D_QKV + D_SHIFT_B:].astype(BF16)
    wa = w_a_out[l].astype(BF16)
    wb = w_b_out[l].astype(BF16)
    wo = w_out[l].astype(BF16)
    wrt = jnp.concatenate([w_router[l].T, jnp.zeros((LANES - N_EXPERTS, D_MODEL), F32)], axis=0)
    rb = router_bias[l].reshape(N_EXPERTS, 1)
    wsg, wsu, wsd = w_s_gate[l].astype(BF16), w_s_up[l].astype(BF16), w_s_down[l].astype(BF16)

    n_c = BATCH + nd
    c_all = jnp.concatenate([c_prompt, c_sample, jnp.zeros((-n_c % 8, D_MODEL), F32)], axis=0)
    mod = _mod_call(c_all, w_ada[l], b_ada[l])
    mod_p = [m.reshape(BATCH, 1, D_MODEL) for m in jnp.split(mod[:BATCH], 6, axis=-1)]
    mod_s = [m.reshape(1, nd, D_MODEL) for m in jnp.split(mod[BATCH:n_c], 6, axis=-1)]

    cos_p, sin_p = _rope_tables(jnp.arange(SEQ, dtype=I32))
    cos_s, sin_s = _rope_tables(jnp.full((nd,), PAST_LEN, I32))

    keep_p = [min(w, SEQ) for w, _ in DILATED_GROUPS]
    tail_rows = max(keep_p)
    dils = tuple(d for _, d in DILATED_GROUPS)

    q0, q1, q2, feat_p, gates_p, tail_p = _inproj_call(
        x_prompt, norm_pre_mix[l], mod_p[1], mod_p[0], cos_p, sin_p, wq, wf, wg,
        tm=256, tail_rows=tail_rows, mod_per_row=False, dils=dils)
    o_parts, lse_parts = [], []
    for gi, qg in enumerate((q0, q1, q2)):
        o, lse = _attn_call(qg, gi)
        o_parts.append(o)
        lse_parts.append(lse)
    ob_p, wkv_p = _wkv_call(feat_p, p)
    x1_p, hp_p, wt_p = _post_call(o_parts, lse_parts, ob_p, gates_p, x_prompt, mod_p[2], mod_p[4], mod_p[3],
                                  p, wa, wb, wo, wrt, rb, tm=512, mod_per_row=False)

    xs3 = x_sample.reshape(1, nd, D_MODEL)
    s0, s1, s2, feat_s, gates_s, tail_s = _inproj_call(
        xs3, norm_pre_mix[l], mod_s[1], mod_s[0], cos_s, sin_s, wq, wf, wg,
        tm=nd, tail_rows=nd, mod_per_row=True, dils=(1, 1, 1))
    qkv_s = jnp.stack([z.reshape(nd, 3, N_HEADS_A, HEAD_DIM_A) for z in (s0, s1, s2)], axis=2)
    qkv_s = qkv_s.reshape(nd, 3 * N_GROUPS_A, N_HEADS_A, HEAD_DIM_A, 1).astype(F32)
    oa_s = _sattn_call(qkv_s, cache_a1_kv[l], cache_a2_kv[l], cache_a3_kv[l])
    r_s, w_s, k_s, v_s, aa_s, bb_s, g_s = _swkv_prep_call(feat_s[0], state_b_shift[l], p)
    nh = nd * N_HEADS_B
    as_row = lambda a: a.reshape(nh, 1, HEAD_DIM_B)
    s_new, y_col = _swkv_step_call(state_b_wkv[l].reshape(nh, HEAD_DIM_B, HEAD_DIM_B), as_row(aa_s), as_row(w_s),
                                   as_row(bb_s), as_row(k_s), as_row(r_s), v_s.reshape(nh, HEAD_DIM_B, 1))
    ob_s = _swkv_fin_call(y_col.reshape(nd, D_B), r_s, k_s, v_s, g_s, p)
    x1_s, hp_s, wt_s = _post_call([oa_s.reshape(1, nd, D_GROUP_A)], None, ob_s.reshape(1, nd, D_B), gates_s, xs3,
                                  mod_s[2], mod_s[4], mod_s[3], p, wa, wb, wo, wrt, rb, tm=nd, mod_per_row=True)

    n_p = BATCH * SEQ
    n_real = n_p + nd
    n_all = -(-n_real // MOE_TILE) * MOE_TILE
    pad = n_all - n_real
    n_blocks = -(-(n_real * TOP_K) // EXPERT_BLOCK) + N_EXPERTS
    n_blocks_pad = -(-n_blocks // LANES) * LANES
    assert n_p % MOE_TILE == 0 and nd <= MOE_TILE
    hp_a = hp_p.reshape(n_p * ROW_TILE_SUBLANES, LANES)
    hp_b = jnp.concatenate([hp_s[0], jnp.zeros((pad * ROW_TILE_SUBLANES, LANES), I32)], axis=0)
    wt_all = jnp.concatenate([wt_p, wt_s, jnp.full((N_EXPERTS, pad), -1.0, F32)], axis=1)
    dest8, w8, tab, etab = _rank_call(wt_all, n_real, n_blocks, n_blocks_pad)
    dest = dest8.reshape(TOP_K, n_all // MOE_TILE, MOE_TILE).transpose(1, 0, 2).reshape(-1)
    xs = _dispatch_call(dest, etab, hp_a, hp_b, n_real, n_blocks * EXPERT_BLOCK)
    ys = _ffn_call(tab[0], xs, w_e_gate[l], w_e_up[l], w_e_down[l], n_blocks)
    y_all = _combine_call(dest, w8, hp_a, hp_b, wsg, wsu, wsd, ys)
    y_prompt = _final_call(x1_p, y_all, 0, mod_p[5], norm_post_ffn[l], tm=512, mod_per_row=False)
    y_sample = _final_call(x1_s, y_all, n_p, mod_s[5], norm_post_ffn[l], tm=nd, mod_per_row=True)

    a_p = [z[None] for z in _cache_from_tail(tail_p, keep_p)]
    a_s = [z.reshape(1, nd, DEC_SEQ, 2, N_HEADS_A, HEAD_DIM_A)
           for z in _cache_from_tail(tail_s.reshape(nd, 1, 2 * D_A), [DEC_SEQ] * N_GROUPS_A)]
    shift_p = feat_p[:, -1][None]
    shift_s = feat_s[0][None]
    return (y_prompt, y_sample.reshape(nd, DEC_SEQ, D_MODEL), a_p[0], a_p[1], a_p[2], wkv_p[None], shift_p,
            a_s[0], a_s[1], a_s[2], s_new.reshape(1, nd, N_HEADS_B, HEAD_DIM_B, HEAD_DIM_B), shift_s)
```

```python
import functools
import math

import jax
import jax.numpy as jnp
from jax import lax
from jax.experimental import pallas as pl
from jax.experimental.pallas import tpu as pltpu

F32 = jnp.float32
BF16 = jnp.bfloat16
I32 = jnp.int32

D_MODEL = 1024
BATCH = 2
SEQ = 8192
DEPTH = 1
DEC_BATCH = 32
DEC_SEQ = 1
PAST_LEN = 16384

HEAD_DIM_A = 64
N_HEADS_A = 8
DILATED_GROUPS = ((128, 1), (512, 4), (2048, 16))
N_GROUPS_A = 3
D_GROUP_A = N_HEADS_A * HEAD_DIM_A
D_A = N_GROUPS_A * D_GROUP_A
D_QKV = 3 * D_A
BAND_BLOCK = 128
ROPE_THETA = 10000.0

HEAD_DIM_B = 64
N_HEADS_B = 16
D_B = 1024
DECAY_LORA = 64
AAA_LORA = 64
GATE_LORA = 160
D_SHIFT_B = 3 * D_B + DECAY_LORA + AAA_LORA + GATE_LORA
LN_X_EPS = 64e-5

N_EXPERTS = 64
TOP_K = 8
N_EXPERT_GROUPS = 8
TOPK_GROUPS = 4
D_EXPERT = 256
ROUTED_SCALE = 2.5
EXPERT_BLOCK = 512
NORM_EPS = 1e-6

LANES = 128
WKV_CHUNK = 64
MOE_TILE = 512
VMEM_LIMIT = 56 * 1024 * 1024
ROW_TILE_SUBLANES = D_MODEL // (2 * LANES)
ZERO_ROWS = 256


def _cparams(sem):
    return pltpu.CompilerParams(dimension_semantics=sem, vmem_limit_bytes=VMEM_LIMIT)


def _dot(a, b):
    return jnp.dot(a, b, preferred_element_type=F32)


def _dot_nt(a, b):
    return lax.dot_general(a, b, (((1,), (1,)), ((), ())), preferred_element_type=F32)


def _dot_tn(a, b):
    return lax.dot_general(a, b, (((0,), (0,)), ((), ())), preferred_element_type=F32)


def _dot_exact(a, b):
    return lax.dot_general(a, b, (((1,), (0,)), ((), ())), precision=lax.Precision.HIGHEST,
                           preferred_element_type=F32)


def _rms(x, gain):
    return x * lax.rsqrt(jnp.mean(x * x, axis=-1, keepdims=True) + NORM_EPS) * gain


def _sigmoid(x):
    return 1.0 / (1.0 + jnp.exp(-x))


def _silu(x):
    return x * _sigmoid(x)


def _softplus(x):
    return jnp.maximum(x, 0.0) + jnp.log(1.0 + jnp.exp(-jnp.abs(x)))


def _pack_pairs(x):
    half = D_MODEL // 2
    lo = lax.bitcast_convert_type(x[:, :half].astype(BF16).astype(F32), I32)
    hi = lax.bitcast_convert_type(x[:, half:].astype(BF16).astype(F32), I32)
    return lax.shift_right_logical(lo, 16) | (hi & jnp.int32(-65536))


def _unpack_pairs(w):
    lo = lax.bitcast_convert_type(w << 16, F32)
    hi = lax.bitcast_convert_type(w & jnp.int32(-65536), F32)
    return jnp.concatenate([lo, hi], axis=1)


def _mod_body(c_ref, w_ref, b_ref, o_ref):
    s = _silu(c_ref[...]).astype(BF16)
    o_ref[...] = _dot(s, w_ref[...].astype(BF16)) + b_ref[...]


def _mod_call(c_all, w_ada, b_ada):
    rows = c_all.shape[0]
    tn = 1536
    return pl.pallas_call(
        _mod_body,
        out_shape=jax.ShapeDtypeStruct((rows, 6 * D_MODEL), F32),
        grid=(6 * D_MODEL // tn,),
        in_specs=[pl.BlockSpec((rows, D_MODEL), lambda j: (0, 0)),
                  pl.BlockSpec((D_MODEL, tn), lambda j: (0, j)),
                  pl.BlockSpec((1, tn), lambda j: (0, j))],
        out_specs=pl.BlockSpec((rows, tn), lambda j: (0, j)),
        compiler_params=_cparams(("arbitrary",)),
        name="mod",
    )(c_all, w_ada, b_ada.reshape(1, -1))


def _inproj_body(x_ref, g_ref, sc_ref, sh_ref, cos_ref, sin_ref, wq_ref, wf_ref, wg_ref,
                 q0_ref, q1_ref, q2_ref, feat_ref, gate_ref, tail_ref, p_ref, *, dils):
    x = x_ref[0]
    tm = x.shape[0]
    h = _rms(x, g_ref[...]) * (1.0 + sc_ref[0]) + sh_ref[0]
    hb = h.astype(BF16)
    p = _dot(hb, wq_ref[...])
    cos = cos_ref[...]
    sin = sin_ref[...]
    lane = lax.broadcasted_iota(I32, cos.shape, 1)
    first_half = (lane % HEAD_DIM_A) < (HEAD_DIM_A // 2)
    for c in range(2 * D_A // LANES):
        xc = p[:, c * LANES:(c + 1) * LANES]
        partner = jnp.where(first_half, pltpu.roll(xc, LANES - HEAD_DIM_A // 2, 1),
                            pltpu.roll(xc, HEAD_DIM_A // 2, 1))
        rc = xc * cos + partner * sin
        if c < D_A // LANES:
            rc = rc * (HEAD_DIM_A ** -0.5)
        p_ref[c] = rc
        if c >= D_A // LANES:
            tail_ref[0, :, (c - D_A // LANES) * LANES:(c - D_A // LANES + 1) * LANES] = rc
    for c in range(2 * D_A // LANES, D_QKV // LANES):
        p_ref[c] = p[:, c * LANES:(c + 1) * LANES]
    tail_ref[0, :, D_A:] = p[:, 2 * D_A:]
    per_group = D_GROUP_A // LANES
    for gi, (out_ref, dil) in enumerate(zip((q0_ref, q1_ref, q2_ref), dils)):
        for which in range(3):
            for j in range(per_group):
                c = (which * D_A + gi * D_GROUP_A) // LANES + j
                dst = slice(which * D_GROUP_A + j * LANES, which * D_GROUP_A + (j + 1) * LANES)
                if dil == 1:
                    out_ref[0, 0, :, dst] = p_ref[c].astype(BF16)
                else:
                    for r in range(dil):
                        out_ref[0, r, :, dst] = p_ref[c, pl.ds(r, tm // dil, stride=dil), :].astype(BF16)
    feat_ref[0] = _dot(hb, wf_ref[...])
    gate_ref[0] = _sigmoid(_dot(hb, wg_ref[...])).astype(BF16)


def _inproj_call(x, gain, scale, shift, cos_t, sin_t, wq, wf, wg, tm, tail_rows, mod_per_row, dils):
    nb, t, _ = x.shape
    nt = t // tm
    tail_first = (t - tail_rows) // tm
    if mod_per_row:
        mod_spec = pl.BlockSpec((1, tm, D_MODEL), lambda b, i: (b, i, 0))
    else:
        mod_spec = pl.BlockSpec((1, 1, D_MODEL), lambda b, i: (b, 0, 0))
    resident = lambda shp: pl.BlockSpec(shp, lambda b, i: (0, 0), pipeline_mode=pl.Buffered(1))
    q_shapes = tuple(jax.ShapeDtypeStruct((nb, d, t // d, 3 * D_GROUP_A), BF16) for d in dils)
    q_specs = tuple(pl.BlockSpec((1, d, tm // d, 3 * D_GROUP_A), lambda b, i: (b, 0, i, 0)) for d in dils)
    return pl.pallas_call(
        functools.partial(_inproj_body, dils=dils),
        out_shape=q_shapes + (jax.ShapeDtypeStruct((nb, t, D_SHIFT_B), F32),
                              jax.ShapeDtypeStruct((nb, t, 2 * D_MODEL), BF16),
                              jax.ShapeDtypeStruct((nb, tail_rows, 2 * D_A), F32)),
        grid=(nb, nt),
        in_specs=[pl.BlockSpec((1, tm, D_MODEL), lambda b, i: (b, i, 0)),
                  pl.BlockSpec((1, D_MODEL), lambda b, i: (0, 0)),
                  mod_spec, mod_spec,
                  pl.BlockSpec((tm, LANES), lambda b, i: (i, 0)),
                  pl.BlockSpec((tm, LANES), lambda b, i: (i, 0)),
                  resident((D_MODEL, D_QKV)), resident((D_MODEL, D_SHIFT_B)),
                  resident((D_MODEL, 2 * D_MODEL))],
        out_specs=q_specs + (pl.BlockSpec((1, tm, D_SHIFT_B), lambda b, i: (b, i, 0)),
                             pl.BlockSpec((1, tm, 2 * D_MODEL), lambda b, i: (b, i, 0)),
                             pl.BlockSpec((1, tm, 2 * D_A), lambda b, i: (b, jnp.maximum(i - tail_first, 0), 0))),
        scratch_shapes=[pltpu.VMEM((D_QKV // LANES, tm, LANES), F32)],
        compiler_params=_cparams(("arbitrary", "arbitrary")),
        name="inproj",
    )(x, gain.reshape(1, -1), scale, shift, cos_t, sin_t, wq, wf, wg)


def _attn_body(q_ref, kc_ref, kp_ref, vc_ref, vp_ref, o_ref, lse_ref):
    mb = pl.program_id(2)
    nq = q_ref.shape[2] // BAND_BLOCK
    q = q_ref[0, 0]
    k = jnp.concatenate([kp_ref[0, 0], kc_ref[0, 0]], axis=0)
    v = jnp.concatenate([vp_ref[0, 0], vc_ref[0, 0]], axis=0)
    qi = lax.broadcasted_iota(I32, (BAND_BLOCK, 2 * BAND_BLOCK), 0)
    ki = lax.broadcasted_iota(I32, (BAND_BLOCK, 2 * BAND_BLOCK), 1)
    dist = qi + BAND_BLOCK - ki
    band = (dist >= 0) & (dist <= BAND_BLOCK)
    masks = [band & ((ki >= BAND_BLOCK) | (mb > 0))] + [band] * (nq - 1)
    lane_q = lax.broadcasted_iota(I32, (BAND_BLOCK, LANES), 1)
    lane_k = lax.broadcasted_iota(I32, (2 * BAND_BLOCK, LANES), 1)
    for hp in range(N_HEADS_A // 2):
        sl = slice(hp * LANES, (hp + 1) * LANES)
        chains = [(j, sub) for j in range(nq) for sub in range(2)]
        qs = [q[j * BAND_BLOCK:(j + 1) * BAND_BLOCK, sl] for j in range(nq)]
        ks = [k[j * BAND_BLOCK:(j + 2) * BAND_BLOCK, sl] for j in range(nq)]
        vs = [v[j * BAND_BLOCK:(j + 2) * BAND_BLOCK, sl] for j in range(nq)]
        mqs = [lane_q < HEAD_DIM_A, lane_q >= HEAD_DIM_A]
        mks = [lane_k < HEAD_DIM_A, lane_k >= HEAD_DIM_A]
        s = [jnp.where(masks[j], _dot_nt(jnp.where(mqs[sub], qs[j], jnp.zeros_like(qs[j])), ks[j]), -jnp.inf)
             for j, sub in chains]
        mx = [jnp.max(z, axis=1, keepdims=True) for z in s]
        p = [jnp.exp(z - m) for z, m in zip(s, mx)]
        l = [jnp.sum(z, axis=1, keepdims=True) for z in p]
        pv = [_dot(p[c].astype(BF16), jnp.where(mks[sub], vs[j], jnp.zeros_like(vs[j])))
              for c, (j, sub) in enumerate(chains)]
        for j in range(nq):
            c0, c1 = 2 * j, 2 * j + 1
            o_pair = pv[c0] / l[c0] + pv[c1] / l[c1]
            lse_pair = jnp.where(mqs[0], mx[c0] + jnp.log(l[c0]), mx[c1] + jnp.log(l[c1]))
            o_ref[0, 0, j * BAND_BLOCK:(j + 1) * BAND_BLOCK, sl] = o_pair.astype(BF16)
            lse_ref[0, 0, j * BAND_BLOCK:(j + 1) * BAND_BLOCK, sl] = lse_pair


def _attn_call(qkv_g, gi):
    b, dil, l, _ = qkv_g.shape
    nq = 4
    nb = l // (nq * BAND_BLOCK)
    blk = (1, 1, nq * BAND_BLOCK, D_GROUP_A)
    cur = lambda which: pl.BlockSpec(blk, lambda bb, r, m: (bb, r, m, which))
    prev = lambda which: pl.BlockSpec((1, 1, BAND_BLOCK, D_GROUP_A),
                                      lambda bb, r, m: (bb, r, jnp.maximum(nq * m - 1, 0), which))
    return pl.pallas_call(
        _attn_body,
        out_shape=(jax.ShapeDtypeStruct((b, dil, l, D_GROUP_A), BF16),
                   jax.ShapeDtypeStruct((b, dil, l, D_GROUP_A), F32)),
        grid=(b, dil, nb),
        in_specs=[cur(0), cur(1), prev(1), cur(2), prev(2)],
        out_specs=(pl.BlockSpec(blk, lambda bb, r, m: (bb, r, m, 0)),
                   pl.BlockSpec(blk, lambda bb, r, m: (bb, r, m, 0))),
        compiler_params=_cparams(("arbitrary", "arbitrary", "arbitrary")),
        name=f"attn{gi}",
    )(qkv_g, qkv_g, qkv_g, qkv_g, qkv_g)


def _sattn_body(qkv_ref, b1_ref, b2_ref, b3_ref, o_ref):
    outs, lses = [], []
    for g, (buf_ref, (_, dil)) in enumerate(zip((b1_ref, b2_ref, b3_ref), DILATED_GROUPS)):
        q = qkv_ref[0, g]
        kn = qkv_ref[0, N_GROUPS_A + g]
        vn = qkv_ref[0, 2 * N_GROUPS_A + g]
        kb = buf_ref[0, 0]
        vb = buf_ref[0, 1]
        wb = kb.shape[-1]
        pos = lax.broadcasted_iota(I32, (1, 1, wb), 2)
        s = jnp.sum(kb * q, axis=1, keepdims=True)
        s = jnp.where(pos % dil == 0, s, -jnp.inf)
        sn = jnp.sum(kn * q, axis=1, keepdims=True)
        m = jnp.maximum(jnp.max(s, axis=2, keepdims=True), sn)
        p = jnp.exp(s - m)
        pn = jnp.exp(sn - m)
        l = jnp.sum(p, axis=2, keepdims=True) + pn
        outs.append((jnp.sum(p * vb, axis=2, keepdims=True) + pn * vn) / l)
        lses.append(m + jnp.log(l))
    mx = jnp.maximum(jnp.maximum(lses[0], lses[1]), lses[2])
    es = [jnp.exp(z - mx) for z in lses]
    o_ref[0] = (es[0] * outs[0] + es[1] * outs[1] + es[2] * outs[2]) / (es[0] + es[1] + es[2])


def _sattn_call(qkv_s, c1, c2, c3):
    n = qkv_s.shape[0]
    views, specs = [], []
    for c in (c1, c2, c3):
        wb = c.shape[1]
        views.append(jnp.transpose(c, (0, 2, 3, 4, 1)))
        specs.append(pl.BlockSpec((1, 2, N_HEADS_A, HEAD_DIM_A, wb), lambda b: (b, 0, 0, 0, 0)))
    return pl.pallas_call(
        _sattn_body,
        out_shape=jax.ShapeDtypeStruct((n, N_HEADS_A, HEAD_DIM_A, 1), F32),
        grid=(n,),
        in_specs=[pl.BlockSpec((1, 3 * N_GROUPS_A, N_HEADS_A, HEAD_DIM_A, 1), lambda b: (b, 0, 0, 0, 0))] + specs,
        out_specs=pl.BlockSpec((1, N_HEADS_A, HEAD_DIM_A, 1), lambda b: (b, 0, 0, 0)),
        compiler_params=_cparams(("arbitrary",)),
        name="sattn",
    )(qkv_s, *views)


def _rwkv_features(xs, w0, ww2, a0, wa2, wg2, k_a):
    r = xs[:, :D_B]
    k = xs[:, D_B:2 * D_B]
    v = xs[:, 2 * D_B:3 * D_B]
    xw = xs[:, 3 * D_B:3 * D_B + DECAY_LORA]
    xa = xs[:, 3 * D_B + DECAY_LORA:3 * D_B + DECAY_LORA + AAA_LORA]
    xg = xs[:, 3 * D_B + DECAY_LORA + AAA_LORA:]
    w_log = -_softplus(-(w0 + _dot(jnp.tanh(xw).astype(BF16), ww2.astype(BF16)))) - 0.5
    a = _sigmoid(a0 + _dot(xa.astype(BF16), wa2.astype(BF16)))
    g = _dot(_sigmoid(xg).astype(BF16), wg2.astype(BF16))
    k_h = k * (1.0 + (a - 1.0) * k_a)
    return r, k, v, w_log, a, g, k_h


def _head_norm(kk_h):
    nrm = jnp.sqrt(jnp.sum(kk_h * kk_h, axis=-1, keepdims=True))
    return kk_h / jnp.maximum(nrm, 1e-12)


def _wkv_finish_head(y, r_h, k_h, v_h, g_h, rk_h, lnw_h, lnb_h):
    mean = jnp.mean(y, axis=-1, keepdims=True)
    var = jnp.mean(jnp.square(y - mean), axis=-1, keepdims=True)
    yn = (y - mean) * lax.rsqrt(var + LN_X_EPS) * lnw_h + lnb_h
    bonus = jnp.sum(r_h * k_h * rk_h, axis=-1, keepdims=True) * v_h
    return (yn + bonus) * g_h


def _wkv_body(f_ref, fp_ref, mu_ref, w0_ref, ww2_ref, a0_ref, wa2_ref, wg2_ref, kk_ref, ka_ref,
              rk_ref, lnw_ref, lnb_ref, o_ref, st_ref, s_ref):
    c = pl.program_id(0)
    C = WKV_CHUNK
    nb = f_ref.shape[0]

    @pl.when(c == 0)
    def _():
        s_ref[...] = jnp.zeros_like(s_ref)

    f = jnp.concatenate([f_ref[b] for b in range(nb)], axis=0)
    row = lax.broadcasted_iota(I32, f.shape, 0)
    prev = pltpu.roll(f, 1, 0)
    for b in range(nb):
        prev = jnp.where(row == b * C, jnp.where(c == 0, 0.0, fp_ref[b][7:8, :]), prev)
    xs = f + mu_ref[...] * (prev - f)
    r, k, v, w_log, a, g, k_h = _rwkv_features(xs, w0_ref[...], ww2_ref[...], a0_ref[...],
                                               wa2_ref[...], wg2_ref[...], ka_ref[...])
    lw = -jnp.exp(w_log)
    kk = k * kk_ref[...]
    jh = lax.broadcasted_iota(I32, (D_B, LANES), 0) // HEAD_DIM_B
    ind = (jh == lax.broadcasted_iota(I32, (D_B, LANES), 1)).astype(BF16)
    ind_t = (lax.broadcasted_iota(I32, (LANES, D_B), 0)
             == lax.broadcasted_iota(I32, (LANES, D_B), 1) // HEAD_DIM_B).astype(BF16)

    def head_sum(z):
        hi = z.astype(BF16)
        lo = (z - hi.astype(F32)).astype(BF16)
        s = _dot(hi, ind) + _dot(lo, ind)
        shi = s.astype(BF16)
        slo = (s - shi.astype(F32)).astype(BF16)
        return _dot(shi, ind_t) + _dot(slo, ind_t)

    kkn = kk / jnp.maximum(jnp.sqrt(head_sum(kk * kk)), 1e-12)

    tr = lax.broadcasted_iota(I32, (nb * C, nb * C), 0)
    sr_ = lax.broadcasted_iota(I32, (nb * C, nb * C), 1)
    tri_incl = ((tr >= sr_) & (tr // C == sr_ // C)).astype(BF16)
    l1 = lw.astype(BF16)
    r1 = lw - l1.astype(F32)
    l2 = r1.astype(BF16)
    l3 = (r1 - l2.astype(F32)).astype(BF16)
    cum = _dot(tri_incl, l1) + _dot(tri_incl, l2) + _dot(tri_incl, l3)
    rhos = [cum[b * C + C // 2 - 1:b * C + C // 2, :] for b in range(nb)]
    rho = jnp.concatenate([jnp.broadcast_to(z, (C, D_B)) for z in rhos], axis=0)
    ep = jnp.exp(cum - rho)
    em = jnp.exp(rho - cum)
    e_a = ep * jnp.exp(-lw)
    r_hat = r * ep
    k_hat = k_h * em
    e_rs = [jnp.exp(z) for z in rhos]
    e_cs = [jnp.exp(cum[b * C + C - 1:b * C + C, :] - rhos[b]) for b in range(nb)]

    ti = lax.broadcasted_iota(I32, (C, C), 0)
    si = lax.broadcasted_iota(I32, (C, C), 1)
    strict = ti > si
    incl = ti >= si
    eye = (ti == si).astype(F32)
    rk = rk_ref[...]
    lnw = lnw_ref[...]
    lnb = lnb_ref[...]
    items = [(b, h) for b in range(nb) for h in range(N_HEADS_B)]
    heads = range(len(items))
    lanes = [slice(h * HEAD_DIM_B, (h + 1) * HEAD_DIM_B) for _, h in items]
    cut = lambda z, i: z[items[i][0] * C:(items[i][0] + 1) * C, lanes[i]]
    e_r = [e_rs[b][:, lanes[i]] for i, (b, _) in enumerate(items)]
    e_c = [e_cs[b][:, lanes[i]] for i, (b, _) in enumerate(items)]
    a_hat_full = (-kkn * e_a).astype(BF16)
    b_hat_full = (kkn * a * em).astype(BF16)
    a_hat_b = [cut(a_hat_full, h) for h in heads]
    b_hat_b = [cut(b_hat_full, h) for h in heads]
    rh = [cut(r_hat, h) for h in heads]
    vb = [cut(v, h).astype(BF16) for h in heads]
    bk = [jnp.concatenate([b_hat_b[h], cut(k_hat, h).astype(BF16)], axis=0) for h in heads]
    p = [_dot_nt(jnp.concatenate([a_hat_b[h], rh[h].astype(BF16)], axis=0), bk[h]) for h in heads]
    l_ab = [jnp.where(strict, z[:C, :C], 0.0) for z in p]
    l_ak = [jnp.where(strict, z[:C, C:], 0.0).astype(BF16) for z in p]
    p_rb = [jnp.where(incl, z[C:, :C], 0.0).astype(BF16) for z in p]
    p_rk = [jnp.where(incl, z[C:, C:], 0.0).astype(BF16) for z in p]
    xb = [z.astype(BF16) for z in l_ab]
    tinv = [eye + z for z in l_ab]
    for _ in range(int(math.log2(C)) - 1):
        xb = [_dot(z, z).astype(BF16) for z in xb]
        tinv = [tinv[h] + _dot(tinv[h].astype(BF16), xb[h]) for h in heads]
    tb = [z.astype(BF16) for z in tinv]
    lv = [_dot(l_ak[h], vb[h]).astype(BF16) for h in heads]
    a_bar = [_dot(tb[h], a_hat_b[h]).astype(BF16) for h in heads]
    u_v = [_dot(tb[h], lv[h]).astype(BF16) for h in heads]
    r_bar = [rh[h] + _dot(p_rb[h], a_bar[h]) for h in heads]
    y_v = [_dot(p_rb[h], u_v[h]) + _dot(p_rk[h], vb[h]) for h in heads]
    ab = [_dot_tn(a_bar[h], b_hat_b[h]).astype(BF16) for h in heads]
    n_t = [_dot_tn(jnp.concatenate([u_v[h], vb[h]], axis=0), bk[h]) for h in heads]
    s0 = [s_ref[b, h] for b, h in items]
    sr = [s0[h] * e_r[h] for h in heads]
    y = [_dot_nt((r_bar[h] * e_r[h]).astype(BF16), s0[h].astype(BF16)) + y_v[h] for h in heads]
    s_new = [(sr[h] + _dot(sr[h].astype(BF16), ab[h]) + n_t[h]) * e_c[h] for h in heads]
    for i, (b, h) in enumerate(items):
        s_ref[b, h] = s_new[i]
    y_full = jnp.concatenate([jnp.concatenate(y[b * N_HEADS_B:(b + 1) * N_HEADS_B], axis=1) for b in range(nb)],
                             axis=0)
    inv_hd = 1.0 / HEAD_DIM_B
    dev = y_full - head_sum(y_full) * inv_hd
    yn = dev * lax.rsqrt(head_sum(dev * dev) * inv_hd + LN_X_EPS) * lnw + lnb
    out = (yn + head_sum(r * k_h * rk) * v) * g
    for b in range(nb):
        o_ref[b] = out[b * C:(b + 1) * C, :]

    @pl.when(c == pl.num_programs(0) - 1)
    def _():
        st_ref[...] = s_ref[...]


def _wkv_call(feat, p):
    b, t, _ = feat.shape
    C = WKV_CHUNK
    nc = t // C
    row = lambda n: pl.BlockSpec((1, n), lambda c: (0, 0))
    mat = lambda m, n: pl.BlockSpec((m, n), lambda c: (0, 0))
    return pl.pallas_call(
        _wkv_body,
        out_shape=(jax.ShapeDtypeStruct((b, t, D_B), F32),
                   jax.ShapeDtypeStruct((b, N_HEADS_B, HEAD_DIM_B, HEAD_DIM_B), F32)),
        grid=(nc,),
        in_specs=[pl.BlockSpec((b, C, D_SHIFT_B), lambda c: (0, c, 0)),
                  pl.BlockSpec((b, 8, D_SHIFT_B), lambda c: (0, jnp.maximum(c * (C // 8) - 1, 0), 0)),
                  row(D_SHIFT_B), row(D_B), mat(DECAY_LORA, D_B), row(D_B), mat(AAA_LORA, D_B),
                  mat(GATE_LORA, D_B), row(D_B), row(D_B), row(D_B), row(D_B), row(D_B)],
        out_specs=(pl.BlockSpec((b, C, D_B), lambda c: (0, c, 0)),
                   pl.BlockSpec((b, N_HEADS_B, HEAD_DIM_B, HEAD_DIM_B), lambda c: (0, 0, 0, 0))),
        scratch_shapes=[pltpu.VMEM((b, N_HEADS_B, HEAD_DIM_B, HEAD_DIM_B), F32)],
        compiler_params=_cparams(("arbitrary",)),
        name="wkv",
    )(feat, feat, p['mu_b'], p['w0_b'], p['w_w2_b'], p['a0_b'], p['w_a2_b'], p['w_g2_b'],
      p['k_k_b'], p['k_a_b'], p['r_k_b'], p['ln_x_w_b'], p['ln_x_b_b'])


def _swkv_prep_body(f_ref, sh_ref, mu_ref, w0_ref, ww2_ref, a0_ref, wa2_ref, wg2_ref, kk_ref, ka_ref,
                    r_ref, w_ref, k_ref, v_ref, aa_ref, bb_ref, g_ref):
    f = f_ref[...]
    xs = f + mu_ref[...] * (sh_ref[...] - f)
    r, k, v, w_log, a, g, k_h = _rwkv_features(xs, w0_ref[...], ww2_ref[...], a0_ref[...],
                                               wa2_ref[...], wg2_ref[...], ka_ref[...])
    kk = k * kk_ref[...]
    kkn = jnp.concatenate([_head_norm(kk[:, h * HEAD_DIM_B:(h + 1) * HEAD_DIM_B]) for h in range(N_HEADS_B)],
                          axis=1)
    r_ref[...] = r
    w_ref[...] = jnp.exp(-jnp.exp(w_log))
    k_ref[...] = k_h
    v_ref[...] = v
    aa_ref[...] = -kkn
    bb_ref[...] = kkn * a
    g_ref[...] = g


def _swkv_prep_call(feat_s, shift0, p):
    n = feat_s.shape[0]
    full = lambda a: pl.BlockSpec(a.shape, lambda: tuple(0 for _ in a.shape))
    args = (feat_s, shift0, p['mu_b'], p['w0_b'], p['w_w2_b'], p['a0_b'], p['w_a2_b'], p['w_g2_b'],
            p['k_k_b'], p['k_a_b'])
    return pl.pallas_call(
        _swkv_prep_body,
        out_shape=tuple(jax.ShapeDtypeStruct((n, D_B), F32) for _ in range(7)),
        in_specs=[full(a) for a in args],
        out_specs=tuple(pl.BlockSpec((n, D_B), lambda: (0, 0)) for _ in range(7)),
        compiler_params=pltpu.CompilerParams(vmem_limit_bytes=VMEM_LIMIT),
        name="swkv_prep",
    )(*args)


def _swkv_step_body(s_ref, a_ref, w_ref, b_ref, k_ref, r_ref, v_ref, so_ref, y_ref):
    s = s_ref[...]
    sa = jnp.sum(s * a_ref[...], axis=-1, keepdims=True)
    s2 = s * w_ref[...] + sa * b_ref[...] + v_ref[...] * k_ref[...]
    so_ref[...] = s2
    y_ref[...] = jnp.sum(s2 * r_ref[...], axis=-1, keepdims=True)


def _swkv_step_call(s0, aa, w, bb, k, r, v_col):
    nh = s0.shape[0]
    th = 64
    rowspec = pl.BlockSpec((th, 1, HEAD_DIM_B), lambda i: (i, 0, 0))
    colspec = pl.BlockSpec((th, HEAD_DIM_B, 1), lambda i: (i, 0, 0))
    stspec = pl.BlockSpec((th, HEAD_DIM_B, HEAD_DIM_B), lambda i: (i, 0, 0))
    return pl.pallas_call(
        _swkv_step_body,
        out_shape=(jax.ShapeDtypeStruct((nh, HEAD_DIM_B, HEAD_DIM_B), F32),
                   jax.ShapeDtypeStruct((nh, HEAD_DIM_B, 1), F32)),
        grid=(nh // th,),
        in_specs=[stspec, rowspec, rowspec, rowspec, rowspec, rowspec, colspec],
        out_specs=(stspec, colspec),
        compiler_params=_cparams(("arbitrary",)),
        name="swkv_step",
    )(s0, aa, w, bb, k, r, v_col)


def _swkv_fin_body(y_ref, r_ref, k_ref, v_ref, g_ref, rk_ref, lnw_ref, lnb_ref, o_ref):
    y, r, k, v, g = y_ref[...], r_ref[...], k_ref[...], v_ref[...], g_ref[...]
    rk, lnw, lnb = rk_ref[...], lnw_ref[...], lnb_ref[...]
    outs = []
    for h in range(N_HEADS_B):
        sl = slice(h * HEAD_DIM_B, (h + 1) * HEAD_DIM_B)
        outs.append(_wkv_finish_head(y[:, sl], r[:, sl], k[:, sl], v[:, sl], g[:, sl],
                                     rk[:, sl], lnw[:, sl], lnb[:, sl]))
    o_ref[...] = jnp.concatenate(outs, axis=1)


def _swkv_fin_call(y, r, k, v, g, p):
    n = y.shape[0]
    args = (y, r, k, v, g, p['r_k_b'], p['ln_x_w_b'], p['ln_x_b_b'])
    full = lambda a: pl.BlockSpec(a.shape, lambda: (0, 0))
    return pl.pallas_call(
        _swkv_fin_body,
        out_shape=jax.ShapeDtypeStruct((n, D_B), F32),
        in_specs=[full(a) for a in args],
        out_specs=pl.BlockSpec((n, D_B), lambda: (0, 0)),
        name="swkv_fin",
    )(*args)


def _route_t(scores, bias_col):
    n = scores.shape[1]
    gsz = N_EXPERTS // N_EXPERT_GROUPS
    choice = scores + bias_col
    ninf = -jnp.inf
    sid = lax.broadcasted_iota(I32, (gsz, n), 0)
    gs = []
    for gidx in range(N_EXPERT_GROUPS):
        blk = choice[gidx * gsz:(gidx + 1) * gsz, :]
        m1 = jnp.max(blk, axis=0, keepdims=True)
        first = jnp.min(jnp.where(blk == m1, sid, gsz), axis=0, keepdims=True)
        m2 = jnp.max(jnp.where(sid == first, ninf, blk), axis=0, keepdims=True)
        gs.append(m1 + m2)
    cur = jnp.concatenate(gs, axis=0)
    gid = lax.broadcasted_iota(I32, (N_EXPERT_GROUPS, n), 0)
    gmask = jnp.zeros((N_EXPERT_GROUPS, n), F32)
    for _ in range(TOPK_GROUPS):
        m = jnp.max(cur, axis=0, keepdims=True)
        first = jnp.min(jnp.where(cur == m, gid, N_EXPERT_GROUPS), axis=0, keepdims=True)
        sel = gid == first
        gmask = jnp.where(sel, 1.0, gmask)
        cur = jnp.where(sel, ninf, cur)
    emask = jnp.concatenate([jnp.broadcast_to(gmask[gidx:gidx + 1, :], (gsz, n))
                             for gidx in range(N_EXPERT_GROUPS)], axis=0)
    cur = jnp.where(emask > 0.5, choice, ninf)
    eid = lax.broadcasted_iota(I32, (N_EXPERTS, n), 0)
    selm = jnp.zeros((N_EXPERTS, n), F32)
    for _ in range(TOP_K):
        m = jnp.max(cur, axis=0, keepdims=True)
        first = jnp.min(jnp.where(cur == m, eid, N_EXPERTS), axis=0, keepdims=True)
        sel = eid == first
        selm = jnp.where(sel, 1.0, selm)
        cur = jnp.where(sel, ninf, cur)
    w = jnp.where(selm > 0.5, scores, 0.0)
    w = w / jnp.sum(w, axis=0, keepdims=True) * ROUTED_SCALE
    return jnp.where(selm > 0.5, w, -1.0)


def _unpermute(blk_ref, scr_ref, dil, tm):
    if dil == 1:
        return blk_ref[0, 0].astype(F32)
    n_chunks = scr_ref.shape[0]
    for r in range(dil):
        rows = blk_ref[0, r].astype(F32)
        for j in range(n_chunks):
            scr_ref[j, pl.ds(r, tm // dil, stride=dil), :] = rows[:, j * LANES:(j + 1) * LANES]
    return jnp.concatenate([scr_ref[j] for j in range(n_chunks)], axis=1)


def _post_body(*refs, combine, dils):
    if combine:
        o_refs, l_refs, rest = refs[:3], refs[3:6], refs[6:]
    else:
        o_refs, rest = refs[:1], refs[1:]
    (ob_ref, gt_ref, x_ref, g1_ref, sc2_ref, sh2_ref, npost_ref, npre_ref, wa_ref, wb_ref, wo_ref,
     wrt_ref, rb_ref, x1_ref, hp_ref, wt_ref) = rest[:16]
    scr = rest[16:]
    tm = x_ref.shape[1]
    if combine:
        os_, ls_ = [], []
        si = 0
        for gi, dil in enumerate(dils):
            os_.append(_unpermute(o_refs[gi], scr[si] if dil > 1 else None, dil, tm))
            ls_.append(_unpermute(l_refs[gi], scr[si + 1] if dil > 1 else None, dil, tm))
            si += 2 if dil > 1 else 0
        mx = jnp.maximum(jnp.maximum(ls_[0], ls_[1]), ls_[2])
        es = [jnp.exp(z - mx) for z in ls_]
        o_a = (es[0] * os_[0] + es[1] * os_[1] + es[2] * os_[2]) / (es[0] + es[1] + es[2])
    else:
        o_a = o_refs[0][0]
    gt = gt_ref[0].astype(F32)
    za = _dot(o_a.astype(BF16), wa_ref[...])
    zb = _dot(ob_ref[0].astype(BF16), wb_ref[...])
    merged = gt[:, :D_MODEL] * za + gt[:, D_MODEL:] * zb
    z = _dot(merged.astype(BF16), wo_ref[...])
    x1 = x_ref[0] + g1_ref[0] * _rms(z, npost_ref[...])
    x1_ref[0] = x1
    h2 = _rms(x1, npre_ref[...]) * (1.0 + sc2_ref[0]) + sh2_ref[0]
    packed = _pack_pairs(h2)
    for s in range(ROW_TILE_SUBLANES):
        hp_ref[0, pl.ds(s, tm, stride=ROW_TILE_SUBLANES), :] = packed[:, s * LANES:(s + 1) * LANES]
    tp =-(-tm // LANES) * LANES
    if tp != tm:
        h2 = jnp.concatenate([h2, jnp.zeros((tp - tm, D_MODEL), F32)], axis=0)
    logits_t = lax.dot_general(wrt_ref[...], h2, (((1,), (1,)), ((), ())),
                               precision=lax.Precision.HIGHEST, preferred_element_type=F32)
    w = _route_t(_sigmoid(logits_t[:N_EXPERTS, :]), rb_ref[...])
    wt_ref[...] = w[:, :tm]


def _post_call(o_parts, lse_parts, ob, gates, x, gate1, scale2, shift2, p, wa, wb, wo, wrt, rb, tm, mod_per_row):
    nb, t, _ = x.shape
    nt = t // tm
    combine = lse_parts is not None
    rowblk = lambda width: pl.BlockSpec((1, tm, width), lambda b, i: (b, i, 0))
    if mod_per_row:
        mod_spec = rowblk(D_MODEL)
    else:
        mod_spec = pl.BlockSpec((1, 1, D_MODEL), lambda b, i: (b, 0, 0))
    const = lambda shp: pl.BlockSpec(shp, lambda b, i: (0, 0))
    scratch = []
    if combine:
        dils = tuple(o.shape[1] for o in o_parts)
        o_args = list(o_parts) + list(lse_parts)
        o_specs = [pl.BlockSpec((1, d, tm // d, D_GROUP_A), lambda b, i: (b, 0, i, 0)) for d in dils] * 2
        for d in dils:
            if d > 1:
                scratch += [pltpu.VMEM((D_GROUP_A // LANES, tm, LANES), F32)] * 2
    else:
        dils = ()
        o_args = [o_parts[0]]
        o_specs = [rowblk(D_GROUP_A)]
    return pl.pallas_call(
        functools.partial(_post_body, combine=combine, dils=dils),
        out_shape=(jax.ShapeDtypeStruct((nb, t, D_MODEL), F32),
                   jax.ShapeDtypeStruct((nb, t * ROW_TILE_SUBLANES, LANES), I32),
                   jax.ShapeDtypeStruct((N_EXPERTS, nb * t), F32)),
        grid=(nb, nt),
        in_specs=o_specs + [rowblk(D_B), rowblk(2 * D_MODEL), rowblk(D_MODEL),
                            mod_spec, mod_spec, mod_spec, const((1, D_MODEL)), const((1, D_MODEL)),
                            const((D_GROUP_A, D_MODEL)), const((D_B, D_MODEL)), const((D_MODEL, D_MODEL)),
                            const((LANES, D_MODEL)), const((N_EXPERTS, 1))],
        out_specs=(rowblk(D_MODEL),
                   pl.BlockSpec((1, tm * ROW_TILE_SUBLANES, LANES), lambda b, i: (b, i, 0)),
                   pl.BlockSpec((N_EXPERTS, tm), lambda b, i: (0, b * nt + i))),
        scratch_shapes=scratch,
        compiler_params=_cparams(("arbitrary", "arbitrary")),
        name="post",
    )(*o_args, ob, gates, x, gate1, scale2, shift2, p['norm_post_mix'].reshape(1, -1),
      p['norm_pre_ffn'].reshape(1, -1), wa, wb, wo, wrt, rb)


def _rank_body(w_ref, dest_ref, w8_ref, tab_ref, etab_ref, cnt_ref, pst_ref, run_ref, *, n_real, n_slots):
    ph = pl.program_id(0)
    i = pl.program_id(1)
    T = MOE_TILE
    w = w_ref[...]
    sel = (w >= 0.0).astype(F32)
    cnt_tile = jnp.broadcast_to(jnp.sum(sel, axis=1, keepdims=True), (N_EXPERTS, LANES))
    ei = lax.broadcasted_iota(I32, (N_EXPERTS, N_EXPERTS), 0)
    ej = lax.broadcasted_iota(I32, (N_EXPERTS, N_EXPERTS), 1)

    @pl.when((ph == 0) & (i == 0))
    def _():
        cnt_ref[...] = jnp.zeros_like(cnt_ref)

    @pl.when(ph == 0)
    def _():
        cnt_ref[...] += cnt_tile

    @pl.when((ph == 1) & (i == 0))
    def _():
        cnt = cnt_ref[...]
        padded = jnp.floor((cnt + (EXPERT_BLOCK - 1)) / EXPERT_BLOCK) * EXPERT_BLOCK
        pstart = _dot_exact((ej < ei).astype(F32), padded)
        pst_ref[...] = pstart
        run_ref[...] = jnp.zeros_like(run_ref)
        pend = pstart + padded
        vend = pstart + cnt
        esub = lax.broadcasted_iota(I32, (N_EXPERTS, LANES), 0)
        lane = lax.broadcasted_iota(I32, (1, LANES), 1)
        tab_ref[...] = jnp.zeros_like(tab_ref)
        for c in range(tab_ref.shape[1] // LANES):
            bs = ((c * LANES + lane) * EXPERT_BLOCK).astype(F32)
            be = jnp.minimum(jnp.sum((pend <= bs).astype(F32), axis=0, keepdims=True), N_EXPERTS - 1.0)
            tab_ref[0:1, c * LANES:(c + 1) * LANES] = be.astype(I32)
            tab_ref[1:2, c * LANES:(c + 1) * LANES] = (pend[N_EXPERTS - 1:, :] / EXPERT_BLOCK).astype(I32)
        on_diag = esub == lax.broadcasted_iota(I32, (N_EXPERTS, LANES), 1)
        etab_ref[...] = jnp.zeros_like(etab_ref)
        lo = jnp.sum(jnp.where(on_diag, vend, 0.0), axis=0, keepdims=True)
        hi = jnp.sum(jnp.where(on_diag, pend, 0.0), axis=0, keepdims=True)
        etab_ref[0:1, :] = jnp.where(lane == N_EXPERTS, pend[N_EXPERTS - 1:, :], lo).astype(I32)
        etab_ref[1:2, :] = jnp.where(lane == N_EXPERTS, float(n_slots), hi).astype(I32)

    @pl.when(ph == 1)
    def _():
        ti = lax.broadcasted_iota(I32, (T, T), 0)
        tj = lax.broadcasted_iota(I32, (T, T), 1)
        selb = sel.astype(BF16)
        rank = _dot(selb, (ti < tj).astype(BF16))
        ordn = _dot((ej < ei).astype(BF16), selb)
        dest_e = pst_ref[:, :1] + run_ref[:, :1] + rank
        run_ref[...] += cnt_tile
        tok = i * T + lax.broadcasted_iota(I32, (1, T), 1)
        dks, wks = [], []
        for k in range(TOP_K):
            m = (sel > 0.5) & (ordn == float(k))
            dk = jnp.sum(jnp.where(m, dest_e, 0.0), axis=0, keepdims=True)
            wk = jnp.sum(jnp.where(m, w, 0.0), axis=0, keepdims=True)
            dks.append(jnp.where(tok < n_real, dk, 0.0))
            wks.append(jnp.where(tok < n_real, wk, 0.0))
        dest_ref[...] = jnp.concatenate(dks, axis=0).astype(I32)
        w8_ref[...] = jnp.concatenate(wks, axis=0)


def _rank_call(w_t, n_real, n_blocks, n_blocks_pad):
    n = w_t.shape[1]
    nt = n // MOE_TILE
    return pl.pallas_call(
        functools.partial(_rank_body, n_real=n_real, n_slots=n_blocks * EXPERT_BLOCK),
        out_shape=(jax.ShapeDtypeStruct((TOP_K, n), I32),
                   jax.ShapeDtypeStruct((TOP_K, n), F32),
                   jax.ShapeDtypeStruct((8, n_blocks_pad), I32),
                   jax.ShapeDtypeStruct((8, LANES), I32)),
        grid=(2, nt),
        in_specs=[pl.BlockSpec((N_EXPERTS, MOE_TILE), lambda ph, i: (0, i))],
        out_specs=(pl.BlockSpec((TOP_K, MOE_TILE), lambda ph, i: (0, i * ph)),
                   pl.BlockSpec((TOP_K, MOE_TILE), lambda ph, i: (0, i * ph)),
                   pl.BlockSpec((8, n_blocks_pad), lambda ph, i: (0, 0)),
                   pl.BlockSpec((8, LANES), lambda ph, i: (0, 0))),
        scratch_shapes=[pltpu.VMEM((N_EXPERTS, LANES), F32)] * 3,
        compiler_params=_cparams(("arbitrary", "arbitrary")),
        name="rank",
    )(w_t)


def _tile_rows(ref, row, n):
    return ref.at[pl.ds(pl.multiple_of(row * ROW_TILE_SUBLANES, ROW_TILE_SUBLANES), n * ROW_TILE_SUBLANES)]


def _zero_fill(etab_ref, zbuf, xs_hbm, zsem, wait):
    def go(src, dst):
        cp = pltpu.make_async_copy(src, dst, zsem)
        if wait:
            cp.wait()
        else:
            cp.start()

    def per_range(e, carry):
        lo = etab_ref[0, e]
        n = etab_ref[1, e] - lo
        n_full = n // ZERO_ROWS

        def full(j, c):
            go(zbuf, _tile_rows(xs_hbm, lo + j * ZERO_ROWS, ZERO_ROWS))
            return c

        lax.fori_loop(0, n_full, full, 0)
        pos = lo + n_full * ZERO_ROWS
        rem = n - n_full * ZERO_ROWS
        size = ZERO_ROWS // 2
        while size >= 1:
            bit = rem & size

            @pl.when(bit != 0)
            def _(size=size, pos=pos):
                go(_tile_rows(zbuf, 0, size), _tile_rows(xs_hbm, pos, size))

            pos = pos + bit
            size //= 2
        return carry

    lax.fori_loop(0, N_EXPERTS + 1, per_range, 0)


def _dispatch_body(dest_ref, etab_ref, xa_ref, xb_ref, xs_hbm, zbuf, sem, zsem, *, n_real, n_full):
    i = pl.program_id(0)
    T = MOE_TILE
    n_tok = jnp.clip(n_real - i * T, 0, T)

    def issue_from(x_ref):
        def issue(t, carry):
            for k in range(TOP_K):
                pltpu.make_async_copy(_tile_rows(x_ref, t, 1), _tile_rows(xs_hbm, dest_ref[k * T + t], 1),
                                      sem).start(priority=k % 2)
            return carry

        lax.fori_loop(0, n_tok, issue, 0)

    @pl.when(i < n_full)
    def _():
        issue_from(xa_ref)

    @pl.when(i >= n_full)
    def _():
        issue_from(xb_ref)

    @pl.when(i == 0)
    def _():
        zbuf[...] = jnp.zeros_like(zbuf)
        _zero_fill(etab_ref, zbuf, xs_hbm, zsem, wait=False)
        _zero_fill(etab_ref, zbuf, xs_hbm, zsem, wait=True)

    @pl.when(n_tok == T)
    def _():
        pltpu.make_async_copy(_tile_rows(xs_hbm, 0, T * TOP_K), _tile_rows(xs_hbm, 0, T * TOP_K), sem).wait()

    @pl.when(n_tok < T)
    def _():
        def drain(j, carry):
            pltpu.make_async_copy(_tile_rows(xs_hbm, 0, 1), _tile_rows(xs_hbm, 0, 1), sem).wait()
            return carry

        lax.fori_loop(0, n_tok * TOP_K, drain, 0)


def _dispatch_call(dest, etab, hp_a, hp_b, n_real, n_slots):
    tile_rows = MOE_TILE * ROW_TILE_SUBLANES
    n_full = hp_a.shape[0] // tile_rows
    return pl.pallas_call(
        functools.partial(_dispatch_body, n_real=n_real, n_full=n_full),
        out_shape=jax.ShapeDtypeStruct((n_slots * ROW_TILE_SUBLANES, LANES), I32),
        grid=(n_full + 1,),
        in_specs=[pl.BlockSpec((TOP_K * MOE_TILE,), lambda i: (i,), memory_space=pltpu.SMEM),
                  pl.BlockSpec((8, LANES), lambda i: (0, 0), memory_space=pltpu.SMEM),
                  pl.BlockSpec((tile_rows, LANES), lambda i: (jnp.minimum(i, n_full - 1), 0)),
                  pl.BlockSpec((tile_rows, LANES), lambda i: (0, 0))],
        out_specs=pl.BlockSpec(memory_space=pl.ANY),
        scratch_shapes=[pltpu.VMEM((ZERO_ROWS * ROW_TILE_SUBLANES, LANES), I32),
                        pltpu.SemaphoreType.DMA, pltpu.SemaphoreType.DMA],
        compiler_params=_cparams(("arbitrary",)),
        name="dispatch",
    )(dest, etab, hp_a, hp_b)


def _rows_from_tiles(ref, lo, n):
    return jnp.concatenate([ref[pl.ds(lo * ROW_TILE_SUBLANES + s, n, stride=ROW_TILE_SUBLANES), :]
                            for s in range(ROW_TILE_SUBLANES)], axis=1)


def _ffn_body(be_ref, nu_ref, xs_ref, wg_ref, wu_ref, wd_ref, ys_ref, wgb, wub, wdb):
    j = pl.program_id(0)

    @pl.when(j < nu_ref[0])
    def _():
        @pl.when((j == 0) | (be_ref[j] != be_ref[jnp.maximum(j - 1, 0)]))
        def _():
            wgb[...] = wg_ref[0].astype(BF16)
            wub[...] = wu_ref[0].astype(BF16)
            wdb[...] = wd_ref[0].astype(BF16)

        x = _unpack_pairs(_rows_from_tiles(xs_ref, 0, EXPERT_BLOCK)).astype(BF16)
        act = _silu(_dot(x, wgb[...])) * _dot(x, wub[...])
        y = _dot(act.astype(BF16), wdb[...])
        packed = _pack_pairs(y)
        for s in range(ROW_TILE_SUBLANES):
            ys_ref[pl.ds(s, EXPERT_BLOCK, stride=ROW_TILE_SUBLANES), :] = packed[:, s * LANES:(s + 1) * LANES]

    @pl.when(j >= nu_ref[0])
    def _():
        ys_ref[...] = jnp.zeros_like(ys_ref)


def _ffn_call(blk_e, n_used, xs, w_gate, w_up, w_down, n_blocks):
    tile_blk = pl.BlockSpec((EXPERT_BLOCK * ROW_TILE_SUBLANES, LANES), lambda j, be, nu: (j, 0))
    last = lambda j, nu: jnp.minimum(j, nu[0] - 1)
    grid_spec = pltpu.PrefetchScalarGridSpec(
        num_scalar_prefetch=2,
        grid=(n_blocks,),
        in_specs=[pl.BlockSpec((EXPERT_BLOCK * ROW_TILE_SUBLANES, LANES), lambda j, be, nu: (last(j, nu), 0)),
                  pl.BlockSpec((1, D_MODEL, D_EXPERT), lambda j, be, nu: (be[last(j, nu)], 0, 0)),
                  pl.BlockSpec((1, D_MODEL, D_EXPERT), lambda j, be, nu: (be[last(j, nu)], 0, 0)),
                  pl.BlockSpec((1, D_EXPERT, D_MODEL), lambda j, be, nu: (be[last(j, nu)], 0, 0))],
        out_specs=tile_blk,
        scratch_shapes=[pltpu.VMEM((D_MODEL, D_EXPERT), BF16), pltpu.VMEM((D_MODEL, D_EXPERT), BF16),
                        pltpu.VMEM((D_EXPERT, D_MODEL), BF16)])
    return pl.pallas_call(
        _ffn_body,
        out_shape=jax.ShapeDtypeStruct((n_blocks * EXPERT_BLOCK * ROW_TILE_SUBLANES, LANES), I32),
        grid_spec=grid_spec,
        compiler_params=_cparams(("arbitrary",)),
        name="ffn",
    )(blk_e, n_used, xs, w_gate, w_up, w_down)


def _combine_body(dest_ref, dnext_ref, w8_ref, xa_ref, xb_ref, sg_ref, su_ref, sd_ref, ys_hbm, y_ref, buf, sem,
                  *, n_full):
    i = pl.program_id(0)
    T = MOE_TILE

    def gather(d_ref, slot):
        def issue(t, carry):
            for k in range(TOP_K):
                pltpu.make_async_copy(_tile_rows(ys_hbm, d_ref[k * T + t], 1), _tile_rows(buf.at[slot], k * T + t, 1),
                                      sem.at[slot]).start(priority=k % 2)
            return carry

        lax.fori_loop(0, T, issue, 0, unroll=2)

    def step(slot):
        @pl.when(i + 1 < pl.num_programs(0))
        def _():
            gather(dnext_ref, 1 - slot)

        x = _unpack_pairs(jnp.where(i < n_full, _rows_from_tiles(xa_ref, 0, T),
                                    _rows_from_tiles(xb_ref, 0, T))).astype(BF16)
        y = _dot((_silu(_dot(x, sg_ref[...])) * _dot(x, su_ref[...])).astype(BF16), sd_ref[...])
        w_t = jnp.concatenate([w8_ref[...], jnp.zeros((LANES - TOP_K, T), F32)], axis=0).T
        pltpu.make_async_copy(_tile_rows(ys_hbm, 0, T * TOP_K), buf.at[slot], sem.at[slot]).wait()
        for k in range(TOP_K):
            y = y + w_t[:, k:k + 1] * _unpack_pairs(_rows_from_tiles(buf.at[slot], k * T, T))
        y_ref[...] = y

    @pl.when(i == 0)
    def _():
        gather(dest_ref, 0)

    @pl.when(i % 2 == 0)
    def _():
        step(0)

    @pl.when(i % 2 == 1)
    def _():
        step(1)


def _combine_call(dest, w8, hp_a, hp_b, wsg, wsu, wsd, ys):
    T = MOE_TILE
    tile_rows = T * ROW_TILE_SUBLANES
    n_full = hp_a.shape[0] // tile_rows
    n_tiles = n_full + 1
    const = lambda shp: pl.BlockSpec(shp, lambda i: (0, 0))
    return pl.pallas_call(
        functools.partial(_combine_body, n_full=n_full),
        out_shape=jax.ShapeDtypeStruct((n_tiles * T, D_MODEL), F32),
        grid=(n_tiles,),
        in_specs=[pl.BlockSpec((TOP_K * T,), lambda i: (i,), memory_space=pltpu.SMEM),
                  pl.BlockSpec((TOP_K * T,), lambda i: (jnp.minimum(i + 1, n_tiles - 1),), memory_space=pltpu.SMEM),
                  pl.BlockSpec((TOP_K, T), lambda i: (0, i)),
                  pl.BlockSpec((tile_rows, LANES), lambda i: (jnp.minimum(i, n_full - 1), 0)),
                  pl.BlockSpec((tile_rows, LANES), lambda i: (0, 0)),
                  const((D_MODEL, D_EXPERT)), const((D_MODEL, D_EXPERT)), const((D_EXPERT, D_MODEL)),
                  pl.BlockSpec(memory_space=pl.ANY)],
        out_specs=pl.BlockSpec((T, D_MODEL), lambda i: (i, 0)),
        scratch_shapes=[pltpu.VMEM((2, TOP_K * tile_rows, LANES), I32), pltpu.SemaphoreType.DMA((2,))],
        compiler_params=_cparams(("arbitrary",)),
        name="combine",
    )(dest, dest, w8, hp_a, hp_b, wsg, wsu, wsd, ys)


def _final_body(x1_ref, y_ref, g2_ref, n_ref, o_ref):
    o_ref[0] = x1_ref[0] + g2_ref[0] * _rms(y_ref[...], n_ref[...])


def _final_call(x1, y_all, row0, gate2, gain, tm, mod_per_row):
    nb, t, _ = x1.shape
    nt = t // tm
    blk0 = row0 // tm
    rowblk = pl.BlockSpec((1, tm, D_MODEL), lambda b, i: (b, i, 0))
    mod_spec = rowblk if mod_per_row else pl.BlockSpec((1, 1, D_MODEL), lambda b, i: (b, 0, 0))
    return pl.pallas_call(
        _final_body,
        out_shape=jax.ShapeDtypeStruct((nb, t, D_MODEL), F32),
        grid=(nb, nt),
        in_specs=[rowblk, pl.BlockSpec((tm, D_MODEL), lambda b, i: (blk0 + b * nt + i, 0)), mod_spec,
                  pl.BlockSpec((1, D_MODEL), lambda b, i: (0, 0))],
        out_specs=rowblk,
        compiler_params=_cparams(("arbitrary", "arbitrary")),
        name="final",
    )(x1, y_all, gate2, gain.reshape(1, -1))


def _rope_tables(pos):
    half = HEAD_DIM_A // 2
    inv_freq = ROPE_THETA ** (-jnp.arange(half, dtype=F32) / half)
    ang = pos.astype(F32)[:, None] * inv_freq[None, :]
    cos = jnp.cos(ang)
    sin = jnp.sin(ang)
    reps = LANES // HEAD_DIM_A
    cos_t = jnp.tile(jnp.concatenate([cos, cos], axis=1), (1, reps))
    sin_t = jnp.tile(jnp.concatenate([-sin, sin], axis=1), (1, reps))
    return cos_t, sin_t


def _cache_from_tail(tail, keep):
    outs = []
    n, rows, _ = tail.shape
    for gi, kp in enumerate(keep):
        k = tail[:, rows - kp:, gi * D_GROUP_A:(gi + 1) * D_GROUP_A]
        v = tail[:, rows - kp:, D_A + gi * D_GROUP_A:D_A + (gi + 1) * D_GROUP_A]
        outs.append(jnp.stack([k, v], axis=2).reshape(n, kp, 2, N_HEADS_A, HEAD_DIM_A))
    return outs


def kernel(x_prompt, x_sample, c_prompt, c_sample, cache_a1_kv, cache_a2_kv, cache_a3_kv, state_b_wkv, state_b_shift, w_ada, b_ada, norm_pre_mix, norm_post_mix, norm_pre_ffn, norm_post_ffn, w_in, w_a_out, mu_b, w0_b, w_w2_b, a0_b, w_a2_b, w_g2_b, k_k_b, k_a_b, r_k_b, ln_x_w_b, ln_x_b_b, w_b_out, w_out, w_router, router_bias, w_e_gate, w_e_up, w_e_down, w_s_gate, w_s_up, w_s_down):
    assert DEPTH == 1
    l = 0
    nd = DEC_BATCH
    row = lambda a: a.reshape(1, -1)
    p = {'mu_b': row(mu_b[l]), 'w0_b': row(w0_b[l]), 'w_w2_b': w_w2_b[l], 'a0_b': row(a0_b[l]),
         'w_a2_b': w_a2_b[l], 'w_g2_b': w_g2_b[l], 'k_k_b': row(k_k_b[l]), 'k_a_b': row(k_a_b[l]),
         'r_k_b': row(r_k_b[l]), 'ln_x_w_b': row(ln_x_w_b[l]), 'ln_x_b_b': row(ln_x_b_b[l]),
         'norm_post_mix': norm_post_mix[l], 'norm_pre_ffn': norm_pre_ffn[l]}

    wq = w_in[l][:, :D_QKV].astype(BF16)
    wf = w_in[l][:, D_QKV:D_QKV + D_SHIFT_B].astype(BF16)
    wg = w_in[l][:, D_QKV + D_SHIFT_B:].astype(BF16)
    wa = w_a_out[l].astype(BF16)
    wb = w_b_out[l].astype(BF16)
    wo = w_out[l].astype(BF16)
    wrt = jnp.concatenate([w_router[l].T, jnp.zeros((LANES - N_EXPERTS, D_MODEL), F32)], axis=0)
    rb = router_bias[l].reshape(N_EXPERTS, 1)
    wsg, wsu, wsd = w_s_gate[l].astype(BF16), w_s_up[l].astype(BF16), w_s_down[l].astype(BF16)

    n_c = BATCH + nd
    c_all = jnp.concatenate([c_prompt, c_sample, jnp.zeros((-n_c % 8, D_MODEL), F32)], axis=0)
    mod = _mod_call(c_all, w_ada[l], b_ada[l])
    mod_p = [m.reshape(BATCH, 1, D_MODEL) for m in jnp.split(mod[:BATCH], 6, axis=-1)]
    mod_s = [m.reshape(1, nd, D_MODEL) for m in jnp.split(mod[BATCH:n_c], 6, axis=-1)]

    cos_p, sin_p = _rope_tables(jnp.arange(SEQ, dtype=I32))
    cos_s, sin_s = _rope_tables(jnp.full((nd,), PAST_LEN, I32))

    keep_p = [min(w, SEQ) for w, _ in DILATED_GROUPS]
    tail_rows = max(keep_p)
    dils = tuple(d for _, d in DILATED_GROUPS)

    q0, q1, q2, feat_p, gates_p, tail_p = _inproj_call(
        x_prompt, norm_pre_mix[l], mod_p[1], mod_p[0], cos_p, sin_p, wq, wf, wg,
        tm=256, tail_rows=tail_rows, mod_per_row=False, dils=dils)
    o_parts, lse_parts = [], []
    for gi, qg in enumerate((q0, q1, q2)):
        o, lse = _attn_call(qg, gi)
        o_parts.append(o)
        lse_parts.append(lse)
    ob_p, wkv_p = _wkv_call(feat_p, p)
    x1_p, hp_p, wt_p = _post_call(o_parts, lse_parts, ob_p, gates_p, x_prompt, mod_p[2], mod_p[4], mod_p[3],
                                  p, wa, wb, wo, wrt, rb, tm=512, mod_per_row=False)

    xs3 = x_sample.reshape(1, nd, D_MODEL)
    s0, s1, s2, feat_s, gates_s, tail_s = _inproj_call(
        xs3, norm_pre_mix[l], mod_s[1], mod_s[0], cos_s, sin_s, wq, wf, wg,
        tm=nd, tail_rows=nd, mod_per_row=True, dils=(1, 1, 1))
    qkv_s = jnp.stack([z.reshape(nd, 3, N_HEADS_A, HEAD_DIM_A) for z in (s0, s1, s2)], axis=2)
    qkv_s = qkv_s.reshape(nd, 3 * N_GROUPS_A, N_HEADS_A, HEAD_DIM_A, 1).astype(F32)
    oa_s = _sattn_call(qkv_s, cache_a1_kv[l], cache_a2_kv[l], cache_a3_kv[l])
    r_s, w_s, k_s, v_s, aa_s, bb_s, g_s = _swkv_prep_call(feat_s[0], state_b_shift[l], p)
    nh = nd * N_HEADS_B
    as_row = lambda a: a.reshape(nh, 1, HEAD_DIM_B)
    s_new, y_col = _swkv_step_call(state_b_wkv[l].reshape(nh, HEAD_DIM_B, HEAD_DIM_B), as_row(aa_s), as_row(w_s),
                                   as_row(bb_s), as_row(k_s), as_row(r_s), v_s.reshape(nh, HEAD_DIM_B, 1))
    ob_s = _swkv_fin_call(y_col.reshape(nd, D_B), r_s, k_s, v_s, g_s, p)
    x1_s, hp_s, wt_s = _post_call([oa_s.reshape(1, nd, D_GROUP_A)], None, ob_s.reshape(1, nd, D_B), gates_s, xs3,
                                  mod_s[2], mod_s[4], mod_s[3], p, wa, wb, wo, wrt, rb, tm=nd, mod_per_row=True)

    n_p = BATCH * SEQ
    n_real = n_p + nd
    n_all = -(-n_real // MOE_TILE) * MOE_TILE
    pad = n_all - n_real
    n_blocks = -(-(n_real * TOP_K) // EXPERT_BLOCK) + N_EXPERTS
    n_blocks_pad = -(-n_blocks // LANES) * LANES
    assert n_p % MOE_TILE == 0 and nd <= MOE_TILE
    hp_a = hp_p.reshape(n_p * ROW_TILE_SUBLANES, LANES)
    hp_b = jnp.concatenate([hp_s[0], jnp.zeros((pad * ROW_TILE_SUBLANES, LANES), I32)], axis=0)
    wt_all = jnp.concatenate([wt_p, wt_s, jnp.full((N_EXPERTS, pad), -1.0, F32)], axis=1)
    dest8, w8, tab, etab = _rank_call(wt_all, n_real, n_blocks, n_blocks_pad)
    dest = dest8.reshape(TOP_K, n_all // MOE_TILE, MOE_TILE).transpose(1, 0, 2).reshape(-1)
    xs = _dispatch_call(dest, etab, hp_a, hp_b, n_real, n_blocks * EXPERT_BLOCK)
    ys = _ffn_call(tab[0], tab[1, :1], xs, w_e_gate[l], w_e_up[l], w_e_down[l], n_blocks)
    y_all = _combine_call(dest, w8, hp_a, hp_b, wsg, wsu, wsd, ys)
    y_prompt = _final_call(x1_p, y_all, 0, mod_p[5], norm_post_ffn[l], tm=512, mod_per_row=False)
    y_sample = _final_call(x1_s, y_all, n_p, mod_s[5], norm_post_ffn[l], tm=nd, mod_per_row=True)

    a_p = [z[None] for z in _cache_from_tail(tail_p, keep_p)]
    a_s = [z.reshape(1, nd, DEC_SEQ, 2, N_HEADS_A, HEAD_DIM_A)
           for z in _cache_from_tail(tail_s.reshape(nd, 1, 2 * D_A), [DEC_SEQ] * N_GROUPS_A)]
    shift_p = feat_p[:, -1][None]
    shift_s = feat_s[0][None]
    return (y_prompt, y_sample.reshape(nd, DEC_SEQ, D_MODEL), a_p[0], a_p[1], a_p[2], wkv_p[None], shift_p,
            a_s[0], a_s[1], a_s[2], s_new.reshape(1, nd, N_HEADS_B, HEAD_DIM_B, HEAD_DIM_B), shift_s)
```

```python
import functools
import math

import jax
import jax.numpy as jnp
from jax import lax
from jax.experimental import pallas as pl
from jax.experimental.pallas import tpu as pltpu

F32 = jnp.float32
BF16 = jnp.bfloat16
I32 = jnp.int32

D_MODEL = 1024
BATCH = 2
SEQ = 8192
DEPTH = 1
DEC_BATCH = 32
DEC_SEQ = 1
PAST_LEN = 16384

HEAD_DIM_A = 64
N_HEADS_A = 8
DILATED_GROUPS = ((128, 1), (512, 4), (2048, 16))
N_GROUPS_A = 3
D_GROUP_A = N_HEADS_A * HEAD_DIM_A
D_A = N_GROUPS_A * D_GROUP_A
D_QKV = 3 * D_A
BAND_BLOCK = 128
ROPE_THETA = 10000.0

HEAD_DIM_B = 64
N_HEADS_B = 16
D_B = 1024
DECAY_LORA = 64
AAA_LORA = 64
GATE_LORA = 160
D_SHIFT_B = 3 * D_B + DECAY_LORA + AAA_LORA + GATE_LORA
LN_X_EPS = 64e-5

N_EXPERTS = 64
TOP_K = 8
N_EXPERT_GROUPS = 8
TOPK_GROUPS = 4
D_EXPERT = 256
ROUTED_SCALE = 2.5
EXPERT_BLOCK = 512
NORM_EPS = 1e-6

LANES = 128
WKV_CHUNK = 64
MOE_TILE = 512
COMBINE_TILE = 256
COMBINE_ROWS = 32
VMEM_LIMIT = 56 * 1024 * 1024
ROW_TILE_SUBLANES = D_MODEL // (2 * LANES)
ZERO_ROWS = 256


def _cparams(sem):
    return pltpu.CompilerParams(dimension_semantics=sem, vmem_limit_bytes=VMEM_LIMIT)


def _dot(a, b):
    return jnp.dot(a, b, preferred_element_type=F32)


def _dot_nt(a, b):
    return lax.dot_general(a, b, (((1,), (1,)), ((), ())), preferred_element_type=F32)


def _dot_tn(a, b):
    return lax.dot_general(a, b, (((0,), (0,)), ((), ())), preferred_element_type=F32)


def _dot_exact(a, b):
    return lax.dot_general(a, b, (((1,), (0,)), ((), ())), precision=lax.Precision.HIGHEST,
                           preferred_element_type=F32)


def _rms(x, gain):
    return x * lax.rsqrt(jnp.mean(x * x, axis=-1, keepdims=True) + NORM_EPS) * gain


def _sigmoid(x):
    return 1.0 / (1.0 + jnp.exp(-x))


def _silu(x):
    return x * _sigmoid(x)


def _softplus(x):
    return jnp.maximum(x, 0.0) + jnp.log(1.0 + jnp.exp(-jnp.abs(x)))


def _pack_pairs(x):
    half = D_MODEL // 2
    lo = lax.bitcast_convert_type(x[:, :half].astype(BF16).astype(F32), I32)
    hi = lax.bitcast_convert_type(x[:, half:].astype(BF16).astype(F32), I32)
    return lax.shift_right_logical(lo, 16) | (hi & jnp.int32(-65536))


def _unpack_pairs(w):
    lo = lax.bitcast_convert_type(w << 16, F32)
    hi = lax.bitcast_convert_type(w & jnp.int32(-65536), F32)
    return jnp.concatenate([lo, hi], axis=1)


def _mod_body(c_ref, w_ref, b_ref, o_ref):
    s = _silu(c_ref[...]).astype(BF16)
    o_ref[...] = _dot(s, w_ref[...].astype(BF16)) + b_ref[...]


def _mod_call(c_all, w_ada, b_ada):
    rows = c_all.shape[0]
    tn = 1536
    return pl.pallas_call(
        _mod_body,
        out_shape=jax.ShapeDtypeStruct((rows, 6 * D_MODEL), F32),
        grid=(6 * D_MODEL // tn,),
        in_specs=[pl.BlockSpec((rows, D_MODEL), lambda j: (0, 0)),
                  pl.BlockSpec((D_MODEL, tn), lambda j: (0, j)),
                  pl.BlockSpec((1, tn), lambda j: (0, j))],
        out_specs=pl.BlockSpec((rows, tn), lambda j: (0, j)),
        compiler_params=_cparams(("arbitrary",)),
        name="mod",
    )(c_all, w_ada, b_ada.reshape(1, -1))


def _inproj_body(x_ref, g_ref, sc_ref, sh_ref, cos_ref, sin_ref, wq_ref, wf_ref, wg_ref,
                 q0_ref, q1_ref, q2_ref, feat_ref, gate_ref, tail_ref, p_ref, *, dils):
    x = x_ref[0]
    tm = x.shape[0]
    h = _rms(x, g_ref[...]) * (1.0 + sc_ref[0]) + sh_ref[0]
    hb = h.astype(BF16)
    p = _dot(hb, wq_ref[...])
    cos = cos_ref[...]
    sin = sin_ref[...]
    lane = lax.broadcasted_iota(I32, cos.shape, 1)
    first_half = (lane % HEAD_DIM_A) < (HEAD_DIM_A // 2)
    for c in range(2 * D_A // LANES):
        xc = p[:, c * LANES:(c + 1) * LANES]
        partner = jnp.where(first_half, pltpu.roll(xc, LANES - HEAD_DIM_A // 2, 1),
                            pltpu.roll(xc, HEAD_DIM_A // 2, 1))
        rc = xc * cos + partner * sin
        if c < D_A // LANES:
            rc = rc * (HEAD_DIM_A ** -0.5)
        p_ref[c] = rc
        if c >= D_A // LANES:
            tail_ref[0, :, (c - D_A // LANES) * LANES:(c - D_A // LANES + 1) * LANES] = rc
    for c in range(2 * D_A // LANES, D_QKV // LANES):
        p_ref[c] = p[:, c * LANES:(c + 1) * LANES]
    tail_ref[0, :, D_A:] = p[:, 2 * D_A:]
    per_group = D_GROUP_A // LANES
    for gi, (out_ref, dil) in enumerate(zip((q0_ref, q1_ref, q2_ref), dils)):
        for which in range(3):
            for j in range(per_group):
                c = (which * D_A + gi * D_GROUP_A) // LANES + j
                dst = slice(which * D_GROUP_A + j * LANES, which * D_GROUP_A + (j + 1) * LANES)
                if dil == 1:
                    out_ref[0, 0, :, dst] = p_ref[c].astype(BF16)
                else:
                    for r in range(dil):
                        out_ref[0, r, :, dst] = p_ref[c, pl.ds(r, tm // dil, stride=dil), :].astype(BF16)
    feat_ref[0] = _dot(hb, wf_ref[...])
    gate_ref[0] = _sigmoid(_dot(hb, wg_ref[...])).astype(BF16)


def _inproj_call(x, gain, scale, shift, cos_t, sin_t, wq, wf, wg, tm, tail_rows, mod_per_row, dils):
    nb, t, _ = x.shape
    nt = t // tm
    tail_first = (t - tail_rows) // tm
    if mod_per_row:
        mod_spec = pl.BlockSpec((1, tm, D_MODEL), lambda b, i: (b, i, 0))
    else:
        mod_spec = pl.BlockSpec((1, 1, D_MODEL), lambda b, i: (b, 0, 0))
    resident = lambda shp: pl.BlockSpec(shp, lambda b, i: (0, 0), pipeline_mode=pl.Buffered(1))
    q_shapes = tuple(jax.ShapeDtypeStruct((nb, d, t // d, 3 * D_GROUP_A), BF16) for d in dils)
    q_specs = tuple(pl.BlockSpec((1, d, tm // d, 3 * D_GROUP_A), lambda b, i: (b, 0, i, 0)) for d in dils)
    return pl.pallas_call(
        functools.partial(_inproj_body, dils=dils),
        out_shape=q_shapes + (jax.ShapeDtypeStruct((nb, t, D_SHIFT_B), F32),
                              jax.ShapeDtypeStruct((nb, t, 2 * D_MODEL), BF16),
                              jax.ShapeDtypeStruct((nb, tail_rows, 2 * D_A), F32)),
        grid=(nb, nt),
        in_specs=[pl.BlockSpec((1, tm, D_MODEL), lambda b, i: (b, i, 0)),
                  pl.BlockSpec((1, D_MODEL), lambda b, i: (0, 0)),
                  mod_spec, mod_spec,
                  pl.BlockSpec((tm, LANES), lambda b, i: (i, 0)),
                  pl.BlockSpec((tm, LANES), lambda b, i: (i, 0)),
                  resident((D_MODEL, D_QKV)), resident((D_MODEL, D_SHIFT_B)),
                  resident((D_MODEL, 2 * D_MODEL))],
        out_specs=q_specs + (pl.BlockSpec((1, tm, D_SHIFT_B), lambda b, i: (b, i, 0)),
                             pl.BlockSpec((1, tm, 2 * D_MODEL), lambda b, i: (b, i, 0)),
                             pl.BlockSpec((1, tm, 2 * D_A), lambda b, i: (b, jnp.maximum(i - tail_first, 0), 0))),
        scratch_shapes=[pltpu.VMEM((D_QKV // LANES, tm, LANES), F32)],
        compiler_params=_cparams(("arbitrary", "arbitrary")),
        name="inproj",
    )(x, gain.reshape(1, -1), scale, shift, cos_t, sin_t, wq, wf, wg)


def _attn_body(q_ref, kc_ref, kp_ref, vc_ref, vp_ref, o_ref, lse_ref):
    mb = pl.program_id(2)
    nq = q_ref.shape[2] // BAND_BLOCK
    q = q_ref[0, 0]
    k = jnp.concatenate([kp_ref[0, 0], kc_ref[0, 0]], axis=0)
    v = jnp.concatenate([vp_ref[0, 0], vc_ref[0, 0]], axis=0)
    qi = lax.broadcasted_iota(I32, (BAND_BLOCK, 2 * BAND_BLOCK), 0)
    ki = lax.broadcasted_iota(I32, (BAND_BLOCK, 2 * BAND_BLOCK), 1)
    dist = qi + BAND_BLOCK - ki
    band = (dist >= 0) & (dist <= BAND_BLOCK)
    masks = [band & ((ki >= BAND_BLOCK) | (mb > 0))] + [band] * (nq - 1)
    lane_q = lax.broadcasted_iota(I32, (BAND_BLOCK, LANES), 1)
    lane_k = lax.broadcasted_iota(I32, (2 * BAND_BLOCK, LANES), 1)
    for hp in range(N_HEADS_A // 2):
        sl = slice(hp * LANES, (hp + 1) * LANES)
        chains = [(j, sub) for j in range(nq) for sub in range(2)]
        qs = [q[j * BAND_BLOCK:(j + 1) * BAND_BLOCK, sl] for j in range(nq)]
        ks = [k[j * BAND_BLOCK:(j + 2) * BAND_BLOCK, sl] for j in range(nq)]
        vs = [v[j * BAND_BLOCK:(j + 2) * BAND_BLOCK, sl] for j in range(nq)]
        mqs = [lane_q < HEAD_DIM_A, lane_q >= HEAD_DIM_A]
        mks = [lane_k < HEAD_DIM_A, lane_k >= HEAD_DIM_A]
        s = [jnp.where(masks[j], _dot_nt(jnp.where(mqs[sub], qs[j], jnp.zeros_like(qs[j])), ks[j]), -jnp.inf)
             for j, sub in chains]
        mx = [jnp.max(z, axis=1, keepdims=True) for z in s]
        p = [jnp.exp(z - m) for z, m in zip(s, mx)]
        l = [jnp.sum(z, axis=1, keepdims=True) for z in p]
        pv = [_dot(p[c].astype(BF16), jnp.where(mks[sub], vs[j], jnp.zeros_like(vs[j])))
              for c, (j, sub) in enumerate(chains)]
        for j in range(nq):
            c0, c1 = 2 * j, 2 * j + 1
            o_pair = pv[c0] / l[c0] + pv[c1] / l[c1]
            lse_pair = jnp.where(mqs[0], mx[c0] + jnp.log(l[c0]), mx[c1] + jnp.log(l[c1]))
            o_ref[0, 0, j * BAND_BLOCK:(j + 1) * BAND_BLOCK, sl] = o_pair.astype(BF16)
            lse_ref[0, 0, j * BAND_BLOCK:(j + 1) * BAND_BLOCK, sl] = lse_pair


def _attn_call(qkv_g, gi):
    b, dil, l, _ = qkv_g.shape
    nq = 4
    nb = l // (nq * BAND_BLOCK)
    blk = (1, 1, nq * BAND_BLOCK, D_GROUP_A)
    cur = lambda which: pl.BlockSpec(blk, lambda bb, r, m: (bb, r, m, which))
    prev = lambda which: pl.BlockSpec((1, 1, BAND_BLOCK, D_GROUP_A),
                                      lambda bb, r, m: (bb, r, jnp.maximum(nq * m - 1, 0), which))
    return pl.pallas_call(
        _attn_body,
        out_shape=(jax.ShapeDtypeStruct((b, dil, l, D_GROUP_A), BF16),
                   jax.ShapeDtypeStruct((b, dil, l, D_GROUP_A), F32)),
        grid=(b, dil, nb),
        in_specs=[cur(0), cur(1), prev(1), cur(2), prev(2)],
        out_specs=(pl.BlockSpec(blk, lambda bb, r, m: (bb, r, m, 0)),
                   pl.BlockSpec(blk, lambda bb, r, m: (bb, r, m, 0))),
        compiler_params=_cparams(("arbitrary", "arbitrary", "arbitrary")),
        name=f"attn{gi}",
    )(qkv_g, qkv_g, qkv_g, qkv_g, qkv_g)


def _sattn_body(qkv_ref, b1_ref, b2_ref, b3_ref, o_ref):
    outs, lses = [], []
    for g, (buf_ref, (_, dil)) in enumerate(zip((b1_ref, b2_ref, b3_ref), DILATED_GROUPS)):
        q = qkv_ref[0, g]
        kn = qkv_ref[0, N_GROUPS_A + g]
        vn = qkv_ref[0, 2 * N_GROUPS_A + g]
        kb = buf_ref[0, 0]
        vb = buf_ref[0, 1]
        wb = kb.shape[-1]
        pos = lax.broadcasted_iota(I32, (1, 1, wb), 2)
        s = jnp.sum(kb * q, axis=1, keepdims=True)
        s = jnp.where(pos % dil == 0, s, -jnp.inf)
        sn = jnp.sum(kn * q, axis=1, keepdims=True)
        m = jnp.maximum(jnp.max(s, axis=2, keepdims=True), sn)
        p = jnp.exp(s - m)
        pn = jnp.exp(sn - m)
        l = jnp.sum(p, axis=2, keepdims=True) + pn
        outs.append((jnp.sum(p * vb, axis=2, keepdims=True) + pn * vn) / l)
        lses.append(m + jnp.log(l))
    mx = jnp.maximum(jnp.maximum(lses[0], lses[1]), lses[2])
    es = [jnp.exp(z - mx) for z in lses]
    o_ref[0] = (es[0] * outs[0] + es[1] * outs[1] + es[2] * outs[2]) / (es[0] + es[1] + es[2])


def _sattn_call(qkv_s, c1, c2, c3):
    n = qkv_s.shape[0]
    views, specs = [], []
    for c in (c1, c2, c3):
        wb = c.shape[1]
        views.append(jnp.transpose(c, (0, 2, 3, 4, 1)))
        specs.append(pl.BlockSpec((1, 2, N_HEADS_A, HEAD_DIM_A, wb), lambda b: (b, 0, 0, 0, 0)))
    return pl.pallas_call(
        _sattn_body,
        out_shape=jax.ShapeDtypeStruct((n, N_HEADS_A, HEAD_DIM_A, 1), F32),
        grid=(n,),
        in_specs=[pl.BlockSpec((1, 3 * N_GROUPS_A, N_HEADS_A, HEAD_DIM_A, 1), lambda b: (b, 0, 0, 0, 0))] + specs,
        out_specs=pl.BlockSpec((1, N_HEADS_A, HEAD_DIM_A, 1), lambda b: (b, 0, 0, 0)),
        compiler_params=_cparams(("arbitrary",)),
        name="sattn",
    )(qkv_s, *views)


def _rwkv_features(xs, w0, ww2, a0, wa2, wg2, k_a):
    r = xs[:, :D_B]
    k = xs[:, D_B:2 * D_B]
    v = xs[:, 2 * D_B:3 * D_B]
    xw = xs[:, 3 * D_B:3 * D_B + DECAY_LORA]
    xa = xs[:, 3 * D_B + DECAY_LORA:3 * D_B + DECAY_LORA + AAA_LORA]
    xg = xs[:, 3 * D_B + DECAY_LORA + AAA_LORA:]
    w_log = -_softplus(-(w0 + _dot(jnp.tanh(xw).astype(BF16), ww2.astype(BF16)))) - 0.5
    a = _sigmoid(a0 + _dot(xa.astype(BF16), wa2.astype(BF16)))
    g = _dot(_sigmoid(xg).astype(BF16), wg2.astype(BF16))
    k_h = k * (1.0 + (a - 1.0) * k_a)
    return r, k, v, w_log, a, g, k_h


def _head_norm(kk_h):
    nrm = jnp.sqrt(jnp.sum(kk_h * kk_h, axis=-1, keepdims=True))
    return kk_h / jnp.maximum(nrm, 1e-12)


def _wkv_finish_head(y, r_h, k_h, v_h, g_h, rk_h, lnw_h, lnb_h):
    mean = jnp.mean(y, axis=-1, keepdims=True)
    var = jnp.mean(jnp.square(y - mean), axis=-1, keepdims=True)
    yn = (y - mean) * lax.rsqrt(var + LN_X_EPS) * lnw_h + lnb_h
    bonus = jnp.sum(r_h * k_h * rk_h, axis=-1, keepdims=True) * v_h
    return (yn + bonus) * g_h


def _wkv_body(f_ref, fp_ref, mu_ref, w0_ref, ww2_ref, a0_ref, wa2_ref, wg2_ref, kk_ref, ka_ref,
              rk_ref, lnw_ref, lnb_ref, o_ref, st_ref, s_ref):
    c = pl.program_id(0)
    C = WKV_CHUNK
    nb = f_ref.shape[0]

    @pl.when(c == 0)
    def _():
        s_ref[...] = jnp.zeros_like(s_ref)

    f = jnp.concatenate([f_ref[b] for b in range(nb)], axis=0)
    row = lax.broadcasted_iota(I32, f.shape, 0)
    prev = pltpu.roll(f, 1, 0)
    for b in range(nb):
        prev = jnp.where(row == b * C, jnp.where(c == 0, 0.0, fp_ref[b][7:8, :]), prev)
    xs = f + mu_ref[...] * (prev - f)
    r, k, v, w_log, a, g, k_h = _rwkv_features(xs, w0_ref[...], ww2_ref[...], a0_ref[...],
                                               wa2_ref[...], wg2_ref[...], ka_ref[...])
    lw = -jnp.exp(w_log)
    kk = k * kk_ref[...]
    jh = lax.broadcasted_iota(I32, (D_B, LANES), 0) // HEAD_DIM_B
    ind = (jh == lax.broadcasted_iota(I32, (D_B, LANES), 1)).astype(BF16)
    ind_t = (lax.broadcasted_iota(I32, (LANES, D_B), 0)
             == lax.broadcasted_iota(I32, (LANES, D_B), 1) // HEAD_DIM_B).astype(BF16)

    def head_sum(z):
        hi = z.astype(BF16)
        lo = (z - hi.astype(F32)).astype(BF16)
        s = _dot(hi, ind) + _dot(lo, ind)
        shi = s.astype(BF16)
        slo = (s - shi.astype(F32)).astype(BF16)
        return _dot(shi, ind_t) + _dot(slo, ind_t)

    kkn = kk / jnp.maximum(jnp.sqrt(head_sum(kk * kk)), 1e-12)

    tr = lax.broadcasted_iota(I32, (nb * C, nb * C), 0)
    sr_ = lax.broadcasted_iota(I32, (nb * C, nb * C), 1)
    tri_incl = ((tr >= sr_) & (tr // C == sr_ // C)).astype(BF16)
    l1 = lw.astype(BF16)
    r1 = lw - l1.astype(F32)
    l2 = r1.astype(BF16)
    l3 = (r1 - l2.astype(F32)).astype(BF16)
    cum = _dot(tri_incl, l1) + _dot(tri_incl, l2) + _dot(tri_incl, l3)
    rhos = [cum[b * C + C // 2 - 1:b * C + C // 2, :] for b in range(nb)]
    rho = jnp.concatenate([jnp.broadcast_to(z, (C, D_B)) for z in rhos], axis=0)
    ep = jnp.exp(cum - rho)
    em = jnp.exp(rho - cum)
    e_a = ep * jnp.exp(-lw)
    r_hat = r * ep
    k_hat = k_h * em
    e_rs = [jnp.exp(z) for z in rhos]
    e_cs = [jnp.exp(cum[b * C + C - 1:b * C + C, :] - rhos[b]) for b in range(nb)]

    ti = lax.broadcasted_iota(I32, (C, C), 0)
    si = lax.broadcasted_iota(I32, (C, C), 1)
    strict = ti > si
    incl = ti >= si
    eye = (ti == si).astype(F32)
    rk = rk_ref[...]
    lnw = lnw_ref[...]
    lnb = lnb_ref[...]
    items = [(b, h) for b in range(nb) for h in range(N_HEADS_B)]
    heads = range(len(items))
    lanes = [slice(h * HEAD_DIM_B, (h + 1) * HEAD_DIM_B) for _, h in items]
    cut = lambda z, i: z[items[i][0] * C:(items[i][0] + 1) * C, lanes[i]]
    e_r = [e_rs[b][:, lanes[i]] for i, (b, _) in enumerate(items)]
    e_c = [e_cs[b][:, lanes[i]] for i, (b, _) in enumerate(items)]
    a_hat_full = (-kkn * e_a).astype(BF16)
    b_hat_full = (kkn * a * em).astype(BF16)
    a_hat_b = [cut(a_hat_full, h) for h in heads]
    b_hat_b = [cut(b_hat_full, h) for h in heads]
    rh = [cut(r_hat, h) for h in heads]
    vb = [cut(v, h).astype(BF16) for h in heads]
    bk = [jnp.concatenate([b_hat_b[h], cut(k_hat, h).astype(BF16)], axis=0) for h in heads]
    p = [_dot_nt(jnp.concatenate([a_hat_b[h], rh[h].astype(BF16)], axis=0), bk[h]) for h in heads]
    l_ab = [jnp.where(strict, z[:C, :C], 0.0) for z in p]
    l_ak = [jnp.where(strict, z[:C, C:], 0.0).astype(BF16) for z in p]
    p_rb = [jnp.where(incl, z[C:, :C], 0.0).astype(BF16) for z in p]
    p_rk = [jnp.where(incl, z[C:, C:], 0.0).astype(BF16) for z in p]
    xb = [z.astype(BF16) for z in l_ab]
    tinv = [eye + z for z in l_ab]
    for _ in range(int(math.log2(C)) - 1):
        xb = [_dot(z, z).astype(BF16) for z in xb]
        tinv = [tinv[h] + _dot(tinv[h].astype(BF16), xb[h]) for h in heads]
    tb = [z.astype(BF16) for z in tinv]
    lv = [_dot(l_ak[h], vb[h]).astype(BF16) for h in heads]
    a_bar = [_dot(tb[h], a_hat_b[h]).astype(BF16) for h in heads]
    u_v = [_dot(tb[h], lv[h]).astype(BF16) for h in heads]
    r_bar = [rh[h] + _dot(p_rb[h], a_bar[h]) for h in heads]
    y_v = [_dot(p_rb[h], u_v[h]) + _dot(p_rk[h], vb[h]) for h in heads]
    ab = [_dot_tn(a_bar[h], b_hat_b[h]).astype(BF16) for h in heads]
    n_t = [_dot_tn(jnp.concatenate([u_v[h], vb[h]], axis=0), bk[h]) for h in heads]
    s0 = [s_ref[b, h] for b, h in items]
    sr = [s0[h] * e_r[h] for h in heads]
    y = [_dot_nt((r_bar[h] * e_r[h]).astype(BF16), s0[h].astype(BF16)) + y_v[h] for h in heads]
    s_new = [(sr[h] + _dot(sr[h].astype(BF16), ab[h]) + n_t[h]) * e_c[h] for h in heads]
    for i, (b, h) in enumerate(items):
        s_ref[b, h] = s_new[i]
    y_full = jnp.concatenate([jnp.concatenate(y[b * N_HEADS_B:(b + 1) * N_HEADS_B], axis=1) for b in range(nb)],
                             axis=0)
    inv_hd = 1.0 / HEAD_DIM_B
    dev = y_full - head_sum(y_full) * inv_hd
    yn = dev * lax.rsqrt(head_sum(dev * dev) * inv_hd + LN_X_EPS) * lnw + lnb
    out = (yn + head_sum(r * k_h * rk) * v) * g
    for b in range(nb):
        o_ref[b] = out[b * C:(b + 1) * C, :]

    @pl.when(c == pl.num_programs(0) - 1)
    def _():
        st_ref[...] = s_ref[...]


def _wkv_call(feat, p):
    b, t, _ = feat.shape
    C = WKV_CHUNK
    nc = t // C
    row = lambda n: pl.BlockSpec((1, n), lambda c: (0, 0))
    mat = lambda m, n: pl.BlockSpec((m, n), lambda c: (0, 0))
    return pl.pallas_call(
        _wkv_body,
        out_shape=(jax.ShapeDtypeStruct((b, t, D_B), F32),
                   jax.ShapeDtypeStruct((b, N_HEADS_B, HEAD_DIM_B, HEAD_DIM_B), F32)),
        grid=(nc,),
        in_specs=[pl.BlockSpec((b, C, D_SHIFT_B), lambda c: (0, c, 0)),
                  pl.BlockSpec((b, 8, D_SHIFT_B), lambda c: (0, jnp.maximum(c * (C // 8) - 1, 0), 0)),
                  row(D_SHIFT_B), row(D_B), mat(DECAY_LORA, D_B), row(D_B), mat(AAA_LORA, D_B),
                  mat(GATE_LORA, D_B), row(D_B), row(D_B), row(D_B), row(D_B), row(D_B)],
        out_specs=(pl.BlockSpec((b, C, D_B), lambda c: (0, c, 0)),
                   pl.BlockSpec((b, N_HEADS_B, HEAD_DIM_B, HEAD_DIM_B), lambda c: (0, 0, 0, 0))),
        scratch_shapes=[pltpu.VMEM((b, N_HEADS_B, HEAD_DIM_B, HEAD_DIM_B), F32)],
        compiler_params=_cparams(("arbitrary",)),
        name="wkv",
    )(feat, feat, p['mu_b'], p['w0_b'], p['w_w2_b'], p['a0_b'], p['w_a2_b'], p['w_g2_b'],
      p['k_k_b'], p['k_a_b'], p['r_k_b'], p['ln_x_w_b'], p['ln_x_b_b'])


def _swkv_prep_body(f_ref, sh_ref, mu_ref, w0_ref, ww2_ref, a0_ref, wa2_ref, wg2_ref, kk_ref, ka_ref,
                    r_ref, w_ref, k_ref, v_ref, aa_ref, bb_ref, g_ref):
    f = f_ref[...]
    xs = f + mu_ref[...] * (sh_ref[...] - f)
    r, k, v, w_log, a, g, k_h = _rwkv_features(xs, w0_ref[...], ww2_ref[...], a0_ref[...],
                                               wa2_ref[...], wg2_ref[...], ka_ref[...])
    kk = k * kk_ref[...]
    kkn = jnp.concatenate([_head_norm(kk[:, h * HEAD_DIM_B:(h + 1) * HEAD_DIM_B]) for h in range(N_HEADS_B)],
                          axis=1)
    r_ref[...] = r
    w_ref[...] = jnp.exp(-jnp.exp(w_log))
    k_ref[...] = k_h
    v_ref[...] = v
    aa_ref[...] = -kkn
    bb_ref[...] = kkn * a
    g_ref[...] = g


def _swkv_prep_call(feat_s, shift0, p):
    n = feat_s.shape[0]
    full = lambda a: pl.BlockSpec(a.shape, lambda: tuple(0 for _ in a.shape))
    args = (feat_s, shift0, p['mu_b'], p['w0_b'], p['w_w2_b'], p['a0_b'], p['w_a2_b'], p['w_g2_b'],
            p['k_k_b'], p['k_a_b'])
    return pl.pallas_call(
        _swkv_prep_body,
        out_shape=tuple(jax.ShapeDtypeStruct((n, D_B), F32) for _ in range(7)),
        in_specs=[full(a) for a in args],
        out_specs=tuple(pl.BlockSpec((n, D_B), lambda: (0, 0)) for _ in range(7)),
        compiler_params=pltpu.CompilerParams(vmem_limit_bytes=VMEM_LIMIT),
        name="swkv_prep",
    )(*args)


def _swkv_step_body(s_ref, a_ref, w_ref, b_ref, k_ref, r_ref, v_ref, so_ref, y_ref):
    s = s_ref[...]
    sa = jnp.sum(s * a_ref[...], axis=-1, keepdims=True)
    s2 = s * w_ref[...] + sa * b_ref[...] + v_ref[...] * k_ref[...]
    so_ref[...] = s2
    y_ref[...] = jnp.sum(s2 * r_ref[...], axis=-1, keepdims=True)


def _swkv_step_call(s0, aa, w, bb, k, r, v_col):
    nh = s0.shape[0]
    th = 64
    rowspec = pl.BlockSpec((th, 1, HEAD_DIM_B), lambda i: (i, 0, 0))
    colspec = pl.BlockSpec((th, HEAD_DIM_B, 1), lambda i: (i, 0, 0))
    stspec = pl.BlockSpec((th, HEAD_DIM_B, HEAD_DIM_B), lambda i: (i, 0, 0))
    return pl.pallas_call(
        _swkv_step_body,
        out_shape=(jax.ShapeDtypeStruct((nh, HEAD_DIM_B, HEAD_DIM_B), F32),
                   jax.ShapeDtypeStruct((nh, HEAD_DIM_B, 1), F32)),
        grid=(nh // th,),
        in_specs=[stspec, rowspec, rowspec, rowspec, rowspec, rowspec, colspec],
        out_specs=(stspec, colspec),
        compiler_params=_cparams(("arbitrary",)),
        name="swkv_step",
    )(s0, aa, w, bb, k, r, v_col)


def _swkv_fin_body(y_ref, r_ref, k_ref, v_ref, g_ref, rk_ref, lnw_ref, lnb_ref, o_ref):
    y, r, k, v, g = y_ref[...], r_ref[...], k_ref[...], v_ref[...], g_ref[...]
    rk, lnw, lnb = rk_ref[...], lnw_ref[...], lnb_ref[...]
    outs = []
    for h in range(N_HEADS_B):
        sl = slice(h * HEAD_DIM_B, (h + 1) * HEAD_DIM_B)
        outs.append(_wkv_finish_head(y[:, sl], r[:, sl], k[:, sl], v[:, sl], g[:, sl],
                                     rk[:, sl], lnw[:, sl], lnb[:, sl]))
    o_ref[...] = jnp.concatenate(outs, axis=1)


def _swkv_fin_call(y, r, k, v, g, p):
    n = y.shape[0]
    args = (y, r, k, v, g, p['r_k_b'], p['ln_x_w_b'], p['ln_x_b_b'])
    full = lambda a: pl.BlockSpec(a.shape, lambda: (0, 0))
    return pl.pallas_call(
        _swkv_fin_body,
        out_shape=jax.ShapeDtypeStruct((n, D_B), F32),
        in_specs=[full(a) for a in args],
        out_specs=pl.BlockSpec((n, D_B), lambda: (0, 0)),
        name="swkv_fin",
    )(*args)


def _route_t(scores, bias_col):
    n = scores.shape[1]
    gsz = N_EXPERTS // N_EXPERT_GROUPS
    choice = scores + bias_col
    ninf = -jnp.inf
    sid = lax.broadcasted_iota(I32, (gsz, n), 0)
    gs = []
    for gidx in range(N_EXPERT_GROUPS):
        blk = choice[gidx * gsz:(gidx + 1) * gsz, :]
        m1 = jnp.max(blk, axis=0, keepdims=True)
        first = jnp.min(jnp.where(blk == m1, sid, gsz), axis=0, keepdims=True)
        m2 = jnp.max(jnp.where(sid == first, ninf, blk), axis=0, keepdims=True)
        gs.append(m1 + m2)
    cur = jnp.concatenate(gs, axis=0)
    gid = lax.broadcasted_iota(I32, (N_EXPERT_GROUPS, n), 0)
    gmask = jnp.zeros((N_EXPERT_GROUPS, n), F32)
    for _ in range(TOPK_GROUPS):
        m = jnp.max(cur, axis=0, keepdims=True)
        first = jnp.min(jnp.where(cur == m, gid, N_EXPERT_GROUPS), axis=0, keepdims=True)
        sel = gid == first
        gmask = jnp.where(sel, 1.0, gmask)
        cur = jnp.where(sel, ninf, cur)
    emask = jnp.concatenate([jnp.broadcast_to(gmask[gidx:gidx + 1, :], (gsz, n))
                             for gidx in range(N_EXPERT_GROUPS)], axis=0)
    cur = jnp.where(emask > 0.5, choice, ninf)
    eid = lax.broadcasted_iota(I32, (N_EXPERTS, n), 0)
    selm = jnp.zeros((N_EXPERTS, n), F32)
    for _ in range(TOP_K):
        m = jnp.max(cur, axis=0, keepdims=True)
        first = jnp.min(jnp.where(cur == m, eid, N_EXPERTS), axis=0, keepdims=True)
        sel = eid == first
        selm = jnp.where(sel, 1.0, selm)
        cur = jnp.where(sel, ninf, cur)
    w = jnp.where(selm > 0.5, scores, 0.0)
    w = w / jnp.sum(w, axis=0, keepdims=True) * ROUTED_SCALE
    return jnp.where(selm > 0.5, w, -1.0)


def _unpermute(blk_ref, scr_ref, dil, tm):
    if dil == 1:
        return blk_ref[0, 0].astype(F32)
    n_chunks = scr_ref.shape[0]
    for r in range(dil):
        rows = blk_ref[0, r].astype(F32)
        for j in range(n_chunks):
            scr_ref[j, pl.ds(r, tm // dil, stride=dil), :] = rows[:, j * LANES:(j + 1) * LANES]
    return jnp.concatenate([scr_ref[j] for j in range(n_chunks)], axis=1)


def _post_body(*refs, combine, dils):
    if combine:
        o_refs, l_refs, rest = refs[:3], refs[3:6], refs[6:]
    else:
        o_refs, rest = refs[:1], refs[1:]
    (ob_ref, gt_ref, x_ref, g1_ref, sc2_ref, sh2_ref, npost_ref, npre_ref, wa_ref, wb_ref, wo_ref,
     wrt_ref, rb_ref, x1_ref, hp_ref, wt_ref) = rest[:16]
    scr = rest[16:]
    tm = x_ref.shape[1]
    if combine:
        os_, ls_ = [], []
        si = 0
        for gi, dil in enumerate(dils):
            os_.append(_unpermute(o_refs[gi], scr[si] if dil > 1 else None, dil, tm))
            ls_.append(_unpermute(l_refs[gi], scr[si + 1] if dil > 1 else None, dil, tm))
            si += 2 if dil > 1 else 0
        mx = jnp.maximum(jnp.maximum(ls_[0], ls_[1]), ls_[2])
        es = [jnp.exp(z - mx) for z in ls_]
        o_a = (es[0] * os_[0] + es[1] * os_[1] + es[2] * os_[2]) / (es[0] + es[1] + es[2])
    else:
        o_a = o_refs[0][0]
    gt = gt_ref[0].astype(F32)
    za = _dot(o_a.astype(BF16), wa_ref[...])
    zb = _dot(ob_ref[0].astype(BF16), wb_ref[...])
    merged = gt[:, :D_MODEL] * za + gt[:, D_MODEL:] * zb
    z = _dot(merged.astype(BF16), wo_ref[...])
    x1 = x_ref[0] + g1_ref[0] * _rms(z, npost_ref[...])
    x1_ref[0] = x1
    h2 = _rms(x1, npre_ref[...]) * (1.0 + sc2_ref[0]) + sh2_ref[0]
    packed = _pack_pairs(h2)
    for s in range(ROW_TILE_SUBLANES):
        hp_ref[0, pl.ds(s, tm, stride=ROW_TILE_SUBLANES), :] = packed[:, s * LANES:(s + 1) * LANES]
    tp =-(-tm // LANES) * LANES
    if tp != tm:
        h2 = jnp.concatenate([h2, jnp.zeros((tp - tm, D_MODEL), F32)], axis=0)
    logits_t = lax.dot_general(wrt_ref[...], h2, (((1,), (1,)), ((), ())),
                               precision=lax.Precision.HIGHEST, preferred_element_type=F32)
    w = _route_t(_sigmoid(logits_t[:N_EXPERTS, :]), rb_ref[...])
    wt_ref[...] = w[:, :tm]


def _post_call(o_parts, lse_parts, ob, gates, x, gate1, scale2, shift2, p, wa, wb, wo, wrt, rb, tm, mod_per_row):
    nb, t, _ = x.shape
    nt = t // tm
    combine = lse_parts is not None
    rowblk = lambda width: pl.BlockSpec((1, tm, width), lambda b, i: (b, i, 0))
    if mod_per_row:
        mod_spec = rowblk(D_MODEL)
    else:
        mod_spec = pl.BlockSpec((1, 1, D_MODEL), lambda b, i: (b, 0, 0))
    const = lambda shp: pl.BlockSpec(shp, lambda b, i: (0, 0))
    scratch = []
    if combine:
        dils = tuple(o.shape[1] for o in o_parts)
        o_args = list(o_parts) + list(lse_parts)
        o_specs = [pl.BlockSpec((1, d, tm // d, D_GROUP_A), lambda b, i: (b, 0, i, 0)) for d in dils] * 2
        for d in dils:
            if d > 1:
                scratch += [pltpu.VMEM((D_GROUP_A // LANES, tm, LANES), F32)] * 2
    else:
        dils = ()
        o_args = [o_parts[0]]
        o_specs = [rowblk(D_GROUP_A)]
    return pl.pallas_call(
        functools.partial(_post_body, combine=combine, dils=dils),
        out_shape=(jax.ShapeDtypeStruct((nb, t, D_MODEL), F32),
                   jax.ShapeDtypeStruct((nb, t * ROW_TILE_SUBLANES, LANES), I32),
                   jax.ShapeDtypeStruct((N_EXPERTS, nb * t), F32)),
        grid=(nb, nt),
        in_specs=o_specs + [rowblk(D_B), rowblk(2 * D_MODEL), rowblk(D_MODEL),
                            mod_spec, mod_spec, mod_spec, const((1, D_MODEL)), const((1, D_MODEL)),
                            const((D_GROUP_A, D_MODEL)), const((D_B, D_MODEL)), const((D_MODEL, D_MODEL)),
                            const((LANES, D_MODEL)), const((N_EXPERTS, 1))],
        out_specs=(rowblk(D_MODEL),
                   pl.BlockSpec((1, tm * ROW_TILE_SUBLANES, LANES), lambda b, i: (b, i, 0)),
                   pl.BlockSpec((N_EXPERTS, tm), lambda b, i: (0, b * nt + i))),
        scratch_shapes=scratch,
        compiler_params=_cparams(("arbitrary", "arbitrary")),
        name="post",
    )(*o_args, ob, gates, x, gate1, scale2, shift2, p['norm_post_mix'].reshape(1, -1),
      p['norm_pre_ffn'].reshape(1, -1), wa, wb, wo, wrt, rb)


def _rank_body(w_ref, dest_ref, w8_ref, tab_ref, etab_ref, cnt_ref, pst_ref, run_ref, *, n_real, n_slots):
    ph = pl.program_id(0)
    i = pl.program_id(1)
    T = MOE_TILE
    w = w_ref[...]
    sel = (w >= 0.0).astype(F32)
    cnt_tile = jnp.broadcast_to(jnp.sum(sel, axis=1, keepdims=True), (N_EXPERTS, LANES))
    ei = lax.broadcasted_iota(I32, (N_EXPERTS, N_EXPERTS), 0)
    ej = lax.broadcasted_iota(I32, (N_EXPERTS, N_EXPERTS), 1)

    @pl.when((ph == 0) & (i == 0))
    def _():
        cnt_ref[...] = jnp.zeros_like(cnt_ref)

    @pl.when(ph == 0)
    def _():
        cnt_ref[...] += cnt_tile

    @pl.when((ph == 1) & (i == 0))
    def _():
        cnt = cnt_ref[...]
        padded = jnp.floor((cnt + (EXPERT_BLOCK - 1)) / EXPERT_BLOCK) * EXPERT_BLOCK
        pstart = _dot_exact((ej < ei).astype(F32), padded)
        pst_ref[...] = pstart
        run_ref[...] = jnp.zeros_like(run_ref)
        pend = pstart + padded
        vend = pstart + cnt
        esub = lax.broadcasted_iota(I32, (N_EXPERTS, LANES), 0)
        lane = lax.broadcasted_iota(I32, (1, LANES), 1)
        tab_ref[...] = jnp.zeros_like(tab_ref)
        for c in range(tab_ref.shape[1] // LANES):
            bs = ((c * LANES + lane) * EXPERT_BLOCK).astype(F32)
            be = jnp.minimum(jnp.sum((pend <= bs).astype(F32), axis=0, keepdims=True), N_EXPERTS - 1.0)
            tab_ref[0:1, c * LANES:(c + 1) * LANES] = be.astype(I32)
            tab_ref[1:2, c * LANES:(c + 1) * LANES] = (pend[N_EXPERTS - 1:, :] / EXPERT_BLOCK).astype(I32)
        on_diag = esub == lax.broadcasted_iota(I32, (N_EXPERTS, LANES), 1)
        etab_ref[...] = jnp.zeros_like(etab_ref)
        lo = jnp.sum(jnp.where(on_diag, vend, 0.0), axis=0, keepdims=True)
        hi = jnp.sum(jnp.where(on_diag, pend, 0.0), axis=0, keepdims=True)
        etab_ref[0:1, :] = jnp.where(lane == N_EXPERTS, pend[N_EXPERTS - 1:, :], lo).astype(I32)
        etab_ref[1:2, :] = jnp.where(lane == N_EXPERTS, float(n_slots), hi).astype(I32)

    @pl.when(ph == 1)
    def _():
        ti = lax.broadcasted_iota(I32, (T, T), 0)
        tj = lax.broadcasted_iota(I32, (T, T), 1)
        selb = sel.astype(BF16)
        rank = _dot(selb, (ti < tj).astype(BF16))
        ordn = _dot((ej < ei).astype(BF16), selb)
        dest_e = pst_ref[:, :1] + run_ref[:, :1] + rank
        run_ref[...] += cnt_tile
        tok = i * T + lax.broadcasted_iota(I32, (1, T), 1)
        dks, wks = [], []
        for k in range(TOP_K):
            m = (sel > 0.5) & (ordn == float(k))
            dk = jnp.sum(jnp.where(m, dest_e, 0.0), axis=0, keepdims=True)
            wk = jnp.sum(jnp.where(m, w, 0.0), axis=0, keepdims=True)
            dks.append(jnp.where(tok < n_real, dk, 0.0))
            wks.append(jnp.where(tok < n_real, wk, 0.0))
        dest_ref[...] = jnp.concatenate(dks, axis=0).astype(I32)
        w8_ref[...] = jnp.concatenate(wks, axis=0)


def _rank_call(w_t, n_real, n_blocks, n_blocks_pad):
    n = w_t.shape[1]
    nt = n // MOE_TILE
    return pl.pallas_call(
        functools.partial(_rank_body, n_real=n_real, n_slots=n_blocks * EXPERT_BLOCK),
        out_shape=(jax.ShapeDtypeStruct((TOP_K, n), I32),
                   jax.ShapeDtypeStruct((TOP_K, n), F32),
                   jax.ShapeDtypeStruct((8, n_blocks_pad), I32),
                   jax.ShapeDtypeStruct((8, LANES), I32)),
        grid=(2, nt),
        in_specs=[pl.BlockSpec((N_EXPERTS, MOE_TILE), lambda ph, i: (0, i))],
        out_specs=(pl.BlockSpec((TOP_K, MOE_TILE), lambda ph, i: (0, i * ph)),
                   pl.BlockSpec((TOP_K, MOE_TILE), lambda ph, i: (0, i * ph)),
                   pl.BlockSpec((8, n_blocks_pad), lambda ph, i: (0, 0)),
                   pl.BlockSpec((8, LANES), lambda ph, i: (0, 0))),
        scratch_shapes=[pltpu.VMEM((N_EXPERTS, LANES), F32)] * 3,
        compiler_params=_cparams(("arbitrary", "arbitrary")),
        name="rank",
    )(w_t)


def _tile_rows(ref, row, n):
    return ref.at[pl.ds(pl.multiple_of(row * ROW_TILE_SUBLANES, ROW_TILE_SUBLANES), n * ROW_TILE_SUBLANES)]


def _zero_fill(etab_ref, zbuf, xs_hbm, zsem, wait):
    def go(src, dst):
        cp = pltpu.make_async_copy(src, dst, zsem)
        if wait:
            cp.wait()
        else:
            cp.start()

    def per_range(e, carry):
        lo = etab_ref[0, e]
        n = etab_ref[1, e] - lo
        n_full = n // ZERO_ROWS

        def full(j, c):
            go(zbuf, _tile_rows(xs_hbm, lo + j * ZERO_ROWS, ZERO_ROWS))
            return c

        lax.fori_loop(0, n_full, full, 0)
        pos = lo + n_full * ZERO_ROWS
        rem = n - n_full * ZERO_ROWS
        size = ZERO_ROWS // 2
        while size >= 1:
            bit = rem & size

            @pl.when(bit != 0)
            def _(size=size, pos=pos):
                go(_tile_rows(zbuf, 0, size), _tile_rows(xs_hbm, pos, size))

            pos = pos + bit
            size //= 2
        return carry

    lax.fori_loop(0, N_EXPERTS + 1, per_range, 0)


def _dispatch_body(dest_ref, etab_ref, xa_ref, xb_ref, xs_hbm, zbuf, sem, zsem, *, n_real, n_full):
    i = pl.program_id(0)
    T = MOE_TILE
    n_tok = jnp.clip(n_real - i * T, 0, T)

    def issue_from(x_ref):
        def issue(t, carry):
            for k in range(TOP_K):
                pltpu.make_async_copy(_tile_rows(x_ref, t, 1), _tile_rows(xs_hbm, dest_ref[k * T + t], 1),
                                      sem).start(priority=k % 2)
            return carry

        lax.fori_loop(0, n_tok, issue, 0)

    @pl.when(i < n_full)
    def _():
        issue_from(xa_ref)

    @pl.when(i >= n_full)
    def _():
        issue_from(xb_ref)

    @pl.when(i == 0)
    def _():
        zbuf[...] = jnp.zeros_like(zbuf)
        _zero_fill(etab_ref, zbuf, xs_hbm, zsem, wait=False)
        _zero_fill(etab_ref, zbuf, xs_hbm, zsem, wait=True)

    @pl.when(n_tok == T)
    def _():
        pltpu.make_async_copy(_tile_rows(xs_hbm, 0, T * TOP_K), _tile_rows(xs_hbm, 0, T * TOP_K), sem).wait()

    @pl.when(n_tok < T)
    def _():
        def drain(j, carry):
            pltpu.make_async_copy(_tile_rows(xs_hbm, 0, 1), _tile_rows(xs_hbm, 0, 1), sem).wait()
            return carry

        lax.fori_loop(0, n_tok * TOP_K, drain, 0)


def _dispatch_call(dest, etab, hp_a, hp_b, n_real, n_slots):
    tile_rows = MOE_TILE * ROW_TILE_SUBLANES
    n_full = hp_a.shape[0] // tile_rows
    return pl.pallas_call(
        functools.partial(_dispatch_body, n_real=n_real, n_full=n_full),
        out_shape=jax.ShapeDtypeStruct((n_slots * ROW_TILE_SUBLANES, LANES), I32),
        grid=(n_full + 1,),
        in_specs=[pl.BlockSpec((TOP_K * MOE_TILE,), lambda i: (i,), memory_space=pltpu.SMEM),
                  pl.BlockSpec((8, LANES), lambda i: (0, 0), memory_space=pltpu.SMEM),
                  pl.BlockSpec((tile_rows, LANES), lambda i: (jnp.minimum(i, n_full - 1), 0)),
                  pl.BlockSpec((tile_rows, LANES), lambda i: (0, 0))],
        out_specs=pl.BlockSpec(memory_space=pl.ANY),
        scratch_shapes=[pltpu.VMEM((ZERO_ROWS * ROW_TILE_SUBLANES, LANES), I32),
                        pltpu.SemaphoreType.DMA, pltpu.SemaphoreType.DMA],
        compiler_params=_cparams(("arbitrary",)),
        name="dispatch",
    )(dest, etab, hp_a, hp_b)


def _rows_from_tiles(ref, lo, n):
    return jnp.concatenate([ref[pl.ds(lo * ROW_TILE_SUBLANES + s, n, stride=ROW_TILE_SUBLANES), :]
                            for s in range(ROW_TILE_SUBLANES)], axis=1)


def _ffn_body(be_ref, nu_ref, xs_ref, wg_ref, wu_ref, wd_ref, ys_ref, wgb, wub, wdb):
    j = pl.program_id(0)

    @pl.when(j < nu_ref[0])
    def _():
        @pl.when((j == 0) | (be_ref[j] != be_ref[jnp.maximum(j - 1, 0)]))
        def _():
            wgb[...] = wg_ref[0].astype(BF16)
            wub[...] = wu_ref[0].astype(BF16)
            wdb[...] = wd_ref[0].astype(BF16)

        x = _unpack_pairs(_rows_from_tiles(xs_ref, 0, EXPERT_BLOCK)).astype(BF16)
        act = _silu(_dot(x, wgb[...])) * _dot(x, wub[...])
        y = _dot(act.astype(BF16), wdb[...])
        packed = _pack_pairs(y)
        for s in range(ROW_TILE_SUBLANES):
            ys_ref[pl.ds(s, EXPERT_BLOCK, stride=ROW_TILE_SUBLANES), :] = packed[:, s * LANES:(s + 1) * LANES]

    @pl.when(j >= nu_ref[0])
    def _():
        ys_ref[...] = jnp.zeros_like(ys_ref)


def _ffn_call(blk_e, n_used, xs, w_gate, w_up, w_down, n_blocks):
    tile_blk = pl.BlockSpec((EXPERT_BLOCK * ROW_TILE_SUBLANES, LANES), lambda j, be, nu: (j, 0))
    last = lambda j, nu: jnp.minimum(j, nu[0] - 1)
    grid_spec = pltpu.PrefetchScalarGridSpec(
        num_scalar_prefetch=2,
        grid=(n_blocks,),
        in_specs=[pl.BlockSpec((EXPERT_BLOCK * ROW_TILE_SUBLANES, LANES), lambda j, be, nu: (last(j, nu), 0)),
                  pl.BlockSpec((1, D_MODEL, D_EXPERT), lambda j, be, nu: (be[last(j, nu)], 0, 0)),
                  pl.BlockSpec((1, D_MODEL, D_EXPERT), lambda j, be, nu: (be[last(j, nu)], 0, 0)),
                  pl.BlockSpec((1, D_EXPERT, D_MODEL), lambda j, be, nu: (be[last(j, nu)], 0, 0))],
        out_specs=tile_blk,
        scratch_shapes=[pltpu.VMEM((D_MODEL, D_EXPERT), BF16), pltpu.VMEM((D_MODEL, D_EXPERT), BF16),
                        pltpu.VMEM((D_EXPERT, D_MODEL), BF16)])
    return pl.pallas_call(
        _ffn_body,
        out_shape=jax.ShapeDtypeStruct((n_blocks * EXPERT_BLOCK * ROW_TILE_SUBLANES, LANES), I32),
        grid_spec=grid_spec,
        compiler_params=_cparams(("arbitrary",)),
        name="ffn",
    )(blk_e, n_used, xs, w_gate, w_up, w_down)


def _combine_body(dest_ref, dnext_ref, w8_ref, xa_ref, xb_ref, x1a_ref, x1b_ref, g2a_ref, g2b_ref, gain_ref,
                  sg_ref, su_ref, sd_ref, ys_hbm, oa_ref, ob_ref, buf, sem):
    j = pl.program_id(0)
    T = COMBINE_TILE
    RC = COMBINE_ROWS

    def gather(d_ref, slot):
        def issue(t, carry):
            for k in range(TOP_K):
                pltpu.make_async_copy(_tile_rows(ys_hbm, d_ref[k * T + t], 1), _tile_rows(buf.at[slot], k * T + t, 1),
                                      sem.at[slot]).start(priority=k % 2)
            return carry

        lax.fori_loop(0, T, issue, 0, unroll=2)

    def step(slot):
        @pl.when(j + 1 < pl.num_programs(0))
        def _():
            gather(dnext_ref, 1 - slot)

        is_tail = j == 0
        x = _unpack_pairs(jnp.where(is_tail, _rows_from_tiles(xb_ref, 0, T),
                                    _rows_from_tiles(xa_ref, 0, T))).astype(BF16)
        shared = _dot((_silu(_dot(x, sg_ref[...])) * _dot(x, su_ref[...])).astype(BF16), sd_ref[...])
        w_t = jnp.concatenate([w8_ref[...], jnp.zeros((LANES - TOP_K, T), F32)], axis=0).T
        oa_ref[...] = shared
        pltpu.make_async_copy(_tile_rows(ys_hbm, 0, T * TOP_K), buf.at[slot], sem.at[slot]).wait()
        for r0 in range(0, T, RC):
            acc = oa_ref[r0:r0 + RC, :]
            for k in range(TOP_K):
                acc = acc + w_t[r0:r0 + RC, k:k + 1] * _unpack_pairs(_rows_from_tiles(buf.at[slot], k * T + r0, RC))
            x1 = jnp.where(is_tail, x1b_ref[r0:r0 + RC, :], x1a_ref[r0:r0 + RC, :])
            g2 = jnp.where(is_tail, g2b_ref[r0:r0 + RC, :], g2a_ref[0])
            oa_ref[r0:r0 + RC, :] = x1 + g2 * _rms(acc, gain_ref[...])

        @pl.when(is_tail)
        def _():
            ob_ref[...] = oa_ref[...]

    @pl.when(j == 0)
    def _():
        gather(dest_ref, 0)

    @pl.when(j % 2 == 0)
    def _():
        step(0)

    @pl.when(j % 2 == 1)
    def _():
        step(1)


def _combine_call(dest, w8, hp_a, hp_b, x1_a, x1_b, gate2_a, gate2_b, gain, wsg, wsu, wsd, ys):
    T = COMBINE_TILE
    tile_rows = T * ROW_TILE_SUBLANES
    n_full = hp_a.shape[0] // tile_rows
    n_tiles = n_full + 1
    seq = x1_a.shape[0] // gate2_a.shape[0]
    tile_of = lambda j: jnp.where(j == 0, n_full, j - 1)
    full_of = lambda j: jnp.maximum(j - 1, 0)
    const = lambda shp: pl.BlockSpec(shp, lambda j: (0, 0))
    return pl.pallas_call(
        _combine_body,
        out_shape=(jax.ShapeDtypeStruct((n_full * T, D_MODEL), F32), jax.ShapeDtypeStruct((T, D_MODEL), F32)),
        grid=(n_tiles,),
        in_specs=[pl.BlockSpec((TOP_K * T,), lambda j: (tile_of(j),), memory_space=pltpu.SMEM),
                  pl.BlockSpec((TOP_K * T,), lambda j: (tile_of(jnp.minimum(j + 1, n_tiles - 1)),),
                               memory_space=pltpu.SMEM),
                  pl.BlockSpec((TOP_K, T), lambda j: (0, tile_of(j))),
                  pl.BlockSpec((tile_rows, LANES), lambda j: (full_of(j), 0)),
                  pl.BlockSpec((tile_rows, LANES), lambda j: (0, 0)),
                  pl.BlockSpec((T, D_MODEL), lambda j: (full_of(j), 0)),
                  const((T, D_MODEL)),
                  pl.BlockSpec((1, 1, D_MODEL), lambda j: (full_of(j) * T // seq, 0, 0)),
                  const((T, D_MODEL)), const((1, D_MODEL)),
                  const((D_MODEL, D_EXPERT)), const((D_MODEL, D_EXPERT)), const((D_EXPERT, D_MODEL)),
                  pl.BlockSpec(memory_space=pl.ANY)],
        out_specs=(pl.BlockSpec((T, D_MODEL), lambda j: (full_of(j), 0)), const((T, D_MODEL))),
        scratch_shapes=[pltpu.VMEM((2, TOP_K * tile_rows, LANES), I32), pltpu.SemaphoreType.DMA((2,))],
        compiler_params=_cparams(("arbitrary",)),
        name="combine",
    )(dest, dest, w8, hp_a, hp_b, x1_a, x1_b, gate2_a, gate2_b, gain.reshape(1, -1), wsg, wsu, wsd, ys)


def _rope_tables(pos):
    half = HEAD_DIM_A // 2
    inv_freq = ROPE_THETA ** (-jnp.arange(half, dtype=F32) / half)
    ang = pos.astype(F32)[:, None] * inv_freq[None, :]
    cos = jnp.cos(ang)
    sin = jnp.sin(ang)
    reps = LANES // HEAD_DIM_A
    cos_t = jnp.tile(jnp.concatenate([cos, cos], axis=1), (1, reps))
    sin_t = jnp.tile(jnp.concatenate([-sin, sin], axis=1), (1, reps))
    return cos_t, sin_t


def _cache_from_tail(tail, keep):
    outs = []
    n, rows, _ = tail.shape
    for gi, kp in enumerate(keep):
        k = tail[:, rows - kp:, gi * D_GROUP_A:(gi + 1) * D_GROUP_A]
        v = tail[:, rows - kp:, D_A + gi * D_GROUP_A:D_A + (gi + 1) * D_GROUP_A]
        outs.append(jnp.stack([k, v], axis=2).reshape(n, kp, 2, N_HEADS_A, HEAD_DIM_A))
    return outs


def kernel(x_prompt, x_sample, c_prompt, c_sample, cache_a1_kv, cache_a2_kv, cache_a3_kv, state_b_wkv, state_b_shift, w_ada, b_ada, norm_pre_mix, norm_post_mix, norm_pre_ffn, norm_post_ffn, w_in, w_a_out, mu_b, w0_b, w_w2_b, a0_b, w_a2_b, w_g2_b, k_k_b, k_a_b, r_k_b, ln_x_w_b, ln_x_b_b, w_b_out, w_out, w_router, router_bias, w_e_gate, w_e_up, w_e_down, w_s_gate, w_s_up, w_s_down):
    assert DEPTH == 1
    l = 0
    nd = DEC_BATCH
    row = lambda a: a.reshape(1, -1)
    p = {'mu_b': row(mu_b[l]), 'w0_b': row(w0_b[l]), 'w_w2_b': w_w2_b[l], 'a0_b': row(a0_b[l]),
         'w_a2_b': w_a2_b[l], 'w_g2_b': w_g2_b[l], 'k_k_b': row(k_k_b[l]), 'k_a_b': row(k_a_b[l]),
         'r_k_b': row(r_k_b[l]), 'ln_x_w_b': row(ln_x_w_b[l]), 'ln_x_b_b': row(ln_x_b_b[l]),
         'norm_post_mix': norm_post_mix[l], 'norm_pre_ffn': norm_pre_ffn[l]}

    wq = w_in[l][:, :D_QKV].astype(BF16)
    wf = w_in[l][:, D_QKV:D_QKV + D_SHIFT_B].astype(BF16)
    wg = w_in[l][:, D_QKV + D_SHIFT_B:].astype(BF16)
    wa = w_a_out[l].astype(BF16)
    wb = w_b_out[l].astype(BF16)
    wo = w_out[l].astype(BF16)
    wrt = jnp.concatenate([w_router[l].T, jnp.zeros((LANES - N_EXPERTS, D_MODEL), F32)], axis=0)
    rb = router_bias[l].reshape(N_EXPERTS, 1)
    wsg, wsu, wsd = w_s_gate[l].astype(BF16), w_s_up[l].astype(BF16), w_s_down[l].astype(BF16)

    n_c = BATCH + nd
    c_all = jnp.concatenate([c_prompt, c_sample, jnp.zeros((-n_c % 8, D_MODEL), F32)], axis=0)
    mod = _mod_call(c_all, w_ada[l], b_ada[l])
    mod_p = [m.reshape(BATCH, 1, D_MODEL) for m in jnp.split(mod[:BATCH], 6, axis=-1)]
    mod_s = [m.reshape(1, nd, D_MODEL) for m in jnp.split(mod[BATCH:n_c], 6, axis=-1)]

    cos_p, sin_p = _rope_tables(jnp.arange(SEQ, dtype=I32))
    cos_s, sin_s = _rope_tables(jnp.full((nd,), PAST_LEN, I32))

    keep_p = [min(w, SEQ) for w, _ in DILATED_GROUPS]
    tail_rows = max(keep_p)
    dils = tuple(d for _, d in DILATED_GROUPS)

    q0, q1, q2, feat_p, gates_p, tail_p = _inproj_call(
        x_prompt, norm_pre_mix[l], mod_p[1], mod_p[0], cos_p, sin_p, wq, wf, wg,
        tm=256, tail_rows=tail_rows, mod_per_row=False, dils=dils)
    o_parts, lse_parts = [], []
    for gi, qg in enumerate((q0, q1, q2)):
        o, lse = _attn_call(qg, gi)
        o_parts.append(o)
        lse_parts.append(lse)
    ob_p, wkv_p = _wkv_call(feat_p, p)
    x1_p, hp_p, wt_p = _post_call(o_parts, lse_parts, ob_p, gates_p, x_prompt, mod_p[2], mod_p[4], mod_p[3],
                                  p, wa, wb, wo, wrt, rb, tm=512, mod_per_row=False)

    xs3 = x_sample.reshape(1, nd, D_MODEL)
    s0, s1, s2, feat_s, gates_s, tail_s = _inproj_call(
        xs3, norm_pre_mix[l], mod_s[1], mod_s[0], cos_s, sin_s, wq, wf, wg,
        tm=nd, tail_rows=nd, mod_per_row=True, dils=(1, 1, 1))
    qkv_s = jnp.stack([z.reshape(nd, 3, N_HEADS_A, HEAD_DIM_A) for z in (s0, s1, s2)], axis=2)
    qkv_s = qkv_s.reshape(nd, 3 * N_GROUPS_A, N_HEADS_A, HEAD_DIM_A, 1).astype(F32)
    oa_s = _sattn_call(qkv_s, cache_a1_kv[l], cache_a2_kv[l], cache_a3_kv[l])
    r_s, w_s, k_s, v_s, aa_s, bb_s, g_s = _swkv_prep_call(feat_s[0], state_b_shift[l], p)
    nh = nd * N_HEADS_B
    as_row = lambda a: a.reshape(nh, 1, HEAD_DIM_B)
    s_new, y_col = _swkv_step_call(state_b_wkv[l].reshape(nh, HEAD_DIM_B, HEAD_DIM_B), as_row(aa_s), as_row(w_s),
                                   as_row(bb_s), as_row(k_s), as_row(r_s), v_s.reshape(nh, HEAD_DIM_B, 1))
    ob_s = _swkv_fin_call(y_col.reshape(nd, D_B), r_s, k_s, v_s, g_s, p)
    x1_s, hp_s, wt_s = _post_call([oa_s.reshape(1, nd, D_GROUP_A)], None, ob_s.reshape(1, nd, D_B), gates_s, xs3,
                                  mod_s[2], mod_s[4], mod_s[3], p, wa, wb, wo, wrt, rb, tm=nd, mod_per_row=True)

    n_p = BATCH * SEQ
    n_real = n_p + nd
    n_all = -(-n_real // MOE_TILE) * MOE_TILE
    pad = n_all - n_real
    n_blocks = -(-(n_real * TOP_K) // EXPERT_BLOCK) + N_EXPERTS
    n_blocks_pad = -(-n_blocks // LANES) * LANES
    assert n_p % MOE_TILE == 0 and nd <= MOE_TILE
    hp_a = hp_p.reshape(n_p * ROW_TILE_SUBLANES, LANES)
    hp_b = jnp.concatenate([hp_s[0], jnp.zeros((pad * ROW_TILE_SUBLANES, LANES), I32)], axis=0)
    wt_all = jnp.concatenate([wt_p, wt_s, jnp.full((N_EXPERTS, pad), -1.0, F32)], axis=1)
    dest8, w8, tab, etab = _rank_call(wt_all, n_real, n_blocks, n_blocks_pad)
    dest = dest8.reshape(TOP_K, n_all // MOE_TILE, MOE_TILE).transpose(1, 0, 2).reshape(-1)
    xs = _dispatch_call(dest, etab, hp_a, hp_b, n_real, n_blocks * EXPERT_BLOCK)
    ys = _ffn_call(tab[0], tab[1, :1], xs, w_e_gate[l], w_e_up[l], w_e_down[l], n_blocks)
    n_ct = n_p // COMBINE_TILE + 1
    dest_c = dest8[:, :n_ct * COMBINE_TILE].reshape(TOP_K, n_ct, COMBINE_TILE).transpose(1, 0, 2).reshape(-1)
    pad_rows = lambda z: jnp.concatenate([z, jnp.zeros((COMBINE_TILE - nd, D_MODEL), F32)], axis=0)
    out_p, out_s = _combine_call(dest_c, w8, hp_a, hp_b, x1_p.reshape(n_p, D_MODEL), pad_rows(x1_s[0]), mod_p[5],
                                 pad_rows(mod_s[5][0]), norm_post_ffn[l], wsg, wsu, wsd, ys)
    y_prompt = out_p.reshape(BATCH, SEQ, D_MODEL)
    y_sample = out_s[:nd]

    a_p = [z[None] for z in _cache_from_tail(tail_p, keep_p)]
    a_s = [z.reshape(1, nd, DEC_SEQ, 2, N_HEADS_A, HEAD_DIM_A)
           for z in _cache_from_tail(tail_s.reshape(nd, 1, 2 * D_A), [DEC_SEQ] * N_GROUPS_A)]
    shift_p = feat_p[:, -1][None]
    shift_s = feat_s[0][None]
    return (y_prompt, y_sample.reshape(nd, DEC_SEQ, D_MODEL), a_p[0], a_p[1], a_p[2], wkv_p[None], shift_p,
            a_s[0], a_s[1], a_s[2], s_new.reshape(1, nd, N_HEADS_B, HEAD_DIM_B, HEAD_DIM_B), shift_s)
```

```python
import functools
import math

import jax
import jax.numpy as jnp
from jax import lax
from jax.experimental import pallas as pl
from jax.experimental.pallas import tpu as pltpu

F32 = jnp.float32
BF16 = jnp.bfloat16
I32 = jnp.int32

D_MODEL = 1024
BATCH = 2
SEQ = 8192
DEPTH = 1
DEC_BATCH = 32
DEC_SEQ = 1
PAST_LEN = 16384

HEAD_DIM_A = 64
N_HEADS_A = 8
DILATED_GROUPS = ((128, 1), (512, 4), (2048, 16))
N_GROUPS_A = 3
D_GROUP_A = N_HEADS_A * HEAD_DIM_A
D_A = N_GROUPS_A * D_GROUP_A
D_QKV = 3 * D_A
BAND_BLOCK = 128
ROPE_THETA = 10000.0

HEAD_DIM_B = 64
N_HEADS_B = 16
D_B = 1024
DECAY_LORA = 64
AAA_LORA = 64
GATE_LORA = 160
D_SHIFT_B = 3 * D_B + DECAY_LORA + AAA_LORA + GATE_LORA
LN_X_EPS = 64e-5

N_EXPERTS = 64
TOP_K = 8
N_EXPERT_GROUPS = 8
TOPK_GROUPS = 4
D_EXPERT = 256
ROUTED_SCALE = 2.5
EXPERT_BLOCK = 512
NORM_EPS = 1e-6

LANES = 128
WKV_CHUNK = 64
MOE_TILE = 1024
COMBINE_TILE = 256
COMBINE_ROWS = 32
VMEM_LIMIT = 56 * 1024 * 1024
ROW_TILE_SUBLANES = D_MODEL // (2 * LANES)
ZERO_ROWS = 256


def _cparams(sem):
    return pltpu.CompilerParams(dimension_semantics=sem, vmem_limit_bytes=VMEM_LIMIT)


def _dot(a, b):
    return jnp.dot(a, b, preferred_element_type=F32)


def _dot_nt(a, b):
    return lax.dot_general(a, b, (((1,), (1,)), ((), ())), preferred_element_type=F32)


def _dot_tn(a, b):
    return lax.dot_general(a, b, (((0,), (0,)), ((), ())), preferred_element_type=F32)


def _dot_nt_split(a, b):
    ah = a.astype(BF16)
    al = (a - ah.astype(F32)).astype(BF16)
    bh = b.astype(BF16)
    bl = (b - bh.astype(F32)).astype(BF16)
    return _dot_nt(ah, bh) + _dot_nt(ah, bl) + _dot_nt(al, bh)


def _dot_exact(a, b):
    return lax.dot_general(a, b, (((1,), (0,)), ((), ())), precision=lax.Precision.HIGHEST,
                           preferred_element_type=F32)


def _rms(x, gain):
    return x * lax.rsqrt(jnp.mean(x * x, axis=-1, keepdims=True) + NORM_EPS) * gain


def _sigmoid(x):
    return 1.0 / (1.0 + jnp.exp(-x))


def _silu(x):
    return x * _sigmoid(x)


def _softplus(x):
    return jnp.maximum(x, 0.0) + jnp.log(1.0 + jnp.exp(-jnp.abs(x)))


def _pack_pairs(x):
    half = D_MODEL // 2
    lo = lax.bitcast_convert_type(x[:, :half].astype(BF16).astype(F32), I32)
    hi = lax.bitcast_convert_type(x[:, half:].astype(BF16).astype(F32), I32)
    return lax.shift_right_logical(lo, 16) | (hi & jnp.int32(-65536))


def _unpack_pairs(w):
    lo = lax.bitcast_convert_type(w << 16, F32)
    hi = lax.bitcast_convert_type(w & jnp.int32(-65536), F32)
    return jnp.concatenate([lo, hi], axis=1)


def _mod_body(c_ref, w_ref, b_ref, o_ref):
    s = _silu(c_ref[...]).astype(BF16)
    o_ref[...] = _dot(s, w_ref[...].astype(BF16)) + b_ref[...]


def _mod_call(c_all, w_ada, b_ada):
    rows = c_all.shape[0]
    tn = 1536
    return pl.pallas_call(
        _mod_body,
        out_shape=jax.ShapeDtypeStruct((rows, 6 * D_MODEL), F32),
        grid=(6 * D_MODEL // tn,),
        in_specs=[pl.BlockSpec((rows, D_MODEL), lambda j: (0, 0)),
                  pl.BlockSpec((D_MODEL, tn), lambda j: (0, j)),
                  pl.BlockSpec((1, tn), lambda j: (0, j))],
        out_specs=pl.BlockSpec((rows, tn), lambda j: (0, j)),
        compiler_params=_cparams(("arbitrary",)),
        name="mod",
    )(c_all, w_ada, b_ada.reshape(1, -1))


def _inproj_body(x_ref, g_ref, sc_ref, sh_ref, cos_ref, sin_ref, wq_ref, wf_ref, wg_ref,
                 q0_ref, q1_ref, q2_ref, feat_ref, gate_ref, tail_ref, p_ref, *, dils):
    x = x_ref[0]
    tm = x.shape[0]
    h = _rms(x, g_ref[...]) * (1.0 + sc_ref[0]) + sh_ref[0]
    hb = h.astype(BF16)
    p = _dot(hb, wq_ref[...])
    cos = cos_ref[...]
    sin = sin_ref[...]
    lane = lax.broadcasted_iota(I32, cos.shape, 1)
    first_half = (lane % HEAD_DIM_A) < (HEAD_DIM_A // 2)
    for c in range(2 * D_A // LANES):
        xc = p[:, c * LANES:(c + 1) * LANES]
        partner = jnp.where(first_half, pltpu.roll(xc, LANES - HEAD_DIM_A // 2, 1),
                            pltpu.roll(xc, HEAD_DIM_A // 2, 1))
        rc = xc * cos + partner * sin
        if c < D_A // LANES:
            rc = rc * (HEAD_DIM_A ** -0.5)
        p_ref[c] = rc
        if c >= D_A // LANES:
            tail_ref[0, :, (c - D_A // LANES) * LANES:(c - D_A // LANES + 1) * LANES] = rc
    for c in range(2 * D_A // LANES, D_QKV // LANES):
        p_ref[c] = p[:, c * LANES:(c + 1) * LANES]
    tail_ref[0, :, D_A:] = p[:, 2 * D_A:]
    per_group = D_GROUP_A // LANES
    for gi, (out_ref, dil) in enumerate(zip((q0_ref, q1_ref, q2_ref), dils)):
        for which in range(3):
            for j in range(per_group):
                c = (which * D_A + gi * D_GROUP_A) // LANES + j
                dst = slice(which * D_GROUP_A + j * LANES, which * D_GROUP_A + (j + 1) * LANES)
                if dil == 1:
                    out_ref[0, 0, :, dst] = p_ref[c].astype(BF16)
                else:
                    for r in range(dil):
                        out_ref[0, r, :, dst] = p_ref[c, pl.ds(r, tm // dil, stride=dil), :].astype(BF16)
    feat_ref[0] = _dot(hb, wf_ref[...])
    gate_ref[0] = _sigmoid(_dot(hb, wg_ref[...])).astype(BF16)


def _inproj_call(x, gain, scale, shift, cos_t, sin_t, wq, wf, wg, tm, tail_rows, mod_per_row, dils):
    nb, t, _ = x.shape
    nt = t // tm
    tail_first = (t - tail_rows) // tm
    if mod_per_row:
        mod_spec = pl.BlockSpec((1, tm, D_MODEL), lambda b, i: (b, i, 0))
    else:
        mod_spec = pl.BlockSpec((1, 1, D_MODEL), lambda b, i: (b, 0, 0))
    resident = lambda shp: pl.BlockSpec(shp, lambda b, i: (0, 0), pipeline_mode=pl.Buffered(1))
    q_shapes = tuple(jax.ShapeDtypeStruct((nb, d, t // d, 3 * D_GROUP_A), BF16) for d in dils)
    q_specs = tuple(pl.BlockSpec((1, d, tm // d, 3 * D_GROUP_A), lambda b, i: (b, 0, i, 0)) for d in dils)
    return pl.pallas_call(
        functools.partial(_inproj_body, dils=dils),
        out_shape=q_shapes + (jax.ShapeDtypeStruct((nb, t, D_SHIFT_B), F32),
                              jax.ShapeDtypeStruct((nb, t, 2 * D_MODEL), BF16),
                              jax.ShapeDtypeStruct((nb, tail_rows, 2 * D_A), F32)),
        grid=(nb, nt),
        in_specs=[pl.BlockSpec((1, tm, D_MODEL), lambda b, i: (b, i, 0)),
                  pl.BlockSpec((1, D_MODEL), lambda b, i: (0, 0)),
                  mod_spec, mod_spec,
                  pl.BlockSpec((tm, LANES), lambda b, i: (i, 0)),
                  pl.BlockSpec((tm, LANES), lambda b, i: (i, 0)),
                  resident((D_MODEL, D_QKV)), resident((D_MODEL, D_SHIFT_B)),
                  resident((D_MODEL, 2 * D_MODEL))],
        out_specs=q_specs + (pl.BlockSpec((1, tm, D_SHIFT_B), lambda b, i: (b, i, 0)),
                             pl.BlockSpec((1, tm, 2 * D_MODEL), lambda b, i: (b, i, 0)),
                             pl.BlockSpec((1, tm, 2 * D_A), lambda b, i: (b, jnp.maximum(i - tail_first, 0), 0))),
        scratch_shapes=[pltpu.VMEM((D_QKV // LANES, tm, LANES), F32)],
        compiler_params=_cparams(("arbitrary", "arbitrary")),
        name="inproj",
    )(x, gain.reshape(1, -1), scale, shift, cos_t, sin_t, wq, wf, wg)


def _attn_body(q_ref, kc_ref, kp_ref, vc_ref, vp_ref, o_ref, lse_ref):
    mb = pl.program_id(2)
    nq = q_ref.shape[2] // BAND_BLOCK
    q = q_ref[0, 0]
    k = jnp.concatenate([kp_ref[0, 0], kc_ref[0, 0]], axis=0)
    v = jnp.concatenate([vp_ref[0, 0], vc_ref[0, 0]], axis=0)
    qi = lax.broadcasted_iota(I32, (BAND_BLOCK, 2 * BAND_BLOCK), 0)
    ki = lax.broadcasted_iota(I32, (BAND_BLOCK, 2 * BAND_BLOCK), 1)
    dist = qi + BAND_BLOCK - ki
    band = (dist >= 0) & (dist <= BAND_BLOCK)
    masks = [band & ((ki >= BAND_BLOCK) | (mb > 0))] + [band] * (nq - 1)
    lane_q = lax.broadcasted_iota(I32, (BAND_BLOCK, LANES), 1)
    lane_k = lax.broadcasted_iota(I32, (2 * BAND_BLOCK, LANES), 1)
    for hp in range(N_HEADS_A // 2):
        sl = slice(hp * LANES, (hp + 1) * LANES)
        chains = [(j, sub) for j in range(nq) for sub in range(2)]
        qs = [q[j * BAND_BLOCK:(j + 1) * BAND_BLOCK, sl] for j in range(nq)]
        ks = [k[j * BAND_BLOCK:(j + 2) * BAND_BLOCK, sl] for j in range(nq)]
        vs = [v[j * BAND_BLOCK:(j + 2) * BAND_BLOCK, sl] for j in range(nq)]
        mqs = [lane_q < HEAD_DIM_A, lane_q >= HEAD_DIM_A]
        mks = [lane_k < HEAD_DIM_A, lane_k >= HEAD_DIM_A]
        s = [jnp.where(masks[j], _dot_nt(jnp.where(mqs[sub], qs[j], jnp.zeros_like(qs[j])), ks[j]), -jnp.inf)
             for j, sub in chains]
        mx = [jnp.max(z, axis=1, keepdims=True) for z in s]
        p = [jnp.exp(z - m) for z, m in zip(s, mx)]
        l = [jnp.sum(z, axis=1, keepdims=True) for z in p]
        pv = [_dot(p[c].astype(BF16), jnp.where(mks[sub], vs[j], jnp.zeros_like(vs[j])))
              for c, (j, sub) in enumerate(chains)]
        for j in range(nq):
            c0, c1 = 2 * j, 2 * j + 1
            o_pair = pv[c0] / l[c0] + pv[c1] / l[c1]
            lse_pair = jnp.where(mqs[0], mx[c0] + jnp.log(l[c0]), mx[c1] + jnp.log(l[c1]))
            o_ref[0, 0, j * BAND_BLOCK:(j + 1) * BAND_BLOCK, sl] = o_pair.astype(BF16)
            lse_ref[0, 0, j * BAND_BLOCK:(j + 1) * BAND_BLOCK, sl] = lse_pair


def _attn_call(qkv_g, gi):
    b, dil, l, _ = qkv_g.shape
    nq = 4
    nb = l // (nq * BAND_BLOCK)
    blk = (1, 1, nq * BAND_BLOCK, D_GROUP_A)
    cur = lambda which: pl.BlockSpec(blk, lambda bb, r, m: (bb, r, m, which))
    prev = lambda which: pl.BlockSpec((1, 1, BAND_BLOCK, D_GROUP_A),
                                      lambda bb, r, m: (bb, r, jnp.maximum(nq * m - 1, 0), which))
    return pl.pallas_call(
        _attn_body,
        out_shape=(jax.ShapeDtypeStruct((b, dil, l, D_GROUP_A), BF16),
                   jax.ShapeDtypeStruct((b, dil, l, D_GROUP_A), F32)),
        grid=(b, dil, nb),
        in_specs=[cur(0), cur(1), prev(1), cur(2), prev(2)],
        out_specs=(pl.BlockSpec(blk, lambda bb, r, m: (bb, r, m, 0)),
                   pl.BlockSpec(blk, lambda bb, r, m: (bb, r, m, 0))),
        compiler_params=_cparams(("arbitrary", "arbitrary", "arbitrary")),
        name=f"attn{gi}",
    )(qkv_g, qkv_g, qkv_g, qkv_g, qkv_g)


def _sattn_body(qkv_ref, b1_ref, b2_ref, b3_ref, o_ref):
    outs, lses = [], []
    for g, (buf_ref, (_, dil)) in enumerate(zip((b1_ref, b2_ref, b3_ref), DILATED_GROUPS)):
        q = qkv_ref[0, g]
        kn = qkv_ref[0, N_GROUPS_A + g]
        vn = qkv_ref[0, 2 * N_GROUPS_A + g]
        kb = buf_ref[0, 0]
        vb = buf_ref[0, 1]
        wb = kb.shape[-1]
        pos = lax.broadcasted_iota(I32, (1, 1, wb), 2)
        s = jnp.sum(kb * q, axis=1, keepdims=True)
        s = jnp.where(pos % dil == 0, s, -jnp.inf)
        sn = jnp.sum(kn * q, axis=1, keepdims=True)
        m = jnp.maximum(jnp.max(s, axis=2, keepdims=True), sn)
        p = jnp.exp(s - m)
        pn = jnp.exp(sn - m)
        l = jnp.sum(p, axis=2, keepdims=True) + pn
        outs.append((jnp.sum(p * vb, axis=2, keepdims=True) + pn * vn) / l)
        lses.append(m + jnp.log(l))
    mx = jnp.maximum(jnp.maximum(lses[0], lses[1]), lses[2])
    es = [jnp.exp(z - mx) for z in lses]
    o_ref[0] = (es[0] * outs[0] + es[1] * outs[1] + es[2] * outs[2]) / (es[0] + es[1] + es[2])


def _sattn_call(qkv_s, c1, c2, c3):
    n = qkv_s.shape[0]
    views, specs = [], []
    for c in (c1, c2, c3):
        wb = c.shape[1]
        views.append(jnp.transpose(c, (0, 2, 3, 4, 1)))
        specs.append(pl.BlockSpec((1, 2, N_HEADS_A, HEAD_DIM_A, wb), lambda b: (b, 0, 0, 0, 0)))
    return pl.pallas_call(
        _sattn_body,
        out_shape=jax.ShapeDtypeStruct((n, N_HEADS_A, HEAD_DIM_A, 1), F32),
        grid=(n,),
        in_specs=[pl.BlockSpec((1, 3 * N_GROUPS_A, N_HEADS_A, HEAD_DIM_A, 1), lambda b: (b, 0, 0, 0, 0))] + specs,
        out_specs=pl.BlockSpec((1, N_HEADS_A, HEAD_DIM_A, 1), lambda b: (b, 0, 0, 0)),
        compiler_params=_cparams(("arbitrary",)),
        name="sattn",
    )(qkv_s, *views)


def _rwkv_features(xs, w0, ww2, a0, wa2, wg2, k_a):
    r = xs[:, :D_B]
    k = xs[:, D_B:2 * D_B]
    v = xs[:, 2 * D_B:3 * D_B]
    xw = xs[:, 3 * D_B:3 * D_B + DECAY_LORA]
    xa = xs[:, 3 * D_B + DECAY_LORA:3 * D_B + DECAY_LORA + AAA_LORA]
    xg = xs[:, 3 * D_B + DECAY_LORA + AAA_LORA:]
    w_log = -_softplus(-(w0 + _dot(jnp.tanh(xw).astype(BF16), ww2.astype(BF16)))) - 0.5
    a = _sigmoid(a0 + _dot(xa.astype(BF16), wa2.astype(BF16)))
    g = _dot(_sigmoid(xg).astype(BF16), wg2.astype(BF16))
    k_h = k * (1.0 + (a - 1.0) * k_a)
    return r, k, v, w_log, a, g, k_h


def _head_norm(kk_h):
    nrm = jnp.sqrt(jnp.sum(kk_h * kk_h, axis=-1, keepdims=True))
    return kk_h / jnp.maximum(nrm, 1e-12)


def _wkv_finish_head(y, r_h, k_h, v_h, g_h, rk_h, lnw_h, lnb_h):
    mean = jnp.mean(y, axis=-1, keepdims=True)
    var = jnp.mean(jnp.square(y - mean), axis=-1, keepdims=True)
    yn = (y - mean) * lax.rsqrt(var + LN_X_EPS) * lnw_h + lnb_h
    bonus = jnp.sum(r_h * k_h * rk_h, axis=-1, keepdims=True) * v_h
    return (yn + bonus) * g_h


def _wkv_body(f_ref, fp_ref, mu_ref, w0_ref, ww2_ref, a0_ref, wa2_ref, wg2_ref, kk_ref, ka_ref,
              rk_ref, lnw_ref, lnb_ref, o_ref, st_ref, s_ref):
    c = pl.program_id(0)
    C = WKV_CHUNK
    nb = f_ref.shape[0]

    @pl.when(c == 0)
    def _():
        s_ref[...] = jnp.zeros_like(s_ref)

    f = jnp.concatenate([f_ref[b] for b in range(nb)], axis=0)
    row = lax.broadcasted_iota(I32, f.shape, 0)
    prev = pltpu.roll(f, 1, 0)
    for b in range(nb):
        prev = jnp.where(row == b * C, jnp.where(c == 0, 0.0, fp_ref[b][7:8, :]), prev)
    xs = f + mu_ref[...] * (prev - f)
    r, k, v, w_log, a, g, k_h = _rwkv_features(xs, w0_ref[...], ww2_ref[...], a0_ref[...],
                                               wa2_ref[...], wg2_ref[...], ka_ref[...])
    lw = -jnp.exp(w_log)
    kk = k * kk_ref[...]
    jh = lax.broadcasted_iota(I32, (D_B, LANES), 0) // HEAD_DIM_B
    ind = (jh == lax.broadcasted_iota(I32, (D_B, LANES), 1)).astype(BF16)
    ind_t = (lax.broadcasted_iota(I32, (LANES, D_B), 0)
             == lax.broadcasted_iota(I32, (LANES, D_B), 1) // HEAD_DIM_B).astype(BF16)

    def head_sum(z):
        hi = z.astype(BF16)
        lo = (z - hi.astype(F32)).astype(BF16)
        s = _dot(hi, ind) + _dot(lo, ind)
        shi = s.astype(BF16)
        slo = (s - shi.astype(F32)).astype(BF16)
        return _dot(shi, ind_t) + _dot(slo, ind_t)

    kkn = kk / jnp.maximum(jnp.sqrt(head_sum(kk * kk)), 1e-12)

    tr = lax.broadcasted_iota(I32, (nb * C, nb * C), 0)
    sr_ = lax.broadcasted_iota(I32, (nb * C, nb * C), 1)
    tri_incl = ((tr >= sr_) & (tr // C == sr_ // C)).astype(BF16)
    l1 = lw.astype(BF16)
    r1 = lw - l1.astype(F32)
    l2 = r1.astype(BF16)
    l3 = (r1 - l2.astype(F32)).astype(BF16)
    cum = _dot(tri_incl, l1) + _dot(tri_incl, l2) + _dot(tri_incl, l3)
    rhos = [cum[b * C + C // 2 - 1:b * C + C // 2, :] for b in range(nb)]
    rho = jnp.concatenate([jnp.broadcast_to(z, (C, D_B)) for z in rhos], axis=0)
    ep = jnp.exp(cum - rho)
    em = jnp.exp(rho - cum)
    e_a = ep * jnp.exp(-lw)
    r_hat = r * ep
    k_hat = k_h * em
    e_rs = [jnp.exp(z) for z in rhos]
    e_cs = [jnp.exp(cum[b * C + C - 1:b * C + C, :] - rhos[b]) for b in range(nb)]

    ti = lax.broadcasted_iota(I32, (C, C), 0)
    si = lax.broadcasted_iota(I32, (C, C), 1)
    strict = ti > si
    incl = ti >= si
    rk = rk_ref[...]
    lnw = lnw_ref[...]
    lnb = lnb_ref[...]
    items = [(b, h) for b in range(nb) for h in range(N_HEADS_B)]
    heads = range(len(items))
    lanes = [slice(h * HEAD_DIM_B, (h + 1) * HEAD_DIM_B) for _, h in items]
    cut = lambda z, i: z[items[i][0] * C:(items[i][0] + 1) * C, lanes[i]]
    e_r = [e_rs[b][:, lanes[i]] for i, (b, _) in enumerate(items)]
    e_c = [e_cs[b][:, lanes[i]] for i, (b, _) in enumerate(items)]
    a_hat_full = (-kkn * e_a).astype(BF16)
    b_hat_full = (kkn * a * em).astype(BF16)
    a_hat_b = [cut(a_hat_full, h) for h in heads]
    b_hat_b = [cut(b_hat_full, h) for h in heads]
    rh = [cut(r_hat, h) for h in heads]
    vb = [cut(v, h).astype(BF16) for h in heads]
    bk = [jnp.concatenate([b_hat_b[h], cut(k_hat, h).astype(BF16)], axis=0) for h in heads]
    p = [_dot_nt(jnp.concatenate([a_hat_b[h], rh[h].astype(BF16)], axis=0), bk[h]) for h in heads]
    l_ak = [jnp.where(strict, z[:C, C:], 0.0).astype(BF16) for z in p]
    p_rb = [jnp.where(incl, z[C:, :C], 0.0).astype(BF16) for z in p]
    p_rk = [jnp.where(incl, z[C:, C:], 0.0).astype(BF16) for z in p]
    col = lax.broadcasted_iota(I32, (C, 2 * C), 1)
    row2 = lax.broadcasted_iota(I32, (C, 2 * C), 0)
    left = col < C
    zt = [jnp.where(left, jnp.where(row2 > col, z[:C], 0.0), (col == row2 + C).astype(F32)) for z in p]
    for _ in range(int(math.log2(C))):
        zb = [z.astype(BF16) for z in zt]
        res = [_dot(z[:, :C], z) for z in zb]
        zt = [jnp.where(left, res[h], zt[h] + res[h]) for h in heads]
    tb = [z.astype(BF16) for z in zt]
    zeros_c = jnp.zeros((C, HEAD_DIM_B), BF16)
    lv = [_dot(l_ak[h], vb[h]).astype(BF16) for h in heads]
    a_bar = [_dot(tb[h], jnp.concatenate([zeros_c, a_hat_b[h]], axis=0)).astype(BF16) for h in heads]
    u_v = [_dot(tb[h], jnp.concatenate([zeros_c, lv[h]], axis=0)).astype(BF16) for h in heads]
    r_bar = [rh[h] + _dot(p_rb[h], a_bar[h]) for h in heads]
    y_v = [_dot(p_rb[h], u_v[h]) + _dot(p_rk[h], vb[h]) for h in heads]
    ab = [_dot_tn(a_bar[h], b_hat_b[h]).astype(BF16) for h in heads]
    n_t = [_dot_tn(jnp.concatenate([u_v[h], vb[h]], axis=0), bk[h]) for h in heads]
    s0 = [s_ref[b, h] for b, h in items]
    sr = [s0[h] * e_r[h] for h in heads]
    y = [_dot_nt((r_bar[h] * e_r[h]).astype(BF16), s0[h].astype(BF16)) + y_v[h] for h in heads]
    s_new = [(sr[h] + _dot(sr[h].astype(BF16), ab[h]) + n_t[h]) * e_c[h] for h in heads]
    for i, (b, h) in enumerate(items):
        s_ref[b, h] = s_new[i]
    y_full = jnp.concatenate([jnp.concatenate(y[b * N_HEADS_B:(b + 1) * N_HEADS_B], axis=1) for b in range(nb)],
                             axis=0)
    inv_hd = 1.0 / HEAD_DIM_B
    dev = y_full - head_sum(y_full) * inv_hd
    yn = dev * lax.rsqrt(head_sum(dev * dev) * inv_hd + LN_X_EPS) * lnw + lnb
    out = (yn + head_sum(r * k_h * rk) * v) * g
    for b in range(nb):
        o_ref[b] = out[b * C:(b + 1) * C, :]

    @pl.when(c == pl.num_programs(0) - 1)
    def _():
        st_ref[...] = s_ref[...]


def _wkv_call(feat, p):
    b, t, _ = feat.shape
    C = WKV_CHUNK
    nc = t // C
    row = lambda n: pl.BlockSpec((1, n), lambda c: (0, 0))
    mat = lambda m, n: pl.BlockSpec((m, n), lambda c: (0, 0))
    return pl.pallas_call(
        _wkv_body,
        out_shape=(jax.ShapeDtypeStruct((b, t, D_B), F32),
                   jax.ShapeDtypeStruct((b, N_HEADS_B, HEAD_DIM_B, HEAD_DIM_B), F32)),
        grid=(nc,),
        in_specs=[pl.BlockSpec((b, C, D_SHIFT_B), lambda c: (0, c, 0)),
                  pl.BlockSpec((b, 8, D_SHIFT_B), lambda c: (0, jnp.maximum(c * (C // 8) - 1, 0), 0)),
                  row(D_SHIFT_B), row(D_B), mat(DECAY_LORA, D_B), row(D_B), mat(AAA_LORA, D_B),
                  mat(GATE_LORA, D_B), row(D_B), row(D_B), row(D_B), row(D_B), row(D_B)],
        out_specs=(pl.BlockSpec((b, C, D_B), lambda c: (0, c, 0)),
                   pl.BlockSpec((b, N_HEADS_B, HEAD_DIM_B, HEAD_DIM_B), lambda c: (0, 0, 0, 0))),
        scratch_shapes=[pltpu.VMEM((b, N_HEADS_B, HEAD_DIM_B, HEAD_DIM_B), F32)],
        compiler_params=_cparams(("arbitrary",)),
        name="wkv",
    )(feat, feat, p['mu_b'], p['w0_b'], p['w_w2_b'], p['a0_b'], p['w_a2_b'], p['w_g2_b'],
      p['k_k_b'], p['k_a_b'], p['r_k_b'], p['ln_x_w_b'], p['ln_x_b_b'])


def _swkv_prep_body(f_ref, sh_ref, mu_ref, w0_ref, ww2_ref, a0_ref, wa2_ref, wg2_ref, kk_ref, ka_ref,
                    r_ref, w_ref, k_ref, v_ref, aa_ref, bb_ref, g_ref):
    f = f_ref[...]
    xs = f + mu_ref[...] * (sh_ref[...] - f)
    r, k, v, w_log, a, g, k_h = _rwkv_features(xs, w0_ref[...], ww2_ref[...], a0_ref[...],
                                               wa2_ref[...], wg2_ref[...], ka_ref[...])
    kk = k * kk_ref[...]
    kkn = jnp.concatenate([_head_norm(kk[:, h * HEAD_DIM_B:(h + 1) * HEAD_DIM_B]) for h in range(N_HEADS_B)],
                          axis=1)
    r_ref[...] = r
    w_ref[...] = jnp.exp(-jnp.exp(w_log))
    k_ref[...] = k_h
    v_ref[...] = v
    aa_ref[...] = -kkn
    bb_ref[...] = kkn * a
    g_ref[...] = g


def _swkv_prep_call(feat_s, shift0, p):
    n = feat_s.shape[0]
    full = lambda a: pl.BlockSpec(a.shape, lambda: tuple(0 for _ in a.shape))
    args = (feat_s, shift0, p['mu_b'], p['w0_b'], p['w_w2_b'], p['a0_b'], p['w_a2_b'], p['w_g2_b'],
            p['k_k_b'], p['k_a_b'])
    return pl.pallas_call(
        _swkv_prep_body,
        out_shape=tuple(jax.ShapeDtypeStruct((n, D_B), F32) for _ in range(7)),
        in_specs=[full(a) for a in args],
        out_specs=tuple(pl.BlockSpec((n, D_B), lambda: (0, 0)) for _ in range(7)),
        compiler_params=pltpu.CompilerParams(vmem_limit_bytes=VMEM_LIMIT),
        name="swkv_prep",
    )(*args)


def _swkv_step_body(s_ref, a_ref, w_ref, b_ref, k_ref, r_ref, v_ref, so_ref, y_ref):
    s = s_ref[...]
    sa = jnp.sum(s * a_ref[...], axis=-1, keepdims=True)
    s2 = s * w_ref[...] + sa * b_ref[...] + v_ref[...] * k_ref[...]
    so_ref[...] = s2
    y_ref[...] = jnp.sum(s2 * r_ref[...], axis=-1, keepdims=True)


def _swkv_step_call(s0, aa, w, bb, k, r, v_col):
    nh = s0.shape[0]
    th = 64
    rowspec = pl.BlockSpec((th, 1, HEAD_DIM_B), lambda i: (i, 0, 0))
    colspec = pl.BlockSpec((th, HEAD_DIM_B, 1), lambda i: (i, 0, 0))
    stspec = pl.BlockSpec((th, HEAD_DIM_B, HEAD_DIM_B), lambda i: (i, 0, 0))
    return pl.pallas_call(
        _swkv_step_body,
        out_shape=(jax.ShapeDtypeStruct((nh, HEAD_DIM_B, HEAD_DIM_B), F32),
                   jax.ShapeDtypeStruct((nh, HEAD_DIM_B, 1), F32)),
        grid=(nh // th,),
        in_specs=[stspec, rowspec, rowspec, rowspec, rowspec, rowspec, colspec],
        out_specs=(stspec, colspec),
        compiler_params=_cparams(("arbitrary",)),
        name="swkv_step",
    )(s0, aa, w, bb, k, r, v_col)


def _swkv_fin_body(y_ref, r_ref, k_ref, v_ref, g_ref, rk_ref, lnw_ref, lnb_ref, o_ref):
    y, r, k, v, g = y_ref[...], r_ref[...], k_ref[...], v_ref[...], g_ref[...]
    rk, lnw, lnb = rk_ref[...], lnw_ref[...], lnb_ref[...]
    outs = []
    for h in range(N_HEADS_B):
        sl = slice(h * HEAD_DIM_B, (h + 1) * HEAD_DIM_B)
        outs.append(_wkv_finish_head(y[:, sl], r[:, sl], k[:, sl], v[:, sl], g[:, sl],
                                     rk[:, sl], lnw[:, sl], lnb[:, sl]))
    o_ref[...] = jnp.concatenate(outs, axis=1)


def _swkv_fin_call(y, r, k, v, g, p):
    n = y.shape[0]
    args = (y, r, k, v, g, p['r_k_b'], p['ln_x_w_b'], p['ln_x_b_b'])
    full = lambda a: pl.BlockSpec(a.shape, lambda: (0, 0))
    return pl.pallas_call(
        _swkv_fin_body,
        out_shape=jax.ShapeDtypeStruct((n, D_B), F32),
        in_specs=[full(a) for a in args],
        out_specs=pl.BlockSpec((n, D_B), lambda: (0, 0)),
        name="swkv_fin",
    )(*args)


def _route_t(scores, bias_col):
    n = scores.shape[1]
    gsz = N_EXPERTS // N_EXPERT_GROUPS
    choice = scores + bias_col
    ninf = -jnp.inf
    sid = lax.broadcasted_iota(I32, (gsz, n), 0)
    gs = []
    for gidx in range(N_EXPERT_GROUPS):
        blk = choice[gidx * gsz:(gidx + 1) * gsz, :]
        m1 = jnp.max(blk, axis=0, keepdims=True)
        first = jnp.min(jnp.where(blk == m1, sid, gsz), axis=0, keepdims=True)
        m2 = jnp.max(jnp.where(sid == first, ninf, blk), axis=0, keepdims=True)
        gs.append(m1 + m2)
    cur = jnp.concatenate(gs, axis=0)
    gid = lax.broadcasted_iota(I32, (N_EXPERT_GROUPS, n), 0)
    gmask = jnp.zeros((N_EXPERT_GROUPS, n), F32)
    for _ in range(TOPK_GROUPS):
        m = jnp.max(cur, axis=0, keepdims=True)
        first = jnp.min(jnp.where(cur == m, gid, N_EXPERT_GROUPS), axis=0, keepdims=True)
        sel = gid == first
        gmask = jnp.where(sel, 1.0, gmask)
        cur = jnp.where(sel, ninf, cur)
    emask = jnp.concatenate([jnp.broadcast_to(gmask[gidx:gidx + 1, :], (gsz, n))
                             for gidx in range(N_EXPERT_GROUPS)], axis=0)
    cur = jnp.where(emask > 0.5, choice, ninf)
    eid = lax.broadcasted_iota(I32, (N_EXPERTS, n), 0)
    selm = jnp.zeros((N_EXPERTS, n), F32)
    for _ in range(TOP_K):
        m = jnp.max(cur, axis=0, keepdims=True)
        first = jnp.min(jnp.where(cur == m, eid, N_EXPERTS), axis=0, keepdims=True)
        sel = eid == first
        selm = jnp.where(sel, 1.0, selm)
        cur = jnp.where(sel, ninf, cur)
    w = jnp.where(selm > 0.5, scores, 0.0)
    w = w / jnp.sum(w, axis=0, keepdims=True) * ROUTED_SCALE
    return jnp.where(selm > 0.5, w, -1.0)


def _unpermute(blk_ref, scr_ref, dil, tm):
    if dil == 1:
        return blk_ref[0, 0].astype(F32)
    n_chunks = scr_ref.shape[0]
    for r in range(dil):
        rows = blk_ref[0, r].astype(F32)
        for j in range(n_chunks):
            scr_ref[j, pl.ds(r, tm // dil, stride=dil), :] = rows[:, j * LANES:(j + 1) * LANES]
    return jnp.concatenate([scr_ref[j] for j in range(n_chunks)], axis=1)


def _post_body(*refs, combine, dils):
    if combine:
        o_refs, l_refs, rest = refs[:3], refs[3:6], refs[6:]
    else:
        o_refs, rest = refs[:1], refs[1:]
    (ob_ref, gt_ref, x_ref, g1_ref, sc2_ref, sh2_ref, npost_ref, npre_ref, wa_ref, wb_ref, wo_ref,
     wrt_ref, rb_ref, x1_ref, hp_ref, wt_ref) = rest[:16]
    scr = rest[16:]
    tm = x_ref.shape[1]
    if combine:
        os_, ls_ = [], []
        si = 0
        for gi, dil in enumerate(dils):
            os_.append(_unpermute(o_refs[gi], scr[si] if dil > 1 else None, dil, tm))
            ls_.append(_unpermute(l_refs[gi], scr[si + 1] if dil > 1 else None, dil, tm))
            si += 2 if dil > 1 else 0
        mx = jnp.maximum(jnp.maximum(ls_[0], ls_[1]), ls_[2])
        es = [jnp.exp(z - mx) for z in ls_]
        o_a = (es[0] * os_[0] + es[1] * os_[1] + es[2] * os_[2]) / (es[0] + es[1] + es[2])
    else:
        o_a = o_refs[0][0]
    gt = gt_ref[0].astype(F32)
    za = _dot(o_a.astype(BF16), wa_ref[...])
    zb = _dot(ob_ref[0].astype(BF16), wb_ref[...])
    merged = gt[:, :D_MODEL] * za + gt[:, D_MODEL:] * zb
    z = _dot(merged.astype(BF16), wo_ref[...])
    x1 = x_ref[0] + g1_ref[0] * _rms(z, npost_ref[...])
    x1_ref[0] = x1
    h2 = _rms(x1, npre_ref[...]) * (1.0 + sc2_ref[0]) + sh2_ref[0]
    packed = _pack_pairs(h2)
    for s in range(ROW_TILE_SUBLANES):
        hp_ref[0, pl.ds(s, tm, stride=ROW_TILE_SUBLANES), :] = packed[:, s * LANES:(s + 1) * LANES]
    tp =-(-tm // LANES) * LANES
    if tp != tm:
        h2 = jnp.concatenate([h2, jnp.zeros((tp - tm, D_MODEL), F32)], axis=0)
    logits_t = _dot_nt_split(wrt_ref[...], h2)
    w = _route_t(_sigmoid(logits_t[:N_EXPERTS, :]), rb_ref[...])
    wt_ref[...] = w[:, :tm]


def _post_call(o_parts, lse_parts, ob, gates, x, gate1, scale2, shift2, p, wa, wb, wo, wrt, rb, tm, mod_per_row):
    nb, t, _ = x.shape
    nt = t // tm
    combine = lse_parts is not None
    rowblk = lambda width: pl.BlockSpec((1, tm, width), lambda b, i: (b, i, 0))
    if mod_per_row:
        mod_spec = rowblk(D_MODEL)
    else:
        mod_spec = pl.BlockSpec((1, 1, D_MODEL), lambda b, i: (b, 0, 0))
    const = lambda shp: pl.BlockSpec(shp, lambda b, i: (0, 0))
    scratch = []
    if combine:
        dils = tuple(o.shape[1] for o in o_parts)
        o_args = list(o_parts) + list(lse_parts)
        o_specs = [pl.BlockSpec((1, d, tm // d, D_GROUP_A), lambda b, i: (b, 0, i, 0)) for d in dils] * 2
        for d in dils:
            if d > 1:
                scratch += [pltpu.VMEM((D_GROUP_A // LANES, tm, LANES), F32)] * 2
    else:
        dils = ()
        o_args = [o_parts[0]]
        o_specs = [rowblk(D_GROUP_A)]
    return pl.pallas_call(
        functools.partial(_post_body, combine=combine, dils=dils),
        out_shape=(jax.ShapeDtypeStruct((nb, t, D_MODEL), F32),
                   jax.ShapeDtypeStruct((nb, t * ROW_TILE_SUBLANES, LANES), I32),
                   jax.ShapeDtypeStruct((N_EXPERTS, nb * t), F32)),
        grid=(nb, nt),
        in_specs=o_specs + [rowblk(D_B), rowblk(2 * D_MODEL), rowblk(D_MODEL),
                            mod_spec, mod_spec, mod_spec, const((1, D_MODEL)), const((1, D_MODEL)),
                            const((D_GROUP_A, D_MODEL)), const((D_B, D_MODEL)), const((D_MODEL, D_MODEL)),
                            const((LANES, D_MODEL)), const((N_EXPERTS, 1))],
        out_specs=(rowblk(D_MODEL),
                   pl.BlockSpec((1, tm * ROW_TILE_SUBLANES, LANES), lambda b, i: (b, i, 0)),
                   pl.BlockSpec((N_EXPERTS, tm), lambda b, i: (0, b * nt + i))),
        scratch_shapes=scratch,
        compiler_params=_cparams(("arbitrary", "arbitrary")),
        name="post",
    )(*o_args, ob, gates, x, gate1, scale2, shift2, p['norm_post_mix'].reshape(1, -1),
      p['norm_pre_ffn'].reshape(1, -1), wa, wb, wo, wrt, rb)


def _rank_body(w_ref, dest_ref, w8_ref, tab_ref, etab_ref, cnt_ref, pst_ref, run_ref, *, n_real, n_slots):
    ph = pl.program_id(0)
    i = pl.program_id(1)
    T = MOE_TILE
    w = w_ref[...]
    sel = (w >= 0.0).astype(F32)
    cnt_tile = jnp.broadcast_to(jnp.sum(sel, axis=1, keepdims=True), (N_EXPERTS, LANES))
    ei = lax.broadcasted_iota(I32, (N_EXPERTS, N_EXPERTS), 0)
    ej = lax.broadcasted_iota(I32, (N_EXPERTS, N_EXPERTS), 1)

    @pl.when((ph == 0) & (i == 0))
    def _():
        cnt_ref[...] = jnp.zeros_like(cnt_ref)

    @pl.when(ph == 0)
    def _():
        cnt_ref[...] += cnt_tile

    @pl.when((ph == 1) & (i == 0))
    def _():
        cnt = cnt_ref[...]
        padded = jnp.floor((cnt + (EXPERT_BLOCK - 1)) / EXPERT_BLOCK) * EXPERT_BLOCK
        pstart = _dot_exact((ej < ei).astype(F32), padded)
        pst_ref[...] = pstart
        run_ref[...] = jnp.zeros_like(run_ref)
        pend = pstart + padded
        vend = pstart + cnt
        esub = lax.broadcasted_iota(I32, (N_EXPERTS, LANES), 0)
        lane = lax.broadcasted_iota(I32, (1, LANES), 1)
        tab_ref[...] = jnp.zeros_like(tab_ref)
        for c in range(tab_ref.shape[1] // LANES):
            bs = ((c * LANES + lane) * EXPERT_BLOCK).astype(F32)
            be = jnp.minimum(jnp.sum((pend <= bs).astype(F32), axis=0, keepdims=True), N_EXPERTS - 1.0)
            tab_ref[0:1, c * LANES:(c + 1) * LANES] = be.astype(I32)
            tab_ref[1:2, c * LANES:(c + 1) * LANES] = (pend[N_EXPERTS - 1:, :] / EXPERT_BLOCK).astype(I32)
        on_diag = esub == lax.broadcasted_iota(I32, (N_EXPERTS, LANES), 1)
        etab_ref[...] = jnp.zeros_like(etab_ref)
        lo = jnp.sum(jnp.where(on_diag, vend, 0.0), axis=0, keepdims=True)
        hi = jnp.sum(jnp.where(on_diag, pend, 0.0), axis=0, keepdims=True)
        etab_ref[0:1, :] = jnp.where(lane == N_EXPERTS, pend[N_EXPERTS - 1:, :], lo).astype(I32)
        etab_ref[1:2, :] = jnp.where(lane == N_EXPERTS, float(n_slots), hi).astype(I32)

    @pl.when(ph == 1)
    def _():
        ti = lax.broadcasted_iota(I32, (T, T), 0)
        tj = lax.broadcasted_iota(I32, (T, T), 1)
        selb = sel.astype(BF16)
        rank = _dot(selb, (ti < tj).astype(BF16))
        ordn = _dot((ej < ei).astype(BF16), selb)
        dest_e = pst_ref[:, :1] + run_ref[:, :1] + rank
        run_ref[...] += cnt_tile
        tok = i * T + lax.broadcasted_iota(I32, (1, T), 1)
        dks, wks = [], []
        for k in range(TOP_K):
            m = (sel > 0.5) & (ordn == float(k))
            dk = jnp.sum(jnp.where(m, dest_e, 0.0), axis=0, keepdims=True)
            wk = jnp.sum(jnp.where(m, w, 0.0), axis=0, keepdims=True)
            dks.append(jnp.where(tok < n_real, dk, 0.0))
            wks.append(jnp.where(tok < n_real, wk, 0.0))
        dest_ref[...] = jnp.concatenate(dks, axis=0).astype(I32)
        w8_ref[...] = jnp.concatenate(wks, axis=0)


def _rank_call(w_t, n_real, n_blocks, n_blocks_pad):
    n = w_t.shape[1]
    nt = n // MOE_TILE
    return pl.pallas_call(
        functools.partial(_rank_body, n_real=n_real, n_slots=n_blocks * EXPERT_BLOCK),
        out_shape=(jax.ShapeDtypeStruct((TOP_K, n), I32),
                   jax.ShapeDtypeStruct((TOP_K, n), F32),
                   jax.ShapeDtypeStruct((8, n_blocks_pad), I32),
                   jax.ShapeDtypeStruct((8, LANES), I32)),
        grid=(2, nt),
        in_specs=[pl.BlockSpec((N_EXPERTS, MOE_TILE), lambda ph, i: (0, i))],
        out_specs=(pl.BlockSpec((TOP_K, MOE_TILE), lambda ph, i: (0, i * ph)),
                   pl.BlockSpec((TOP_K, MOE_TILE), lambda ph, i: (0, i * ph)),
                   pl.BlockSpec((8, n_blocks_pad), lambda ph, i: (0, 0)),
                   pl.BlockSpec((8, LANES), lambda ph, i: (0, 0))),
        scratch_shapes=[pltpu.VMEM((N_EXPERTS, LANES), F32)] * 3,
        compiler_params=_cparams(("arbitrary", "arbitrary")),
        name="rank",
    )(w_t)


def _tile_rows(ref, row, n):
    return ref.at[pl.ds(pl.multiple_of(row * ROW_TILE_SUBLANES, ROW_TILE_SUBLANES), n * ROW_TILE_SUBLANES)]


def _zero_fill(etab_ref, zbuf, xs_hbm, zsem, wait):
    def go(src, dst):
        cp = pltpu.make_async_copy(src, dst, zsem)
        if wait:
            cp.wait()
        else:
            cp.start()

    def per_range(e, carry):
        lo = etab_ref[0, e]
        n = etab_ref[1, e] - lo
        n_full = n // ZERO_ROWS

        def full(j, c):
            go(zbuf, _tile_rows(xs_hbm, lo + j * ZERO_ROWS, ZERO_ROWS))
            return c

        lax.fori_loop(0, n_full, full, 0)
        pos = lo + n_full * ZERO_ROWS
        rem = n - n_full * ZERO_ROWS
        size = ZERO_ROWS // 2
        while size >= 1:
            bit = rem & size

            @pl.when(bit != 0)
            def _(size=size, pos=pos):
                go(_tile_rows(zbuf, 0, size), _tile_rows(xs_hbm, pos, size))

            pos = pos + bit
            size //= 2
        return carry

    lax.fori_loop(0, N_EXPERTS + 1, per_range, 0)


def _dispatch_body(dest_ref, etab_ref, xa_ref, xb_ref, xs_hbm, zbuf, sem, zsem, *, n_real, n_full):
    i = pl.program_id(0)
    T = MOE_TILE
    n_tok = jnp.clip(n_real - i * T, 0, T)

    def issue_from(x_ref):
        def issue(t, carry):
            for k in range(TOP_K):
                pltpu.make_async_copy(_tile_rows(x_ref, t, 1), _tile_rows(xs_hbm, dest_ref[k * T + t], 1),
                                      sem).start(priority=k % 2)
            return carry

        lax.fori_loop(0, n_tok, issue, 0)

    @pl.when(i < n_full)
    def _():
        issue_from(xa_ref)

    @pl.when(i >= n_full)
    def _():
        issue_from(xb_ref)

    @pl.when(i == 0)
    def _():
        zbuf[...] = jnp.zeros_like(zbuf)
        _zero_fill(etab_ref, zbuf, xs_hbm, zsem, wait=False)
        _zero_fill(etab_ref, zbuf, xs_hbm, zsem, wait=True)

    @pl.when(n_tok == T)
    def _():
        pltpu.make_async_copy(_tile_rows(xs_hbm, 0, T * TOP_K), _tile_rows(xs_hbm, 0, T * TOP_K), sem).wait()

    @pl.when(n_tok < T)
    def _():
        def drain(j, carry):
            pltpu.make_async_copy(_tile_rows(xs_hbm, 0, 1), _tile_rows(xs_hbm, 0, 1), sem).wait()
            return carry

        lax.fori_loop(0, n_tok * TOP_K, drain, 0)


def _dispatch_call(dest, etab, hp_a, hp_b, n_real, n_slots):
    tile_rows = MOE_TILE * ROW_TILE_SUBLANES
    n_full = hp_a.shape[0] // tile_rows
    return pl.pallas_call(
        functools.partial(_dispatch_body, n_real=n_real, n_full=n_full),
        out_shape=jax.ShapeDtypeStruct((n_slots * ROW_TILE_SUBLANES, LANES), I32),
        grid=(n_full + 1,),
        in_specs=[pl.BlockSpec((TOP_K * MOE_TILE,), lambda i: (i,), memory_space=pltpu.SMEM),
                  pl.BlockSpec((8, LANES), lambda i: (0, 0), memory_space=pltpu.SMEM),
                  pl.BlockSpec((tile_rows, LANES), lambda i: (jnp.minimum(i, n_full - 1), 0)),
                  pl.BlockSpec((tile_rows, LANES), lambda i: (0, 0))],
        out_specs=pl.BlockSpec(memory_space=pl.ANY),
        scratch_shapes=[pltpu.VMEM((ZERO_ROWS * ROW_TILE_SUBLANES, LANES), I32),
                        pltpu.SemaphoreType.DMA, pltpu.SemaphoreType.DMA],
        compiler_params=_cparams(("arbitrary",)),
        name="dispatch",
    )(dest, etab, hp_a, hp_b)


def _rows_from_tiles(ref, lo, n):
    return jnp.concatenate([ref[pl.ds(lo * ROW_TILE_SUBLANES + s, n, stride=ROW_TILE_SUBLANES), :]
                            for s in range(ROW_TILE_SUBLANES)], axis=1)


def _ffn_body(be_ref, nu_ref, xs_ref, wg_ref, wu_ref, wd_ref, ys_ref, wgb, wub, wdb):
    j = pl.program_id(0)

    @pl.when(j < nu_ref[0])
    def _():
        @pl.when((j == 0) | (be_ref[j] != be_ref[jnp.maximum(j - 1, 0)]))
        def _():
            wgb[...] = wg_ref[0].astype(BF16)
            wub[...] = wu_ref[0].astype(BF16)
            wdb[...] = wd_ref[0].astype(BF16)

        x = _unpack_pairs(_rows_from_tiles(xs_ref, 0, EXPERT_BLOCK)).astype(BF16)
        act = _silu(_dot(x, wgb[...])) * _dot(x, wub[...])
        y = _dot(act.astype(BF16), wdb[...])
        packed = _pack_pairs(y)
        for s in range(ROW_TILE_SUBLANES):
            ys_ref[pl.ds(s, EXPERT_BLOCK, stride=ROW_TILE_SUBLANES), :] = packed[:, s * LANES:(s + 1) * LANES]

    @pl.when(j >= nu_ref[0])
    def _():
        ys_ref[...] = jnp.zeros_like(ys_ref)


def _ffn_call(blk_e, n_used, xs, w_gate, w_up, w_down, n_blocks):
    tile_blk = pl.BlockSpec((EXPERT_BLOCK * ROW_TILE_SUBLANES, LANES), lambda j, be, nu: (j, 0))
    last = lambda j, nu: jnp.minimum(j, nu[0] - 1)
    grid_spec = pltpu.PrefetchScalarGridSpec(
        num_scalar_prefetch=2,
        grid=(n_blocks,),
        in_specs=[pl.BlockSpec((EXPERT_BLOCK * ROW_TILE_SUBLANES, LANES), lambda j, be, nu: (last(j, nu), 0)),
                  pl.BlockSpec((1, D_MODEL, D_EXPERT), lambda j, be, nu: (be[last(j, nu)], 0, 0)),
                  pl.BlockSpec((1, D_MODEL, D_EXPERT), lambda j, be, nu: (be[last(j, nu)], 0, 0)),
                  pl.BlockSpec((1, D_EXPERT, D_MODEL), lambda j, be, nu: (be[last(j, nu)], 0, 0))],
        out_specs=tile_blk,
        scratch_shapes=[pltpu.VMEM((D_MODEL, D_EXPERT), BF16), pltpu.VMEM((D_MODEL, D_EXPERT), BF16),
                        pltpu.VMEM((D_EXPERT, D_MODEL), BF16)])
    return pl.pallas_call(
        _ffn_body,
        out_shape=jax.ShapeDtypeStruct((n_blocks * EXPERT_BLOCK * ROW_TILE_SUBLANES, LANES), I32),
        grid_spec=grid_spec,
        compiler_params=_cparams(("arbitrary",)),
        name="ffn",
    )(blk_e, n_used, xs, w_gate, w_up, w_down)


def _combine_body(dest_ref, dnext_ref, w8_ref, xa_ref, xb_ref, x1a_ref, x1b_ref, g2a_ref, g2b_ref, gain_ref,
                  sg_ref, su_ref, sd_ref, ys_hbm, oa_ref, ob_ref, buf, sem):
    j = pl.program_id(0)
    T = COMBINE_TILE
    RC = COMBINE_ROWS

    def gather(d_ref, slot):
        def issue(t, carry):
            for k in range(TOP_K):
                pltpu.make_async_copy(_tile_rows(ys_hbm, d_ref[k * T + t], 1), _tile_rows(buf.at[slot], k * T + t, 1),
                                      sem.at[slot]).start(priority=k % 2)
            return carry

        lax.fori_loop(0, T, issue, 0, unroll=2)

    def step(slot):
        @pl.when(j + 1 < pl.num_programs(0))
        def _():
            gather(dnext_ref, 1 - slot)

        is_tail = j == 0
        x = _unpack_pairs(jnp.where(is_tail, _rows_from_tiles(xb_ref, 0, T),
                                    _rows_from_tiles(xa_ref, 0, T))).astype(BF16)
        shared = _dot((_silu(_dot(x, sg_ref[...])) * _dot(x, su_ref[...])).astype(BF16), sd_ref[...])
        w_t = jnp.concatenate([w8_ref[...], jnp.zeros((LANES - TOP_K, T), F32)], axis=0).T
        oa_ref[...] = shared
        pltpu.make_async_copy(_tile_rows(ys_hbm, 0, T * TOP_K), buf.at[slot], sem.at[slot]).wait()
        for r0 in range(0, T, RC):
            acc = oa_ref[r0:r0 + RC, :]
            for k in range(TOP_K):
                acc = acc + w_t[r0:r0 + RC, k:k + 1] * _unpack_pairs(_rows_from_tiles(buf.at[slot], k * T + r0, RC))
            x1 = jnp.where(is_tail, x1b_ref[r0:r0 + RC, :], x1a_ref[r0:r0 + RC, :])
            g2 = jnp.where(is_tail, g2b_ref[r0:r0 + RC, :], g2a_ref[0])
            oa_ref[r0:r0 + RC, :] = x1 + g2 * _rms(acc, gain_ref[...])

        @pl.when(is_tail)
        def _():
            ob_ref[...] = oa_ref[...]

    @pl.when(j == 0)
    def _():
        gather(dest_ref, 0)

    @pl.when(j % 2 == 0)
    def _():
        step(0)

    @pl.when(j % 2 == 1)
    def _():
        step(1)


def _combine_call(dest, w8, hp_a, hp_b, x1_a, x1_b, gate2_a, gate2_b, gain, wsg, wsu, wsd, ys):
    T = COMBINE_TILE
    tile_rows = T * ROW_TILE_SUBLANES
    n_full = hp_a.shape[0] // tile_rows
    n_tiles = n_full + 1
    seq = x1_a.shape[0] // gate2_a.shape[0]
    tile_of = lambda j: jnp.where(j == 0, n_full, j - 1)
    full_of = lambda j: jnp.maximum(j - 1, 0)
    const = lambda shp: pl.BlockSpec(shp, lambda j: (0, 0))
    return pl.pallas_call(
        _combine_body,
        out_shape=(jax.ShapeDtypeStruct((n_full * T, D_MODEL), F32), jax.ShapeDtypeStruct((T, D_MODEL), F32)),
        grid=(n_tiles,),
        in_specs=[pl.BlockSpec((TOP_K * T,), lambda j: (tile_of(j),), memory_space=pltpu.SMEM),
                  pl.BlockSpec((TOP_K * T,), lambda j: (tile_of(jnp.minimum(j + 1, n_tiles - 1)),),
                               memory_space=pltpu.SMEM),
                  pl.BlockSpec((TOP_K, T), lambda j: (0, tile_of(j))),
                  pl.BlockSpec((tile_rows, LANES), lambda j: (full_of(j), 0)),
                  pl.BlockSpec((tile_rows, LANES), lambda j: (0, 0)),
                  pl.BlockSpec((T, D_MODEL), lambda j: (full_of(j), 0)),
                  const((T, D_MODEL)),
                  pl.BlockSpec((1, 1, D_MODEL), lambda j: (full_of(j) * T // seq, 0, 0)),
                  const((T, D_MODEL)), const((1, D_MODEL)),
                  const((D_MODEL, D_EXPERT)), const((D_MODEL, D_EXPERT)), const((D_EXPERT, D_MODEL)),
                  pl.BlockSpec(memory_space=pl.ANY)],
        out_specs=(pl.BlockSpec((T, D_MODEL), lambda j: (full_of(j), 0)), const((T, D_MODEL))),
        scratch_shapes=[pltpu.VMEM((2, TOP_K * tile_rows, LANES), I32), pltpu.SemaphoreType.DMA((2,))],
        compiler_params=_cparams(("arbitrary",)),
        name="combine",
    )(dest, dest, w8, hp_a, hp_b, x1_a, x1_b, gate2_a, gate2_b, gain.reshape(1, -1), wsg, wsu, wsd, ys)


def _rope_tables(pos):
    half = HEAD_DIM_A // 2
    inv_freq = ROPE_THETA ** (-jnp.arange(half, dtype=F32) / half)
    ang = pos.astype(F32)[:, None] * inv_freq[None, :]
    cos = jnp.cos(ang)
    sin = jnp.sin(ang)
    reps = LANES // HEAD_DIM_A
    cos_t = jnp.tile(jnp.concatenate([cos, cos], axis=1), (1, reps))
    sin_t = jnp.tile(jnp.concatenate([-sin, sin], axis=1), (1, reps))
    return cos_t, sin_t


def _cache_from_tail(tail, keep):
    outs = []
    n, rows, _ = tail.shape
    for gi, kp in enumerate(keep):
        k = tail[:, rows - kp:, gi * D_GROUP_A:(gi + 1) * D_GROUP_A]
        v = tail[:, rows - kp:, D_A + gi * D_GROUP_A:D_A + (gi + 1) * D_GROUP_A]
        outs.append(jnp.stack([k, v], axis=2).reshape(n, kp, 2, N_HEADS_A, HEAD_DIM_A))
    return outs


def kernel(x_prompt, x_sample, c_prompt, c_sample, cache_a1_kv, cache_a2_kv, cache_a3_kv, state_b_wkv, state_b_shift, w_ada, b_ada, norm_pre_mix, norm_post_mix, norm_pre_ffn, norm_post_ffn, w_in, w_a_out, mu_b, w0_b, w_w2_b, a0_b, w_a2_b, w_g2_b, k_k_b, k_a_b, r_k_b, ln_x_w_b, ln_x_b_b, w_b_out, w_out, w_router, router_bias, w_e_gate, w_e_up, w_e_down, w_s_gate, w_s_up, w_s_down):
    assert DEPTH == 1
    l = 0
    nd = DEC_BATCH
    row = lambda a: a.reshape(1, -1)
    p = {'mu_b': row(mu_b[l]), 'w0_b': row(w0_b[l]), 'w_w2_b': w_w2_b[l], 'a0_b': row(a0_b[l]),
         'w_a2_b': w_a2_b[l], 'w_g2_b': w_g2_b[l], 'k_k_b': row(k_k_b[l]), 'k_a_b': row(k_a_b[l]),
         'r_k_b': row(r_k_b[l]), 'ln_x_w_b': row(ln_x_w_b[l]), 'ln_x_b_b': row(ln_x_b_b[l]),
         'norm_post_mix': norm_post_mix[l], 'norm_pre_ffn': norm_pre_ffn[l]}

    wq = w_in[l][:, :D_QKV].astype(BF16)
    wf = w_in[l][:, D_QKV:D_QKV + D_SHIFT_B].astype(BF16)
    wg = w_in[l][:, D_QKV + D_SHIFT_B:].astype(BF16)
    wa = w_a_out[l].astype(BF16)
    wb = w_b_out[l].astype(BF16)
    wo = w_out[l].astype(BF16)
    wrt = jnp.concatenate([w_router[l].T, jnp.zeros((LANES - N_EXPERTS, D_MODEL), F32)], axis=0)
    rb = router_bias[l].reshape(N_EXPERTS, 1)
    wsg, wsu, wsd = w_s_gate[l].astype(BF16), w_s_up[l].astype(BF16), w_s_down[l].astype(BF16)

    n_c = BATCH + nd
    c_all = jnp.concatenate([c_prompt, c_sample, jnp.zeros((-n_c % 8, D_MODEL), F32)], axis=0)
    mod = _mod_call(c_all, w_ada[l], b_ada[l])
    mod_p = [m.reshape(BATCH, 1, D_MODEL) for m in jnp.split(mod[:BATCH], 6, axis=-1)]
    mod_s = [m.reshape(1, nd, D_MODEL) for m in jnp.split(mod[BATCH:n_c], 6, axis=-1)]

    cos_p, sin_p = _rope_tables(jnp.arange(SEQ, dtype=I32))
    cos_s, sin_s = _rope_tables(jnp.full((nd,), PAST_LEN, I32))

    keep_p = [min(w, SEQ) for w, _ in DILATED_GROUPS]
    tail_rows = max(keep_p)
    dils = tuple(d for _, d in DILATED_GROUPS)

    q0, q1, q2, feat_p, gates_p, tail_p = _inproj_call(
        x_prompt, norm_pre_mix[l], mod_p[1], mod_p[0], cos_p, sin_p, wq, wf, wg,
        tm=256, tail_rows=tail_rows, mod_per_row=False, dils=dils)
    o_parts, lse_parts = [], []
    for gi, qg in enumerate((q0, q1, q2)):
        o, lse = _attn_call(qg, gi)
        o_parts.append(o)
        lse_parts.append(lse)
    ob_p, wkv_p = _wkv_call(feat_p, p)
    x1_p, hp_p, wt_p = _post_call(o_parts, lse_parts, ob_p, gates_p, x_prompt, mod_p[2], mod_p[4], mod_p[3],
                                  p, wa, wb, wo, wrt, rb, tm=512, mod_per_row=False)

    xs3 = x_sample.reshape(1, nd, D_MODEL)
    s0, s1, s2, feat_s, gates_s, tail_s = _inproj_call(
        xs3, norm_pre_mix[l], mod_s[1], mod_s[0], cos_s, sin_s, wq, wf, wg,
        tm=nd, tail_rows=nd, mod_per_row=True, dils=(1, 1, 1))
    qkv_s = jnp.stack([z.reshape(nd, 3, N_HEADS_A, HEAD_DIM_A) for z in (s0, s1, s2)], axis=2)
    qkv_s = qkv_s.reshape(nd, 3 * N_GROUPS_A, N_HEADS_A, HEAD_DIM_A, 1).astype(F32)
    oa_s = _sattn_call(qkv_s, cache_a1_kv[l], cache_a2_kv[l], cache_a3_kv[l])
    r_s, w_s, k_s, v_s, aa_s, bb_s, g_s = _swkv_prep_call(feat_s[0], state_b_shift[l], p)
    nh = nd * N_HEADS_B
    as_row = lambda a: a.reshape(nh, 1, HEAD_DIM_B)
    s_new, y_col = _swkv_step_call(state_b_wkv[l].reshape(nh, HEAD_DIM_B, HEAD_DIM_B), as_row(aa_s), as_row(w_s),
                                   as_row(bb_s), as_row(k_s), as_row(r_s), v_s.reshape(nh, HEAD_DIM_B, 1))
    ob_s = _swkv_fin_call(y_col.reshape(nd, D_B), r_s, k_s, v_s, g_s, p)
    x1_s, hp_s, wt_s = _post_call([oa_s.reshape(1, nd, D_GROUP_A)], None, ob_s.reshape(1, nd, D_B), gates_s, xs3,
                                  mod_s[2], mod_s[4], mod_s[3], p, wa, wb, wo, wrt, rb, tm=nd, mod_per_row=True)

    n_p = BATCH * SEQ
    n_real = n_p + nd
    n_all = -(-n_real // MOE_TILE) * MOE_TILE
    pad = n_all - n_real
    n_blocks = -(-(n_real * TOP_K) // EXPERT_BLOCK) + N_EXPERTS
    n_blocks_pad = -(-n_blocks // LANES) * LANES
    assert n_p % MOE_TILE == 0 and nd <= MOE_TILE
    hp_a = hp_p.reshape(n_p * ROW_TILE_SUBLANES, LANES)
    hp_b = jnp.concatenate([hp_s[0], jnp.zeros((pad * ROW_TILE_SUBLANES, LANES), I32)], axis=0)
    wt_all = jnp.concatenate([wt_p, wt_s, jnp.full((N_EXPERTS, pad), -1.0, F32)], axis=1)
    dest8, w8, tab, etab = _rank_call(wt_all, n_real, n_blocks, n_blocks_pad)
    dest = dest8.reshape(TOP_K, n_all // MOE_TILE, MOE_TILE).transpose(1, 0, 2).reshape(-1)
    xs = _dispatch_call(dest, etab, hp_a, hp_b, n_real, n_blocks * EXPERT_BLOCK)
    ys = _ffn_call(tab[0], tab[1, :1], xs, w_e_gate[l], w_e_up[l], w_e_down[l], n_blocks)
    n_ct = n_p // COMBINE_TILE + 1
    dest_c = dest8[:, :n_ct * COMBINE_TILE].reshape(TOP_K, n_ct, COMBINE_TILE).transpose(1, 0, 2).reshape(-1)
    pad_rows = lambda z: jnp.concatenate([z, jnp.zeros((COMBINE_TILE - nd, D_MODEL), F32)], axis=0)
    out_p, out_s = _combine_call(dest_c, w8, hp_a, hp_b, x1_p.reshape(n_p, D_MODEL), pad_rows(x1_s[0]), mod_p[5],
                                 pad_rows(mod_s[5][0]), norm_post_ffn[l], wsg, wsu, wsd, ys)
    y_prompt = out_p.reshape(BATCH, SEQ, D_MODEL)
    y_sample = out_s[:nd]

    a_p = [z[None] for z in _cache_from_tail(tail_p, keep_p)]
    a_s = [z.reshape(1, nd, DEC_SEQ, 2, N_HEADS_A, HEAD_DIM_A)
           for z in _cache_from_tail(tail_s.reshape(nd, 1, 2 * D_A), [DEC_SEQ] * N_GROUPS_A)]
    shift_p = feat_p[:, -1][None]
    shift_s = feat_s[0][None]
    return (y_prompt, y_sample.reshape(nd, DEC_SEQ, D_MODEL), a_p[0], a_p[1], a_p[2], wkv_p[None], shift_p,
            a_s[0], a_s[1], a_s[2], s_new.reshape(1, nd, N_HEADS_B, HEAD_DIM_B, HEAD_DIM_B), shift_s)
```

```python
import functools
import math

import jax
import jax.numpy as jnp
from jax import lax
from jax.experimental import pallas as pl
from jax.experimental.pallas import tpu as pltpu

F32 = jnp.float32
BF16 = jnp.bfloat16
I32 = jnp.int32

D_MODEL = 1024
BATCH = 2
SEQ = 8192
DEPTH = 1
DEC_BATCH = 32
DEC_SEQ = 1
PAST_LEN = 16384

HEAD_DIM_A = 64
N_HEADS_A = 8
DILATED_GROUPS = ((128, 1), (512, 4), (2048, 16))
N_GROUPS_A = 3
D_GROUP_A = N_HEADS_A * HEAD_DIM_A
D_A = N_GROUPS_A * D_GROUP_A
D_QKV = 3 * D_A
BAND_BLOCK = 128
ROPE_THETA = 10000.0

HEAD_DIM_B = 64
N_HEADS_B = 16
D_B = 1024
DECAY_LORA = 64
AAA_LORA = 64
GATE_LORA = 160
D_SHIFT_B = 3 * D_B + DECAY_LORA + AAA_LORA + GATE_LORA
LN_X_EPS = 64e-5

N_EXPERTS = 64
TOP_K = 8
N_EXPERT_GROUPS = 8
TOPK_GROUPS = 4
D_EXPERT = 256
ROUTED_SCALE = 2.5
EXPERT_BLOCK = 512
NORM_EPS = 1e-6

LANES = 128
WKV_CHUNK = 64
MOE_TILE = 1024
COMBINE_TILE = 256
COMBINE_ROWS = 32
VMEM_LIMIT = 56 * 1024 * 1024
ROW_TILE_SUBLANES = D_MODEL // (2 * LANES)
ZERO_ROWS = 256


def _cparams(sem):
    return pltpu.CompilerParams(dimension_semantics=sem, vmem_limit_bytes=VMEM_LIMIT)


def _dot(a, b):
    return jnp.dot(a, b, preferred_element_type=F32)


def _dot_nt(a, b):
    return lax.dot_general(a, b, (((1,), (1,)), ((), ())), preferred_element_type=F32)


def _dot_tn(a, b):
    return lax.dot_general(a, b, (((0,), (0,)), ((), ())), preferred_element_type=F32)


def _dot_nt_split(a, b):
    ah = a.astype(BF16)
    al = (a - ah.astype(F32)).astype(BF16)
    bh = b.astype(BF16)
    bl = (b - bh.astype(F32)).astype(BF16)
    return _dot_nt(ah, bh) + _dot_nt(ah, bl) + _dot_nt(al, bh)


def _dot_exact(a, b):
    return lax.dot_general(a, b, (((1,), (0,)), ((), ())), precision=lax.Precision.HIGHEST,
                           preferred_element_type=F32)


def _rms(x, gain):
    return x * lax.rsqrt(jnp.mean(x * x, axis=-1, keepdims=True) + NORM_EPS) * gain


def _sigmoid(x):
    return 1.0 / (1.0 + jnp.exp(-x))


def _silu(x):
    return x * _sigmoid(x)


def _softplus(x):
    return jnp.maximum(x, 0.0) + jnp.log(1.0 + jnp.exp(-jnp.abs(x)))


def _pack_pairs(x):
    half = D_MODEL // 2
    lo = lax.bitcast_convert_type(x[:, :half].astype(BF16).astype(F32), I32)
    hi = lax.bitcast_convert_type(x[:, half:].astype(BF16).astype(F32), I32)
    return lax.shift_right_logical(lo, 16) | (hi & jnp.int32(-65536))


def _unpack_pairs(w):
    lo = lax.bitcast_convert_type(w << 16, F32)
    hi = lax.bitcast_convert_type(w & jnp.int32(-65536), F32)
    return jnp.concatenate([lo, hi], axis=1)


def _mod_body(c_ref, w_ref, b_ref, o_ref):
    s = _silu(c_ref[...]).astype(BF16)
    o_ref[...] = _dot(s, w_ref[...].astype(BF16)) + b_ref[...]


def _mod_call(c_all, w_ada, b_ada):
    rows = c_all.shape[0]
    tn = 1536
    return pl.pallas_call(
        _mod_body,
        out_shape=jax.ShapeDtypeStruct((rows, 6 * D_MODEL), F32),
        grid=(6 * D_MODEL // tn,),
        in_specs=[pl.BlockSpec((rows, D_MODEL), lambda j: (0, 0)),
                  pl.BlockSpec((D_MODEL, tn), lambda j: (0, j)),
                  pl.BlockSpec((1, tn), lambda j: (0, j))],
        out_specs=pl.BlockSpec((rows, tn), lambda j: (0, j)),
        compiler_params=_cparams(("arbitrary",)),
        name="mod",
    )(c_all, w_ada, b_ada.reshape(1, -1))


def _inproj_body(x_ref, g_ref, sc_ref, sh_ref, cos_ref, sin_ref, wq_ref, wf_ref, wg_ref,
                 q0_ref, q1_ref, q2_ref, feat_ref, gate_ref, tail_ref, p_ref, *, dils):
    x = x_ref[0]
    tm = x.shape[0]
    h = _rms(x, g_ref[...]) * (1.0 + sc_ref[0]) + sh_ref[0]
    hb = h.astype(BF16)
    p = _dot(hb, wq_ref[...])
    cos = cos_ref[...]
    sin = sin_ref[...]
    lane = lax.broadcasted_iota(I32, cos.shape, 1)
    first_half = (lane % HEAD_DIM_A) < (HEAD_DIM_A // 2)
    for c in range(2 * D_A // LANES):
        xc = p[:, c * LANES:(c + 1) * LANES]
        partner = jnp.where(first_half, pltpu.roll(xc, LANES - HEAD_DIM_A // 2, 1),
                            pltpu.roll(xc, HEAD_DIM_A // 2, 1))
        rc = xc * cos + partner * sin
        if c < D_A // LANES:
            rc = rc * (HEAD_DIM_A ** -0.5)
        p_ref[c] = rc
        if c >= D_A // LANES:
            tail_ref[0, :, (c - D_A // LANES) * LANES:(c - D_A // LANES + 1) * LANES] = rc
    for c in range(2 * D_A // LANES, D_QKV // LANES):
        p_ref[c] = p[:, c * LANES:(c + 1) * LANES]
    tail_ref[0, :, D_A:] = p[:, 2 * D_A:]
    per_group = D_GROUP_A // LANES
    for gi, (out_ref, dil) in enumerate(zip((q0_ref, q1_ref, q2_ref), dils)):
        for which in range(3):
            for j in range(per_group):
                c = (which * D_A + gi * D_GROUP_A) // LANES + j
                dst = slice(which * D_GROUP_A + j * LANES, which * D_GROUP_A + (j + 1) * LANES)
                if dil == 1:
                    out_ref[0, 0, :, dst] = p_ref[c].astype(BF16)
                else:
                    for r in range(dil):
                        out_ref[0, r, :, dst] = p_ref[c, pl.ds(r, tm // dil, stride=dil), :].astype(BF16)
    feat_ref[0] = _dot(hb, wf_ref[...])
    gate_ref[0] = _sigmoid(_dot(hb, wg_ref[...])).astype(BF16)


def _inproj_call(x, gain, scale, shift, cos_t, sin_t, wq, wf, wg, tm, tail_rows, mod_per_row, dils):
    nb, t, _ = x.shape
    nt = t // tm
    tail_first = (t - tail_rows) // tm
    if mod_per_row:
        mod_spec = pl.BlockSpec((1, tm, D_MODEL), lambda b, i: (b, i, 0))
    else:
        mod_spec = pl.BlockSpec((1, 1, D_MODEL), lambda b, i: (b, 0, 0))
    resident = lambda shp: pl.BlockSpec(shp, lambda b, i: (0, 0), pipeline_mode=pl.Buffered(1))
    q_shapes = tuple(jax.ShapeDtypeStruct((nb, d, t // d, 3 * D_GROUP_A), BF16) for d in dils)
    q_specs = tuple(pl.BlockSpec((1, d, tm // d, 3 * D_GROUP_A), lambda b, i: (b, 0, i, 0)) for d in dils)
    return pl.pallas_call(
        functools.partial(_inproj_body, dils=dils),
        out_shape=q_shapes + (jax.ShapeDtypeStruct((nb, t, D_SHIFT_B), F32),
                              jax.ShapeDtypeStruct((nb, t, 2 * D_MODEL), BF16),
                              jax.ShapeDtypeStruct((nb, tail_rows, 2 * D_A), F32)),
        grid=(nb, nt),
        in_specs=[pl.BlockSpec((1, tm, D_MODEL), lambda b, i: (b, i, 0)),
                  pl.BlockSpec((1, D_MODEL), lambda b, i: (0, 0)),
                  mod_spec, mod_spec,
                  pl.BlockSpec((tm, LANES), lambda b, i: (i, 0)),
                  pl.BlockSpec((tm, LANES), lambda b, i: (i, 0)),
                  resident((D_MODEL, D_QKV)), resident((D_MODEL, D_SHIFT_B)),
                  resident((D_MODEL, 2 * D_MODEL))],
        out_specs=q_specs + (pl.BlockSpec((1, tm, D_SHIFT_B), lambda b, i: (b, i, 0)),
                             pl.BlockSpec((1, tm, 2 * D_MODEL), lambda b, i: (b, i, 0)),
                             pl.BlockSpec((1, tm, 2 * D_A), lambda b, i: (b, jnp.maximum(i - tail_first, 0), 0))),
        scratch_shapes=[pltpu.VMEM((D_QKV // LANES, tm, LANES), F32)],
        compiler_params=_cparams(("arbitrary", "arbitrary")),
        name="inproj",
    )(x, gain.reshape(1, -1), scale, shift, cos_t, sin_t, wq, wf, wg)


def _attn_body(q_ref, kc_ref, kp_ref, vc_ref, vp_ref, o_ref, lse_ref):
    mb = pl.program_id(2)
    nq = q_ref.shape[2] // BAND_BLOCK
    q = q_ref[0, 0]
    k = jnp.concatenate([kp_ref[0, 0], kc_ref[0, 0]], axis=0)
    v = jnp.concatenate([vp_ref[0, 0], vc_ref[0, 0]], axis=0)
    qi = lax.broadcasted_iota(I32, (BAND_BLOCK, 2 * BAND_BLOCK), 0)
    ki = lax.broadcasted_iota(I32, (BAND_BLOCK, 2 * BAND_BLOCK), 1)
    dist = qi + BAND_BLOCK - ki
    band = (dist >= 0) & (dist <= BAND_BLOCK)
    masks = [band & ((ki >= BAND_BLOCK) | (mb > 0))] + [band] * (nq - 1)
    lane_q = lax.broadcasted_iota(I32, (BAND_BLOCK, LANES), 1)
    lane_k = lax.broadcasted_iota(I32, (2 * BAND_BLOCK, LANES), 1)
    for hp in range(N_HEADS_A // 2):
        sl = slice(hp * LANES, (hp + 1) * LANES)
        chains = [(j, sub) for j in range(nq) for sub in range(2)]
        qs = [q[j * BAND_BLOCK:(j + 1) * BAND_BLOCK, sl] for j in range(nq)]
        ks = [k[j * BAND_BLOCK:(j + 2) * BAND_BLOCK, sl] for j in range(nq)]
        vs = [v[j * BAND_BLOCK:(j + 2) * BAND_BLOCK, sl] for j in range(nq)]
        mqs = [lane_q < HEAD_DIM_A, lane_q >= HEAD_DIM_A]
        mks = [lane_k < HEAD_DIM_A, lane_k >= HEAD_DIM_A]
        s = [jnp.where(masks[j], _dot_nt(jnp.where(mqs[sub], qs[j], jnp.zeros_like(qs[j])), ks[j]), -jnp.inf)
             for j, sub in chains]
        mx = [jnp.max(z, axis=1, keepdims=True) for z in s]
        p = [jnp.exp(z - m) for z, m in zip(s, mx)]
        l = [jnp.sum(z, axis=1, keepdims=True) for z in p]
        pv = [_dot(p[c].astype(BF16), jnp.where(mks[sub], vs[j], jnp.zeros_like(vs[j])))
              for c, (j, sub) in enumerate(chains)]
        for j in range(nq):
            c0, c1 = 2 * j, 2 * j + 1
            o_pair = pv[c0] / l[c0] + pv[c1] / l[c1]
            lse_pair = jnp.where(mqs[0], mx[c0] + jnp.log(l[c0]), mx[c1] + jnp.log(l[c1]))
            o_ref[0, 0, j * BAND_BLOCK:(j + 1) * BAND_BLOCK, sl] = o_pair.astype(BF16)
            lse_ref[0, 0, j * BAND_BLOCK:(j + 1) * BAND_BLOCK, sl] = lse_pair


def _attn_call(qkv_g, gi):
    b, dil, l, _ = qkv_g.shape
    nq = 4
    nb = l // (nq * BAND_BLOCK)
    blk = (1, 1, nq * BAND_BLOCK, D_GROUP_A)
    cur = lambda which: pl.BlockSpec(blk, lambda bb, r, m: (bb, r, m, which))
    prev = lambda which: pl.BlockSpec((1, 1, BAND_BLOCK, D_GROUP_A),
                                      lambda bb, r, m: (bb, r, jnp.maximum(nq * m - 1, 0), which))
    return pl.pallas_call(
        _attn_body,
        out_shape=(jax.ShapeDtypeStruct((b, dil, l, D_GROUP_A), BF16),
                   jax.ShapeDtypeStruct((b, dil, l, D_GROUP_A), F32)),
        grid=(b, dil, nb),
        in_specs=[cur(0), cur(1), prev(1), cur(2), prev(2)],
        out_specs=(pl.BlockSpec(blk, lambda bb, r, m: (bb, r, m, 0)),
                   pl.BlockSpec(blk, lambda bb, r, m: (bb, r, m, 0))),
        compiler_params=_cparams(("arbitrary", "arbitrary", "arbitrary")),
        name=f"attn{gi}",
    )(qkv_g, qkv_g, qkv_g, qkv_g, qkv_g)


def _sattn_body(qkv_ref, b1_ref, b2_ref, b3_ref, o_ref):
    n_rows = 3 * N_GROUPS_A * N_HEADS_A
    sq = jnp.concatenate([qkv_ref[0], jnp.zeros((LANES - n_rows, HEAD_DIM_A), F32)], axis=0)
    cols = jnp.concatenate([sq, jnp.zeros((LANES, LANES - HEAD_DIM_A), F32)], axis=1).T
    col3 = lambda first: jnp.stack([cols[:HEAD_DIM_A, first + h:first + h + 1] for h in range(N_HEADS_A)], axis=0)
    outs, lses = [], []
    for g, (buf_ref, (_, dil)) in enumerate(zip((b1_ref, b2_ref, b3_ref), DILATED_GROUPS)):
        q = col3(g * N_HEADS_A)
        kn = col3((N_GROUPS_A + g) * N_HEADS_A)
        vn = col3((2 * N_GROUPS_A + g) * N_HEADS_A)
        kb = buf_ref[0, 0]
        vb = buf_ref[0, 1]
        wb = kb.shape[-1]
        pos = lax.broadcasted_iota(I32, (1, 1, wb), 2)
        s = jnp.sum(kb * q, axis=1, keepdims=True)
        s = jnp.where(pos % dil == 0, s, -jnp.inf)
        sn = jnp.sum(kn * q, axis=1, keepdims=True)
        m = jnp.maximum(jnp.max(s, axis=2, keepdims=True), sn)
        p = jnp.exp(s - m)
        pn = jnp.exp(sn - m)
        l = jnp.sum(p, axis=2, keepdims=True) + pn
        outs.append((jnp.sum(p * vb, axis=2, keepdims=True) + pn * vn) / l)
        lses.append(m + jnp.log(l))
    mx = jnp.maximum(jnp.maximum(lses[0], lses[1]), lses[2])
    es = [jnp.exp(z - mx) for z in lses]
    o_a = (es[0] * outs[0] + es[1] * outs[1] + es[2] * outs[2]) / (es[0] + es[1] + es[2])
    o_cols = jnp.concatenate([o_a[h] for h in range(N_HEADS_A)] +
                             [jnp.zeros((HEAD_DIM_A, LANES - N_HEADS_A), F32)], axis=1)
    o_rows = jnp.concatenate([o_cols, jnp.zeros((LANES - HEAD_DIM_A, LANES), F32)], axis=0).T
    o_ref[0] = o_rows[:N_HEADS_A, :HEAD_DIM_A]


def _sattn_call(qkv_s, c1, c2, c3):
    n = qkv_s.shape[0]
    views, specs = [], []
    for c in (c1, c2, c3):
        wb = c.shape[1]
        views.append(jnp.transpose(c, (0, 2, 3, 4, 1)))
        specs.append(pl.BlockSpec((1, 2, N_HEADS_A, HEAD_DIM_A, wb), lambda b: (b, 0, 0, 0, 0)))
    return pl.pallas_call(
        _sattn_body,
        out_shape=jax.ShapeDtypeStruct((n, N_HEADS_A, HEAD_DIM_A), F32),
        grid=(n,),
        in_specs=[pl.BlockSpec((1, 3 * N_GROUPS_A * N_HEADS_A, HEAD_DIM_A), lambda b: (b, 0, 0))] + specs,
        out_specs=pl.BlockSpec((1, N_HEADS_A, HEAD_DIM_A), lambda b: (b, 0, 0)),
        compiler_params=_cparams(("arbitrary",)),
        name="sattn",
    )(qkv_s, *views)


def _rwkv_features(xs, w0, ww2, a0, wa2, wg2, k_a):
    r = xs[:, :D_B]
    k = xs[:, D_B:2 * D_B]
    v = xs[:, 2 * D_B:3 * D_B]
    xw = xs[:, 3 * D_B:3 * D_B + DECAY_LORA]
    xa = xs[:, 3 * D_B + DECAY_LORA:3 * D_B + DECAY_LORA + AAA_LORA]
    xg = xs[:, 3 * D_B + DECAY_LORA + AAA_LORA:]
    w_log = -_softplus(-(w0 + _dot(jnp.tanh(xw).astype(BF16), ww2.astype(BF16)))) - 0.5
    a = _sigmoid(a0 + _dot(xa.astype(BF16), wa2.astype(BF16)))
    g = _dot(_sigmoid(xg).astype(BF16), wg2.astype(BF16))
    k_h = k * (1.0 + (a - 1.0) * k_a)
    return r, k, v, w_log, a, g, k_h


def _head_norm(kk_h):
    nrm = jnp.sqrt(jnp.sum(kk_h * kk_h, axis=-1, keepdims=True))
    return kk_h / jnp.maximum(nrm, 1e-12)


def _wkv_finish_head(y, r_h, k_h, v_h, g_h, rk_h, lnw_h, lnb_h):
    mean = jnp.mean(y, axis=-1, keepdims=True)
    var = jnp.mean(jnp.square(y - mean), axis=-1, keepdims=True)
    yn = (y - mean) * lax.rsqrt(var + LN_X_EPS) * lnw_h + lnb_h
    bonus = jnp.sum(r_h * k_h * rk_h, axis=-1, keepdims=True) * v_h
    return (yn + bonus) * g_h


def _wkv_body(f_ref, fp_ref, mu_ref, w0_ref, ww2_ref, a0_ref, wa2_ref, wg2_ref, kk_ref, ka_ref,
              rk_ref, lnw_ref, lnb_ref, o_ref, st_ref, s_ref):
    c = pl.program_id(0)
    C = WKV_CHUNK
    nb = f_ref.shape[0]

    @pl.when(c == 0)
    def _():
        s_ref[...] = jnp.zeros_like(s_ref)

    f = jnp.concatenate([f_ref[b] for b in range(nb)], axis=0)
    row = lax.broadcasted_iota(I32, f.shape, 0)
    prev = pltpu.roll(f, 1, 0)
    for b in range(nb):
        prev = jnp.where(row == b * C, jnp.where(c == 0, 0.0, fp_ref[b][7:8, :]), prev)
    xs = f + mu_ref[...] * (prev - f)
    r, k, v, w_log, a, g, k_h = _rwkv_features(xs, w0_ref[...], ww2_ref[...], a0_ref[...],
                                               wa2_ref[...], wg2_ref[...], ka_ref[...])
    lw = -jnp.exp(w_log)
    kk = k * kk_ref[...]
    jh = lax.broadcasted_iota(I32, (D_B, LANES), 0) // HEAD_DIM_B
    ind = (jh == lax.broadcasted_iota(I32, (D_B, LANES), 1)).astype(BF16)
    ind_t = (lax.broadcasted_iota(I32, (LANES, D_B), 0)
             == lax.broadcasted_iota(I32, (LANES, D_B), 1) // HEAD_DIM_B).astype(BF16)

    def head_sum(z):
        hi = z.astype(BF16)
        lo = (z - hi.astype(F32)).astype(BF16)
        s = _dot(hi, ind) + _dot(lo, ind)
        shi = s.astype(BF16)
        slo = (s - shi.astype(F32)).astype(BF16)
        return _dot(shi, ind_t) + _dot(slo, ind_t)

    kkn = kk / jnp.maximum(jnp.sqrt(head_sum(kk * kk)), 1e-12)

    tr = lax.broadcasted_iota(I32, (nb * C, nb * C), 0)
    sr_ = lax.broadcasted_iota(I32, (nb * C, nb * C), 1)
    tri_incl = ((tr >= sr_) & (tr // C == sr_ // C)).astype(BF16)
    l1 = lw.astype(BF16)
    r1 = lw - l1.astype(F32)
    l2 = r1.astype(BF16)
    l3 = (r1 - l2.astype(F32)).astype(BF16)
    cum = _dot(tri_incl, l1) + _dot(tri_incl, l2) + _dot(tri_incl, l3)
    rhos = [cum[b * C + C // 2 - 1:b * C + C // 2, :] for b in range(nb)]
    rho = jnp.concatenate([jnp.broadcast_to(z, (C, D_B)) for z in rhos], axis=0)
    ep = jnp.exp(cum - rho)
    em = jnp.exp(rho - cum)
    e_a = ep * jnp.exp(-lw)
    r_hat = r * ep
    k_hat = k_h * em
    e_rs = [jnp.exp(z) for z in rhos]
    e_cs = [jnp.exp(cum[b * C + C - 1:b * C + C, :] - rhos[b]) for b in range(nb)]

    ti = lax.broadcasted_iota(I32, (C, C), 0)
    si = lax.broadcasted_iota(I32, (C, C), 1)
    strict = ti > si
    incl = ti >= si
    rk = rk_ref[...]
    lnw = lnw_ref[...]
    lnb = lnb_ref[...]
    items = [(b, h) for b in range(nb) for h in range(N_HEADS_B)]
    heads = range(len(items))
    lanes = [slice(h * HEAD_DIM_B, (h + 1) * HEAD_DIM_B) for _, h in items]
    cut = lambda z, i: z[items[i][0] * C:(items[i][0] + 1) * C, lanes[i]]
    e_r = [e_rs[b][:, lanes[i]] for i, (b, _) in enumerate(items)]
    e_c = [e_cs[b][:, lanes[i]] for i, (b, _) in enumerate(items)]
    a_hat_full = (-kkn * e_a).astype(BF16)
    b_hat_full = (kkn * a * em).astype(BF16)
    a_hat_b = [cut(a_hat_full, h) for h in heads]
    b_hat_b = [cut(b_hat_full, h) for h in heads]
    rh = [cut(r_hat, h) for h in heads]
    vb = [cut(v, h).astype(BF16) for h in heads]
    bk = [jnp.concatenate([b_hat_b[h], cut(k_hat, h).astype(BF16)], axis=0) for h in heads]
    p = [_dot_nt(jnp.concatenate([a_hat_b[h], rh[h].astype(BF16)], axis=0), bk[h]) for h in heads]
    l_ak = [jnp.where(strict, z[:C, C:], 0.0).astype(BF16) for z in p]
    p_rb = [jnp.where(incl, z[C:, :C], 0.0).astype(BF16) for z in p]
    p_rk = [jnp.where(incl, z[C:, C:], 0.0).astype(BF16) for z in p]
    col = lax.broadcasted_iota(I32, (C, 2 * C), 1)
    row2 = lax.broadcasted_iota(I32, (C, 2 * C), 0)
    left = col < C
    zt = [jnp.where(left, jnp.where(row2 > col, z[:C], 0.0), (col == row2 + C).astype(F32)) for z in p]
    for _ in range(int(math.log2(C))):
        zb = [z.astype(BF16) for z in zt]
        res = [_dot(z[:, :C], z) for z in zb]
        zt = [jnp.where(left, res[h], zt[h] + res[h]) for h in heads]
    tb = [z.astype(BF16) for z in zt]
    zeros_c = jnp.zeros((C, HEAD_DIM_B), BF16)
    lv = [_dot(l_ak[h], vb[h]).astype(BF16) for h in heads]
    a_bar = [_dot(tb[h], jnp.concatenate([zeros_c, a_hat_b[h]], axis=0)).astype(BF16) for h in heads]
    u_v = [_dot(tb[h], jnp.concatenate([zeros_c, lv[h]], axis=0)).astype(BF16) for h in heads]
    r_bar = [rh[h] + _dot(p_rb[h], a_bar[h]) for h in heads]
    y_v = [_dot(p_rb[h], u_v[h]) + _dot(p_rk[h], vb[h]) for h in heads]
    ab = [_dot_tn(a_bar[h], b_hat_b[h]).astype(BF16) for h in heads]
    n_t = [_dot_tn(jnp.concatenate([u_v[h], vb[h]], axis=0), bk[h]) for h in heads]
    s0 = [s_ref[b, h] for b, h in items]
    sr = [s0[h] * e_r[h] for h in heads]
    y = [_dot_nt((r_bar[h] * e_r[h]).astype(BF16), s0[h].astype(BF16)) + y_v[h] for h in heads]
    s_new = [(sr[h] + _dot(sr[h].astype(BF16), ab[h]) + n_t[h]) * e_c[h] for h in heads]
    for i, (b, h) in enumerate(items):
        s_ref[b, h] = s_new[i]
    y_full = jnp.concatenate([jnp.concatenate(y[b * N_HEADS_B:(b + 1) * N_HEADS_B], axis=1) for b in range(nb)],
                             axis=0)
    inv_hd = 1.0 / HEAD_DIM_B
    dev = y_full - head_sum(y_full) * inv_hd
    yn = dev * lax.rsqrt(head_sum(dev * dev) * inv_hd + LN_X_EPS) * lnw + lnb
    out = (yn + head_sum(r * k_h * rk) * v) * g
    for b in range(nb):
        o_ref[b] = out[b * C:(b + 1) * C, :]

    @pl.when(c == pl.num_programs(0) - 1)
    def _():
        st_ref[...] = s_ref[...]


def _wkv_call(feat, p):
    b, t, _ = feat.shape
    C = WKV_CHUNK
    nc = t // C
    row = lambda n: pl.BlockSpec((1, n), lambda c: (0, 0))
    mat = lambda m, n: pl.BlockSpec((m, n), lambda c: (0, 0))
    return pl.pallas_call(
        _wkv_body,
        out_shape=(jax.ShapeDtypeStruct((b, t, D_B), F32),
                   jax.ShapeDtypeStruct((b, N_HEADS_B, HEAD_DIM_B, HEAD_DIM_B), F32)),
        grid=(nc,),
        in_specs=[pl.BlockSpec((b, C, D_SHIFT_B), lambda c: (0, c, 0)),
                  pl.BlockSpec((b, 8, D_SHIFT_B), lambda c: (0, jnp.maximum(c * (C // 8) - 1, 0), 0)),
                  row(D_SHIFT_B), row(D_B), mat(DECAY_LORA, D_B), row(D_B), mat(AAA_LORA, D_B),
                  mat(GATE_LORA, D_B), row(D_B), row(D_B), row(D_B), row(D_B), row(D_B)],
        out_specs=(pl.BlockSpec((b, C, D_B), lambda c: (0, c, 0)),
                   pl.BlockSpec((b, N_HEADS_B, HEAD_DIM_B, HEAD_DIM_B), lambda c: (0, 0, 0, 0))),
        scratch_shapes=[pltpu.VMEM((b, N_HEADS_B, HEAD_DIM_B, HEAD_DIM_B), F32)],
        compiler_params=_cparams(("arbitrary",)),
        name="wkv",
    )(feat, feat, p['mu_b'], p['w0_b'], p['w_w2_b'], p['a0_b'], p['w_a2_b'], p['w_g2_b'],
      p['k_k_b'], p['k_a_b'], p['r_k_b'], p['ln_x_w_b'], p['ln_x_b_b'])


def _swkv_prep_body(f_ref, sh_ref, mu_ref, w0_ref, ww2_ref, a0_ref, wa2_ref, wg2_ref, kk_ref, ka_ref,
                    r_ref, w_ref, k_ref, v_ref, aa_ref, bb_ref, g_ref):
    f = f_ref[...]
    xs = f + mu_ref[...] * (sh_ref[...] - f)
    r, k, v, w_log, a, g, k_h = _rwkv_features(xs, w0_ref[...], ww2_ref[...], a0_ref[...],
                                               wa2_ref[...], wg2_ref[...], ka_ref[...])
    kk = k * kk_ref[...]
    kkn = jnp.concatenate([_head_norm(kk[:, h * HEAD_DIM_B:(h + 1) * HEAD_DIM_B]) for h in range(N_HEADS_B)],
                          axis=1)
    r_ref[...] = r
    w_ref[...] = jnp.exp(-jnp.exp(w_log))
    k_ref[...] = k_h
    v_ref[...] = v
    aa_ref[...] = -kkn
    bb_ref[...] = kkn * a
    g_ref[...] = g


def _swkv_prep_call(feat_s, shift0, p):
    n = feat_s.shape[0]
    full = lambda a: pl.BlockSpec(a.shape, lambda: tuple(0 for _ in a.shape))
    args = (feat_s, shift0, p['mu_b'], p['w0_b'], p['w_w2_b'], p['a0_b'], p['w_a2_b'], p['w_g2_b'],
            p['k_k_b'], p['k_a_b'])
    return pl.pallas_call(
        _swkv_prep_body,
        out_shape=tuple(jax.ShapeDtypeStruct((n, D_B), F32) for _ in range(7)),
        in_specs=[full(a) for a in args],
        out_specs=tuple(pl.BlockSpec((n, D_B), lambda: (0, 0)) for _ in range(7)),
        compiler_params=pltpu.CompilerParams(vmem_limit_bytes=VMEM_LIMIT),
        name="swkv_prep",
    )(*args)


def _swkv_step_body(s_ref, a_ref, w_ref, b_ref, k_ref, r_ref, v_ref, so_ref, y_ref):
    s = s_ref[...]
    th = s.shape[0]
    pad_sq = lambda z: jnp.concatenate(
        [jnp.concatenate([z, jnp.zeros((z.shape[0], LANES - z.shape[1]), F32)], axis=1),
         jnp.zeros((LANES - z.shape[0], LANES), F32)], axis=0)
    v_t = pad_sq(v_ref[...]).T
    v_col = jnp.stack([v_t[:HEAD_DIM_B, j:j + 1] for j in range(th)], axis=0)
    sa = jnp.sum(s * a_ref[...], axis=-1, keepdims=True)
    s2 = s * w_ref[...] + sa * b_ref[...] + v_col * k_ref[...]
    so_ref[...] = s2
    y = jnp.sum(s2 * r_ref[...], axis=-1, keepdims=True)
    y_t = jnp.concatenate([y[j] for j in range(th)], axis=1)
    y_ref[...] = pad_sq(y_t).T[:th, :HEAD_DIM_B]


def _swkv_step_call(s0, aa, w, bb, k, r, v):
    nh = s0.shape[0]
    th = 64
    rowspec = pl.BlockSpec((th, 1, HEAD_DIM_B), lambda i: (i, 0, 0))
    matspec = pl.BlockSpec((th, HEAD_DIM_B), lambda i: (i, 0))
    stspec = pl.BlockSpec((th, HEAD_DIM_B, HEAD_DIM_B), lambda i: (i, 0, 0))
    return pl.pallas_call(
        _swkv_step_body,
        out_shape=(jax.ShapeDtypeStruct((nh, HEAD_DIM_B, HEAD_DIM_B), F32),
                   jax.ShapeDtypeStruct((nh, HEAD_DIM_B), F32)),
        grid=(nh // th,),
        in_specs=[stspec, rowspec, rowspec, rowspec, rowspec, rowspec, matspec],
        out_specs=(stspec, matspec),
        compiler_params=_cparams(("arbitrary",)),
        name="swkv_step",
    )(s0, aa, w, bb, k, r, v)


def _swkv_fin_body(y_ref, r_ref, k_ref, v_ref, g_ref, rk_ref, lnw_ref, lnb_ref, o_ref):
    y, r, k, v, g = y_ref[...], r_ref[...], k_ref[...], v_ref[...], g_ref[...]
    rk, lnw, lnb = rk_ref[...], lnw_ref[...], lnb_ref[...]
    outs = []
    for h in range(N_HEADS_B):
        sl = slice(h * HEAD_DIM_B, (h + 1) * HEAD_DIM_B)
        outs.append(_wkv_finish_head(y[:, sl], r[:, sl], k[:, sl], v[:, sl], g[:, sl],
                                     rk[:, sl], lnw[:, sl], lnb[:, sl]))
    o_ref[...] = jnp.concatenate(outs, axis=1)


def _swkv_fin_call(y, r, k, v, g, p):
    n = y.shape[0]
    args = (y, r, k, v, g, p['r_k_b'], p['ln_x_w_b'], p['ln_x_b_b'])
    full = lambda a: pl.BlockSpec(a.shape, lambda: (0, 0))
    return pl.pallas_call(
        _swkv_fin_body,
        out_shape=jax.ShapeDtypeStruct((n, D_B), F32),
        in_specs=[full(a) for a in args],
        out_specs=pl.BlockSpec((n, D_B), lambda: (0, 0)),
        name="swkv_fin",
    )(*args)


def _route_t(scores, bias_col):
    n = scores.shape[1]
    gsz = N_EXPERTS // N_EXPERT_GROUPS
    choice = scores + bias_col
    ninf = -jnp.inf
    sid = lax.broadcasted_iota(I32, (gsz, n), 0)
    gs = []
    for gidx in range(N_EXPERT_GROUPS):
        blk = choice[gidx * gsz:(gidx + 1) * gsz, :]
        m1 = jnp.max(blk, axis=0, keepdims=True)
        first = jnp.min(jnp.where(blk == m1, sid, gsz), axis=0, keepdims=True)
        m2 = jnp.max(jnp.where(sid == first, ninf, blk), axis=0, keepdims=True)
        gs.append(m1 + m2)
    cur = jnp.concatenate(gs, axis=0)
    gid = lax.broadcasted_iota(I32, (N_EXPERT_GROUPS, n), 0)
    gmask = jnp.zeros((N_EXPERT_GROUPS, n), F32)
    for _ in range(TOPK_GROUPS):
        m = jnp.max(cur, axis=0, keepdims=True)
        first = jnp.min(jnp.where(cur == m, gid, N_EXPERT_GROUPS), axis=0, keepdims=True)
        sel = gid == first
        gmask = jnp.where(sel, 1.0, gmask)
        cur = jnp.where(sel, ninf, cur)
    emask = jnp.concatenate([jnp.broadcast_to(gmask[gidx:gidx + 1, :], (gsz, n))
                             for gidx in range(N_EXPERT_GROUPS)], axis=0)
    cur = jnp.where(emask > 0.5, choice, ninf)
    eid = lax.broadcasted_iota(I32, (N_EXPERTS, n), 0)
    selm = jnp.zeros((N_EXPERTS, n), F32)
    for _ in range(TOP_K):
        m = jnp.max(cur, axis=0, keepdims=True)
        first = jnp.min(jnp.where(cur == m, eid, N_EXPERTS), axis=0, keepdims=True)
        sel = eid == first
        selm = jnp.where(sel, 1.0, selm)
        cur = jnp.where(sel, ninf, cur)
    w = jnp.where(selm > 0.5, scores, 0.0)
    w = w / jnp.sum(w, axis=0, keepdims=True) * ROUTED_SCALE
    return jnp.where(selm > 0.5, w, -1.0)


def _unpermute(blk_ref, scr_ref, dil, tm):
    if dil == 1:
        return blk_ref[0, 0].astype(F32)
    n_chunks = scr_ref.shape[0]
    for r in range(dil):
        rows = blk_ref[0, r].astype(F32)
        for j in range(n_chunks):
            scr_ref[j, pl.ds(r, tm // dil, stride=dil), :] = rows[:, j * LANES:(j + 1) * LANES]
    return jnp.concatenate([scr_ref[j] for j in range(n_chunks)], axis=1)


def _post_body(*refs, combine, dils):
    if combine:
        o_refs, l_refs, rest = refs[:3], refs[3:6], refs[6:]
    else:
        o_refs, rest = refs[:1], refs[1:]
    (ob_ref, gt_ref, x_ref, g1_ref, sc2_ref, sh2_ref, npost_ref, npre_ref, wa_ref, wb_ref, wo_ref,
     wrt_ref, rb_ref, x1_ref, hp_ref, wt_ref) = rest[:16]
    scr = rest[16:]
    tm = x_ref.shape[1]
    if combine:
        os_, ls_ = [], []
        si = 0
        for gi, dil in enumerate(dils):
            os_.append(_unpermute(o_refs[gi], scr[si] if dil > 1 else None, dil, tm))
            ls_.append(_unpermute(l_refs[gi], scr[si + 1] if dil > 1 else None, dil, tm))
            si += 2 if dil > 1 else 0
        mx = jnp.maximum(jnp.maximum(ls_[0], ls_[1]), ls_[2])
        es = [jnp.exp(z - mx) for z in ls_]
        o_a = (es[0] * os_[0] + es[1] * os_[1] + es[2] * os_[2]) / (es[0] + es[1] + es[2])
    else:
        o_a = o_refs[0][0]
    gt = gt_ref[0].astype(F32)
    za = _dot(o_a.astype(BF16), wa_ref[...])
    zb = _dot(ob_ref[0].astype(BF16), wb_ref[...])
    merged = gt[:, :D_MODEL] * za + gt[:, D_MODEL:] * zb
    z = _dot(merged.astype(BF16), wo_ref[...])
    x1 = x_ref[0] + g1_ref[0] * _rms(z, npost_ref[...])
    x1_ref[0] = x1
    h2 = _rms(x1, npre_ref[...]) * (1.0 + sc2_ref[0]) + sh2_ref[0]
    packed = _pack_pairs(h2)
    for s in range(ROW_TILE_SUBLANES):
        hp_ref[0, pl.ds(s, tm, stride=ROW_TILE_SUBLANES), :] = packed[:, s * LANES:(s + 1) * LANES]
    tp =-(-tm // LANES) * LANES
    if tp != tm:
        h2 = jnp.concatenate([h2, jnp.zeros((tp - tm, D_MODEL), F32)], axis=0)
    logits_t = _dot_nt_split(wrt_ref[...], h2)
    w = _route_t(_sigmoid(logits_t[:N_EXPERTS, :]), rb_ref[...])
    wt_ref[...] = w[:, :tm]


def _post_call(o_parts, lse_parts, ob, gates, x, gate1, scale2, shift2, p, wa, wb, wo, wrt, rb, tm, mod_per_row):
    nb, t, _ = x.shape
    nt = t // tm
    combine = lse_parts is not None
    rowblk = lambda width: pl.BlockSpec((1, tm, width), lambda b, i: (b, i, 0))
    if mod_per_row:
        mod_spec = rowblk(D_MODEL)
    else:
        mod_spec = pl.BlockSpec((1, 1, D_MODEL), lambda b, i: (b, 0, 0))
    const = lambda shp: pl.BlockSpec(shp, lambda b, i: (0, 0))
    scratch = []
    if combine:
        dils = tuple(o.shape[1] for o in o_parts)
        o_args = list(o_parts) + list(lse_parts)
        o_specs = [pl.BlockSpec((1, d, tm // d, D_GROUP_A), lambda b, i: (b, 0, i, 0)) for d in dils] * 2
        for d in dils:
            if d > 1:
                scratch += [pltpu.VMEM((D_GROUP_A // LANES, tm, LANES), F32)] * 2
    else:
        dils = ()
        o_args = [o_parts[0]]
        o_specs = [rowblk(D_GROUP_A)]
    return pl.pallas_call(
        functools.partial(_post_body, combine=combine, dils=dils),
        out_shape=(jax.ShapeDtypeStruct((nb, t, D_MODEL), F32),
                   jax.ShapeDtypeStruct((nb, t * ROW_TILE_SUBLANES, LANES), I32),
                   jax.ShapeDtypeStruct((N_EXPERTS, nb * t), F32)),
        grid=(nb, nt),
        in_specs=o_specs + [rowblk(D_B), rowblk(2 * D_MODEL), rowblk(D_MODEL),
                            mod_spec, mod_spec, mod_spec, const((1, D_MODEL)), const((1, D_MODEL)),
                            const((D_GROUP_A, D_MODEL)), const((D_B, D_MODEL)), const((D_MODEL, D_MODEL)),
                            const((LANES, D_MODEL)), const((N_EXPERTS, 1))],
        out_specs=(rowblk(D_MODEL),
                   pl.BlockSpec((1, tm * ROW_TILE_SUBLANES, LANES), lambda b, i: (b, i, 0)),
                   pl.BlockSpec((N_EXPERTS, tm), lambda b, i: (0, b * nt + i))),
        scratch_shapes=scratch,
        compiler_params=_cparams(("arbitrary", "arbitrary")),
        name="post",
    )(*o_args, ob, gates, x, gate1, scale2, shift2, p['norm_post_mix'].reshape(1, -1),
      p['norm_pre_ffn'].reshape(1, -1), wa, wb, wo, wrt, rb)


def _rank_body(w_ref, dest_ref, w8_ref, tab_ref, etab_ref, cnt_ref, pst_ref, run_ref, *, n_real, n_slots):
    ph = pl.program_id(0)
    i = pl.program_id(1)
    T = MOE_TILE
    w = w_ref[...]
    sel = (w >= 0.0).astype(F32)
    cnt_tile = jnp.broadcast_to(jnp.sum(sel, axis=1, keepdims=True), (N_EXPERTS, LANES))
    ei = lax.broadcasted_iota(I32, (N_EXPERTS, N_EXPERTS), 0)
    ej = lax.broadcasted_iota(I32, (N_EXPERTS, N_EXPERTS), 1)

    @pl.when((ph == 0) & (i == 0))
    def _():
        cnt_ref[...] = jnp.zeros_like(cnt_ref)

    @pl.when(ph == 0)
    def _():
        cnt_ref[...] += cnt_tile

    @pl.when((ph == 1) & (i == 0))
    def _():
        cnt = cnt_ref[...]
        padded = jnp.floor((cnt + (EXPERT_BLOCK - 1)) / EXPERT_BLOCK) * EXPERT_BLOCK
        pstart = _dot_exact((ej < ei).astype(F32), padded)
        pst_ref[...] = pstart
        run_ref[...] = jnp.zeros_like(run_ref)
        pend = pstart + padded
        vend = pstart + cnt
        esub = lax.broadcasted_iota(I32, (N_EXPERTS, LANES), 0)
        lane = lax.broadcasted_iota(I32, (1, LANES), 1)
        tab_ref[...] = jnp.zeros_like(tab_ref)
        for c in range(tab_ref.shape[1] // LANES):
            bs = ((c * LANES + lane) * EXPERT_BLOCK).astype(F32)
            be = jnp.minimum(jnp.sum((pend <= bs).astype(F32), axis=0, keepdims=True), N_EXPERTS - 1.0)
            tab_ref[0:1, c * LANES:(c + 1) * LANES] = be.astype(I32)
            tab_ref[1:2, c * LANES:(c + 1) * LANES] = (pend[N_EXPERTS - 1:, :] / EXPERT_BLOCK).astype(I32)
        on_diag = esub == lax.broadcasted_iota(I32, (N_EXPERTS, LANES), 1)
        etab_ref[...] = jnp.zeros_like(etab_ref)
        lo = jnp.sum(jnp.where(on_diag, vend, 0.0), axis=0, keepdims=True)
        hi = jnp.sum(jnp.where(on_diag, pend, 0.0), axis=0, keepdims=True)
        etab_ref[0:1, :] = jnp.where(lane == N_EXPERTS, pend[N_EXPERTS - 1:, :], lo).astype(I32)
        etab_ref[1:2, :] = jnp.where(lane == N_EXPERTS, float(n_slots), hi).astype(I32)

    @pl.when(ph == 1)
    def _():
        ti = lax.broadcasted_iota(I32, (T, T), 0)
        tj = lax.broadcasted_iota(I32, (T, T), 1)
        selb = sel.astype(BF16)
        rank = _dot(selb, (ti < tj).astype(BF16))
        ordn = _dot((ej < ei).astype(BF16), selb)
        dest_e = pst_ref[:, :1] + run_ref[:, :1] + rank
        run_ref[...] += cnt_tile
        tok = i * T + lax.broadcasted_iota(I32, (1, T), 1)
        dks, wks = [], []
        for k in range(TOP_K):
            m = (sel > 0.5) & (ordn == float(k))
            dk = jnp.sum(jnp.where(m, dest_e, 0.0), axis=0, keepdims=True)
            wk = jnp.sum(jnp.where(m, w, 0.0), axis=0, keepdims=True)
            dks.append(jnp.where(tok < n_real, dk, 0.0))
            wks.append(jnp.where(tok < n_real, wk, 0.0))
        dest_ref[...] = jnp.concatenate(dks, axis=0).astype(I32)
        w8_ref[...] = jnp.concatenate(wks, axis=0)


def _rank_call(w_t, n_real, n_blocks, n_blocks_pad):
    n = w_t.shape[1]
    nt = n // MOE_TILE
    return pl.pallas_call(
        functools.partial(_rank_body, n_real=n_real, n_slots=n_blocks * EXPERT_BLOCK),
        out_shape=(jax.ShapeDtypeStruct((TOP_K, n), I32),
                   jax.ShapeDtypeStruct((TOP_K, n), F32),
                   jax.ShapeDtypeStruct((8, n_blocks_pad), I32),
                   jax.ShapeDtypeStruct((8, LANES), I32)),
        grid=(2, nt),
        in_specs=[pl.BlockSpec((N_EXPERTS, MOE_TILE), lambda ph, i: (0, i))],
        out_specs=(pl.BlockSpec((TOP_K, MOE_TILE), lambda ph, i: (0, i * ph)),
                   pl.BlockSpec((TOP_K, MOE_TILE), lambda ph, i: (0, i * ph)),
                   pl.BlockSpec((8, n_blocks_pad), lambda ph, i: (0, 0)),
                   pl.BlockSpec((8, LANES), lambda ph, i: (0, 0))),
        scratch_shapes=[pltpu.VMEM((N_EXPERTS, LANES), F32)] * 3,
        compiler_params=_cparams(("arbitrary", "arbitrary")),
        name="rank",
    )(w_t)


def _tile_rows(ref, row, n):
    return ref.at[pl.ds(pl.multiple_of(row * ROW_TILE_SUBLANES, ROW_TILE_SUBLANES), n * ROW_TILE_SUBLANES)]


def _zero_fill(etab_ref, zbuf, xs_hbm, zsem, wait):
    def go(src, dst):
        cp = pltpu.make_async_copy(src, dst, zsem)
        if wait:
            cp.wait()
        else:
            cp.start()

    def per_range(e, carry):
        lo = etab_ref[0, e]
        n = etab_ref[1, e] - lo
        n_full = n // ZERO_ROWS

        def full(j, c):
            go(zbuf, _tile_rows(xs_hbm, lo + j * ZERO_ROWS, ZERO_ROWS))
            return c

        lax.fori_loop(0, n_full, full, 0)
        pos = lo + n_full * ZERO_ROWS
        rem = n - n_full * ZERO_ROWS
        size = ZERO_ROWS // 2
        while size >= 1:
            bit = rem & size

            @pl.when(bit != 0)
            def _(size=size, pos=pos):
                go(_tile_rows(zbuf, 0, size), _tile_rows(xs_hbm, pos, size))

            pos = pos + bit
            size //= 2
        return carry

    lax.fori_loop(0, N_EXPERTS + 1, per_range, 0)


def _dispatch_body(dest_ref, etab_ref, xa_ref, xb_ref, xs_hbm, zbuf, sem, zsem, *, n_real, n_full):
    i = pl.program_id(0)
    T = MOE_TILE
    n_tok = jnp.clip(n_real - i * T, 0, T)

    def issue_from(x_ref):
        def issue(t, carry):
            for k in range(TOP_K):
                pltpu.make_async_copy(_tile_rows(x_ref, t, 1), _tile_rows(xs_hbm, dest_ref[k * T + t], 1),
                                      sem).start(priority=k % 2)
            return carry

        lax.fori_loop(0, n_tok, issue, 0)

    @pl.when(i < n_full)
    def _():
        issue_from(xa_ref)

    @pl.when(i >= n_full)
    def _():
        issue_from(xb_ref)

    @pl.when(i == 0)
    def _():
        zbuf[...] = jnp.zeros_like(zbuf)
        _zero_fill(etab_ref, zbuf, xs_hbm, zsem, wait=False)
        _zero_fill(etab_ref, zbuf, xs_hbm, zsem, wait=True)

    @pl.when(n_tok == T)
    def _():
        pltpu.make_async_copy(_tile_rows(xs_hbm, 0, T * TOP_K), _tile_rows(xs_hbm, 0, T * TOP_K), sem).wait()

    @pl.when(n_tok < T)
    def _():
        def drain(j, carry):
            pltpu.make_async_copy(_tile_rows(xs_hbm, 0, 1), _tile_rows(xs_hbm, 0, 1), sem).wait()
            return carry

        lax.fori_loop(0, n_tok * TOP_K, drain, 0)


def _dispatch_call(dest, etab, hp_a, hp_b, n_real, n_slots):
    tile_rows = MOE_TILE * ROW_TILE_SUBLANES
    n_full = hp_a.shape[0] // tile_rows
    return pl.pallas_call(
        functools.partial(_dispatch_body, n_real=n_real, n_full=n_full),
        out_shape=jax.ShapeDtypeStruct((n_slots * ROW_TILE_SUBLANES, LANES), I32),
        grid=(n_full + 1,),
        in_specs=[pl.BlockSpec((TOP_K * MOE_TILE,), lambda i: (i,), memory_space=pltpu.SMEM),
                  pl.BlockSpec((8, LANES), lambda i: (0, 0), memory_space=pltpu.SMEM),
                  pl.BlockSpec((tile_rows, LANES), lambda i: (jnp.minimum(i, n_full - 1), 0)),
                  pl.BlockSpec((tile_rows, LANES), lambda i: (0, 0))],
        out_specs=pl.BlockSpec(memory_space=pl.ANY),
        scratch_shapes=[pltpu.VMEM((ZERO_ROWS * ROW_TILE_SUBLANES, LANES), I32),
                        pltpu.SemaphoreType.DMA, pltpu.SemaphoreType.DMA],
        compiler_params=_cparams(("arbitrary",)),
        name="dispatch",
    )(dest, etab, hp_a, hp_b)


def _rows_from_tiles(ref, lo, n):
    return jnp.concatenate([ref[pl.ds(lo * ROW_TILE_SUBLANES + s, n, stride=ROW_TILE_SUBLANES), :]
                            for s in range(ROW_TILE_SUBLANES)], axis=1)


def _ffn_body(be_ref, nu_ref, xs_ref, wg_ref, wu_ref, wd_ref, ys_ref, wgb, wub, wdb):
    j = pl.program_id(0)

    @pl.when(j < nu_ref[0])
    def _():
        @pl.when((j == 0) | (be_ref[j] != be_ref[jnp.maximum(j - 1, 0)]))
        def _():
            wgb[...] = wg_ref[0].astype(BF16)
            wub[...] = wu_ref[0].astype(BF16)
            wdb[...] = wd_ref[0].astype(BF16)

        x = _unpack_pairs(_rows_from_tiles(xs_ref, 0, EXPERT_BLOCK)).astype(BF16)
        act = _silu(_dot(x, wgb[...])) * _dot(x, wub[...])
        y = _dot(act.astype(BF16), wdb[...])
        packed = _pack_pairs(y)
        for s in range(ROW_TILE_SUBLANES):
            ys_ref[pl.ds(s, EXPERT_BLOCK, stride=ROW_TILE_SUBLANES), :] = packed[:, s * LANES:(s + 1) * LANES]

    @pl.when(j >= nu_ref[0])
    def _():
        ys_ref[...] = jnp.zeros_like(ys_ref)


def _ffn_call(blk_e, n_used, xs, w_gate, w_up, w_down, n_blocks):
    tile_blk = pl.BlockSpec((EXPERT_BLOCK * ROW_TILE_SUBLANES, LANES), lambda j, be, nu: (j, 0))
    last = lambda j, nu: jnp.minimum(j, nu[0] - 1)
    grid_spec = pltpu.PrefetchScalarGridSpec(
        num_scalar_prefetch=2,
        grid=(n_blocks,),
        in_specs=[pl.BlockSpec((EXPERT_BLOCK * ROW_TILE_SUBLANES, LANES), lambda j, be, nu: (last(j, nu), 0)),
                  pl.BlockSpec((1, D_MODEL, D_EXPERT), lambda j, be, nu: (be[last(j, nu)], 0, 0)),
                  pl.BlockSpec((1, D_MODEL, D_EXPERT), lambda j, be, nu: (be[last(j, nu)], 0, 0)),
                  pl.BlockSpec((1, D_EXPERT, D_MODEL), lambda j, be, nu: (be[last(j, nu)], 0, 0))],
        out_specs=tile_blk,
        scratch_shapes=[pltpu.VMEM((D_MODEL, D_EXPERT), BF16), pltpu.VMEM((D_MODEL, D_EXPERT), BF16),
                        pltpu.VMEM((D_EXPERT, D_MODEL), BF16)])
    return pl.pallas_call(
        _ffn_body,
        out_shape=jax.ShapeDtypeStruct((n_blocks * EXPERT_BLOCK * ROW_TILE_SUBLANES, LANES), I32),
        grid_spec=grid_spec,
        compiler_params=_cparams(("arbitrary",)),
        name="ffn",
    )(blk_e, n_used, xs, w_gate, w_up, w_down)


def _combine_body(dest_ref, dnext_ref, w8_ref, xa_ref, xb_ref, x1a_ref, x1b_ref, g2a_ref, g2b_ref, gain_ref,
                  sg_ref, su_ref, sd_ref, ys_hbm, oa_ref, ob_ref, buf, sem):
    j = pl.program_id(0)
    T = COMBINE_TILE
    RC = COMBINE_ROWS

    def gather(d_ref, slot):
        def issue(t, carry):
            for k in range(TOP_K):
                pltpu.make_async_copy(_tile_rows(ys_hbm, d_ref[k * T + t], 1), _tile_rows(buf.at[slot], k * T + t, 1),
                                      sem.at[slot]).start(priority=k % 2)
            return carry

        lax.fori_loop(0, T, issue, 0, unroll=2)

    def step(slot):
        @pl.when(j + 1 < pl.num_programs(0))
        def _():
            gather(dnext_ref, 1 - slot)

        is_tail = j == 0
        x = _unpack_pairs(jnp.where(is_tail, _rows_from_tiles(xb_ref, 0, T),
                                    _rows_from_tiles(xa_ref, 0, T))).astype(BF16)
        shared = _dot((_silu(_dot(x, sg_ref[...])) * _dot(x, su_ref[...])).astype(BF16), sd_ref[...])
        w_t = jnp.concatenate([w8_ref[...], jnp.zeros((LANES - TOP_K, T), F32)], axis=0).T
        oa_ref[...] = shared
        pltpu.make_async_copy(_tile_rows(ys_hbm, 0, T * TOP_K), buf.at[slot], sem.at[slot]).wait()
        for r0 in range(0, T, RC):
            acc = oa_ref[r0:r0 + RC, :]
            for k in range(TOP_K):
                acc = acc + w_t[r0:r0 + RC, k:k + 1] * _unpack_pairs(_rows_from_tiles(buf.at[slot], k * T + r0, RC))
            x1 = jnp.where(is_tail, x1b_ref[r0:r0 + RC, :], x1a_ref[r0:r0 + RC, :])
            g2 = jnp.where(is_tail, g2b_ref[r0:r0 + RC, :], g2a_ref[0])
            oa_ref[r0:r0 + RC, :] = x1 + g2 * _rms(acc, gain_ref[...])

        @pl.when(is_tail)
        def _():
            ob_ref[...] = oa_ref[...]

    @pl.when(j == 0)
    def _():
        gather(dest_ref, 0)

    @pl.when(j % 2 == 0)
    def _():
        step(0)

    @pl.when(j % 2 == 1)
    def _():
        step(1)


def _combine_call(dest, w8, hp_a, hp_b, x1_a, x1_b, gate2_a, gate2_b, gain, wsg, wsu, wsd, ys):
    T = COMBINE_TILE
    tile_rows = T * ROW_TILE_SUBLANES
    n_full = hp_a.shape[0] // tile_rows
    n_tiles = n_full + 1
    seq = x1_a.shape[0] // gate2_a.shape[0]
    tile_of = lambda j: jnp.where(j == 0, n_full, j - 1)
    full_of = lambda j: jnp.maximum(j - 1, 0)
    const = lambda shp: pl.BlockSpec(shp, lambda j: (0, 0))
    return pl.pallas_call(
        _combine_body,
        out_shape=(jax.ShapeDtypeStruct((n_full * T, D_MODEL), F32), jax.ShapeDtypeStruct((T, D_MODEL), F32)),
        grid=(n_tiles,),
        in_specs=[pl.BlockSpec((TOP_K * T,), lambda j: (tile_of(j),), memory_space=pltpu.SMEM),
                  pl.BlockSpec((TOP_K * T,), lambda j: (tile_of(jnp.minimum(j + 1, n_tiles - 1)),),
                               memory_space=pltpu.SMEM),
                  pl.BlockSpec((TOP_K, T), lambda j: (0, tile_of(j))),
                  pl.BlockSpec((tile_rows, LANES), lambda j: (full_of(j), 0)),
                  pl.BlockSpec((tile_rows, LANES), lambda j: (0, 0)),
                  pl.BlockSpec((T, D_MODEL), lambda j: (full_of(j), 0)),
                  const((T, D_MODEL)),
                  pl.BlockSpec((1, 1, D_MODEL), lambda j: (full_of(j) * T // seq, 0, 0)),
                  const((T, D_MODEL)), const((1, D_MODEL)),
                  const((D_MODEL, D_EXPERT)), const((D_MODEL, D_EXPERT)), const((D_EXPERT, D_MODEL)),
                  pl.BlockSpec(memory_space=pl.ANY)],
        out_specs=(pl.BlockSpec((T, D_MODEL), lambda j: (full_of(j), 0)), const((T, D_MODEL))),
        scratch_shapes=[pltpu.VMEM((2, TOP_K * tile_rows, LANES), I32), pltpu.SemaphoreType.DMA((2,))],
        compiler_params=_cparams(("arbitrary",)),
        name="combine",
    )(dest, dest, w8, hp_a, hp_b, x1_a, x1_b, gate2_a, gate2_b, gain.reshape(1, -1), wsg, wsu, wsd, ys)


def _rope_tables(pos):
    half = HEAD_DIM_A // 2
    inv_freq = ROPE_THETA ** (-jnp.arange(half, dtype=F32) / half)
    ang = pos.astype(F32)[:, None] * inv_freq[None, :]
    cos = jnp.cos(ang)
    sin = jnp.sin(ang)
    reps = LANES // HEAD_DIM_A
    cos_t = jnp.tile(jnp.concatenate([cos, cos], axis=1), (1, reps))
    sin_t = jnp.tile(jnp.concatenate([-sin, sin], axis=1), (1, reps))
    return cos_t, sin_t


def _cache_from_tail(tail, keep):
    outs = []
    n, rows, _ = tail.shape
    for gi, kp in enumerate(keep):
        k = tail[:, rows - kp:, gi * D_GROUP_A:(gi + 1) * D_GROUP_A]
        v = tail[:, rows - kp:, D_A + gi * D_GROUP_A:D_A + (gi + 1) * D_GROUP_A]
        outs.append(jnp.stack([k, v], axis=2).reshape(n, kp, 2, N_HEADS_A, HEAD_DIM_A))
    return outs


def kernel(x_prompt, x_sample, c_prompt, c_sample, cache_a1_kv, cache_a2_kv, cache_a3_kv, state_b_wkv, state_b_shift, w_ada, b_ada, norm_pre_mix, norm_post_mix, norm_pre_ffn, norm_post_ffn, w_in, w_a_out, mu_b, w0_b, w_w2_b, a0_b, w_a2_b, w_g2_b, k_k_b, k_a_b, r_k_b, ln_x_w_b, ln_x_b_b, w_b_out, w_out, w_router, router_bias, w_e_gate, w_e_up, w_e_down, w_s_gate, w_s_up, w_s_down):
    assert DEPTH == 1
    l = 0
    nd = DEC_BATCH
    row = lambda a: a.reshape(1, -1)
    p = {'mu_b': row(mu_b[l]), 'w0_b': row(w0_b[l]), 'w_w2_b': w_w2_b[l], 'a0_b': row(a0_b[l]),
         'w_a2_b': w_a2_b[l], 'w_g2_b': w_g2_b[l], 'k_k_b': row(k_k_b[l]), 'k_a_b': row(k_a_b[l]),
         'r_k_b': row(r_k_b[l]), 'ln_x_w_b': row(ln_x_w_b[l]), 'ln_x_b_b': row(ln_x_b_b[l]),
         'norm_post_mix': norm_post_mix[l], 'norm_pre_ffn': norm_pre_ffn[l]}

    wq = w_in[l][:, :D_QKV].astype(BF16)
    wf = w_in[l][:, D_QKV:D_QKV + D_SHIFT_B].astype(BF16)
    wg = w_in[l][:, D_QKV + D_SHIFT_B:].astype(BF16)
    wa = w_a_out[l].astype(BF16)
    wb = w_b_out[l].astype(BF16)
    wo = w_out[l].astype(BF16)
    wrt = jnp.concatenate([w_router[l].T, jnp.zeros((LANES - N_EXPERTS, D_MODEL), F32)], axis=0)
    rb = router_bias[l].reshape(N_EXPERTS, 1)
    wsg, wsu, wsd = w_s_gate[l].astype(BF16), w_s_up[l].astype(BF16), w_s_down[l].astype(BF16)

    n_c = BATCH + nd
    c_all = jnp.concatenate([c_prompt, c_sample, jnp.zeros((-n_c % 8, D_MODEL), F32)], axis=0)
    mod = _mod_call(c_all, w_ada[l], b_ada[l])
    mod_p = [m.reshape(BATCH, 1, D_MODEL) for m in jnp.split(mod[:BATCH], 6, axis=-1)]
    mod_s = [m.reshape(1, nd, D_MODEL) for m in jnp.split(mod[BATCH:n_c], 6, axis=-1)]

    cos_p, sin_p = _rope_tables(jnp.arange(SEQ, dtype=I32))
    cos_s, sin_s = _rope_tables(jnp.full((nd,), PAST_LEN, I32))

    keep_p = [min(w, SEQ) for w, _ in DILATED_GROUPS]
    tail_rows = max(keep_p)
    dils = tuple(d for _, d in DILATED_GROUPS)

    q0, q1, q2, feat_p, gates_p, tail_p = _inproj_call(
        x_prompt, norm_pre_mix[l], mod_p[1], mod_p[0], cos_p, sin_p, wq, wf, wg,
        tm=256, tail_rows=tail_rows, mod_per_row=False, dils=dils)
    o_parts, lse_parts = [], []
    for gi, qg in enumerate((q0, q1, q2)):
        o, lse = _attn_call(qg, gi)
        o_parts.append(o)
        lse_parts.append(lse)
    ob_p, wkv_p = _wkv_call(feat_p, p)
    x1_p, hp_p, wt_p = _post_call(o_parts, lse_parts, ob_p, gates_p, x_prompt, mod_p[2], mod_p[4], mod_p[3],
                                  p, wa, wb, wo, wrt, rb, tm=512, mod_per_row=False)

    xs3 = x_sample.reshape(1, nd, D_MODEL)
    s0, s1, s2, feat_s, gates_s, tail_s = _inproj_call(
        xs3, norm_pre_mix[l], mod_s[1], mod_s[0], cos_s, sin_s, wq, wf, wg,
        tm=nd, tail_rows=nd, mod_per_row=True, dils=(1, 1, 1))
    qkv_s = jnp.stack([z.reshape(nd, 3, N_HEADS_A, HEAD_DIM_A) for z in (s0, s1, s2)], axis=2)
    qkv_s = qkv_s.reshape(nd, 3 * N_GROUPS_A * N_HEADS_A, HEAD_DIM_A).astype(F32)
    oa_s = _sattn_call(qkv_s, cache_a1_kv[l], cache_a2_kv[l], cache_a3_kv[l])
    r_s, w_s, k_s, v_s, aa_s, bb_s, g_s = _swkv_prep_call(feat_s[0], state_b_shift[l], p)
    nh = nd * N_HEADS_B
    as_row = lambda a: a.reshape(nh, 1, HEAD_DIM_B)
    s_new, y_col = _swkv_step_call(state_b_wkv[l].reshape(nh, HEAD_DIM_B, HEAD_DIM_B), as_row(aa_s), as_row(w_s),
                                   as_row(bb_s), as_row(k_s), as_row(r_s), v_s.reshape(nh, HEAD_DIM_B))
    ob_s = _swkv_fin_call(y_col.reshape(nd, D_B), r_s, k_s, v_s, g_s, p)
    x1_s, hp_s, wt_s = _post_call([oa_s.reshape(1, nd, D_GROUP_A)], None, ob_s.reshape(1, nd, D_B), gates_s, xs3,
                                  mod_s[2], mod_s[4], mod_s[3], p, wa, wb, wo, wrt, rb, tm=nd, mod_per_row=True)

    n_p = BATCH * SEQ
    n_real = n_p + nd
    n_all = -(-n_real // MOE_TILE) * MOE_TILE
    pad = n_all - n_real
    n_blocks = -(-(n_real * TOP_K) // EXPERT_BLOCK) + N_EXPERTS
    n_blocks_pad = -(-n_blocks // LANES) * LANES
    assert n_p % MOE_TILE == 0 and nd <= MOE_TILE
    hp_a = hp_p.reshape(n_p * ROW_TILE_SUBLANES, LANES)
    hp_b = jnp.concatenate([hp_s[0], jnp.zeros((pad * ROW_TILE_SUBLANES, LANES), I32)], axis=0)
    wt_all = jnp.concatenate([wt_p, wt_s, jnp.full((N_EXPERTS, pad), -1.0, F32)], axis=1)
    dest8, w8, tab, etab = _rank_call(wt_all, n_real, n_blocks, n_blocks_pad)
    dest = dest8.reshape(TOP_K, n_all // MOE_TILE, MOE_TILE).transpose(1, 0, 2).reshape(-1)
    xs = _dispatch_call(dest, etab, hp_a, hp_b, n_real, n_blocks * EXPERT_BLOCK)
    ys = _ffn_call(tab[0], tab[1, :1], xs, w_e_gate[l], w_e_up[l], w_e_down[l], n_blocks)
    n_ct = n_p // COMBINE_TILE + 1
    dest_c = dest8[:, :n_ct * COMBINE_TILE].reshape(TOP_K, n_ct, COMBINE_TILE).transpose(1, 0, 2).reshape(-1)
    pad_rows = lambda z: jnp.concatenate([z, jnp.zeros((COMBINE_TILE - nd, D_MODEL), F32)], axis=0)
    out_p, out_s = _combine_call(dest_c, w8, hp_a, hp_b, x1_p.reshape(n_p, D_MODEL), pad_rows(x1_s[0]), mod_p[5],
                                 pad_rows(mod_s[5][0]), norm_post_ffn[l], wsg, wsu, wsd, ys)
    y_prompt = out_p.reshape(BATCH, SEQ, D_MODEL)
    y_sample = out_s[:nd]

    a_p = [z[None] for z in _cache_from_tail(tail_p, keep_p)]
    a_s = [z.reshape(1, nd, DEC_SEQ, 2, N_HEADS_A, HEAD_DIM_A)
           for z in _cache_from_tail(tail_s.reshape(nd, 1, 2 * D_A), [DEC_SEQ] * N_GROUPS_A)]
    shift_p = feat_p[:, -1][None]
    shift_s = feat_s[0][None]
    return (y_prompt, y_sample.reshape(nd, DEC_SEQ, D_MODEL), a_p[0], a_p[1], a_p[2], wkv_p[None], shift_p,
            a_s[0], a_s[1], a_s[2], s_new.reshape(1, nd, N_HEADS_B, HEAD_DIM_B, HEAD_DIM_B), shift_s)
```

```python
import functools
import math

import jax
import jax.numpy as jnp
from jax import lax
from jax.experimental import pallas as pl
from jax.experimental.pallas import tpu as pltpu

F32 = jnp.float32
BF16 = jnp.bfloat16
I32 = jnp.int32

D_MODEL = 1024
BATCH = 2
SEQ = 8192
DEPTH = 1
DEC_BATCH = 32
DEC_SEQ = 1
PAST_LEN = 16384

HEAD_DIM_A = 64
N_HEADS_A = 8
DILATED_GROUPS = ((128, 1), (512, 4), (2048, 16))
N_GROUPS_A = 3
D_GROUP_A = N_HEADS_A * HEAD_DIM_A
D_A = N_GROUPS_A * D_GROUP_A
D_QKV = 3 * D_A
BAND_BLOCK = 128
ROPE_THETA = 10000.0

HEAD_DIM_B = 64
N_HEADS_B = 16
D_B = 1024
DECAY_LORA = 64
AAA_LORA = 64
GATE_LORA = 160
D_SHIFT_B = 3 * D_B + DECAY_LORA + AAA_LORA + GATE_LORA
LN_X_EPS = 64e-5

N_EXPERTS = 64
TOP_K = 8
N_EXPERT_GROUPS = 8
TOPK_GROUPS = 4
D_EXPERT = 256
ROUTED_SCALE = 2.5
EXPERT_BLOCK = 512
NORM_EPS = 1e-6

LANES = 128
WKV_CHUNK = 64
MOE_TILE = 1024
COMBINE_TILE = 256
COMBINE_ROWS = 32
VMEM_LIMIT = 56 * 1024 * 1024
ROW_TILE_SUBLANES = D_MODEL // (2 * LANES)
ZERO_ROWS = 256


def _cparams(sem):
    return pltpu.CompilerParams(dimension_semantics=sem, vmem_limit_bytes=VMEM_LIMIT)


def _dot(a, b):
    return jnp.dot(a, b, preferred_element_type=F32)


def _dot_nt(a, b):
    return lax.dot_general(a, b, (((1,), (1,)), ((), ())), preferred_element_type=F32)


def _dot_tn(a, b):
    return lax.dot_general(a, b, (((0,), (0,)), ((), ())), preferred_element_type=F32)


def _dot_nt_split(a, b):
    ah = a.astype(BF16)
    al = (a - ah.astype(F32)).astype(BF16)
    bh = b.astype(BF16)
    bl = (b - bh.astype(F32)).astype(BF16)
    return _dot_nt(ah, bh) + _dot_nt(ah, bl) + _dot_nt(al, bh)


def _dot_exact(a, b):
    return lax.dot_general(a, b, (((1,), (0,)), ((), ())), precision=lax.Precision.HIGHEST,
                           preferred_element_type=F32)


def _rms(x, gain):
    return x * lax.rsqrt(jnp.mean(x * x, axis=-1, keepdims=True) + NORM_EPS) * gain


def _sigmoid(x):
    return 1.0 / (1.0 + jnp.exp(-x))


def _silu(x):
    return x * _sigmoid(x)


def _softplus(x):
    return jnp.maximum(x, 0.0) + jnp.log(1.0 + jnp.exp(-jnp.abs(x)))


def _pack_pairs(x):
    half = D_MODEL // 2
    lo = lax.bitcast_convert_type(x[:, :half].astype(BF16).astype(F32), I32)
    hi = lax.bitcast_convert_type(x[:, half:].astype(BF16).astype(F32), I32)
    return lax.shift_right_logical(lo, 16) | (hi & jnp.int32(-65536))


def _unpack_pairs(w):
    lo = lax.bitcast_convert_type(w << 16, F32)
    hi = lax.bitcast_convert_type(w & jnp.int32(-65536), F32)
    return jnp.concatenate([lo, hi], axis=1)


def _mod_body(c_ref, w_ref, b_ref, o_ref):
    s = _silu(c_ref[...]).astype(BF16)
    o_ref[...] = _dot(s, w_ref[...].astype(BF16)) + b_ref[...]


def _mod_call(c_all, w_ada, b_ada):
    rows = c_all.shape[0]
    tn = 1536
    return pl.pallas_call(
        _mod_body,
        out_shape=jax.ShapeDtypeStruct((rows, 6 * D_MODEL), F32),
        grid=(6 * D_MODEL // tn,),
        in_specs=[pl.BlockSpec((rows, D_MODEL), lambda j: (0, 0)),
                  pl.BlockSpec((D_MODEL, tn), lambda j: (0, j)),
                  pl.BlockSpec((1, tn), lambda j: (0, j))],
        out_specs=pl.BlockSpec((rows, tn), lambda j: (0, j)),
        compiler_params=_cparams(("arbitrary",)),
        name="mod",
    )(c_all, w_ada, b_ada.reshape(1, -1))


def _wsplit_body(w_ref, q_ref, f_ref, g_ref):
    w = w_ref[...]
    q_ref[...] = w[:, :D_QKV].astype(BF16)
    f_ref[...] = w[:, D_QKV:D_QKV + D_SHIFT_B].astype(BF16)
    g_ref[...] = w[:, D_QKV + D_SHIFT_B:].astype(BF16)


def _wsplit_call(w):
    rows, cols = w.shape
    tr = 128
    widths = (D_QKV, D_SHIFT_B, cols - D_QKV - D_SHIFT_B)
    return pl.pallas_call(
        _wsplit_body,
        out_shape=tuple(jax.ShapeDtypeStruct((rows, n), BF16) for n in widths),
        grid=(rows // tr,),
        in_specs=[pl.BlockSpec((tr, cols), lambda i: (i, 0))],
        out_specs=tuple(pl.BlockSpec((tr, n), lambda i: (i, 0)) for n in widths),
        compiler_params=_cparams(("arbitrary",)),
        name="wsplit",
    )(w)


def _inproj_body(x_ref, g_ref, sc_ref, sh_ref, cos_ref, sin_ref, wq_ref, wf_ref, wg_ref,
                 q0_ref, q1_ref, q2_ref, feat_ref, gate_ref, t0_ref, t1_ref, t2_ref, p_ref, *, dils):
    x = x_ref[0]
    tm = x.shape[0]
    h = _rms(x, g_ref[...]) * (1.0 + sc_ref[0]) + sh_ref[0]
    hb = h.astype(BF16)
    p = _dot(hb, wq_ref[...])
    cos = cos_ref[...]
    sin = sin_ref[...]
    lane = lax.broadcasted_iota(I32, cos.shape, 1)
    first_half = (lane % HEAD_DIM_A) < (HEAD_DIM_A // 2)
    for c in range(2 * D_A // LANES):
        xc = p[:, c * LANES:(c + 1) * LANES]
        partner = jnp.where(first_half, pltpu.roll(xc, LANES - HEAD_DIM_A // 2, 1),
                            pltpu.roll(xc, HEAD_DIM_A // 2, 1))
        rc = xc * cos + partner * sin
        if c < D_A // LANES:
            rc = rc * (HEAD_DIM_A ** -0.5)
        p_ref[c] = rc
    for c in range(2 * D_A // LANES, D_QKV // LANES):
        p_ref[c] = p[:, c * LANES:(c + 1) * LANES]
    per_group = D_GROUP_A // LANES
    for gi, t_ref in enumerate((t0_ref, t1_ref, t2_ref)):
        rows = t_ref.shape[1]
        for which in (1, 2):
            for j in range(per_group):
                c = (which * D_A + gi * D_GROUP_A) // LANES + j
                t_ref[0, :, (which - 1) * D_GROUP_A + j * LANES:(which - 1) * D_GROUP_A + (j + 1) * LANES] = \
                    p_ref[c, tm - rows:tm, :]
    for gi, (out_ref, dil) in enumerate(zip((q0_ref, q1_ref, q2_ref), dils)):
        for which in range(3):
            for j in range(per_group):
                c = (which * D_A + gi * D_GROUP_A) // LANES + j
                dst = slice(which * D_GROUP_A + j * LANES, which * D_GROUP_A + (j + 1) * LANES)
                if dil == 1:
                    out_ref[0, 0, :, dst] = p_ref[c].astype(BF16)
                else:
                    for r in range(dil):
                        out_ref[0, r, :, dst] = p_ref[c, pl.ds(r, tm // dil, stride=dil), :].astype(BF16)
    feat_ref[0] = _dot(hb, wf_ref[...])
    gate_ref[0] = _sigmoid(_dot(hb, wg_ref[...])).astype(BF16)


def _inproj_call(x, gain, scale, shift, cos_t, sin_t, wq, wf, wg, tm, keeps, mod_per_row, dils):
    nb, t, _ = x.shape
    nt = t // tm

    def tail_spec(keep):
        if keep <= tm:
            return pl.BlockSpec((1, keep, 2 * D_GROUP_A), lambda b, i: (b, 0, 0))
        first = (t - keep) // tm
        return pl.BlockSpec((1, tm, 2 * D_GROUP_A), lambda b, i: (b, jnp.maximum(i - first, 0), 0))

    if mod_per_row:
        mod_spec = pl.BlockSpec((1, tm, D_MODEL), lambda b, i: (b, i, 0))
    else:
        mod_spec = pl.BlockSpec((1, 1, D_MODEL), lambda b, i: (b, 0, 0))
    resident = lambda shp: pl.BlockSpec(shp, lambda b, i: (0, 0), pipeline_mode=pl.Buffered(1))
    q_shapes = tuple(jax.ShapeDtypeStruct((nb, d, t // d, 3 * D_GROUP_A), BF16) for d in dils)
    q_specs = tuple(pl.BlockSpec((1, d, tm // d, 3 * D_GROUP_A), lambda b, i: (b, 0, i, 0)) for d in dils)
    return pl.pallas_call(
        functools.partial(_inproj_body, dils=dils),
        out_shape=q_shapes + (jax.ShapeDtypeStruct((nb, t, D_SHIFT_B), F32),
                              jax.ShapeDtypeStruct((nb, t, 2 * D_MODEL), BF16),
                              ) + tuple(jax.ShapeDtypeStruct((nb, kp, 2 * D_GROUP_A), F32) for kp in keeps),
        grid=(nb, nt),
        in_specs=[pl.BlockSpec((1, tm, D_MODEL), lambda b, i: (b, i, 0)),
                  pl.BlockSpec((1, D_MODEL), lambda b, i: (0, 0)),
                  mod_spec, mod_spec,
                  pl.BlockSpec((tm, LANES), lambda b, i: (i, 0)),
                  pl.BlockSpec((tm, LANES), lambda b, i: (i, 0)),
                  resident((D_MODEL, D_QKV)), resident((D_MODEL, D_SHIFT_B)),
                  resident((D_MODEL, 2 * D_MODEL))],
        out_specs=q_specs + (pl.BlockSpec((1, tm, D_SHIFT_B), lambda b, i: (b, i, 0)),
                             pl.BlockSpec((1, tm, 2 * D_MODEL), lambda b, i: (b, i, 0)),
                             ) + tuple(tail_spec(kp) for kp in keeps),
        scratch_shapes=[pltpu.VMEM((D_QKV // LANES, tm, LANES), F32)],
        compiler_params=_cparams(("arbitrary", "arbitrary")),
        name="inproj",
    )(x, gain.reshape(1, -1), scale, shift, cos_t, sin_t, wq, wf, wg)


def _attn_body(q_ref, kc_ref, kp_ref, vc_ref, vp_ref, o_ref, lse_ref):
    mb = pl.program_id(2)
    nq = q_ref.shape[2] // BAND_BLOCK
    q = q_ref[0, 0]
    k = jnp.concatenate([kp_ref[0, 0], kc_ref[0, 0]], axis=0)
    v = jnp.concatenate([vp_ref[0, 0], vc_ref[0, 0]], axis=0)
    qi = lax.broadcasted_iota(I32, (BAND_BLOCK, 2 * BAND_BLOCK), 0)
    ki = lax.broadcasted_iota(I32, (BAND_BLOCK, 2 * BAND_BLOCK), 1)
    dist = qi + BAND_BLOCK - ki
    band = (dist >= 0) & (dist <= BAND_BLOCK)
    masks = [band & ((ki >= BAND_BLOCK) | (mb > 0))] + [band] * (nq - 1)
    lane_q = lax.broadcasted_iota(I32, (BAND_BLOCK, LANES), 1)
    lane_k = lax.broadcasted_iota(I32, (2 * BAND_BLOCK, LANES), 1)
    for hp in range(N_HEADS_A // 2):
        sl = slice(hp * LANES, (hp + 1) * LANES)
        chains = [(j, sub) for j in range(nq) for sub in range(2)]
        qs = [q[j * BAND_BLOCK:(j + 1) * BAND_BLOCK, sl] for j in range(nq)]
        ks = [k[j * BAND_BLOCK:(j + 2) * BAND_BLOCK, sl] for j in range(nq)]
        vs = [v[j * BAND_BLOCK:(j + 2) * BAND_BLOCK, sl] for j in range(nq)]
        mqs = [lane_q < HEAD_DIM_A, lane_q >= HEAD_DIM_A]
        mks = [lane_k < HEAD_DIM_A, lane_k >= HEAD_DIM_A]
        s = [jnp.where(masks[j], _dot_nt(jnp.where(mqs[sub], qs[j], jnp.zeros_like(qs[j])), ks[j]), -jnp.inf)
             for j, sub in chains]
        mx = [jnp.max(z, axis=1, keepdims=True) for z in s]
        p = [jnp.exp(z - m) for z, m in zip(s, mx)]
        l = [jnp.sum(z, axis=1, keepdims=True) for z in p]
        pv = [_dot(p[c].astype(BF16), jnp.where(mks[sub], vs[j], jnp.zeros_like(vs[j])))
              for c, (j, sub) in enumerate(chains)]
        for j in range(nq):
            c0, c1 = 2 * j, 2 * j + 1
            o_pair = pv[c0] / l[c0] + pv[c1] / l[c1]
            lse_pair = jnp.where(mqs[0], mx[c0] + jnp.log(l[c0]), mx[c1] + jnp.log(l[c1]))
            o_ref[0, 0, j * BAND_BLOCK:(j + 1) * BAND_BLOCK, sl] = o_pair.astype(BF16)
            lse_ref[0, 0, j * BAND_BLOCK:(j + 1) * BAND_BLOCK, sl] = lse_pair


def _attn_call(qkv_g, gi):
    b, dil, l, _ = qkv_g.shape
    nq = 4
    nb = l // (nq * BAND_BLOCK)
    blk = (1, 1, nq * BAND_BLOCK, D_GROUP_A)
    cur = lambda which: pl.BlockSpec(blk, lambda bb, r, m: (bb, r, m, which))
    prev = lambda which: pl.BlockSpec((1, 1, BAND_BLOCK, D_GROUP_A),
                                      lambda bb, r, m: (bb, r, jnp.maximum(nq * m - 1, 0), which))
    return pl.pallas_call(
        _attn_body,
        out_shape=(jax.ShapeDtypeStruct((b, dil, l, D_GROUP_A), BF16),
                   jax.ShapeDtypeStruct((b, dil, l, D_GROUP_A), F32)),
        grid=(b, dil, nb),
        in_specs=[cur(0), cur(1), prev(1), cur(2), prev(2)],
        out_specs=(pl.BlockSpec(blk, lambda bb, r, m: (bb, r, m, 0)),
                   pl.BlockSpec(blk, lambda bb, r, m: (bb, r, m, 0))),
        compiler_params=_cparams(("arbitrary", "arbitrary", "arbitrary")),
        name=f"attn{gi}",
    )(qkv_g, qkv_g, qkv_g, qkv_g, qkv_g)


def _sattn_body(qkv_ref, b1_ref, b2_ref, b3_ref, o_ref):
    n_rows = 3 * N_GROUPS_A * N_HEADS_A
    sq = jnp.concatenate([qkv_ref[0], jnp.zeros((LANES - n_rows, HEAD_DIM_A), F32)], axis=0)
    cols = jnp.concatenate([sq, jnp.zeros((LANES, LANES - HEAD_DIM_A), F32)], axis=1).T
    col3 = lambda first: jnp.stack([cols[:HEAD_DIM_A, first + h:first + h + 1] for h in range(N_HEADS_A)], axis=0)
    outs, lses = [], []
    for g, (buf_ref, (_, dil)) in enumerate(zip((b1_ref, b2_ref, b3_ref), DILATED_GROUPS)):
        q = col3(g * N_HEADS_A)
        kn = col3((N_GROUPS_A + g) * N_HEADS_A)
        vn = col3((2 * N_GROUPS_A + g) * N_HEADS_A)
        kb = buf_ref[0, 0]
        vb = buf_ref[0, 1]
        wb = kb.shape[-1]
        pos = lax.broadcasted_iota(I32, (1, 1, wb), 2)
        s = jnp.sum(kb * q, axis=1, keepdims=True)
        s = jnp.where(pos % dil == 0, s, -jnp.inf)
        sn = jnp.sum(kn * q, axis=1, keepdims=True)
        m = jnp.maximum(jnp.max(s, axis=2, keepdims=True), sn)
        p = jnp.exp(s - m)
        pn = jnp.exp(sn - m)
        l = jnp.sum(p, axis=2, keepdims=True) + pn
        outs.append((jnp.sum(p * vb, axis=2, keepdims=True) + pn * vn) / l)
        lses.append(m + jnp.log(l))
    mx = jnp.maximum(jnp.maximum(lses[0], lses[1]), lses[2])
    es = [jnp.exp(z - mx) for z in lses]
    o_a = (es[0] * outs[0] + es[1] * outs[1] + es[2] * outs[2]) / (es[0] + es[1] + es[2])
    o_cols = jnp.concatenate([o_a[h] for h in range(N_HEADS_A)] +
                             [jnp.zeros((HEAD_DIM_A, LANES - N_HEADS_A), F32)], axis=1)
    o_rows = jnp.concatenate([o_cols, jnp.zeros((LANES - HEAD_DIM_A, LANES), F32)], axis=0).T
    o_ref[0] = o_rows[:N_HEADS_A, :HEAD_DIM_A]


def _sattn_call(qkv_s, c1, c2, c3):
    n = qkv_s.shape[0]
    views, specs = [], []
    for c in (c1, c2, c3):
        wb = c.shape[1]
        views.append(jnp.transpose(c, (0, 2, 3, 4, 1)))
        specs.append(pl.BlockSpec((1, 2, N_HEADS_A, HEAD_DIM_A, wb), lambda b: (b, 0, 0, 0, 0)))
    return pl.pallas_call(
        _sattn_body,
        out_shape=jax.ShapeDtypeStruct((n, N_HEADS_A, HEAD_DIM_A), F32),
        grid=(n,),
        in_specs=[pl.BlockSpec((1, 3 * N_GROUPS_A * N_HEADS_A, HEAD_DIM_A), lambda b: (b, 0, 0))] + specs,
        out_specs=pl.BlockSpec((1, N_HEADS_A, HEAD_DIM_A), lambda b: (b, 0, 0)),
        compiler_params=_cparams(("arbitrary",)),
        name="sattn",
    )(qkv_s, *views)


def _rwkv_features(xs, w0, ww2, a0, wa2, wg2, k_a):
    r = xs[:, :D_B]
    k = xs[:, D_B:2 * D_B]
    v = xs[:, 2 * D_B:3 * D_B]
    xw = xs[:, 3 * D_B:3 * D_B + DECAY_LORA]
    xa = xs[:, 3 * D_B + DECAY_LORA:3 * D_B + DECAY_LORA + AAA_LORA]
    xg = xs[:, 3 * D_B + DECAY_LORA + AAA_LORA:]
    w_log = -_softplus(-(w0 + _dot(jnp.tanh(xw).astype(BF16), ww2.astype(BF16)))) - 0.5
    a = _sigmoid(a0 + _dot(xa.astype(BF16), wa2.astype(BF16)))
    g = _dot(_sigmoid(xg).astype(BF16), wg2.astype(BF16))
    k_h = k * (1.0 + (a - 1.0) * k_a)
    return r, k, v, w_log, a, g, k_h


def _head_norm(kk_h):
    nrm = jnp.sqrt(jnp.sum(kk_h * kk_h, axis=-1, keepdims=True))
    return kk_h / jnp.maximum(nrm, 1e-12)


def _wkv_finish_head(y, r_h, k_h, v_h, g_h, rk_h, lnw_h, lnb_h):
    mean = jnp.mean(y, axis=-1, keepdims=True)
    var = jnp.mean(jnp.square(y - mean), axis=-1, keepdims=True)
    yn = (y - mean) * lax.rsqrt(var + LN_X_EPS) * lnw_h + lnb_h
    bonus = jnp.sum(r_h * k_h * rk_h, axis=-1, keepdims=True) * v_h
    return (yn + bonus) * g_h


def _wkv_body(f_ref, fp_ref, mu_ref, w0_ref, ww2_ref, a0_ref, wa2_ref, wg2_ref, kk_ref, ka_ref,
              rk_ref, lnw_ref, lnb_ref, o_ref, st_ref, s_ref):
    c = pl.program_id(0)
    C = WKV_CHUNK
    nb = f_ref.shape[0]

    @pl.when(c == 0)
    def _():
        s_ref[...] = jnp.zeros_like(s_ref)

    f = jnp.concatenate([f_ref[b] for b in range(nb)], axis=0)
    row = lax.broadcasted_iota(I32, f.shape, 0)
    prev = pltpu.roll(f, 1, 0)
    for b in range(nb):
        prev = jnp.where(row == b * C, jnp.where(c == 0, 0.0, fp_ref[b][7:8, :]), prev)
    xs = f + mu_ref[...] * (prev - f)
    r, k, v, w_log, a, g, k_h = _rwkv_features(xs, w0_ref[...], ww2_ref[...], a0_ref[...],
                                               wa2_ref[...], wg2_ref[...], ka_ref[...])
    lw = -jnp.exp(w_log)
    kk = k * kk_ref[...]
    jh = lax.broadcasted_iota(I32, (D_B, LANES), 0) // HEAD_DIM_B
    ind = (jh == lax.broadcasted_iota(I32, (D_B, LANES), 1)).astype(BF16)
    ind_t = (lax.broadcasted_iota(I32, (LANES, D_B), 0)
             == lax.broadcasted_iota(I32, (LANES, D_B), 1) // HEAD_DIM_B).astype(BF16)

    def head_sum(z):
        hi = z.astype(BF16)
        lo = (z - hi.astype(F32)).astype(BF16)
        s = _dot(hi, ind) + _dot(lo, ind)
        shi = s.astype(BF16)
        slo = (s - shi.astype(F32)).astype(BF16)
        return _dot(shi, ind_t) + _dot(slo, ind_t)

    kkn = kk / jnp.maximum(jnp.sqrt(head_sum(kk * kk)), 1e-12)

    tr = lax.broadcasted_iota(I32, (nb * C, nb * C), 0)
    sr_ = lax.broadcasted_iota(I32, (nb * C, nb * C), 1)
    tri_incl = ((tr >= sr_) & (tr // C == sr_ // C)).astype(BF16)
    l1 = lw.astype(BF16)
    r1 = lw - l1.astype(F32)
    l2 = r1.astype(BF16)
    l3 = (r1 - l2.astype(F32)).astype(BF16)
    cum = _dot(tri_incl, l1) + _dot(tri_incl, l2) + _dot(tri_incl, l3)
    rhos = [cum[b * C + C // 2 - 1:b * C + C // 2, :] for b in range(nb)]
    rho = jnp.concatenate([jnp.broadcast_to(z, (C, D_B)) for z in rhos], axis=0)
    ep = jnp.exp(cum - rho)
    em = jnp.exp(rho - cum)
    e_a = ep * jnp.exp(-lw)
    r_hat = r * ep
    k_hat = k_h * em
    e_rs = [jnp.exp(z) for z in rhos]
    e_cs = [jnp.exp(cum[b * C + C - 1:b * C + C, :] - rhos[b]) for b in range(nb)]

    ti = lax.broadcasted_iota(I32, (C, C), 0)
    si = lax.broadcasted_iota(I32, (C, C), 1)
    strict = ti > si
    incl = ti >= si
    rk = rk_ref[...]
    lnw = lnw_ref[...]
    lnb = lnb_ref[...]
    items = [(b, h) for b in range(nb) for h in range(N_HEADS_B)]
    heads = range(len(items))
    lanes = [slice(h * HEAD_DIM_B, (h + 1) * HEAD_DIM_B) for _, h in items]
    cut = lambda z, i: z[items[i][0] * C:(items[i][0] + 1) * C, lanes[i]]
    e_r = [e_rs[b][:, lanes[i]] for i, (b, _) in enumerate(items)]
    e_c = [e_cs[b][:, lanes[i]] for i, (b, _) in enumerate(items)]
    a_hat_full = (-kkn * e_a).astype(BF16)
    b_hat_full = (kkn * a * em).astype(BF16)
    a_hat_b = [cut(a_hat_full, h) for h in heads]
    b_hat_b = [cut(b_hat_full, h) for h in heads]
    rh = [cut(r_hat, h) for h in heads]
    vb = [cut(v, h).astype(BF16) for h in heads]
    bk = [jnp.concatenate([b_hat_b[h], cut(k_hat, h).astype(BF16)], axis=0) for h in heads]
    p = [_dot_nt(jnp.concatenate([a_hat_b[h], rh[h].astype(BF16)], axis=0), bk[h]) for h in heads]
    l_ak = [jnp.where(strict, z[:C, C:], 0.0).astype(BF16) for z in p]
    p_rb = [jnp.where(incl, z[C:, :C], 0.0).astype(BF16) for z in p]
    p_rk = [jnp.where(incl, z[C:, C:], 0.0).astype(BF16) for z in p]
    col = lax.broadcasted_iota(I32, (C, 2 * C), 1)
    row2 = lax.broadcasted_iota(I32, (C, 2 * C), 0)
    left = col < C
    zt = [jnp.where(left, jnp.where(row2 > col, z[:C], 0.0), (col == row2 + C).astype(F32)) for z in p]
    for _ in range(int(math.log2(C))):
        zb = [z.astype(BF16) for z in zt]
        res = [_dot(z[:, :C], z) for z in zb]
        zt = [jnp.where(left, res[h], zt[h] + res[h]) for h in heads]
    tb = [z.astype(BF16) for z in zt]
    zeros_c = jnp.zeros((C, HEAD_DIM_B), BF16)
    lv = [_dot(l_ak[h], vb[h]).astype(BF16) for h in heads]
    a_bar = [_dot(tb[h], jnp.concatenate([zeros_c, a_hat_b[h]], axis=0)).astype(BF16) for h in heads]
    u_v = [_dot(tb[h], jnp.concatenate([zeros_c, lv[h]], axis=0)).astype(BF16) for h in heads]
    r_bar = [rh[h] + _dot(p_rb[h], a_bar[h]) for h in heads]
    y_v = [_dot(p_rb[h], u_v[h]) + _dot(p_rk[h], vb[h]) for h in heads]
    ab = [_dot_tn(a_bar[h], b_hat_b[h]).astype(BF16) for h in heads]
    n_t = [_dot_tn(jnp.concatenate([u_v[h], vb[h]], axis=0), bk[h]) for h in heads]
    s0 = [s_ref[b, h] for b, h in items]
    sr = [s0[h] * e_r[h] for h in heads]
    y = [_dot_nt((r_bar[h] * e_r[h]).astype(BF16), s0[h].astype(BF16)) + y_v[h] for h in heads]
    s_new = [(sr[h] + _dot(sr[h].astype(BF16), ab[h]) + n_t[h]) * e_c[h] for h in heads]
    for i, (b, h) in enumerate(items):
        s_ref[b, h] = s_new[i]
    y_full = jnp.concatenate([jnp.concatenate(y[b * N_HEADS_B:(b + 1) * N_HEADS_B], axis=1) for b in range(nb)],
                             axis=0)
    inv_hd = 1.0 / HEAD_DIM_B
    dev = y_full - head_sum(y_full) * inv_hd
    yn = dev * lax.rsqrt(head_sum(dev * dev) * inv_hd + LN_X_EPS) * lnw + lnb
    out = (yn + head_sum(r * k_h * rk) * v) * g
    for b in range(nb):
        o_ref[b] = out[b * C:(b + 1) * C, :]

    @pl.when(c == pl.num_programs(0) - 1)
    def _():
        st_ref[...] = s_ref[...]


def _wkv_call(feat, p):
    b, t, _ = feat.shape
    C = WKV_CHUNK
    nc = t // C
    row = lambda n: pl.BlockSpec((1, n), lambda c: (0, 0))
    mat = lambda m, n: pl.BlockSpec((m, n), lambda c: (0, 0))
    return pl.pallas_call(
        _wkv_body,
        out_shape=(jax.ShapeDtypeStruct((b, t, D_B), F32),
                   jax.ShapeDtypeStruct((b, N_HEADS_B, HEAD_DIM_B, HEAD_DIM_B), F32)),
        grid=(nc,),
        in_specs=[pl.BlockSpec((b, C, D_SHIFT_B), lambda c: (0, c, 0)),
                  pl.BlockSpec((b, 8, D_SHIFT_B), lambda c: (0, jnp.maximum(c * (C // 8) - 1, 0), 0)),
                  row(D_SHIFT_B), row(D_B), mat(DECAY_LORA, D_B), row(D_B), mat(AAA_LORA, D_B),
                  mat(GATE_LORA, D_B), row(D_B), row(D_B), row(D_B), row(D_B), row(D_B)],
        out_specs=(pl.BlockSpec((b, C, D_B), lambda c: (0, c, 0)),
                   pl.BlockSpec((b, N_HEADS_B, HEAD_DIM_B, HEAD_DIM_B), lambda c: (0, 0, 0, 0))),
        scratch_shapes=[pltpu.VMEM((b, N_HEADS_B, HEAD_DIM_B, HEAD_DIM_B), F32)],
        compiler_params=_cparams(("arbitrary",)),
        name="wkv",
    )(feat, feat, p['mu_b'], p['w0_b'], p['w_w2_b'], p['a0_b'], p['w_a2_b'], p['w_g2_b'],
      p['k_k_b'], p['k_a_b'], p['r_k_b'], p['ln_x_w_b'], p['ln_x_b_b'])


def _swkv_prep_body(f_ref, sh_ref, mu_ref, w0_ref, ww2_ref, a0_ref, wa2_ref, wg2_ref, kk_ref, ka_ref,
                    r_ref, w_ref, k_ref, v_ref, aa_ref, bb_ref, g_ref):
    f = f_ref[...]
    xs = f + mu_ref[...] * (sh_ref[...] - f)
    r, k, v, w_log, a, g, k_h = _rwkv_features(xs, w0_ref[...], ww2_ref[...], a0_ref[...],
                                               wa2_ref[...], wg2_ref[...], ka_ref[...])
    kk = k * kk_ref[...]
    kkn = jnp.concatenate([_head_norm(kk[:, h * HEAD_DIM_B:(h + 1) * HEAD_DIM_B]) for h in range(N_HEADS_B)],
                          axis=1)
    r_ref[...] = r
    w_ref[...] = jnp.exp(-jnp.exp(w_log))
    k_ref[...] = k_h
    v_ref[...] = v
    aa_ref[...] = -kkn
    bb_ref[...] = kkn * a
    g_ref[...] = g


def _swkv_prep_call(feat_s, shift0, p):
    n = feat_s.shape[0]
    full = lambda a: pl.BlockSpec(a.shape, lambda: tuple(0 for _ in a.shape))
    args = (feat_s, shift0, p['mu_b'], p['w0_b'], p['w_w2_b'], p['a0_b'], p['w_a2_b'], p['w_g2_b'],
            p['k_k_b'], p['k_a_b'])
    return pl.pallas_call(
        _swkv_prep_body,
        out_shape=tuple(jax.ShapeDtypeStruct((n, D_B), F32) for _ in range(7)),
        in_specs=[full(a) for a in args],
        out_specs=tuple(pl.BlockSpec((n, D_B), lambda: (0, 0)) for _ in range(7)),
        compiler_params=pltpu.CompilerParams(vmem_limit_bytes=VMEM_LIMIT),
        name="swkv_prep",
    )(*args)


def _swkv_step_body(s_ref, a_ref, w_ref, b_ref, k_ref, r_ref, v_ref, so_ref, y_ref):
    s = s_ref[...]
    th = s.shape[0]
    pad_sq = lambda z: jnp.concatenate(
        [jnp.concatenate([z, jnp.zeros((z.shape[0], LANES - z.shape[1]), F32)], axis=1),
         jnp.zeros((LANES - z.shape[0], LANES), F32)], axis=0)
    v_t = pad_sq(v_ref[...]).T
    v_col = jnp.stack([v_t[:HEAD_DIM_B, j:j + 1] for j in range(th)], axis=0)
    sa = jnp.sum(s * a_ref[...], axis=-1, keepdims=True)
    s2 = s * w_ref[...] + sa * b_ref[...] + v_col * k_ref[...]
    so_ref[...] = s2
    y = jnp.sum(s2 * r_ref[...], axis=-1, keepdims=True)
    y_t = jnp.concatenate([y[j] for j in range(th)], axis=1)
    y_ref[...] = pad_sq(y_t).T[:th, :HEAD_DIM_B]


def _swkv_step_call(s0, aa, w, bb, k, r, v):
    nh = s0.shape[0]
    th = 64
    rowspec = pl.BlockSpec((th, 1, HEAD_DIM_B), lambda i: (i, 0, 0))
    matspec = pl.BlockSpec((th, HEAD_DIM_B), lambda i: (i, 0))
    stspec = pl.BlockSpec((th, HEAD_DIM_B, HEAD_DIM_B), lambda i: (i, 0, 0))
    return pl.pallas_call(
        _swkv_step_body,
        out_shape=(jax.ShapeDtypeStruct((nh, HEAD_DIM_B, HEAD_DIM_B), F32),
                   jax.ShapeDtypeStruct((nh, HEAD_DIM_B), F32)),
        grid=(nh // th,),
        in_specs=[stspec, rowspec, rowspec, rowspec, rowspec, rowspec, matspec],
        out_specs=(stspec, matspec),
        compiler_params=_cparams(("arbitrary",)),
        name="swkv_step",
    )(s0, aa, w, bb, k, r, v)


def _swkv_fin_body(y_ref, r_ref, k_ref, v_ref, g_ref, rk_ref, lnw_ref, lnb_ref, o_ref):
    y, r, k, v, g = y_ref[...], r_ref[...], k_ref[...], v_ref[...], g_ref[...]
    rk, lnw, lnb = rk_ref[...], lnw_ref[...], lnb_ref[...]
    outs = []
    for h in range(N_HEADS_B):
        sl = slice(h * HEAD_DIM_B, (h + 1) * HEAD_DIM_B)
        outs.append(_wkv_finish_head(y[:, sl], r[:, sl], k[:, sl], v[:, sl], g[:, sl],
                                     rk[:, sl], lnw[:, sl], lnb[:, sl]))
    o_ref[...] = jnp.concatenate(outs, axis=1)


def _swkv_fin_call(y, r, k, v, g, p):
    n = y.shape[0]
    args = (y, r, k, v, g, p['r_k_b'], p['ln_x_w_b'], p['ln_x_b_b'])
    full = lambda a: pl.BlockSpec(a.shape, lambda: (0, 0))
    return pl.pallas_call(
        _swkv_fin_body,
        out_shape=jax.ShapeDtypeStruct((n, D_B), F32),
        in_specs=[full(a) for a in args],
        out_specs=pl.BlockSpec((n, D_B), lambda: (0, 0)),
        name="swkv_fin",
    )(*args)


def _route_t(scores, bias_col):
    n = scores.shape[1]
    gsz = N_EXPERTS // N_EXPERT_GROUPS
    choice = scores + bias_col
    ninf = -jnp.inf
    sid = lax.broadcasted_iota(I32, (gsz, n), 0)
    gs = []
    for gidx in range(N_EXPERT_GROUPS):
        blk = choice[gidx * gsz:(gidx + 1) * gsz, :]
        m1 = jnp.max(blk, axis=0, keepdims=True)
        first = jnp.min(jnp.where(blk == m1, sid, gsz), axis=0, keepdims=True)
        m2 = jnp.max(jnp.where(sid == first, ninf, blk), axis=0, keepdims=True)
        gs.append(m1 + m2)
    cur = jnp.concatenate(gs, axis=0)
    gid = lax.broadcasted_iota(I32, (N_EXPERT_GROUPS, n), 0)
    gmask = jnp.zeros((N_EXPERT_GROUPS, n), F32)
    for _ in range(TOPK_GROUPS):
        m = jnp.max(cur, axis=0, keepdims=True)
        first = jnp.min(jnp.where(cur == m, gid, N_EXPERT_GROUPS), axis=0, keepdims=True)
        sel = gid == first
        gmask = jnp.where(sel, 1.0, gmask)
        cur = jnp.where(sel, ninf, cur)
    emask = jnp.concatenate([jnp.broadcast_to(gmask[gidx:gidx + 1, :], (gsz, n))
                             for gidx in range(N_EXPERT_GROUPS)], axis=0)
    cur = jnp.where(emask > 0.5, choice, ninf)
    eid = lax.broadcasted_iota(I32, (N_EXPERTS, n), 0)
    selm = jnp.zeros((N_EXPERTS, n), F32)
    for _ in range(TOP_K):
        m = jnp.max(cur, axis=0, keepdims=True)
        first = jnp.min(jnp.where(cur == m, eid, N_EXPERTS), axis=0, keepdims=True)
        sel = eid == first
        selm = jnp.where(sel, 1.0, selm)
        cur = jnp.where(sel, ninf, cur)
    w = jnp.where(selm > 0.5, scores, 0.0)
    w = w / jnp.sum(w, axis=0, keepdims=True) * ROUTED_SCALE
    return jnp.where(selm > 0.5, w, -1.0)


def _unpermute(blk_ref, scr_ref, dil, tm):
    if dil == 1:
        return blk_ref[0, 0].astype(F32)
    n_chunks = scr_ref.shape[0]
    for r in range(dil):
        rows = blk_ref[0, r].astype(F32)
        for j in range(n_chunks):
            scr_ref[j, pl.ds(r, tm // dil, stride=dil), :] = rows[:, j * LANES:(j + 1) * LANES]
    return jnp.concatenate([scr_ref[j] for j in range(n_chunks)], axis=1)


def _post_body(*refs, combine, dils):
    if combine:
        o_refs, l_refs, rest = refs[:3], refs[3:6], refs[6:]
    else:
        o_refs, rest = refs[:1], refs[1:]
    (ob_ref, gt_ref, x_ref, g1_ref, sc2_ref, sh2_ref, npost_ref, npre_ref, wa_ref, wb_ref, wo_ref,
     wrt_ref, rb_ref, x1_ref, hp_ref, wt_ref) = rest[:16]
    scr = rest[16:]
    tm = x_ref.shape[1]
    if combine:
        os_, ls_ = [], []
        si = 0
        for gi, dil in enumerate(dils):
            os_.append(_unpermute(o_refs[gi], scr[si] if dil > 1 else None, dil, tm))
            ls_.append(_unpermute(l_refs[gi], scr[si + 1] if dil > 1 else None, dil, tm))
            si += 2 if dil > 1 else 0
        mx = jnp.maximum(jnp.maximum(ls_[0], ls_[1]), ls_[2])
        es = [jnp.exp(z - mx) for z in ls_]
        o_a = (es[0] * os_[0] + es[1] * os_[1] + es[2] * os_[2]) / (es[0] + es[1] + es[2])
    else:
        o_a = o_refs[0][0]
    gt = gt_ref[0].astype(F32)
    za = _dot(o_a.astype(BF16), wa_ref[...])
    zb = _dot(ob_ref[0].astype(BF16), wb_ref[...])
    merged = gt[:, :D_MODEL] * za + gt[:, D_MODEL:] * zb
    z = _dot(merged.astype(BF16), wo_ref[...])
    x1 = x_ref[0] + g1_ref[0] * _rms(z, npost_ref[...])
    x1_ref[0] = x1
    h2 = _rms(x1, npre_ref[...]) * (1.0 + sc2_ref[0]) + sh2_ref[0]
    packed = _pack_pairs(h2)
    for s in range(ROW_TILE_SUBLANES):
        hp_ref[0, pl.ds(s, tm, stride=ROW_TILE_SUBLANES), :] = packed[:, s * LANES:(s + 1) * LANES]
    tp =-(-tm // LANES) * LANES
    if tp != tm:
        h2 = jnp.concatenate([h2, jnp.zeros((tp - tm, D_MODEL), F32)], axis=0)
    logits_t = _dot_nt_split(wrt_ref[...], h2)
    w = _route_t(_sigmoid(logits_t[:N_EXPERTS, :]), rb_ref[...])
    wt_ref[...] = w[:, :tm]


def _post_call(o_parts, lse_parts, ob, gates, x, gate1, scale2, shift2, p, wa, wb, wo, wrt, rb, tm, mod_per_row):
    nb, t, _ = x.shape
    nt = t // tm
    combine = lse_parts is not None
    rowblk = lambda width: pl.BlockSpec((1, tm, width), lambda b, i: (b, i, 0))
    if mod_per_row:
        mod_spec = rowblk(D_MODEL)
    else:
        mod_spec = pl.BlockSpec((1, 1, D_MODEL), lambda b, i: (b, 0, 0))
    const = lambda shp: pl.BlockSpec(shp, lambda b, i: (0, 0))
    scratch = []
    if combine:
        dils = tuple(o.shape[1] for o in o_parts)
        o_args = list(o_parts) + list(lse_parts)
        o_specs = [pl.BlockSpec((1, d, tm // d, D_GROUP_A), lambda b, i: (b, 0, i, 0)) for d in dils] * 2
        for d in dils:
            if d > 1:
                scratch += [pltpu.VMEM((D_GROUP_A // LANES, tm, LANES), F32)] * 2
    else:
        dils = ()
        o_args = [o_parts[0]]
        o_specs = [rowblk(D_GROUP_A)]
    return pl.pallas_call(
        functools.partial(_post_body, combine=combine, dils=dils),
        out_shape=(jax.ShapeDtypeStruct((nb, t, D_MODEL), F32),
                   jax.ShapeDtypeStruct((nb, t * ROW_TILE_SUBLANES, LANES), I32),
                   jax.ShapeDtypeStruct((N_EXPERTS, nb * t), F32)),
        grid=(nb, nt),
        in_specs=o_specs + [rowblk(D_B), rowblk(2 * D_MODEL), rowblk(D_MODEL),
                            mod_spec, mod_spec, mod_spec, const((1, D_MODEL)), const((1, D_MODEL)),
                            const((D_GROUP_A, D_MODEL)), const((D_B, D_MODEL)), const((D_MODEL, D_MODEL)),
                            const((LANES, D_MODEL)), const((N_EXPERTS, 1))],
        out_specs=(rowblk(D_MODEL),
                   pl.BlockSpec((1, tm * ROW_TILE_SUBLANES, LANES), lambda b, i: (b, i, 0)),
                   pl.BlockSpec((N_EXPERTS, tm), lambda b, i: (0, b * nt + i))),
        scratch_shapes=scratch,
        compiler_params=_cparams(("arbitrary", "arbitrary")),
        name="post",
    )(*o_args, ob, gates, x, gate1, scale2, shift2, p['norm_post_mix'].reshape(1, -1),
      p['norm_pre_ffn'].reshape(1, -1), wa, wb, wo, wrt, rb)


def _rank_body(w_ref, dest_ref, w8_ref, tab_ref, etab_ref, cnt_ref, pst_ref, run_ref, *, n_real, n_slots):
    ph = pl.program_id(0)
    i = pl.program_id(1)
    T = MOE_TILE
    w = w_ref[...]
    sel = (w >= 0.0).astype(F32)
    cnt_tile = jnp.broadcast_to(jnp.sum(sel, axis=1, keepdims=True), (N_EXPERTS, LANES))
    ei = lax.broadcasted_iota(I32, (N_EXPERTS, N_EXPERTS), 0)
    ej = lax.broadcasted_iota(I32, (N_EXPERTS, N_EXPERTS), 1)

    @pl.when((ph == 0) & (i == 0))
    def _():
        cnt_ref[...] = jnp.zeros_like(cnt_ref)

    @pl.when(ph == 0)
    def _():
        cnt_ref[...] += cnt_tile

    @pl.when((ph == 1) & (i == 0))
    def _():
        cnt = cnt_ref[...]
        padded = jnp.floor((cnt + (EXPERT_BLOCK - 1)) / EXPERT_BLOCK) * EXPERT_BLOCK
        pstart = _dot_exact((ej < ei).astype(F32), padded)
        pst_ref[...] = pstart
        run_ref[...] = jnp.zeros_like(run_ref)
        pend = pstart + padded
        vend = pstart + cnt
        esub = lax.broadcasted_iota(I32, (N_EXPERTS, LANES), 0)
        lane = lax.broadcasted_iota(I32, (1, LANES), 1)
        tab_ref[...] = jnp.zeros_like(tab_ref)
        for c in range(tab_ref.shape[1] // LANES):
            bs = ((c * LANES + lane) * EXPERT_BLOCK).astype(F32)
            be = jnp.minimum(jnp.sum((pend <= bs).astype(F32), axis=0, keepdims=True), N_EXPERTS - 1.0)
            tab_ref[0:1, c * LANES:(c + 1) * LANES] = be.astype(I32)
            tab_ref[1:2, c * LANES:(c + 1) * LANES] = (pend[N_EXPERTS - 1:, :] / EXPERT_BLOCK).astype(I32)
        on_diag = esub == lax.broadcasted_iota(I32, (N_EXPERTS, LANES), 1)
        etab_ref[...] = jnp.zeros_like(etab_ref)
        lo = jnp.sum(jnp.where(on_diag, vend, 0.0), axis=0, keepdims=True)
        hi = jnp.sum(jnp.where(on_diag, pend, 0.0), axis=0, keepdims=True)
        etab_ref[0:1, :] = jnp.where(lane == N_EXPERTS, pend[N_EXPERTS - 1:, :], lo).astype(I32)
        etab_ref[1:2, :] = jnp.where(lane == N_EXPERTS, float(n_slots), hi).astype(I32)

    @pl.when(ph == 1)
    def _():
        ti = lax.broadcasted_iota(I32, (T, T), 0)
        tj = lax.broadcasted_iota(I32, (T, T), 1)
        selb = sel.astype(BF16)
        rank = _dot(selb, (ti < tj).astype(BF16))
        ordn = _dot((ej < ei).astype(BF16), selb)
        dest_e = pst_ref[:, :1] + run_ref[:, :1] + rank
        run_ref[...] += cnt_tile
        tok = i * T + lax.broadcasted_iota(I32, (1, T), 1)
        dks, wks = [], []
        for k in range(TOP_K):
            m = (sel > 0.5) & (ordn == float(k))
            dk = jnp.sum(jnp.where(m, dest_e, 0.0), axis=0, keepdims=True)
            wk = jnp.sum(jnp.where(m, w, 0.0), axis=0, keepdims=True)
            dks.append(jnp.where(tok < n_real, dk, 0.0))
            wks.append(jnp.where(tok < n_real, wk, 0.0))
        dest_ref[...] = jnp.concatenate(dks, axis=0).astype(I32)
        w8_ref[...] = jnp.concatenate(wks, axis=0)


def _rank_call(w_t, n_real, n_blocks, n_blocks_pad):
    n = w_t.shape[1]
    nt = n // MOE_TILE
    return pl.pallas_call(
        functools.partial(_rank_body, n_real=n_real, n_slots=n_blocks * EXPERT_BLOCK),
        out_shape=(jax.ShapeDtypeStruct((TOP_K, n), I32),
                   jax.ShapeDtypeStruct((TOP_K, n), F32),
                   jax.ShapeDtypeStruct((8, n_blocks_pad), I32),
                   jax.ShapeDtypeStruct((8, LANES), I32)),
        grid=(2, nt),
        in_specs=[pl.BlockSpec((N_EXPERTS, MOE_TILE), lambda ph, i: (0, i))],
        out_specs=(pl.BlockSpec((TOP_K, MOE_TILE), lambda ph, i: (0, i * ph)),
                   pl.BlockSpec((TOP_K, MOE_TILE), lambda ph, i: (0, i * ph)),
                   pl.BlockSpec((8, n_blocks_pad), lambda ph, i: (0, 0)),
                   pl.BlockSpec((8, LANES), lambda ph, i: (0, 0))),
        scratch_shapes=[pltpu.VMEM((N_EXPERTS, LANES), F32)] * 3,
        compiler_params=_cparams(("arbitrary", "arbitrary")),
        name="rank",
    )(w_t)


def _tile_rows(ref, row, n):
    return ref.at[pl.ds(pl.multiple_of(row * ROW_TILE_SUBLANES, ROW_TILE_SUBLANES), n * ROW_TILE_SUBLANES)]


def _zero_fill(etab_ref, zbuf, xs_hbm, zsem, wait):
    def go(src, dst):
        cp = pltpu.make_async_copy(src, dst, zsem)
        if wait:
            cp.wait()
        else:
            cp.start()

    def per_range(e, carry):
        lo = etab_ref[0, e]
        n = etab_ref[1, e] - lo
        n_full = n // ZERO_ROWS

        def full(j, c):
            go(zbuf, _tile_rows(xs_hbm, lo + j * ZERO_ROWS, ZERO_ROWS))
            return c

        lax.fori_loop(0, n_full, full, 0)
        pos = lo + n_full * ZERO_ROWS
        rem = n - n_full * ZERO_ROWS
        size = ZERO_ROWS // 2
        while size >= 1:
            bit = rem & size

            @pl.when(bit != 0)
            def _(size=size, pos=pos):
                go(_tile_rows(zbuf, 0, size), _tile_rows(xs_hbm, pos, size))

            pos = pos + bit
            size //= 2
        return carry

    lax.fori_loop(0, N_EXPERTS + 1, per_range, 0)


def _dispatch_body(dest_ref, etab_ref, xa_ref, xb_ref, xs_hbm, zbuf, sem, zsem, *, n_real, n_full):
    i = pl.program_id(0)
    T = MOE_TILE
    n_tok = jnp.clip(n_real - i * T, 0, T)

    def issue_from(x_ref):
        def issue(t, carry):
            for k in range(TOP_K):
                pltpu.make_async_copy(_tile_rows(x_ref, t, 1), _tile_rows(xs_hbm, dest_ref[k * T + t], 1),
                                      sem).start(priority=k % 2)
            return carry

        lax.fori_loop(0, n_tok, issue, 0)

    @pl.when(i < n_full)
    def _():
        issue_from(xa_ref)

    @pl.when(i >= n_full)
    def _():
        issue_from(xb_ref)

    @pl.when(i == 0)
    def _():
        zbuf[...] = jnp.zeros_like(zbuf)
        _zero_fill(etab_ref, zbuf, xs_hbm, zsem, wait=False)
        _zero_fill(etab_ref, zbuf, xs_hbm, zsem, wait=True)

    @pl.when(n_tok == T)
    def _():
        pltpu.make_async_copy(_tile_rows(xs_hbm, 0, T * TOP_K), _tile_rows(xs_hbm, 0, T * TOP_K), sem).wait()

    @pl.when(n_tok < T)
    def _():
        def drain(j, carry):
            pltpu.make_async_copy(_tile_rows(xs_hbm, 0, 1), _tile_rows(xs_hbm, 0, 1), sem).wait()
            return carry

        lax.fori_loop(0, n_tok * TOP_K, drain, 0)


def _dispatch_call(dest, etab, hp_a, hp_b, n_real, n_slots):
    tile_rows = MOE_TILE * ROW_TILE_SUBLANES
    n_full = hp_a.shape[0] // tile_rows
    return pl.pallas_call(
        functools.partial(_dispatch_body, n_real=n_real, n_full=n_full),
        out_shape=jax.ShapeDtypeStruct((n_slots * ROW_TILE_SUBLANES, LANES), I32),
        grid=(n_full + 1,),
        in_specs=[pl.BlockSpec((TOP_K * MOE_TILE,), lambda i: (i,), memory_space=pltpu.SMEM),
                  pl.BlockSpec((8, LANES), lambda i: (0, 0), memory_space=pltpu.SMEM),
                  pl.BlockSpec((tile_rows, LANES), lambda i: (jnp.minimum(i, n_full - 1), 0)),
                  pl.BlockSpec((tile_rows, LANES), lambda i: (0, 0))],
        out_specs=pl.BlockSpec(memory_space=pl.ANY),
        scratch_shapes=[pltpu.VMEM((ZERO_ROWS * ROW_TILE_SUBLANES, LANES), I32),
                        pltpu.SemaphoreType.DMA, pltpu.SemaphoreType.DMA],
        compiler_params=_cparams(("arbitrary",)),
        name="dispatch",
    )(dest, etab, hp_a, hp_b)


def _rows_from_tiles(ref, lo, n):
    return jnp.concatenate([ref[pl.ds(lo * ROW_TILE_SUBLANES + s, n, stride=ROW_TILE_SUBLANES), :]
                            for s in range(ROW_TILE_SUBLANES)], axis=1)


def _ffn_body(be_ref, nu_ref, xs_ref, wg_ref, wu_ref, wd_ref, ys_ref, wgb, wub, wdb):
    j = pl.program_id(0)

    @pl.when(j < nu_ref[0])
    def _():
        @pl.when((j == 0) | (be_ref[j] != be_ref[jnp.maximum(j - 1, 0)]))
        def _():
            wgb[...] = wg_ref[0].astype(BF16)
            wub[...] = wu_ref[0].astype(BF16)
            wdb[...] = wd_ref[0].astype(BF16)

        x = _unpack_pairs(_rows_from_tiles(xs_ref, 0, EXPERT_BLOCK)).astype(BF16)
        act = _silu(_dot(x, wgb[...])) * _dot(x, wub[...])
        y = _dot(act.astype(BF16), wdb[...])
        packed = _pack_pairs(y)
        for s in range(ROW_TILE_SUBLANES):
            ys_ref[pl.ds(s, EXPERT_BLOCK, stride=ROW_TILE_SUBLANES), :] = packed[:, s * LANES:(s + 1) * LANES]

    @pl.when(j >= nu_ref[0])
    def _():
        ys_ref[...] = jnp.zeros_like(ys_ref)


def _ffn_call(blk_e, n_used, xs, w_gate, w_up, w_down, n_blocks):
    tile_blk = pl.BlockSpec((EXPERT_BLOCK * ROW_TILE_SUBLANES, LANES), lambda j, be, nu: (j, 0))
    last = lambda j, nu: jnp.minimum(j, nu[0] - 1)
    grid_spec = pltpu.PrefetchScalarGridSpec(
        num_scalar_prefetch=2,
        grid=(n_blocks,),
        in_specs=[pl.BlockSpec((EXPERT_BLOCK * ROW_TILE_SUBLANES, LANES), lambda j, be, nu: (last(j, nu), 0)),
                  pl.BlockSpec((1, D_MODEL, D_EXPERT), lambda j, be, nu: (be[last(j, nu)], 0, 0)),
                  pl.BlockSpec((1, D_MODEL, D_EXPERT), lambda j, be, nu: (be[last(j, nu)], 0, 0)),
                  pl.BlockSpec((1, D_EXPERT, D_MODEL), lambda j, be, nu: (be[last(j, nu)], 0, 0))],
        out_specs=tile_blk,
        scratch_shapes=[pltpu.VMEM((D_MODEL, D_EXPERT), BF16), pltpu.VMEM((D_MODEL, D_EXPERT), BF16),
                        pltpu.VMEM((D_EXPERT, D_MODEL), BF16)])
    return pl.pallas_call(
        _ffn_body,
        out_shape=jax.ShapeDtypeStruct((n_blocks * EXPERT_BLOCK * ROW_TILE_SUBLANES, LANES), I32),
        grid_spec=grid_spec,
        compiler_params=_cparams(("arbitrary",)),
        name="ffn",
    )(blk_e, n_used, xs, w_gate, w_up, w_down)


def _combine_body(dest_ref, dnext_ref, w8_ref, xa_ref, xb_ref, x1a_ref, x1b_ref, g2a_ref, g2b_ref, gain_ref,
                  sg_ref, su_ref, sd_ref, ys_hbm, oa_ref, ob_ref, buf, sem):
    j = pl.program_id(0)
    T = COMBINE_TILE
    RC = COMBINE_ROWS

    def gather(d_ref, slot):
        def issue(t, carry):
            for k in range(TOP_K):
                pltpu.make_async_copy(_tile_rows(ys_hbm, d_ref[k * T + t], 1), _tile_rows(buf.at[slot], k * T + t, 1),
                                      sem.at[slot]).start(priority=k % 2)
            return carry

        lax.fori_loop(0, T, issue, 0, unroll=2)

    def step(slot):
        @pl.when(j + 1 < pl.num_programs(0))
        def _():
            gather(dnext_ref, 1 - slot)

        is_tail = j == 0
        x = _unpack_pairs(jnp.where(is_tail, _rows_from_tiles(xb_ref, 0, T),
                                    _rows_from_tiles(xa_ref, 0, T))).astype(BF16)
        shared = _dot((_silu(_dot(x, sg_ref[...])) * _dot(x, su_ref[...])).astype(BF16), sd_ref[...])
        w_t = jnp.concatenate([w8_ref[...], jnp.zeros((LANES - TOP_K, T), F32)], axis=0).T
        oa_ref[...] = shared
        pltpu.make_async_copy(_tile_rows(ys_hbm, 0, T * TOP_K), buf.at[slot], sem.at[slot]).wait()
        for r0 in range(0, T, RC):
            acc = oa_ref[r0:r0 + RC, :]
            for k in range(TOP_K):
                acc = acc + w_t[r0:r0 + RC, k:k + 1] * _unpack_pairs(_rows_from_tiles(buf.at[slot], k * T + r0, RC))
            x1 = jnp.where(is_tail, x1b_ref[r0:r0 + RC, :], x1a_ref[r0:r0 + RC, :])
            g2 = jnp.where(is_tail, g2b_ref[r0:r0 + RC, :], g2a_ref[0])
            oa_ref[r0:r0 + RC, :] = x1 + g2 * _rms(acc, gain_ref[...])

        @pl.when(is_tail)
        def _():
            ob_ref[...] = oa_ref[...]

    @pl.when(j == 0)
    def _():
        gather(dest_ref, 0)

    @pl.when(j % 2 == 0)
    def _():
        step(0)

    @pl.when(j % 2 == 1)
    def _():
        step(1)


def _combine_call(dest, w8, hp_a, hp_b, x1_a, x1_b, gate2_a, gate2_b, gain, wsg, wsu, wsd, ys):
    T = COMBINE_TILE
    tile_rows = T * ROW_TILE_SUBLANES
    n_full = hp_a.shape[0] // tile_rows
    n_tiles = n_full + 1
    seq = x1_a.shape[0] // gate2_a.shape[0]
    tile_of = lambda j: jnp.where(j == 0, n_full, j - 1)
    full_of = lambda j: jnp.maximum(j - 1, 0)
    const = lambda shp: pl.BlockSpec(shp, lambda j: (0, 0))
    return pl.pallas_call(
        _combine_body,
        out_shape=(jax.ShapeDtypeStruct((n_full * T, D_MODEL), F32), jax.ShapeDtypeStruct((T, D_MODEL), F32)),
        grid=(n_tiles,),
        in_specs=[pl.BlockSpec((TOP_K * T,), lambda j: (tile_of(j),), memory_space=pltpu.SMEM),
                  pl.BlockSpec((TOP_K * T,), lambda j: (tile_of(jnp.minimum(j + 1, n_tiles - 1)),),
                               memory_space=pltpu.SMEM),
                  pl.BlockSpec((TOP_K, T), lambda j: (0, tile_of(j))),
                  pl.BlockSpec((tile_rows, LANES), lambda j: (full_of(j), 0)),
                  pl.BlockSpec((tile_rows, LANES), lambda j: (0, 0)),
                  pl.BlockSpec((T, D_MODEL), lambda j: (full_of(j), 0)),
                  const((T, D_MODEL)),
                  pl.BlockSpec((1, 1, D_MODEL), lambda j: (full_of(j) * T // seq, 0, 0)),
                  const((T, D_MODEL)), const((1, D_MODEL)),
                  const((D_MODEL, D_EXPERT)), const((D_MODEL, D_EXPERT)), const((D_EXPERT, D_MODEL)),
                  pl.BlockSpec(memory_space=pl.ANY)],
        out_specs=(pl.BlockSpec((T, D_MODEL), lambda j: (full_of(j), 0)), const((T, D_MODEL))),
        scratch_shapes=[pltpu.VMEM((2, TOP_K * tile_rows, LANES), I32), pltpu.SemaphoreType.DMA((2,))],
        compiler_params=_cparams(("arbitrary",)),
        name="combine",
    )(dest, dest, w8, hp_a, hp_b, x1_a, x1_b, gate2_a, gate2_b, gain.reshape(1, -1), wsg, wsu, wsd, ys)


def _rope_tables(pos):
    half = HEAD_DIM_A // 2
    inv_freq = ROPE_THETA ** (-jnp.arange(half, dtype=F32) / half)
    ang = pos.astype(F32)[:, None] * inv_freq[None, :]
    cos = jnp.cos(ang)
    sin = jnp.sin(ang)
    reps = LANES // HEAD_DIM_A
    cos_t = jnp.tile(jnp.concatenate([cos, cos], axis=1), (1, reps))
    sin_t = jnp.tile(jnp.concatenate([-sin, sin], axis=1), (1, reps))
    return cos_t, sin_t


def kernel(x_prompt, x_sample, c_prompt, c_sample, cache_a1_kv, cache_a2_kv, cache_a3_kv, state_b_wkv, state_b_shift, w_ada, b_ada, norm_pre_mix, norm_post_mix, norm_pre_ffn, norm_post_ffn, w_in, w_a_out, mu_b, w0_b, w_w2_b, a0_b, w_a2_b, w_g2_b, k_k_b, k_a_b, r_k_b, ln_x_w_b, ln_x_b_b, w_b_out, w_out, w_router, router_bias, w_e_gate, w_e_up, w_e_down, w_s_gate, w_s_up, w_s_down):
    assert DEPTH == 1
    l = 0
    nd = DEC_BATCH
    row = lambda a: a.reshape(1, -1)
    p = {'mu_b': row(mu_b[l]), 'w0_b': row(w0_b[l]), 'w_w2_b': w_w2_b[l], 'a0_b': row(a0_b[l]),
         'w_a2_b': w_a2_b[l], 'w_g2_b': w_g2_b[l], 'k_k_b': row(k_k_b[l]), 'k_a_b': row(k_a_b[l]),
         'r_k_b': row(r_k_b[l]), 'ln_x_w_b': row(ln_x_w_b[l]), 'ln_x_b_b': row(ln_x_b_b[l]),
         'norm_post_mix': norm_post_mix[l], 'norm_pre_ffn': norm_pre_ffn[l]}

    wq, wf, wg = _wsplit_call(w_in[l])
    wa = w_a_out[l].astype(BF16)
    wb = w_b_out[l].astype(BF16)
    wo = w_out[l].astype(BF16)
    wrt = jnp.concatenate([w_router[l].T, jnp.zeros((LANES - N_EXPERTS, D_MODEL), F32)], axis=0)
    rb = router_bias[l].reshape(N_EXPERTS, 1)
    wsg, wsu, wsd = w_s_gate[l].astype(BF16), w_s_up[l].astype(BF16), w_s_down[l].astype(BF16)

    n_c = BATCH + nd
    c_all = jnp.concatenate([c_prompt, c_sample, jnp.zeros((-n_c % 8, D_MODEL), F32)], axis=0)
    mod = _mod_call(c_all, w_ada[l], b_ada[l])
    mod_p = [m.reshape(BATCH, 1, D_MODEL) for m in jnp.split(mod[:BATCH], 6, axis=-1)]
    mod_s = [m.reshape(1, nd, D_MODEL) for m in jnp.split(mod[BATCH:n_c], 6, axis=-1)]

    cos_p, sin_p = _rope_tables(jnp.arange(SEQ, dtype=I32))
    cos_s, sin_s = _rope_tables(jnp.full((nd,), PAST_LEN, I32))

    keep_p = [min(w, SEQ) for w, _ in DILATED_GROUPS]
    dils = tuple(d for _, d in DILATED_GROUPS)

    q0, q1, q2, feat_p, gates_p, *tails_p = _inproj_call(
        x_prompt, norm_pre_mix[l], mod_p[1], mod_p[0], cos_p, sin_p, wq, wf, wg,
        tm=256, keeps=keep_p, mod_per_row=False, dils=dils)
    o_parts, lse_parts = [], []
    for gi, qg in enumerate((q0, q1, q2)):
        o, lse = _attn_call(qg, gi)
        o_parts.append(o)
        lse_parts.append(lse)
    ob_p, wkv_p = _wkv_call(feat_p, p)
    x1_p, hp_p, wt_p = _post_call(o_parts, lse_parts, ob_p, gates_p, x_prompt, mod_p[2], mod_p[4], mod_p[3],
                                  p, wa, wb, wo, wrt, rb, tm=512, mod_per_row=False)

    xs3 = x_sample.reshape(1, nd, D_MODEL)
    s0, s1, s2, feat_s, gates_s, *tails_s = _inproj_call(
        xs3, norm_pre_mix[l], mod_s[1], mod_s[0], cos_s, sin_s, wq, wf, wg,
        tm=nd, keeps=(nd,) * N_GROUPS_A, mod_per_row=True, dils=(1, 1, 1))
    qkv_s = jnp.stack([z.reshape(nd, 3, N_HEADS_A, HEAD_DIM_A) for z in (s0, s1, s2)], axis=2)
    qkv_s = qkv_s.reshape(nd, 3 * N_GROUPS_A * N_HEADS_A, HEAD_DIM_A).astype(F32)
    oa_s = _sattn_call(qkv_s, cache_a1_kv[l], cache_a2_kv[l], cache_a3_kv[l])
    r_s, w_s, k_s, v_s, aa_s, bb_s, g_s = _swkv_prep_call(feat_s[0], state_b_shift[l], p)
    nh = nd * N_HEADS_B
    as_row = lambda a: a.reshape(nh, 1, HEAD_DIM_B)
    s_new, y_col = _swkv_step_call(state_b_wkv[l].reshape(nh, HEAD_DIM_B, HEAD_DIM_B), as_row(aa_s), as_row(w_s),
                                   as_row(bb_s), as_row(k_s), as_row(r_s), v_s.reshape(nh, HEAD_DIM_B))
    ob_s = _swkv_fin_call(y_col.reshape(nd, D_B), r_s, k_s, v_s, g_s, p)
    x1_s, hp_s, wt_s = _post_call([oa_s.reshape(1, nd, D_GROUP_A)], None, ob_s.reshape(1, nd, D_B), gates_s, xs3,
                                  mod_s[2], mod_s[4], mod_s[3], p, wa, wb, wo, wrt, rb, tm=nd, mod_per_row=True)

    n_p = BATCH * SEQ
    n_real = n_p + nd
    n_all = -(-n_real // MOE_TILE) * MOE_TILE
    pad = n_all - n_real
    n_blocks = -(-(n_real * TOP_K) // EXPERT_BLOCK) + N_EXPERTS
    n_blocks_pad = -(-n_blocks // LANES) * LANES
    assert n_p % MOE_TILE == 0 and nd <= MOE_TILE
    hp_a = hp_p.reshape(n_p * ROW_TILE_SUBLANES, LANES)
    hp_b = jnp.concatenate([hp_s[0], jnp.zeros((pad * ROW_TILE_SUBLANES, LANES), I32)], axis=0)
    wt_all = jnp.concatenate([wt_p, wt_s, jnp.full((N_EXPERTS, pad), -1.0, F32)], axis=1)
    dest8, w8, tab, etab = _rank_call(wt_all, n_real, n_blocks, n_blocks_pad)
    dest = dest8.reshape(TOP_K, n_all // MOE_TILE, MOE_TILE).transpose(1, 0, 2).reshape(-1)
    xs = _dispatch_call(dest, etab, hp_a, hp_b, n_real, n_blocks * EXPERT_BLOCK)
    ys = _ffn_call(tab[0], tab[1, :1], xs, w_e_gate[l], w_e_up[l], w_e_down[l], n_blocks)
    n_ct = n_p // COMBINE_TILE + 1
    dest_c = dest8[:, :n_ct * COMBINE_TILE].reshape(TOP_K, n_ct, COMBINE_TILE).transpose(1, 0, 2).reshape(-1)
    pad_rows = lambda z: jnp.concatenate([z, jnp.zeros((COMBINE_TILE - nd, D_MODEL), F32)], axis=0)
    out_p, out_s = _combine_call(dest_c, w8, hp_a, hp_b, x1_p.reshape(n_p, D_MODEL), pad_rows(x1_s[0]), mod_p[5],
                                 pad_rows(mod_s[5][0]), norm_post_ffn[l], wsg, wsu, wsd, ys)
    y_prompt = out_p.reshape(BATCH, SEQ, D_MODEL)
    y_sample = out_s[:nd]

    a_p = [z.reshape(1, BATCH, kp, 2, N_HEADS_A, HEAD_DIM_A) for z, kp in zip(tails_p, keep_p)]
    a_s = [z.reshape(1, nd, DEC_SEQ, 2, N_HEADS_A, HEAD_DIM_A) for z in tails_s]
    shift_p = feat_p[:, -1][None]
    shift_s = feat_s[0][None]
    return (y_prompt, y_sample.reshape(nd, DEC_SEQ, D_MODEL), a_p[0], a_p[1], a_p[2], wkv_p[None], shift_p,
            a_s[0], a_s[1], a_s[2], s_new.reshape(1, nd, N_HEADS_B, HEAD_DIM_B, HEAD_DIM_B), shift_s)
```

```python
import functools
import math

import jax
import jax.numpy as jnp
from jax import lax
from jax.experimental import pallas as pl
from jax.experimental.pallas import tpu as pltpu

F32 = jnp.float32
BF16 = jnp.bfloat16
I32 = jnp.int32

D_MODEL = 1024
BATCH = 2
SEQ = 8192
DEPTH = 1
DEC_BATCH = 32
DEC_SEQ = 1
PAST_LEN = 16384

HEAD_DIM_A = 64
N_HEADS_A = 8
DILATED_GROUPS = ((128, 1), (512, 4), (2048, 16))
N_GROUPS_A = 3
D_GROUP_A = N_HEADS_A * HEAD_DIM_A
D_A = N_GROUPS_A * D_GROUP_A
D_QKV = 3 * D_A
BAND_BLOCK = 128
ROPE_THETA = 10000.0

HEAD_DIM_B = 64
N_HEADS_B = 16
D_B = 1024
DECAY_LORA = 64
AAA_LORA = 64
GATE_LORA = 160
D_SHIFT_B = 3 * D_B + DECAY_LORA + AAA_LORA + GATE_LORA
LN_X_EPS = 64e-5

N_EXPERTS = 64
TOP_K = 8
N_EXPERT_GROUPS = 8
TOPK_GROUPS = 4
D_EXPERT = 256
ROUTED_SCALE = 2.5
EXPERT_BLOCK = 512
NORM_EPS = 1e-6

LANES = 128
WKV_CHUNK = 64
MOE_TILE = 1024
COMBINE_TILE = 256
COMBINE_ROWS = 32
VMEM_LIMIT = 56 * 1024 * 1024
ROW_TILE_SUBLANES = D_MODEL // (2 * LANES)
ZERO_ROWS = 256


def _cparams(sem):
    return pltpu.CompilerParams(dimension_semantics=sem, vmem_limit_bytes=VMEM_LIMIT)


def _dot(a, b):
    return jnp.dot(a, b, preferred_element_type=F32)


def _dot_nt(a, b):
    return lax.dot_general(a, b, (((1,), (1,)), ((), ())), preferred_element_type=F32)


def _dot_tn(a, b):
    return lax.dot_general(a, b, (((0,), (0,)), ((), ())), preferred_element_type=F32)


def _dot_nt_split(a, b):
    ah = a.astype(BF16)
    al = (a - ah.astype(F32)).astype(BF16)
    bh = b.astype(BF16)
    bl = (b - bh.astype(F32)).astype(BF16)
    return _dot_nt(ah, bh) + _dot_nt(ah, bl) + _dot_nt(al, bh)


def _dot_exact(a, b):
    return lax.dot_general(a, b, (((1,), (0,)), ((), ())), precision=lax.Precision.HIGHEST,
                           preferred_element_type=F32)


def _rms(x, gain):
    return x * lax.rsqrt(jnp.mean(x * x, axis=-1, keepdims=True) + NORM_EPS) * gain


def _sigmoid(x):
    return 1.0 / (1.0 + jnp.exp(-x))


def _silu(x):
    return x * _sigmoid(x)


def _softplus(x):
    return jnp.maximum(x, 0.0) + jnp.log(1.0 + jnp.exp(-jnp.abs(x)))


def _pack_pairs(x):
    half = D_MODEL // 2
    lo = lax.bitcast_convert_type(x[:, :half].astype(BF16).astype(F32), I32)
    hi = lax.bitcast_convert_type(x[:, half:].astype(BF16).astype(F32), I32)
    return lax.shift_right_logical(lo, 16) | (hi & jnp.int32(-65536))


def _unpack_pairs(w):
    lo = lax.bitcast_convert_type(w << 16, F32)
    hi = lax.bitcast_convert_type(w & jnp.int32(-65536), F32)
    return jnp.concatenate([lo, hi], axis=1)


def _mod_body(c_ref, w_ref, b_ref, o_ref):
    s = _silu(c_ref[...]).astype(BF16)
    o_ref[...] = _dot(s, w_ref[...].astype(BF16)) + b_ref[...]


def _mod_call(c_all, w_ada, b_ada):
    rows = c_all.shape[0]
    tn = 1536
    return pl.pallas_call(
        _mod_body,
        out_shape=jax.ShapeDtypeStruct((rows, 6 * D_MODEL), F32),
        grid=(6 * D_MODEL // tn,),
        in_specs=[pl.BlockSpec((rows, D_MODEL), lambda j: (0, 0)),
                  pl.BlockSpec((D_MODEL, tn), lambda j: (0, j)),
                  pl.BlockSpec((1, tn), lambda j: (0, j))],
        out_specs=pl.BlockSpec((rows, tn), lambda j: (0, j)),
        compiler_params=_cparams(("arbitrary",)),
        name="mod",
    )(c_all, w_ada, b_ada.reshape(1, -1))


def _wsplit_body(w_ref, q_ref, f_ref, g_ref):
    w = w_ref[...]
    q_ref[...] = w[:, :D_QKV].astype(BF16)
    f_ref[...] = w[:, D_QKV:D_QKV + D_SHIFT_B].astype(BF16)
    g_ref[...] = w[:, D_QKV + D_SHIFT_B:].astype(BF16)


def _wsplit_call(w):
    rows, cols = w.shape
    tr = 128
    widths = (D_QKV, D_SHIFT_B, cols - D_QKV - D_SHIFT_B)
    return pl.pallas_call(
        _wsplit_body,
        out_shape=tuple(jax.ShapeDtypeStruct((rows, n), BF16) for n in widths),
        grid=(rows // tr,),
        in_specs=[pl.BlockSpec((tr, cols), lambda i: (i, 0))],
        out_specs=tuple(pl.BlockSpec((tr, n), lambda i: (i, 0)) for n in widths),
        compiler_params=_cparams(("arbitrary",)),
        name="wsplit",
    )(w)


def _inproj_body(x_ref, g_ref, sc_ref, sh_ref, cos_ref, sin_ref, wq_ref, wf_ref, wg_ref,
                 q0_ref, q1_ref, q2_ref, feat_ref, gate_ref, t0_ref, t1_ref, t2_ref, p_ref, *, dils):
    x = x_ref[0]
    tm = x.shape[0]
    h = _rms(x, g_ref[...]) * (1.0 + sc_ref[0]) + sh_ref[0]
    hb = h.astype(BF16)
    p = _dot(hb, wq_ref[...])
    cos = cos_ref[...]
    sin = sin_ref[...]
    lane = lax.broadcasted_iota(I32, cos.shape, 1)
    first_half = (lane % HEAD_DIM_A) < (HEAD_DIM_A // 2)
    for c in range(2 * D_A // LANES):
        xc = p[:, c * LANES:(c + 1) * LANES]
        partner = jnp.where(first_half, pltpu.roll(xc, LANES - HEAD_DIM_A // 2, 1),
                            pltpu.roll(xc, HEAD_DIM_A // 2, 1))
        rc = xc * cos + partner * sin
        if c < D_A // LANES:
            rc = rc * (HEAD_DIM_A ** -0.5)
        p_ref[c] = rc
    for c in range(2 * D_A // LANES, D_QKV // LANES):
        p_ref[c] = p[:, c * LANES:(c + 1) * LANES]
    per_group = D_GROUP_A // LANES
    for gi, t_ref in enumerate((t0_ref, t1_ref, t2_ref)):
        rows = t_ref.shape[1]
        for which in (1, 2):
            for j in range(per_group):
                c = (which * D_A + gi * D_GROUP_A) // LANES + j
                t_ref[0, :, (which - 1) * D_GROUP_A + j * LANES:(which - 1) * D_GROUP_A + (j + 1) * LANES] = \
                    p_ref[c, tm - rows:tm, :]
    for gi, (out_ref, dil) in enumerate(zip((q0_ref, q1_ref, q2_ref), dils)):
        for which in range(3):
            for j in range(per_group):
                c = (which * D_A + gi * D_GROUP_A) // LANES + j
                dst = slice(which * D_GROUP_A + j * LANES, which * D_GROUP_A + (j + 1) * LANES)
                if dil == 1:
                    out_ref[0, 0, :, dst] = p_ref[c].astype(BF16)
                else:
                    for r in range(dil):
                        out_ref[0, r, :, dst] = p_ref[c, pl.ds(r, tm // dil, stride=dil), :].astype(BF16)
    feat_ref[0] = _dot(hb, wf_ref[...])
    gate_ref[0] = _sigmoid(_dot(hb, wg_ref[...])).astype(BF16)


def _inproj_call(x, gain, scale, shift, cos_t, sin_t, wq, wf, wg, tm, keeps, mod_per_row, dils):
    nb, t, _ = x.shape
    nt = t // tm

    def tail_spec(keep):
        if keep <= tm:
            return pl.BlockSpec((1, keep, 2 * D_GROUP_A), lambda b, i: (b, 0, 0))
        first = (t - keep) // tm
        return pl.BlockSpec((1, tm, 2 * D_GROUP_A), lambda b, i: (b, jnp.maximum(i - first, 0), 0))

    if mod_per_row:
        mod_spec = pl.BlockSpec((1, tm, D_MODEL), lambda b, i: (b, i, 0))
    else:
        mod_spec = pl.BlockSpec((1, 1, D_MODEL), lambda b, i: (b, 0, 0))
    resident = lambda shp: pl.BlockSpec(shp, lambda b, i: (0, 0), pipeline_mode=pl.Buffered(1))
    q_shapes = tuple(jax.ShapeDtypeStruct((nb, d, t // d, 3 * D_GROUP_A), BF16) for d in dils)
    q_specs = tuple(pl.BlockSpec((1, d, tm // d, 3 * D_GROUP_A), lambda b, i: (b, 0, i, 0)) for d in dils)
    return pl.pallas_call(
        functools.partial(_inproj_body, dils=dils),
        out_shape=q_shapes + (jax.ShapeDtypeStruct((nb, t, D_SHIFT_B), F32),
                              jax.ShapeDtypeStruct((nb, t, 2 * D_MODEL), BF16),
                              ) + tuple(jax.ShapeDtypeStruct((nb, kp, 2 * D_GROUP_A), F32) for kp in keeps),
        grid=(nb, nt),
        in_specs=[pl.BlockSpec((1, tm, D_MODEL), lambda b, i: (b, i, 0)),
                  pl.BlockSpec((1, D_MODEL), lambda b, i: (0, 0)),
                  mod_spec, mod_spec,
                  pl.BlockSpec((tm, LANES), lambda b, i: (i, 0)),
                  pl.BlockSpec((tm, LANES), lambda b, i: (i, 0)),
                  resident((D_MODEL, D_QKV)), resident((D_MODEL, D_SHIFT_B)),
                  resident((D_MODEL, 2 * D_MODEL))],
        out_specs=q_specs + (pl.BlockSpec((1, tm, D_SHIFT_B), lambda b, i: (b, i, 0)),
                             pl.BlockSpec((1, tm, 2 * D_MODEL), lambda b, i: (b, i, 0)),
                             ) + tuple(tail_spec(kp) for kp in keeps),
        scratch_shapes=[pltpu.VMEM((D_QKV // LANES, tm, LANES), F32)],
        compiler_params=_cparams(("arbitrary", "arbitrary")),
        name="inproj",
    )(x, gain.reshape(1, -1), scale, shift, cos_t, sin_t, wq, wf, wg)


def _attn_body(q_ref, kc_ref, kp_ref, vc_ref, vp_ref, o_ref, lse_ref):
    mb = pl.program_id(2)
    nq = q_ref.shape[2] // BAND_BLOCK
    q = q_ref[0, 0]
    k = jnp.concatenate([kp_ref[0, 0], kc_ref[0, 0]], axis=0)
    v = jnp.concatenate([vp_ref[0, 0], vc_ref[0, 0]], axis=0)
    qi = lax.broadcasted_iota(I32, (BAND_BLOCK, 2 * BAND_BLOCK), 0)
    ki = lax.broadcasted_iota(I32, (BAND_BLOCK, 2 * BAND_BLOCK), 1)
    dist = qi + BAND_BLOCK - ki
    band = (dist >= 0) & (dist <= BAND_BLOCK)
    masks = [band & ((ki >= BAND_BLOCK) | (mb > 0))] + [band] * (nq - 1)
    lane_q = lax.broadcasted_iota(I32, (BAND_BLOCK, LANES), 1)
    lane_k = lax.broadcasted_iota(I32, (2 * BAND_BLOCK, LANES), 1)
    for hp in range(N_HEADS_A // 2):
        sl = slice(hp * LANES, (hp + 1) * LANES)
        chains = [(j, sub) for j in range(nq) for sub in range(2)]
        qs = [q[j * BAND_BLOCK:(j + 1) * BAND_BLOCK, sl] for j in range(nq)]
        ks = [k[j * BAND_BLOCK:(j + 2) * BAND_BLOCK, sl] for j in range(nq)]
        vs = [v[j * BAND_BLOCK:(j + 2) * BAND_BLOCK, sl] for j in range(nq)]
        mqs = [lane_q < HEAD_DIM_A, lane_q >= HEAD_DIM_A]
        mks = [lane_k < HEAD_DIM_A, lane_k >= HEAD_DIM_A]
        s = [jnp.where(masks[j], _dot_nt(jnp.where(mqs[sub], qs[j], jnp.zeros_like(qs[j])), ks[j]), -jnp.inf)
             for j, sub in chains]
        mx = [jnp.max(z, axis=1, keepdims=True) for z in s]
        p = [jnp.exp(z - m) for z, m in zip(s, mx)]
        l = [jnp.sum(z, axis=1, keepdims=True) for z in p]
        pv = [_dot(p[c].astype(BF16), jnp.where(mks[sub], vs[j], jnp.zeros_like(vs[j])))
              for c, (j, sub) in enumerate(chains)]
        for j in range(nq):
            c0, c1 = 2 * j, 2 * j + 1
            o_pair = pv[c0] / l[c0] + pv[c1] / l[c1]
            lse_pair = jnp.where(mqs[0], mx[c0] + jnp.log(l[c0]), mx[c1] + jnp.log(l[c1]))
            o_ref[0, 0, j * BAND_BLOCK:(j + 1) * BAND_BLOCK, sl] = o_pair.astype(BF16)
            lse_ref[0, 0, j * BAND_BLOCK:(j + 1) * BAND_BLOCK, sl] = lse_pair


def _attn_call(qkv_g, gi):
    b, dil, l, _ = qkv_g.shape
    nq = 4
    nb = l // (nq * BAND_BLOCK)
    blk = (1, 1, nq * BAND_BLOCK, D_GROUP_A)
    cur = lambda which: pl.BlockSpec(blk, lambda bb, r, m: (bb, r, m, which))
    prev = lambda which: pl.BlockSpec((1, 1, BAND_BLOCK, D_GROUP_A),
                                      lambda bb, r, m: (bb, r, jnp.maximum(nq * m - 1, 0), which))
    return pl.pallas_call(
        _attn_body,
        out_shape=(jax.ShapeDtypeStruct((b, dil, l, D_GROUP_A), BF16),
                   jax.ShapeDtypeStruct((b, dil, l, D_GROUP_A), F32)),
        grid=(b, dil, nb),
        in_specs=[cur(0), cur(1), prev(1), cur(2), prev(2)],
        out_specs=(pl.BlockSpec(blk, lambda bb, r, m: (bb, r, m, 0)),
                   pl.BlockSpec(blk, lambda bb, r, m: (bb, r, m, 0))),
        compiler_params=_cparams(("arbitrary", "arbitrary", "arbitrary")),
        name=f"attn{gi}",
    )(qkv_g, qkv_g, qkv_g, qkv_g, qkv_g)


def _sattn_body(qkv_ref, b1_ref, b2_ref, b3_ref, o_ref):
    n_rows = 3 * N_GROUPS_A * N_HEADS_A
    sq = jnp.concatenate([qkv_ref[0], jnp.zeros((LANES - n_rows, HEAD_DIM_A), F32)], axis=0)
    cols = jnp.concatenate([sq, jnp.zeros((LANES, LANES - HEAD_DIM_A), F32)], axis=1).T
    col3 = lambda first: jnp.stack([cols[:HEAD_DIM_A, first + h:first + h + 1] for h in range(N_HEADS_A)], axis=0)
    outs, lses = [], []
    for g, (buf_ref, (_, dil)) in enumerate(zip((b1_ref, b2_ref, b3_ref), DILATED_GROUPS)):
        q = col3(g * N_HEADS_A)
        kn = col3((N_GROUPS_A + g) * N_HEADS_A)
        vn = col3((2 * N_GROUPS_A + g) * N_HEADS_A)
        kb = buf_ref[0, 0]
        vb = buf_ref[0, 1]
        wb = kb.shape[-1]
        pos = lax.broadcasted_iota(I32, (1, 1, wb), 2)
        s = jnp.sum(kb * q, axis=1, keepdims=True)
        s = jnp.where(pos % dil == 0, s, -jnp.inf)
        sn = jnp.sum(kn * q, axis=1, keepdims=True)
        m = jnp.maximum(jnp.max(s, axis=2, keepdims=True), sn)
        p = jnp.exp(s - m)
        pn = jnp.exp(sn - m)
        l = jnp.sum(p, axis=2, keepdims=True) + pn
        outs.append((jnp.sum(p * vb, axis=2, keepdims=True) + pn * vn) / l)
        lses.append(m + jnp.log(l))
    mx = jnp.maximum(jnp.maximum(lses[0], lses[1]), lses[2])
    es = [jnp.exp(z - mx) for z in lses]
    o_a = (es[0] * outs[0] + es[1] * outs[1] + es[2] * outs[2]) / (es[0] + es[1] + es[2])
    o_cols = jnp.concatenate([o_a[h] for h in range(N_HEADS_A)] +
                             [jnp.zeros((HEAD_DIM_A, LANES - N_HEADS_A), F32)], axis=1)
    o_rows = jnp.concatenate([o_cols, jnp.zeros((LANES - HEAD_DIM_A, LANES), F32)], axis=0).T
    o_ref[0] = o_rows[:N_HEADS_A, :HEAD_DIM_A]


def _sattn_call(qkv_s, c1, c2, c3):
    n = qkv_s.shape[0]
    views, specs = [], []
    for c in (c1, c2, c3):
        wb = c.shape[1]
        views.append(jnp.transpose(c, (0, 2, 3, 4, 1)))
        specs.append(pl.BlockSpec((1, 2, N_HEADS_A, HEAD_DIM_A, wb), lambda b: (b, 0, 0, 0, 0)))
    return pl.pallas_call(
        _sattn_body,
        out_shape=jax.ShapeDtypeStruct((n, N_HEADS_A, HEAD_DIM_A), F32),
        grid=(n,),
        in_specs=[pl.BlockSpec((1, 3 * N_GROUPS_A * N_HEADS_A, HEAD_DIM_A), lambda b: (b, 0, 0))] + specs,
        out_specs=pl.BlockSpec((1, N_HEADS_A, HEAD_DIM_A), lambda b: (b, 0, 0)),
        compiler_params=_cparams(("arbitrary",)),
        name="sattn",
    )(qkv_s, *views)


def _rwkv_features(xs, w0, ww2, a0, wa2, wg2, k_a):
    r = xs[:, :D_B]
    k = xs[:, D_B:2 * D_B]
    v = xs[:, 2 * D_B:3 * D_B]
    xw = xs[:, 3 * D_B:3 * D_B + DECAY_LORA]
    xa = xs[:, 3 * D_B + DECAY_LORA:3 * D_B + DECAY_LORA + AAA_LORA]
    xg = xs[:, 3 * D_B + DECAY_LORA + AAA_LORA:]
    w_log = -_softplus(-(w0 + _dot(jnp.tanh(xw).astype(BF16), ww2.astype(BF16)))) - 0.5
    a = _sigmoid(a0 + _dot(xa.astype(BF16), wa2.astype(BF16)))
    g = _dot(_sigmoid(xg).astype(BF16), wg2.astype(BF16))
    k_h = k * (1.0 + (a - 1.0) * k_a)
    return r, k, v, w_log, a, g, k_h


def _head_norm(kk_h):
    nrm = jnp.sqrt(jnp.sum(kk_h * kk_h, axis=-1, keepdims=True))
    return kk_h / jnp.maximum(nrm, 1e-12)


def _wkv_finish_head(y, r_h, k_h, v_h, g_h, rk_h, lnw_h, lnb_h):
    mean = jnp.mean(y, axis=-1, keepdims=True)
    var = jnp.mean(jnp.square(y - mean), axis=-1, keepdims=True)
    yn = (y - mean) * lax.rsqrt(var + LN_X_EPS) * lnw_h + lnb_h
    bonus = jnp.sum(r_h * k_h * rk_h, axis=-1, keepdims=True) * v_h
    return (yn + bonus) * g_h


def _wkv_body(f_ref, fp_ref, mu_ref, w0_ref, ww2_ref, a0_ref, wa2_ref, wg2_ref, kk_ref, ka_ref,
              rk_ref, lnw_ref, lnb_ref, o_ref, st_ref, s_ref):
    c = pl.program_id(0)
    C = WKV_CHUNK
    nb = f_ref.shape[0]

    @pl.when(c == 0)
    def _():
        s_ref[...] = jnp.zeros_like(s_ref)

    f = jnp.concatenate([f_ref[b] for b in range(nb)], axis=0)
    row = lax.broadcasted_iota(I32, f.shape, 0)
    prev = pltpu.roll(f, 1, 0)
    for b in range(nb):
        prev = jnp.where(row == b * C, jnp.where(c == 0, 0.0, fp_ref[b][7:8, :]), prev)
    xs = f + mu_ref[...] * (prev - f)
    r, k, v, w_log, a, g, k_h = _rwkv_features(xs, w0_ref[...], ww2_ref[...], a0_ref[...],
                                               wa2_ref[...], wg2_ref[...], ka_ref[...])
    lw = -jnp.exp(w_log)
    kk = k * kk_ref[...]
    jh = lax.broadcasted_iota(I32, (D_B, LANES), 0) // HEAD_DIM_B
    ind = (jh == lax.broadcasted_iota(I32, (D_B, LANES), 1)).astype(BF16)
    ind_t = (lax.broadcasted_iota(I32, (LANES, D_B), 0)
             == lax.broadcasted_iota(I32, (LANES, D_B), 1) // HEAD_DIM_B).astype(BF16)

    def head_sum(z):
        hi = z.astype(BF16)
        lo = (z - hi.astype(F32)).astype(BF16)
        s = _dot(hi, ind) + _dot(lo, ind)
        shi = s.astype(BF16)
        slo = (s - shi.astype(F32)).astype(BF16)
        return _dot(shi, ind_t) + _dot(slo, ind_t)

    kkn = kk / jnp.maximum(jnp.sqrt(head_sum(kk * kk)), 1e-12)

    tr = lax.broadcasted_iota(I32, (nb * C, nb * C), 0)
    sr_ = lax.broadcasted_iota(I32, (nb * C, nb * C), 1)
    tri_incl = ((tr >= sr_) & (tr // C == sr_ // C)).astype(BF16)
    l1 = lw.astype(BF16)
    r1 = lw - l1.astype(F32)
    l2 = r1.astype(BF16)
    l3 = (r1 - l2.astype(F32)).astype(BF16)
    cum = _dot(tri_incl, l1) + _dot(tri_incl, l2) + _dot(tri_incl, l3)
    rhos = [cum[b * C + C // 2 - 1:b * C + C // 2, :] for b in range(nb)]
    rho = jnp.concatenate([jnp.broadcast_to(z, (C, D_B)) for z in rhos], axis=0)
    ep = jnp.exp(cum - rho)
    em = jnp.exp(rho - cum)
    e_a = ep * jnp.exp(-lw)
    r_hat = r * ep
    k_hat = k_h * em
    e_rs = [jnp.exp(z) for z in rhos]
    e_cs = [jnp.exp(cum[b * C + C - 1:b * C + C, :] - rhos[b]) for b in range(nb)]

    ti = lax.broadcasted_iota(I32, (C, C), 0)
    si = lax.broadcasted_iota(I32, (C, C), 1)
    strict = ti > si
    incl = ti >= si
    rk = rk_ref[...]
    lnw = lnw_ref[...]
    lnb = lnb_ref[...]
    items = [(b, h) for b in range(nb) for h in range(N_HEADS_B)]
    heads = range(len(items))
    lanes = [slice(h * HEAD_DIM_B, (h + 1) * HEAD_DIM_B) for _, h in items]
    cut = lambda z, i: z[items[i][0] * C:(items[i][0] + 1) * C, lanes[i]]
    e_r = [e_rs[b][:, lanes[i]] for i, (b, _) in enumerate(items)]
    e_c = [e_cs[b][:, lanes[i]] for i, (b, _) in enumerate(items)]
    a_hat_full = (-kkn * e_a).astype(BF16)
    b_hat_full = (kkn * a * em).astype(BF16)
    a_hat_b = [cut(a_hat_full, h) for h in heads]
    b_hat_b = [cut(b_hat_full, h) for h in heads]
    rh = [cut(r_hat, h) for h in heads]
    vb = [cut(v, h).astype(BF16) for h in heads]
    bk = [jnp.concatenate([b_hat_b[h], cut(k_hat, h).astype(BF16)], axis=0) for h in heads]
    p = [_dot_nt(jnp.concatenate([a_hat_b[h], rh[h].astype(BF16)], axis=0), bk[h]) for h in heads]
    l_ak = [jnp.where(strict, z[:C, C:], 0.0).astype(BF16) for z in p]
    p_rb = [jnp.where(incl, z[C:, :C], 0.0).astype(BF16) for z in p]
    p_rk = [jnp.where(incl, z[C:, C:], 0.0).astype(BF16) for z in p]
    col = lax.broadcasted_iota(I32, (C, 2 * C), 1)
    row2 = lax.broadcasted_iota(I32, (C, 2 * C), 0)
    left = col < C
    zt = [jnp.where(left, jnp.where(row2 > col, z[:C], 0.0), (col == row2 + C).astype(F32)) for z in p]
    for _ in range(int(math.log2(C))):
        zb = [z.astype(BF16) for z in zt]
        res = [_dot(z[:, :C], z) for z in zb]
        zt = [jnp.where(left, res[h], zt[h] + res[h]) for h in heads]
    tb = [z.astype(BF16) for z in zt]
    zeros_c = jnp.zeros((C, HEAD_DIM_B), BF16)
    lv = [_dot(l_ak[h], vb[h]).astype(BF16) for h in heads]
    a_bar = [_dot(tb[h], jnp.concatenate([zeros_c, a_hat_b[h]], axis=0)).astype(BF16) for h in heads]
    u_v = [_dot(tb[h], jnp.concatenate([zeros_c, lv[h]], axis=0)).astype(BF16) for h in heads]
    r_bar = [rh[h] + _dot(p_rb[h], a_bar[h]) for h in heads]
    y_v = [_dot(p_rb[h], u_v[h]) + _dot(p_rk[h], vb[h]) for h in heads]
    ab = [_dot_tn(a_bar[h], b_hat_b[h]).astype(BF16) for h in heads]
    n_t = [_dot_tn(jnp.concatenate([u_v[h], vb[h]], axis=0), bk[h]) for h in heads]
    s0 = [s_ref[b, h] for b, h in items]
    sr = [s0[h] * e_r[h] for h in heads]
    y = [_dot_nt((r_bar[h] * e_r[h]).astype(BF16), s0[h].astype(BF16)) + y_v[h] for h in heads]
    s_new = [(sr[h] + _dot(sr[h].astype(BF16), ab[h]) + n_t[h]) * e_c[h] for h in heads]
    for i, (b, h) in enumerate(items):
        s_ref[b, h] = s_new[i]
    y_full = jnp.concatenate([jnp.concatenate(y[b * N_HEADS_B:(b + 1) * N_HEADS_B], axis=1) for b in range(nb)],
                             axis=0)
    inv_hd = 1.0 / HEAD_DIM_B
    dev = y_full - head_sum(y_full) * inv_hd
    yn = dev * lax.rsqrt(head_sum(dev * dev) * inv_hd + LN_X_EPS) * lnw + lnb
    out = (yn + head_sum(r * k_h * rk) * v) * g
    for b in range(nb):
        o_ref[b] = out[b * C:(b + 1) * C, :]

    @pl.when(c == pl.num_programs(0) - 1)
    def _():
        st_ref[...] = s_ref[...]


def _wkv_call(feat, p):
    b, t, _ = feat.shape
    C = WKV_CHUNK
    nc = t // C
    row = lambda n: pl.BlockSpec((1, n), lambda c: (0, 0))
    mat = lambda m, n: pl.BlockSpec((m, n), lambda c: (0, 0))
    return pl.pallas_call(
        _wkv_body,
        out_shape=(jax.ShapeDtypeStruct((b, t, D_B), F32),
                   jax.ShapeDtypeStruct((b, N_HEADS_B, HEAD_DIM_B, HEAD_DIM_B), F32)),
        grid=(nc,),
        in_specs=[pl.BlockSpec((b, C, D_SHIFT_B), lambda c: (0, c, 0)),
                  pl.BlockSpec((b, 8, D_SHIFT_B), lambda c: (0, jnp.maximum(c * (C // 8) - 1, 0), 0)),
                  row(D_SHIFT_B), row(D_B), mat(DECAY_LORA, D_B), row(D_B), mat(AAA_LORA, D_B),
                  mat(GATE_LORA, D_B), row(D_B), row(D_B), row(D_B), row(D_B), row(D_B)],
        out_specs=(pl.BlockSpec((b, C, D_B), lambda c: (0, c, 0)),
                   pl.BlockSpec((b, N_HEADS_B, HEAD_DIM_B, HEAD_DIM_B), lambda c: (0, 0, 0, 0))),
        scratch_shapes=[pltpu.VMEM((b, N_HEADS_B, HEAD_DIM_B, HEAD_DIM_B), F32)],
        compiler_params=_cparams(("arbitrary",)),
        name="wkv",
    )(feat, feat, p['mu_b'], p['w0_b'], p['w_w2_b'], p['a0_b'], p['w_a2_b'], p['w_g2_b'],
      p['k_k_b'], p['k_a_b'], p['r_k_b'], p['ln_x_w_b'], p['ln_x_b_b'])


def _swkv_prep_body(f_ref, sh_ref, mu_ref, w0_ref, ww2_ref, a0_ref, wa2_ref, wg2_ref, kk_ref, ka_ref,
                    r_ref, w_ref, k_ref, v_ref, aa_ref, bb_ref, g_ref):
    f = f_ref[...]
    xs = f + mu_ref[...] * (sh_ref[...] - f)
    r, k, v, w_log, a, g, k_h = _rwkv_features(xs, w0_ref[...], ww2_ref[...], a0_ref[...],
                                               wa2_ref[...], wg2_ref[...], ka_ref[...])
    kk = k * kk_ref[...]
    kkn = jnp.concatenate([_head_norm(kk[:, h * HEAD_DIM_B:(h + 1) * HEAD_DIM_B]) for h in range(N_HEADS_B)],
                          axis=1)
    r_ref[...] = r
    w_ref[...] = jnp.exp(-jnp.exp(w_log))
    k_ref[...] = k_h
    v_ref[...] = v
    aa_ref[...] = -kkn
    bb_ref[...] = kkn * a
    g_ref[...] = g


def _swkv_prep_call(feat_s, shift0, p):
    n = feat_s.shape[0]
    full = lambda a: pl.BlockSpec(a.shape, lambda: tuple(0 for _ in a.shape))
    args = (feat_s, shift0, p['mu_b'], p['w0_b'], p['w_w2_b'], p['a0_b'], p['w_a2_b'], p['w_g2_b'],
            p['k_k_b'], p['k_a_b'])
    return pl.pallas_call(
        _swkv_prep_body,
        out_shape=tuple(jax.ShapeDtypeStruct((n, D_B), F32) for _ in range(7)),
        in_specs=[full(a) for a in args],
        out_specs=tuple(pl.BlockSpec((n, D_B), lambda: (0, 0)) for _ in range(7)),
        compiler_params=pltpu.CompilerParams(vmem_limit_bytes=VMEM_LIMIT),
        name="swkv_prep",
    )(*args)


def _swkv_step_body(s_ref, a_ref, w_ref, b_ref, k_ref, r_ref, v_ref, so_ref, y_ref):
    s = s_ref[...]
    th = s.shape[0]
    pad_sq = lambda z: jnp.concatenate(
        [jnp.concatenate([z, jnp.zeros((z.shape[0], LANES - z.shape[1]), F32)], axis=1),
         jnp.zeros((LANES - z.shape[0], LANES), F32)], axis=0)
    v_t = pad_sq(v_ref[...]).T
    v_col = jnp.stack([v_t[:HEAD_DIM_B, j:j + 1] for j in range(th)], axis=0)
    sa = jnp.sum(s * a_ref[...], axis=-1, keepdims=True)
    s2 = s * w_ref[...] + sa * b_ref[...] + v_col * k_ref[...]
    so_ref[...] = s2
    y = jnp.sum(s2 * r_ref[...], axis=-1, keepdims=True)
    y_t = jnp.concatenate([y[j] for j in range(th)], axis=1)
    y_ref[...] = pad_sq(y_t).T[:th, :HEAD_DIM_B]


def _swkv_step_call(s0, aa, w, bb, k, r, v):
    nh = s0.shape[0]
    th = 64
    rowspec = pl.BlockSpec((th, 1, HEAD_DIM_B), lambda i: (i, 0, 0))
    matspec = pl.BlockSpec((th, HEAD_DIM_B), lambda i: (i, 0))
    stspec = pl.BlockSpec((th, HEAD_DIM_B, HEAD_DIM_B), lambda i: (i, 0, 0))
    return pl.pallas_call(
        _swkv_step_body,
        out_shape=(jax.ShapeDtypeStruct((nh, HEAD_DIM_B, HEAD_DIM_B), F32),
                   jax.ShapeDtypeStruct((nh, HEAD_DIM_B), F32)),
        grid=(nh // th,),
        in_specs=[stspec, rowspec, rowspec, rowspec, rowspec, rowspec, matspec],
        out_specs=(stspec, matspec),
        compiler_params=_cparams(("arbitrary",)),
        name="swkv_step",
    )(s0, aa, w, bb, k, r, v)


def _swkv_fin_body(y_ref, r_ref, k_ref, v_ref, g_ref, rk_ref, lnw_ref, lnb_ref, o_ref):
    y, r, k, v, g = y_ref[...], r_ref[...], k_ref[...], v_ref[...], g_ref[...]
    rk, lnw, lnb = rk_ref[...], lnw_ref[...], lnb_ref[...]
    outs = []
    for h in range(N_HEADS_B):
        sl = slice(h * HEAD_DIM_B, (h + 1) * HEAD_DIM_B)
        outs.append(_wkv_finish_head(y[:, sl], r[:, sl], k[:, sl], v[:, sl], g[:, sl],
                                     rk[:, sl], lnw[:, sl], lnb[:, sl]))
    o_ref[...] = jnp.concatenate(outs, axis=1)


def _swkv_fin_call(y, r, k, v, g, p):
    n = y.shape[0]
    args = (y, r, k, v, g, p['r_k_b'], p['ln_x_w_b'], p['ln_x_b_b'])
    full = lambda a: pl.BlockSpec(a.shape, lambda: (0, 0))
    return pl.pallas_call(
        _swkv_fin_body,
        out_shape=jax.ShapeDtypeStruct((n, D_B), F32),
        in_specs=[full(a) for a in args],
        out_specs=pl.BlockSpec((n, D_B), lambda: (0, 0)),
        name="swkv_fin",
    )(*args)


def _route_t(scores, bias_col):
    n = scores.shape[1]
    gsz = N_EXPERTS // N_EXPERT_GROUPS
    choice = scores + bias_col
    ninf = -jnp.inf
    sid = lax.broadcasted_iota(I32, (gsz, n), 0)
    gs = []
    for gidx in range(N_EXPERT_GROUPS):
        blk = choice[gidx * gsz:(gidx + 1) * gsz, :]
        m1 = jnp.max(blk, axis=0, keepdims=True)
        first = jnp.min(jnp.where(blk == m1, sid, gsz), axis=0, keepdims=True)
        m2 = jnp.max(jnp.where(sid == first, ninf, blk), axis=0, keepdims=True)
        gs.append(m1 + m2)
    cur = jnp.concatenate(gs, axis=0)
    gid = lax.broadcasted_iota(I32, (N_EXPERT_GROUPS, n), 0)
    gmask = jnp.zeros((N_EXPERT_GROUPS, n), F32)
    for _ in range(TOPK_GROUPS):
        m = jnp.max(cur, axis=0, keepdims=True)
        first = jnp.min(jnp.where(cur == m, gid, N_EXPERT_GROUPS), axis=0, keepdims=True)
        sel = gid == first
        gmask = jnp.where(sel, 1.0, gmask)
        cur = jnp.where(sel, ninf, cur)
    emask = jnp.concatenate([jnp.broadcast_to(gmask[gidx:gidx + 1, :], (gsz, n))
                             for gidx in range(N_EXPERT_GROUPS)], axis=0)
    cur = jnp.where(emask > 0.5, choice, ninf)
    eid = lax.broadcasted_iota(I32, (N_EXPERTS, n), 0)
    selm = jnp.zeros((N_EXPERTS, n), F32)
    for _ in range(TOP_K):
        m = jnp.max(cur, axis=0, keepdims=True)
        first = jnp.min(jnp.where(cur == m, eid, N_EXPERTS), axis=0, keepdims=True)
        sel = eid == first
        selm = jnp.where(sel, 1.0, selm)
        cur = jnp.where(sel, ninf, cur)
    w = jnp.where(selm > 0.5, scores, 0.0)
    w = w / jnp.sum(w, axis=0, keepdims=True) * ROUTED_SCALE
    return jnp.where(selm > 0.5, w, -1.0)


def _unpermute(blk_ref, scr_ref, dil, tm):
    if dil == 1:
        return blk_ref[0, 0].astype(F32)
    n_chunks = scr_ref.shape[0]
    for r in range(dil):
        rows = blk_ref[0, r].astype(F32)
        for j in range(n_chunks):
            scr_ref[j, pl.ds(r, tm // dil, stride=dil), :] = rows[:, j * LANES:(j + 1) * LANES]
    return jnp.concatenate([scr_ref[j] for j in range(n_chunks)], axis=1)


def _post_body(*refs, combine, dils):
    if combine:
        o_refs, l_refs, rest = refs[:3], refs[3:6], refs[6:]
    else:
        o_refs, rest = refs[:1], refs[1:]
    (ob_ref, gt_ref, x_ref, g1_ref, sc2_ref, sh2_ref, npost_ref, npre_ref, wa_ref, wb_ref, wo_ref,
     wrt_ref, rb_ref, x1_ref, hp_ref, wt_ref) = rest[:16]
    scr = rest[16:]
    tm = x_ref.shape[1]
    if combine:
        os_, ls_ = [], []
        si = 0
        for gi, dil in enumerate(dils):
            os_.append(_unpermute(o_refs[gi], scr[si] if dil > 1 else None, dil, tm))
            ls_.append(_unpermute(l_refs[gi], scr[si + 1] if dil > 1 else None, dil, tm))
            si += 2 if dil > 1 else 0
        mx = jnp.maximum(jnp.maximum(ls_[0], ls_[1]), ls_[2])
        es = [jnp.exp(z - mx) for z in ls_]
        o_a = (es[0] * os_[0] + es[1] * os_[1] + es[2] * os_[2]) / (es[0] + es[1] + es[2])
    else:
        o_a = o_refs[0][0]
    gt = gt_ref[0].astype(F32)
    za = _dot(o_a.astype(BF16), wa_ref[...])
    zb = _dot(ob_ref[0].astype(BF16), wb_ref[...])
    merged = gt[:, :D_MODEL] * za + gt[:, D_MODEL:] * zb
    z = _dot(merged.astype(BF16), wo_ref[...])
    x1 = x_ref[0] + g1_ref[0] * _rms(z, npost_ref[...])
    x1_ref[0] = x1
    h2 = _rms(x1, npre_ref[...]) * (1.0 + sc2_ref[0]) + sh2_ref[0]
    packed = _pack_pairs(h2)
    for s in range(ROW_TILE_SUBLANES):
        hp_ref[0, pl.ds(s, tm, stride=ROW_TILE_SUBLANES), :] = packed[:, s * LANES:(s + 1) * LANES]
    tp =-(-tm // LANES) * LANES
    if tp != tm:
        h2 = jnp.concatenate([h2, jnp.zeros((tp - tm, D_MODEL), F32)], axis=0)
    logits_t = _dot_nt_split(wrt_ref[...], h2)
    w = _route_t(_sigmoid(logits_t[:N_EXPERTS, :]), rb_ref[...])
    wt_ref[...] = w[:, :tm]


def _post_call(o_parts, lse_parts, ob, gates, x, gate1, scale2, shift2, p, wa, wb, wo, wrt, rb, tm, mod_per_row):
    nb, t, _ = x.shape
    nt = t // tm
    combine = lse_parts is not None
    rowblk = lambda width: pl.BlockSpec((1, tm, width), lambda b, i: (b, i, 0))
    if mod_per_row:
        mod_spec = rowblk(D_MODEL)
    else:
        mod_spec = pl.BlockSpec((1, 1, D_MODEL), lambda b, i: (b, 0, 0))
    const = lambda shp: pl.BlockSpec(shp, lambda b, i: (0, 0))
    scratch = []
    if combine:
        dils = tuple(o.shape[1] for o in o_parts)
        o_args = list(o_parts) + list(lse_parts)
        o_specs = [pl.BlockSpec((1, d, tm // d, D_GROUP_A), lambda b, i: (b, 0, i, 0)) for d in dils] * 2
        for d in dils:
            if d > 1:
                scratch += [pltpu.VMEM((D_GROUP_A // LANES, tm, LANES), F32)] * 2
    else:
        dils = ()
        o_args = [o_parts[0]]
        o_specs = [rowblk(D_GROUP_A)]
    return pl.pallas_call(
        functools.partial(_post_body, combine=combine, dils=dils),
        out_shape=(jax.ShapeDtypeStruct((nb, t, D_MODEL), F32),
                   jax.ShapeDtypeStruct((nb, t * ROW_TILE_SUBLANES, LANES), I32),
                   jax.ShapeDtypeStruct((N_EXPERTS, nb * t), F32)),
        grid=(nb, nt),
        in_specs=o_specs + [rowblk(D_B), rowblk(2 * D_MODEL), rowblk(D_MODEL),
                            mod_spec, mod_spec, mod_spec, const((1, D_MODEL)), const((1, D_MODEL)),
                            const((D_GROUP_A, D_MODEL)), const((D_B, D_MODEL)), const((D_MODEL, D_MODEL)),
                            const((LANES, D_MODEL)), const((N_EXPERTS, 1))],
        out_specs=(rowblk(D_MODEL),
                   pl.BlockSpec((1, tm * ROW_TILE_SUBLANES, LANES), lambda b, i: (b, i, 0)),
                   pl.BlockSpec((N_EXPERTS, tm), lambda b, i: (0, b * nt + i))),
        scratch_shapes=scratch,
        compiler_params=_cparams(("arbitrary", "arbitrary")),
        name="post",
    )(*o_args, ob, gates, x, gate1, scale2, shift2, p['norm_post_mix'].reshape(1, -1),
      p['norm_pre_ffn'].reshape(1, -1), wa, wb, wo, wrt, rb)


def _rank_body(w_ref, dest_ref, w8_ref, tab_ref, etab_ref, cnt_ref, pst_ref, run_ref, *, n_real, n_slots):
    ph = pl.program_id(0)
    i = pl.program_id(1)
    T = MOE_TILE
    w = w_ref[...]
    sel = (w >= 0.0).astype(F32)
    cnt_tile = jnp.broadcast_to(jnp.sum(sel, axis=1, keepdims=True), (N_EXPERTS, LANES))
    ei = lax.broadcasted_iota(I32, (N_EXPERTS, N_EXPERTS), 0)
    ej = lax.broadcasted_iota(I32, (N_EXPERTS, N_EXPERTS), 1)

    @pl.when((ph == 0) & (i == 0))
    def _():
        cnt_ref[...] = jnp.zeros_like(cnt_ref)

    @pl.when(ph == 0)
    def _():
        cnt_ref[...] += cnt_tile

    @pl.when((ph == 1) & (i == 0))
    def _():
        cnt = cnt_ref[...]
        padded = jnp.floor((cnt + (EXPERT_BLOCK - 1)) / EXPERT_BLOCK) * EXPERT_BLOCK
        pstart = _dot_exact((ej < ei).astype(F32), padded)
        pst_ref[...] = pstart
        run_ref[...] = jnp.zeros_like(run_ref)
        pend = pstart + padded
        vend = pstart + cnt
        esub = lax.broadcasted_iota(I32, (N_EXPERTS, LANES), 0)
        lane = lax.broadcasted_iota(I32, (1, LANES), 1)
        tab_ref[...] = jnp.zeros_like(tab_ref)
        for c in range(tab_ref.shape[1] // LANES):
            bs = ((c * LANES + lane) * EXPERT_BLOCK).astype(F32)
            be = jnp.minimum(jnp.sum((pend <= bs).astype(F32), axis=0, keepdims=True), N_EXPERTS - 1.0)
            tab_ref[0:1, c * LANES:(c + 1) * LANES] = be.astype(I32)
            tab_ref[1:2, c * LANES:(c + 1) * LANES] = (pend[N_EXPERTS - 1:, :] / EXPERT_BLOCK).astype(I32)
        on_diag = esub == lax.broadcasted_iota(I32, (N_EXPERTS, LANES), 1)
        etab_ref[...] = jnp.zeros_like(etab_ref)
        lo = jnp.sum(jnp.where(on_diag, vend, 0.0), axis=0, keepdims=True)
        hi = jnp.sum(jnp.where(on_diag, pend, 0.0), axis=0, keepdims=True)
        etab_ref[0:1, :] = jnp.where(lane == N_EXPERTS, pend[N_EXPERTS - 1:, :], lo).astype(I32)
        etab_ref[1:2, :] = jnp.where(lane == N_EXPERTS, float(n_slots), hi).astype(I32)

    @pl.when(ph == 1)
    def _():
        ti = lax.broadcasted_iota(I32, (T, T), 0)
        tj = lax.broadcasted_iota(I32, (T, T), 1)
        selb = sel.astype(BF16)
        rank = _dot(selb, (ti < tj).astype(BF16))
        ordn = _dot((ej < ei).astype(BF16), selb)
        dest_e = pst_ref[:, :1] + run_ref[:, :1] + rank
        run_ref[...] += cnt_tile
        tok = i * T + lax.broadcasted_iota(I32, (1, T), 1)
        dks, wks = [], []
        for k in range(TOP_K):
            m = (sel > 0.5) & (ordn == float(k))
            dk = jnp.sum(jnp.where(m, dest_e, 0.0), axis=0, keepdims=True)
            wk = jnp.sum(jnp.where(m, w, 0.0), axis=0, keepdims=True)
            dks.append(jnp.where(tok < n_real, dk, 0.0))
            wks.append(jnp.where(tok < n_real, wk, 0.0))
        dest_ref[...] = jnp.concatenate(dks, axis=0).astype(I32)
        w8_ref[...] = jnp.concatenate(wks, axis=0)


def _rank_call(w_t, n_real, n_blocks, n_blocks_pad):
    n = w_t.shape[1]
    nt = n // MOE_TILE
    return pl.pallas_call(
        functools.partial(_rank_body, n_real=n_real, n_slots=n_blocks * EXPERT_BLOCK),
        out_shape=(jax.ShapeDtypeStruct((TOP_K, n), I32),
                   jax.ShapeDtypeStruct((TOP_K, n), F32),
                   jax.ShapeDtypeStruct((8, n_blocks_pad), I32),
                   jax.ShapeDtypeStruct((8, LANES), I32)),
        grid=(2, nt),
        in_specs=[pl.BlockSpec((N_EXPERTS, MOE_TILE), lambda ph, i: (0, i))],
        out_specs=(pl.BlockSpec((TOP_K, MOE_TILE), lambda ph, i: (0, i * ph)),
                   pl.BlockSpec((TOP_K, MOE_TILE), lambda ph, i: (0, i * ph)),
                   pl.BlockSpec((8, n_blocks_pad), lambda ph, i: (0, 0)),
                   pl.BlockSpec((8, LANES), lambda ph, i: (0, 0))),
        scratch_shapes=[pltpu.VMEM((N_EXPERTS, LANES), F32)] * 3,
        compiler_params=_cparams(("arbitrary", "arbitrary")),
        name="rank",
    )(w_t)


def _tile_rows(ref, row, n):
    return ref.at[pl.ds(pl.multiple_of(row * ROW_TILE_SUBLANES, ROW_TILE_SUBLANES), n * ROW_TILE_SUBLANES)]


def _zero_fill(etab_ref, zbuf, xs_hbm, zsem, wait):
    def go(src, dst):
        cp = pltpu.make_async_copy(src, dst, zsem)
        if wait:
            cp.wait()
        else:
            cp.start()

    def per_range(e, carry):
        lo = etab_ref[0, e]
        n = etab_ref[1, e] - lo
        n_full = n // ZERO_ROWS

        def full(j, c):
            go(zbuf, _tile_rows(xs_hbm, lo + j * ZERO_ROWS, ZERO_ROWS))
            return c

        lax.fori_loop(0, n_full, full, 0)
        pos = lo + n_full * ZERO_ROWS
        rem = n - n_full * ZERO_ROWS
        size = ZERO_ROWS // 2
        while size >= 1:
            bit = rem & size

            @pl.when(bit != 0)
            def _(size=size, pos=pos):
                go(_tile_rows(zbuf, 0, size), _tile_rows(xs_hbm, pos, size))

            pos = pos + bit
            size //= 2
        return carry

    lax.fori_loop(0, N_EXPERTS + 1, per_range, 0)


def _dispatch_body(dest_ref, etab_ref, xa_ref, xb_ref, xs_hbm, zbuf, sem, zsem, *, n_real, n_full):
    i = pl.program_id(0)
    T = MOE_TILE
    n_tok = jnp.clip(n_real - i * T, 0, T)

    def issue_from(x_ref):
        def issue(t, carry):
            for k in range(TOP_K):
                pltpu.make_async_copy(_tile_rows(x_ref, t, 1), _tile_rows(xs_hbm, dest_ref[k * T + t], 1),
                                      sem).start(priority=k % 2)
            return carry

        lax.fori_loop(0, n_tok, issue, 0)

    @pl.when(i < n_full)
    def _():
        issue_from(xa_ref)

    @pl.when(i >= n_full)
    def _():
        issue_from(xb_ref)

    @pl.when(i == 0)
    def _():
        zbuf[...] = jnp.zeros_like(zbuf)
        _zero_fill(etab_ref, zbuf, xs_hbm, zsem, wait=False)
        _zero_fill(etab_ref, zbuf, xs_hbm, zsem, wait=True)

    @pl.when(n_tok == T)
    def _():
        pltpu.make_async_copy(_tile_rows(xs_hbm, 0, T * TOP_K), _tile_rows(xs_hbm, 0, T * TOP_K), sem).wait()

    @pl.when(n_tok < T)
    def _():
        def drain(j, carry):
            pltpu.make_async_copy(_tile_rows(xs_hbm, 0, 1), _tile_rows(xs_hbm, 0, 1), sem).wait()
            return carry

        lax.fori_loop(0, n_tok * TOP_K, drain, 0)


def _dispatch_call(dest, etab, hp_a, hp_b, n_real, n_slots):
    tile_rows = MOE_TILE * ROW_TILE_SUBLANES
    n_full = hp_a.shape[0] // tile_rows
    return pl.pallas_call(
        functools.partial(_dispatch_body, n_real=n_real, n_full=n_full),
        out_shape=jax.ShapeDtypeStruct((n_slots * ROW_TILE_SUBLANES, LANES), I32),
        grid=(n_full + 1,),
        in_specs=[pl.BlockSpec((TOP_K * MOE_TILE,), lambda i: (i,), memory_space=pltpu.SMEM),
                  pl.BlockSpec((8, LANES), lambda i: (0, 0), memory_space=pltpu.SMEM),
                  pl.BlockSpec((tile_rows, LANES), lambda i: (jnp.minimum(i, n_full - 1), 0)),
                  pl.BlockSpec((tile_rows, LANES), lambda i: (0, 0))],
        out_specs=pl.BlockSpec(memory_space=pl.ANY),
        scratch_shapes=[pltpu.VMEM((ZERO_ROWS * ROW_TILE_SUBLANES, LANES), I32),
                        pltpu.SemaphoreType.DMA, pltpu.SemaphoreType.DMA],
        compiler_params=_cparams(("arbitrary",)),
        name="dispatch",
    )(dest, etab, hp_a, hp_b)


def _rows_from_tiles(ref, lo, n):
    return jnp.concatenate([ref[pl.ds(lo * ROW_TILE_SUBLANES + s, n, stride=ROW_TILE_SUBLANES), :]
                            for s in range(ROW_TILE_SUBLANES)], axis=1)


def _ffn_body(be_ref, nu_ref, xs_ref, wg_ref, wu_ref, wd_ref, ys_ref, wgb, wub, wdb):
    j = pl.program_id(0)

    @pl.when(j < nu_ref[0])
    def _():
        @pl.when((j == 0) | (be_ref[j] != be_ref[jnp.maximum(j - 1, 0)]))
        def _():
            wgb[...] = wg_ref[0].astype(BF16)
            wub[...] = wu_ref[0].astype(BF16)
            wdb[...] = wd_ref[0].astype(BF16)

        x = _unpack_pairs(_rows_from_tiles(xs_ref, 0, EXPERT_BLOCK)).astype(BF16)
        act = _silu(_dot(x, wgb[...])) * _dot(x, wub[...])
        y = _dot(act.astype(BF16), wdb[...])
        packed = _pack_pairs(y)
        for s in range(ROW_TILE_SUBLANES):
            ys_ref[pl.ds(s, EXPERT_BLOCK, stride=ROW_TILE_SUBLANES), :] = packed[:, s * LANES:(s + 1) * LANES]

    @pl.when(j >= nu_ref[0])
    def _():
        ys_ref[...] = jnp.zeros_like(ys_ref)


def _ffn_call(blk_e, n_used, xs, w_gate, w_up, w_down, n_blocks):
    tile_blk = pl.BlockSpec((EXPERT_BLOCK * ROW_TILE_SUBLANES, LANES), lambda j, be, nu: (j, 0))
    last = lambda j, nu: jnp.minimum(j, nu[0] - 1)
    grid_spec = pltpu.PrefetchScalarGridSpec(
        num_scalar_prefetch=2,
        grid=(n_blocks,),
        in_specs=[pl.BlockSpec((EXPERT_BLOCK * ROW_TILE_SUBLANES, LANES), lambda j, be, nu: (last(j, nu), 0)),
                  pl.BlockSpec((1, D_MODEL, D_EXPERT), lambda j, be, nu: (be[last(j, nu)], 0, 0)),
                  pl.BlockSpec((1, D_MODEL, D_EXPERT), lambda j, be, nu: (be[last(j, nu)], 0, 0)),
                  pl.BlockSpec((1, D_EXPERT, D_MODEL), lambda j, be, nu: (be[last(j, nu)], 0, 0))],
        out_specs=tile_blk,
        scratch_shapes=[pltpu.VMEM((D_MODEL, D_EXPERT), BF16), pltpu.VMEM((D_MODEL, D_EXPERT), BF16),
                        pltpu.VMEM((D_EXPERT, D_MODEL), BF16)])
    return pl.pallas_call(
        _ffn_body,
        out_shape=jax.ShapeDtypeStruct((n_blocks * EXPERT_BLOCK * ROW_TILE_SUBLANES, LANES), I32),
        grid_spec=grid_spec,
        compiler_params=_cparams(("arbitrary",)),
        name="ffn",
    )(blk_e, n_used, xs, w_gate, w_up, w_down)


def _combine_body(dest_ref, dnext_ref, w8_ref, xa_ref, xb_ref, x1a_ref, x1b_ref, g2a_ref, g2b_ref, gain_ref,
                  sg_ref, su_ref, sd_ref, ys_hbm, oa_ref, ob_ref, buf, sem):
    j = pl.program_id(0)
    T = COMBINE_TILE
    RC = COMBINE_ROWS

    def issue(d_ref, slot, t):
        for k in range(TOP_K):
            pltpu.make_async_copy(_tile_rows(ys_hbm, d_ref[k * T + t], 1), _tile_rows(buf.at[slot], k * T + t, 1),
                                  sem.at[slot]).start(priority=k % 2)

    def wait(slot):
        pltpu.make_async_copy(_tile_rows(ys_hbm, 0, T * TOP_K), buf.at[slot], sem.at[slot]).wait()

    def step(slot):
        is_tail = j == 0
        wait(slot)
        for t in range(RC):
            issue(dnext_ref, 1 - slot, t)
        x = _unpack_pairs(jnp.where(is_tail, _rows_from_tiles(xb_ref, 0, T),
                                    _rows_from_tiles(xa_ref, 0, T))).astype(BF16)
        shared = _dot((_silu(_dot(x, sg_ref[...])) * _dot(x, su_ref[...])).astype(BF16), sd_ref[...])
        w_t = jnp.concatenate([w8_ref[...], jnp.zeros((LANES - TOP_K, T), F32)], axis=0).T
        oa_ref[...] = shared
        for r0 in range(0, T, RC):
            if r0 > 0:
                for t in range(r0, r0 + RC):
                    issue(dnext_ref, 1 - slot, t)
            acc = oa_ref[r0:r0 + RC, :]
            for k in range(TOP_K):
                acc = acc + w_t[r0:r0 + RC, k:k + 1] * _unpack_pairs(_rows_from_tiles(buf.at[slot], k * T + r0, RC))
            x1 = jnp.where(is_tail, x1b_ref[r0:r0 + RC, :], x1a_ref[r0:r0 + RC, :])
            g2 = jnp.where(is_tail, g2b_ref[r0:r0 + RC, :], g2a_ref[0])
            oa_ref[r0:r0 + RC, :] = x1 + g2 * _rms(acc, gain_ref[...])

        @pl.when(is_tail)
        def _():
            ob_ref[...] = oa_ref[...]

        @pl.when(j + 1 == pl.num_programs(0))
        def _():
            wait(1 - slot)

    @pl.when(j == 0)
    def _():
        lax.fori_loop(0, T, lambda t, c: (issue(dest_ref, 0, t), c)[1], 0, unroll=2)

    @pl.when(j % 2 == 0)
    def _():
        step(0)

    @pl.when(j % 2 == 1)
    def _():
        step(1)


def _combine_call(dest, w8, hp_a, hp_b, x1_a, x1_b, gate2_a, gate2_b, gain, wsg, wsu, wsd, ys):
    T = COMBINE_TILE
    tile_rows = T * ROW_TILE_SUBLANES
    n_full = hp_a.shape[0] // tile_rows
    n_tiles = n_full + 1
    seq = x1_a.shape[0] // gate2_a.shape[0]
    tile_of = lambda j: jnp.where(j == 0, n_full, j - 1)
    full_of = lambda j: jnp.maximum(j - 1, 0)
    const = lambda shp: pl.BlockSpec(shp, lambda j: (0, 0))
    return pl.pallas_call(
        _combine_body,
        out_shape=(jax.ShapeDtypeStruct((n_full * T, D_MODEL), F32), jax.ShapeDtypeStruct((T, D_MODEL), F32)),
        grid=(n_tiles,),
        in_specs=[pl.BlockSpec((TOP_K * T,), lambda j: (tile_of(j),), memory_space=pltpu.SMEM),
                  pl.BlockSpec((TOP_K * T,), lambda j: (tile_of(jnp.minimum(j + 1, n_tiles - 1)),),
                               memory_space=pltpu.SMEM),
                  pl.BlockSpec((TOP_K, T), lambda j: (0, tile_of(j))),
                  pl.BlockSpec((tile_rows, LANES), lambda j: (full_of(j), 0)),
                  pl.BlockSpec((tile_rows, LANES), lambda j: (0, 0)),
                  pl.BlockSpec((T, D_MODEL), lambda j: (full_of(j), 0)),
                  const((T, D_MODEL)),
                  pl.BlockSpec((1, 1, D_MODEL), lambda j: (full_of(j) * T // seq, 0, 0)),
                  const((T, D_MODEL)), const((1, D_MODEL)),
                  const((D_MODEL, D_EXPERT)), const((D_MODEL, D_EXPERT)), const((D_EXPERT, D_MODEL)),
                  pl.BlockSpec(memory_space=pl.ANY)],
        out_specs=(pl.BlockSpec((T, D_MODEL), lambda j: (full_of(j), 0)), const((T, D_MODEL))),
        scratch_shapes=[pltpu.VMEM((2, TOP_K * tile_rows, LANES), I32), pltpu.SemaphoreType.DMA((2,))],
        compiler_params=_cparams(("arbitrary",)),
        name="combine",
    )(dest, dest, w8, hp_a, hp_b, x1_a, x1_b, gate2_a, gate2_b, gain.reshape(1, -1), wsg, wsu, wsd, ys)


def _rope_tables(pos):
    half = HEAD_DIM_A // 2
    inv_freq = ROPE_THETA ** (-jnp.arange(half, dtype=F32) / half)
    ang = pos.astype(F32)[:, None] * inv_freq[None, :]
    cos = jnp.cos(ang)
    sin = jnp.sin(ang)
    reps = LANES // HEAD_DIM_A
    cos_t = jnp.tile(jnp.concatenate([cos, cos], axis=1), (1, reps))
    sin_t = jnp.tile(jnp.concatenate([-sin, sin], axis=1), (1, reps))
    return cos_t, sin_t


def kernel(x_prompt, x_sample, c_prompt, c_sample, cache_a1_kv, cache_a2_kv, cache_a3_kv, state_b_wkv, state_b_shift, w_ada, b_ada, norm_pre_mix, norm_post_mix, norm_pre_ffn, norm_post_ffn, w_in, w_a_out, mu_b, w0_b, w_w2_b, a0_b, w_a2_b, w_g2_b, k_k_b, k_a_b, r_k_b, ln_x_w_b, ln_x_b_b, w_b_out, w_out, w_router, router_bias, w_e_gate, w_e_up, w_e_down, w_s_gate, w_s_up, w_s_down):
    assert DEPTH == 1
    l = 0
    nd = DEC_BATCH
    row = lambda a: a.reshape(1, -1)
    p = {'mu_b': row(mu_b[l]), 'w0_b': row(w0_b[l]), 'w_w2_b': w_w2_b[l], 'a0_b': row(a0_b[l]),
         'w_a2_b': w_a2_b[l], 'w_g2_b': w_g2_b[l], 'k_k_b': row(k_k_b[l]), 'k_a_b': row(k_a_b[l]),
         'r_k_b': row(r_k_b[l]), 'ln_x_w_b': row(ln_x_w_b[l]), 'ln_x_b_b': row(ln_x_b_b[l]),
         'norm_post_mix': norm_post_mix[l], 'norm_pre_ffn': norm_pre_ffn[l]}

    wq, wf, wg = _wsplit_call(w_in[l])
    wa = w_a_out[l].astype(BF16)
    wb = w_b_out[l].astype(BF16)
    wo = w_out[l].astype(BF16)
    wrt = jnp.concatenate([w_router[l].T, jnp.zeros((LANES - N_EXPERTS, D_MODEL), F32)], axis=0)
    rb = router_bias[l].reshape(N_EXPERTS, 1)
    wsg, wsu, wsd = w_s_gate[l].astype(BF16), w_s_up[l].astype(BF16), w_s_down[l].astype(BF16)

    n_c = BATCH + nd
    c_all = jnp.concatenate([c_prompt, c_sample, jnp.zeros((-n_c % 8, D_MODEL), F32)], axis=0)
    mod = _mod_call(c_all, w_ada[l], b_ada[l])
    mod_p = [m.reshape(BATCH, 1, D_MODEL) for m in jnp.split(mod[:BATCH], 6, axis=-1)]
    mod_s = [m.reshape(1, nd, D_MODEL) for m in jnp.split(mod[BATCH:n_c], 6, axis=-1)]

    cos_p, sin_p = _rope_tables(jnp.arange(SEQ, dtype=I32))
    cos_s, sin_s = _rope_tables(jnp.full((nd,), PAST_LEN, I32))

    keep_p = [min(w, SEQ) for w, _ in DILATED_GROUPS]
    dils = tuple(d for _, d in DILATED_GROUPS)

    q0, q1, q2, feat_p, gates_p, *tails_p = _inproj_call(
        x_prompt, norm_pre_mix[l], mod_p[1], mod_p[0], cos_p, sin_p, wq, wf, wg,
        tm=256, keeps=keep_p, mod_per_row=False, dils=dils)
    o_parts, lse_parts = [], []
    for gi, qg in enumerate((q0, q1, q2)):
        o, lse = _attn_call(qg, gi)
        o_parts.append(o)
        lse_parts.append(lse)
    ob_p, wkv_p = _wkv_call(feat_p, p)
    x1_p, hp_p, wt_p = _post_call(o_parts, lse_parts, ob_p, gates_p, x_prompt, mod_p[2], mod_p[4], mod_p[3],
                                  p, wa, wb, wo, wrt, rb, tm=512, mod_per_row=False)

    xs3 = x_sample.reshape(1, nd, D_MODEL)
    s0, s1, s2, feat_s, gates_s, *tails_s = _inproj_call(
        xs3, norm_pre_mix[l], mod_s[1], mod_s[0], cos_s, sin_s, wq, wf, wg,
        tm=nd, keeps=(nd,) * N_GROUPS_A, mod_per_row=True, dils=(1, 1, 1))
    qkv_s = jnp.stack([z.reshape(nd, 3, N_HEADS_A, HEAD_DIM_A) for z in (s0, s1, s2)], axis=2)
    qkv_s = qkv_s.reshape(nd, 3 * N_GROUPS_A * N_HEADS_A, HEAD_DIM_A).astype(F32)
    oa_s = _sattn_call(qkv_s, cache_a1_kv[l], cache_a2_kv[l], cache_a3_kv[l])
    r_s, w_s, k_s, v_s, aa_s, bb_s, g_s = _swkv_prep_call(feat_s[0], state_b_shift[l], p)
    nh = nd * N_HEADS_B
    as_row = lambda a: a.reshape(nh, 1, HEAD_DIM_B)
    s_new, y_col = _swkv_step_call(state_b_wkv[l].reshape(nh, HEAD_DIM_B, HEAD_DIM_B), as_row(aa_s), as_row(w_s),
                                   as_row(bb_s), as_row(k_s), as_row(r_s), v_s.reshape(nh, HEAD_DIM_B))
    ob_s = _swkv_fin_call(y_col.reshape(nd, D_B), r_s, k_s, v_s, g_s, p)
    x1_s, hp_s, wt_s = _post_call([oa_s.reshape(1, nd, D_GROUP_A)], None, ob_s.reshape(1, nd, D_B), gates_s, xs3,
                                  mod_s[2], mod_s[4], mod_s[3], p, wa, wb, wo, wrt, rb, tm=nd, mod_per_row=True)

    n_p = BATCH * SEQ
    n_real = n_p + nd
    n_all = -(-n_real // MOE_TILE) * MOE_TILE
    pad = n_all - n_real
    n_blocks = -(-(n_real * TOP_K) // EXPERT_BLOCK) + N_EXPERTS
    n_blocks_pad = -(-n_blocks // LANES) * LANES
    assert n_p % MOE_TILE == 0 and nd <= MOE_TILE
    hp_a = hp_p.reshape(n_p * ROW_TILE_SUBLANES, LANES)
    hp_b = jnp.concatenate([hp_s[0], jnp.zeros((pad * ROW_TILE_SUBLANES, LANES), I32)], axis=0)
    wt_all = jnp.concatenate([wt_p, wt_s, jnp.full((N_EXPERTS, pad), -1.0, F32)], axis=1)
    dest8, w8, tab, etab = _rank_call(wt_all, n_real, n_blocks, n_blocks_pad)
    dest = dest8.reshape(TOP_K, n_all // MOE_TILE, MOE_TILE).transpose(1, 0, 2).reshape(-1)
    xs = _dispatch_call(dest, etab, hp_a, hp_b, n_real, n_blocks * EXPERT_BLOCK)
    ys = _ffn_call(tab[0], tab[1, :1], xs, w_e_gate[l], w_e_up[l], w_e_down[l], n_blocks)
    n_ct = n_p // COMBINE_TILE + 1
    dest_c = dest8[:, :n_ct * COMBINE_TILE].reshape(TOP_K, n_ct, COMBINE_TILE).transpose(1, 0, 2).reshape(-1)
    pad_rows = lambda z: jnp.concatenate([z, jnp.zeros((COMBINE_TILE - nd, D_MODEL), F32)], axis=0)
    out_p, out_s = _combine_call(dest_c, w8, hp_a, hp_b, x1_p.reshape(n_p, D_MODEL), pad_rows(x1_s[0]), mod_p[5],
                                 pad_rows(mod_s[5][0]), norm_post_ffn[l], wsg, wsu, wsd, ys)
    y_prompt = out_p.reshape(BATCH, SEQ, D_MODEL)
    y_sample = out_s[:nd]

    a_p = [z.reshape(1, BATCH, kp, 2, N_HEADS_A, HEAD_DIM_A) for z, kp in zip(tails_p, keep_p)]
    a_s = [z.reshape(1, nd, DEC_SEQ, 2, N_HEADS_A, HEAD_DIM_A) for z in tails_s]
    shift_p = feat_p[:, -1][None]
    shift_s = feat_s[0][None]
    return (y_prompt, y_sample.reshape(nd, DEC_SEQ, D_MODEL), a_p[0], a_p[1], a_p[2], wkv_p[None], shift_p,
            a_s[0], a_s[1], a_s[2], s_new.reshape(1, nd, N_HEADS_B, HEAD_DIM_B, HEAD_DIM_B), shift_s)
```

```python
import functools
import math

import jax
import jax.numpy as jnp
from jax import lax
from jax.experimental import pallas as pl
from jax.experimental.pallas import tpu as pltpu

F32 = jnp.float32
BF16 = jnp.bfloat16
I32 = jnp.int32

D_MODEL = 1024
BATCH = 2
SEQ = 8192
DEPTH = 1
DEC_BATCH = 32
DEC_SEQ = 1
PAST_LEN = 16384

HEAD_DIM_A = 64
N_HEADS_A = 8
DILATED_GROUPS = ((128, 1), (512, 4), (2048, 16))
N_GROUPS_A = 3
D_GROUP_A = N_HEADS_A * HEAD_DIM_A
D_A = N_GROUPS_A * D_GROUP_A
D_QKV = 3 * D_A
BAND_BLOCK = 128
ROPE_THETA = 10000.0

HEAD_DIM_B = 64
N_HEADS_B = 16
D_B = 1024
DECAY_LORA = 64
AAA_LORA = 64
GATE_LORA = 160
D_SHIFT_B = 3 * D_B + DECAY_LORA + AAA_LORA + GATE_LORA
LN_X_EPS = 64e-5

N_EXPERTS = 64
TOP_K = 8
N_EXPERT_GROUPS = 8
TOPK_GROUPS = 4
D_EXPERT = 256
ROUTED_SCALE = 2.5
EXPERT_BLOCK = 512
NORM_EPS = 1e-6

LANES = 128
WKV_CHUNK = 64
MOE_TILE = 1024
COMBINE_TILE = 256
COMBINE_ROWS = 32
VMEM_LIMIT = 56 * 1024 * 1024
ROW_TILE_SUBLANES = D_MODEL // (2 * LANES)
ROW_PITCH = ROW_TILE_SUBLANES + 1
ZERO_ROWS = 256


def _cparams(sem):
    return pltpu.CompilerParams(dimension_semantics=sem, vmem_limit_bytes=VMEM_LIMIT)


def _dot(a, b):
    return jnp.dot(a, b, preferred_element_type=F32)


def _dot_nt(a, b):
    return lax.dot_general(a, b, (((1,), (1,)), ((), ())), preferred_element_type=F32)


def _dot_tn(a, b):
    return lax.dot_general(a, b, (((0,), (0,)), ((), ())), preferred_element_type=F32)


def _dot_nt_split(a, b):
    ah = a.astype(BF16)
    al = (a - ah.astype(F32)).astype(BF16)
    bh = b.astype(BF16)
    bl = (b - bh.astype(F32)).astype(BF16)
    return _dot_nt(ah, bh) + _dot_nt(ah, bl) + _dot_nt(al, bh)


def _dot_exact(a, b):
    return lax.dot_general(a, b, (((1,), (0,)), ((), ())), precision=lax.Precision.HIGHEST,
                           preferred_element_type=F32)


def _rms(x, gain):
    return x * lax.rsqrt(jnp.mean(x * x, axis=-1, keepdims=True) + NORM_EPS) * gain


def _sigmoid(x):
    return 1.0 / (1.0 + jnp.exp(-x))


def _silu(x):
    return x * _sigmoid(x)


def _softplus(x):
    return jnp.maximum(x, 0.0) + jnp.log(1.0 + jnp.exp(-jnp.abs(x)))


def _pack_pairs(x):
    half = D_MODEL // 2
    lo = lax.bitcast_convert_type(x[:, :half].astype(BF16).astype(F32), I32)
    hi = lax.bitcast_convert_type(x[:, half:].astype(BF16).astype(F32), I32)
    return lax.shift_right_logical(lo, 16) | (hi & jnp.int32(-65536))


def _unpack_pairs(w):
    lo = lax.bitcast_convert_type(w << 16, F32)
    hi = lax.bitcast_convert_type(w & jnp.int32(-65536), F32)
    return jnp.concatenate([lo, hi], axis=1)


def _mod_body(c_ref, w_ref, b_ref, o_ref):
    s = _silu(c_ref[...]).astype(BF16)
    o_ref[...] = _dot(s, w_ref[...].astype(BF16)) + b_ref[...]


def _mod_call(c_all, w_ada, b_ada):
    rows = c_all.shape[0]
    tn = 1536
    return pl.pallas_call(
        _mod_body,
        out_shape=jax.ShapeDtypeStruct((rows, 6 * D_MODEL), F32),
        grid=(6 * D_MODEL // tn,),
        in_specs=[pl.BlockSpec((rows, D_MODEL), lambda j: (0, 0)),
                  pl.BlockSpec((D_MODEL, tn), lambda j: (0, j)),
                  pl.BlockSpec((1, tn), lambda j: (0, j))],
        out_specs=pl.BlockSpec((rows, tn), lambda j: (0, j)),
        compiler_params=_cparams(("arbitrary",)),
        name="mod",
    )(c_all, w_ada, b_ada.reshape(1, -1))


def _wsplit_body(w_ref, q_ref, f_ref, g_ref):
    w = w_ref[...]
    q_ref[...] = w[:, :D_QKV].astype(BF16)
    f_ref[...] = w[:, D_QKV:D_QKV + D_SHIFT_B].astype(BF16)
    g_ref[...] = w[:, D_QKV + D_SHIFT_B:].astype(BF16)


def _wsplit_call(w):
    rows, cols = w.shape
    tr = 128
    widths = (D_QKV, D_SHIFT_B, cols - D_QKV - D_SHIFT_B)
    return pl.pallas_call(
        _wsplit_body,
        out_shape=tuple(jax.ShapeDtypeStruct((rows, n), BF16) for n in widths),
        grid=(rows // tr,),
        in_specs=[pl.BlockSpec((tr, cols), lambda i: (i, 0))],
        out_specs=tuple(pl.BlockSpec((tr, n), lambda i: (i, 0)) for n in widths),
        compiler_params=_cparams(("arbitrary",)),
        name="wsplit",
    )(w)


def _inproj_body(x_ref, g_ref, sc_ref, sh_ref, cos_ref, sin_ref, wq_ref, wf_ref, wg_ref,
                 q0_ref, q1_ref, q2_ref, feat_ref, gate_ref, t0_ref, t1_ref, t2_ref, p_ref, *, dils):
    x = x_ref[0]
    tm = x.shape[0]
    h = _rms(x, g_ref[...]) * (1.0 + sc_ref[0]) + sh_ref[0]
    hb = h.astype(BF16)
    p = _dot(hb, wq_ref[...])
    cos = cos_ref[...]
    sin = sin_ref[...]
    lane = lax.broadcasted_iota(I32, cos.shape, 1)
    first_half = (lane % HEAD_DIM_A) < (HEAD_DIM_A // 2)
    for c in range(2 * D_A // LANES):
        xc = p[:, c * LANES:(c + 1) * LANES]
        partner = jnp.where(first_half, pltpu.roll(xc, LANES - HEAD_DIM_A // 2, 1),
                            pltpu.roll(xc, HEAD_DIM_A // 2, 1))
        rc = xc * cos + partner * sin
        if c < D_A // LANES:
            rc = rc * (HEAD_DIM_A ** -0.5)
        p_ref[c] = rc
    for c in range(2 * D_A // LANES, D_QKV // LANES):
        p_ref[c] = p[:, c * LANES:(c + 1) * LANES]
    per_group = D_GROUP_A // LANES
    for gi, t_ref in enumerate((t0_ref, t1_ref, t2_ref)):
        rows = t_ref.shape[1]
        for which in (1, 2):
            for j in range(per_group):
                c = (which * D_A + gi * D_GROUP_A) // LANES + j
                t_ref[0, :, (which - 1) * D_GROUP_A + j * LANES:(which - 1) * D_GROUP_A + (j + 1) * LANES] = \
                    p_ref[c, tm - rows:tm, :]
    for gi, (out_ref, dil) in enumerate(zip((q0_ref, q1_ref, q2_ref), dils)):
        for which in range(3):
            for j in range(per_group):
                c = (which * D_A + gi * D_GROUP_A) // LANES + j
                dst = slice(which * D_GROUP_A + j * LANES, which * D_GROUP_A + (j + 1) * LANES)
                if dil == 1:
                    out_ref[0, 0, :, dst] = p_ref[c].astype(BF16)
                else:
                    for r in range(dil):
                        out_ref[0, r, :, dst] = p_ref[c, pl.ds(r, tm // dil, stride=dil), :].astype(BF16)
    feat_ref[0] = _dot(hb, wf_ref[...])
    gate_ref[0] = _sigmoid(_dot(hb, wg_ref[...])).astype(BF16)


def _inproj_call(x, gain, scale, shift, cos_t, sin_t, wq, wf, wg, tm, keeps, mod_per_row, dils):
    nb, t, _ = x.shape
    nt = t // tm

    def tail_spec(keep):
        if keep <= tm:
            return pl.BlockSpec((1, keep, 2 * D_GROUP_A), lambda b, i: (b, 0, 0))
        first = (t - keep) // tm
        return pl.BlockSpec((1, tm, 2 * D_GROUP_A), lambda b, i: (b, jnp.maximum(i - first, 0), 0))

    if mod_per_row:
        mod_spec = pl.BlockSpec((1, tm, D_MODEL), lambda b, i: (b, i, 0))
    else:
        mod_spec = pl.BlockSpec((1, 1, D_MODEL), lambda b, i: (b, 0, 0))
    resident = lambda shp: pl.BlockSpec(shp, lambda b, i: (0, 0), pipeline_mode=pl.Buffered(1))
    q_shapes = tuple(jax.ShapeDtypeStruct((nb, d, t // d, 3 * D_GROUP_A), BF16) for d in dils)
    q_specs = tuple(pl.BlockSpec((1, d, tm // d, 3 * D_GROUP_A), lambda b, i: (b, 0, i, 0)) for d in dils)
    return pl.pallas_call(
        functools.partial(_inproj_body, dils=dils),
        out_shape=q_shapes + (jax.ShapeDtypeStruct((nb, t, D_SHIFT_B), F32),
                              jax.ShapeDtypeStruct((nb, t, 2 * D_MODEL), BF16),
                              ) + tuple(jax.ShapeDtypeStruct((nb, kp, 2 * D_GROUP_A), F32) for kp in keeps),
        grid=(nb, nt),
        in_specs=[pl.BlockSpec((1, tm, D_MODEL), lambda b, i: (b, i, 0)),
                  pl.BlockSpec((1, D_MODEL), lambda b, i: (0, 0)),
                  mod_spec, mod_spec,
                  pl.BlockSpec((tm, LANES), lambda b, i: (i, 0)),
                  pl.BlockSpec((tm, LANES), lambda b, i: (i, 0)),
                  resident((D_MODEL, D_QKV)), resident((D_MODEL, D_SHIFT_B)),
                  resident((D_MODEL, 2 * D_MODEL))],
        out_specs=q_specs + (pl.BlockSpec((1, tm, D_SHIFT_B), lambda b, i: (b, i, 0)),
                             pl.BlockSpec((1, tm, 2 * D_MODEL), lambda b, i: (b, i, 0)),
                             ) + tuple(tail_spec(kp) for kp in keeps),
        scratch_shapes=[pltpu.VMEM((D_QKV // LANES, tm, LANES), F32)],
        compiler_params=_cparams(("arbitrary", "arbitrary")),
        name="inproj",
    )(x, gain.reshape(1, -1), scale, shift, cos_t, sin_t, wq, wf, wg)


def _attn_body(q_ref, kc_ref, kp_ref, vc_ref, vp_ref, o_ref, lse_ref):
    mb = pl.program_id(2)
    nq = q_ref.shape[2] // BAND_BLOCK
    q = q_ref[0, 0]
    k = jnp.concatenate([kp_ref[0, 0], kc_ref[0, 0]], axis=0)
    v = jnp.concatenate([vp_ref[0, 0], vc_ref[0, 0]], axis=0)
    qi = lax.broadcasted_iota(I32, (BAND_BLOCK, 2 * BAND_BLOCK), 0)
    ki = lax.broadcasted_iota(I32, (BAND_BLOCK, 2 * BAND_BLOCK), 1)
    dist = qi + BAND_BLOCK - ki
    band = (dist >= 0) & (dist <= BAND_BLOCK)
    masks = [band & ((ki >= BAND_BLOCK) | (mb > 0))] + [band] * (nq - 1)
    lane_q = lax.broadcasted_iota(I32, (BAND_BLOCK, LANES), 1)
    lane_k = lax.broadcasted_iota(I32, (2 * BAND_BLOCK, LANES), 1)
    for hp in range(N_HEADS_A // 2):
        sl = slice(hp * LANES, (hp + 1) * LANES)
        chains = [(j, sub) for j in range(nq) for sub in range(2)]
        qs = [q[j * BAND_BLOCK:(j + 1) * BAND_BLOCK, sl] for j in range(nq)]
        ks = [k[j * BAND_BLOCK:(j + 2) * BAND_BLOCK, sl] for j in range(nq)]
        vs = [v[j * BAND_BLOCK:(j + 2) * BAND_BLOCK, sl] for j in range(nq)]
        mqs = [lane_q < HEAD_DIM_A, lane_q >= HEAD_DIM_A]
        mks = [lane_k < HEAD_DIM_A, lane_k >= HEAD_DIM_A]
        s = [jnp.where(masks[j], _dot_nt(jnp.where(mqs[sub], qs[j], jnp.zeros_like(qs[j])), ks[j]), -jnp.inf)
             for j, sub in chains]
        mx = [jnp.max(z, axis=1, keepdims=True) for z in s]
        p = [jnp.exp(z - m) for z, m in zip(s, mx)]
        l = [jnp.sum(z, axis=1, keepdims=True) for z in p]
        pv = [_dot(p[c].astype(BF16), jnp.where(mks[sub], vs[j], jnp.zeros_like(vs[j])))
              for c, (j, sub) in enumerate(chains)]
        for j in range(nq):
            c0, c1 = 2 * j, 2 * j + 1
            o_pair = pv[c0] / l[c0] + pv[c1] / l[c1]
            lse_pair = jnp.where(mqs[0], mx[c0] + jnp.log(l[c0]), mx[c1] + jnp.log(l[c1]))
            o_ref[0, 0, j * BAND_BLOCK:(j + 1) * BAND_BLOCK, sl] = o_pair.astype(BF16)
            lse_ref[0, 0, j * BAND_BLOCK:(j + 1) * BAND_BLOCK, sl] = lse_pair


def _attn_call(qkv_g, gi):
    b, dil, l, _ = qkv_g.shape
    nq = 4
    nb = l // (nq * BAND_BLOCK)
    blk = (1, 1, nq * BAND_BLOCK, D_GROUP_A)
    cur = lambda which: pl.BlockSpec(blk, lambda bb, r, m: (bb, r, m, which))
    prev = lambda which: pl.BlockSpec((1, 1, BAND_BLOCK, D_GROUP_A),
                                      lambda bb, r, m: (bb, r, jnp.maximum(nq * m - 1, 0), which))
    return pl.pallas_call(
        _attn_body,
        out_shape=(jax.ShapeDtypeStruct((b, dil, l, D_GROUP_A), BF16),
                   jax.ShapeDtypeStruct((b, dil, l, D_GROUP_A), F32)),
        grid=(b, dil, nb),
        in_specs=[cur(0), cur(1), prev(1), cur(2), prev(2)],
        out_specs=(pl.BlockSpec(blk, lambda bb, r, m: (bb, r, m, 0)),
                   pl.BlockSpec(blk, lambda bb, r, m: (bb, r, m, 0))),
        compiler_params=_cparams(("arbitrary", "arbitrary", "arbitrary")),
        name=f"attn{gi}",
    )(qkv_g, qkv_g, qkv_g, qkv_g, qkv_g)


def _sattn_body(qkv_ref, b1_ref, b2_ref, b3_ref, o_ref):
    n_rows = 3 * N_GROUPS_A * N_HEADS_A
    sq = jnp.concatenate([qkv_ref[0], jnp.zeros((LANES - n_rows, HEAD_DIM_A), F32)], axis=0)
    cols = jnp.concatenate([sq, jnp.zeros((LANES, LANES - HEAD_DIM_A), F32)], axis=1).T
    col3 = lambda first: jnp.stack([cols[:HEAD_DIM_A, first + h:first + h + 1] for h in range(N_HEADS_A)], axis=0)
    outs, lses = [], []
    for g, (buf_ref, (_, dil)) in enumerate(zip((b1_ref, b2_ref, b3_ref), DILATED_GROUPS)):
        q = col3(g * N_HEADS_A)
        kn = col3((N_GROUPS_A + g) * N_HEADS_A)
        vn = col3((2 * N_GROUPS_A + g) * N_HEADS_A)
        kb = buf_ref[0, 0]
        vb = buf_ref[0, 1]
        wb = kb.shape[-1]
        pos = lax.broadcasted_iota(I32, (1, 1, wb), 2)
        s = jnp.sum(kb * q, axis=1, keepdims=True)
        s = jnp.where(pos % dil == 0, s, -jnp.inf)
        sn = jnp.sum(kn * q, axis=1, keepdims=True)
        m = jnp.maximum(jnp.max(s, axis=2, keepdims=True), sn)
        p = jnp.exp(s - m)
        pn = jnp.exp(sn - m)
        l = jnp.sum(p, axis=2, keepdims=True) + pn
        outs.append((jnp.sum(p * vb, axis=2, keepdims=True) + pn * vn) / l)
        lses.append(m + jnp.log(l))
    mx = jnp.maximum(jnp.maximum(lses[0], lses[1]), lses[2])
    es = [jnp.exp(z - mx) for z in lses]
    o_a = (es[0] * outs[0] + es[1] * outs[1] + es[2] * outs[2]) / (es[0] + es[1] + es[2])
    o_cols = jnp.concatenate([o_a[h] for h in range(N_HEADS_A)] +
                             [jnp.zeros((HEAD_DIM_A, LANES - N_HEADS_A), F32)], axis=1)
    o_rows = jnp.concatenate([o_cols, jnp.zeros((LANES - HEAD_DIM_A, LANES), F32)], axis=0).T
    o_ref[0] = o_rows[:N_HEADS_A, :HEAD_DIM_A]


def _sattn_call(qkv_s, c1, c2, c3):
    n = qkv_s.shape[0]
    views, specs = [], []
    for c in (c1, c2, c3):
        wb = c.shape[1]
        views.append(jnp.transpose(c, (0, 2, 3, 4, 1)))
        specs.append(pl.BlockSpec((1, 2, N_HEADS_A, HEAD_DIM_A, wb), lambda b: (b, 0, 0, 0, 0)))
    return pl.pallas_call(
        _sattn_body,
        out_shape=jax.ShapeDtypeStruct((n, N_HEADS_A, HEAD_DIM_A), F32),
        grid=(n,),
        in_specs=[pl.BlockSpec((1, 3 * N_GROUPS_A * N_HEADS_A, HEAD_DIM_A), lambda b: (b, 0, 0))] + specs,
        out_specs=pl.BlockSpec((1, N_HEADS_A, HEAD_DIM_A), lambda b: (b, 0, 0)),
        compiler_params=_cparams(("arbitrary",)),
        name="sattn",
    )(qkv_s, *views)


def _rwkv_features(xs, w0, ww2, a0, wa2, wg2, k_a):
    r = xs[:, :D_B]
    k = xs[:, D_B:2 * D_B]
    v = xs[:, 2 * D_B:3 * D_B]
    xw = xs[:, 3 * D_B:3 * D_B + DECAY_LORA]
    xa = xs[:, 3 * D_B + DECAY_LORA:3 * D_B + DECAY_LORA + AAA_LORA]
    xg = xs[:, 3 * D_B + DECAY_LORA + AAA_LORA:]
    w_log = -_softplus(-(w0 + _dot(jnp.tanh(xw).astype(BF16), ww2.astype(BF16)))) - 0.5
    a = _sigmoid(a0 + _dot(xa.astype(BF16), wa2.astype(BF16)))
    g = _dot(_sigmoid(xg).astype(BF16), wg2.astype(BF16))
    k_h = k * (1.0 + (a - 1.0) * k_a)
    return r, k, v, w_log, a, g, k_h


def _head_norm(kk_h):
    nrm = jnp.sqrt(jnp.sum(kk_h * kk_h, axis=-1, keepdims=True))
    return kk_h / jnp.maximum(nrm, 1e-12)


def _wkv_finish_head(y, r_h, k_h, v_h, g_h, rk_h, lnw_h, lnb_h):
    mean = jnp.mean(y, axis=-1, keepdims=True)
    var = jnp.mean(jnp.square(y - mean), axis=-1, keepdims=True)
    yn = (y - mean) * lax.rsqrt(var + LN_X_EPS) * lnw_h + lnb_h
    bonus = jnp.sum(r_h * k_h * rk_h, axis=-1, keepdims=True) * v_h
    return (yn + bonus) * g_h


def _wkv_body(f_ref, fp_ref, mu_ref, w0_ref, ww2_ref, a0_ref, wa2_ref, wg2_ref, kk_ref, ka_ref,
              rk_ref, lnw_ref, lnb_ref, o_ref, st_ref, s_ref):
    c = pl.program_id(0)
    C = WKV_CHUNK
    nb = f_ref.shape[0]

    @pl.when(c == 0)
    def _():
        s_ref[...] = jnp.zeros_like(s_ref)

    f = jnp.concatenate([f_ref[b] for b in range(nb)], axis=0)
    row = lax.broadcasted_iota(I32, f.shape, 0)
    prev = pltpu.roll(f, 1, 0)
    for b in range(nb):
        prev = jnp.where(row == b * C, jnp.where(c == 0, 0.0, fp_ref[b][7:8, :]), prev)
    xs = f + mu_ref[...] * (prev - f)
    r, k, v, w_log, a, g, k_h = _rwkv_features(xs, w0_ref[...], ww2_ref[...], a0_ref[...],
                                               wa2_ref[...], wg2_ref[...], ka_ref[...])
    lw = -jnp.exp(w_log)
    kk = k * kk_ref[...]
    jh = lax.broadcasted_iota(I32, (D_B, LANES), 0) // HEAD_DIM_B
    ind = (jh == lax.broadcasted_iota(I32, (D_B, LANES), 1)).astype(BF16)
    ind_t = (lax.broadcasted_iota(I32, (LANES, D_B), 0)
             == lax.broadcasted_iota(I32, (LANES, D_B), 1) // HEAD_DIM_B).astype(BF16)

    def head_sum(z):
        hi = z.astype(BF16)
        lo = (z - hi.astype(F32)).astype(BF16)
        s = _dot(hi, ind) + _dot(lo, ind)
        shi = s.astype(BF16)
        slo = (s - shi.astype(F32)).astype(BF16)
        return _dot(shi, ind_t) + _dot(slo, ind_t)

    kkn = kk / jnp.maximum(jnp.sqrt(head_sum(kk * kk)), 1e-12)

    tr = lax.broadcasted_iota(I32, (nb * C, nb * C), 0)
    sr_ = lax.broadcasted_iota(I32, (nb * C, nb * C), 1)
    tri_incl = ((tr >= sr_) & (tr // C == sr_ // C)).astype(BF16)
    l1 = lw.astype(BF16)
    r1 = lw - l1.astype(F32)
    l2 = r1.astype(BF16)
    l3 = (r1 - l2.astype(F32)).astype(BF16)
    cum = _dot(tri_incl, l1) + _dot(tri_incl, l2) + _dot(tri_incl, l3)
    rhos = [cum[b * C + C // 2 - 1:b * C + C // 2, :] for b in range(nb)]
    rho = jnp.concatenate([jnp.broadcast_to(z, (C, D_B)) for z in rhos], axis=0)
    ep = jnp.exp(cum - rho)
    em = jnp.exp(rho - cum)
    e_a = ep * jnp.exp(-lw)
    r_hat = r * ep
    k_hat = k_h * em
    e_rs = [jnp.exp(z) for z in rhos]
    e_cs = [jnp.exp(cum[b * C + C - 1:b * C + C, :] - rhos[b]) for b in range(nb)]

    ti = lax.broadcasted_iota(I32, (C, C), 0)
    si = lax.broadcasted_iota(I32, (C, C), 1)
    strict = ti > si
    incl = ti >= si
    rk = rk_ref[...]
    lnw = lnw_ref[...]
    lnb = lnb_ref[...]
    items = [(b, h) for b in range(nb) for h in range(N_HEADS_B)]
    heads = range(len(items))
    lanes = [slice(h * HEAD_DIM_B, (h + 1) * HEAD_DIM_B) for _, h in items]
    cut = lambda z, i: z[items[i][0] * C:(items[i][0] + 1) * C, lanes[i]]
    e_r = [e_rs[b][:, lanes[i]] for i, (b, _) in enumerate(items)]
    e_c = [e_cs[b][:, lanes[i]] for i, (b, _) in enumerate(items)]
    a_hat_full = (-kkn * e_a).astype(BF16)
    b_hat_full = (kkn * a * em).astype(BF16)
    a_hat_b = [cut(a_hat_full, h) for h in heads]
    b_hat_b = [cut(b_hat_full, h) for h in heads]
    rh = [cut(r_hat, h) for h in heads]
    vb = [cut(v, h).astype(BF16) for h in heads]
    bk = [jnp.concatenate([b_hat_b[h], cut(k_hat, h).astype(BF16)], axis=0) for h in heads]
    p = [_dot_nt(jnp.concatenate([a_hat_b[h], rh[h].astype(BF16)], axis=0), bk[h]) for h in heads]
    l_ak = [jnp.where(strict, z[:C, C:], 0.0).astype(BF16) for z in p]
    p_rb = [jnp.where(incl, z[C:, :C], 0.0).astype(BF16) for z in p]
    p_rk = [jnp.where(incl, z[C:, C:], 0.0).astype(BF16) for z in p]
    col = lax.broadcasted_iota(I32, (C, 2 * C), 1)
    row2 = lax.broadcasted_iota(I32, (C, 2 * C), 0)
    left = col < C
    zt = [jnp.where(left, jnp.where(row2 > col, z[:C], 0.0), (col == row2 + C).astype(F32)) for z in p]
    for _ in range(int(math.log2(C))):
        zb = [z.astype(BF16) for z in zt]
        res = [_dot(z[:, :C], z) for z in zb]
        zt = [jnp.where(left, res[h], zt[h] + res[h]) for h in heads]
    tb = [z.astype(BF16) for z in zt]
    zeros_c = jnp.zeros((C, HEAD_DIM_B), BF16)
    lv = [_dot(l_ak[h], vb[h]).astype(BF16) for h in heads]
    a_bar = [_dot(tb[h], jnp.concatenate([zeros_c, a_hat_b[h]], axis=0)).astype(BF16) for h in heads]
    u_v = [_dot(tb[h], jnp.concatenate([zeros_c, lv[h]], axis=0)).astype(BF16) for h in heads]
    r_bar = [rh[h] + _dot(p_rb[h], a_bar[h]) for h in heads]
    y_v = [_dot(p_rb[h], u_v[h]) + _dot(p_rk[h], vb[h]) for h in heads]
    ab = [_dot_tn(a_bar[h], b_hat_b[h]).astype(BF16) for h in heads]
    n_t = [_dot_tn(jnp.concatenate([u_v[h], vb[h]], axis=0), bk[h]) for h in heads]
    s0 = [s_ref[b, h] for b, h in items]
    sr = [s0[h] * e_r[h] for h in heads]
    y = [_dot_nt((r_bar[h] * e_r[h]).astype(BF16), s0[h].astype(BF16)) + y_v[h] for h in heads]
    s_new = [(sr[h] + _dot(sr[h].astype(BF16), ab[h]) + n_t[h]) * e_c[h] for h in heads]
    for i, (b, h) in enumerate(items):
        s_ref[b, h] = s_new[i]
    y_full = jnp.concatenate([jnp.concatenate(y[b * N_HEADS_B:(b + 1) * N_HEADS_B], axis=1) for b in range(nb)],
                             axis=0)
    inv_hd = 1.0 / HEAD_DIM_B
    dev = y_full - head_sum(y_full) * inv_hd
    yn = dev * lax.rsqrt(head_sum(dev * dev) * inv_hd + LN_X_EPS) * lnw + lnb
    out = (yn + head_sum(r * k_h * rk) * v) * g
    for b in range(nb):
        o_ref[b] = out[b * C:(b + 1) * C, :]

    @pl.when(c == pl.num_programs(0) - 1)
    def _():
        st_ref[...] = s_ref[...]


def _wkv_call(feat, p):
    b, t, _ = feat.shape
    C = WKV_CHUNK
    nc = t // C
    row = lambda n: pl.BlockSpec((1, n), lambda c: (0, 0))
    mat = lambda m, n: pl.BlockSpec((m, n), lambda c: (0, 0))
    return pl.pallas_call(
        _wkv_body,
        out_shape=(jax.ShapeDtypeStruct((b, t, D_B), F32),
                   jax.ShapeDtypeStruct((b, N_HEADS_B, HEAD_DIM_B, HEAD_DIM_B), F32)),
        grid=(nc,),
        in_specs=[pl.BlockSpec((b, C, D_SHIFT_B), lambda c: (0, c, 0)),
                  pl.BlockSpec((b, 8, D_SHIFT_B), lambda c: (0, jnp.maximum(c * (C // 8) - 1, 0), 0)),
                  row(D_SHIFT_B), row(D_B), mat(DECAY_LORA, D_B), row(D_B), mat(AAA_LORA, D_B),
                  mat(GATE_LORA, D_B), row(D_B), row(D_B), row(D_B), row(D_B), row(D_B)],
        out_specs=(pl.BlockSpec((b, C, D_B), lambda c: (0, c, 0)),
                   pl.BlockSpec((b, N_HEADS_B, HEAD_DIM_B, HEAD_DIM_B), lambda c: (0, 0, 0, 0))),
        scratch_shapes=[pltpu.VMEM((b, N_HEADS_B, HEAD_DIM_B, HEAD_DIM_B), F32)],
        compiler_params=_cparams(("arbitrary",)),
        name="wkv",
    )(feat, feat, p['mu_b'], p['w0_b'], p['w_w2_b'], p['a0_b'], p['w_a2_b'], p['w_g2_b'],
      p['k_k_b'], p['k_a_b'], p['r_k_b'], p['ln_x_w_b'], p['ln_x_b_b'])


def _swkv_prep_body(f_ref, sh_ref, mu_ref, w0_ref, ww2_ref, a0_ref, wa2_ref, wg2_ref, kk_ref, ka_ref,
                    r_ref, w_ref, k_ref, v_ref, aa_ref, bb_ref, g_ref):
    f = f_ref[...]
    xs = f + mu_ref[...] * (sh_ref[...] - f)
    r, k, v, w_log, a, g, k_h = _rwkv_features(xs, w0_ref[...], ww2_ref[...], a0_ref[...],
                                               wa2_ref[...], wg2_ref[...], ka_ref[...])
    kk = k * kk_ref[...]
    kkn = jnp.concatenate([_head_norm(kk[:, h * HEAD_DIM_B:(h + 1) * HEAD_DIM_B]) for h in range(N_HEADS_B)],
                          axis=1)
    r_ref[...] = r
    w_ref[...] = jnp.exp(-jnp.exp(w_log))
    k_ref[...] = k_h
    v_ref[...] = v
    aa_ref[...] = -kkn
    bb_ref[...] = kkn * a
    g_ref[...] = g


def _swkv_prep_call(feat_s, shift0, p):
    n = feat_s.shape[0]
    full = lambda a: pl.BlockSpec(a.shape, lambda: tuple(0 for _ in a.shape))
    args = (feat_s, shift0, p['mu_b'], p['w0_b'], p['w_w2_b'], p['a0_b'], p['w_a2_b'], p['w_g2_b'],
            p['k_k_b'], p['k_a_b'])
    return pl.pallas_call(
        _swkv_prep_body,
        out_shape=tuple(jax.ShapeDtypeStruct((n, D_B), F32) for _ in range(7)),
        in_specs=[full(a) for a in args],
        out_specs=tuple(pl.BlockSpec((n, D_B), lambda: (0, 0)) for _ in range(7)),
        compiler_params=pltpu.CompilerParams(vmem_limit_bytes=VMEM_LIMIT),
        name="swkv_prep",
    )(*args)


def _swkv_step_body(s_ref, a_ref, w_ref, b_ref, k_ref, r_ref, v_ref, so_ref, y_ref):
    s = s_ref[...]
    th = s.shape[0]
    pad_sq = lambda z: jnp.concatenate(
        [jnp.concatenate([z, jnp.zeros((z.shape[0], LANES - z.shape[1]), F32)], axis=1),
         jnp.zeros((LANES - z.shape[0], LANES), F32)], axis=0)
    v_t = pad_sq(v_ref[...]).T
    v_col = jnp.stack([v_t[:HEAD_DIM_B, j:j + 1] for j in range(th)], axis=0)
    sa = jnp.sum(s * a_ref[...], axis=-1, keepdims=True)
    s2 = s * w_ref[...] + sa * b_ref[...] + v_col * k_ref[...]
    so_ref[...] = s2
    y = jnp.sum(s2 * r_ref[...], axis=-1, keepdims=True)
    y_t = jnp.concatenate([y[j] for j in range(th)], axis=1)
    y_ref[...] = pad_sq(y_t).T[:th, :HEAD_DIM_B]


def _swkv_step_call(s0, aa, w, bb, k, r, v):
    nh = s0.shape[0]
    th = 64
    rowspec = pl.BlockSpec((th, 1, HEAD_DIM_B), lambda i: (i, 0, 0))
    matspec = pl.BlockSpec((th, HEAD_DIM_B), lambda i: (i, 0))
    stspec = pl.BlockSpec((th, HEAD_DIM_B, HEAD_DIM_B), lambda i: (i, 0, 0))
    return pl.pallas_call(
        _swkv_step_body,
        out_shape=(jax.ShapeDtypeStruct((nh, HEAD_DIM_B, HEAD_DIM_B), F32),
                   jax.ShapeDtypeStruct((nh, HEAD_DIM_B), F32)),
        grid=(nh // th,),
        in_specs=[stspec, rowspec, rowspec, rowspec, rowspec, rowspec, matspec],
        out_specs=(stspec, matspec),
        compiler_params=_cparams(("arbitrary",)),
        name="swkv_step",
    )(s0, aa, w, bb, k, r, v)


def _swkv_fin_body(y_ref, r_ref, k_ref, v_ref, g_ref, rk_ref, lnw_ref, lnb_ref, o_ref):
    y, r, k, v, g = y_ref[...], r_ref[...], k_ref[...], v_ref[...], g_ref[...]
    rk, lnw, lnb = rk_ref[...], lnw_ref[...], lnb_ref[...]
    outs = []
    for h in range(N_HEADS_B):
        sl = slice(h * HEAD_DIM_B, (h + 1) * HEAD_DIM_B)
        outs.append(_wkv_finish_head(y[:, sl], r[:, sl], k[:, sl], v[:, sl], g[:, sl],
                                     rk[:, sl], lnw[:, sl], lnb[:, sl]))
    o_ref[...] = jnp.concatenate(outs, axis=1)


def _swkv_fin_call(y, r, k, v, g, p):
    n = y.shape[0]
    args = (y, r, k, v, g, p['r_k_b'], p['ln_x_w_b'], p['ln_x_b_b'])
    full = lambda a: pl.BlockSpec(a.shape, lambda: (0, 0))
    return pl.pallas_call(
        _swkv_fin_body,
        out_shape=jax.ShapeDtypeStruct((n, D_B), F32),
        in_specs=[full(a) for a in args],
        out_specs=pl.BlockSpec((n, D_B), lambda: (0, 0)),
        name="swkv_fin",
    )(*args)


def _route_t(scores, bias_col):
    n = scores.shape[1]
    gsz = N_EXPERTS // N_EXPERT_GROUPS
    choice = scores + bias_col
    ninf = -jnp.inf
    sid = lax.broadcasted_iota(I32, (gsz, n), 0)
    gs = []
    for gidx in range(N_EXPERT_GROUPS):
        blk = choice[gidx * gsz:(gidx + 1) * gsz, :]
        m1 = jnp.max(blk, axis=0, keepdims=True)
        first = jnp.min(jnp.where(blk == m1, sid, gsz), axis=0, keepdims=True)
        m2 = jnp.max(jnp.where(sid == first, ninf, blk), axis=0, keepdims=True)
        gs.append(m1 + m2)
    cur = jnp.concatenate(gs, axis=0)
    gid = lax.broadcasted_iota(I32, (N_EXPERT_GROUPS, n), 0)
    gmask = jnp.zeros((N_EXPERT_GROUPS, n), F32)
    for _ in range(TOPK_GROUPS):
        m = jnp.max(cur, axis=0, keepdims=True)
        first = jnp.min(jnp.where(cur == m, gid, N_EXPERT_GROUPS), axis=0, keepdims=True)
        sel = gid == first
        gmask = jnp.where(sel, 1.0, gmask)
        cur = jnp.where(sel, ninf, cur)
    emask = jnp.concatenate([jnp.broadcast_to(gmask[gidx:gidx + 1, :], (gsz, n))
                             for gidx in range(N_EXPERT_GROUPS)], axis=0)
    cur = jnp.where(emask > 0.5, choice, ninf)
    eid = lax.broadcasted_iota(I32, (N_EXPERTS, n), 0)
    selm = jnp.zeros((N_EXPERTS, n), F32)
    for _ in range(TOP_K):
        m = jnp.max(cur, axis=0, keepdims=True)
        first = jnp.min(jnp.where(cur == m, eid, N_EXPERTS), axis=0, keepdims=True)
        sel = eid == first
        selm = jnp.where(sel, 1.0, selm)
        cur = jnp.where(sel, ninf, cur)
    w = jnp.where(selm > 0.5, scores, 0.0)
    w = w / jnp.sum(w, axis=0, keepdims=True) * ROUTED_SCALE
    return jnp.where(selm > 0.5, w, -1.0)


def _unpermute(blk_ref, scr_ref, dil, tm):
    if dil == 1:
        return blk_ref[0, 0].astype(F32)
    n_chunks = scr_ref.shape[0]
    for r in range(dil):
        rows = blk_ref[0, r].astype(F32)
        for j in range(n_chunks):
            scr_ref[j, pl.ds(r, tm // dil, stride=dil), :] = rows[:, j * LANES:(j + 1) * LANES]
    return jnp.concatenate([scr_ref[j] for j in range(n_chunks)], axis=1)


def _post_body(*refs, combine, dils):
    if combine:
        o_refs, l_refs, rest = refs[:3], refs[3:6], refs[6:]
    else:
        o_refs, rest = refs[:1], refs[1:]
    (ob_ref, gt_ref, x_ref, g1_ref, sc2_ref, sh2_ref, npost_ref, npre_ref, wa_ref, wb_ref, wo_ref,
     wrt_ref, rb_ref, x1_ref, hp_ref, wt_ref) = rest[:16]
    scr = rest[16:]
    tm = x_ref.shape[1]
    if combine:
        os_, ls_ = [], []
        si = 0
        for gi, dil in enumerate(dils):
            os_.append(_unpermute(o_refs[gi], scr[si] if dil > 1 else None, dil, tm))
            ls_.append(_unpermute(l_refs[gi], scr[si + 1] if dil > 1 else None, dil, tm))
            si += 2 if dil > 1 else 0
        mx = jnp.maximum(jnp.maximum(ls_[0], ls_[1]), ls_[2])
        es = [jnp.exp(z - mx) for z in ls_]
        o_a = (es[0] * os_[0] + es[1] * os_[1] + es[2] * os_[2]) / (es[0] + es[1] + es[2])
    else:
        o_a = o_refs[0][0]
    gt = gt_ref[0].astype(F32)
    za = _dot(o_a.astype(BF16), wa_ref[...])
    zb = _dot(ob_ref[0].astype(BF16), wb_ref[...])
    merged = gt[:, :D_MODEL] * za + gt[:, D_MODEL:] * zb
    z = _dot(merged.astype(BF16), wo_ref[...])
    x1 = x_ref[0] + g1_ref[0] * _rms(z, npost_ref[...])
    x1_ref[0] = x1
    h2 = _rms(x1, npre_ref[...]) * (1.0 + sc2_ref[0]) + sh2_ref[0]
    _rows_to_tiles(hp_ref.at[0], _pack_pairs(h2))
    tp =-(-tm // LANES) * LANES
    if tp != tm:
        h2 = jnp.concatenate([h2, jnp.zeros((tp - tm, D_MODEL), F32)], axis=0)
    logits_t = _dot_nt_split(wrt_ref[...], h2)
    w = _route_t(_sigmoid(logits_t[:N_EXPERTS, :]), rb_ref[...])
    wt_ref[...] = w[:, :tm]


def _post_call(o_parts, lse_parts, ob, gates, x, gate1, scale2, shift2, p, wa, wb, wo, wrt, rb, tm, mod_per_row):
    nb, t, _ = x.shape
    nt = t // tm
    combine = lse_parts is not None
    rowblk = lambda width: pl.BlockSpec((1, tm, width), lambda b, i: (b, i, 0))
    if mod_per_row:
        mod_spec = rowblk(D_MODEL)
    else:
        mod_spec = pl.BlockSpec((1, 1, D_MODEL), lambda b, i: (b, 0, 0))
    const = lambda shp: pl.BlockSpec(shp, lambda b, i: (0, 0))
    scratch = []
    if combine:
        dils = tuple(o.shape[1] for o in o_parts)
        o_args = list(o_parts) + list(lse_parts)
        o_specs = [pl.BlockSpec((1, d, tm // d, D_GROUP_A), lambda b, i: (b, 0, i, 0)) for d in dils] * 2
        for d in dils:
            if d > 1:
                scratch += [pltpu.VMEM((D_GROUP_A // LANES, tm, LANES), F32)] * 2
    else:
        dils = ()
        o_args = [o_parts[0]]
        o_specs = [rowblk(D_GROUP_A)]
    return pl.pallas_call(
        functools.partial(_post_body, combine=combine, dils=dils),
        out_shape=(jax.ShapeDtypeStruct((nb, t, D_MODEL), F32),
                   jax.ShapeDtypeStruct((nb, t * ROW_PITCH, LANES), I32),
                   jax.ShapeDtypeStruct((N_EXPERTS, nb * t), F32)),
        grid=(nb, nt),
        in_specs=o_specs + [rowblk(D_B), rowblk(2 * D_MODEL), rowblk(D_MODEL),
                            mod_spec, mod_spec, mod_spec, const((1, D_MODEL)), const((1, D_MODEL)),
                            const((D_GROUP_A, D_MODEL)), const((D_B, D_MODEL)), const((D_MODEL, D_MODEL)),
                            const((LANES, D_MODEL)), const((N_EXPERTS, 1))],
        out_specs=(rowblk(D_MODEL),
                   pl.BlockSpec((1, tm * ROW_PITCH, LANES), lambda b, i: (b, i, 0)),
                   pl.BlockSpec((N_EXPERTS, tm), lambda b, i: (0, b * nt + i))),
        scratch_shapes=scratch,
        compiler_params=_cparams(("arbitrary", "arbitrary")),
        name="post",
    )(*o_args, ob, gates, x, gate1, scale2, shift2, p['norm_post_mix'].reshape(1, -1),
      p['norm_pre_ffn'].reshape(1, -1), wa, wb, wo, wrt, rb)


def _rank_body(w_ref, dest_ref, w8_ref, tab_ref, etab_ref, cnt_ref, pst_ref, run_ref, *, n_real, n_slots):
    ph = pl.program_id(0)
    i = pl.program_id(1)
    T = MOE_TILE
    w = w_ref[...]
    sel = (w >= 0.0).astype(F32)
    cnt_tile = jnp.broadcast_to(jnp.sum(sel, axis=1, keepdims=True), (N_EXPERTS, LANES))
    ei = lax.broadcasted_iota(I32, (N_EXPERTS, N_EXPERTS), 0)
    ej = lax.broadcasted_iota(I32, (N_EXPERTS, N_EXPERTS), 1)

    @pl.when((ph == 0) & (i == 0))
    def _():
        cnt_ref[...] = jnp.zeros_like(cnt_ref)

    @pl.when(ph == 0)
    def _():
        cnt_ref[...] += cnt_tile

    @pl.when((ph == 1) & (i == 0))
    def _():
        cnt = cnt_ref[...]
        padded = jnp.floor((cnt + (EXPERT_BLOCK - 1)) / EXPERT_BLOCK) * EXPERT_BLOCK
        pstart = _dot_exact((ej < ei).astype(F32), padded)
        pst_ref[...] = pstart
        run_ref[...] = jnp.zeros_like(run_ref)
        pend = pstart + padded
        vend = pstart + cnt
        esub = lax.broadcasted_iota(I32, (N_EXPERTS, LANES), 0)
        lane = lax.broadcasted_iota(I32, (1, LANES), 1)
        tab_ref[...] = jnp.zeros_like(tab_ref)
        for c in range(tab_ref.shape[1] // LANES):
            bs = ((c * LANES + lane) * EXPERT_BLOCK).astype(F32)
            be = jnp.minimum(jnp.sum((pend <= bs).astype(F32), axis=0, keepdims=True), N_EXPERTS - 1.0)
            tab_ref[0:1, c * LANES:(c + 1) * LANES] = be.astype(I32)
            tab_ref[1:2, c * LANES:(c + 1) * LANES] = (pend[N_EXPERTS - 1:, :] / EXPERT_BLOCK).astype(I32)
        on_diag = esub == lax.broadcasted_iota(I32, (N_EXPERTS, LANES), 1)
        etab_ref[...] = jnp.zeros_like(etab_ref)
        lo = jnp.sum(jnp.where(on_diag, vend, 0.0), axis=0, keepdims=True)
        hi = jnp.sum(jnp.where(on_diag, pend, 0.0), axis=0, keepdims=True)
        etab_ref[0:1, :] = jnp.where(lane == N_EXPERTS, pend[N_EXPERTS - 1:, :], lo).astype(I32)
        etab_ref[1:2, :] = jnp.where(lane == N_EXPERTS, float(n_slots), hi).astype(I32)

    @pl.when(ph == 1)
    def _():
        ti = lax.broadcasted_iota(I32, (T, T), 0)
        tj = lax.broadcasted_iota(I32, (T, T), 1)
        selb = sel.astype(BF16)
        rank = _dot(selb, (ti < tj).astype(BF16))
        ordn = _dot((ej < ei).astype(BF16), selb)
        dest_e = pst_ref[:, :1] + run_ref[:, :1] + rank
        run_ref[...] += cnt_tile
        tok = i * T + lax.broadcasted_iota(I32, (1, T), 1)
        dks, wks = [], []
        for k in range(TOP_K):
            m = (sel > 0.5) & (ordn == float(k))
            dk = jnp.sum(jnp.where(m, dest_e, 0.0), axis=0, keepdims=True)
            wk = jnp.sum(jnp.where(m, w, 0.0), axis=0, keepdims=True)
            dks.append(jnp.where(tok < n_real, dk, 0.0))
            wks.append(jnp.where(tok < n_real, wk, 0.0))
        dest_ref[...] = jnp.concatenate(dks, axis=0).astype(I32)
        w8_ref[...] = jnp.concatenate(wks, axis=0)


def _rank_call(w_t, n_real, n_blocks, n_blocks_pad):
    n = w_t.shape[1]
    nt = n // MOE_TILE
    return pl.pallas_call(
        functools.partial(_rank_body, n_real=n_real, n_slots=n_blocks * EXPERT_BLOCK),
        out_shape=(jax.ShapeDtypeStruct((TOP_K, n), I32),
                   jax.ShapeDtypeStruct((TOP_K, n), F32),
                   jax.ShapeDtypeStruct((8, n_blocks_pad), I32),
                   jax.ShapeDtypeStruct((8, LANES), I32)),
        grid=(2, nt),
        in_specs=[pl.BlockSpec((N_EXPERTS, MOE_TILE), lambda ph, i: (0, i))],
        out_specs=(pl.BlockSpec((TOP_K, MOE_TILE), lambda ph, i: (0, i * ph)),
                   pl.BlockSpec((TOP_K, MOE_TILE), lambda ph, i: (0, i * ph)),
                   pl.BlockSpec((8, n_blocks_pad), lambda ph, i: (0, 0)),
                   pl.BlockSpec((8, LANES), lambda ph, i: (0, 0))),
        scratch_shapes=[pltpu.VMEM((N_EXPERTS, LANES), F32)] * 3,
        compiler_params=_cparams(("arbitrary", "arbitrary")),
        name="rank",
    )(w_t)


def _tile_rows(ref, row, n):
    return ref.at[pl.ds(row * ROW_PITCH, n * ROW_PITCH)]


def _zero_fill(etab_ref, zbuf, xs_hbm, zsem, wait):
    def go(src, dst):
        cp = pltpu.make_async_copy(src, dst, zsem)
        if wait:
            cp.wait()
        else:
            cp.start()

    def per_range(e, carry):
        lo = etab_ref[0, e]
        n = etab_ref[1, e] - lo
        n_full = n // ZERO_ROWS

        def full(j, c):
            go(zbuf, _tile_rows(xs_hbm, lo + j * ZERO_ROWS, ZERO_ROWS))
            return c

        lax.fori_loop(0, n_full, full, 0)
        pos = lo + n_full * ZERO_ROWS
        rem = n - n_full * ZERO_ROWS
        size = ZERO_ROWS // 2
        while size >= 1:
            bit = rem & size

            @pl.when(bit != 0)
            def _(size=size, pos=pos):
                go(_tile_rows(zbuf, 0, size), _tile_rows(xs_hbm, pos, size))

            pos = pos + bit
            size //= 2
        return carry

    lax.fori_loop(0, N_EXPERTS + 1, per_range, 0)


def _dispatch_body(dest_ref, etab_ref, xa_ref, xb_ref, xs_hbm, zbuf, sem, zsem, *, n_real, n_full):
    i = pl.program_id(0)
    T = MOE_TILE
    n_tok = jnp.clip(n_real - i * T, 0, T)

    def issue_from(x_ref):
        def issue(t, carry):
            for k in range(TOP_K):
                pltpu.make_async_copy(_tile_rows(x_ref, t, 1), _tile_rows(xs_hbm, dest_ref[k * T + t], 1),
                                      sem).start(priority=k % 2)
            return carry

        lax.fori_loop(0, n_tok, issue, 0)

    @pl.when(i < n_full)
    def _():
        issue_from(xa_ref)

    @pl.when(i >= n_full)
    def _():
        issue_from(xb_ref)

    @pl.when(i == 0)
    def _():
        zbuf[...] = jnp.zeros_like(zbuf)
        _zero_fill(etab_ref, zbuf, xs_hbm, zsem, wait=False)
        _zero_fill(etab_ref, zbuf, xs_hbm, zsem, wait=True)

    @pl.when(n_tok == T)
    def _():
        pltpu.make_async_copy(_tile_rows(xs_hbm, 0, T * TOP_K), _tile_rows(xs_hbm, 0, T * TOP_K), sem).wait()

    @pl.when(n_tok < T)
    def _():
        def drain(j, carry):
            pltpu.make_async_copy(_tile_rows(xs_hbm, 0, 1), _tile_rows(xs_hbm, 0, 1), sem).wait()
            return carry

        lax.fori_loop(0, n_tok * TOP_K, drain, 0)


def _dispatch_call(dest, etab, hp_a, hp_b, n_real, n_slots):
    tile_rows = MOE_TILE * ROW_PITCH
    n_full = hp_a.shape[0] // tile_rows
    return pl.pallas_call(
        functools.partial(_dispatch_body, n_real=n_real, n_full=n_full),
        out_shape=jax.ShapeDtypeStruct((n_slots * ROW_PITCH, LANES), I32),
        grid=(n_full + 1,),
        in_specs=[pl.BlockSpec((TOP_K * MOE_TILE,), lambda i: (i,), memory_space=pltpu.SMEM),
                  pl.BlockSpec((8, LANES), lambda i: (0, 0), memory_space=pltpu.SMEM),
                  pl.BlockSpec((tile_rows, LANES), lambda i: (jnp.minimum(i, n_full - 1), 0)),
                  pl.BlockSpec((tile_rows, LANES), lambda i: (0, 0))],
        out_specs=pl.BlockSpec(memory_space=pl.ANY),
        scratch_shapes=[pltpu.VMEM((ZERO_ROWS * ROW_PITCH, LANES), I32),
                        pltpu.SemaphoreType.DMA, pltpu.SemaphoreType.DMA],
        compiler_params=_cparams(("arbitrary",)),
        name="dispatch",
    )(dest, etab, hp_a, hp_b)


def _rows_from_tiles(ref, lo, n):
    return jnp.concatenate([ref[pl.ds(lo * ROW_PITCH + s, n, stride=ROW_PITCH), :]
                            for s in range(ROW_TILE_SUBLANES)], axis=1)


def _rows_to_tiles(ref, packed):
    n = packed.shape[0]
    for s in range(ROW_TILE_SUBLANES):
        ref[pl.ds(s, n, stride=ROW_PITCH), :] = packed[:, s * LANES:(s + 1) * LANES]
    for s in range(ROW_TILE_SUBLANES, ROW_PITCH):
        ref[pl.ds(s, n, stride=ROW_PITCH), :] = jnp.zeros((n, LANES), packed.dtype)


def _ffn_body(be_ref, nu_ref, xs_ref, wg_ref, wu_ref, wd_ref, ys_ref, wgb, wub, wdb):
    j = pl.program_id(0)

    @pl.when(j < nu_ref[0])
    def _():
        @pl.when((j == 0) | (be_ref[j] != be_ref[jnp.maximum(j - 1, 0)]))
        def _():
            wgb[...] = wg_ref[0].astype(BF16)
            wub[...] = wu_ref[0].astype(BF16)
            wdb[...] = wd_ref[0].astype(BF16)

        x = _unpack_pairs(_rows_from_tiles(xs_ref, 0, EXPERT_BLOCK)).astype(BF16)
        act = _silu(_dot(x, wgb[...])) * _dot(x, wub[...])
        y = _dot(act.astype(BF16), wdb[...])
        _rows_to_tiles(ys_ref, _pack_pairs(y))

    @pl.when(j >= nu_ref[0])
    def _():
        ys_ref[...] = jnp.zeros_like(ys_ref)


def _ffn_call(blk_e, n_used, xs, w_gate, w_up, w_down, n_blocks):
    tile_blk = pl.BlockSpec((EXPERT_BLOCK * ROW_PITCH, LANES), lambda j, be, nu: (j, 0))
    last = lambda j, nu: jnp.minimum(j, nu[0] - 1)
    grid_spec = pltpu.PrefetchScalarGridSpec(
        num_scalar_prefetch=2,
        grid=(n_blocks,),
        in_specs=[pl.BlockSpec((EXPERT_BLOCK * ROW_PITCH, LANES), lambda j, be, nu: (last(j, nu), 0)),
                  pl.BlockSpec((1, D_MODEL, D_EXPERT), lambda j, be, nu: (be[last(j, nu)], 0, 0)),
                  pl.BlockSpec((1, D_MODEL, D_EXPERT), lambda j, be, nu: (be[last(j, nu)], 0, 0)),
                  pl.BlockSpec((1, D_EXPERT, D_MODEL), lambda j, be, nu: (be[last(j, nu)], 0, 0))],
        out_specs=tile_blk,
        scratch_shapes=[pltpu.VMEM((D_MODEL, D_EXPERT), BF16), pltpu.VMEM((D_MODEL, D_EXPERT), BF16),
                        pltpu.VMEM((D_EXPERT, D_MODEL), BF16)])
    return pl.pallas_call(
        _ffn_body,
        out_shape=jax.ShapeDtypeStruct((n_blocks * EXPERT_BLOCK * ROW_PITCH, LANES), I32),
        grid_spec=grid_spec,
        compiler_params=_cparams(("arbitrary",)),
        name="ffn",
    )(blk_e, n_used, xs, w_gate, w_up, w_down)


def _combine_body(dest_ref, dnext_ref, w8_ref, xa_ref, xb_ref, x1a_ref, x1b_ref, g2a_ref, g2b_ref, gain_ref,
                  sg_ref, su_ref, sd_ref, ys_hbm, oa_ref, ob_ref, buf, sem):
    j = pl.program_id(0)
    T = COMBINE_TILE
    RC = COMBINE_ROWS

    def issue(d_ref, slot, t):
        for k in range(TOP_K):
            pltpu.make_async_copy(_tile_rows(ys_hbm, d_ref[k * T + t], 1), _tile_rows(buf.at[slot], k * T + t, 1),
                                  sem.at[slot]).start(priority=k % 2)

    def wait(slot):
        pltpu.make_async_copy(_tile_rows(ys_hbm, 0, T * TOP_K), buf.at[slot], sem.at[slot]).wait()

    def step(slot):
        is_tail = j == 0
        wait(slot)
        for t in range(RC):
            issue(dnext_ref, 1 - slot, t)
        x = _unpack_pairs(jnp.where(is_tail, _rows_from_tiles(xb_ref, 0, T),
                                    _rows_from_tiles(xa_ref, 0, T))).astype(BF16)
        shared = _dot((_silu(_dot(x, sg_ref[...])) * _dot(x, su_ref[...])).astype(BF16), sd_ref[...])
        w_t = jnp.concatenate([w8_ref[...], jnp.zeros((LANES - TOP_K, T), F32)], axis=0).T
        oa_ref[...] = shared
        for r0 in range(0, T, RC):
            if r0 > 0:
                for t in range(r0, r0 + RC):
                    issue(dnext_ref, 1 - slot, t)
            acc = oa_ref[r0:r0 + RC, :]
            for k in range(TOP_K):
                acc = acc + w_t[r0:r0 + RC, k:k + 1] * _unpack_pairs(_rows_from_tiles(buf.at[slot], k * T + r0, RC))
            x1 = jnp.where(is_tail, x1b_ref[r0:r0 + RC, :], x1a_ref[r0:r0 + RC, :])
            g2 = jnp.where(is_tail, g2b_ref[r0:r0 + RC, :], g2a_ref[0])
            oa_ref[r0:r0 + RC, :] = x1 + g2 * _rms(acc, gain_ref[...])

        @pl.when(is_tail)
        def _():
            ob_ref[...] = oa_ref[...]

        @pl.when(j + 1 == pl.num_programs(0))
        def _():
            wait(1 - slot)

    @pl.when(j == 0)
    def _():
        lax.fori_loop(0, T, lambda t, c: (issue(dest_ref, 0, t), c)[1], 0, unroll=2)

    @pl.when(j % 2 == 0)
    def _():
        step(0)

    @pl.when(j % 2 == 1)
    def _():
        step(1)


def _combine_call(dest, w8, hp_a, hp_b, x1_a, x1_b, gate2_a, gate2_b, gain, wsg, wsu, wsd, ys):
    T = COMBINE_TILE
    tile_rows = T * ROW_PITCH
    n_full = hp_a.shape[0] // tile_rows
    n_tiles = n_full + 1
    seq = x1_a.shape[0] // gate2_a.shape[0]
    tile_of = lambda j: jnp.where(j == 0, n_full, j - 1)
    full_of = lambda j: jnp.maximum(j - 1, 0)
    const = lambda shp: pl.BlockSpec(shp, lambda j: (0, 0))
    return pl.pallas_call(
        _combine_body,
        out_shape=(jax.ShapeDtypeStruct((n_full * T, D_MODEL), F32), jax.ShapeDtypeStruct((T, D_MODEL), F32)),
        grid=(n_tiles,),
        in_specs=[pl.BlockSpec((TOP_K * T,), lambda j: (tile_of(j),), memory_space=pltpu.SMEM),
                  pl.BlockSpec((TOP_K * T,), lambda j: (tile_of(jnp.minimum(j + 1, n_tiles - 1)),),
                               memory_space=pltpu.SMEM),
                  pl.BlockSpec((TOP_K, T), lambda j: (0, tile_of(j))),
                  pl.BlockSpec((tile_rows, LANES), lambda j: (full_of(j), 0)),
                  pl.BlockSpec((tile_rows, LANES), lambda j: (0, 0)),
                  pl.BlockSpec((T, D_MODEL), lambda j: (full_of(j), 0)),
                  const((T, D_MODEL)),
                  pl.BlockSpec((1, 1, D_MODEL), lambda j: (full_of(j) * T // seq, 0, 0)),
                  const((T, D_MODEL)), const((1, D_MODEL)),
                  const((D_MODEL, D_EXPERT)), const((D_MODEL, D_EXPERT)), const((D_EXPERT, D_MODEL)),
                  pl.BlockSpec(memory_space=pl.ANY)],
        out_specs=(pl.BlockSpec((T, D_MODEL), lambda j: (full_of(j), 0)), const((T, D_MODEL))),
        scratch_shapes=[pltpu.VMEM((2, TOP_K * tile_rows, LANES), I32), pltpu.SemaphoreType.DMA((2,))],
        compiler_params=_cparams(("arbitrary",)),
        name="combine",
    )(dest, dest, w8, hp_a, hp_b, x1_a, x1_b, gate2_a, gate2_b, gain.reshape(1, -1), wsg, wsu, wsd, ys)


def _rope_tables(pos):
    half = HEAD_DIM_A // 2
    inv_freq = ROPE_THETA ** (-jnp.arange(half, dtype=F32) / half)
    ang = pos.astype(F32)[:, None] * inv_freq[None, :]
    cos = jnp.cos(ang)
    sin = jnp.sin(ang)
    reps = LANES // HEAD_DIM_A
    cos_t = jnp.tile(jnp.concatenate([cos, cos], axis=1), (1, reps))
    sin_t = jnp.tile(jnp.concatenate([-sin, sin], axis=1), (1, reps))
    return cos_t, sin_t


def kernel(x_prompt, x_sample, c_prompt, c_sample, cache_a1_kv, cache_a2_kv, cache_a3_kv, state_b_wkv, state_b_shift, w_ada, b_ada, norm_pre_mix, norm_post_mix, norm_pre_ffn, norm_post_ffn, w_in, w_a_out, mu_b, w0_b, w_w2_b, a0_b, w_a2_b, w_g2_b, k_k_b, k_a_b, r_k_b, ln_x_w_b, ln_x_b_b, w_b_out, w_out, w_router, router_bias, w_e_gate, w_e_up, w_e_down, w_s_gate, w_s_up, w_s_down):
    assert DEPTH == 1
    l = 0
    nd = DEC_BATCH
    row = lambda a: a.reshape(1, -1)
    p = {'mu_b': row(mu_b[l]), 'w0_b': row(w0_b[l]), 'w_w2_b': w_w2_b[l], 'a0_b': row(a0_b[l]),
         'w_a2_b': w_a2_b[l], 'w_g2_b': w_g2_b[l], 'k_k_b': row(k_k_b[l]), 'k_a_b': row(k_a_b[l]),
         'r_k_b': row(r_k_b[l]), 'ln_x_w_b': row(ln_x_w_b[l]), 'ln_x_b_b': row(ln_x_b_b[l]),
         'norm_post_mix': norm_post_mix[l], 'norm_pre_ffn': norm_pre_ffn[l]}

    wq, wf, wg = _wsplit_call(w_in[l])
    wa = w_a_out[l].astype(BF16)
    wb = w_b_out[l].astype(BF16)
    wo = w_out[l].astype(BF16)
    wrt = jnp.concatenate([w_router[l].T, jnp.zeros((LANES - N_EXPERTS, D_MODEL), F32)], axis=0)
    rb = router_bias[l].reshape(N_EXPERTS, 1)
    wsg, wsu, wsd = w_s_gate[l].astype(BF16), w_s_up[l].astype(BF16), w_s_down[l].astype(BF16)

    n_c = BATCH + nd
    c_all = jnp.concatenate([c_prompt, c_sample, jnp.zeros((-n_c % 8, D_MODEL), F32)], axis=0)
    mod = _mod_call(c_all, w_ada[l], b_ada[l])
    mod_p = [m.reshape(BATCH, 1, D_MODEL) for m in jnp.split(mod[:BATCH], 6, axis=-1)]
    mod_s = [m.reshape(1, nd, D_MODEL) for m in jnp.split(mod[BATCH:n_c], 6, axis=-1)]

    cos_p, sin_p = _rope_tables(jnp.arange(SEQ, dtype=I32))
    cos_s, sin_s = _rope_tables(jnp.full((nd,), PAST_LEN, I32))

    keep_p = [min(w, SEQ) for w, _ in DILATED_GROUPS]
    dils = tuple(d for _, d in DILATED_GROUPS)

    q0, q1, q2, feat_p, gates_p, *tails_p = _inproj_call(
        x_prompt, norm_pre_mix[l], mod_p[1], mod_p[0], cos_p, sin_p, wq, wf, wg,
        tm=256, keeps=keep_p, mod_per_row=False, dils=dils)
    o_parts, lse_parts = [], []
    for gi, qg in enumerate((q0, q1, q2)):
        o, lse = _attn_call(qg, gi)
        o_parts.append(o)
        lse_parts.append(lse)
    ob_p, wkv_p = _wkv_call(feat_p, p)
    x1_p, hp_p, wt_p = _post_call(o_parts, lse_parts, ob_p, gates_p, x_prompt, mod_p[2], mod_p[4], mod_p[3],
                                  p, wa, wb, wo, wrt, rb, tm=512, mod_per_row=False)

    xs3 = x_sample.reshape(1, nd, D_MODEL)
    s0, s1, s2, feat_s, gates_s, *tails_s = _inproj_call(
        xs3, norm_pre_mix[l], mod_s[1], mod_s[0], cos_s, sin_s, wq, wf, wg,
        tm=nd, keeps=(nd,) * N_GROUPS_A, mod_per_row=True, dils=(1, 1, 1))
    qkv_s = jnp.stack([z.reshape(nd, 3, N_HEADS_A, HEAD_DIM_A) for z in (s0, s1, s2)], axis=2)
    qkv_s = qkv_s.reshape(nd, 3 * N_GROUPS_A * N_HEADS_A, HEAD_DIM_A).astype(F32)
    oa_s = _sattn_call(qkv_s, cache_a1_kv[l], cache_a2_kv[l], cache_a3_kv[l])
    r_s, w_s, k_s, v_s, aa_s, bb_s, g_s = _swkv_prep_call(feat_s[0], state_b_shift[l], p)
    nh = nd * N_HEADS_B
    as_row = lambda a: a.reshape(nh, 1, HEAD_DIM_B)
    s_new, y_col = _swkv_step_call(state_b_wkv[l].reshape(nh, HEAD_DIM_B, HEAD_DIM_B), as_row(aa_s), as_row(w_s),
                                   as_row(bb_s), as_row(k_s), as_row(r_s), v_s.reshape(nh, HEAD_DIM_B))
    ob_s = _swkv_fin_call(y_col.reshape(nd, D_B), r_s, k_s, v_s, g_s, p)
    x1_s, hp_s, wt_s = _post_call([oa_s.reshape(1, nd, D_GROUP_A)], None, ob_s.reshape(1, nd, D_B), gates_s, xs3,
                                  mod_s[2], mod_s[4], mod_s[3], p, wa, wb, wo, wrt, rb, tm=nd, mod_per_row=True)

    n_p = BATCH * SEQ
    n_real = n_p + nd
    n_all = -(-n_real // MOE_TILE) * MOE_TILE
    pad = n_all - n_real
    n_blocks = -(-(n_real * TOP_K) // EXPERT_BLOCK) + N_EXPERTS
    n_blocks_pad = -(-n_blocks // LANES) * LANES
    assert n_p % MOE_TILE == 0 and nd <= MOE_TILE
    hp_a = hp_p.reshape(n_p * ROW_PITCH, LANES)
    hp_b = jnp.concatenate([hp_s[0], jnp.zeros((pad * ROW_PITCH, LANES), I32)], axis=0)
    wt_all = jnp.concatenate([wt_p, wt_s, jnp.full((N_EXPERTS, pad), -1.0, F32)], axis=1)
    dest8, w8, tab, etab = _rank_call(wt_all, n_real, n_blocks, n_blocks_pad)
    dest = dest8.reshape(TOP_K, n_all // MOE_TILE, MOE_TILE).transpose(1, 0, 2).reshape(-1)
    xs = _dispatch_call(dest, etab, hp_a, hp_b, n_real, n_blocks * EXPERT_BLOCK)
    ys = _ffn_call(tab[0], tab[1, :1], xs, w_e_gate[l], w_e_up[l], w_e_down[l], n_blocks)
    n_ct = n_p // COMBINE_TILE + 1
    dest_c = dest8[:, :n_ct * COMBINE_TILE].reshape(TOP_K, n_ct, COMBINE_TILE).transpose(1, 0, 2).reshape(-1)
    pad_rows = lambda z: jnp.concatenate([z, jnp.zeros((COMBINE_TILE - nd, D_MODEL), F32)], axis=0)
    out_p, out_s = _combine_call(dest_c, w8, hp_a, hp_b, x1_p.reshape(n_p, D_MODEL), pad_rows(x1_s[0]), mod_p[5],
                                 pad_rows(mod_s[5][0]), norm_post_ffn[l], wsg, wsu, wsd, ys)
    y_prompt = out_p.reshape(BATCH, SEQ, D_MODEL)
    y_sample = out_s[:nd]

    a_p = [z.reshape(1, BATCH, kp, 2, N_HEADS_A, HEAD_DIM_A) for z, kp in zip(tails_p, keep_p)]
    a_s = [z.reshape(1, nd, DEC_SEQ, 2, N_HEADS_A, HEAD_DIM_A) for z in tails_s]
    shift_p = feat_p[:, -1][None]
    shift_s = feat_s[0][None]
    return (y_prompt, y_sample.reshape(nd, DEC_SEQ, D_MODEL), a_p[0], a_p[1], a_p[2], wkv_p[None], shift_p,
            a_s[0], a_s[1], a_s[2], s_new.reshape(1, nd, N_HEADS_B, HEAD_DIM_B, HEAD_DIM_B), shift_s)
```

```python
import functools
import math

import jax
import jax.numpy as jnp
from jax import lax
from jax.experimental import pallas as pl
from jax.experimental.pallas import tpu as pltpu

F32 = jnp.float32
BF16 = jnp.bfloat16
I32 = jnp.int32

D_MODEL = 1024
BATCH = 2
SEQ = 8192
DEPTH = 1
DEC_BATCH = 32
DEC_SEQ = 1
PAST_LEN = 16384

HEAD_DIM_A = 64
N_HEADS_A = 8
DILATED_GROUPS = ((128, 1), (512, 4), (2048, 16))
N_GROUPS_A = 3
D_GROUP_A = N_HEADS_A * HEAD_DIM_A
D_A = N_GROUPS_A * D_GROUP_A
D_QKV = 3 * D_A
BAND_BLOCK = 128
ROPE_THETA = 10000.0

HEAD_DIM_B = 64
N_HEADS_B = 16
D_B = 1024
DECAY_LORA = 64
AAA_LORA = 64
GATE_LORA = 160
D_SHIFT_B = 3 * D_B + DECAY_LORA + AAA_LORA + GATE_LORA
LN_X_EPS = 64e-5

N_EXPERTS = 64
TOP_K = 8
N_EXPERT_GROUPS = 8
TOPK_GROUPS = 4
D_EXPERT = 256
ROUTED_SCALE = 2.5
EXPERT_BLOCK = 512
NORM_EPS = 1e-6

LANES = 128
WKV_CHUNK = 64
MOE_TILE = 1024
COMBINE_TILE = 256
COMBINE_ROWS = 32
VMEM_LIMIT = 56 * 1024 * 1024
ROW_TILE_SUBLANES = D_MODEL // (2 * LANES)
ZERO_ROWS = 256


def _cparams(sem):
    return pltpu.CompilerParams(dimension_semantics=sem, vmem_limit_bytes=VMEM_LIMIT)


def _dot(a, b):
    return jnp.dot(a, b, preferred_element_type=F32)


def _dot_nt(a, b):
    return lax.dot_general(a, b, (((1,), (1,)), ((), ())), preferred_element_type=F32)


def _dot_tn(a, b):
    return lax.dot_general(a, b, (((0,), (0,)), ((), ())), preferred_element_type=F32)


def _dot_nt_split(a, b):
    ah = a.astype(BF16)
    al = (a - ah.astype(F32)).astype(BF16)
    bh = b.astype(BF16)
    bl = (b - bh.astype(F32)).astype(BF16)
    return _dot_nt(ah, bh) + _dot_nt(ah, bl) + _dot_nt(al, bh)


def _dot_exact(a, b):
    return lax.dot_general(a, b, (((1,), (0,)), ((), ())), precision=lax.Precision.HIGHEST,
                           preferred_element_type=F32)


def _rms(x, gain):
    return x * lax.rsqrt(jnp.mean(x * x, axis=-1, keepdims=True) + NORM_EPS) * gain


def _sigmoid(x):
    return 1.0 / (1.0 + jnp.exp(-x))


def _silu(x):
    return x * _sigmoid(x)


def _softplus(x):
    return jnp.maximum(x, 0.0) + jnp.log(1.0 + jnp.exp(-jnp.abs(x)))


def _pack_pairs(x):
    half = D_MODEL // 2
    lo = lax.bitcast_convert_type(x[:, :half].astype(BF16).astype(F32), I32)
    hi = lax.bitcast_convert_type(x[:, half:].astype(BF16).astype(F32), I32)
    return lax.shift_right_logical(lo, 16) | (hi & jnp.int32(-65536))


def _unpack_pairs(w):
    lo = lax.bitcast_convert_type(w << 16, F32)
    hi = lax.bitcast_convert_type(w & jnp.int32(-65536), F32)
    return jnp.concatenate([lo, hi], axis=1)


def _mod_body(c_ref, w_ref, b_ref, o_ref):
    s = _silu(c_ref[...]).astype(BF16)
    o_ref[...] = _dot(s, w_ref[...].astype(BF16)) + b_ref[...]


def _mod_call(c_all, w_ada, b_ada):
    rows = c_all.shape[0]
    tn = 1536
    return pl.pallas_call(
        _mod_body,
        out_shape=jax.ShapeDtypeStruct((rows, 6 * D_MODEL), F32),
        grid=(6 * D_MODEL // tn,),
        in_specs=[pl.BlockSpec((rows, D_MODEL), lambda j: (0, 0)),
                  pl.BlockSpec((D_MODEL, tn), lambda j: (0, j)),
                  pl.BlockSpec((1, tn), lambda j: (0, j))],
        out_specs=pl.BlockSpec((rows, tn), lambda j: (0, j)),
        compiler_params=_cparams(("arbitrary",)),
        name="mod",
    )(c_all, w_ada, b_ada.reshape(1, -1))


def _wsplit_body(w_ref, q_ref, f_ref, g_ref):
    w = w_ref[...]
    q_ref[...] = w[:, :D_QKV].astype(BF16)
    f_ref[...] = w[:, D_QKV:D_QKV + D_SHIFT_B].astype(BF16)
    g_ref[...] = w[:, D_QKV + D_SHIFT_B:].astype(BF16)


def _wsplit_call(w):
    rows, cols = w.shape
    tr = 128
    widths = (D_QKV, D_SHIFT_B, cols - D_QKV - D_SHIFT_B)
    return pl.pallas_call(
        _wsplit_body,
        out_shape=tuple(jax.ShapeDtypeStruct((rows, n), BF16) for n in widths),
        grid=(rows // tr,),
        in_specs=[pl.BlockSpec((tr, cols), lambda i: (i, 0))],
        out_specs=tuple(pl.BlockSpec((tr, n), lambda i: (i, 0)) for n in widths),
        compiler_params=_cparams(("arbitrary",)),
        name="wsplit",
    )(w)


def _inproj_body(x_ref, g_ref, sc_ref, sh_ref, cos_ref, sin_ref, wq_ref, wf_ref, wg_ref,
                 q0_ref, q1_ref, q2_ref, feat_ref, gate_ref, t0_ref, t1_ref, t2_ref, p_ref, *, dils):
    x = x_ref[0]
    tm = x.shape[0]
    h = _rms(x, g_ref[...]) * (1.0 + sc_ref[0]) + sh_ref[0]
    hb = h.astype(BF16)
    p = _dot(hb, wq_ref[...])
    cos = cos_ref[...]
    sin = sin_ref[...]
    lane = lax.broadcasted_iota(I32, cos.shape, 1)
    first_half = (lane % HEAD_DIM_A) < (HEAD_DIM_A // 2)
    for c in range(2 * D_A // LANES):
        xc = p[:, c * LANES:(c + 1) * LANES]
        partner = jnp.where(first_half, pltpu.roll(xc, LANES - HEAD_DIM_A // 2, 1),
                            pltpu.roll(xc, HEAD_DIM_A // 2, 1))
        rc = xc * cos + partner * sin
        if c < D_A // LANES:
            rc = rc * (HEAD_DIM_A ** -0.5)
        p_ref[c] = rc
    for c in range(2 * D_A // LANES, D_QKV // LANES):
        p_ref[c] = p[:, c * LANES:(c + 1) * LANES]
    per_group = D_GROUP_A // LANES
    for gi, t_ref in enumerate((t0_ref, t1_ref, t2_ref)):
        rows = t_ref.shape[1]
        for which in (1, 2):
            for j in range(per_group):
                c = (which * D_A + gi * D_GROUP_A) // LANES + j
                t_ref[0, :, (which - 1) * D_GROUP_A + j * LANES:(which - 1) * D_GROUP_A + (j + 1) * LANES] = \
                    p_ref[c, tm - rows:tm, :]
    for gi, (out_ref, dil) in enumerate(zip((q0_ref, q1_ref, q2_ref), dils)):
        for which in range(3):
            for j in range(per_group):
                c = (which * D_A + gi * D_GROUP_A) // LANES + j
                dst = slice(which * D_GROUP_A + j * LANES, which * D_GROUP_A + (j + 1) * LANES)
                if dil == 1:
                    out_ref[0, 0, :, dst] = p_ref[c].astype(BF16)
                else:
                    for r in range(dil):
                        out_ref[0, r, :, dst] = p_ref[c, pl.ds(r, tm // dil, stride=dil), :].astype(BF16)
    feat_ref[0] = _dot(hb, wf_ref[...])
    gate_ref[0] = _sigmoid(_dot(hb, wg_ref[...])).astype(BF16)


def _inproj_call(x, gain, scale, shift, cos_t, sin_t, wq, wf, wg, tm, keeps, mod_per_row, dils):
    nb, t, _ = x.shape
    nt = t // tm

    def tail_spec(keep):
        if keep <= tm:
            return pl.BlockSpec((1, keep, 2 * D_GROUP_A), lambda b, i: (b, 0, 0))
        first = (t - keep) // tm
        return pl.BlockSpec((1, tm, 2 * D_GROUP_A), lambda b, i: (b, jnp.maximum(i - first, 0), 0))

    if mod_per_row:
        mod_spec = pl.BlockSpec((1, tm, D_MODEL), lambda b, i: (b, i, 0))
    else:
        mod_spec = pl.BlockSpec((1, 1, D_MODEL), lambda b, i: (b, 0, 0))
    resident = lambda shp: pl.BlockSpec(shp, lambda b, i: (0, 0), pipeline_mode=pl.Buffered(1))
    q_shapes = tuple(jax.ShapeDtypeStruct((nb, d, t // d, 3 * D_GROUP_A), BF16) for d in dils)
    q_specs = tuple(pl.BlockSpec((1, d, tm // d, 3 * D_GROUP_A), lambda b, i: (b, 0, i, 0)) for d in dils)
    return pl.pallas_call(
        functools.partial(_inproj_body, dils=dils),
        out_shape=q_shapes + (jax.ShapeDtypeStruct((nb, t, D_SHIFT_B), F32),
                              jax.ShapeDtypeStruct((nb, t, 2 * D_MODEL), BF16),
                              ) + tuple(jax.ShapeDtypeStruct((nb, kp, 2 * D_GROUP_A), F32) for kp in keeps),
        grid=(nb, nt),
        in_specs=[pl.BlockSpec((1, tm, D_MODEL), lambda b, i: (b, i, 0)),
                  pl.BlockSpec((1, D_MODEL), lambda b, i: (0, 0)),
                  mod_spec, mod_spec,
                  pl.BlockSpec((tm, LANES), lambda b, i: (i, 0)),
                  pl.BlockSpec((tm, LANES), lambda b, i: (i, 0)),
                  resident((D_MODEL, D_QKV)), resident((D_MODEL, D_SHIFT_B)),
                  resident((D_MODEL, 2 * D_MODEL))],
        out_specs=q_specs + (pl.BlockSpec((1, tm, D_SHIFT_B), lambda b, i: (b, i, 0)),
                             pl.BlockSpec((1, tm, 2 * D_MODEL), lambda b, i: (b, i, 0)),
                             ) + tuple(tail_spec(kp) for kp in keeps),
        scratch_shapes=[pltpu.VMEM((D_QKV // LANES, tm, LANES), F32)],
        compiler_params=_cparams(("arbitrary", "arbitrary")),
        name="inproj",
    )(x, gain.reshape(1, -1), scale, shift, cos_t, sin_t, wq, wf, wg)


def _attn_body(q_ref, kc_ref, kp_ref, vc_ref, vp_ref, o_ref, lse_ref):
    mb = pl.program_id(2)
    nq = q_ref.shape[2] // BAND_BLOCK
    q = q_ref[0, 0]
    k = jnp.concatenate([kp_ref[0, 0], kc_ref[0, 0]], axis=0)
    v = jnp.concatenate([vp_ref[0, 0], vc_ref[0, 0]], axis=0)
    qi = lax.broadcasted_iota(I32, (BAND_BLOCK, 2 * BAND_BLOCK), 0)
    ki = lax.broadcasted_iota(I32, (BAND_BLOCK, 2 * BAND_BLOCK), 1)
    dist = qi + BAND_BLOCK - ki
    band = (dist >= 0) & (dist <= BAND_BLOCK)
    masks = [band & ((ki >= BAND_BLOCK) | (mb > 0))] + [band] * (nq - 1)
    lane_q = lax.broadcasted_iota(I32, (BAND_BLOCK, LANES), 1)
    lane_k = lax.broadcasted_iota(I32, (2 * BAND_BLOCK, LANES), 1)
    for hp in range(N_HEADS_A // 2):
        sl = slice(hp * LANES, (hp + 1) * LANES)
        chains = [(j, sub) for j in range(nq) for sub in range(2)]
        qs = [q[j * BAND_BLOCK:(j + 1) * BAND_BLOCK, sl] for j in range(nq)]
        ks = [k[j * BAND_BLOCK:(j + 2) * BAND_BLOCK, sl] for j in range(nq)]
        vs = [v[j * BAND_BLOCK:(j + 2) * BAND_BLOCK, sl] for j in range(nq)]
        mqs = [lane_q < HEAD_DIM_A, lane_q >= HEAD_DIM_A]
        mks = [lane_k < HEAD_DIM_A, lane_k >= HEAD_DIM_A]
        s = [jnp.where(masks[j], _dot_nt(jnp.where(mqs[sub], qs[j], jnp.zeros_like(qs[j])), ks[j]), -jnp.inf)
             for j, sub in chains]
        mx = [jnp.max(z, axis=1, keepdims=True) for z in s]
        p = [jnp.exp(z - m) for z, m in zip(s, mx)]
        l = [jnp.sum(z, axis=1, keepdims=True) for z in p]
        pv = [_dot(p[c].astype(BF16), jnp.where(mks[sub], vs[j], jnp.zeros_like(vs[j])))
              for c, (j, sub) in enumerate(chains)]
        for j in range(nq):
            c0, c1 = 2 * j, 2 * j + 1
            o_pair = pv[c0] / l[c0] + pv[c1] / l[c1]
            lse_pair = jnp.where(mqs[0], mx[c0] + jnp.log(l[c0]), mx[c1] + jnp.log(l[c1]))
            o_ref[0, 0, j * BAND_BLOCK:(j + 1) * BAND_BLOCK, sl] = o_pair.astype(BF16)
            lse_ref[0, 0, j * BAND_BLOCK:(j + 1) * BAND_BLOCK, sl] = lse_pair


def _attn_call(qkv_g, gi):
    b, dil, l, _ = qkv_g.shape
    nq = 4
    nb = l // (nq * BAND_BLOCK)
    blk = (1, 1, nq * BAND_BLOCK, D_GROUP_A)
    cur = lambda which: pl.BlockSpec(blk, lambda bb, r, m: (bb, r, m, which))
    prev = lambda which: pl.BlockSpec((1, 1, BAND_BLOCK, D_GROUP_A),
                                      lambda bb, r, m: (bb, r, jnp.maximum(nq * m - 1, 0), which))
    return pl.pallas_call(
        _attn_body,
        out_shape=(jax.ShapeDtypeStruct((b, dil, l, D_GROUP_A), BF16),
                   jax.ShapeDtypeStruct((b, dil, l, D_GROUP_A), F32)),
        grid=(b, dil, nb),
        in_specs=[cur(0), cur(1), prev(1), cur(2), prev(2)],
        out_specs=(pl.BlockSpec(blk, lambda bb, r, m: (bb, r, m, 0)),
                   pl.BlockSpec(blk, lambda bb, r, m: (bb, r, m, 0))),
        compiler_params=_cparams(("arbitrary", "arbitrary", "arbitrary")),
        name=f"attn{gi}",
    )(qkv_g, qkv_g, qkv_g, qkv_g, qkv_g)


def _sattn_body(qkv_ref, b1_ref, b2_ref, b3_ref, o_ref):
    n_rows = 3 * N_GROUPS_A * N_HEADS_A
    sq = jnp.concatenate([qkv_ref[0], jnp.zeros((LANES - n_rows, HEAD_DIM_A), F32)], axis=0)
    cols = jnp.concatenate([sq, jnp.zeros((LANES, LANES - HEAD_DIM_A), F32)], axis=1).T
    col3 = lambda first: jnp.stack([cols[:HEAD_DIM_A, first + h:first + h + 1] for h in range(N_HEADS_A)], axis=0)
    outs, lses = [], []
    for g, (buf_ref, (_, dil)) in enumerate(zip((b1_ref, b2_ref, b3_ref), DILATED_GROUPS)):
        q = col3(g * N_HEADS_A)
        kn = col3((N_GROUPS_A + g) * N_HEADS_A)
        vn = col3((2 * N_GROUPS_A + g) * N_HEADS_A)
        kb = buf_ref[0, 0]
        vb = buf_ref[0, 1]
        wb = kb.shape[-1]
        pos = lax.broadcasted_iota(I32, (1, 1, wb), 2)
        s = jnp.sum(kb * q, axis=1, keepdims=True)
        s = jnp.where(pos % dil == 0, s, -jnp.inf)
        sn = jnp.sum(kn * q, axis=1, keepdims=True)
        m = jnp.maximum(jnp.max(s, axis=2, keepdims=True), sn)
        p = jnp.exp(s - m)
        pn = jnp.exp(sn - m)
        l = jnp.sum(p, axis=2, keepdims=True) + pn
        outs.append((jnp.sum(p * vb, axis=2, keepdims=True) + pn * vn) / l)
        lses.append(m + jnp.log(l))
    mx = jnp.maximum(jnp.maximum(lses[0], lses[1]), lses[2])
    es = [jnp.exp(z - mx) for z in lses]
    o_a = (es[0] * outs[0] + es[1] * outs[1] + es[2] * outs[2]) / (es[0] + es[1] + es[2])
    o_cols = jnp.concatenate([o_a[h] for h in range(N_HEADS_A)] +
                             [jnp.zeros((HEAD_DIM_A, LANES - N_HEADS_A), F32)], axis=1)
    o_rows = jnp.concatenate([o_cols, jnp.zeros((LANES - HEAD_DIM_A, LANES), F32)], axis=0).T
    o_ref[0] = o_rows[:N_HEADS_A, :HEAD_DIM_A]


def _sattn_call(qkv_s, c1, c2, c3):
    n = qkv_s.shape[0]
    views, specs = [], []
    for c in (c1, c2, c3):
        wb = c.shape[1]
        views.append(jnp.transpose(c, (0, 2, 3, 4, 1)))
        specs.append(pl.BlockSpec((1, 2, N_HEADS_A, HEAD_DIM_A, wb), lambda b: (b, 0, 0, 0, 0)))
    return pl.pallas_call(
        _sattn_body,
        out_shape=jax.ShapeDtypeStruct((n, N_HEADS_A, HEAD_DIM_A), F32),
        grid=(n,),
        in_specs=[pl.BlockSpec((1, 3 * N_GROUPS_A * N_HEADS_A, HEAD_DIM_A), lambda b: (b, 0, 0))] + specs,
        out_specs=pl.BlockSpec((1, N_HEADS_A, HEAD_DIM_A), lambda b: (b, 0, 0)),
        compiler_params=_cparams(("arbitrary",)),
        name="sattn",
    )(qkv_s, *views)


def _rwkv_features(xs, w0, ww2, a0, wa2, wg2, k_a):
    r = xs[:, :D_B]
    k = xs[:, D_B:2 * D_B]
    v = xs[:, 2 * D_B:3 * D_B]
    xw = xs[:, 3 * D_B:3 * D_B + DECAY_LORA]
    xa = xs[:, 3 * D_B + DECAY_LORA:3 * D_B + DECAY_LORA + AAA_LORA]
    xg = xs[:, 3 * D_B + DECAY_LORA + AAA_LORA:]
    w_log = -_softplus(-(w0 + _dot(jnp.tanh(xw).astype(BF16), ww2.astype(BF16)))) - 0.5
    a = _sigmoid(a0 + _dot(xa.astype(BF16), wa2.astype(BF16)))
    g = _dot(_sigmoid(xg).astype(BF16), wg2.astype(BF16))
    k_h = k * (1.0 + (a - 1.0) * k_a)
    return r, k, v, w_log, a, g, k_h


def _head_norm(kk_h):
    nrm = jnp.sqrt(jnp.sum(kk_h * kk_h, axis=-1, keepdims=True))
    return kk_h / jnp.maximum(nrm, 1e-12)


def _wkv_finish_head(y, r_h, k_h, v_h, g_h, rk_h, lnw_h, lnb_h):
    mean = jnp.mean(y, axis=-1, keepdims=True)
    var = jnp.mean(jnp.square(y - mean), axis=-1, keepdims=True)
    yn = (y - mean) * lax.rsqrt(var + LN_X_EPS) * lnw_h + lnb_h
    bonus = jnp.sum(r_h * k_h * rk_h, axis=-1, keepdims=True) * v_h
    return (yn + bonus) * g_h


def _wkv_body(f_ref, fp_ref, mu_ref, w0_ref, ww2_ref, a0_ref, wa2_ref, wg2_ref, kk_ref, ka_ref,
              rk_ref, lnw_ref, lnb_ref, o_ref, st_ref, s_ref):
    c = pl.program_id(0)
    C = WKV_CHUNK
    nb = f_ref.shape[0]

    @pl.when(c == 0)
    def _():
        s_ref[...] = jnp.zeros_like(s_ref)

    f = jnp.concatenate([f_ref[b] for b in range(nb)], axis=0)
    row = lax.broadcasted_iota(I32, f.shape, 0)
    prev = pltpu.roll(f, 1, 0)
    for b in range(nb):
        prev = jnp.where(row == b * C, jnp.where(c == 0, 0.0, fp_ref[b][7:8, :]), prev)
    xs = f + mu_ref[...] * (prev - f)
    r, k, v, w_log, a, g, k_h = _rwkv_features(xs, w0_ref[...], ww2_ref[...], a0_ref[...],
                                               wa2_ref[...], wg2_ref[...], ka_ref[...])
    lw = -jnp.exp(w_log)
    kk = k * kk_ref[...]
    jh = lax.broadcasted_iota(I32, (D_B, LANES), 0) // HEAD_DIM_B
    ind = (jh == lax.broadcasted_iota(I32, (D_B, LANES), 1)).astype(BF16)
    ind_t = (lax.broadcasted_iota(I32, (LANES, D_B), 0)
             == lax.broadcasted_iota(I32, (LANES, D_B), 1) // HEAD_DIM_B).astype(BF16)

    def head_sum(z):
        hi = z.astype(BF16)
        lo = (z - hi.astype(F32)).astype(BF16)
        s = _dot(hi, ind) + _dot(lo, ind)
        shi = s.astype(BF16)
        slo = (s - shi.astype(F32)).astype(BF16)
        return _dot(shi, ind_t) + _dot(slo, ind_t)

    kkn = kk / jnp.maximum(jnp.sqrt(head_sum(kk * kk)), 1e-12)

    tr = lax.broadcasted_iota(I32, (nb * C, nb * C), 0)
    sr_ = lax.broadcasted_iota(I32, (nb * C, nb * C), 1)
    tri_incl = ((tr >= sr_) & (tr // C == sr_ // C)).astype(BF16)
    l1 = lw.astype(BF16)
    r1 = lw - l1.astype(F32)
    l2 = r1.astype(BF16)
    l3 = (r1 - l2.astype(F32)).astype(BF16)
    cum = _dot(tri_incl, l1) + _dot(tri_incl, l2) + _dot(tri_incl, l3)
    rhos = [cum[b * C + C // 2 - 1:b * C + C // 2, :] for b in range(nb)]
    rho = jnp.concatenate([jnp.broadcast_to(z, (C, D_B)) for z in rhos], axis=0)
    ep = jnp.exp(cum - rho)
    em = jnp.exp(rho - cum)
    e_a = ep * jnp.exp(-lw)
    r_hat = r * ep
    k_hat = k_h * em
    e_rs = [jnp.exp(z) for z in rhos]
    e_cs = [jnp.exp(cum[b * C + C - 1:b * C + C, :] - rhos[b]) for b in range(nb)]

    ti = lax.broadcasted_iota(I32, (C, C), 0)
    si = lax.broadcasted_iota(I32, (C, C), 1)
    strict = ti > si
    incl = ti >= si
    rk = rk_ref[...]
    lnw = lnw_ref[...]
    lnb = lnb_ref[...]
    items = [(b, h) for b in range(nb) for h in range(N_HEADS_B)]
    heads = range(len(items))
    lanes = [slice(h * HEAD_DIM_B, (h + 1) * HEAD_DIM_B) for _, h in items]
    cut = lambda z, i: z[items[i][0] * C:(items[i][0] + 1) * C, lanes[i]]
    e_r = [e_rs[b][:, lanes[i]] for i, (b, _) in enumerate(items)]
    e_c = [e_cs[b][:, lanes[i]] for i, (b, _) in enumerate(items)]
    a_hat_full = (-kkn * e_a).astype(BF16)
    b_hat_full = (kkn * a * em).astype(BF16)
    a_hat_b = [cut(a_hat_full, h) for h in heads]
    b_hat_b = [cut(b_hat_full, h) for h in heads]
    rh = [cut(r_hat, h) for h in heads]
    vb = [cut(v, h).astype(BF16) for h in heads]
    bk = [jnp.concatenate([b_hat_b[h], cut(k_hat, h).astype(BF16)], axis=0) for h in heads]
    p = [_dot_nt(jnp.concatenate([a_hat_b[h], rh[h].astype(BF16)], axis=0), bk[h]) for h in heads]
    l_ak = [jnp.where(strict, z[:C, C:], 0.0).astype(BF16) for z in p]
    p_rb = [jnp.where(incl, z[C:, :C], 0.0).astype(BF16) for z in p]
    p_rk = [jnp.where(incl, z[C:, C:], 0.0).astype(BF16) for z in p]
    col = lax.broadcasted_iota(I32, (C, 2 * C), 1)
    row2 = lax.broadcasted_iota(I32, (C, 2 * C), 0)
    left = col < C
    zt = [jnp.where(left, jnp.where(row2 > col, z[:C], 0.0), (col == row2 + C).astype(F32)) for z in p]
    for _ in range(int(math.log2(C))):
        zb = [z.astype(BF16) for z in zt]
        res = [_dot(z[:, :C], z) for z in zb]
        zt = [jnp.where(left, res[h], zt[h] + res[h]) for h in heads]
    tb = [z.astype(BF16) for z in zt]
    zeros_c = jnp.zeros((C, HEAD_DIM_B), BF16)
    lv = [_dot(l_ak[h], vb[h]).astype(BF16) for h in heads]
    a_bar = [_dot(tb[h], jnp.concatenate([zeros_c, a_hat_b[h]], axis=0)).astype(BF16) for h in heads]
    u_v = [_dot(tb[h], jnp.concatenate([zeros_c, lv[h]], axis=0)).astype(BF16) for h in heads]
    r_bar = [rh[h] + _dot(p_rb[h], a_bar[h]) for h in heads]
    y_v = [_dot(p_rb[h], u_v[h]) + _dot(p_rk[h], vb[h]) for h in heads]
    ab = [_dot_tn(a_bar[h], b_hat_b[h]).astype(BF16) for h in heads]
    n_t = [_dot_tn(jnp.concatenate([u_v[h], vb[h]], axis=0), bk[h]) for h in heads]
    s0 = [s_ref[b, h] for b, h in items]
    sr = [s0[h] * e_r[h] for h in heads]
    y = [_dot_nt((r_bar[h] * e_r[h]).astype(BF16), s0[h].astype(BF16)) + y_v[h] for h in heads]
    s_new = [(sr[h] + _dot(sr[h].astype(BF16), ab[h]) + n_t[h]) * e_c[h] for h in heads]
    for i, (b, h) in enumerate(items):
        s_ref[b, h] = s_new[i]
    y_full = jnp.concatenate([jnp.concatenate(y[b * N_HEADS_B:(b + 1) * N_HEADS_B], axis=1) for b in range(nb)],
                             axis=0)
    inv_hd = 1.0 / HEAD_DIM_B
    dev = y_full - head_sum(y_full) * inv_hd
    yn = dev * lax.rsqrt(head_sum(dev * dev) * inv_hd + LN_X_EPS) * lnw + lnb
    out = (yn + head_sum(r * k_h * rk) * v) * g
    for b in range(nb):
        o_ref[b] = out[b * C:(b + 1) * C, :]

    @pl.when(c == pl.num_programs(0) - 1)
    def _():
        st_ref[...] = s_ref[...]


def _wkv_call(feat, p):
    b, t, _ = feat.shape
    C = WKV_CHUNK
    nc = t // C
    row = lambda n: pl.BlockSpec((1, n), lambda c: (0, 0))
    mat = lambda m, n: pl.BlockSpec((m, n), lambda c: (0, 0))
    return pl.pallas_call(
        _wkv_body,
        out_shape=(jax.ShapeDtypeStruct((b, t, D_B), F32),
                   jax.ShapeDtypeStruct((b, N_HEADS_B, HEAD_DIM_B, HEAD_DIM_B), F32)),
        grid=(nc,),
        in_specs=[pl.BlockSpec((b, C, D_SHIFT_B), lambda c: (0, c, 0)),
                  pl.BlockSpec((b, 8, D_SHIFT_B), lambda c: (0, jnp.maximum(c * (C // 8) - 1, 0), 0)),
                  row(D_SHIFT_B), row(D_B), mat(DECAY_LORA, D_B), row(D_B), mat(AAA_LORA, D_B),
                  mat(GATE_LORA, D_B), row(D_B), row(D_B), row(D_B), row(D_B), row(D_B)],
        out_specs=(pl.BlockSpec((b, C, D_B), lambda c: (0, c, 0)),
                   pl.BlockSpec((b, N_HEADS_B, HEAD_DIM_B, HEAD_DIM_B), lambda c: (0, 0, 0, 0))),
        scratch_shapes=[pltpu.VMEM((b, N_HEADS_B, HEAD_DIM_B, HEAD_DIM_B), F32)],
        compiler_params=_cparams(("arbitrary",)),
        name="wkv",
    )(feat, feat, p['mu_b'], p['w0_b'], p['w_w2_b'], p['a0_b'], p['w_a2_b'], p['w_g2_b'],
      p['k_k_b'], p['k_a_b'], p['r_k_b'], p['ln_x_w_b'], p['ln_x_b_b'])


def _swkv_prep_body(f_ref, sh_ref, mu_ref, w0_ref, ww2_ref, a0_ref, wa2_ref, wg2_ref, kk_ref, ka_ref,
                    r_ref, w_ref, k_ref, v_ref, aa_ref, bb_ref, g_ref):
    f = f_ref[...]
    xs = f + mu_ref[...] * (sh_ref[...] - f)
    r, k, v, w_log, a, g, k_h = _rwkv_features(xs, w0_ref[...], ww2_ref[...], a0_ref[...],
                                               wa2_ref[...], wg2_ref[...], ka_ref[...])
    kk = k * kk_ref[...]
    kkn = jnp.concatenate([_head_norm(kk[:, h * HEAD_DIM_B:(h + 1) * HEAD_DIM_B]) for h in range(N_HEADS_B)],
                          axis=1)
    r_ref[...] = r
    w_ref[...] = jnp.exp(-jnp.exp(w_log))
    k_ref[...] = k_h
    v_ref[...] = v
    aa_ref[...] = -kkn
    bb_ref[...] = kkn * a
    g_ref[...] = g


def _swkv_prep_call(feat_s, shift0, p):
    n = feat_s.shape[0]
    full = lambda a: pl.BlockSpec(a.shape, lambda: tuple(0 for _ in a.shape))
    args = (feat_s, shift0, p['mu_b'], p['w0_b'], p['w_w2_b'], p['a0_b'], p['w_a2_b'], p['w_g2_b'],
            p['k_k_b'], p['k_a_b'])
    return pl.pallas_call(
        _swkv_prep_body,
        out_shape=tuple(jax.ShapeDtypeStruct((n, D_B), F32) for _ in range(7)),
        in_specs=[full(a) for a in args],
        out_specs=tuple(pl.BlockSpec((n, D_B), lambda: (0, 0)) for _ in range(7)),
        compiler_params=pltpu.CompilerParams(vmem_limit_bytes=VMEM_LIMIT),
        name="swkv_prep",
    )(*args)


def _swkv_step_body(s_ref, a_ref, w_ref, b_ref, k_ref, r_ref, v_ref, so_ref, y_ref):
    s = s_ref[...]
    th = s.shape[0]
    pad_sq = lambda z: jnp.concatenate(
        [jnp.concatenate([z, jnp.zeros((z.shape[0], LANES - z.shape[1]), F32)], axis=1),
         jnp.zeros((LANES - z.shape[0], LANES), F32)], axis=0)
    v_t = pad_sq(v_ref[...]).T
    v_col = jnp.stack([v_t[:HEAD_DIM_B, j:j + 1] for j in range(th)], axis=0)
    sa = jnp.sum(s * a_ref[...], axis=-1, keepdims=True)
    s2 = s * w_ref[...] + sa * b_ref[...] + v_col * k_ref[...]
    so_ref[...] = s2
    y = jnp.sum(s2 * r_ref[...], axis=-1, keepdims=True)
    y_t = jnp.concatenate([y[j] for j in range(th)], axis=1)
    y_ref[...] = pad_sq(y_t).T[:th, :HEAD_DIM_B]


def _swkv_step_call(s0, aa, w, bb, k, r, v):
    nh = s0.shape[0]
    th = 64
    rowspec = pl.BlockSpec((th, 1, HEAD_DIM_B), lambda i: (i, 0, 0))
    matspec = pl.BlockSpec((th, HEAD_DIM_B), lambda i: (i, 0))
    stspec = pl.BlockSpec((th, HEAD_DIM_B, HEAD_DIM_B), lambda i: (i, 0, 0))
    return pl.pallas_call(
        _swkv_step_body,
        out_shape=(jax.ShapeDtypeStruct((nh, HEAD_DIM_B, HEAD_DIM_B), F32),
                   jax.ShapeDtypeStruct((nh, HEAD_DIM_B), F32)),
        grid=(nh // th,),
        in_specs=[stspec, rowspec, rowspec, rowspec, rowspec, rowspec, matspec],
        out_specs=(stspec, matspec),
        compiler_params=_cparams(("arbitrary",)),
        name="swkv_step",
    )(s0, aa, w, bb, k, r, v)


def _swkv_fin_body(y_ref, r_ref, k_ref, v_ref, g_ref, rk_ref, lnw_ref, lnb_ref, o_ref):
    y, r, k, v, g = y_ref[...], r_ref[...], k_ref[...], v_ref[...], g_ref[...]
    rk, lnw, lnb = rk_ref[...], lnw_ref[...], lnb_ref[...]
    outs = []
    for h in range(N_HEADS_B):
        sl = slice(h * HEAD_DIM_B, (h + 1) * HEAD_DIM_B)
        outs.append(_wkv_finish_head(y[:, sl], r[:, sl], k[:, sl], v[:, sl], g[:, sl],
                                     rk[:, sl], lnw[:, sl], lnb[:, sl]))
    o_ref[...] = jnp.concatenate(outs, axis=1)


def _swkv_fin_call(y, r, k, v, g, p):
    n = y.shape[0]
    args = (y, r, k, v, g, p['r_k_b'], p['ln_x_w_b'], p['ln_x_b_b'])
    full = lambda a: pl.BlockSpec(a.shape, lambda: (0, 0))
    return pl.pallas_call(
        _swkv_fin_body,
        out_shape=jax.ShapeDtypeStruct((n, D_B), F32),
        in_specs=[full(a) for a in args],
        out_specs=pl.BlockSpec((n, D_B), lambda: (0, 0)),
        name="swkv_fin",
    )(*args)


def _route_t(scores, bias_col):
    n = scores.shape[1]
    gsz = N_EXPERTS // N_EXPERT_GROUPS
    choice = scores + bias_col
    ninf = -jnp.inf
    sid = lax.broadcasted_iota(I32, (gsz, n), 0)
    gs = []
    for gidx in range(N_EXPERT_GROUPS):
        blk = choice[gidx * gsz:(gidx + 1) * gsz, :]
        m1 = jnp.max(blk, axis=0, keepdims=True)
        first = jnp.min(jnp.where(blk == m1, sid, gsz), axis=0, keepdims=True)
        m2 = jnp.max(jnp.where(sid == first, ninf, blk), axis=0, keepdims=True)
        gs.append(m1 + m2)
    cur = jnp.concatenate(gs, axis=0)
    gid = lax.broadcasted_iota(I32, (N_EXPERT_GROUPS, n), 0)
    gmask = jnp.zeros((N_EXPERT_GROUPS, n), F32)
    for _ in range(TOPK_GROUPS):
        m = jnp.max(cur, axis=0, keepdims=True)
        first = jnp.min(jnp.where(cur == m, gid, N_EXPERT_GROUPS), axis=0, keepdims=True)
        sel = gid == first
        gmask = jnp.where(sel, 1.0, gmask)
        cur = jnp.where(sel, ninf, cur)
    emask = jnp.concatenate([jnp.broadcast_to(gmask[gidx:gidx + 1, :], (gsz, n))
                             for gidx in range(N_EXPERT_GROUPS)], axis=0)
    cur = jnp.where(emask > 0.5, choice, ninf)
    eid = lax.broadcasted_iota(I32, (N_EXPERTS, n), 0)
    selm = jnp.zeros((N_EXPERTS, n), F32)
    for _ in range(TOP_K):
        m = jnp.max(cur, axis=0, keepdims=True)
        first = jnp.min(jnp.where(cur == m, eid, N_EXPERTS), axis=0, keepdims=True)
        sel = eid == first
        selm = jnp.where(sel, 1.0, selm)
        cur = jnp.where(sel, ninf, cur)
    w = jnp.where(selm > 0.5, scores, 0.0)
    w = w / jnp.sum(w, axis=0, keepdims=True) * ROUTED_SCALE
    return jnp.where(selm > 0.5, w, -1.0)


def _unpermute(blk_ref, scr_ref, dil, tm):
    if dil == 1:
        return blk_ref[0, 0].astype(F32)
    n_chunks = scr_ref.shape[0]
    for r in range(dil):
        rows = blk_ref[0, r].astype(F32)
        for j in range(n_chunks):
            scr_ref[j, pl.ds(r, tm // dil, stride=dil), :] = rows[:, j * LANES:(j + 1) * LANES]
    return jnp.concatenate([scr_ref[j] for j in range(n_chunks)], axis=1)


def _post_body(*refs, combine, dils):
    if combine:
        o_refs, l_refs, rest = refs[:3], refs[3:6], refs[6:]
    else:
        o_refs, rest = refs[:1], refs[1:]
    (ob_ref, gt_ref, x_ref, g1_ref, sc2_ref, sh2_ref, npost_ref, npre_ref, wa_ref, wb_ref, wo_ref,
     wrt_ref, rb_ref, x1_ref, hp_ref, wt_ref) = rest[:16]
    scr = rest[16:]
    tm = x_ref.shape[1]
    if combine:
        os_, ls_ = [], []
        si = 0
        for gi, dil in enumerate(dils):
            os_.append(_unpermute(o_refs[gi], scr[si] if dil > 1 else None, dil, tm))
            ls_.append(_unpermute(l_refs[gi], scr[si + 1] if dil > 1 else None, dil, tm))
            si += 2 if dil > 1 else 0
        mx = jnp.maximum(jnp.maximum(ls_[0], ls_[1]), ls_[2])
        es = [jnp.exp(z - mx) for z in ls_]
        o_a = (es[0] * os_[0] + es[1] * os_[1] + es[2] * os_[2]) / (es[0] + es[1] + es[2])
    else:
        o_a = o_refs[0][0]
    gt = gt_ref[0].astype(F32)
    za = _dot(o_a.astype(BF16), wa_ref[...])
    zb = _dot(ob_ref[0].astype(BF16), wb_ref[...])
    merged = gt[:, :D_MODEL] * za + gt[:, D_MODEL:] * zb
    z = _dot(merged.astype(BF16), wo_ref[...])
    x1 = x_ref[0] + g1_ref[0] * _rms(z, npost_ref[...])
    x1_ref[0] = x1
    h2 = _rms(x1, npre_ref[...]) * (1.0 + sc2_ref[0]) + sh2_ref[0]
    packed = _pack_pairs(h2)
    for s in range(ROW_TILE_SUBLANES):
        hp_ref[0, pl.ds(s, tm, stride=ROW_TILE_SUBLANES), :] = packed[:, s * LANES:(s + 1) * LANES]
    tp =-(-tm // LANES) * LANES
    if tp != tm:
        h2 = jnp.concatenate([h2, jnp.zeros((tp - tm, D_MODEL), F32)], axis=0)
    logits_t = _dot_nt_split(wrt_ref[...], h2)
    w = _route_t(_sigmoid(logits_t[:N_EXPERTS, :]), rb_ref[...])
    wt_ref[...] = w[:, :tm]


def _post_call(o_parts, lse_parts, ob, gates, x, gate1, scale2, shift2, p, wa, wb, wo, wrt, rb, tm, mod_per_row):
    nb, t, _ = x.shape
    nt = t // tm
    combine = lse_parts is not None
    rowblk = lambda width: pl.BlockSpec((1, tm, width), lambda b, i: (b, i, 0))
    if mod_per_row:
        mod_spec = rowblk(D_MODEL)
    else:
        mod_spec = pl.BlockSpec((1, 1, D_MODEL), lambda b, i: (b, 0, 0))
    const = lambda shp: pl.BlockSpec(shp, lambda b, i: (0, 0))
    scratch = []
    if combine:
        dils = tuple(o.shape[1] for o in o_parts)
        o_args = list(o_parts) + list(lse_parts)
        o_specs = [pl.BlockSpec((1, d, tm // d, D_GROUP_A), lambda b, i: (b, 0, i, 0)) for d in dils] * 2
        for d in dils:
            if d > 1:
                scratch += [pltpu.VMEM((D_GROUP_A // LANES, tm, LANES), F32)] * 2
    else:
        dils = ()
        o_args = [o_parts[0]]
        o_specs = [rowblk(D_GROUP_A)]
    return pl.pallas_call(
        functools.partial(_post_body, combine=combine, dils=dils),
        out_shape=(jax.ShapeDtypeStruct((nb, t, D_MODEL), F32),
                   jax.ShapeDtypeStruct((nb, t * ROW_TILE_SUBLANES, LANES), I32),
                   jax.ShapeDtypeStruct((N_EXPERTS, nb * t), F32)),
        grid=(nb, nt),
        in_specs=o_specs + [rowblk(D_B), rowblk(2 * D_MODEL), rowblk(D_MODEL),
                            mod_spec, mod_spec, mod_spec, const((1, D_MODEL)), const((1, D_MODEL)),
                            const((D_GROUP_A, D_MODEL)), const((D_B, D_MODEL)), const((D_MODEL, D_MODEL)),
                            const((LANES, D_MODEL)), const((N_EXPERTS, 1))],
        out_specs=(rowblk(D_MODEL),
                   pl.BlockSpec((1, tm * ROW_TILE_SUBLANES, LANES), lambda b, i: (b, i, 0)),
                   pl.BlockSpec((N_EXPERTS, tm), lambda b, i: (0, b * nt + i))),
        scratch_shapes=scratch,
        compiler_params=_cparams(("arbitrary", "arbitrary")),
        name="post",
    )(*o_args, ob, gates, x, gate1, scale2, shift2, p['norm_post_mix'].reshape(1, -1),
      p['norm_pre_ffn'].reshape(1, -1), wa, wb, wo, wrt, rb)


def _rank_body(w_ref, dest_ref, w8_ref, tab_ref, etab_ref, cnt_ref, pst_ref, run_ref, *, n_real, n_slots):
    ph = pl.program_id(0)
    i = pl.program_id(1)
    T = MOE_TILE
    w = w_ref[...]
    sel = (w >= 0.0).astype(F32)
    cnt_tile = jnp.broadcast_to(jnp.sum(sel, axis=1, keepdims=True), (N_EXPERTS, LANES))
    ei = lax.broadcasted_iota(I32, (N_EXPERTS, N_EXPERTS), 0)
    ej = lax.broadcasted_iota(I32, (N_EXPERTS, N_EXPERTS), 1)

    @pl.when((ph == 0) & (i == 0))
    def _():
        cnt_ref[...] = jnp.zeros_like(cnt_ref)

    @pl.when(ph == 0)
    def _():
        cnt_ref[...] += cnt_tile

    @pl.when((ph == 1) & (i == 0))
    def _():
        cnt = cnt_ref[...]
        padded = jnp.floor((cnt + (EXPERT_BLOCK - 1)) / EXPERT_BLOCK) * EXPERT_BLOCK
        pstart = _dot_exact((ej < ei).astype(F32), padded)
        pst_ref[...] = pstart
        run_ref[...] = jnp.zeros_like(run_ref)
        pend = pstart + padded
        vend = pstart + cnt
        esub = lax.broadcasted_iota(I32, (N_EXPERTS, LANES), 0)
        lane = lax.broadcasted_iota(I32, (1, LANES), 1)
        tab_ref[...] = jnp.zeros_like(tab_ref)
        for c in range(tab_ref.shape[1] // LANES):
            bs = ((c * LANES + lane) * EXPERT_BLOCK).astype(F32)
            be = jnp.minimum(jnp.sum((pend <= bs).astype(F32), axis=0, keepdims=True), N_EXPERTS - 1.0)
            tab_ref[0:1, c * LANES:(c + 1) * LANES] = be.astype(I32)
            tab_ref[1:2, c * LANES:(c + 1) * LANES] = (pend[N_EXPERTS - 1:, :] / EXPERT_BLOCK).astype(I32)
        on_diag = esub == lax.broadcasted_iota(I32, (N_EXPERTS, LANES), 1)
        etab_ref[...] = jnp.zeros_like(etab_ref)
        lo = jnp.sum(jnp.where(on_diag, vend, 0.0), axis=0, keepdims=True)
        hi = jnp.sum(jnp.where(on_diag, pend, 0.0), axis=0, keepdims=True)
        etab_ref[0:1, :] = jnp.where(lane == N_EXPERTS, pend[N_EXPERTS - 1:, :], lo).astype(I32)
        etab_ref[1:2, :] = jnp.where(lane == N_EXPERTS, float(n_slots), hi).astype(I32)

    @pl.when(ph == 1)
    def _():
        ti = lax.broadcasted_iota(I32, (T, T), 0)
        tj = lax.broadcasted_iota(I32, (T, T), 1)
        selb = sel.astype(BF16)
        rank = _dot(selb, (ti < tj).astype(BF16))
        ordn = _dot((ej < ei).astype(BF16), selb)
        dest_e = pst_ref[:, :1] + run_ref[:, :1] + rank
        run_ref[...] += cnt_tile
        tok = i * T + lax.broadcasted_iota(I32, (1, T), 1)
        dks, wks = [], []
        for k in range(TOP_K):
            m = (sel > 0.5) & (ordn == float(k))
            dk = jnp.sum(jnp.where(m, dest_e, 0.0), axis=0, keepdims=True)
            wk = jnp.sum(jnp.where(m, w, 0.0), axis=0, keepdims=True)
            dks.append(jnp.where(tok < n_real, dk, 0.0))
            wks.append(jnp.where(tok < n_real, wk, 0.0))
        dest_ref[...] = jnp.concatenate(dks, axis=0).astype(I32)
        w8_ref[...] = jnp.concatenate(wks, axis=0)


def _rank_call(w_t, n_real, n_blocks, n_blocks_pad):
    n = w_t.shape[1]
    nt = n // MOE_TILE
    return pl.pallas_call(
        functools.partial(_rank_body, n_real=n_real, n_slots=n_blocks * EXPERT_BLOCK),
        out_shape=(jax.ShapeDtypeStruct((TOP_K, n), I32),
                   jax.ShapeDtypeStruct((TOP_K, n), F32),
                   jax.ShapeDtypeStruct((8, n_blocks_pad), I32),
                   jax.ShapeDtypeStruct((8, LANES), I32)),
        grid=(2, nt),
        in_specs=[pl.BlockSpec((N_EXPERTS, MOE_TILE), lambda ph, i: (0, i))],
        out_specs=(pl.BlockSpec((TOP_K, MOE_TILE), lambda ph, i: (0, i * ph)),
                   pl.BlockSpec((TOP_K, MOE_TILE), lambda ph, i: (0, i * ph)),
                   pl.BlockSpec((8, n_blocks_pad), lambda ph, i: (0, 0)),
                   pl.BlockSpec((8, LANES), lambda ph, i: (0, 0))),
        scratch_shapes=[pltpu.VMEM((N_EXPERTS, LANES), F32)] * 3,
        compiler_params=_cparams(("arbitrary", "arbitrary")),
        name="rank",
    )(w_t)


def _tile_rows(ref, row, n):
    return ref.at[pl.ds(pl.multiple_of(row * ROW_TILE_SUBLANES, ROW_TILE_SUBLANES), n * ROW_TILE_SUBLANES)]


def _zero_fill(etab_ref, zbuf, xs_hbm, zsem, wait):
    def go(src, dst):
        cp = pltpu.make_async_copy(src, dst, zsem)
        if wait:
            cp.wait()
        else:
            cp.start()

    def per_range(e, carry):
        lo = etab_ref[0, e]
        n = etab_ref[1, e] - lo
        n_full = n // ZERO_ROWS

        def full(j, c):
            go(zbuf, _tile_rows(xs_hbm, lo + j * ZERO_ROWS, ZERO_ROWS))
            return c

        lax.fori_loop(0, n_full, full, 0)
        pos = lo + n_full * ZERO_ROWS
        rem = n - n_full * ZERO_ROWS
        size = ZERO_ROWS // 2
        while size >= 1:
            bit = rem & size

            @pl.when(bit != 0)
            def _(size=size, pos=pos):
                go(_tile_rows(zbuf, 0, size), _tile_rows(xs_hbm, pos, size))

            pos = pos + bit
            size //= 2
        return carry

    lax.fori_loop(0, N_EXPERTS + 1, per_range, 0)


def _dispatch_body(dest_ref, etab_ref, xa_ref, xb_ref, xs_hbm, zbuf, sem, zsem, *, n_real, n_full):
    i = pl.program_id(0)
    T = MOE_TILE
    n_tok = jnp.clip(n_real - i * T, 0, T)

    def issue_from(x_ref):
        def issue(t, carry):
            for k in range(TOP_K):
                pltpu.make_async_copy(_tile_rows(x_ref, t, 1), _tile_rows(xs_hbm, dest_ref[k * T + t], 1),
                                      sem).start(priority=k % 2)
            return carry

        lax.fori_loop(0, n_tok, issue, 0)

    @pl.when(i < n_full)
    def _():
        issue_from(xa_ref)

    @pl.when(i >= n_full)
    def _():
        issue_from(xb_ref)

    @pl.when(i == 0)
    def _():
        zbuf[...] = jnp.zeros_like(zbuf)
        _zero_fill(etab_ref, zbuf, xs_hbm, zsem, wait=False)
        _zero_fill(etab_ref, zbuf, xs_hbm, zsem, wait=True)

    @pl.when(n_tok == T)
    def _():
        pltpu.make_async_copy(_tile_rows(xs_hbm, 0, T * TOP_K), _tile_rows(xs_hbm, 0, T * TOP_K), sem).wait()

    @pl.when(n_tok < T)
    def _():
        def drain(j, carry):
            pltpu.make_async_copy(_tile_rows(xs_hbm, 0, 1), _tile_rows(xs_hbm, 0, 1), sem).wait()
            return carry

        lax.fori_loop(0, n_tok * TOP_K, drain, 0)


def _dispatch_call(dest, etab, hp_a, hp_b, n_real, n_slots):
    tile_rows = MOE_TILE * ROW_TILE_SUBLANES
    n_full = hp_a.shape[0] // tile_rows
    return pl.pallas_call(
        functools.partial(_dispatch_body, n_real=n_real, n_full=n_full),
        out_shape=jax.ShapeDtypeStruct((n_slots * ROW_TILE_SUBLANES, LANES), I32),
        grid=(n_full + 1,),
        in_specs=[pl.BlockSpec((TOP_K * MOE_TILE,), lambda i: (i,), memory_space=pltpu.SMEM),
                  pl.BlockSpec((8, LANES), lambda i: (0, 0), memory_space=pltpu.SMEM),
                  pl.BlockSpec((tile_rows, LANES), lambda i: (jnp.minimum(i, n_full - 1), 0)),
                  pl.BlockSpec((tile_rows, LANES), lambda i: (0, 0))],
        out_specs=pl.BlockSpec(memory_space=pl.ANY),
        scratch_shapes=[pltpu.VMEM((ZERO_ROWS * ROW_TILE_SUBLANES, LANES), I32),
                        pltpu.SemaphoreType.DMA, pltpu.SemaphoreType.DMA],
        compiler_params=_cparams(("arbitrary",)),
        name="dispatch",
    )(dest, etab, hp_a, hp_b)


def _rows_from_tiles(ref, lo, n):
    return jnp.concatenate([ref[pl.ds(lo * ROW_TILE_SUBLANES + s, n, stride=ROW_TILE_SUBLANES), :]
                            for s in range(ROW_TILE_SUBLANES)], axis=1)


def _ffn_body(be_ref, nu_ref, xs_ref, wg_hbm, wu_hbm, wd_hbm, ys_ref, wgf, wuf, wdf, wgb, wub, wdb, slot_ref, sem):
    j = pl.program_id(0)
    n_used = nu_ref[0]

    def fetch(e, slot):
        return [pltpu.make_async_copy(w_hbm.at[e], wf.at[slot], sem.at[slot])
                for w_hbm, wf in ((wg_hbm, wgf), (wu_hbm, wuf), (wd_hbm, wdf))]

    @pl.when(j < n_used)
    def _():
        e = be_ref[j]

        @pl.when(j == 0)
        def _():
            slot_ref[0] = 0
            for cp in fetch(e, 0):
                cp.start()

        @pl.when((j == 0) | (e != be_ref[jnp.maximum(j - 1, 0)]))
        def _():
            slot = slot_ref[0]
            for cp in fetch(e, slot):
                cp.wait()
            wgb[...] = wgf[slot].astype(BF16)
            wub[...] = wuf[slot].astype(BF16)
            wdb[...] = wdf[slot].astype(BF16)
            last = be_ref.shape[0] - 1
            nxt = lax.while_loop(lambda i: (i < n_used) & (be_ref[jnp.minimum(i, last)] == e), lambda i: i + 1, j + 1)

            @pl.when(nxt < n_used)
            def _():
                for cp in fetch(be_ref[jnp.minimum(nxt, last)], 1 - slot):
                    cp.start()

            slot_ref[0] = 1 - slot

        x = _unpack_pairs(_rows_from_tiles(xs_ref, 0, EXPERT_BLOCK)).astype(BF16)
        act = _silu(_dot(x, wgb[...])) * _dot(x, wub[...])
        y = _dot(act.astype(BF16), wdb[...])
        packed = _pack_pairs(y)
        for s in range(ROW_TILE_SUBLANES):
            ys_ref[pl.ds(s, EXPERT_BLOCK, stride=ROW_TILE_SUBLANES), :] = packed[:, s * LANES:(s + 1) * LANES]

    @pl.when(j >= nu_ref[0])
    def _():
        ys_ref[...] = jnp.zeros_like(ys_ref)


def _ffn_call(blk_e, n_used, xs, w_gate, w_up, w_down, n_blocks):
    tile_blk = pl.BlockSpec((EXPERT_BLOCK * ROW_TILE_SUBLANES, LANES), lambda j, be, nu: (j, 0))
    last = lambda j, nu: jnp.minimum(j, nu[0] - 1)
    grid_spec = pltpu.PrefetchScalarGridSpec(
        num_scalar_prefetch=2,
        grid=(n_blocks,),
        in_specs=[pl.BlockSpec((EXPERT_BLOCK * ROW_TILE_SUBLANES, LANES), lambda j, be, nu: (last(j, nu), 0)),
                  pl.BlockSpec(memory_space=pl.ANY), pl.BlockSpec(memory_space=pl.ANY),
                  pl.BlockSpec(memory_space=pl.ANY)],
        out_specs=tile_blk,
        scratch_shapes=[pltpu.VMEM((2, D_MODEL, D_EXPERT), F32), pltpu.VMEM((2, D_MODEL, D_EXPERT), F32),
                        pltpu.VMEM((2, D_EXPERT, D_MODEL), F32),
                        pltpu.VMEM((D_MODEL, D_EXPERT), BF16), pltpu.VMEM((D_MODEL, D_EXPERT), BF16),
                        pltpu.VMEM((D_EXPERT, D_MODEL), BF16), pltpu.SMEM((1,), I32), pltpu.SemaphoreType.DMA((2,))])
    return pl.pallas_call(
        _ffn_body,
        out_shape=jax.ShapeDtypeStruct((n_blocks * EXPERT_BLOCK * ROW_TILE_SUBLANES, LANES), I32),
        grid_spec=grid_spec,
        compiler_params=_cparams(("arbitrary",)),
        name="ffn",
    )(blk_e, n_used, xs, w_gate, w_up, w_down)


def _combine_body(dest_ref, dnext_ref, w8_ref, xa_ref, xb_ref, x1a_ref, x1b_ref, g2a_ref, g2b_ref, gain_ref,
                  sg_ref, su_ref, sd_ref, ys_hbm, oa_ref, ob_ref, buf, sem):
    j = pl.program_id(0)
    T = COMBINE_TILE
    RC = COMBINE_ROWS

    def issue(d_ref, slot, t):
        for k in range(TOP_K):
            pltpu.make_async_copy(_tile_rows(ys_hbm, d_ref[k * T + t], 1), _tile_rows(buf.at[slot], k * T + t, 1),
                                  sem.at[slot]).start(priority=k % 2)

    def wait(slot):
        pltpu.make_async_copy(_tile_rows(ys_hbm, 0, T * TOP_K), buf.at[slot], sem.at[slot]).wait()

    def step(slot):
        is_tail = j == 0
        wait(slot)
        for t in range(RC):
            issue(dnext_ref, 1 - slot, t)
        x = _unpack_pairs(jnp.where(is_tail, _rows_from_tiles(xb_ref, 0, T),
                                    _rows_from_tiles(xa_ref, 0, T))).astype(BF16)
        shared = _dot((_silu(_dot(x, sg_ref[...])) * _dot(x, su_ref[...])).astype(BF16), sd_ref[...])
        w_t = jnp.concatenate([w8_ref[...], jnp.zeros((LANES - TOP_K, T), F32)], axis=0).T
        oa_ref[...] = shared
        for r0 in range(0, T, RC):
            if r0 > 0:
                for t in range(r0, r0 + RC):
                    issue(dnext_ref, 1 - slot, t)
            acc = oa_ref[r0:r0 + RC, :]
            for k in range(TOP_K):
                acc = acc + w_t[r0:r0 + RC, k:k + 1] * _unpack_pairs(_rows_from_tiles(buf.at[slot], k * T + r0, RC))
            x1 = jnp.where(is_tail, x1b_ref[r0:r0 + RC, :], x1a_ref[r0:r0 + RC, :])
            g2 = jnp.where(is_tail, g2b_ref[r0:r0 + RC, :], g2a_ref[0])
            oa_ref[r0:r0 + RC, :] = x1 + g2 * _rms(acc, gain_ref[...])

        @pl.when(is_tail)
        def _():
            ob_ref[...] = oa_ref[...]

        @pl.when(j + 1 == pl.num_programs(0))
        def _():
            wait(1 - slot)

    @pl.when(j == 0)
    def _():
        lax.fori_loop(0, T, lambda t, c: (issue(dest_ref, 0, t), c)[1], 0, unroll=2)

    @pl.when(j % 2 == 0)
    def _():
        step(0)

    @pl.when(j % 2 == 1)
    def _():
        step(1)


def _combine_call(dest, w8, hp_a, hp_b, x1_a, x1_b, gate2_a, gate2_b, gain, wsg, wsu, wsd, ys):
    T = COMBINE_TILE
    tile_rows = T * ROW_TILE_SUBLANES
    n_full = hp_a.shape[0] // tile_rows
    n_tiles = n_full + 1
    seq = x1_a.shape[0] // gate2_a.shape[0]
    tile_of = lambda j: jnp.where(j == 0, n_full, j - 1)
    full_of = lambda j: jnp.maximum(j - 1, 0)
    const = lambda shp: pl.BlockSpec(shp, lambda j: (0, 0))
    return pl.pallas_call(
        _combine_body,
        out_shape=(jax.ShapeDtypeStruct((n_full * T, D_MODEL), F32), jax.ShapeDtypeStruct((T, D_MODEL), F32)),
        grid=(n_tiles,),
        in_specs=[pl.BlockSpec((TOP_K * T,), lambda j: (tile_of(j),), memory_space=pltpu.SMEM),
                  pl.BlockSpec((TOP_K * T,), lambda j: (tile_of(jnp.minimum(j + 1, n_tiles - 1)),),
                               memory_space=pltpu.SMEM),
                  pl.BlockSpec((TOP_K, T), lambda j: (0, tile_of(j))),
                  pl.BlockSpec((tile_rows, LANES), lambda j: (full_of(j), 0)),
                  pl.BlockSpec((tile_rows, LANES), lambda j: (0, 0)),
                  pl.BlockSpec((T, D_MODEL), lambda j: (full_of(j), 0)),
                  const((T, D_MODEL)),
                  pl.BlockSpec((1, 1, D_MODEL), lambda j: (full_of(j) * T // seq, 0, 0)),
                  const((T, D_MODEL)), const((1, D_MODEL)),
                  const((D_MODEL, D_EXPERT)), const((D_MODEL, D_EXPERT)), const((D_EXPERT, D_MODEL)),
                  pl.BlockSpec(memory_space=pl.ANY)],
        out_specs=(pl.BlockSpec((T, D_MODEL), lambda j: (full_of(j), 0)), const((T, D_MODEL))),
        scratch_shapes=[pltpu.VMEM((2, TOP_K * tile_rows, LANES), I32), pltpu.SemaphoreType.DMA((2,))],
        compiler_params=_cparams(("arbitrary",)),
        name="combine",
    )(dest, dest, w8, hp_a, hp_b, x1_a, x1_b, gate2_a, gate2_b, gain.reshape(1, -1), wsg, wsu, wsd, ys)


def _rope_tables(pos):
    half = HEAD_DIM_A // 2
    inv_freq = ROPE_THETA ** (-jnp.arange(half, dtype=F32) / half)
    ang = pos.astype(F32)[:, None] * inv_freq[None, :]
    cos = jnp.cos(ang)
    sin = jnp.sin(ang)
    reps = LANES // HEAD_DIM_A
    cos_t = jnp.tile(jnp.concatenate([cos, cos], axis=1), (1, reps))
    sin_t = jnp.tile(jnp.concatenate([-sin, sin], axis=1), (1, reps))
    return cos_t, sin_t


def kernel(x_prompt, x_sample, c_prompt, c_sample, cache_a1_kv, cache_a2_kv, cache_a3_kv, state_b_wkv, state_b_shift, w_ada, b_ada, norm_pre_mix, norm_post_mix, norm_pre_ffn, norm_post_ffn, w_in, w_a_out, mu_b, w0_b, w_w2_b, a0_b, w_a2_b, w_g2_b, k_k_b, k_a_b, r_k_b, ln_x_w_b, ln_x_b_b, w_b_out, w_out, w_router, router_bias, w_e_gate, w_e_up, w_e_down, w_s_gate, w_s_up, w_s_down):
    assert DEPTH == 1
    l = 0
    nd = DEC_BATCH
    row = lambda a: a.reshape(1, -1)
    p = {'mu_b': row(mu_b[l]), 'w0_b': row(w0_b[l]), 'w_w2_b': w_w2_b[l], 'a0_b': row(a0_b[l]),
         'w_a2_b': w_a2_b[l], 'w_g2_b': w_g2_b[l], 'k_k_b': row(k_k_b[l]), 'k_a_b': row(k_a_b[l]),
         'r_k_b': row(r_k_b[l]), 'ln_x_w_b': row(ln_x_w_b[l]), 'ln_x_b_b': row(ln_x_b_b[l]),
         'norm_post_mix': norm_post_mix[l], 'norm_pre_ffn': norm_pre_ffn[l]}

    wq, wf, wg = _wsplit_call(w_in[l])
    wa = w_a_out[l].astype(BF16)
    wb = w_b_out[l].astype(BF16)
    wo = w_out[l].astype(BF16)
    wrt = jnp.concatenate([w_router[l].T, jnp.zeros((LANES - N_EXPERTS, D_MODEL), F32)], axis=0)
    rb = router_bias[l].reshape(N_EXPERTS, 1)
    wsg, wsu, wsd = w_s_gate[l].astype(BF16), w_s_up[l].astype(BF16), w_s_down[l].astype(BF16)

    n_c = BATCH + nd
    c_all = jnp.concatenate([c_prompt, c_sample, jnp.zeros((-n_c % 8, D_MODEL), F32)], axis=0)
    mod = _mod_call(c_all, w_ada[l], b_ada[l])
    mod_p = [m.reshape(BATCH, 1, D_MODEL) for m in jnp.split(mod[:BATCH], 6, axis=-1)]
    mod_s = [m.reshape(1, nd, D_MODEL) for m in jnp.split(mod[BATCH:n_c], 6, axis=-1)]

    cos_p, sin_p = _rope_tables(jnp.arange(SEQ, dtype=I32))
    cos_s, sin_s = _rope_tables(jnp.full((nd,), PAST_LEN, I32))

    keep_p = [min(w, SEQ) for w, _ in DILATED_GROUPS]
    dils = tuple(d for _, d in DILATED_GROUPS)

    q0, q1, q2, feat_p, gates_p, *tails_p = _inproj_call(
        x_prompt, norm_pre_mix[l], mod_p[1], mod_p[0], cos_p, sin_p, wq, wf, wg,
        tm=256, keeps=keep_p, mod_per_row=False, dils=dils)
    o_parts, lse_parts = [], []
    for gi, qg in enumerate((q0, q1, q2)):
        o, lse = _attn_call(qg, gi)
        o_parts.append(o)
        lse_parts.append(lse)
    ob_p, wkv_p = _wkv_call(feat_p, p)
    x1_p, hp_p, wt_p = _post_call(o_parts, lse_parts, ob_p, gates_p, x_prompt, mod_p[2], mod_p[4], mod_p[3],
                                  p, wa, wb, wo, wrt, rb, tm=512, mod_per_row=False)

    xs3 = x_sample.reshape(1, nd, D_MODEL)
    s0, s1, s2, feat_s, gates_s, *tails_s = _inproj_call(
        xs3, norm_pre_mix[l], mod_s[1], mod_s[0], cos_s, sin_s, wq, wf, wg,
        tm=nd, keeps=(nd,) * N_GROUPS_A, mod_per_row=True, dils=(1, 1, 1))
    qkv_s = jnp.stack([z.reshape(nd, 3, N_HEADS_A, HEAD_DIM_A) for z in (s0, s1, s2)], axis=2)
    qkv_s = qkv_s.reshape(nd, 3 * N_GROUPS_A * N_HEADS_A, HEAD_DIM_A).astype(F32)
    oa_s = _sattn_call(qkv_s, cache_a1_kv[l], cache_a2_kv[l], cache_a3_kv[l])
    r_s, w_s, k_s, v_s, aa_s, bb_s, g_s = _swkv_prep_call(feat_s[0], state_b_shift[l], p)
    nh = nd * N_HEADS_B
    as_row = lambda a: a.reshape(nh, 1, HEAD_DIM_B)
    s_new, y_col = _swkv_step_call(state_b_wkv[l].reshape(nh, HEAD_DIM_B, HEAD_DIM_B), as_row(aa_s), as_row(w_s),
                                   as_row(bb_s), as_row(k_s), as_row(r_s), v_s.reshape(nh, HEAD_DIM_B))
    ob_s = _swkv_fin_call(y_col.reshape(nd, D_B), r_s, k_s, v_s, g_s, p)
    x1_s, hp_s, wt_s = _post_call([oa_s.reshape(1, nd, D_GROUP_A)], None, ob_s.reshape(1, nd, D_B), gates_s, xs3,
                                  mod_s[2], mod_s[4], mod_s[3], p, wa, wb, wo, wrt, rb, tm=nd, mod_per_row=True)

    n_p = BATCH * SEQ
    n_real = n_p + nd
    n_all = -(-n_real // MOE_TILE) * MOE_TILE
    pad = n_all - n_real
    n_blocks = -(-(n_real * TOP_K) // EXPERT_BLOCK) + N_EXPERTS
    n_blocks_pad = -(-n_blocks // LANES) * LANES
    assert n_p % MOE_TILE == 0 and nd <= MOE_TILE
    hp_a = hp_p.reshape(n_p * ROW_TILE_SUBLANES, LANES)
    hp_b = jnp.concatenate([hp_s[0], jnp.zeros((pad * ROW_TILE_SUBLANES, LANES), I32)], axis=0)
    wt_all = jnp.concatenate([wt_p, wt_s, jnp.full((N_EXPERTS, pad), -1.0, F32)], axis=1)
    dest8, w8, tab, etab = _rank_call(wt_all, n_real, n_blocks, n_blocks_pad)
    dest = dest8.reshape(TOP_K, n_all // MOE_TILE, MOE_TILE).transpose(1, 0, 2).reshape(-1)
    xs = _dispatch_call(dest, etab, hp_a, hp_b, n_real, n_blocks * EXPERT_BLOCK)
    ys = _ffn_call(tab[0], tab[1, :1], xs, w_e_gate[l], w_e_up[l], w_e_down[l], n_blocks)
    n_ct = n_p // COMBINE_TILE + 1
    dest_c = dest8[:, :n_ct * COMBINE_TILE].reshape(TOP_K, n_ct, COMBINE_TILE).transpose(1, 0, 2).reshape(-1)
    pad_rows = lambda z: jnp.concatenate([z, jnp.zeros((COMBINE_TILE - nd, D_MODEL), F32)], axis=0)
    out_p, out_s = _combine_call(dest_c, w8, hp_a, hp_b, x1_p.reshape(n_p, D_MODEL), pad_rows(x1_s[0]), mod_p[5],
                                 pad_rows(mod_s[5][0]), norm_post_ffn[l], wsg, wsu, wsd, ys)
    y_prompt = out_p.reshape(BATCH, SEQ, D_MODEL)
    y_sample = out_s[:nd]

    a_p = [z.reshape(1, BATCH, kp, 2, N_HEADS_A, HEAD_DIM_A) for z, kp in zip(tails_p, keep_p)]
    a_s = [z.reshape(1, nd, DEC_SEQ, 2, N_HEADS_A, HEAD_DIM_A) for z in tails_s]
    shift_p = feat_p[:, -1][None]
    shift_s = feat_s[0][None]
    return (y_prompt, y_sample.reshape(nd, DEC_SEQ, D_MODEL), a_p[0], a_p[1], a_p[2], wkv_p[None], shift_p,
            a_s[0], a_s[1], a_s[2], s_new.reshape(1, nd, N_HEADS_B, HEAD_DIM_B, HEAD_DIM_B), shift_s)
```

```python
import functools
import math

import jax
import jax.numpy as jnp
from jax import lax
from jax.experimental import pallas as pl
from jax.experimental.pallas import tpu as pltpu

F32 = jnp.float32
BF16 = jnp.bfloat16
I32 = jnp.int32

D_MODEL = 1024
BATCH = 2
SEQ = 8192
DEPTH = 1
DEC_BATCH = 32
DEC_SEQ = 1
PAST_LEN = 16384

HEAD_DIM_A = 64
N_HEADS_A = 8
DILATED_GROUPS = ((128, 1), (512, 4), (2048, 16))
N_GROUPS_A = 3
D_GROUP_A = N_HEADS_A * HEAD_DIM_A
D_A = N_GROUPS_A * D_GROUP_A
D_QKV = 3 * D_A
BAND_BLOCK = 128
ROPE_THETA = 10000.0

HEAD_DIM_B = 64
N_HEADS_B = 16
D_B = 1024
DECAY_LORA = 64
AAA_LORA = 64
GATE_LORA = 160
D_SHIFT_B = 3 * D_B + DECAY_LORA + AAA_LORA + GATE_LORA
LN_X_EPS = 64e-5

N_EXPERTS = 64
TOP_K = 8
N_EXPERT_GROUPS = 8
TOPK_GROUPS = 4
D_EXPERT = 256
ROUTED_SCALE = 2.5
EXPERT_BLOCK = 1024
NORM_EPS = 1e-6

LANES = 128
WKV_CHUNK = 64
MOE_TILE = 1024
COMBINE_TILE = 256
COMBINE_ROWS = 32
VMEM_LIMIT = 56 * 1024 * 1024
ROW_TILE_SUBLANES = D_MODEL // (2 * LANES)
ZERO_ROWS = 256


def _cparams(sem):
    return pltpu.CompilerParams(dimension_semantics=sem, vmem_limit_bytes=VMEM_LIMIT)


def _dot(a, b):
    return jnp.dot(a, b, preferred_element_type=F32)


def _dot_nt(a, b):
    return lax.dot_general(a, b, (((1,), (1,)), ((), ())), preferred_element_type=F32)


def _dot_tn(a, b):
    return lax.dot_general(a, b, (((0,), (0,)), ((), ())), preferred_element_type=F32)


def _dot_nt_split(a, b):
    ah = a.astype(BF16)
    al = (a - ah.astype(F32)).astype(BF16)
    bh = b.astype(BF16)
    bl = (b - bh.astype(F32)).astype(BF16)
    return _dot_nt(ah, bh) + _dot_nt(ah, bl) + _dot_nt(al, bh)


def _dot_exact(a, b):
    return lax.dot_general(a, b, (((1,), (0,)), ((), ())), precision=lax.Precision.HIGHEST,
                           preferred_element_type=F32)


def _rms(x, gain):
    return x * lax.rsqrt(jnp.mean(x * x, axis=-1, keepdims=True) + NORM_EPS) * gain


def _sigmoid(x):
    return 1.0 / (1.0 + jnp.exp(-x))


def _silu(x):
    return x * _sigmoid(x)


def _softplus(x):
    return jnp.maximum(x, 0.0) + jnp.log(1.0 + jnp.exp(-jnp.abs(x)))


def _pack_pairs(x):
    half = D_MODEL // 2
    lo = lax.bitcast_convert_type(x[:, :half].astype(BF16).astype(F32), I32)
    hi = lax.bitcast_convert_type(x[:, half:].astype(BF16).astype(F32), I32)
    return lax.shift_right_logical(lo, 16) | (hi & jnp.int32(-65536))


def _unpack_pairs(w):
    lo = lax.bitcast_convert_type(w << 16, F32)
    hi = lax.bitcast_convert_type(w & jnp.int32(-65536), F32)
    return jnp.concatenate([lo, hi], axis=1)


def _mod_body(c_ref, w_ref, b_ref, o_ref):
    s = _silu(c_ref[...]).astype(BF16)
    o_ref[...] = _dot(s, w_ref[...].astype(BF16)) + b_ref[...]


def _mod_call(c_all, w_ada, b_ada):
    rows = c_all.shape[0]
    tn = 1536
    return pl.pallas_call(
        _mod_body,
        out_shape=jax.ShapeDtypeStruct((rows, 6 * D_MODEL), F32),
        grid=(6 * D_MODEL // tn,),
        in_specs=[pl.BlockSpec((rows, D_MODEL), lambda j: (0, 0)),
                  pl.BlockSpec((D_MODEL, tn), lambda j: (0, j)),
                  pl.BlockSpec((1, tn), lambda j: (0, j))],
        out_specs=pl.BlockSpec((rows, tn), lambda j: (0, j)),
        compiler_params=_cparams(("arbitrary",)),
        name="mod",
    )(c_all, w_ada, b_ada.reshape(1, -1))


def _wsplit_body(w_ref, q_ref, f_ref, g_ref):
    w = w_ref[...]
    q_ref[...] = w[:, :D_QKV].astype(BF16)
    f_ref[...] = w[:, D_QKV:D_QKV + D_SHIFT_B].astype(BF16)
    g_ref[...] = w[:, D_QKV + D_SHIFT_B:].astype(BF16)


def _wsplit_call(w):
    rows, cols = w.shape
    tr = 128
    widths = (D_QKV, D_SHIFT_B, cols - D_QKV - D_SHIFT_B)
    return pl.pallas_call(
        _wsplit_body,
        out_shape=tuple(jax.ShapeDtypeStruct((rows, n), BF16) for n in widths),
        grid=(rows // tr,),
        in_specs=[pl.BlockSpec((tr, cols), lambda i: (i, 0))],
        out_specs=tuple(pl.BlockSpec((tr, n), lambda i: (i, 0)) for n in widths),
        compiler_params=_cparams(("arbitrary",)),
        name="wsplit",
    )(w)


def _inproj_body(x_ref, g_ref, sc_ref, sh_ref, cos_ref, sin_ref, wq_ref, wf_ref, wg_ref,
                 q0_ref, q1_ref, q2_ref, feat_ref, gate_ref, t0_ref, t1_ref, t2_ref, p_ref, *, dils):
    x = x_ref[0]
    tm = x.shape[0]
    h = _rms(x, g_ref[...]) * (1.0 + sc_ref[0]) + sh_ref[0]
    hb = h.astype(BF16)
    p = _dot(hb, wq_ref[...])
    cos = cos_ref[...]
    sin = sin_ref[...]
    lane = lax.broadcasted_iota(I32, cos.shape, 1)
    first_half = (lane % HEAD_DIM_A) < (HEAD_DIM_A // 2)
    for c in range(2 * D_A // LANES):
        xc = p[:, c * LANES:(c + 1) * LANES]
        partner = jnp.where(first_half, pltpu.roll(xc, LANES - HEAD_DIM_A // 2, 1),
                            pltpu.roll(xc, HEAD_DIM_A // 2, 1))
        rc = xc * cos + partner * sin
        if c < D_A // LANES:
            rc = rc * (HEAD_DIM_A ** -0.5)
        p_ref[c] = rc
    for c in range(2 * D_A // LANES, D_QKV // LANES):
        p_ref[c] = p[:, c * LANES:(c + 1) * LANES]
    per_group = D_GROUP_A // LANES
    for gi, t_ref in enumerate((t0_ref, t1_ref, t2_ref)):
        rows = t_ref.shape[1]
        for which in (1, 2):
            for j in range(per_group):
                c = (which * D_A + gi * D_GROUP_A) // LANES + j
                t_ref[0, :, (which - 1) * D_GROUP_A + j * LANES:(which - 1) * D_GROUP_A + (j + 1) * LANES] = \
                    p_ref[c, tm - rows:tm, :]
    for gi, (out_ref, dil) in enumerate(zip((q0_ref, q1_ref, q2_ref), dils)):
        for which in range(3):
            for j in range(per_group):
                c = (which * D_A + gi * D_GROUP_A) // LANES + j
                dst = slice(which * D_GROUP_A + j * LANES, which * D_GROUP_A + (j + 1) * LANES)
                if dil == 1:
                    out_ref[0, 0, :, dst] = p_ref[c].astype(BF16)
                else:
                    for r in range(dil):
                        out_ref[0, r, :, dst] = p_ref[c, pl.ds(r, tm // dil, stride=dil), :].astype(BF16)
    feat_ref[0] = _dot(hb, wf_ref[...])
    gate_ref[0] = _sigmoid(_dot(hb, wg_ref[...])).astype(BF16)


def _inproj_call(x, gain, scale, shift, cos_t, sin_t, wq, wf, wg, tm, keeps, mod_per_row, dils):
    nb, t, _ = x.shape
    nt = t // tm

    def tail_spec(keep):
        if keep <= tm:
            return pl.BlockSpec((1, keep, 2 * D_GROUP_A), lambda b, i: (b, 0, 0))
        first = (t - keep) // tm
        return pl.BlockSpec((1, tm, 2 * D_GROUP_A), lambda b, i: (b, jnp.maximum(i - first, 0), 0))

    if mod_per_row:
        mod_spec = pl.BlockSpec((1, tm, D_MODEL), lambda b, i: (b, i, 0))
    else:
        mod_spec = pl.BlockSpec((1, 1, D_MODEL), lambda b, i: (b, 0, 0))
    resident = lambda shp: pl.BlockSpec(shp, lambda b, i: (0, 0), pipeline_mode=pl.Buffered(1))
    q_shapes = tuple(jax.ShapeDtypeStruct((nb, d, t // d, 3 * D_GROUP_A), BF16) for d in dils)
    q_specs = tuple(pl.BlockSpec((1, d, tm // d, 3 * D_GROUP_A), lambda b, i: (b, 0, i, 0)) for d in dils)
    return pl.pallas_call(
        functools.partial(_inproj_body, dils=dils),
        out_shape=q_shapes + (jax.ShapeDtypeStruct((nb, t, D_SHIFT_B), F32),
                              jax.ShapeDtypeStruct((nb, t, 2 * D_MODEL), BF16),
                              ) + tuple(jax.ShapeDtypeStruct((nb, kp, 2 * D_GROUP_A), F32) for kp in keeps),
        grid=(nb, nt),
        in_specs=[pl.BlockSpec((1, tm, D_MODEL), lambda b, i: (b, i, 0)),
                  pl.BlockSpec((1, D_MODEL), lambda b, i: (0, 0)),
                  mod_spec, mod_spec,
                  pl.BlockSpec((tm, LANES), lambda b, i: (i, 0)),
                  pl.BlockSpec((tm, LANES), lambda b, i: (i, 0)),
                  resident((D_MODEL, D_QKV)), resident((D_MODEL, D_SHIFT_B)),
                  resident((D_MODEL, 2 * D_MODEL))],
        out_specs=q_specs + (pl.BlockSpec((1, tm, D_SHIFT_B), lambda b, i: (b, i, 0)),
                             pl.BlockSpec((1, tm, 2 * D_MODEL), lambda b, i: (b, i, 0)),
                             ) + tuple(tail_spec(kp) for kp in keeps),
        scratch_shapes=[pltpu.VMEM((D_QKV // LANES, tm, LANES), F32)],
        compiler_params=_cparams(("arbitrary", "arbitrary")),
        name="inproj",
    )(x, gain.reshape(1, -1), scale, shift, cos_t, sin_t, wq, wf, wg)


def _attn_body(q_ref, kc_ref, kp_ref, vc_ref, vp_ref, o_ref, lse_ref):
    mb = pl.program_id(2)
    nq = q_ref.shape[2] // BAND_BLOCK
    q = q_ref[0, 0]
    k = jnp.concatenate([kp_ref[0, 0], kc_ref[0, 0]], axis=0)
    v = jnp.concatenate([vp_ref[0, 0], vc_ref[0, 0]], axis=0)
    qi = lax.broadcasted_iota(I32, (BAND_BLOCK, 2 * BAND_BLOCK), 0)
    ki = lax.broadcasted_iota(I32, (BAND_BLOCK, 2 * BAND_BLOCK), 1)
    dist = qi + BAND_BLOCK - ki
    band = (dist >= 0) & (dist <= BAND_BLOCK)
    masks = [band & ((ki >= BAND_BLOCK) | (mb > 0))] + [band] * (nq - 1)
    lane_q = lax.broadcasted_iota(I32, (BAND_BLOCK, LANES), 1)
    lane_k = lax.broadcasted_iota(I32, (2 * BAND_BLOCK, LANES), 1)
    for hp in range(N_HEADS_A // 2):
        sl = slice(hp * LANES, (hp + 1) * LANES)
        chains = [(j, sub) for j in range(nq) for sub in range(2)]
        qs = [q[j * BAND_BLOCK:(j + 1) * BAND_BLOCK, sl] for j in range(nq)]
        ks = [k[j * BAND_BLOCK:(j + 2) * BAND_BLOCK, sl] for j in range(nq)]
        vs = [v[j * BAND_BLOCK:(j + 2) * BAND_BLOCK, sl] for j in range(nq)]
        mqs = [lane_q < HEAD_DIM_A, lane_q >= HEAD_DIM_A]
        mks = [lane_k < HEAD_DIM_A, lane_k >= HEAD_DIM_A]
        s = [jnp.where(masks[j], _dot_nt(jnp.where(mqs[sub], qs[j], jnp.zeros_like(qs[j])), ks[j]), -jnp.inf)
             for j, sub in chains]
        mx = [jnp.max(z, axis=1, keepdims=True) for z in s]
        p = [jnp.exp(z - m) for z, m in zip(s, mx)]
        l = [jnp.sum(z, axis=1, keepdims=True) for z in p]
        pv = [_dot(p[c].astype(BF16), jnp.where(mks[sub], vs[j], jnp.zeros_like(vs[j])))
              for c, (j, sub) in enumerate(chains)]
        for j in range(nq):
            c0, c1 = 2 * j, 2 * j + 1
            o_pair = pv[c0] / l[c0] + pv[c1] / l[c1]
            lse_pair = jnp.where(mqs[0], mx[c0] + jnp.log(l[c0]), mx[c1] + jnp.log(l[c1]))
            o_ref[0, 0, j * BAND_BLOCK:(j + 1) * BAND_BLOCK, sl] = o_pair.astype(BF16)
            lse_ref[0, 0, j * BAND_BLOCK:(j + 1) * BAND_BLOCK, sl] = lse_pair


def _attn_call(qkv_g, gi):
    b, dil, l, _ = qkv_g.shape
    nq = 4
    nb = l // (nq * BAND_BLOCK)
    blk = (1, 1, nq * BAND_BLOCK, D_GROUP_A)
    cur = lambda which: pl.BlockSpec(blk, lambda bb, r, m: (bb, r, m, which))
    prev = lambda which: pl.BlockSpec((1, 1, BAND_BLOCK, D_GROUP_A),
                                      lambda bb, r, m: (bb, r, jnp.maximum(nq * m - 1, 0), which))
    return pl.pallas_call(
        _attn_body,
        out_shape=(jax.ShapeDtypeStruct((b, dil, l, D_GROUP_A), BF16),
                   jax.ShapeDtypeStruct((b, dil, l, D_GROUP_A), F32)),
        grid=(b, dil, nb),
        in_specs=[cur(0), cur(1), prev(1), cur(2), prev(2)],
        out_specs=(pl.BlockSpec(blk, lambda bb, r, m: (bb, r, m, 0)),
                   pl.BlockSpec(blk, lambda bb, r, m: (bb, r, m, 0))),
        compiler_params=_cparams(("arbitrary", "arbitrary", "arbitrary")),
        name=f"attn{gi}",
    )(qkv_g, qkv_g, qkv_g, qkv_g, qkv_g)


def _sattn_body(qkv_ref, b1_ref, b2_ref, b3_ref, o_ref):
    n_rows = 3 * N_GROUPS_A * N_HEADS_A
    sq = jnp.concatenate([qkv_ref[0], jnp.zeros((LANES - n_rows, HEAD_DIM_A), F32)], axis=0)
    cols = jnp.concatenate([sq, jnp.zeros((LANES, LANES - HEAD_DIM_A), F32)], axis=1).T
    col3 = lambda first: jnp.stack([cols[:HEAD_DIM_A, first + h:first + h + 1] for h in range(N_HEADS_A)], axis=0)
    outs, lses = [], []
    for g, (buf_ref, (_, dil)) in enumerate(zip((b1_ref, b2_ref, b3_ref), DILATED_GROUPS)):
        q = col3(g * N_HEADS_A)
        kn = col3((N_GROUPS_A + g) * N_HEADS_A)
        vn = col3((2 * N_GROUPS_A + g) * N_HEADS_A)
        kb = buf_ref[0, 0]
        vb = buf_ref[0, 1]
        wb = kb.shape[-1]
        pos = lax.broadcasted_iota(I32, (1, 1, wb), 2)
        s = jnp.sum(kb * q, axis=1, keepdims=True)
        s = jnp.where(pos % dil == 0, s, -jnp.inf)
        sn = jnp.sum(kn * q, axis=1, keepdims=True)
        m = jnp.maximum(jnp.max(s, axis=2, keepdims=True), sn)
        p = jnp.exp(s - m)
        pn = jnp.exp(sn - m)
        l = jnp.sum(p, axis=2, keepdims=True) + pn
        outs.append((jnp.sum(p * vb, axis=2, keepdims=True) + pn * vn) / l)
        lses.append(m + jnp.log(l))
    mx = jnp.maximum(jnp.maximum(lses[0], lses[1]), lses[2])
    es = [jnp.exp(z - mx) for z in lses]
    o_a = (es[0] * outs[0] + es[1] * outs[1] + es[2] * outs[2]) / (es[0] + es[1] + es[2])
    o_cols = jnp.concatenate([o_a[h] for h in range(N_HEADS_A)] +
                             [jnp.zeros((HEAD_DIM_A, LANES - N_HEADS_A), F32)], axis=1)
    o_rows = jnp.concatenate([o_cols, jnp.zeros((LANES - HEAD_DIM_A, LANES), F32)], axis=0).T
    o_ref[0] = o_rows[:N_HEADS_A, :HEAD_DIM_A]


def _sattn_call(qkv_s, c1, c2, c3):
    n = qkv_s.shape[0]
    views, specs = [], []
    for c in (c1, c2, c3):
        wb = c.shape[1]
        views.append(jnp.transpose(c, (0, 2, 3, 4, 1)))
        specs.append(pl.BlockSpec((1, 2, N_HEADS_A, HEAD_DIM_A, wb), lambda b: (b, 0, 0, 0, 0)))
    return pl.pallas_call(
        _sattn_body,
        out_shape=jax.ShapeDtypeStruct((n, N_HEADS_A, HEAD_DIM_A), F32),
        grid=(n,),
        in_specs=[pl.BlockSpec((1, 3 * N_GROUPS_A * N_HEADS_A, HEAD_DIM_A), lambda b: (b, 0, 0))] + specs,
        out_specs=pl.BlockSpec((1, N_HEADS_A, HEAD_DIM_A), lambda b: (b, 0, 0)),
        compiler_params=_cparams(("arbitrary",)),
        name="sattn",
    )(qkv_s, *views)


def _rwkv_features(xs, w0, ww2, a0, wa2, wg2, k_a):
    r = xs[:, :D_B]
    k = xs[:, D_B:2 * D_B]
    v = xs[:, 2 * D_B:3 * D_B]
    xw = xs[:, 3 * D_B:3 * D_B + DECAY_LORA]
    xa = xs[:, 3 * D_B + DECAY_LORA:3 * D_B + DECAY_LORA + AAA_LORA]
    xg = xs[:, 3 * D_B + DECAY_LORA + AAA_LORA:]
    w_log = -_softplus(-(w0 + _dot(jnp.tanh(xw).astype(BF16), ww2.astype(BF16)))) - 0.5
    a = _sigmoid(a0 + _dot(xa.astype(BF16), wa2.astype(BF16)))
    g = _dot(_sigmoid(xg).astype(BF16), wg2.astype(BF16))
    k_h = k * (1.0 + (a - 1.0) * k_a)
    return r, k, v, w_log, a, g, k_h


def _head_norm(kk_h):
    nrm = jnp.sqrt(jnp.sum(kk_h * kk_h, axis=-1, keepdims=True))
    return kk_h / jnp.maximum(nrm, 1e-12)


def _wkv_finish_head(y, r_h, k_h, v_h, g_h, rk_h, lnw_h, lnb_h):
    mean = jnp.mean(y, axis=-1, keepdims=True)
    var = jnp.mean(jnp.square(y - mean), axis=-1, keepdims=True)
    yn = (y - mean) * lax.rsqrt(var + LN_X_EPS) * lnw_h + lnb_h
    bonus = jnp.sum(r_h * k_h * rk_h, axis=-1, keepdims=True) * v_h
    return (yn + bonus) * g_h


def _wkv_body(f_ref, fp_ref, mu_ref, w0_ref, ww2_ref, a0_ref, wa2_ref, wg2_ref, kk_ref, ka_ref,
              rk_ref, lnw_ref, lnb_ref, o_ref, st_ref, s_ref):
    c = pl.program_id(0)
    C = WKV_CHUNK
    nb = f_ref.shape[0]

    @pl.when(c == 0)
    def _():
        s_ref[...] = jnp.zeros_like(s_ref)

    f = jnp.concatenate([f_ref[b] for b in range(nb)], axis=0)
    row = lax.broadcasted_iota(I32, f.shape, 0)
    prev = pltpu.roll(f, 1, 0)
    for b in range(nb):
        prev = jnp.where(row == b * C, jnp.where(c == 0, 0.0, fp_ref[b][7:8, :]), prev)
    xs = f + mu_ref[...] * (prev - f)
    r, k, v, w_log, a, g, k_h = _rwkv_features(xs, w0_ref[...], ww2_ref[...], a0_ref[...],
                                               wa2_ref[...], wg2_ref[...], ka_ref[...])
    lw = -jnp.exp(w_log)
    kk = k * kk_ref[...]
    jh = lax.broadcasted_iota(I32, (D_B, LANES), 0) // HEAD_DIM_B
    ind = (jh == lax.broadcasted_iota(I32, (D_B, LANES), 1)).astype(BF16)
    ind_t = (lax.broadcasted_iota(I32, (LANES, D_B), 0)
             == lax.broadcasted_iota(I32, (LANES, D_B), 1) // HEAD_DIM_B).astype(BF16)

    def head_sum(z):
        hi = z.astype(BF16)
        lo = (z - hi.astype(F32)).astype(BF16)
        s = _dot(hi, ind) + _dot(lo, ind)
        shi = s.astype(BF16)
        slo = (s - shi.astype(F32)).astype(BF16)
        return _dot(shi, ind_t) + _dot(slo, ind_t)

    kkn = kk / jnp.maximum(jnp.sqrt(head_sum(kk * kk)), 1e-12)

    tr = lax.broadcasted_iota(I32, (nb * C, nb * C), 0)
    sr_ = lax.broadcasted_iota(I32, (nb * C, nb * C), 1)
    tri_incl = ((tr >= sr_) & (tr // C == sr_ // C)).astype(BF16)
    l1 = lw.astype(BF16)
    r1 = lw - l1.astype(F32)
    l2 = r1.astype(BF16)
    l3 = (r1 - l2.astype(F32)).astype(BF16)
    cum = _dot(tri_incl, l1) + _dot(tri_incl, l2) + _dot(tri_incl, l3)
    rhos = [cum[b * C + C // 2 - 1:b * C + C // 2, :] for b in range(nb)]
    rho = jnp.concatenate([jnp.broadcast_to(z, (C, D_B)) for z in rhos], axis=0)
    ep = jnp.exp(cum - rho)
    em = jnp.exp(rho - cum)
    e_a = ep * jnp.exp(-lw)
    r_hat = r * ep
    k_hat = k_h * em
    e_rs = [jnp.exp(z) for z in rhos]
    e_cs = [jnp.exp(cum[b * C + C - 1:b * C + C, :] - rhos[b]) for b in range(nb)]

    ti = lax.broadcasted_iota(I32, (C, C), 0)
    si = lax.broadcasted_iota(I32, (C, C), 1)
    strict = ti > si
    incl = ti >= si
    rk = rk_ref[...]
    lnw = lnw_ref[...]
    lnb = lnb_ref[...]
    items = [(b, h) for b in range(nb) for h in range(N_HEADS_B)]
    heads = range(len(items))
    lanes = [slice(h * HEAD_DIM_B, (h + 1) * HEAD_DIM_B) for _, h in items]
    cut = lambda z, i: z[items[i][0] * C:(items[i][0] + 1) * C, lanes[i]]
    e_r = [e_rs[b][:, lanes[i]] for i, (b, _) in enumerate(items)]
    e_c = [e_cs[b][:, lanes[i]] for i, (b, _) in enumerate(items)]
    a_hat_full = (-kkn * e_a).astype(BF16)
    b_hat_full = (kkn * a * em).astype(BF16)
    a_hat_b = [cut(a_hat_full, h) for h in heads]
    b_hat_b = [cut(b_hat_full, h) for h in heads]
    rh = [cut(r_hat, h) for h in heads]
    vb = [cut(v, h).astype(BF16) for h in heads]
    bk = [jnp.concatenate([b_hat_b[h], cut(k_hat, h).astype(BF16)], axis=0) for h in heads]
    p = [_dot_nt(jnp.concatenate([a_hat_b[h], rh[h].astype(BF16)], axis=0), bk[h]) for h in heads]
    l_ak = [jnp.where(strict, z[:C, C:], 0.0).astype(BF16) for z in p]
    p_rb = [jnp.where(incl, z[C:, :C], 0.0).astype(BF16) for z in p]
    p_rk = [jnp.where(incl, z[C:, C:], 0.0).astype(BF16) for z in p]
    col = lax.broadcasted_iota(I32, (C, 2 * C), 1)
    row2 = lax.broadcasted_iota(I32, (C, 2 * C), 0)
    left = col < C
    zt = [jnp.where(left, jnp.where(row2 > col, z[:C], 0.0), (col == row2 + C).astype(F32)) for z in p]
    for _ in range(int(math.log2(C))):
        zb = [z.astype(BF16) for z in zt]
        res = [_dot(z[:, :C], z) for z in zb]
        zt = [jnp.where(left, res[h], zt[h] + res[h]) for h in heads]
    tb = [z.astype(BF16) for z in zt]
    zeros_c = jnp.zeros((C, HEAD_DIM_B), BF16)
    lv = [_dot(l_ak[h], vb[h]).astype(BF16) for h in heads]
    a_bar = [_dot(tb[h], jnp.concatenate([zeros_c, a_hat_b[h]], axis=0)).astype(BF16) for h in heads]
    u_v = [_dot(tb[h], jnp.concatenate([zeros_c, lv[h]], axis=0)).astype(BF16) for h in heads]
    r_bar = [rh[h] + _dot(p_rb[h], a_bar[h]) for h in heads]
    y_v = [_dot(p_rb[h], u_v[h]) + _dot(p_rk[h], vb[h]) for h in heads]
    ab = [_dot_tn(a_bar[h], b_hat_b[h]).astype(BF16) for h in heads]
    n_t = [_dot_tn(jnp.concatenate([u_v[h], vb[h]], axis=0), bk[h]) for h in heads]
    s0 = [s_ref[b, h] for b, h in items]
    sr = [s0[h] * e_r[h] for h in heads]
    y = [_dot_nt((r_bar[h] * e_r[h]).astype(BF16), s0[h].astype(BF16)) + y_v[h] for h in heads]
    s_new = [(sr[h] + _dot(sr[h].astype(BF16), ab[h]) + n_t[h]) * e_c[h] for h in heads]
    for i, (b, h) in enumerate(items):
        s_ref[b, h] = s_new[i]
    y_full = jnp.concatenate([jnp.concatenate(y[b * N_HEADS_B:(b + 1) * N_HEADS_B], axis=1) for b in range(nb)],
                             axis=0)
    inv_hd = 1.0 / HEAD_DIM_B
    dev = y_full - head_sum(y_full) * inv_hd
    yn = dev * lax.rsqrt(head_sum(dev * dev) * inv_hd + LN_X_EPS) * lnw + lnb
    out = (yn + head_sum(r * k_h * rk) * v) * g
    for b in range(nb):
        o_ref[b] = out[b * C:(b + 1) * C, :]

    @pl.when(c == pl.num_programs(0) - 1)
    def _():
        st_ref[...] = s_ref[...]


def _wkv_call(feat, p):
    b, t, _ = feat.shape
    C = WKV_CHUNK
    nc = t // C
    row = lambda n: pl.BlockSpec((1, n), lambda c: (0, 0))
    mat = lambda m, n: pl.BlockSpec((m, n), lambda c: (0, 0))
    return pl.pallas_call(
        _wkv_body,
        out_shape=(jax.ShapeDtypeStruct((b, t, D_B), F32),
                   jax.ShapeDtypeStruct((b, N_HEADS_B, HEAD_DIM_B, HEAD_DIM_B), F32)),
        grid=(nc,),
        in_specs=[pl.BlockSpec((b, C, D_SHIFT_B), lambda c: (0, c, 0)),
                  pl.BlockSpec((b, 8, D_SHIFT_B), lambda c: (0, jnp.maximum(c * (C // 8) - 1, 0), 0)),
                  row(D_SHIFT_B), row(D_B), mat(DECAY_LORA, D_B), row(D_B), mat(AAA_LORA, D_B),
                  mat(GATE_LORA, D_B), row(D_B), row(D_B), row(D_B), row(D_B), row(D_B)],
        out_specs=(pl.BlockSpec((b, C, D_B), lambda c: (0, c, 0)),
                   pl.BlockSpec((b, N_HEADS_B, HEAD_DIM_B, HEAD_DIM_B), lambda c: (0, 0, 0, 0))),
        scratch_shapes=[pltpu.VMEM((b, N_HEADS_B, HEAD_DIM_B, HEAD_DIM_B), F32)],
        compiler_params=_cparams(("arbitrary",)),
        name="wkv",
    )(feat, feat, p['mu_b'], p['w0_b'], p['w_w2_b'], p['a0_b'], p['w_a2_b'], p['w_g2_b'],
      p['k_k_b'], p['k_a_b'], p['r_k_b'], p['ln_x_w_b'], p['ln_x_b_b'])


def _swkv_prep_body(f_ref, sh_ref, mu_ref, w0_ref, ww2_ref, a0_ref, wa2_ref, wg2_ref, kk_ref, ka_ref,
                    r_ref, w_ref, k_ref, v_ref, aa_ref, bb_ref, g_ref):
    f = f_ref[...]
    xs = f + mu_ref[...] * (sh_ref[...] - f)
    r, k, v, w_log, a, g, k_h = _rwkv_features(xs, w0_ref[...], ww2_ref[...], a0_ref[...],
                                               wa2_ref[...], wg2_ref[...], ka_ref[...])
    kk = k * kk_ref[...]
    kkn = jnp.concatenate([_head_norm(kk[:, h * HEAD_DIM_B:(h + 1) * HEAD_DIM_B]) for h in range(N_HEADS_B)],
                          axis=1)
    r_ref[...] = r
    w_ref[...] = jnp.exp(-jnp.exp(w_log))
    k_ref[...] = k_h
    v_ref[...] = v
    aa_ref[...] = -kkn
    bb_ref[...] = kkn * a
    g_ref[...] = g


def _swkv_prep_call(feat_s, shift0, p):
    n = feat_s.shape[0]
    full = lambda a: pl.BlockSpec(a.shape, lambda: tuple(0 for _ in a.shape))
    args = (feat_s, shift0, p['mu_b'], p['w0_b'], p['w_w2_b'], p['a0_b'], p['w_a2_b'], p['w_g2_b'],
            p['k_k_b'], p['k_a_b'])
    return pl.pallas_call(
        _swkv_prep_body,
        out_shape=tuple(jax.ShapeDtypeStruct((n, D_B), F32) for _ in range(7)),
        in_specs=[full(a) for a in args],
        out_specs=tuple(pl.BlockSpec((n, D_B), lambda: (0, 0)) for _ in range(7)),
        compiler_params=pltpu.CompilerParams(vmem_limit_bytes=VMEM_LIMIT),
        name="swkv_prep",
    )(*args)


def _swkv_step_body(s_ref, a_ref, w_ref, b_ref, k_ref, r_ref, v_ref, so_ref, y_ref):
    s = s_ref[...]
    th = s.shape[0]
    pad_sq = lambda z: jnp.concatenate(
        [jnp.concatenate([z, jnp.zeros((z.shape[0], LANES - z.shape[1]), F32)], axis=1),
         jnp.zeros((LANES - z.shape[0], LANES), F32)], axis=0)
    v_t = pad_sq(v_ref[...]).T
    v_col = jnp.stack([v_t[:HEAD_DIM_B, j:j + 1] for j in range(th)], axis=0)
    sa = jnp.sum(s * a_ref[...], axis=-1, keepdims=True)
    s2 = s * w_ref[...] + sa * b_ref[...] + v_col * k_ref[...]
    so_ref[...] = s2
    y = jnp.sum(s2 * r_ref[...], axis=-1, keepdims=True)
    y_t = jnp.concatenate([y[j] for j in range(th)], axis=1)
    y_ref[...] = pad_sq(y_t).T[:th, :HEAD_DIM_B]


def _swkv_step_call(s0, aa, w, bb, k, r, v):
    nh = s0.shape[0]
    th = 64
    rowspec = pl.BlockSpec((th, 1, HEAD_DIM_B), lambda i: (i, 0, 0))
    matspec = pl.BlockSpec((th, HEAD_DIM_B), lambda i: (i, 0))
    stspec = pl.BlockSpec((th, HEAD_DIM_B, HEAD_DIM_B), lambda i: (i, 0, 0))
    return pl.pallas_call(
        _swkv_step_body,
        out_shape=(jax.ShapeDtypeStruct((nh, HEAD_DIM_B, HEAD_DIM_B), F32),
                   jax.ShapeDtypeStruct((nh, HEAD_DIM_B), F32)),
        grid=(nh // th,),
        in_specs=[stspec, rowspec, rowspec, rowspec, rowspec, rowspec, matspec],
        out_specs=(stspec, matspec),
        compiler_params=_cparams(("arbitrary",)),
        name="swkv_step",
    )(s0, aa, w, bb, k, r, v)


def _swkv_fin_body(y_ref, r_ref, k_ref, v_ref, g_ref, rk_ref, lnw_ref, lnb_ref, o_ref):
    y, r, k, v, g = y_ref[...], r_ref[...], k_ref[...], v_ref[...], g_ref[...]
    rk, lnw, lnb = rk_ref[...], lnw_ref[...], lnb_ref[...]
    outs = []
    for h in range(N_HEADS_B):
        sl = slice(h * HEAD_DIM_B, (h + 1) * HEAD_DIM_B)
        outs.append(_wkv_finish_head(y[:, sl], r[:, sl], k[:, sl], v[:, sl], g[:, sl],
                                     rk[:, sl], lnw[:, sl], lnb[:, sl]))
    o_ref[...] = jnp.concatenate(outs, axis=1)


def _swkv_fin_call(y, r, k, v, g, p):
    n = y.shape[0]
    args = (y, r, k, v, g, p['r_k_b'], p['ln_x_w_b'], p['ln_x_b_b'])
    full = lambda a: pl.BlockSpec(a.shape, lambda: (0, 0))
    return pl.pallas_call(
        _swkv_fin_body,
        out_shape=jax.ShapeDtypeStruct((n, D_B), F32),
        in_specs=[full(a) for a in args],
        out_specs=pl.BlockSpec((n, D_B), lambda: (0, 0)),
        name="swkv_fin",
    )(*args)


def _route_t(scores, bias_col):
    n = scores.shape[1]
    gsz = N_EXPERTS // N_EXPERT_GROUPS
    choice = scores + bias_col
    ninf = -jnp.inf
    sid = lax.broadcasted_iota(I32, (gsz, n), 0)
    gs = []
    for gidx in range(N_EXPERT_GROUPS):
        blk = choice[gidx * gsz:(gidx + 1) * gsz, :]
        m1 = jnp.max(blk, axis=0, keepdims=True)
        first = jnp.min(jnp.where(blk == m1, sid, gsz), axis=0, keepdims=True)
        m2 = jnp.max(jnp.where(sid == first, ninf, blk), axis=0, keepdims=True)
        gs.append(m1 + m2)
    cur = jnp.concatenate(gs, axis=0)
    gid = lax.broadcasted_iota(I32, (N_EXPERT_GROUPS, n), 0)
    gmask = jnp.zeros((N_EXPERT_GROUPS, n), F32)
    for _ in range(TOPK_GROUPS):
        m = jnp.max(cur, axis=0, keepdims=True)
        first = jnp.min(jnp.where(cur == m, gid, N_EXPERT_GROUPS), axis=0, keepdims=True)
        sel = gid == first
        gmask = jnp.where(sel, 1.0, gmask)
        cur = jnp.where(sel, ninf, cur)
    emask = jnp.concatenate([jnp.broadcast_to(gmask[gidx:gidx + 1, :], (gsz, n))
                             for gidx in range(N_EXPERT_GROUPS)], axis=0)
    cur = jnp.where(emask > 0.5, choice, ninf)
    eid = lax.broadcasted_iota(I32, (N_EXPERTS, n), 0)
    selm = jnp.zeros((N_EXPERTS, n), F32)
    for _ in range(TOP_K):
        m = jnp.max(cur, axis=0, keepdims=True)
        first = jnp.min(jnp.where(cur == m, eid, N_EXPERTS), axis=0, keepdims=True)
        sel = eid == first
        selm = jnp.where(sel, 1.0, selm)
        cur = jnp.where(sel, ninf, cur)
    w = jnp.where(selm > 0.5, scores, 0.0)
    w = w / jnp.sum(w, axis=0, keepdims=True) * ROUTED_SCALE
    return jnp.where(selm > 0.5, w, -1.0)


def _unpermute(blk_ref, scr_ref, dil, tm):
    if dil == 1:
        return blk_ref[0, 0].astype(F32)
    n_chunks = scr_ref.shape[0]
    for r in range(dil):
        rows = blk_ref[0, r].astype(F32)
        for j in range(n_chunks):
            scr_ref[j, pl.ds(r, tm // dil, stride=dil), :] = rows[:, j * LANES:(j + 1) * LANES]
    return jnp.concatenate([scr_ref[j] for j in range(n_chunks)], axis=1)


def _post_body(*refs, combine, dils):
    if combine:
        o_refs, l_refs, rest = refs[:3], refs[3:6], refs[6:]
    else:
        o_refs, rest = refs[:1], refs[1:]
    (ob_ref, gt_ref, x_ref, g1_ref, sc2_ref, sh2_ref, npost_ref, npre_ref, wa_ref, wb_ref, wo_ref,
     wrt_ref, rb_ref, x1_ref, hp_ref, wt_ref) = rest[:16]
    scr = rest[16:]
    tm = x_ref.shape[1]
    if combine:
        os_, ls_ = [], []
        si = 0
        for gi, dil in enumerate(dils):
            os_.append(_unpermute(o_refs[gi], scr[si] if dil > 1 else None, dil, tm))
            ls_.append(_unpermute(l_refs[gi], scr[si + 1] if dil > 1 else None, dil, tm))
            si += 2 if dil > 1 else 0
        mx = jnp.maximum(jnp.maximum(ls_[0], ls_[1]), ls_[2])
        es = [jnp.exp(z - mx) for z in ls_]
        o_a = (es[0] * os_[0] + es[1] * os_[1] + es[2] * os_[2]) / (es[0] + es[1] + es[2])
    else:
        o_a = o_refs[0][0]
    gt = gt_ref[0].astype(F32)
    za = _dot(o_a.astype(BF16), wa_ref[...])
    zb = _dot(ob_ref[0].astype(BF16), wb_ref[...])
    merged = gt[:, :D_MODEL] * za + gt[:, D_MODEL:] * zb
    z = _dot(merged.astype(BF16), wo_ref[...])
    x1 = x_ref[0] + g1_ref[0] * _rms(z, npost_ref[...])
    x1_ref[0] = x1
    h2 = _rms(x1, npre_ref[...]) * (1.0 + sc2_ref[0]) + sh2_ref[0]
    packed = _pack_pairs(h2)
    for s in range(ROW_TILE_SUBLANES):
        hp_ref[0, pl.ds(s, tm, stride=ROW_TILE_SUBLANES), :] = packed[:, s * LANES:(s + 1) * LANES]
    tp =-(-tm // LANES) * LANES
    if tp != tm:
        h2 = jnp.concatenate([h2, jnp.zeros((tp - tm, D_MODEL), F32)], axis=0)
    logits_t = _dot_nt_split(wrt_ref[...], h2)
    w = _route_t(_sigmoid(logits_t[:N_EXPERTS, :]), rb_ref[...])
    wt_ref[...] = w[:, :tm]


def _post_call(o_parts, lse_parts, ob, gates, x, gate1, scale2, shift2, p, wa, wb, wo, wrt, rb, tm, mod_per_row):
    nb, t, _ = x.shape
    nt = t // tm
    combine = lse_parts is not None
    rowblk = lambda width: pl.BlockSpec((1, tm, width), lambda b, i: (b, i, 0))
    if mod_per_row:
        mod_spec = rowblk(D_MODEL)
    else:
        mod_spec = pl.BlockSpec((1, 1, D_MODEL), lambda b, i: (b, 0, 0))
    const = lambda shp: pl.BlockSpec(shp, lambda b, i: (0, 0))
    scratch = []
    if combine:
        dils = tuple(o.shape[1] for o in o_parts)
        o_args = list(o_parts) + list(lse_parts)
        o_specs = [pl.BlockSpec((1, d, tm // d, D_GROUP_A), lambda b, i: (b, 0, i, 0)) for d in dils] * 2
        for d in dils:
            if d > 1:
                scratch += [pltpu.VMEM((D_GROUP_A // LANES, tm, LANES), F32)] * 2
    else:
        dils = ()
        o_args = [o_parts[0]]
        o_specs = [rowblk(D_GROUP_A)]
    return pl.pallas_call(
        functools.partial(_post_body, combine=combine, dils=dils),
        out_shape=(jax.ShapeDtypeStruct((nb, t, D_MODEL), F32),
                   jax.ShapeDtypeStruct((nb, t * ROW_TILE_SUBLANES, LANES), I32),
                   jax.ShapeDtypeStruct((N_EXPERTS, nb * t), F32)),
        grid=(nb, nt),
        in_specs=o_specs + [rowblk(D_B), rowblk(2 * D_MODEL), rowblk(D_MODEL),
                            mod_spec, mod_spec, mod_spec, const((1, D_MODEL)), const((1, D_MODEL)),
                            const((D_GROUP_A, D_MODEL)), const((D_B, D_MODEL)), const((D_MODEL, D_MODEL)),
                            const((LANES, D_MODEL)), const((N_EXPERTS, 1))],
        out_specs=(rowblk(D_MODEL),
                   pl.BlockSpec((1, tm * ROW_TILE_SUBLANES, LANES), lambda b, i: (b, i, 0)),
                   pl.BlockSpec((N_EXPERTS, tm), lambda b, i: (0, b * nt + i))),
        scratch_shapes=scratch,
        compiler_params=_cparams(("arbitrary", "arbitrary")),
        name="post",
    )(*o_args, ob, gates, x, gate1, scale2, shift2, p['norm_post_mix'].reshape(1, -1),
      p['norm_pre_ffn'].reshape(1, -1), wa, wb, wo, wrt, rb)


def _rank_body(w_ref, dest_ref, w8_ref, tab_ref, etab_ref, cnt_ref, pst_ref, run_ref, *, n_real, n_slots):
    ph = pl.program_id(0)
    i = pl.program_id(1)
    T = MOE_TILE
    w = w_ref[...]
    sel = (w >= 0.0).astype(F32)
    cnt_tile = jnp.broadcast_to(jnp.sum(sel, axis=1, keepdims=True), (N_EXPERTS, LANES))
    ei = lax.broadcasted_iota(I32, (N_EXPERTS, N_EXPERTS), 0)
    ej = lax.broadcasted_iota(I32, (N_EXPERTS, N_EXPERTS), 1)

    @pl.when((ph == 0) & (i == 0))
    def _():
        cnt_ref[...] = jnp.zeros_like(cnt_ref)

    @pl.when(ph == 0)
    def _():
        cnt_ref[...] += cnt_tile

    @pl.when((ph == 1) & (i == 0))
    def _():
        cnt = cnt_ref[...]
        padded = jnp.floor((cnt + (EXPERT_BLOCK - 1)) / EXPERT_BLOCK) * EXPERT_BLOCK
        pstart = _dot_exact((ej < ei).astype(F32), padded)
        pst_ref[...] = pstart
        run_ref[...] = jnp.zeros_like(run_ref)
        pend = pstart + padded
        vend = pstart + cnt
        esub = lax.broadcasted_iota(I32, (N_EXPERTS, LANES), 0)
        lane = lax.broadcasted_iota(I32, (1, LANES), 1)
        tab_ref[...] = jnp.zeros_like(tab_ref)
        for c in range(tab_ref.shape[1] // LANES):
            bs = ((c * LANES + lane) * EXPERT_BLOCK).astype(F32)
            be = jnp.minimum(jnp.sum((pend <= bs).astype(F32), axis=0, keepdims=True), N_EXPERTS - 1.0)
            tab_ref[0:1, c * LANES:(c + 1) * LANES] = be.astype(I32)
            tab_ref[1:2, c * LANES:(c + 1) * LANES] = (pend[N_EXPERTS - 1:, :] / EXPERT_BLOCK).astype(I32)
        on_diag = esub == lax.broadcasted_iota(I32, (N_EXPERTS, LANES), 1)
        etab_ref[...] = jnp.zeros_like(etab_ref)
        lo = jnp.sum(jnp.where(on_diag, vend, 0.0), axis=0, keepdims=True)
        hi = jnp.sum(jnp.where(on_diag, pend, 0.0), axis=0, keepdims=True)
        etab_ref[0:1, :] = jnp.where(lane == N_EXPERTS, pend[N_EXPERTS - 1:, :], lo).astype(I32)
        etab_ref[1:2, :] = jnp.where(lane == N_EXPERTS, float(n_slots), hi).astype(I32)

    @pl.when(ph == 1)
    def _():
        ti = lax.broadcasted_iota(I32, (T, T), 0)
        tj = lax.broadcasted_iota(I32, (T, T), 1)
        selb = sel.astype(BF16)
        rank = _dot(selb, (ti < tj).astype(BF16))
        ordn = _dot((ej < ei).astype(BF16), selb)
        dest_e = pst_ref[:, :1] + run_ref[:, :1] + rank
        run_ref[...] += cnt_tile
        tok = i * T + lax.broadcasted_iota(I32, (1, T), 1)
        dks, wks = [], []
        for k in range(TOP_K):
            m = (sel > 0.5) & (ordn == float(k))
            dk = jnp.sum(jnp.where(m, dest_e, 0.0), axis=0, keepdims=True)
            wk = jnp.sum(jnp.where(m, w, 0.0), axis=0, keepdims=True)
            dks.append(jnp.where(tok < n_real, dk, 0.0))
            wks.append(jnp.where(tok < n_real, wk, 0.0))
        dest_ref[...] = jnp.concatenate(dks, axis=0).astype(I32)
        w8_ref[...] = jnp.concatenate(wks, axis=0)


def _rank_call(w_t, n_real, n_blocks, n_blocks_pad):
    n = w_t.shape[1]
    nt = n // MOE_TILE
    return pl.pallas_call(
        functools.partial(_rank_body, n_real=n_real, n_slots=n_blocks * EXPERT_BLOCK),
        out_shape=(jax.ShapeDtypeStruct((TOP_K, n), I32),
                   jax.ShapeDtypeStruct((TOP_K, n), F32),
                   jax.ShapeDtypeStruct((8, n_blocks_pad), I32),
                   jax.ShapeDtypeStruct((8, LANES), I32)),
        grid=(2, nt),
        in_specs=[pl.BlockSpec((N_EXPERTS, MOE_TILE), lambda ph, i: (0, i))],
        out_specs=(pl.BlockSpec((TOP_K, MOE_TILE), lambda ph, i: (0, i * ph)),
                   pl.BlockSpec((TOP_K, MOE_TILE), lambda ph, i: (0, i * ph)),
                   pl.BlockSpec((8, n_blocks_pad), lambda ph, i: (0, 0)),
                   pl.BlockSpec((8, LANES), lambda ph, i: (0, 0))),
        scratch_shapes=[pltpu.VMEM((N_EXPERTS, LANES), F32)] * 3,
        compiler_params=_cparams(("arbitrary", "arbitrary")),
        name="rank",
    )(w_t)


def _tile_rows(ref, row, n):
    return ref.at[pl.ds(pl.multiple_of(row * ROW_TILE_SUBLANES, ROW_TILE_SUBLANES), n * ROW_TILE_SUBLANES)]


def _zero_fill(etab_ref, zbuf, xs_hbm, zsem, wait):
    def go(src, dst):
        cp = pltpu.make_async_copy(src, dst, zsem)
        if wait:
            cp.wait()
        else:
            cp.start()

    def per_range(e, carry):
        lo = etab_ref[0, e]
        n = etab_ref[1, e] - lo
        n_full = n // ZERO_ROWS

        def full(j, c):
            go(zbuf, _tile_rows(xs_hbm, lo + j * ZERO_ROWS, ZERO_ROWS))
            return c

        lax.fori_loop(0, n_full, full, 0)
        pos = lo + n_full * ZERO_ROWS
        rem = n - n_full * ZERO_ROWS
        size = ZERO_ROWS // 2
        while size >= 1:
            bit = rem & size

            @pl.when(bit != 0)
            def _(size=size, pos=pos):
                go(_tile_rows(zbuf, 0, size), _tile_rows(xs_hbm, pos, size))

            pos = pos + bit
            size //= 2
        return carry

    lax.fori_loop(0, N_EXPERTS + 1, per_range, 0)


def _dispatch_body(dest_ref, etab_ref, xa_ref, xb_ref, xs_hbm, zbuf, sem, zsem, *, n_real, n_full):
    i = pl.program_id(0)
    T = MOE_TILE
    n_tok = jnp.clip(n_real - i * T, 0, T)

    def issue_from(x_ref):
        def issue(t, carry):
            for k in range(TOP_K):
                pltpu.make_async_copy(_tile_rows(x_ref, t, 1), _tile_rows(xs_hbm, dest_ref[k * T + t], 1),
                                      sem).start(priority=k % 2)
            return carry

        lax.fori_loop(0, n_tok, issue, 0)

    @pl.when(i < n_full)
    def _():
        issue_from(xa_ref)

    @pl.when(i >= n_full)
    def _():
        issue_from(xb_ref)

    @pl.when(i == 0)
    def _():
        zbuf[...] = jnp.zeros_like(zbuf)
        _zero_fill(etab_ref, zbuf, xs_hbm, zsem, wait=False)
        _zero_fill(etab_ref, zbuf, xs_hbm, zsem, wait=True)

    @pl.when(n_tok == T)
    def _():
        pltpu.make_async_copy(_tile_rows(xs_hbm, 0, T * TOP_K), _tile_rows(xs_hbm, 0, T * TOP_K), sem).wait()

    @pl.when(n_tok < T)
    def _():
        def drain(j, carry):
            pltpu.make_async_copy(_tile_rows(xs_hbm, 0, 1), _tile_rows(xs_hbm, 0, 1), sem).wait()
            return carry

        lax.fori_loop(0, n_tok * TOP_K, drain, 0)


def _dispatch_call(dest, etab, hp_a, hp_b, n_real, n_slots):
    tile_rows = MOE_TILE * ROW_TILE_SUBLANES
    n_full = hp_a.shape[0] // tile_rows
    return pl.pallas_call(
        functools.partial(_dispatch_body, n_real=n_real, n_full=n_full),
        out_shape=jax.ShapeDtypeStruct((n_slots * ROW_TILE_SUBLANES, LANES), I32),
        grid=(n_full + 1,),
        in_specs=[pl.BlockSpec((TOP_K * MOE_TILE,), lambda i: (i,), memory_space=pltpu.SMEM),
                  pl.BlockSpec((8, LANES), lambda i: (0, 0), memory_space=pltpu.SMEM),
                  pl.BlockSpec((tile_rows, LANES), lambda i: (jnp.minimum(i, n_full - 1), 0)),
                  pl.BlockSpec((tile_rows, LANES), lambda i: (0, 0))],
        out_specs=pl.BlockSpec(memory_space=pl.ANY),
        scratch_shapes=[pltpu.VMEM((ZERO_ROWS * ROW_TILE_SUBLANES, LANES), I32),
                        pltpu.SemaphoreType.DMA, pltpu.SemaphoreType.DMA],
        compiler_params=_cparams(("arbitrary",)),
        name="dispatch",
    )(dest, etab, hp_a, hp_b)


def _rows_from_tiles(ref, lo, n):
    return jnp.concatenate([ref[pl.ds(lo * ROW_TILE_SUBLANES + s, n, stride=ROW_TILE_SUBLANES), :]
                            for s in range(ROW_TILE_SUBLANES)], axis=1)


def _ffn_body(be_ref, nu_ref, xs_ref, wg_hbm, wu_hbm, wd_hbm, ys_ref, wgf, wuf, wdf, wgb, wub, wdb, slot_ref, sem):
    j = pl.program_id(0)
    n_used = nu_ref[0]

    def fetch(e, slot):
        return [pltpu.make_async_copy(w_hbm.at[e], wf.at[slot], sem.at[slot])
                for w_hbm, wf in ((wg_hbm, wgf), (wu_hbm, wuf), (wd_hbm, wdf))]

    @pl.when(j < n_used)
    def _():
        e = be_ref[j]

        @pl.when(j == 0)
        def _():
            slot_ref[0] = 0
            for cp in fetch(e, 0):
                cp.start()

        @pl.when((j == 0) | (e != be_ref[jnp.maximum(j - 1, 0)]))
        def _():
            slot = slot_ref[0]
            for cp in fetch(e, slot):
                cp.wait()
            wgb[...] = wgf[slot].astype(BF16)
            wub[...] = wuf[slot].astype(BF16)
            wdb[...] = wdf[slot].astype(BF16)
            last = be_ref.shape[0] - 1
            nxt = lax.while_loop(lambda i: (i < n_used) & (be_ref[jnp.minimum(i, last)] == e), lambda i: i + 1, j + 1)

            @pl.when(nxt < n_used)
            def _():
                for cp in fetch(be_ref[jnp.minimum(nxt, last)], 1 - slot):
                    cp.start()

            slot_ref[0] = 1 - slot

        x = _unpack_pairs(_rows_from_tiles(xs_ref, 0, EXPERT_BLOCK)).astype(BF16)
        act = _silu(_dot(x, wgb[...])) * _dot(x, wub[...])
        y = _dot(act.astype(BF16), wdb[...])
        packed = _pack_pairs(y)
        for s in range(ROW_TILE_SUBLANES):
            ys_ref[pl.ds(s, EXPERT_BLOCK, stride=ROW_TILE_SUBLANES), :] = packed[:, s * LANES:(s + 1) * LANES]

    @pl.when(j >= nu_ref[0])
    def _():
        ys_ref[...] = jnp.zeros_like(ys_ref)


def _ffn_call(blk_e, n_used, xs, w_gate, w_up, w_down, n_blocks):
    tile_blk = pl.BlockSpec((EXPERT_BLOCK * ROW_TILE_SUBLANES, LANES), lambda j, be, nu: (j, 0))
    last = lambda j, nu: jnp.minimum(j, nu[0] - 1)
    grid_spec = pltpu.PrefetchScalarGridSpec(
        num_scalar_prefetch=2,
        grid=(n_blocks,),
        in_specs=[pl.BlockSpec((EXPERT_BLOCK * ROW_TILE_SUBLANES, LANES), lambda j, be, nu: (last(j, nu), 0)),
                  pl.BlockSpec(memory_space=pl.ANY), pl.BlockSpec(memory_space=pl.ANY),
                  pl.BlockSpec(memory_space=pl.ANY)],
        out_specs=tile_blk,
        scratch_shapes=[pltpu.VMEM((2, D_MODEL, D_EXPERT), F32), pltpu.VMEM((2, D_MODEL, D_EXPERT), F32),
                        pltpu.VMEM((2, D_EXPERT, D_MODEL), F32),
                        pltpu.VMEM((D_MODEL, D_EXPERT), BF16), pltpu.VMEM((D_MODEL, D_EXPERT), BF16),
                        pltpu.VMEM((D_EXPERT, D_MODEL), BF16), pltpu.SMEM((1,), I32), pltpu.SemaphoreType.DMA((2,))])
    return pl.pallas_call(
        _ffn_body,
        out_shape=jax.ShapeDtypeStruct((n_blocks * EXPERT_BLOCK * ROW_TILE_SUBLANES, LANES), I32),
        grid_spec=grid_spec,
        compiler_params=_cparams(("arbitrary",)),
        name="ffn",
    )(blk_e, n_used, xs, w_gate, w_up, w_down)


def _combine_body(dest_ref, dnext_ref, w8_ref, xa_ref, xb_ref, x1a_ref, x1b_ref, g2a_ref, g2b_ref, gain_ref,
                  sg_ref, su_ref, sd_ref, ys_hbm, oa_ref, ob_ref, buf, sem):
    j = pl.program_id(0)
    T = COMBINE_TILE
    RC = COMBINE_ROWS

    def issue(d_ref, slot, t):
        for k in range(TOP_K):
            pltpu.make_async_copy(_tile_rows(ys_hbm, d_ref[k * T + t], 1), _tile_rows(buf.at[slot], k * T + t, 1),
                                  sem.at[slot]).start(priority=k % 2)

    def wait(slot):
        pltpu.make_async_copy(_tile_rows(ys_hbm, 0, T * TOP_K), buf.at[slot], sem.at[slot]).wait()

    def step(slot):
        is_tail = j == 0
        wait(slot)
        for t in range(RC):
            issue(dnext_ref, 1 - slot, t)
        x = _unpack_pairs(jnp.where(is_tail, _rows_from_tiles(xb_ref, 0, T),
                                    _rows_from_tiles(xa_ref, 0, T))).astype(BF16)
        shared = _dot((_silu(_dot(x, sg_ref[...])) * _dot(x, su_ref[...])).astype(BF16), sd_ref[...])
        w_t = jnp.concatenate([w8_ref[...], jnp.zeros((LANES - TOP_K, T), F32)], axis=0).T
        oa_ref[...] = shared
        for r0 in range(0, T, RC):
            if r0 > 0:
                for t in range(r0, r0 + RC):
                    issue(dnext_ref, 1 - slot, t)
            acc = oa_ref[r0:r0 + RC, :]
            for k in range(TOP_K):
                acc = acc + w_t[r0:r0 + RC, k:k + 1] * _unpack_pairs(_rows_from_tiles(buf.at[slot], k * T + r0, RC))
            x1 = jnp.where(is_tail, x1b_ref[r0:r0 + RC, :], x1a_ref[r0:r0 + RC, :])
            g2 = jnp.where(is_tail, g2b_ref[r0:r0 + RC, :], g2a_ref[0])
            oa_ref[r0:r0 + RC, :] = x1 + g2 * _rms(acc, gain_ref[...])

        @pl.when(is_tail)
        def _():
            ob_ref[...] = oa_ref[...]

        @pl.when(j + 1 == pl.num_programs(0))
        def _():
            wait(1 - slot)

    @pl.when(j == 0)
    def _():
        lax.fori_loop(0, T, lambda t, c: (issue(dest_ref, 0, t), c)[1], 0, unroll=2)

    @pl.when(j % 2 == 0)
    def _():
        step(0)

    @pl.when(j % 2 == 1)
    def _():
        step(1)


def _combine_call(dest, w8, hp_a, hp_b, x1_a, x1_b, gate2_a, gate2_b, gain, wsg, wsu, wsd, ys):
    T = COMBINE_TILE
    tile_rows = T * ROW_TILE_SUBLANES
    n_full = hp_a.shape[0] // tile_rows
    n_tiles = n_full + 1
    seq = x1_a.shape[0] // gate2_a.shape[0]
    tile_of = lambda j: jnp.where(j == 0, n_full, j - 1)
    full_of = lambda j: jnp.maximum(j - 1, 0)
    const = lambda shp: pl.BlockSpec(shp, lambda j: (0, 0))
    return pl.pallas_call(
        _combine_body,
        out_shape=(jax.ShapeDtypeStruct((n_full * T, D_MODEL), F32), jax.ShapeDtypeStruct((T, D_MODEL), F32)),
        grid=(n_tiles,),
        in_specs=[pl.BlockSpec((TOP_K * T,), lambda j: (tile_of(j),), memory_space=pltpu.SMEM),
                  pl.BlockSpec((TOP_K * T,), lambda j: (tile_of(jnp.minimum(j + 1, n_tiles - 1)),),
                               memory_space=pltpu.SMEM),
                  pl.BlockSpec((TOP_K, T), lambda j: (0, tile_of(j))),
                  pl.BlockSpec((tile_rows, LANES), lambda j: (full_of(j), 0)),
                  pl.BlockSpec((tile_rows, LANES), lambda j: (0, 0)),
                  pl.BlockSpec((T, D_MODEL), lambda j: (full_of(j), 0)),
                  const((T, D_MODEL)),
                  pl.BlockSpec((1, 1, D_MODEL), lambda j: (full_of(j) * T // seq, 0, 0)),
                  const((T, D_MODEL)), const((1, D_MODEL)),
                  const((D_MODEL, D_EXPERT)), const((D_MODEL, D_EXPERT)), const((D_EXPERT, D_MODEL)),
                  pl.BlockSpec(memory_space=pl.ANY)],
        out_specs=(pl.BlockSpec((T, D_MODEL), lambda j: (full_of(j), 0)), const((T, D_MODEL))),
        scratch_shapes=[pltpu.VMEM((2, TOP_K * tile_rows, LANES), I32), pltpu.SemaphoreType.DMA((2,))],
        compiler_params=_cparams(("arbitrary",)),
        name="combine",
    )(dest, dest, w8, hp_a, hp_b, x1_a, x1_b, gate2_a, gate2_b, gain.reshape(1, -1), wsg, wsu, wsd, ys)


def _rope_tables(pos):
    half = HEAD_DIM_A // 2
    inv_freq = ROPE_THETA ** (-jnp.arange(half, dtype=F32) / half)
    ang = pos.astype(F32)[:, None] * inv_freq[None, :]
    cos = jnp.cos(ang)
    sin = jnp.sin(ang)
    reps = LANES // HEAD_DIM_A
    cos_t = jnp.tile(jnp.concatenate([cos, cos], axis=1), (1, reps))
    sin_t = jnp.tile(jnp.concatenate([-sin, sin], axis=1), (1, reps))
    return cos_t, sin_t


def kernel(x_prompt, x_sample, c_prompt, c_sample, cache_a1_kv, cache_a2_kv, cache_a3_kv, state_b_wkv, state_b_shift, w_ada, b_ada, norm_pre_mix, norm_post_mix, norm_pre_ffn, norm_post_ffn, w_in, w_a_out, mu_b, w0_b, w_w2_b, a0_b, w_a2_b, w_g2_b, k_k_b, k_a_b, r_k_b, ln_x_w_b, ln_x_b_b, w_b_out, w_out, w_router, router_bias, w_e_gate, w_e_up, w_e_down, w_s_gate, w_s_up, w_s_down):
    assert DEPTH == 1
    l = 0
    nd = DEC_BATCH
    row = lambda a: a.reshape(1, -1)
    p = {'mu_b': row(mu_b[l]), 'w0_b': row(w0_b[l]), 'w_w2_b': w_w2_b[l], 'a0_b': row(a0_b[l]),
         'w_a2_b': w_a2_b[l], 'w_g2_b': w_g2_b[l], 'k_k_b': row(k_k_b[l]), 'k_a_b': row(k_a_b[l]),
         'r_k_b': row(r_k_b[l]), 'ln_x_w_b': row(ln_x_w_b[l]), 'ln_x_b_b': row(ln_x_b_b[l]),
         'norm_post_mix': norm_post_mix[l], 'norm_pre_ffn': norm_pre_ffn[l]}

    wq, wf, wg = _wsplit_call(w_in[l])
    wa = w_a_out[l].astype(BF16)
    wb = w_b_out[l].astype(BF16)
    wo = w_out[l].astype(BF16)
    wrt = jnp.concatenate([w_router[l].T, jnp.zeros((LANES - N_EXPERTS, D_MODEL), F32)], axis=0)
    rb = router_bias[l].reshape(N_EXPERTS, 1)
    wsg, wsu, wsd = w_s_gate[l].astype(BF16), w_s_up[l].astype(BF16), w_s_down[l].astype(BF16)

    n_c = BATCH + nd
    c_all = jnp.concatenate([c_prompt, c_sample, jnp.zeros((-n_c % 8, D_MODEL), F32)], axis=0)
    mod = _mod_call(c_all, w_ada[l], b_ada[l])
    mod_p = [m.reshape(BATCH, 1, D_MODEL) for m in jnp.split(mod[:BATCH], 6, axis=-1)]
    mod_s = [m.reshape(1, nd, D_MODEL) for m in jnp.split(mod[BATCH:n_c], 6, axis=-1)]

    cos_p, sin_p = _rope_tables(jnp.arange(SEQ, dtype=I32))
    cos_s, sin_s = _rope_tables(jnp.full((nd,), PAST_LEN, I32))

    keep_p = [min(w, SEQ) for w, _ in DILATED_GROUPS]
    dils = tuple(d for _, d in DILATED_GROUPS)

    q0, q1, q2, feat_p, gates_p, *tails_p = _inproj_call(
        x_prompt, norm_pre_mix[l], mod_p[1], mod_p[0], cos_p, sin_p, wq, wf, wg,
        tm=256, keeps=keep_p, mod_per_row=False, dils=dils)
    o_parts, lse_parts = [], []
    for gi, qg in enumerate((q0, q1, q2)):
        o, lse = _attn_call(qg, gi)
        o_parts.append(o)
        lse_parts.append(lse)
    ob_p, wkv_p = _wkv_call(feat_p, p)
    x1_p, hp_p, wt_p = _post_call(o_parts, lse_parts, ob_p, gates_p, x_prompt, mod_p[2], mod_p[4], mod_p[3],
                                  p, wa, wb, wo, wrt, rb, tm=512, mod_per_row=False)

    xs3 = x_sample.reshape(1, nd, D_MODEL)
    s0, s1, s2, feat_s, gates_s, *tails_s = _inproj_call(
        xs3, norm_pre_mix[l], mod_s[1], mod_s[0], cos_s, sin_s, wq, wf, wg,
        tm=nd, keeps=(nd,) * N_GROUPS_A, mod_per_row=True, dils=(1, 1, 1))
    qkv_s = jnp.stack([z.reshape(nd, 3, N_HEADS_A, HEAD_DIM_A) for z in (s0, s1, s2)], axis=2)
    qkv_s = qkv_s.reshape(nd, 3 * N_GROUPS_A * N_HEADS_A, HEAD_DIM_A).astype(F32)
    oa_s = _sattn_call(qkv_s, cache_a1_kv[l], cache_a2_kv[l], cache_a3_kv[l])
    r_s, w_s, k_s, v_s, aa_s, bb_s, g_s = _swkv_prep_call(feat_s[0], state_b_shift[l], p)
    nh = nd * N_HEADS_B
    as_row = lambda a: a.reshape(nh, 1, HEAD_DIM_B)
    s_new, y_col = _swkv_step_call(state_b_wkv[l].reshape(nh, HEAD_DIM_B, HEAD_DIM_B), as_row(aa_s), as_row(w_s),
                                   as_row(bb_s), as_row(k_s), as_row(r_s), v_s.reshape(nh, HEAD_DIM_B))
    ob_s = _swkv_fin_call(y_col.reshape(nd, D_B), r_s, k_s, v_s, g_s, p)
    x1_s, hp_s, wt_s = _post_call([oa_s.reshape(1, nd, D_GROUP_A)], None, ob_s.reshape(1, nd, D_B), gates_s, xs3,
                                  mod_s[2], mod_s[4], mod_s[3], p, wa, wb, wo, wrt, rb, tm=nd, mod_per_row=True)

    n_p = BATCH * SEQ
    n_real = n_p + nd
    n_all = -(-n_real // MOE_TILE) * MOE_TILE
    pad = n_all - n_real
    n_blocks = -(-(n_real * TOP_K) // EXPERT_BLOCK) + N_EXPERTS
    n_blocks_pad = -(-n_blocks // LANES) * LANES
    assert n_p % MOE_TILE == 0 and nd <= MOE_TILE
    hp_a = hp_p.reshape(n_p * ROW_TILE_SUBLANES, LANES)
    hp_b = jnp.concatenate([hp_s[0], jnp.zeros((pad * ROW_TILE_SUBLANES, LANES), I32)], axis=0)
    wt_all = jnp.concatenate([wt_p, wt_s, jnp.full((N_EXPERTS, pad), -1.0, F32)], axis=1)
    dest8, w8, tab, etab = _rank_call(wt_all, n_real, n_blocks, n_blocks_pad)
    dest = dest8.reshape(TOP_K, n_all // MOE_TILE, MOE_TILE).transpose(1, 0, 2).reshape(-1)
    xs = _dispatch_call(dest, etab, hp_a, hp_b, n_real, n_blocks * EXPERT_BLOCK)
    ys = _ffn_call(tab[0], tab[1, :1], xs, w_e_gate[l], w_e_up[l], w_e_down[l], n_blocks)
    n_ct = n_p // COMBINE_TILE + 1
    dest_c = dest8[:, :n_ct * COMBINE_TILE].reshape(TOP_K, n_ct, COMBINE_TILE).transpose(1, 0, 2).reshape(-1)
    pad_rows = lambda z: jnp.concatenate([z, jnp.zeros((COMBINE_TILE - nd, D_MODEL), F32)], axis=0)
    out_p, out_s = _combine_call(dest_c, w8, hp_a, hp_b, x1_p.reshape(n_p, D_MODEL), pad_rows(x1_s[0]), mod_p[5],
                                 pad_rows(mod_s[5][0]), norm_post_ffn[l], wsg, wsu, wsd, ys)
    y_prompt = out_p.reshape(BATCH, SEQ, D_MODEL)
    y_sample = out_s[:nd]

    a_p = [z.reshape(1, BATCH, kp, 2, N_HEADS_A, HEAD_DIM_A) for z, kp in zip(tails_p, keep_p)]
    a_s = [z.reshape(1, nd, DEC_SEQ, 2, N_HEADS_A, HEAD_DIM_A) for z in tails_s]
    shift_p = feat_p[:, -1][None]
    shift_s = feat_s[0][None]
    return (y_prompt, y_sample.reshape(nd, DEC_SEQ, D_MODEL), a_p[0], a_p[1], a_p[2], wkv_p[None], shift_p,
            a_s[0], a_s[1], a_s[2], s_new.reshape(1, nd, N_HEADS_B, HEAD_DIM_B, HEAD_DIM_B), shift_s)
```

```python
import functools
import math

import jax
import jax.numpy as jnp
from jax import lax
from jax.experimental import pallas as pl
from jax.experimental.pallas import tpu as pltpu

F32 = jnp.float32
BF16 = jnp.bfloat16
I32 = jnp.int32

D_MODEL = 1024
BATCH = 2
SEQ = 8192
DEPTH = 1
DEC_BATCH = 32
DEC_SEQ = 1
PAST_LEN = 16384

HEAD_DIM_A = 64
N_HEADS_A = 8
DILATED_GROUPS = ((128, 1), (512, 4), (2048, 16))
N_GROUPS_A = 3
D_GROUP_A = N_HEADS_A * HEAD_DIM_A
D_A = N_GROUPS_A * D_GROUP_A
D_QKV = 3 * D_A
BAND_BLOCK = 128
ROPE_THETA = 10000.0

HEAD_DIM_B = 64
N_HEADS_B = 16
D_B = 1024
DECAY_LORA = 64
AAA_LORA = 64
GATE_LORA = 160
D_SHIFT_B = 3 * D_B + DECAY_LORA + AAA_LORA + GATE_LORA
LN_X_EPS = 64e-5

N_EXPERTS = 64
TOP_K = 8
N_EXPERT_GROUPS = 8
TOPK_GROUPS = 4
D_EXPERT = 256
ROUTED_SCALE = 2.5
EXPERT_BLOCK = 1024
NORM_EPS = 1e-6

LANES = 128
WKV_CHUNK = 64
MOE_TILE = 1024
COMBINE_TILE = 256
COMBINE_ROWS = 32
VMEM_LIMIT = 56 * 1024 * 1024
ROW_TILE_SUBLANES = D_MODEL // (2 * LANES)
FFN_RING = 3
ZERO_ROWS = 256


def _cparams(sem):
    return pltpu.CompilerParams(dimension_semantics=sem, vmem_limit_bytes=VMEM_LIMIT)


def _dot(a, b):
    return jnp.dot(a, b, preferred_element_type=F32)


def _dot_nt(a, b):
    return lax.dot_general(a, b, (((1,), (1,)), ((), ())), preferred_element_type=F32)


def _dot_tn(a, b):
    return lax.dot_general(a, b, (((0,), (0,)), ((), ())), preferred_element_type=F32)


def _dot_nt_split(a, b):
    ah = a.astype(BF16)
    al = (a - ah.astype(F32)).astype(BF16)
    bh = b.astype(BF16)
    bl = (b - bh.astype(F32)).astype(BF16)
    return _dot_nt(ah, bh) + _dot_nt(ah, bl) + _dot_nt(al, bh)


def _dot_exact(a, b):
    return lax.dot_general(a, b, (((1,), (0,)), ((), ())), precision=lax.Precision.HIGHEST,
                           preferred_element_type=F32)


def _rms(x, gain):
    return x * lax.rsqrt(jnp.mean(x * x, axis=-1, keepdims=True) + NORM_EPS) * gain


def _sigmoid(x):
    return 1.0 / (1.0 + jnp.exp(-x))


def _silu(x):
    return x * _sigmoid(x)


def _softplus(x):
    return jnp.maximum(x, 0.0) + jnp.log(1.0 + jnp.exp(-jnp.abs(x)))


def _pack_pairs(x):
    half = D_MODEL // 2
    lo = lax.bitcast_convert_type(x[:, :half].astype(BF16).astype(F32), I32)
    hi = lax.bitcast_convert_type(x[:, half:].astype(BF16).astype(F32), I32)
    return lax.shift_right_logical(lo, 16) | (hi & jnp.int32(-65536))


def _unpack_pairs(w):
    lo = lax.bitcast_convert_type(w << 16, F32)
    hi = lax.bitcast_convert_type(w & jnp.int32(-65536), F32)
    return jnp.concatenate([lo, hi], axis=1)


def _mod_body(c_ref, w_ref, b_ref, o_ref):
    s = _silu(c_ref[...]).astype(BF16)
    o_ref[...] = _dot(s, w_ref[...].astype(BF16)) + b_ref[...]


def _mod_call(c_all, w_ada, b_ada):
    rows = c_all.shape[0]
    tn = 1536
    return pl.pallas_call(
        _mod_body,
        out_shape=jax.ShapeDtypeStruct((rows, 6 * D_MODEL), F32),
        grid=(6 * D_MODEL // tn,),
        in_specs=[pl.BlockSpec((rows, D_MODEL), lambda j: (0, 0)),
                  pl.BlockSpec((D_MODEL, tn), lambda j: (0, j)),
                  pl.BlockSpec((1, tn), lambda j: (0, j))],
        out_specs=pl.BlockSpec((rows, tn), lambda j: (0, j)),
        compiler_params=_cparams(("arbitrary",)),
        name="mod",
    )(c_all, w_ada, b_ada.reshape(1, -1))


def _wsplit_body(w_ref, q_ref, f_ref, g_ref):
    w = w_ref[...]
    q_ref[...] = w[:, :D_QKV].astype(BF16)
    f_ref[...] = w[:, D_QKV:D_QKV + D_SHIFT_B].astype(BF16)
    g_ref[...] = w[:, D_QKV + D_SHIFT_B:].astype(BF16)


def _wsplit_call(w):
    rows, cols = w.shape
    tr = 128
    widths = (D_QKV, D_SHIFT_B, cols - D_QKV - D_SHIFT_B)
    return pl.pallas_call(
        _wsplit_body,
        out_shape=tuple(jax.ShapeDtypeStruct((rows, n), BF16) for n in widths),
        grid=(rows // tr,),
        in_specs=[pl.BlockSpec((tr, cols), lambda i: (i, 0))],
        out_specs=tuple(pl.BlockSpec((tr, n), lambda i: (i, 0)) for n in widths),
        compiler_params=_cparams(("arbitrary",)),
        name="wsplit",
    )(w)


def _inproj_body(x_ref, g_ref, sc_ref, sh_ref, cos_ref, sin_ref, wq_ref, wf_ref, wg_ref,
                 q0_ref, q1_ref, q2_ref, feat_ref, gate_ref, t0_ref, t1_ref, t2_ref, p_ref, *, dils):
    x = x_ref[0]
    tm = x.shape[0]
    h = _rms(x, g_ref[...]) * (1.0 + sc_ref[0]) + sh_ref[0]
    hb = h.astype(BF16)
    p = _dot(hb, wq_ref[...])
    cos = cos_ref[...]
    sin = sin_ref[...]
    lane = lax.broadcasted_iota(I32, cos.shape, 1)
    first_half = (lane % HEAD_DIM_A) < (HEAD_DIM_A // 2)
    for c in range(2 * D_A // LANES):
        xc = p[:, c * LANES:(c + 1) * LANES]
        partner = jnp.where(first_half, pltpu.roll(xc, LANES - HEAD_DIM_A // 2, 1),
                            pltpu.roll(xc, HEAD_DIM_A // 2, 1))
        rc = xc * cos + partner * sin
        if c < D_A // LANES:
            rc = rc * (HEAD_DIM_A ** -0.5)
        p_ref[c] = rc
    for c in range(2 * D_A // LANES, D_QKV // LANES):
        p_ref[c] = p[:, c * LANES:(c + 1) * LANES]
    per_group = D_GROUP_A // LANES
    for gi, t_ref in enumerate((t0_ref, t1_ref, t2_ref)):
        rows = t_ref.shape[1]
        for which in (1, 2):
            for j in range(per_group):
                c = (which * D_A + gi * D_GROUP_A) // LANES + j
                t_ref[0, :, (which - 1) * D_GROUP_A + j * LANES:(which - 1) * D_GROUP_A + (j + 1) * LANES] = \
                    p_ref[c, tm - rows:tm, :]
    for gi, (out_ref, dil) in enumerate(zip((q0_ref, q1_ref, q2_ref), dils)):
        for which in range(3):
            for j in range(per_group):
                c = (which * D_A + gi * D_GROUP_A) // LANES + j
                dst = slice(which * D_GROUP_A + j * LANES, which * D_GROUP_A + (j + 1) * LANES)
                if dil == 1:
                    out_ref[0, 0, :, dst] = p_ref[c].astype(BF16)
                else:
                    for r in range(dil):
                        out_ref[0, r, :, dst] = p_ref[c, pl.ds(r, tm // dil, stride=dil), :].astype(BF16)
    feat_ref[0] = _dot(hb, wf_ref[...])
    gate_ref[0] = _sigmoid(_dot(hb, wg_ref[...])).astype(BF16)


def _inproj_call(x, gain, scale, shift, cos_t, sin_t, wq, wf, wg, tm, keeps, mod_per_row, dils):
    nb, t, _ = x.shape
    nt = t // tm

    def tail_spec(keep):
        if keep <= tm:
            return pl.BlockSpec((1, keep, 2 * D_GROUP_A), lambda b, i: (b, 0, 0))
        first = (t - keep) // tm
        return pl.BlockSpec((1, tm, 2 * D_GROUP_A), lambda b, i: (b, jnp.maximum(i - first, 0), 0))

    if mod_per_row:
        mod_spec = pl.BlockSpec((1, tm, D_MODEL), lambda b, i: (b, i, 0))
    else:
        mod_spec = pl.BlockSpec((1, 1, D_MODEL), lambda b, i: (b, 0, 0))
    resident = lambda shp: pl.BlockSpec(shp, lambda b, i: (0, 0), pipeline_mode=pl.Buffered(1))
    q_shapes = tuple(jax.ShapeDtypeStruct((nb, d, t // d, 3 * D_GROUP_A), BF16) for d in dils)
    q_specs = tuple(pl.BlockSpec((1, d, tm // d, 3 * D_GROUP_A), lambda b, i: (b, 0, i, 0)) for d in dils)
    return pl.pallas_call(
        functools.partial(_inproj_body, dils=dils),
        out_shape=q_shapes + (jax.ShapeDtypeStruct((nb, t, D_SHIFT_B), F32),
                              jax.ShapeDtypeStruct((nb, t, 2 * D_MODEL), BF16),
                              ) + tuple(jax.ShapeDtypeStruct((nb, kp, 2 * D_GROUP_A), F32) for kp in keeps),
        grid=(nb, nt),
        in_specs=[pl.BlockSpec((1, tm, D_MODEL), lambda b, i: (b, i, 0)),
                  pl.BlockSpec((1, D_MODEL), lambda b, i: (0, 0)),
                  mod_spec, mod_spec,
                  pl.BlockSpec((tm, LANES), lambda b, i: (i, 0)),
                  pl.BlockSpec((tm, LANES), lambda b, i: (i, 0)),
                  resident((D_MODEL, D_QKV)), resident((D_MODEL, D_SHIFT_B)),
                  resident((D_MODEL, 2 * D_MODEL))],
        out_specs=q_specs + (pl.BlockSpec((1, tm, D_SHIFT_B), lambda b, i: (b, i, 0)),
                             pl.BlockSpec((1, tm, 2 * D_MODEL), lambda b, i: (b, i, 0)),
                             ) + tuple(tail_spec(kp) for kp in keeps),
        scratch_shapes=[pltpu.VMEM((D_QKV // LANES, tm, LANES), F32)],
        compiler_params=_cparams(("arbitrary", "arbitrary")),
        name="inproj",
    )(x, gain.reshape(1, -1), scale, shift, cos_t, sin_t, wq, wf, wg)


def _attn_body(q_ref, kc_ref, kp_ref, vc_ref, vp_ref, o_ref, lse_ref):
    mb = pl.program_id(2)
    nq = q_ref.shape[2] // BAND_BLOCK
    q = q_ref[0, 0]
    k = jnp.concatenate([kp_ref[0, 0], kc_ref[0, 0]], axis=0)
    v = jnp.concatenate([vp_ref[0, 0], vc_ref[0, 0]], axis=0)
    qi = lax.broadcasted_iota(I32, (BAND_BLOCK, 2 * BAND_BLOCK), 0)
    ki = lax.broadcasted_iota(I32, (BAND_BLOCK, 2 * BAND_BLOCK), 1)
    dist = qi + BAND_BLOCK - ki
    band = (dist >= 0) & (dist <= BAND_BLOCK)
    masks = [band & ((ki >= BAND_BLOCK) | (mb > 0))] + [band] * (nq - 1)
    lane_q = lax.broadcasted_iota(I32, (BAND_BLOCK, LANES), 1)
    lane_k = lax.broadcasted_iota(I32, (2 * BAND_BLOCK, LANES), 1)
    for hp in range(N_HEADS_A // 2):
        sl = slice(hp * LANES, (hp + 1) * LANES)
        chains = [(j, sub) for j in range(nq) for sub in range(2)]
        qs = [q[j * BAND_BLOCK:(j + 1) * BAND_BLOCK, sl] for j in range(nq)]
        ks = [k[j * BAND_BLOCK:(j + 2) * BAND_BLOCK, sl] for j in range(nq)]
        vs = [v[j * BAND_BLOCK:(j + 2) * BAND_BLOCK, sl] for j in range(nq)]
        mqs = [lane_q < HEAD_DIM_A, lane_q >= HEAD_DIM_A]
        mks = [lane_k < HEAD_DIM_A, lane_k >= HEAD_DIM_A]
        s = [jnp.where(masks[j], _dot_nt(jnp.where(mqs[sub], qs[j], jnp.zeros_like(qs[j])), ks[j]), -jnp.inf)
             for j, sub in chains]
        mx = [jnp.max(z, axis=1, keepdims=True) for z in s]
        p = [jnp.exp(z - m) for z, m in zip(s, mx)]
        l = [jnp.sum(z, axis=1, keepdims=True) for z in p]
        pv = [_dot(p[c].astype(BF16), jnp.where(mks[sub], vs[j], jnp.zeros_like(vs[j])))
              for c, (j, sub) in enumerate(chains)]
        for j in range(nq):
            c0, c1 = 2 * j, 2 * j + 1
            o_pair = pv[c0] / l[c0] + pv[c1] / l[c1]
            lse_pair = jnp.where(mqs[0], mx[c0] + jnp.log(l[c0]), mx[c1] + jnp.log(l[c1]))
            o_ref[0, 0, j * BAND_BLOCK:(j + 1) * BAND_BLOCK, sl] = o_pair.astype(BF16)
            lse_ref[0, 0, j * BAND_BLOCK:(j + 1) * BAND_BLOCK, sl] = lse_pair


def _attn_call(qkv_g, gi):
    b, dil, l, _ = qkv_g.shape
    nq = 4
    nb = l // (nq * BAND_BLOCK)
    blk = (1, 1, nq * BAND_BLOCK, D_GROUP_A)
    cur = lambda which: pl.BlockSpec(blk, lambda bb, r, m: (bb, r, m, which))
    prev = lambda which: pl.BlockSpec((1, 1, BAND_BLOCK, D_GROUP_A),
                                      lambda bb, r, m: (bb, r, jnp.maximum(nq * m - 1, 0), which))
    return pl.pallas_call(
        _attn_body,
        out_shape=(jax.ShapeDtypeStruct((b, dil, l, D_GROUP_A), BF16),
                   jax.ShapeDtypeStruct((b, dil, l, D_GROUP_A), F32)),
        grid=(b, dil, nb),
        in_specs=[cur(0), cur(1), prev(1), cur(2), prev(2)],
        out_specs=(pl.BlockSpec(blk, lambda bb, r, m: (bb, r, m, 0)),
                   pl.BlockSpec(blk, lambda bb, r, m: (bb, r, m, 0))),
        compiler_params=_cparams(("arbitrary", "arbitrary", "arbitrary")),
        name=f"attn{gi}",
    )(qkv_g, qkv_g, qkv_g, qkv_g, qkv_g)


def _sattn_body(qkv_ref, b1_ref, b2_ref, b3_ref, o_ref):
    n_rows = 3 * N_GROUPS_A * N_HEADS_A
    sq = jnp.concatenate([qkv_ref[0], jnp.zeros((LANES - n_rows, HEAD_DIM_A), F32)], axis=0)
    cols = jnp.concatenate([sq, jnp.zeros((LANES, LANES - HEAD_DIM_A), F32)], axis=1).T
    col3 = lambda first: jnp.stack([cols[:HEAD_DIM_A, first + h:first + h + 1] for h in range(N_HEADS_A)], axis=0)
    outs, lses = [], []
    for g, (buf_ref, (_, dil)) in enumerate(zip((b1_ref, b2_ref, b3_ref), DILATED_GROUPS)):
        q = col3(g * N_HEADS_A)
        kn = col3((N_GROUPS_A + g) * N_HEADS_A)
        vn = col3((2 * N_GROUPS_A + g) * N_HEADS_A)
        kb = buf_ref[0, 0]
        vb = buf_ref[0, 1]
        wb = kb.shape[-1]
        pos = lax.broadcasted_iota(I32, (1, 1, wb), 2)
        s = jnp.sum(kb * q, axis=1, keepdims=True)
        s = jnp.where(pos % dil == 0, s, -jnp.inf)
        sn = jnp.sum(kn * q, axis=1, keepdims=True)
        m = jnp.maximum(jnp.max(s, axis=2, keepdims=True), sn)
        p = jnp.exp(s - m)
        pn = jnp.exp(sn - m)
        l = jnp.sum(p, axis=2, keepdims=True) + pn
        outs.append((jnp.sum(p * vb, axis=2, keepdims=True) + pn * vn) / l)
        lses.append(m + jnp.log(l))
    mx = jnp.maximum(jnp.maximum(lses[0], lses[1]), lses[2])
    es = [jnp.exp(z - mx) for z in lses]
    o_a = (es[0] * outs[0] + es[1] * outs[1] + es[2] * outs[2]) / (es[0] + es[1] + es[2])
    o_cols = jnp.concatenate([o_a[h] for h in range(N_HEADS_A)] +
                             [jnp.zeros((HEAD_DIM_A, LANES - N_HEADS_A), F32)], axis=1)
    o_rows = jnp.concatenate([o_cols, jnp.zeros((LANES - HEAD_DIM_A, LANES), F32)], axis=0).T
    o_ref[0] = o_rows[:N_HEADS_A, :HEAD_DIM_A]


def _sattn_call(qkv_s, c1, c2, c3):
    n = qkv_s.shape[0]
    views, specs = [], []
    for c in (c1, c2, c3):
        wb = c.shape[1]
        views.append(jnp.transpose(c, (0, 2, 3, 4, 1)))
        specs.append(pl.BlockSpec((1, 2, N_HEADS_A, HEAD_DIM_A, wb), lambda b: (b, 0, 0, 0, 0)))
    return pl.pallas_call(
        _sattn_body,
        out_shape=jax.ShapeDtypeStruct((n, N_HEADS_A, HEAD_DIM_A), F32),
        grid=(n,),
        in_specs=[pl.BlockSpec((1, 3 * N_GROUPS_A * N_HEADS_A, HEAD_DIM_A), lambda b: (b, 0, 0))] + specs,
        out_specs=pl.BlockSpec((1, N_HEADS_A, HEAD_DIM_A), lambda b: (b, 0, 0)),
        compiler_params=_cparams(("arbitrary",)),
        name="sattn",
    )(qkv_s, *views)


def _rwkv_features(xs, w0, ww2, a0, wa2, wg2, k_a):
    r = xs[:, :D_B]
    k = xs[:, D_B:2 * D_B]
    v = xs[:, 2 * D_B:3 * D_B]
    xw = xs[:, 3 * D_B:3 * D_B + DECAY_LORA]
    xa = xs[:, 3 * D_B + DECAY_LORA:3 * D_B + DECAY_LORA + AAA_LORA]
    xg = xs[:, 3 * D_B + DECAY_LORA + AAA_LORA:]
    w_log = -_softplus(-(w0 + _dot(jnp.tanh(xw).astype(BF16), ww2.astype(BF16)))) - 0.5
    a = _sigmoid(a0 + _dot(xa.astype(BF16), wa2.astype(BF16)))
    g = _dot(_sigmoid(xg).astype(BF16), wg2.astype(BF16))
    k_h = k * (1.0 + (a - 1.0) * k_a)
    return r, k, v, w_log, a, g, k_h


def _head_norm(kk_h):
    nrm = jnp.sqrt(jnp.sum(kk_h * kk_h, axis=-1, keepdims=True))
    return kk_h / jnp.maximum(nrm, 1e-12)


def _wkv_finish_head(y, r_h, k_h, v_h, g_h, rk_h, lnw_h, lnb_h):
    mean = jnp.mean(y, axis=-1, keepdims=True)
    var = jnp.mean(jnp.square(y - mean), axis=-1, keepdims=True)
    yn = (y - mean) * lax.rsqrt(var + LN_X_EPS) * lnw_h + lnb_h
    bonus = jnp.sum(r_h * k_h * rk_h, axis=-1, keepdims=True) * v_h
    return (yn + bonus) * g_h


def _wkv_body(f_ref, fp_ref, mu_ref, w0_ref, ww2_ref, a0_ref, wa2_ref, wg2_ref, kk_ref, ka_ref,
              rk_ref, lnw_ref, lnb_ref, o_ref, st_ref, s_ref):
    c = pl.program_id(0)
    C = WKV_CHUNK
    nb = f_ref.shape[0]

    @pl.when(c == 0)
    def _():
        s_ref[...] = jnp.zeros_like(s_ref)

    f = jnp.concatenate([f_ref[b] for b in range(nb)], axis=0)
    row = lax.broadcasted_iota(I32, f.shape, 0)
    prev = pltpu.roll(f, 1, 0)
    for b in range(nb):
        prev = jnp.where(row == b * C, jnp.where(c == 0, 0.0, fp_ref[b][7:8, :]), prev)
    xs = f + mu_ref[...] * (prev - f)
    r, k, v, w_log, a, g, k_h = _rwkv_features(xs, w0_ref[...], ww2_ref[...], a0_ref[...],
                                               wa2_ref[...], wg2_ref[...], ka_ref[...])
    lw = -jnp.exp(w_log)
    kk = k * kk_ref[...]
    jh = lax.broadcasted_iota(I32, (D_B, LANES), 0) // HEAD_DIM_B
    ind = (jh == lax.broadcasted_iota(I32, (D_B, LANES), 1)).astype(BF16)
    ind_t = (lax.broadcasted_iota(I32, (LANES, D_B), 0)
             == lax.broadcasted_iota(I32, (LANES, D_B), 1) // HEAD_DIM_B).astype(BF16)

    def head_sum(z):
        hi = z.astype(BF16)
        lo = (z - hi.astype(F32)).astype(BF16)
        s = _dot(hi, ind) + _dot(lo, ind)
        shi = s.astype(BF16)
        slo = (s - shi.astype(F32)).astype(BF16)
        return _dot(shi, ind_t) + _dot(slo, ind_t)

    kkn = kk / jnp.maximum(jnp.sqrt(head_sum(kk * kk)), 1e-12)

    tr = lax.broadcasted_iota(I32, (nb * C, nb * C), 0)
    sr_ = lax.broadcasted_iota(I32, (nb * C, nb * C), 1)
    tri_incl = ((tr >= sr_) & (tr // C == sr_ // C)).astype(BF16)
    l1 = lw.astype(BF16)
    r1 = lw - l1.astype(F32)
    l2 = r1.astype(BF16)
    l3 = (r1 - l2.astype(F32)).astype(BF16)
    cum = _dot(tri_incl, l1) + _dot(tri_incl, l2) + _dot(tri_incl, l3)
    rhos = [cum[b * C + C // 2 - 1:b * C + C // 2, :] for b in range(nb)]
    rho = jnp.concatenate([jnp.broadcast_to(z, (C, D_B)) for z in rhos], axis=0)
    ep = jnp.exp(cum - rho)
    em = jnp.exp(rho - cum)
    e_a = ep * jnp.exp(-lw)
    r_hat = r * ep
    k_hat = k_h * em
    e_rs = [jnp.exp(z) for z in rhos]
    e_cs = [jnp.exp(cum[b * C + C - 1:b * C + C, :] - rhos[b]) for b in range(nb)]

    ti = lax.broadcasted_iota(I32, (C, C), 0)
    si = lax.broadcasted_iota(I32, (C, C), 1)
    strict = ti > si
    incl = ti >= si
    rk = rk_ref[...]
    lnw = lnw_ref[...]
    lnb = lnb_ref[...]
    items = [(b, h) for b in range(nb) for h in range(N_HEADS_B)]
    heads = range(len(items))
    lanes = [slice(h * HEAD_DIM_B, (h + 1) * HEAD_DIM_B) for _, h in items]
    cut = lambda z, i: z[items[i][0] * C:(items[i][0] + 1) * C, lanes[i]]
    e_r = [e_rs[b][:, lanes[i]] for i, (b, _) in enumerate(items)]
    e_c = [e_cs[b][:, lanes[i]] for i, (b, _) in enumerate(items)]
    a_hat_full = (-kkn * e_a).astype(BF16)
    b_hat_full = (kkn * a * em).astype(BF16)
    a_hat_b = [cut(a_hat_full, h) for h in heads]
    b_hat_b = [cut(b_hat_full, h) for h in heads]
    rh = [cut(r_hat, h) for h in heads]
    vb = [cut(v, h).astype(BF16) for h in heads]
    bk = [jnp.concatenate([b_hat_b[h], cut(k_hat, h).astype(BF16)], axis=0) for h in heads]
    p = [_dot_nt(jnp.concatenate([a_hat_b[h], rh[h].astype(BF16)], axis=0), bk[h]) for h in heads]
    l_ak = [jnp.where(strict, z[:C, C:], 0.0).astype(BF16) for z in p]
    p_rb = [jnp.where(incl, z[C:, :C], 0.0).astype(BF16) for z in p]
    p_rk = [jnp.where(incl, z[C:, C:], 0.0).astype(BF16) for z in p]
    col = lax.broadcasted_iota(I32, (C, 2 * C), 1)
    row2 = lax.broadcasted_iota(I32, (C, 2 * C), 0)
    left = col < C
    zt = [jnp.where(left, jnp.where(row2 > col, z[:C], 0.0), (col == row2 + C).astype(F32)) for z in p]
    for _ in range(int(math.log2(C))):
        zb = [z.astype(BF16) for z in zt]
        res = [_dot(z[:, :C], z) for z in zb]
        zt = [jnp.where(left, res[h], zt[h] + res[h]) for h in heads]
    tb = [z.astype(BF16) for z in zt]
    zeros_c = jnp.zeros((C, HEAD_DIM_B), BF16)
    lv = [_dot(l_ak[h], vb[h]).astype(BF16) for h in heads]
    a_bar = [_dot(tb[h], jnp.concatenate([zeros_c, a_hat_b[h]], axis=0)).astype(BF16) for h in heads]
    u_v = [_dot(tb[h], jnp.concatenate([zeros_c, lv[h]], axis=0)).astype(BF16) for h in heads]
    r_bar = [rh[h] + _dot(p_rb[h], a_bar[h]) for h in heads]
    y_v = [_dot(p_rb[h], u_v[h]) + _dot(p_rk[h], vb[h]) for h in heads]
    ab = [_dot_tn(a_bar[h], b_hat_b[h]).astype(BF16) for h in heads]
    n_t = [_dot_tn(jnp.concatenate([u_v[h], vb[h]], axis=0), bk[h]) for h in heads]
    s0 = [s_ref[b, h] for b, h in items]
    sr = [s0[h] * e_r[h] for h in heads]
    y = [_dot_nt((r_bar[h] * e_r[h]).astype(BF16), s0[h].astype(BF16)) + y_v[h] for h in heads]
    s_new = [(sr[h] + _dot(sr[h].astype(BF16), ab[h]) + n_t[h]) * e_c[h] for h in heads]
    for i, (b, h) in enumerate(items):
        s_ref[b, h] = s_new[i]
    y_full = jnp.concatenate([jnp.concatenate(y[b * N_HEADS_B:(b + 1) * N_HEADS_B], axis=1) for b in range(nb)],
                             axis=0)
    inv_hd = 1.0 / HEAD_DIM_B
    dev = y_full - head_sum(y_full) * inv_hd
    yn = dev * lax.rsqrt(head_sum(dev * dev) * inv_hd + LN_X_EPS) * lnw + lnb
    out = (yn + head_sum(r * k_h * rk) * v) * g
    for b in range(nb):
        o_ref[b] = out[b * C:(b + 1) * C, :]

    @pl.when(c == pl.num_programs(0) - 1)
    def _():
        st_ref[...] = s_ref[...]


def _wkv_call(feat, p):
    b, t, _ = feat.shape
    C = WKV_CHUNK
    nc = t // C
    row = lambda n: pl.BlockSpec((1, n), lambda c: (0, 0))
    mat = lambda m, n: pl.BlockSpec((m, n), lambda c: (0, 0))
    return pl.pallas_call(
        _wkv_body,
        out_shape=(jax.ShapeDtypeStruct((b, t, D_B), F32),
                   jax.ShapeDtypeStruct((b, N_HEADS_B, HEAD_DIM_B, HEAD_DIM_B), F32)),
        grid=(nc,),
        in_specs=[pl.BlockSpec((b, C, D_SHIFT_B), lambda c: (0, c, 0)),
                  pl.BlockSpec((b, 8, D_SHIFT_B), lambda c: (0, jnp.maximum(c * (C // 8) - 1, 0), 0)),
                  row(D_SHIFT_B), row(D_B), mat(DECAY_LORA, D_B), row(D_B), mat(AAA_LORA, D_B),
                  mat(GATE_LORA, D_B), row(D_B), row(D_B), row(D_B), row(D_B), row(D_B)],
        out_specs=(pl.BlockSpec((b, C, D_B), lambda c: (0, c, 0)),
                   pl.BlockSpec((b, N_HEADS_B, HEAD_DIM_B, HEAD_DIM_B), lambda c: (0, 0, 0, 0))),
        scratch_shapes=[pltpu.VMEM((b, N_HEADS_B, HEAD_DIM_B, HEAD_DIM_B), F32)],
        compiler_params=_cparams(("arbitrary",)),
        name="wkv",
    )(feat, feat, p['mu_b'], p['w0_b'], p['w_w2_b'], p['a0_b'], p['w_a2_b'], p['w_g2_b'],
      p['k_k_b'], p['k_a_b'], p['r_k_b'], p['ln_x_w_b'], p['ln_x_b_b'])


def _swkv_prep_body(f_ref, sh_ref, mu_ref, w0_ref, ww2_ref, a0_ref, wa2_ref, wg2_ref, kk_ref, ka_ref,
                    r_ref, w_ref, k_ref, v_ref, aa_ref, bb_ref, g_ref):
    f = f_ref[...]
    xs = f + mu_ref[...] * (sh_ref[...] - f)
    r, k, v, w_log, a, g, k_h = _rwkv_features(xs, w0_ref[...], ww2_ref[...], a0_ref[...],
                                               wa2_ref[...], wg2_ref[...], ka_ref[...])
    kk = k * kk_ref[...]
    kkn = jnp.concatenate([_head_norm(kk[:, h * HEAD_DIM_B:(h + 1) * HEAD_DIM_B]) for h in range(N_HEADS_B)],
                          axis=1)
    r_ref[...] = r
    w_ref[...] = jnp.exp(-jnp.exp(w_log))
    k_ref[...] = k_h
    v_ref[...] = v
    aa_ref[...] = -kkn
    bb_ref[...] = kkn * a
    g_ref[...] = g


def _swkv_prep_call(feat_s, shift0, p):
    n = feat_s.shape[0]
    full = lambda a: pl.BlockSpec(a.shape, lambda: tuple(0 for _ in a.shape))
    args = (feat_s, shift0, p['mu_b'], p['w0_b'], p['w_w2_b'], p['a0_b'], p['w_a2_b'], p['w_g2_b'],
            p['k_k_b'], p['k_a_b'])
    return pl.pallas_call(
        _swkv_prep_body,
        out_shape=tuple(jax.ShapeDtypeStruct((n, D_B), F32) for _ in range(7)),
        in_specs=[full(a) for a in args],
        out_specs=tuple(pl.BlockSpec((n, D_B), lambda: (0, 0)) for _ in range(7)),
        compiler_params=pltpu.CompilerParams(vmem_limit_bytes=VMEM_LIMIT),
        name="swkv_prep",
    )(*args)


def _swkv_step_body(s_ref, a_ref, w_ref, b_ref, k_ref, r_ref, v_ref, so_ref, y_ref):
    s = s_ref[...]
    th = s.shape[0]
    pad_sq = lambda z: jnp.concatenate(
        [jnp.concatenate([z, jnp.zeros((z.shape[0], LANES - z.shape[1]), F32)], axis=1),
         jnp.zeros((LANES - z.shape[0], LANES), F32)], axis=0)
    v_t = pad_sq(v_ref[...]).T
    v_col = jnp.stack([v_t[:HEAD_DIM_B, j:j + 1] for j in range(th)], axis=0)
    sa = jnp.sum(s * a_ref[...], axis=-1, keepdims=True)
    s2 = s * w_ref[...] + sa * b_ref[...] + v_col * k_ref[...]
    so_ref[...] = s2
    y = jnp.sum(s2 * r_ref[...], axis=-1, keepdims=True)
    y_t = jnp.concatenate([y[j] for j in range(th)], axis=1)
    y_ref[...] = pad_sq(y_t).T[:th, :HEAD_DIM_B]


def _swkv_step_call(s0, aa, w, bb, k, r, v):
    nh = s0.shape[0]
    th = 64
    rowspec = pl.BlockSpec((th, 1, HEAD_DIM_B), lambda i: (i, 0, 0))
    matspec = pl.BlockSpec((th, HEAD_DIM_B), lambda i: (i, 0))
    stspec = pl.BlockSpec((th, HEAD_DIM_B, HEAD_DIM_B), lambda i: (i, 0, 0))
    return pl.pallas_call(
        _swkv_step_body,
        out_shape=(jax.ShapeDtypeStruct((nh, HEAD_DIM_B, HEAD_DIM_B), F32),
                   jax.ShapeDtypeStruct((nh, HEAD_DIM_B), F32)),
        grid=(nh // th,),
        in_specs=[stspec, rowspec, rowspec, rowspec, rowspec, rowspec, matspec],
        out_specs=(stspec, matspec),
        compiler_params=_cparams(("arbitrary",)),
        name="swkv_step",
    )(s0, aa, w, bb, k, r, v)


def _swkv_fin_body(y_ref, r_ref, k_ref, v_ref, g_ref, rk_ref, lnw_ref, lnb_ref, o_ref):
    y, r, k, v, g = y_ref[...], r_ref[...], k_ref[...], v_ref[...], g_ref[...]
    rk, lnw, lnb = rk_ref[...], lnw_ref[...], lnb_ref[...]
    outs = []
    for h in range(N_HEADS_B):
        sl = slice(h * HEAD_DIM_B, (h + 1) * HEAD_DIM_B)
        outs.append(_wkv_finish_head(y[:, sl], r[:, sl], k[:, sl], v[:, sl], g[:, sl],
                                     rk[:, sl], lnw[:, sl], lnb[:, sl]))
    o_ref[...] = jnp.concatenate(outs, axis=1)


def _swkv_fin_call(y, r, k, v, g, p):
    n = y.shape[0]
    args = (y, r, k, v, g, p['r_k_b'], p['ln_x_w_b'], p['ln_x_b_b'])
    full = lambda a: pl.BlockSpec(a.shape, lambda: (0, 0))
    return pl.pallas_call(
        _swkv_fin_body,
        out_shape=jax.ShapeDtypeStruct((n, D_B), F32),
        in_specs=[full(a) for a in args],
        out_specs=pl.BlockSpec((n, D_B), lambda: (0, 0)),
        name="swkv_fin",
    )(*args)


def _route_t(scores, bias_col):
    n = scores.shape[1]
    gsz = N_EXPERTS // N_EXPERT_GROUPS
    choice = scores + bias_col
    ninf = -jnp.inf
    sid = lax.broadcasted_iota(I32, (gsz, n), 0)
    gs = []
    for gidx in range(N_EXPERT_GROUPS):
        blk = choice[gidx * gsz:(gidx + 1) * gsz, :]
        m1 = jnp.max(blk, axis=0, keepdims=True)
        first = jnp.min(jnp.where(blk == m1, sid, gsz), axis=0, keepdims=True)
        m2 = jnp.max(jnp.where(sid == first, ninf, blk), axis=0, keepdims=True)
        gs.append(m1 + m2)
    cur = jnp.concatenate(gs, axis=0)
    gid = lax.broadcasted_iota(I32, (N_EXPERT_GROUPS, n), 0)
    gmask = jnp.zeros((N_EXPERT_GROUPS, n), F32)
    for _ in range(TOPK_GROUPS):
        m = jnp.max(cur, axis=0, keepdims=True)
        first = jnp.min(jnp.where(cur == m, gid, N_EXPERT_GROUPS), axis=0, keepdims=True)
        sel = gid == first
        gmask = jnp.where(sel, 1.0, gmask)
        cur = jnp.where(sel, ninf, cur)
    emask = jnp.concatenate([jnp.broadcast_to(gmask[gidx:gidx + 1, :], (gsz, n))
                             for gidx in range(N_EXPERT_GROUPS)], axis=0)
    cur = jnp.where(emask > 0.5, choice, ninf)
    eid = lax.broadcasted_iota(I32, (N_EXPERTS, n), 0)
    selm = jnp.zeros((N_EXPERTS, n), F32)
    for _ in range(TOP_K):
        m = jnp.max(cur, axis=0, keepdims=True)
        first = jnp.min(jnp.where(cur == m, eid, N_EXPERTS), axis=0, keepdims=True)
        sel = eid == first
        selm = jnp.where(sel, 1.0, selm)
        cur = jnp.where(sel, ninf, cur)
    w = jnp.where(selm > 0.5, scores, 0.0)
    w = w / jnp.sum(w, axis=0, keepdims=True) * ROUTED_SCALE
    return jnp.where(selm > 0.5, w, -1.0)


def _unpermute(blk_ref, scr_ref, dil, tm):
    if dil == 1:
        return blk_ref[0, 0].astype(F32)
    n_chunks = scr_ref.shape[0]
    for r in range(dil):
        rows = blk_ref[0, r].astype(F32)
        for j in range(n_chunks):
            scr_ref[j, pl.ds(r, tm // dil, stride=dil), :] = rows[:, j * LANES:(j + 1) * LANES]
    return jnp.concatenate([scr_ref[j] for j in range(n_chunks)], axis=1)


def _post_body(*refs, combine, dils):
    if combine:
        o_refs, l_refs, rest = refs[:3], refs[3:6], refs[6:]
    else:
        o_refs, rest = refs[:1], refs[1:]
    (ob_ref, gt_ref, x_ref, g1_ref, sc2_ref, sh2_ref, npost_ref, npre_ref, wa_ref, wb_ref, wo_ref,
     wrt_ref, rb_ref, x1_ref, hp_ref, wt_ref) = rest[:16]
    scr = rest[16:]
    tm = x_ref.shape[1]
    if combine:
        os_, ls_ = [], []
        si = 0
        for gi, dil in enumerate(dils):
            os_.append(_unpermute(o_refs[gi], scr[si] if dil > 1 else None, dil, tm))
            ls_.append(_unpermute(l_refs[gi], scr[si + 1] if dil > 1 else None, dil, tm))
            si += 2 if dil > 1 else 0
        mx = jnp.maximum(jnp.maximum(ls_[0], ls_[1]), ls_[2])
        es = [jnp.exp(z - mx) for z in ls_]
        o_a = (es[0] * os_[0] + es[1] * os_[1] + es[2] * os_[2]) / (es[0] + es[1] + es[2])
    else:
        o_a = o_refs[0][0]
    gt = gt_ref[0].astype(F32)
    za = _dot(o_a.astype(BF16), wa_ref[...])
    zb = _dot(ob_ref[0].astype(BF16), wb_ref[...])
    merged = gt[:, :D_MODEL] * za + gt[:, D_MODEL:] * zb
    z = _dot(merged.astype(BF16), wo_ref[...])
    x1 = x_ref[0] + g1_ref[0] * _rms(z, npost_ref[...])
    x1_ref[0] = x1
    h2 = _rms(x1, npre_ref[...]) * (1.0 + sc2_ref[0]) + sh2_ref[0]
    packed = _pack_pairs(h2)
    for s in range(ROW_TILE_SUBLANES):
        hp_ref[0, pl.ds(s, tm, stride=ROW_TILE_SUBLANES), :] = packed[:, s * LANES:(s + 1) * LANES]
    tp =-(-tm // LANES) * LANES
    if tp != tm:
        h2 = jnp.concatenate([h2, jnp.zeros((tp - tm, D_MODEL), F32)], axis=0)
    logits_t = _dot_nt_split(wrt_ref[...], h2)
    w = _route_t(_sigmoid(logits_t[:N_EXPERTS, :]), rb_ref[...])
    wt_ref[...] = w[:, :tm]


def _post_call(o_parts, lse_parts, ob, gates, x, gate1, scale2, shift2, p, wa, wb, wo, wrt, rb, tm, mod_per_row):
    nb, t, _ = x.shape
    nt = t // tm
    combine = lse_parts is not None
    rowblk = lambda width: pl.BlockSpec((1, tm, width), lambda b, i: (b, i, 0))
    if mod_per_row:
        mod_spec = rowblk(D_MODEL)
    else:
        mod_spec = pl.BlockSpec((1, 1, D_MODEL), lambda b, i: (b, 0, 0))
    const = lambda shp: pl.BlockSpec(shp, lambda b, i: (0, 0))
    scratch = []
    if combine:
        dils = tuple(o.shape[1] for o in o_parts)
        o_args = list(o_parts) + list(lse_parts)
        o_specs = [pl.BlockSpec((1, d, tm // d, D_GROUP_A), lambda b, i: (b, 0, i, 0)) for d in dils] * 2
        for d in dils:
            if d > 1:
                scratch += [pltpu.VMEM((D_GROUP_A // LANES, tm, LANES), F32)] * 2
    else:
        dils = ()
        o_args = [o_parts[0]]
        o_specs = [rowblk(D_GROUP_A)]
    return pl.pallas_call(
        functools.partial(_post_body, combine=combine, dils=dils),
        out_shape=(jax.ShapeDtypeStruct((nb, t, D_MODEL), F32),
                   jax.ShapeDtypeStruct((nb, t * ROW_TILE_SUBLANES, LANES), I32),
                   jax.ShapeDtypeStruct((N_EXPERTS, nb * t), F32)),
        grid=(nb, nt),
        in_specs=o_specs + [rowblk(D_B), rowblk(2 * D_MODEL), rowblk(D_MODEL),
                            mod_spec, mod_spec, mod_spec, const((1, D_MODEL)), const((1, D_MODEL)),
                            const((D_GROUP_A, D_MODEL)), const((D_B, D_MODEL)), const((D_MODEL, D_MODEL)),
                            const((LANES, D_MODEL)), const((N_EXPERTS, 1))],
        out_specs=(rowblk(D_MODEL),
                   pl.BlockSpec((1, tm * ROW_TILE_SUBLANES, LANES), lambda b, i: (b, i, 0)),
                   pl.BlockSpec((N_EXPERTS, tm), lambda b, i: (0, b * nt + i))),
        scratch_shapes=scratch,
        compiler_params=_cparams(("arbitrary", "arbitrary")),
        name="post",
    )(*o_args, ob, gates, x, gate1, scale2, shift2, p['norm_post_mix'].reshape(1, -1),
      p['norm_pre_ffn'].reshape(1, -1), wa, wb, wo, wrt, rb)


def _rank_body(w_ref, dest_ref, w8_ref, tab_ref, etab_ref, cnt_ref, pst_ref, run_ref, *, n_real, n_slots):
    ph = pl.program_id(0)
    i = pl.program_id(1)
    T = MOE_TILE
    w = w_ref[...]
    sel = (w >= 0.0).astype(F32)
    cnt_tile = jnp.broadcast_to(jnp.sum(sel, axis=1, keepdims=True), (N_EXPERTS, LANES))
    ei = lax.broadcasted_iota(I32, (N_EXPERTS, N_EXPERTS), 0)
    ej = lax.broadcasted_iota(I32, (N_EXPERTS, N_EXPERTS), 1)

    @pl.when((ph == 0) & (i == 0))
    def _():
        cnt_ref[...] = jnp.zeros_like(cnt_ref)

    @pl.when(ph == 0)
    def _():
        cnt_ref[...] += cnt_tile

    @pl.when((ph == 1) & (i == 0))
    def _():
        cnt = cnt_ref[...]
        padded = jnp.floor((cnt + (EXPERT_BLOCK - 1)) / EXPERT_BLOCK) * EXPERT_BLOCK
        pstart = _dot_exact((ej < ei).astype(F32), padded)
        pst_ref[...] = pstart
        run_ref[...] = jnp.zeros_like(run_ref)
        pend = pstart + padded
        vend = pstart + cnt
        esub = lax.broadcasted_iota(I32, (N_EXPERTS, LANES), 0)
        lane = lax.broadcasted_iota(I32, (1, LANES), 1)
        tab_ref[...] = jnp.zeros_like(tab_ref)
        for c in range(tab_ref.shape[1] // LANES):
            bs = ((c * LANES + lane) * EXPERT_BLOCK).astype(F32)
            be = jnp.minimum(jnp.sum((pend <= bs).astype(F32), axis=0, keepdims=True), N_EXPERTS - 1.0)
            tab_ref[0:1, c * LANES:(c + 1) * LANES] = be.astype(I32)
            tab_ref[1:2, c * LANES:(c + 1) * LANES] = (pend[N_EXPERTS - 1:, :] / EXPERT_BLOCK).astype(I32)
        on_diag = esub == lax.broadcasted_iota(I32, (N_EXPERTS, LANES), 1)
        etab_ref[...] = jnp.zeros_like(etab_ref)
        lo = jnp.sum(jnp.where(on_diag, vend, 0.0), axis=0, keepdims=True)
        hi = jnp.sum(jnp.where(on_diag, pend, 0.0), axis=0, keepdims=True)
        etab_ref[0:1, :] = jnp.where(lane == N_EXPERTS, pend[N_EXPERTS - 1:, :], lo).astype(I32)
        etab_ref[1:2, :] = jnp.where(lane == N_EXPERTS, float(n_slots), hi).astype(I32)

    @pl.when(ph == 1)
    def _():
        ti = lax.broadcasted_iota(I32, (T, T), 0)
        tj = lax.broadcasted_iota(I32, (T, T), 1)
        selb = sel.astype(BF16)
        rank = _dot(selb, (ti < tj).astype(BF16))
        ordn = _dot((ej < ei).astype(BF16), selb)
        dest_e = pst_ref[:, :1] + run_ref[:, :1] + rank
        run_ref[...] += cnt_tile
        tok = i * T + lax.broadcasted_iota(I32, (1, T), 1)
        dks, wks = [], []
        for k in range(TOP_K):
            m = (sel > 0.5) & (ordn == float(k))
            dk = jnp.sum(jnp.where(m, dest_e, 0.0), axis=0, keepdims=True)
            wk = jnp.sum(jnp.where(m, w, 0.0), axis=0, keepdims=True)
            dks.append(jnp.where(tok < n_real, dk, 0.0))
            wks.append(jnp.where(tok < n_real, wk, 0.0))
        dest_ref[...] = jnp.concatenate(dks, axis=0).astype(I32)
        w8_ref[...] = jnp.concatenate(wks, axis=0)


def _rank_call(w_t, n_real, n_blocks, n_blocks_pad):
    n = w_t.shape[1]
    nt = n // MOE_TILE
    return pl.pallas_call(
        functools.partial(_rank_body, n_real=n_real, n_slots=n_blocks * EXPERT_BLOCK),
        out_shape=(jax.ShapeDtypeStruct((TOP_K, n), I32),
                   jax.ShapeDtypeStruct((TOP_K, n), F32),
                   jax.ShapeDtypeStruct((8, n_blocks_pad), I32),
                   jax.ShapeDtypeStruct((8, LANES), I32)),
        grid=(2, nt),
        in_specs=[pl.BlockSpec((N_EXPERTS, MOE_TILE), lambda ph, i: (0, i))],
        out_specs=(pl.BlockSpec((TOP_K, MOE_TILE), lambda ph, i: (0, i * ph)),
                   pl.BlockSpec((TOP_K, MOE_TILE), lambda ph, i: (0, i * ph)),
                   pl.BlockSpec((8, n_blocks_pad), lambda ph, i: (0, 0)),
                   pl.BlockSpec((8, LANES), lambda ph, i: (0, 0))),
        scratch_shapes=[pltpu.VMEM((N_EXPERTS, LANES), F32)] * 3,
        compiler_params=_cparams(("arbitrary", "arbitrary")),
        name="rank",
    )(w_t)


def _tile_rows(ref, row, n):
    return ref.at[pl.ds(pl.multiple_of(row * ROW_TILE_SUBLANES, ROW_TILE_SUBLANES), n * ROW_TILE_SUBLANES)]


def _zero_fill(etab_ref, zbuf, xs_hbm, zsem, wait):
    def go(src, dst):
        cp = pltpu.make_async_copy(src, dst, zsem)
        if wait:
            cp.wait()
        else:
            cp.start()

    def per_range(e, carry):
        lo = etab_ref[0, e]
        n = etab_ref[1, e] - lo
        n_full = n // ZERO_ROWS

        def full(j, c):
            go(zbuf, _tile_rows(xs_hbm, lo + j * ZERO_ROWS, ZERO_ROWS))
            return c

        lax.fori_loop(0, n_full, full, 0)
        pos = lo + n_full * ZERO_ROWS
        rem = n - n_full * ZERO_ROWS
        size = ZERO_ROWS // 2
        while size >= 1:
            bit = rem & size

            @pl.when(bit != 0)
            def _(size=size, pos=pos):
                go(_tile_rows(zbuf, 0, size), _tile_rows(xs_hbm, pos, size))

            pos = pos + bit
            size //= 2
        return carry

    lax.fori_loop(0, N_EXPERTS + 1, per_range, 0)


def _dispatch_body(dest_ref, etab_ref, xa_ref, xb_ref, xs_hbm, zbuf, sem, zsem, *, n_real, n_full):
    i = pl.program_id(0)
    T = MOE_TILE
    n_tok = jnp.clip(n_real - i * T, 0, T)

    def issue_from(x_ref):
        def issue(t, carry):
            for k in range(TOP_K):
                pltpu.make_async_copy(_tile_rows(x_ref, t, 1), _tile_rows(xs_hbm, dest_ref[k * T + t], 1),
                                      sem).start(priority=k % 2)
            return carry

        lax.fori_loop(0, n_tok, issue, 0)

    @pl.when(i < n_full)
    def _():
        issue_from(xa_ref)

    @pl.when(i >= n_full)
    def _():
        issue_from(xb_ref)

    @pl.when(i == 0)
    def _():
        zbuf[...] = jnp.zeros_like(zbuf)
        _zero_fill(etab_ref, zbuf, xs_hbm, zsem, wait=False)
        _zero_fill(etab_ref, zbuf, xs_hbm, zsem, wait=True)

    @pl.when(n_tok == T)
    def _():
        pltpu.make_async_copy(_tile_rows(xs_hbm, 0, T * TOP_K), _tile_rows(xs_hbm, 0, T * TOP_K), sem).wait()

    @pl.when(n_tok < T)
    def _():
        def drain(j, carry):
            pltpu.make_async_copy(_tile_rows(xs_hbm, 0, 1), _tile_rows(xs_hbm, 0, 1), sem).wait()
            return carry

        lax.fori_loop(0, n_tok * TOP_K, drain, 0)


def _dispatch_call(dest, etab, hp_a, hp_b, n_real, n_slots):
    tile_rows = MOE_TILE * ROW_TILE_SUBLANES
    n_full = hp_a.shape[0] // tile_rows
    return pl.pallas_call(
        functools.partial(_dispatch_body, n_real=n_real, n_full=n_full),
        out_shape=jax.ShapeDtypeStruct((n_slots * ROW_TILE_SUBLANES, LANES), I32),
        grid=(n_full + 1,),
        in_specs=[pl.BlockSpec((TOP_K * MOE_TILE,), lambda i: (i,), memory_space=pltpu.SMEM),
                  pl.BlockSpec((8, LANES), lambda i: (0, 0), memory_space=pltpu.SMEM),
                  pl.BlockSpec((tile_rows, LANES), lambda i: (jnp.minimum(i, n_full - 1), 0)),
                  pl.BlockSpec((tile_rows, LANES), lambda i: (0, 0))],
        out_specs=pl.BlockSpec(memory_space=pl.ANY),
        scratch_shapes=[pltpu.VMEM((ZERO_ROWS * ROW_TILE_SUBLANES, LANES), I32),
                        pltpu.SemaphoreType.DMA, pltpu.SemaphoreType.DMA],
        compiler_params=_cparams(("arbitrary",)),
        name="dispatch",
    )(dest, etab, hp_a, hp_b)


def _rows_from_tiles(ref, lo, n):
    return jnp.concatenate([ref[pl.ds(lo * ROW_TILE_SUBLANES + s, n, stride=ROW_TILE_SUBLANES), :]
                            for s in range(ROW_TILE_SUBLANES)], axis=1)


def _ffn_body(be_ref, nu_ref, xs_hbm, wg_hbm, wu_hbm, wd_hbm, ys_ref, xbuf, wgf, wuf, wdf, wgb, wub, wdb, slot_ref,
              sem, xsem):
    j = pl.program_id(0)
    n_used = nu_ref[0]

    def fetch(e, slot):
        return [pltpu.make_async_copy(w_hbm.at[e], wf.at[slot], sem.at[slot])
                for w_hbm, wf in ((wg_hbm, wgf), (wu_hbm, wuf), (wd_hbm, wdf))]

    def rows_in(b):
        ring = b % FFN_RING
        return pltpu.make_async_copy(_tile_rows(xs_hbm, b * EXPERT_BLOCK, EXPERT_BLOCK), xbuf.at[ring], xsem.at[ring])

    @pl.when(j < n_used)
    def _():
        e = be_ref[j]

        @pl.when(j == 0)
        def _():
            slot_ref[0] = 0
            for cp in fetch(e, 0):
                cp.start()
            for b in range(FFN_RING - 1):
                @pl.when(b < n_used)
                def _(b=b):
                    rows_in(b).start()

        @pl.when(j + FFN_RING - 1 < n_used)
        def _():
            rows_in(j + FFN_RING - 1).start()

        @pl.when((j == 0) | (e != be_ref[jnp.maximum(j - 1, 0)]))
        def _():
            slot = slot_ref[0]
            for cp in fetch(e, slot):
                cp.wait()
            wgb[...] = wgf[slot].astype(BF16)
            wub[...] = wuf[slot].astype(BF16)
            wdb[...] = wdf[slot].astype(BF16)
            last = be_ref.shape[0] - 1
            nxt = lax.while_loop(lambda i: (i < n_used) & (be_ref[jnp.minimum(i, last)] == e), lambda i: i + 1, j + 1)

            @pl.when(nxt < n_used)
            def _():
                for cp in fetch(be_ref[jnp.minimum(nxt, last)], 1 - slot):
                    cp.start()

            slot_ref[0] = 1 - slot

        rows_in(j).wait()
        x = _unpack_pairs(_rows_from_tiles(xbuf.at[j % FFN_RING], 0, EXPERT_BLOCK)).astype(BF16)
        act = _silu(_dot(x, wgb[...])) * _dot(x, wub[...])
        y = _dot(act.astype(BF16), wdb[...])
        packed = _pack_pairs(y)
        for s in range(ROW_TILE_SUBLANES):
            ys_ref[pl.ds(s, EXPERT_BLOCK, stride=ROW_TILE_SUBLANES), :] = packed[:, s * LANES:(s + 1) * LANES]

    @pl.when(j >= nu_ref[0])
    def _():
        ys_ref[...] = jnp.zeros_like(ys_ref)


def _ffn_call(blk_e, n_used, xs, w_gate, w_up, w_down, n_blocks):
    tile_blk = pl.BlockSpec((EXPERT_BLOCK * ROW_TILE_SUBLANES, LANES), lambda j, be, nu: (j, 0))
    grid_spec = pltpu.PrefetchScalarGridSpec(
        num_scalar_prefetch=2,
        grid=(n_blocks,),
        in_specs=[pl.BlockSpec(memory_space=pl.ANY)] * 4,
        out_specs=tile_blk,
        scratch_shapes=[pltpu.VMEM((FFN_RING, EXPERT_BLOCK * ROW_TILE_SUBLANES, LANES), I32),
                        pltpu.VMEM((2, D_MODEL, D_EXPERT), F32), pltpu.VMEM((2, D_MODEL, D_EXPERT), F32),
                        pltpu.VMEM((2, D_EXPERT, D_MODEL), F32),
                        pltpu.VMEM((D_MODEL, D_EXPERT), BF16), pltpu.VMEM((D_MODEL, D_EXPERT), BF16),
                        pltpu.VMEM((D_EXPERT, D_MODEL), BF16), pltpu.SMEM((1,), I32), pltpu.SemaphoreType.DMA((2,)),
                        pltpu.SemaphoreType.DMA((FFN_RING,))])
    return pl.pallas_call(
        _ffn_body,
        out_shape=jax.ShapeDtypeStruct((n_blocks * EXPERT_BLOCK * ROW_TILE_SUBLANES, LANES), I32),
        grid_spec=grid_spec,
        compiler_params=_cparams(("arbitrary",)),
        name="ffn",
    )(blk_e, n_used, xs, w_gate, w_up, w_down)


def _combine_body(dest_ref, dnext_ref, w8_ref, xa_ref, xb_ref, x1a_ref, x1b_ref, g2a_ref, g2b_ref, gain_ref,
                  sg_ref, su_ref, sd_ref, ys_hbm, oa_ref, ob_ref, buf, sem):
    j = pl.program_id(0)
    T = COMBINE_TILE
    RC = COMBINE_ROWS

    def issue(d_ref, slot, t):
        for k in range(TOP_K):
            pltpu.make_async_copy(_tile_rows(ys_hbm, d_ref[k * T + t], 1), _tile_rows(buf.at[slot], k * T + t, 1),
                                  sem.at[slot]).start(priority=k % 2)

    def wait(slot):
        pltpu.make_async_copy(_tile_rows(ys_hbm, 0, T * TOP_K), buf.at[slot], sem.at[slot]).wait()

    def step(slot):
        is_tail = j == 0
        wait(slot)
        for t in range(RC):
            issue(dnext_ref, 1 - slot, t)
        x = _unpack_pairs(jnp.where(is_tail, _rows_from_tiles(xb_ref, 0, T),
                                    _rows_from_tiles(xa_ref, 0, T))).astype(BF16)
        shared = _dot((_silu(_dot(x, sg_ref[...])) * _dot(x, su_ref[...])).astype(BF16), sd_ref[...])
        w_t = jnp.concatenate([w8_ref[...], jnp.zeros((LANES - TOP_K, T), F32)], axis=0).T
        oa_ref[...] = shared
        for r0 in range(0, T, RC):
            if r0 > 0:
                for t in range(r0, r0 + RC):
                    issue(dnext_ref, 1 - slot, t)
            acc = oa_ref[r0:r0 + RC, :]
            for k in range(TOP_K):
                acc = acc + w_t[r0:r0 + RC, k:k + 1] * _unpack_pairs(_rows_from_tiles(buf.at[slot], k * T + r0, RC))
            x1 = jnp.where(is_tail, x1b_ref[r0:r0 + RC, :], x1a_ref[r0:r0 + RC, :])
            g2 = jnp.where(is_tail, g2b_ref[r0:r0 + RC, :], g2a_ref[0])
            oa_ref[r0:r0 + RC, :] = x1 + g2 * _rms(acc, gain_ref[...])

        @pl.when(is_tail)
        def _():
            ob_ref[...] = oa_ref[...]

        @pl.when(j + 1 == pl.num_programs(0))
        def _():
            wait(1 - slot)

    @pl.when(j == 0)
    def _():
        lax.fori_loop(0, T, lambda t, c: (issue(dest_ref, 0, t), c)[1], 0, unroll=2)

    @pl.when(j % 2 == 0)
    def _():
        step(0)

    @pl.when(j % 2 == 1)
    def _():
        step(1)


def _combine_call(dest, w8, hp_a, hp_b, x1_a, x1_b, gate2_a, gate2_b, gain, wsg, wsu, wsd, ys):
    T = COMBINE_TILE
    tile_rows = T * ROW_TILE_SUBLANES
    n_full = hp_a.shape[0] // tile_rows
    n_tiles = n_full + 1
    seq = x1_a.shape[0] // gate2_a.shape[0]
    tile_of = lambda j: jnp.where(j == 0, n_full, j - 1)
    full_of = lambda j: jnp.maximum(j - 1, 0)
    const = lambda shp: pl.BlockSpec(shp, lambda j: (0, 0))
    return pl.pallas_call(
        _combine_body,
        out_shape=(jax.ShapeDtypeStruct((n_full * T, D_MODEL), F32), jax.ShapeDtypeStruct((T, D_MODEL), F32)),
        grid=(n_tiles,),
        in_specs=[pl.BlockSpec((TOP_K * T,), lambda j: (tile_of(j),), memory_space=pltpu.SMEM),
                  pl.BlockSpec((TOP_K * T,), lambda j: (tile_of(jnp.minimum(j + 1, n_tiles - 1)),),
                               memory_space=pltpu.SMEM),
                  pl.BlockSpec((TOP_K, T), lambda j: (0, tile_of(j))),
                  pl.BlockSpec((tile_rows, LANES), lambda j: (full_of(j), 0)),
                  pl.BlockSpec((tile_rows, LANES), lambda j: (0, 0)),
                  pl.BlockSpec((T, D_MODEL), lambda j: (full_of(j), 0)),
                  const((T, D_MODEL)),
                  pl.BlockSpec((1, 1, D_MODEL), lambda j: (full_of(j) * T // seq, 0, 0)),
                  const((T, D_MODEL)), const((1, D_MODEL)),
                  const((D_MODEL, D_EXPERT)), const((D_MODEL, D_EXPERT)), const((D_EXPERT, D_MODEL)),
                  pl.BlockSpec(memory_space=pl.ANY)],
        out_specs=(pl.BlockSpec((T, D_MODEL), lambda j: (full_of(j), 0)), const((T, D_MODEL))),
        scratch_shapes=[pltpu.VMEM((2, TOP_K * tile_rows, LANES), I32), pltpu.SemaphoreType.DMA((2,))],
        compiler_params=_cparams(("arbitrary",)),
        name="combine",
    )(dest, dest, w8, hp_a, hp_b, x1_a, x1_b, gate2_a, gate2_b, gain.reshape(1, -1), wsg, wsu, wsd, ys)


def _rope_tables(pos):
    half = HEAD_DIM_A // 2
    inv_freq = ROPE_THETA ** (-jnp.arange(half, dtype=F32) / half)
    ang = pos.astype(F32)[:, None] * inv_freq[None, :]
    cos = jnp.cos(ang)
    sin = jnp.sin(ang)
    reps = LANES // HEAD_DIM_A
    cos_t = jnp.tile(jnp.concatenate([cos, cos], axis=1), (1, reps))
    sin_t = jnp.tile(jnp.concatenate([-sin, sin], axis=1), (1, reps))
    return cos_t, sin_t


def kernel(x_prompt, x_sample, c_prompt, c_sample, cache_a1_kv, cache_a2_kv, cache_a3_kv, state_b_wkv, state_b_shift, w_ada, b_ada, norm_pre_mix, norm_post_mix, norm_pre_ffn, norm_post_ffn, w_in, w_a_out, mu_b, w0_b, w_w2_b, a0_b, w_a2_b, w_g2_b, k_k_b, k_a_b, r_k_b, ln_x_w_b, ln_x_b_b, w_b_out, w_out, w_router, router_bias, w_e_gate, w_e_up, w_e_down, w_s_gate, w_s_up, w_s_down):
    assert DEPTH == 1
    l = 0
    nd = DEC_BATCH
    row = lambda a: a.reshape(1, -1)
    p = {'mu_b': row(mu_b[l]), 'w0_b': row(w0_b[l]), 'w_w2_b': w_w2_b[l], 'a0_b': row(a0_b[l]),
         'w_a2_b': w_a2_b[l], 'w_g2_b': w_g2_b[l], 'k_k_b': row(k_k_b[l]), 'k_a_b': row(k_a_b[l]),
         'r_k_b': row(r_k_b[l]), 'ln_x_w_b': row(ln_x_w_b[l]), 'ln_x_b_b': row(ln_x_b_b[l]),
         'norm_post_mix': norm_post_mix[l], 'norm_pre_ffn': norm_pre_ffn[l]}

    wq, wf, wg = _wsplit_call(w_in[l])
    wa = w_a_out[l].astype(BF16)
    wb = w_b_out[l].astype(BF16)
    wo = w_out[l].astype(BF16)
    wrt = jnp.concatenate([w_router[l].T, jnp.zeros((LANES - N_EXPERTS, D_MODEL), F32)], axis=0)
    rb = router_bias[l].reshape(N_EXPERTS, 1)
    wsg, wsu, wsd = w_s_gate[l].astype(BF16), w_s_up[l].astype(BF16), w_s_down[l].astype(BF16)

    n_c = BATCH + nd
    c_all = jnp.concatenate([c_prompt, c_sample, jnp.zeros((-n_c % 8, D_MODEL), F32)], axis=0)
    mod = _mod_call(c_all, w_ada[l], b_ada[l])
    mod_p = [m.reshape(BATCH, 1, D_MODEL) for m in jnp.split(mod[:BATCH], 6, axis=-1)]
    mod_s = [m.reshape(1, nd, D_MODEL) for m in jnp.split(mod[BATCH:n_c], 6, axis=-1)]

    cos_p, sin_p = _rope_tables(jnp.arange(SEQ, dtype=I32))
    cos_s, sin_s = _rope_tables(jnp.full((nd,), PAST_LEN, I32))

    keep_p = [min(w, SEQ) for w, _ in DILATED_GROUPS]
    dils = tuple(d for _, d in DILATED_GROUPS)

    q0, q1, q2, feat_p, gates_p, *tails_p = _inproj_call(
        x_prompt, norm_pre_mix[l], mod_p[1], mod_p[0], cos_p, sin_p, wq, wf, wg,
        tm=256, keeps=keep_p, mod_per_row=False, dils=dils)
    o_parts, lse_parts = [], []
    for gi, qg in enumerate((q0, q1, q2)):
        o, lse = _attn_call(qg, gi)
        o_parts.append(o)
        lse_parts.append(lse)
    ob_p, wkv_p = _wkv_call(feat_p, p)
    x1_p, hp_p, wt_p = _post_call(o_parts, lse_parts, ob_p, gates_p, x_prompt, mod_p[2], mod_p[4], mod_p[3],
                                  p, wa, wb, wo, wrt, rb, tm=512, mod_per_row=False)

    xs3 = x_sample.reshape(1, nd, D_MODEL)
    s0, s1, s2, feat_s, gates_s, *tails_s = _inproj_call(
        xs3, norm_pre_mix[l], mod_s[1], mod_s[0], cos_s, sin_s, wq, wf, wg,
        tm=nd, keeps=(nd,) * N_GROUPS_A, mod_per_row=True, dils=(1, 1, 1))
    qkv_s = jnp.stack([z.reshape(nd, 3, N_HEADS_A, HEAD_DIM_A) for z in (s0, s1, s2)], axis=2)
    qkv_s = qkv_s.reshape(nd, 3 * N_GROUPS_A * N_HEADS_A, HEAD_DIM_A).astype(F32)
    oa_s = _sattn_call(qkv_s, cache_a1_kv[l], cache_a2_kv[l], cache_a3_kv[l])
    r_s, w_s, k_s, v_s, aa_s, bb_s, g_s = _swkv_prep_call(feat_s[0], state_b_shift[l], p)
    nh = nd * N_HEADS_B
    as_row = lambda a: a.reshape(nh, 1, HEAD_DIM_B)
    s_new, y_col = _swkv_step_call(state_b_wkv[l].reshape(nh, HEAD_DIM_B, HEAD_DIM_B), as_row(aa_s), as_row(w_s),
                                   as_row(bb_s), as_row(k_s), as_row(r_s), v_s.reshape(nh, HEAD_DIM_B))
    ob_s = _swkv_fin_call(y_col.reshape(nd, D_B), r_s, k_s, v_s, g_s, p)
    x1_s, hp_s, wt_s = _post_call([oa_s.reshape(1, nd, D_GROUP_A)], None, ob_s.reshape(1, nd, D_B), gates_s, xs3,
                                  mod_s[2], mod_s[4], mod_s[3], p, wa, wb, wo, wrt, rb, tm=nd, mod_per_row=True)

    n_p = BATCH * SEQ
    n_real = n_p + nd
    n_all = -(-n_real // MOE_TILE) * MOE_TILE
    pad = n_all - n_real
    n_blocks = -(-(n_real * TOP_K) // EXPERT_BLOCK) + N_EXPERTS
    n_blocks_pad = -(-n_blocks // LANES) * LANES
    assert n_p % MOE_TILE == 0 and nd <= MOE_TILE
    hp_a = hp_p.reshape(n_p * ROW_TILE_SUBLANES, LANES)
    hp_b = jnp.concatenate([hp_s[0], jnp.zeros((pad * ROW_TILE_SUBLANES, LANES), I32)], axis=0)
    wt_all = jnp.concatenate([wt_p, wt_s, jnp.full((N_EXPERTS, pad), -1.0, F32)], axis=1)
    dest8, w8, tab, etab = _rank_call(wt_all, n_real, n_blocks, n_blocks_pad)
    dest = dest8.reshape(TOP_K, n_all // MOE_TILE, MOE_TILE).transpose(1, 0, 2).reshape(-1)
    xs = _dispatch_call(dest, etab, hp_a, hp_b, n_real, n_blocks * EXPERT_BLOCK)
    ys = _ffn_call(tab[0], tab[1, :1], xs, w_e_gate[l], w_e_up[l], w_e_down[l], n_blocks)
    n_ct = n_p // COMBINE_TILE + 1
    dest_c = dest8[:, :n_ct * COMBINE_TILE].reshape(TOP_K, n_ct, COMBINE_TILE).transpose(1, 0, 2).reshape(-1)
    pad_rows = lambda z: jnp.concatenate([z, jnp.zeros((COMBINE_TILE - nd, D_MODEL), F32)], axis=0)
    out_p, out_s = _combine_call(dest_c, w8, hp_a, hp_b, x1_p.reshape(n_p, D_MODEL), pad_rows(x1_s[0]), mod_p[5],
                                 pad_rows(mod_s[5][0]), norm_post_ffn[l], wsg, wsu, wsd, ys)
    y_prompt = out_p.reshape(BATCH, SEQ, D_MODEL)
    y_sample = out_s[:nd]

    a_p = [z.reshape(1, BATCH, kp, 2, N_HEADS_A, HEAD_DIM_A) for z, kp in zip(tails_p, keep_p)]
    a_s = [z.reshape(1, nd, DEC_SEQ, 2, N_HEADS_A, HEAD_DIM_A) for z in tails_s]
    shift_p = feat_p[:, -1][None]
    shift_s = feat_s[0][None]
    return (y_prompt, y_sample.reshape(nd, DEC_SEQ, D_MODEL), a_p[0], a_p[1], a_p[2], wkv_p[None], shift_p,
            a_s[0], a_s[1], a_s[2], s_new.reshape(1, nd, N_HEADS_B, HEAD_DIM_B, HEAD_DIM_B), shift_s)
```

```python
import functools
import math

import jax
import jax.numpy as jnp
from jax import lax
from jax.experimental import pallas as pl
from jax.experimental.pallas import tpu as pltpu

F32 = jnp.float32
BF16 = jnp.bfloat16
I32 = jnp.int32

D_MODEL = 1024
BATCH = 2
SEQ = 8192
DEPTH = 1
DEC_BATCH = 32
DEC_SEQ = 1
PAST_LEN = 16384

HEAD_DIM_A = 64
N_HEADS_A = 8
DILATED_GROUPS = ((128, 1), (512, 4), (2048, 16))
N_GROUPS_A = 3
D_GROUP_A = N_HEADS_A * HEAD_DIM_A
D_A = N_GROUPS_A * D_GROUP_A
D_QKV = 3 * D_A
BAND_BLOCK = 128
ROPE_THETA = 10000.0

HEAD_DIM_B = 64
N_HEADS_B = 16
D_B = 1024
DECAY_LORA = 64
AAA_LORA = 64
GATE_LORA = 160
D_SHIFT_B = 3 * D_B + DECAY_LORA + AAA_LORA + GATE_LORA
LN_X_EPS = 64e-5

N_EXPERTS = 64
TOP_K = 8
N_EXPERT_GROUPS = 8
TOPK_GROUPS = 4
D_EXPERT = 256
ROUTED_SCALE = 2.5
EXPERT_BLOCK = 1024
NORM_EPS = 1e-6

LANES = 128
WKV_CHUNK = 64
MOE_TILE = 1024
COMBINE_TILE = 256
COMBINE_ROWS = 32
VMEM_LIMIT = 56 * 1024 * 1024
ROW_TILE_SUBLANES = D_MODEL // (2 * LANES)
FFN_RING = 3
ZERO_ROWS = 256


def _cparams(sem):
    return pltpu.CompilerParams(dimension_semantics=sem, vmem_limit_bytes=VMEM_LIMIT)


def _dot(a, b):
    return jnp.dot(a, b, preferred_element_type=F32)


def _dot_nt(a, b):
    return lax.dot_general(a, b, (((1,), (1,)), ((), ())), preferred_element_type=F32)


def _dot_tn(a, b):
    return lax.dot_general(a, b, (((0,), (0,)), ((), ())), preferred_element_type=F32)


def _dot_nt_split(a, b):
    ah = a.astype(BF16)
    al = (a - ah.astype(F32)).astype(BF16)
    bh = b.astype(BF16)
    bl = (b - bh.astype(F32)).astype(BF16)
    return _dot_nt(ah, bh) + _dot_nt(ah, bl) + _dot_nt(al, bh)


def _dot_exact(a, b):
    return lax.dot_general(a, b, (((1,), (0,)), ((), ())), precision=lax.Precision.HIGHEST,
                           preferred_element_type=F32)


def _rms(x, gain):
    return x * lax.rsqrt(jnp.mean(x * x, axis=-1, keepdims=True) + NORM_EPS) * gain


def _sigmoid(x):
    return 1.0 / (1.0 + jnp.exp(-x))


def _silu(x):
    return x * _sigmoid(x)


def _softplus(x):
    return jnp.maximum(x, 0.0) + jnp.log(1.0 + jnp.exp(-jnp.abs(x)))


def _pack_pairs(x):
    half = D_MODEL // 2
    lo = lax.bitcast_convert_type(x[:, :half].astype(BF16).astype(F32), I32)
    hi = lax.bitcast_convert_type(x[:, half:].astype(BF16).astype(F32), I32)
    return lax.shift_right_logical(lo, 16) | (hi & jnp.int32(-65536))


def _unpack_pairs(w):
    lo = lax.bitcast_convert_type(w << 16, F32)
    hi = lax.bitcast_convert_type(w & jnp.int32(-65536), F32)
    return jnp.concatenate([lo, hi], axis=1)


def _mod_body(c_ref, w_ref, b_ref, o_ref):
    s = _silu(c_ref[...]).astype(BF16)
    o_ref[...] = _dot(s, w_ref[...].astype(BF16)) + b_ref[...]


def _mod_call(c_all, w_ada, b_ada):
    rows = c_all.shape[0]
    tn = 1536
    return pl.pallas_call(
        _mod_body,
        out_shape=jax.ShapeDtypeStruct((rows, 6 * D_MODEL), F32),
        grid=(6 * D_MODEL // tn,),
        in_specs=[pl.BlockSpec((rows, D_MODEL), lambda j: (0, 0)),
                  pl.BlockSpec((D_MODEL, tn), lambda j: (0, j)),
                  pl.BlockSpec((1, tn), lambda j: (0, j))],
        out_specs=pl.BlockSpec((rows, tn), lambda j: (0, j)),
        compiler_params=_cparams(("arbitrary",)),
        name="mod",
    )(c_all, w_ada, b_ada.reshape(1, -1))


def _wsplit_body(w_ref, q_ref, f_ref, g_ref):
    w = w_ref[...]
    q_ref[...] = w[:, :D_QKV].astype(BF16)
    f_ref[...] = w[:, D_QKV:D_QKV + D_SHIFT_B].astype(BF16)
    g_ref[...] = w[:, D_QKV + D_SHIFT_B:].astype(BF16)


def _wsplit_call(w):
    rows, cols = w.shape
    tr = 128
    widths = (D_QKV, D_SHIFT_B, cols - D_QKV - D_SHIFT_B)
    return pl.pallas_call(
        _wsplit_body,
        out_shape=tuple(jax.ShapeDtypeStruct((rows, n), BF16) for n in widths),
        grid=(rows // tr,),
        in_specs=[pl.BlockSpec((tr, cols), lambda i: (i, 0))],
        out_specs=tuple(pl.BlockSpec((tr, n), lambda i: (i, 0)) for n in widths),
        compiler_params=_cparams(("arbitrary",)),
        name="wsplit",
    )(w)


def _inproj_body(x_ref, g_ref, sc_ref, sh_ref, cos_ref, sin_ref, wq_ref, wf_ref, wg_ref,
                 q0_ref, q1_ref, q2_ref, feat_ref, gate_ref, t0_ref, t1_ref, t2_ref, p_ref, *, dils):
    x = x_ref[0]
    tm = x.shape[0]
    h = _rms(x, g_ref[...]) * (1.0 + sc_ref[0]) + sh_ref[0]
    hb = h.astype(BF16)
    p = _dot(hb, wq_ref[...])
    cos = cos_ref[...]
    sin = sin_ref[...]
    lane = lax.broadcasted_iota(I32, cos.shape, 1)
    first_half = (lane % HEAD_DIM_A) < (HEAD_DIM_A // 2)
    for c in range(2 * D_A // LANES):
        xc = p[:, c * LANES:(c + 1) * LANES]
        partner = jnp.where(first_half, pltpu.roll(xc, LANES - HEAD_DIM_A // 2, 1),
                            pltpu.roll(xc, HEAD_DIM_A // 2, 1))
        rc = xc * cos + partner * sin
        if c < D_A // LANES:
            rc = rc * (HEAD_DIM_A ** -0.5)
        p_ref[c] = rc
    for c in range(2 * D_A // LANES, D_QKV // LANES):
        p_ref[c] = p[:, c * LANES:(c + 1) * LANES]
    per_group = D_GROUP_A // LANES
    for gi, t_ref in enumerate((t0_ref, t1_ref, t2_ref)):
        rows = t_ref.shape[1]
        for which in (1, 2):
            for j in range(per_group):
                c = (which * D_A + gi * D_GROUP_A) // LANES + j
                t_ref[0, :, (which - 1) * D_GROUP_A + j * LANES:(which - 1) * D_GROUP_A + (j + 1) * LANES] = \
                    p_ref[c, tm - rows:tm, :]
    for gi, (out_ref, dil) in enumerate(zip((q0_ref, q1_ref, q2_ref), dils)):
        for which in range(3):
            for j in range(per_group):
                c = (which * D_A + gi * D_GROUP_A) // LANES + j
                dst = slice(which * D_GROUP_A + j * LANES, which * D_GROUP_A + (j + 1) * LANES)
                if dil == 1:
                    out_ref[0, 0, :, dst] = p_ref[c].astype(BF16)
                else:
                    for r in range(dil):
                        out_ref[0, r, :, dst] = p_ref[c, pl.ds(r, tm // dil, stride=dil), :].astype(BF16)
    feat_ref[0] = _dot(hb, wf_ref[...])
    gate_ref[0] = _sigmoid(_dot(hb, wg_ref[...])).astype(BF16)


def _inproj_call(x, gain, scale, shift, cos_t, sin_t, wq, wf, wg, tm, keeps, mod_per_row, dils):
    nb, t, _ = x.shape
    nt = t // tm

    def tail_spec(keep):
        if keep <= tm:
            return pl.BlockSpec((1, keep, 2 * D_GROUP_A), lambda b, i: (b, 0, 0))
        first = (t - keep) // tm
        return pl.BlockSpec((1, tm, 2 * D_GROUP_A), lambda b, i: (b, jnp.maximum(i - first, 0), 0))

    if mod_per_row:
        mod_spec = pl.BlockSpec((1, tm, D_MODEL), lambda b, i: (b, i, 0))
    else:
        mod_spec = pl.BlockSpec((1, 1, D_MODEL), lambda b, i: (b, 0, 0))
    resident = lambda shp: pl.BlockSpec(shp, lambda b, i: (0, 0), pipeline_mode=pl.Buffered(1))
    q_shapes = tuple(jax.ShapeDtypeStruct((nb, d, t // d, 3 * D_GROUP_A), BF16) for d in dils)
    q_specs = tuple(pl.BlockSpec((1, d, tm // d, 3 * D_GROUP_A), lambda b, i: (b, 0, i, 0)) for d in dils)
    return pl.pallas_call(
        functools.partial(_inproj_body, dils=dils),
        out_shape=q_shapes + (jax.ShapeDtypeStruct((nb, t, D_SHIFT_B), F32),
                              jax.ShapeDtypeStruct((nb, t, 2 * D_MODEL), BF16),
                              ) + tuple(jax.ShapeDtypeStruct((nb, kp, 2 * D_GROUP_A), F32) for kp in keeps),
        grid=(nb, nt),
        in_specs=[pl.BlockSpec((1, tm, D_MODEL), lambda b, i: (b, i, 0)),
                  pl.BlockSpec((1, D_MODEL), lambda b, i: (0, 0)),
                  mod_spec, mod_spec,
                  pl.BlockSpec((tm, LANES), lambda b, i: (i, 0)),
                  pl.BlockSpec((tm, LANES), lambda b, i: (i, 0)),
                  resident((D_MODEL, D_QKV)), resident((D_MODEL, D_SHIFT_B)),
                  resident((D_MODEL, 2 * D_MODEL))],
        out_specs=q_specs + (pl.BlockSpec((1, tm, D_SHIFT_B), lambda b, i: (b, i, 0)),
                             pl.BlockSpec((1, tm, 2 * D_MODEL), lambda b, i: (b, i, 0)),
                             ) + tuple(tail_spec(kp) for kp in keeps),
        scratch_shapes=[pltpu.VMEM((D_QKV // LANES, tm, LANES), F32)],
        compiler_params=_cparams(("arbitrary", "arbitrary")),
        name="inproj",
    )(x, gain.reshape(1, -1), scale, shift, cos_t, sin_t, wq, wf, wg)


def _attn_body(q_ref, kc_ref, kp_ref, vc_ref, vp_ref, o_ref, lse_ref):
    mb = pl.program_id(2)
    nq = q_ref.shape[2] // BAND_BLOCK
    q = q_ref[0, 0]
    k = jnp.concatenate([kp_ref[0, 0], kc_ref[0, 0]], axis=0)
    v = jnp.concatenate([vp_ref[0, 0], vc_ref[0, 0]], axis=0)
    qi = lax.broadcasted_iota(I32, (BAND_BLOCK, 2 * BAND_BLOCK), 0)
    ki = lax.broadcasted_iota(I32, (BAND_BLOCK, 2 * BAND_BLOCK), 1)
    dist = qi + BAND_BLOCK - ki
    band = (dist >= 0) & (dist <= BAND_BLOCK)
    masks = [band & ((ki >= BAND_BLOCK) | (mb > 0))] + [band] * (nq - 1)
    lane_q = lax.broadcasted_iota(I32, (BAND_BLOCK, LANES), 1)
    lane_k = lax.broadcasted_iota(I32, (2 * BAND_BLOCK, LANES), 1)
    for hp in range(N_HEADS_A // 2):
        sl = slice(hp * LANES, (hp + 1) * LANES)
        chains = [(j, sub) for j in range(nq) for sub in range(2)]
        qs = [q[j * BAND_BLOCK:(j + 1) * BAND_BLOCK, sl] for j in range(nq)]
        ks = [k[j * BAND_BLOCK:(j + 2) * BAND_BLOCK, sl] for j in range(nq)]
        vs = [v[j * BAND_BLOCK:(j + 2) * BAND_BLOCK, sl] for j in range(nq)]
        mqs = [lane_q < HEAD_DIM_A, lane_q >= HEAD_DIM_A]
        mks = [lane_k < HEAD_DIM_A, lane_k >= HEAD_DIM_A]
        s = [jnp.where(masks[j], _dot_nt(jnp.where(mqs[sub], qs[j], jnp.zeros_like(qs[j])), ks[j]), -jnp.inf)
             for j, sub in chains]
        mx = [jnp.max(z, axis=1, keepdims=True) for z in s]
        p = [jnp.exp(z - m) for z, m in zip(s, mx)]
        l = [jnp.sum(z, axis=1, keepdims=True) for z in p]
        pv = [_dot(p[c].astype(BF16), jnp.where(mks[sub], vs[j], jnp.zeros_like(vs[j])))
              for c, (j, sub) in enumerate(chains)]
        for j in range(nq):
            c0, c1 = 2 * j, 2 * j + 1
            o_pair = pv[c0] / l[c0] + pv[c1] / l[c1]
            lse_pair = jnp.where(mqs[0], mx[c0] + jnp.log(l[c0]), mx[c1] + jnp.log(l[c1]))
            o_ref[0, 0, j * BAND_BLOCK:(j + 1) * BAND_BLOCK, sl] = o_pair.astype(BF16)
            lse_ref[0, 0, j * BAND_BLOCK:(j + 1) * BAND_BLOCK, sl] = lse_pair


def _attn_call(qkv_g, gi):
    b, dil, l, _ = qkv_g.shape
    nq = 4
    nb = l // (nq * BAND_BLOCK)
    blk = (1, 1, nq * BAND_BLOCK, D_GROUP_A)
    cur = lambda which: pl.BlockSpec(blk, lambda bb, r, m: (bb, r, m, which))
    prev = lambda which: pl.BlockSpec((1, 1, BAND_BLOCK, D_GROUP_A),
                                      lambda bb, r, m: (bb, r, jnp.maximum(nq * m - 1, 0), which))
    return pl.pallas_call(
        _attn_body,
        out_shape=(jax.ShapeDtypeStruct((b, dil, l, D_GROUP_A), BF16),
                   jax.ShapeDtypeStruct((b, dil, l, D_GROUP_A), F32)),
        grid=(b, dil, nb),
        in_specs=[cur(0), cur(1), prev(1), cur(2), prev(2)],
        out_specs=(pl.BlockSpec(blk, lambda bb, r, m: (bb, r, m, 0)),
                   pl.BlockSpec(blk, lambda bb, r, m: (bb, r, m, 0))),
        compiler_params=_cparams(("arbitrary", "arbitrary", "arbitrary")),
        name=f"attn{gi}",
    )(qkv_g, qkv_g, qkv_g, qkv_g, qkv_g)


def _sattn_body(qkv_ref, b1_ref, b2_ref, b3_ref, o_ref):
    n_rows = 3 * N_GROUPS_A * N_HEADS_A
    sq = jnp.concatenate([qkv_ref[0], jnp.zeros((LANES - n_rows, HEAD_DIM_A), F32)], axis=0)
    cols = jnp.concatenate([sq, jnp.zeros((LANES, LANES - HEAD_DIM_A), F32)], axis=1).T
    col3 = lambda first: jnp.stack([cols[:HEAD_DIM_A, first + h:first + h + 1] for h in range(N_HEADS_A)], axis=0)
    outs, lses = [], []
    for g, (buf_ref, (_, dil)) in enumerate(zip((b1_ref, b2_ref, b3_ref), DILATED_GROUPS)):
        q = col3(g * N_HEADS_A)
        kn = col3((N_GROUPS_A + g) * N_HEADS_A)
        vn = col3((2 * N_GROUPS_A + g) * N_HEADS_A)
        kb = buf_ref[0, 0]
        vb = buf_ref[0, 1]
        wb = kb.shape[-1]
        pos = lax.broadcasted_iota(I32, (1, 1, wb), 2)
        s = jnp.sum(kb * q, axis=1, keepdims=True)
        s = jnp.where(pos % dil == 0, s, -jnp.inf)
        sn = jnp.sum(kn * q, axis=1, keepdims=True)
        m = jnp.maximum(jnp.max(s, axis=2, keepdims=True), sn)
        p = jnp.exp(s - m)
        pn = jnp.exp(sn - m)
        l = jnp.sum(p, axis=2, keepdims=True) + pn
        outs.append((jnp.sum(p * vb, axis=2, keepdims=True) + pn * vn) / l)
        lses.append(m + jnp.log(l))
    mx = jnp.maximum(jnp.maximum(lses[0], lses[1]), lses[2])
    es = [jnp.exp(z - mx) for z in lses]
    o_a = (es[0] * outs[0] + es[1] * outs[1] + es[2] * outs[2]) / (es[0] + es[1] + es[2])
    o_cols = jnp.concatenate([o_a[h] for h in range(N_HEADS_A)] +
                             [jnp.zeros((HEAD_DIM_A, LANES - N_HEADS_A), F32)], axis=1)
    o_rows = jnp.concatenate([o_cols, jnp.zeros((LANES - HEAD_DIM_A, LANES), F32)], axis=0).T
    o_ref[0] = o_rows[:N_HEADS_A, :HEAD_DIM_A]


def _sattn_call(qkv_s, c1, c2, c3):
    n = qkv_s.shape[0]
    views, specs = [], []
    for c in (c1, c2, c3):
        wb = c.shape[1]
        views.append(jnp.transpose(c, (0, 2, 3, 4, 1)))
        specs.append(pl.BlockSpec((1, 2, N_HEADS_A, HEAD_DIM_A, wb), lambda b: (b, 0, 0, 0, 0)))
    return pl.pallas_call(
        _sattn_body,
        out_shape=jax.ShapeDtypeStruct((n, N_HEADS_A, HEAD_DIM_A), F32),
        grid=(n,),
        in_specs=[pl.BlockSpec((1, 3 * N_GROUPS_A * N_HEADS_A, HEAD_DIM_A), lambda b: (b, 0, 0))] + specs,
        out_specs=pl.BlockSpec((1, N_HEADS_A, HEAD_DIM_A), lambda b: (b, 0, 0)),
        compiler_params=_cparams(("arbitrary",)),
        name="sattn",
    )(qkv_s, *views)


def _rwkv_features(xs, w0, ww2, a0, wa2, wg2, k_a):
    r = xs[:, :D_B]
    k = xs[:, D_B:2 * D_B]
    v = xs[:, 2 * D_B:3 * D_B]
    xw = xs[:, 3 * D_B:3 * D_B + DECAY_LORA]
    xa = xs[:, 3 * D_B + DECAY_LORA:3 * D_B + DECAY_LORA + AAA_LORA]
    xg = xs[:, 3 * D_B + DECAY_LORA + AAA_LORA:]
    w_log = -_softplus(-(w0 + _dot(jnp.tanh(xw).astype(BF16), ww2.astype(BF16)))) - 0.5
    a = _sigmoid(a0 + _dot(xa.astype(BF16), wa2.astype(BF16)))
    g = _dot(_sigmoid(xg).astype(BF16), wg2.astype(BF16))
    k_h = k * (1.0 + (a - 1.0) * k_a)
    return r, k, v, w_log, a, g, k_h


def _head_norm(kk_h):
    nrm = jnp.sqrt(jnp.sum(kk_h * kk_h, axis=-1, keepdims=True))
    return kk_h / jnp.maximum(nrm, 1e-12)


def _wkv_finish_head(y, r_h, k_h, v_h, g_h, rk_h, lnw_h, lnb_h):
    mean = jnp.mean(y, axis=-1, keepdims=True)
    var = jnp.mean(jnp.square(y - mean), axis=-1, keepdims=True)
    yn = (y - mean) * lax.rsqrt(var + LN_X_EPS) * lnw_h + lnb_h
    bonus = jnp.sum(r_h * k_h * rk_h, axis=-1, keepdims=True) * v_h
    return (yn + bonus) * g_h


def _wkv_body(f_ref, fp_ref, mu_ref, w0_ref, ww2_ref, a0_ref, wa2_ref, wg2_ref, kk_ref, ka_ref,
              rk_ref, lnw_ref, lnb_ref, o_ref, st_ref, s_ref):
    c = pl.program_id(0)
    C = WKV_CHUNK
    nb = f_ref.shape[0]

    @pl.when(c == 0)
    def _():
        s_ref[...] = jnp.zeros_like(s_ref)

    f = jnp.concatenate([f_ref[b] for b in range(nb)], axis=0)
    row = lax.broadcasted_iota(I32, f.shape, 0)
    prev = pltpu.roll(f, 1, 0)
    for b in range(nb):
        prev = jnp.where(row == b * C, jnp.where(c == 0, 0.0, fp_ref[b][7:8, :]), prev)
    xs = f + mu_ref[...] * (prev - f)
    r, k, v, w_log, a, g, k_h = _rwkv_features(xs, w0_ref[...], ww2_ref[...], a0_ref[...],
                                               wa2_ref[...], wg2_ref[...], ka_ref[...])
    lw = -jnp.exp(w_log)
    kk = k * kk_ref[...]
    jh = lax.broadcasted_iota(I32, (D_B, LANES), 0) // HEAD_DIM_B
    ind = (jh == lax.broadcasted_iota(I32, (D_B, LANES), 1)).astype(BF16)
    ind_t = (lax.broadcasted_iota(I32, (LANES, D_B), 0)
             == lax.broadcasted_iota(I32, (LANES, D_B), 1) // HEAD_DIM_B).astype(BF16)

    def head_sum(z):
        hi = z.astype(BF16)
        lo = (z - hi.astype(F32)).astype(BF16)
        s = _dot(hi, ind) + _dot(lo, ind)
        shi = s.astype(BF16)
        slo = (s - shi.astype(F32)).astype(BF16)
        return _dot(shi, ind_t) + _dot(slo, ind_t)

    kkn = kk / jnp.maximum(jnp.sqrt(head_sum(kk * kk)), 1e-12)

    tr = lax.broadcasted_iota(I32, (nb * C, nb * C), 0)
    sr_ = lax.broadcasted_iota(I32, (nb * C, nb * C), 1)
    tri_incl = ((tr >= sr_) & (tr // C == sr_ // C)).astype(BF16)
    l1 = lw.astype(BF16)
    r1 = lw - l1.astype(F32)
    l2 = r1.astype(BF16)
    l3 = (r1 - l2.astype(F32)).astype(BF16)
    cum = _dot(tri_incl, l1) + _dot(tri_incl, l2) + _dot(tri_incl, l3)
    rhos = [cum[b * C + C // 2 - 1:b * C + C // 2, :] for b in range(nb)]
    rho = jnp.concatenate([jnp.broadcast_to(z, (C, D_B)) for z in rhos], axis=0)
    ep = jnp.exp(cum - rho)
    em = jnp.exp(rho - cum)
    e_a = ep * jnp.exp(-lw)
    r_hat = r * ep
    k_hat = k_h * em
    e_rs = [jnp.exp(z) for z in rhos]
    e_cs = [jnp.exp(cum[b * C + C - 1:b * C + C, :] - rhos[b]) for b in range(nb)]

    ti = lax.broadcasted_iota(I32, (C, C), 0)
    si = lax.broadcasted_iota(I32, (C, C), 1)
    strict = ti > si
    incl = ti >= si
    rk = rk_ref[...]
    lnw = lnw_ref[...]
    lnb = lnb_ref[...]
    items = [(b, h) for b in range(nb) for h in range(N_HEADS_B)]
    heads = range(len(items))
    lanes = [slice(h * HEAD_DIM_B, (h + 1) * HEAD_DIM_B) for _, h in items]
    cut = lambda z, i: z[items[i][0] * C:(items[i][0] + 1) * C, lanes[i]]
    e_r = [e_rs[b][:, lanes[i]] for i, (b, _) in enumerate(items)]
    e_c = [e_cs[b][:, lanes[i]] for i, (b, _) in enumerate(items)]
    a_hat_full = (-kkn * e_a).astype(BF16)
    b_hat_full = (kkn * a * em).astype(BF16)
    a_hat_b = [cut(a_hat_full, h) for h in heads]
    b_hat_b = [cut(b_hat_full, h) for h in heads]
    rh = [cut(r_hat, h) for h in heads]
    vb = [cut(v, h).astype(BF16) for h in heads]
    bk = [jnp.concatenate([b_hat_b[h], cut(k_hat, h).astype(BF16)], axis=0) for h in heads]
    p = [_dot_nt(jnp.concatenate([a_hat_b[h], rh[h].astype(BF16)], axis=0), bk[h]) for h in heads]
    l_ak = [jnp.where(strict, z[:C, C:], 0.0).astype(BF16) for z in p]
    p_rb = [jnp.where(incl, z[C:, :C], 0.0).astype(BF16) for z in p]
    p_rk = [jnp.where(incl, z[C:, C:], 0.0).astype(BF16) for z in p]
    col = lax.broadcasted_iota(I32, (C, 2 * C), 1)
    row2 = lax.broadcasted_iota(I32, (C, 2 * C), 0)
    left = col < C
    zt = [jnp.where(left, jnp.where(row2 > col, z[:C], 0.0), (col == row2 + C).astype(F32)) for z in p]
    for _ in range(int(math.log2(C))):
        zb = [z.astype(BF16) for z in zt]
        res = [_dot(z[:, :C], z) for z in zb]
        zt = [jnp.where(left, res[h], zt[h] + res[h]) for h in heads]
    tb = [z.astype(BF16) for z in zt]
    zeros_c = jnp.zeros((C, HEAD_DIM_B), BF16)
    lv = [_dot(l_ak[h], vb[h]).astype(BF16) for h in heads]
    a_bar = [_dot(tb[h], jnp.concatenate([zeros_c, a_hat_b[h]], axis=0)).astype(BF16) for h in heads]
    u_v = [_dot(tb[h], jnp.concatenate([zeros_c, lv[h]], axis=0)).astype(BF16) for h in heads]
    r_bar = [rh[h] + _dot(p_rb[h], a_bar[h]) for h in heads]
    y_v = [_dot(p_rb[h], u_v[h]) + _dot(p_rk[h], vb[h]) for h in heads]
    ab = [_dot_tn(a_bar[h], b_hat_b[h]).astype(BF16) for h in heads]
    n_t = [_dot_tn(jnp.concatenate([u_v[h], vb[h]], axis=0), bk[h]) for h in heads]
    s0 = [s_ref[b, h] for b, h in items]
    sr = [s0[h] * e_r[h] for h in heads]
    y = [_dot_nt((r_bar[h] * e_r[h]).astype(BF16), s0[h].astype(BF16)) + y_v[h] for h in heads]
    s_new = [(sr[h] + _dot(sr[h].astype(BF16), ab[h]) + n_t[h]) * e_c[h] for h in heads]
    for i, (b, h) in enumerate(items):
        s_ref[b, h] = s_new[i]
    y_full = jnp.concatenate([jnp.concatenate(y[b * N_HEADS_B:(b + 1) * N_HEADS_B], axis=1) for b in range(nb)],
                             axis=0)
    inv_hd = 1.0 / HEAD_DIM_B
    dev = y_full - head_sum(y_full) * inv_hd
    yn = dev * lax.rsqrt(head_sum(dev * dev) * inv_hd + LN_X_EPS) * lnw + lnb
    out = (yn + head_sum(r * k_h * rk) * v) * g
    for b in range(nb):
        o_ref[b] = out[b * C:(b + 1) * C, :]

    @pl.when(c == pl.num_programs(0) - 1)
    def _():
        st_ref[...] = s_ref[...]


def _wkv_call(feat, p):
    b, t, _ = feat.shape
    C = WKV_CHUNK
    nc = t // C
    row = lambda n: pl.BlockSpec((1, n), lambda c: (0, 0))
    mat = lambda m, n: pl.BlockSpec((m, n), lambda c: (0, 0))
    return pl.pallas_call(
        _wkv_body,
        out_shape=(jax.ShapeDtypeStruct((b, t, D_B), F32),
                   jax.ShapeDtypeStruct((b, N_HEADS_B, HEAD_DIM_B, HEAD_DIM_B), F32)),
        grid=(nc,),
        in_specs=[pl.BlockSpec((b, C, D_SHIFT_B), lambda c: (0, c, 0)),
                  pl.BlockSpec((b, 8, D_SHIFT_B), lambda c: (0, jnp.maximum(c * (C // 8) - 1, 0), 0)),
                  row(D_SHIFT_B), row(D_B), mat(DECAY_LORA, D_B), row(D_B), mat(AAA_LORA, D_B),
                  mat(GATE_LORA, D_B), row(D_B), row(D_B), row(D_B), row(D_B), row(D_B)],
        out_specs=(pl.BlockSpec((b, C, D_B), lambda c: (0, c, 0)),
                   pl.BlockSpec((b, N_HEADS_B, HEAD_DIM_B, HEAD_DIM_B), lambda c: (0, 0, 0, 0))),
        scratch_shapes=[pltpu.VMEM((b, N_HEADS_B, HEAD_DIM_B, HEAD_DIM_B), F32)],
        compiler_params=_cparams(("arbitrary",)),
        name="wkv",
    )(feat, feat, p['mu_b'], p['w0_b'], p['w_w2_b'], p['a0_b'], p['w_a2_b'], p['w_g2_b'],
      p['k_k_b'], p['k_a_b'], p['r_k_b'], p['ln_x_w_b'], p['ln_x_b_b'])


def _swkv_prep_body(f_ref, sh_ref, mu_ref, w0_ref, ww2_ref, a0_ref, wa2_ref, wg2_ref, kk_ref, ka_ref,
                    r_ref, w_ref, k_ref, v_ref, aa_ref, bb_ref, g_ref):
    f = f_ref[...]
    xs = f + mu_ref[...] * (sh_ref[...] - f)
    r, k, v, w_log, a, g, k_h = _rwkv_features(xs, w0_ref[...], ww2_ref[...], a0_ref[...],
                                               wa2_ref[...], wg2_ref[...], ka_ref[...])
    kk = k * kk_ref[...]
    kkn = jnp.concatenate([_head_norm(kk[:, h * HEAD_DIM_B:(h + 1) * HEAD_DIM_B]) for h in range(N_HEADS_B)],
                          axis=1)
    r_ref[...] = r
    w_ref[...] = jnp.exp(-jnp.exp(w_log))
    k_ref[...] = k_h
    v_ref[...] = v
    aa_ref[...] = -kkn
    bb_ref[...] = kkn * a
    g_ref[...] = g


def _swkv_prep_call(feat_s, shift0, p):
    n = feat_s.shape[0]
    full = lambda a: pl.BlockSpec(a.shape, lambda: tuple(0 for _ in a.shape))
    args = (feat_s, shift0, p['mu_b'], p['w0_b'], p['w_w2_b'], p['a0_b'], p['w_a2_b'], p['w_g2_b'],
            p['k_k_b'], p['k_a_b'])
    return pl.pallas_call(
        _swkv_prep_body,
        out_shape=tuple(jax.ShapeDtypeStruct((n, D_B), F32) for _ in range(7)),
        in_specs=[full(a) for a in args],
        out_specs=tuple(pl.BlockSpec((n, D_B), lambda: (0, 0)) for _ in range(7)),
        compiler_params=pltpu.CompilerParams(vmem_limit_bytes=VMEM_LIMIT),
        name="swkv_prep",
    )(*args)


def _swkv_step_body(s_ref, a_ref, w_ref, b_ref, k_ref, r_ref, v_ref, so_ref, y_ref):
    s = s_ref[...]
    th = s.shape[0]
    pad_sq = lambda z: jnp.concatenate(
        [jnp.concatenate([z, jnp.zeros((z.shape[0], LANES - z.shape[1]), F32)], axis=1),
         jnp.zeros((LANES - z.shape[0], LANES), F32)], axis=0)
    v_t = pad_sq(v_ref[...]).T
    v_col = jnp.stack([v_t[:HEAD_DIM_B, j:j + 1] for j in range(th)], axis=0)
    sa = jnp.sum(s * a_ref[...], axis=-1, keepdims=True)
    s2 = s * w_ref[...] + sa * b_ref[...] + v_col * k_ref[...]
    so_ref[...] = s2
    y = jnp.sum(s2 * r_ref[...], axis=-1, keepdims=True)
    y_t = jnp.concatenate([y[j] for j in range(th)], axis=1)
    y_ref[...] = pad_sq(y_t).T[:th, :HEAD_DIM_B]


def _swkv_step_call(s0, aa, w, bb, k, r, v):
    nh = s0.shape[0]
    th = 64
    rowspec = pl.BlockSpec((th, 1, HEAD_DIM_B), lambda i: (i, 0, 0))
    matspec = pl.BlockSpec((th, HEAD_DIM_B), lambda i: (i, 0))
    stspec = pl.BlockSpec((th, HEAD_DIM_B, HEAD_DIM_B), lambda i: (i, 0, 0))
    return pl.pallas_call(
        _swkv_step_body,
        out_shape=(jax.ShapeDtypeStruct((nh, HEAD_DIM_B, HEAD_DIM_B), F32),
                   jax.ShapeDtypeStruct((nh, HEAD_DIM_B), F32)),
        grid=(nh // th,),
        in_specs=[stspec, rowspec, rowspec, rowspec, rowspec, rowspec, matspec],
        out_specs=(stspec, matspec),
        compiler_params=_cparams(("arbitrary",)),
        name="swkv_step",
    )(s0, aa, w, bb, k, r, v)


def _swkv_fin_body(y_ref, r_ref, k_ref, v_ref, g_ref, rk_ref, lnw_ref, lnb_ref, o_ref):
    y, r, k, v, g = y_ref[...], r_ref[...], k_ref[...], v_ref[...], g_ref[...]
    rk, lnw, lnb = rk_ref[...], lnw_ref[...], lnb_ref[...]
    outs = []
    for h in range(N_HEADS_B):
        sl = slice(h * HEAD_DIM_B, (h + 1) * HEAD_DIM_B)
        outs.append(_wkv_finish_head(y[:, sl], r[:, sl], k[:, sl], v[:, sl], g[:, sl],
                                     rk[:, sl], lnw[:, sl], lnb[:, sl]))
    o_ref[...] = jnp.concatenate(outs, axis=1)


def _swkv_fin_call(y, r, k, v, g, p):
    n = y.shape[0]
    args = (y, r, k, v, g, p['r_k_b'], p['ln_x_w_b'], p['ln_x_b_b'])
    full = lambda a: pl.BlockSpec(a.shape, lambda: (0, 0))
    return pl.pallas_call(
        _swkv_fin_body,
        out_shape=jax.ShapeDtypeStruct((n, D_B), F32),
        in_specs=[full(a) for a in args],
        out_specs=pl.BlockSpec((n, D_B), lambda: (0, 0)),
        name="swkv_fin",
    )(*args)


def _route_t(scores, bias_col):
    n = scores.shape[1]
    gsz = N_EXPERTS // N_EXPERT_GROUPS
    choice = scores + bias_col
    ninf = -jnp.inf
    sid = lax.broadcasted_iota(I32, (gsz, n), 0)
    gs = []
    for gidx in range(N_EXPERT_GROUPS):
        blk = choice[gidx * gsz:(gidx + 1) * gsz, :]
        m1 = jnp.max(blk, axis=0, keepdims=True)
        first = jnp.min(jnp.where(blk == m1, sid, gsz), axis=0, keepdims=True)
        m2 = jnp.max(jnp.where(sid == first, ninf, blk), axis=0, keepdims=True)
        gs.append(m1 + m2)
    cur = jnp.concatenate(gs, axis=0)
    gid = lax.broadcasted_iota(I32, (N_EXPERT_GROUPS, n), 0)
    gmask = jnp.zeros((N_EXPERT_GROUPS, n), F32)
    for _ in range(TOPK_GROUPS):
        m = jnp.max(cur, axis=0, keepdims=True)
        first = jnp.min(jnp.where(cur == m, gid, N_EXPERT_GROUPS), axis=0, keepdims=True)
        sel = gid == first
        gmask = jnp.where(sel, 1.0, gmask)
        cur = jnp.where(sel, ninf, cur)
    emask = jnp.concatenate([jnp.broadcast_to(gmask[gidx:gidx + 1, :], (gsz, n))
                             for gidx in range(N_EXPERT_GROUPS)], axis=0)
    cur = jnp.where(emask > 0.5, choice, ninf)
    eid = lax.broadcasted_iota(I32, (N_EXPERTS, n), 0)
    selm = jnp.zeros((N_EXPERTS, n), F32)
    for _ in range(TOP_K):
        m = jnp.max(cur, axis=0, keepdims=True)
        first = jnp.min(jnp.where(cur == m, eid, N_EXPERTS), axis=0, keepdims=True)
        sel = eid == first
        selm = jnp.where(sel, 1.0, selm)
        cur = jnp.where(sel, ninf, cur)
    w = jnp.where(selm > 0.5, scores, 0.0)
    w = w / jnp.sum(w, axis=0, keepdims=True) * ROUTED_SCALE
    return jnp.where(selm > 0.5, w, -1.0)


def _unpermute(blk_ref, scr_ref, dil, tm):
    if dil == 1:
        return blk_ref[0, 0].astype(F32)
    n_chunks = scr_ref.shape[0]
    for r in range(dil):
        rows = blk_ref[0, r].astype(F32)
        for j in range(n_chunks):
            scr_ref[j, pl.ds(r, tm // dil, stride=dil), :] = rows[:, j * LANES:(j + 1) * LANES]
    return jnp.concatenate([scr_ref[j] for j in range(n_chunks)], axis=1)


def _post_body(*refs, combine, dils):
    if combine:
        o_refs, l_refs, rest = refs[:3], refs[3:6], refs[6:]
    else:
        o_refs, rest = refs[:1], refs[1:]
    (ob_ref, gt_ref, x_ref, g1_ref, sc2_ref, sh2_ref, npost_ref, npre_ref, wa_ref, wb_ref, wo_ref,
     wrt_ref, rb_ref, x1_ref, hp_ref, wt_ref) = rest[:16]
    scr = rest[16:]
    tm = x_ref.shape[1]
    if combine:
        os_, ls_ = [], []
        si = 0
        for gi, dil in enumerate(dils):
            os_.append(_unpermute(o_refs[gi], scr[si] if dil > 1 else None, dil, tm))
            ls_.append(_unpermute(l_refs[gi], scr[si + 1] if dil > 1 else None, dil, tm))
            si += 2 if dil > 1 else 0
        mx = jnp.maximum(jnp.maximum(ls_[0], ls_[1]), ls_[2])
        es = [jnp.exp(z - mx) for z in ls_]
        o_a = (es[0] * os_[0] + es[1] * os_[1] + es[2] * os_[2]) / (es[0] + es[1] + es[2])
    else:
        o_a = o_refs[0][0]
    gt = gt_ref[0].astype(F32)
    za = _dot(o_a.astype(BF16), wa_ref[...])
    zb = _dot(ob_ref[0].astype(BF16), wb_ref[...])
    merged = gt[:, :D_MODEL] * za + gt[:, D_MODEL:] * zb
    z = _dot(merged.astype(BF16), wo_ref[...])
    x1 = x_ref[0] + g1_ref[0] * _rms(z, npost_ref[...])
    x1_ref[0] = x1
    h2 = _rms(x1, npre_ref[...]) * (1.0 + sc2_ref[0]) + sh2_ref[0]
    packed = _pack_pairs(h2)
    for s in range(ROW_TILE_SUBLANES):
        hp_ref[0, pl.ds(s, tm, stride=ROW_TILE_SUBLANES), :] = packed[:, s * LANES:(s + 1) * LANES]
    tp =-(-tm // LANES) * LANES
    if tp != tm:
        h2 = jnp.concatenate([h2, jnp.zeros((tp - tm, D_MODEL), F32)], axis=0)
    logits_t = _dot_nt_split(wrt_ref[...], h2)
    w = _route_t(_sigmoid(logits_t[:N_EXPERTS, :]), rb_ref[...])
    wt_ref[...] = w[:, :tm]


def _post_call(o_parts, lse_parts, ob, gates, x, gate1, scale2, shift2, p, wa, wb, wo, wrt, rb, tm, mod_per_row):
    nb, t, _ = x.shape
    nt = t // tm
    combine = lse_parts is not None
    rowblk = lambda width: pl.BlockSpec((1, tm, width), lambda b, i: (b, i, 0))
    if mod_per_row:
        mod_spec = rowblk(D_MODEL)
    else:
        mod_spec = pl.BlockSpec((1, 1, D_MODEL), lambda b, i: (b, 0, 0))
    const = lambda shp: pl.BlockSpec(shp, lambda b, i: (0, 0))
    scratch = []
    if combine:
        dils = tuple(o.shape[1] for o in o_parts)
        o_args = list(o_parts) + list(lse_parts)
        o_specs = [pl.BlockSpec((1, d, tm // d, D_GROUP_A), lambda b, i: (b, 0, i, 0)) for d in dils] * 2
        for d in dils:
            if d > 1:
                scratch += [pltpu.VMEM((D_GROUP_A // LANES, tm, LANES), F32)] * 2
    else:
        dils = ()
        o_args = [o_parts[0]]
        o_specs = [rowblk(D_GROUP_A)]
    return pl.pallas_call(
        functools.partial(_post_body, combine=combine, dils=dils),
        out_shape=(jax.ShapeDtypeStruct((nb, t, D_MODEL), F32),
                   jax.ShapeDtypeStruct((nb, t * ROW_TILE_SUBLANES, LANES), I32),
                   jax.ShapeDtypeStruct((N_EXPERTS, nb * t), F32)),
        grid=(nb, nt),
        in_specs=o_specs + [rowblk(D_B), rowblk(2 * D_MODEL), rowblk(D_MODEL),
                            mod_spec, mod_spec, mod_spec, const((1, D_MODEL)), const((1, D_MODEL)),
                            const((D_GROUP_A, D_MODEL)), const((D_B, D_MODEL)), const((D_MODEL, D_MODEL)),
                            const((LANES, D_MODEL)), const((N_EXPERTS, 1))],
        out_specs=(rowblk(D_MODEL),
                   pl.BlockSpec((1, tm * ROW_TILE_SUBLANES, LANES), lambda b, i: (b, i, 0)),
                   pl.BlockSpec((N_EXPERTS, tm), lambda b, i: (0, b * nt + i))),
        scratch_shapes=scratch,
        compiler_params=_cparams(("arbitrary", "arbitrary")),
        name="post",
    )(*o_args, ob, gates, x, gate1, scale2, shift2, p['norm_post_mix'].reshape(1, -1),
      p['norm_pre_ffn'].reshape(1, -1), wa, wb, wo, wrt, rb)


def _rank_body(w_ref, dest_ref, w8_ref, tab_ref, etab_ref, cnt_ref, pst_ref, run_ref, *, n_real, n_slots):
    ph = pl.program_id(0)
    i = pl.program_id(1)
    T = MOE_TILE
    w = w_ref[...]
    sel = (w >= 0.0).astype(F32)
    cnt_tile = jnp.broadcast_to(jnp.sum(sel, axis=1, keepdims=True), (N_EXPERTS, LANES))
    ei = lax.broadcasted_iota(I32, (N_EXPERTS, N_EXPERTS), 0)
    ej = lax.broadcasted_iota(I32, (N_EXPERTS, N_EXPERTS), 1)

    @pl.when((ph == 0) & (i == 0))
    def _():
        cnt_ref[...] = jnp.zeros_like(cnt_ref)

    @pl.when(ph == 0)
    def _():
        cnt_ref[...] += cnt_tile

    @pl.when((ph == 1) & (i == 0))
    def _():
        cnt = cnt_ref[...]
        padded = jnp.floor((cnt + (EXPERT_BLOCK - 1)) / EXPERT_BLOCK) * EXPERT_BLOCK
        pstart = _dot_exact((ej < ei).astype(F32), padded)
        pst_ref[...] = pstart
        run_ref[...] = jnp.zeros_like(run_ref)
        pend = pstart + padded
        vend = pstart + cnt
        esub = lax.broadcasted_iota(I32, (N_EXPERTS, LANES), 0)
        lane = lax.broadcasted_iota(I32, (1, LANES), 1)
        tab_ref[...] = jnp.zeros_like(tab_ref)
        for c in range(tab_ref.shape[1] // LANES):
            bs = ((c * LANES + lane) * EXPERT_BLOCK).astype(F32)
            be = jnp.minimum(jnp.sum((pend <= bs).astype(F32), axis=0, keepdims=True), N_EXPERTS - 1.0)
            tab_ref[0:1, c * LANES:(c + 1) * LANES] = be.astype(I32)
            tab_ref[1:2, c * LANES:(c + 1) * LANES] = (pend[N_EXPERTS - 1:, :] / EXPERT_BLOCK).astype(I32)
        on_diag = esub == lax.broadcasted_iota(I32, (N_EXPERTS, LANES), 1)
        etab_ref[...] = jnp.zeros_like(etab_ref)
        lo = jnp.sum(jnp.where(on_diag, vend, 0.0), axis=0, keepdims=True)
        hi = jnp.sum(jnp.where(on_diag, pend, 0.0), axis=0, keepdims=True)
        etab_ref[0:1, :] = jnp.where(lane == N_EXPERTS, pend[N_EXPERTS - 1:, :], lo).astype(I32)
        etab_ref[1:2, :] = jnp.where(lane == N_EXPERTS, float(n_slots), hi).astype(I32)

    @pl.when(ph == 1)
    def _():
        ti = lax.broadcasted_iota(I32, (T, T), 0)
        tj = lax.broadcasted_iota(I32, (T, T), 1)
        selb = sel.astype(BF16)
        rank = _dot(selb, (ti < tj).astype(BF16))
        ordn = _dot((ej < ei).astype(BF16), selb)
        dest_e = pst_ref[:, :1] + run_ref[:, :1] + rank
        run_ref[...] += cnt_tile
        tok = i * T + lax.broadcasted_iota(I32, (1, T), 1)
        dks, wks = [], []
        for k in range(TOP_K):
            m = (sel > 0.5) & (ordn == float(k))
            dk = jnp.sum(jnp.where(m, dest_e, 0.0), axis=0, keepdims=True)
            wk = jnp.sum(jnp.where(m, w, 0.0), axis=0, keepdims=True)
            dks.append(jnp.where(tok < n_real, dk, 0.0))
            wks.append(jnp.where(tok < n_real, wk, 0.0))
        dest_ref[...] = jnp.concatenate(dks, axis=0).astype(I32)
        w8_ref[...] = jnp.concatenate(wks, axis=0)


def _rank_call(w_t, n_real, n_blocks, n_blocks_pad):
    n = w_t.shape[1]
    nt = n // MOE_TILE
    return pl.pallas_call(
        functools.partial(_rank_body, n_real=n_real, n_slots=n_blocks * EXPERT_BLOCK),
        out_shape=(jax.ShapeDtypeStruct((TOP_K, n), I32),
                   jax.ShapeDtypeStruct((TOP_K, n), F32),
                   jax.ShapeDtypeStruct((8, n_blocks_pad), I32),
                   jax.ShapeDtypeStruct((8, LANES), I32)),
        grid=(2, nt),
        in_specs=[pl.BlockSpec((N_EXPERTS, MOE_TILE), lambda ph, i: (0, i))],
        out_specs=(pl.BlockSpec((TOP_K, MOE_TILE), lambda ph, i: (0, i * ph)),
                   pl.BlockSpec((TOP_K, MOE_TILE), lambda ph, i: (0, i * ph)),
                   pl.BlockSpec((8, n_blocks_pad), lambda ph, i: (0, 0)),
                   pl.BlockSpec((8, LANES), lambda ph, i: (0, 0))),
        scratch_shapes=[pltpu.VMEM((N_EXPERTS, LANES), F32)] * 3,
        compiler_params=_cparams(("arbitrary", "arbitrary")),
        name="rank",
    )(w_t)


def _tile_rows(ref, row, n):
    return ref.at[pl.ds(pl.multiple_of(row * ROW_TILE_SUBLANES, ROW_TILE_SUBLANES), n * ROW_TILE_SUBLANES)]


def _zero_fill(etab_ref, zbuf, xs_hbm, zsem, wait):
    def go(src, dst):
        cp = pltpu.make_async_copy(src, dst, zsem)
        if wait:
            cp.wait()
        else:
            cp.start()

    def per_range(e, carry):
        lo = etab_ref[0, e]
        n = etab_ref[1, e] - lo
        n_full = n // ZERO_ROWS

        def full(j, c):
            go(zbuf, _tile_rows(xs_hbm, lo + j * ZERO_ROWS, ZERO_ROWS))
            return c

        lax.fori_loop(0, n_full, full, 0)
        pos = lo + n_full * ZERO_ROWS
        rem = n - n_full * ZERO_ROWS
        size = ZERO_ROWS // 2
        while size >= 1:
            bit = rem & size

            @pl.when(bit != 0)
            def _(size=size, pos=pos):
                go(_tile_rows(zbuf, 0, size), _tile_rows(xs_hbm, pos, size))

            pos = pos + bit
            size //= 2
        return carry

    lax.fori_loop(0, N_EXPERTS + 1, per_range, 0)


def _dispatch_body(dest_ref, etab_ref, xa_ref, xb_ref, xs_hbm, zbuf, sem, zsem, *, n_real, n_full):
    i = pl.program_id(0)
    T = MOE_TILE
    n_tok = jnp.clip(n_real - i * T, 0, T)

    def issue_from(x_ref):
        def issue(t, carry):
            for k in range(TOP_K):
                pltpu.make_async_copy(_tile_rows(x_ref, t, 1), _tile_rows(xs_hbm, dest_ref[k * T + t], 1),
                                      sem).start(priority=k % 2)
            return carry

        lax.fori_loop(0, n_tok, issue, 0)

    @pl.when(i < n_full)
    def _():
        issue_from(xa_ref)

    @pl.when(i >= n_full)
    def _():
        issue_from(xb_ref)

    @pl.when(i == 0)
    def _():
        zbuf[...] = jnp.zeros_like(zbuf)
        _zero_fill(etab_ref, zbuf, xs_hbm, zsem, wait=False)
        _zero_fill(etab_ref, zbuf, xs_hbm, zsem, wait=True)

    @pl.when(n_tok == T)
    def _():
        pltpu.make_async_copy(_tile_rows(xs_hbm, 0, T * TOP_K), _tile_rows(xs_hbm, 0, T * TOP_K), sem).wait()

    @pl.when(n_tok < T)
    def _():
        def drain(j, carry):
            pltpu.make_async_copy(_tile_rows(xs_hbm, 0, 1), _tile_rows(xs_hbm, 0, 1), sem).wait()
            return carry

        lax.fori_loop(0, n_tok * TOP_K, drain, 0)


def _dispatch_call(dest, etab, hp_a, hp_b, n_real, n_slots):
    tile_rows = MOE_TILE * ROW_TILE_SUBLANES
    n_full = hp_a.shape[0] // tile_rows
    return pl.pallas_call(
        functools.partial(_dispatch_body, n_real=n_real, n_full=n_full),
        out_shape=jax.ShapeDtypeStruct((n_slots * ROW_TILE_SUBLANES, LANES), I32),
        grid=(n_full + 1,),
        in_specs=[pl.BlockSpec((TOP_K * MOE_TILE,), lambda i: (i,), memory_space=pltpu.SMEM),
                  pl.BlockSpec((8, LANES), lambda i: (0, 0), memory_space=pltpu.SMEM),
                  pl.BlockSpec((tile_rows, LANES), lambda i: (jnp.minimum(i, n_full - 1), 0)),
                  pl.BlockSpec((tile_rows, LANES), lambda i: (0, 0))],
        out_specs=pl.BlockSpec(memory_space=pl.ANY),
        scratch_shapes=[pltpu.VMEM((ZERO_ROWS * ROW_TILE_SUBLANES, LANES), I32),
                        pltpu.SemaphoreType.DMA, pltpu.SemaphoreType.DMA],
        compiler_params=_cparams(("arbitrary",)),
        name="dispatch",
    )(dest, etab, hp_a, hp_b)


def _rows_from_tiles(ref, lo, n):
    return jnp.concatenate([ref[pl.ds(lo * ROW_TILE_SUBLANES + s, n, stride=ROW_TILE_SUBLANES), :]
                            for s in range(ROW_TILE_SUBLANES)], axis=1)


def _ffn_body(be_ref, nu_ref, xs_hbm, wg_hbm, wu_hbm, wd_hbm, ys_ref, xbuf, wgf, wuf, wdf, wgb, wub, wdb, slot_ref,
              sem, xsem):
    j = pl.program_id(0)
    n_used = nu_ref[0]

    def fetch(e, slot):
        return [pltpu.make_async_copy(w_hbm.at[e], wf.at[slot], sem.at[slot])
                for w_hbm, wf in ((wg_hbm, wgf), (wu_hbm, wuf), (wd_hbm, wdf))]

    def rows_in(b):
        ring = b % FFN_RING
        return pltpu.make_async_copy(_tile_rows(xs_hbm, b * EXPERT_BLOCK, EXPERT_BLOCK), xbuf.at[ring], xsem.at[ring])

    @pl.when(j < n_used)
    def _():
        e = be_ref[j]

        @pl.when(j == 0)
        def _():
            slot_ref[0] = 0
            for cp in fetch(e, 0):
                cp.start()
            for b in range(FFN_RING - 1):
                @pl.when(b < n_used)
                def _(b=b):
                    rows_in(b).start()

        @pl.when(j + FFN_RING - 1 < n_used)
        def _():
            rows_in(j + FFN_RING - 1).start()

        @pl.when((j == 0) | (e != be_ref[jnp.maximum(j - 1, 0)]))
        def _():
            slot = slot_ref[0]
            for cp in fetch(e, slot):
                cp.wait()
            wgb[...] = wgf[slot].astype(BF16)
            wub[...] = wuf[slot].astype(BF16)
            wdb[...] = wdf[slot].astype(BF16)
            last = be_ref.shape[0] - 1
            nxt = lax.while_loop(lambda i: (i < n_used) & (be_ref[jnp.minimum(i, last)] == e), lambda i: i + 1, j + 1)

            @pl.when(nxt < n_used)
            def _():
                for cp in fetch(be_ref[jnp.minimum(nxt, last)], 1 - slot):
                    cp.start(priority=1)

            slot_ref[0] = 1 - slot

        rows_in(j).wait()
        x = _unpack_pairs(_rows_from_tiles(xbuf.at[j % FFN_RING], 0, EXPERT_BLOCK)).astype(BF16)
        act = _silu(_dot(x, wgb[...])) * _dot(x, wub[...])
        y = _dot(act.astype(BF16), wdb[...])
        packed = _pack_pairs(y)
        for s in range(ROW_TILE_SUBLANES):
            ys_ref[pl.ds(s, EXPERT_BLOCK, stride=ROW_TILE_SUBLANES), :] = packed[:, s * LANES:(s + 1) * LANES]

    @pl.when(j >= nu_ref[0])
    def _():
        ys_ref[...] = jnp.zeros_like(ys_ref)


def _ffn_call(blk_e, n_used, xs, w_gate, w_up, w_down, n_blocks):
    tile_blk = pl.BlockSpec((EXPERT_BLOCK * ROW_TILE_SUBLANES, LANES), lambda j, be, nu: (j, 0))
    grid_spec = pltpu.PrefetchScalarGridSpec(
        num_scalar_prefetch=2,
        grid=(n_blocks,),
        in_specs=[pl.BlockSpec(memory_space=pl.ANY)] * 4,
        out_specs=tile_blk,
        scratch_shapes=[pltpu.VMEM((FFN_RING, EXPERT_BLOCK * ROW_TILE_SUBLANES, LANES), I32),
                        pltpu.VMEM((2, D_MODEL, D_EXPERT), F32), pltpu.VMEM((2, D_MODEL, D_EXPERT), F32),
                        pltpu.VMEM((2, D_EXPERT, D_MODEL), F32),
                        pltpu.VMEM((D_MODEL, D_EXPERT), BF16), pltpu.VMEM((D_MODEL, D_EXPERT), BF16),
                        pltpu.VMEM((D_EXPERT, D_MODEL), BF16), pltpu.SMEM((1,), I32), pltpu.SemaphoreType.DMA((2,)),
                        pltpu.SemaphoreType.DMA((FFN_RING,))])
    return pl.pallas_call(
        _ffn_body,
        out_shape=jax.ShapeDtypeStruct((n_blocks * EXPERT_BLOCK * ROW_TILE_SUBLANES, LANES), I32),
        grid_spec=grid_spec,
        compiler_params=_cparams(("arbitrary",)),
        name="ffn",
    )(blk_e, n_used, xs, w_gate, w_up, w_down)


def _combine_body(dest_ref, dnext_ref, w8_ref, xa_ref, xb_ref, x1a_ref, x1b_ref, g2a_ref, g2b_ref, gain_ref,
                  sg_ref, su_ref, sd_ref, ys_hbm, oa_ref, ob_ref, buf, sem):
    j = pl.program_id(0)
    T = COMBINE_TILE
    RC = COMBINE_ROWS

    def issue(d_ref, slot, t):
        for k in range(TOP_K):
            pltpu.make_async_copy(_tile_rows(ys_hbm, d_ref[k * T + t], 1), _tile_rows(buf.at[slot], k * T + t, 1),
                                  sem.at[slot]).start(priority=k % 2)

    def wait(slot):
        pltpu.make_async_copy(_tile_rows(ys_hbm, 0, T * TOP_K), buf.at[slot], sem.at[slot]).wait()

    def step(slot):
        is_tail = j == 0
        wait(slot)
        for t in range(RC):
            issue(dnext_ref, 1 - slot, t)
        x = _unpack_pairs(jnp.where(is_tail, _rows_from_tiles(xb_ref, 0, T),
                                    _rows_from_tiles(xa_ref, 0, T))).astype(BF16)
        shared = _dot((_silu(_dot(x, sg_ref[...])) * _dot(x, su_ref[...])).astype(BF16), sd_ref[...])
        w_t = jnp.concatenate([w8_ref[...], jnp.zeros((LANES - TOP_K, T), F32)], axis=0).T
        oa_ref[...] = shared
        for r0 in range(0, T, RC):
            if r0 > 0:
                for t in range(r0, r0 + RC):
                    issue(dnext_ref, 1 - slot, t)
            acc = oa_ref[r0:r0 + RC, :]
            for k in range(TOP_K):
                acc = acc + w_t[r0:r0 + RC, k:k + 1] * _unpack_pairs(_rows_from_tiles(buf.at[slot], k * T + r0, RC))
            x1 = jnp.where(is_tail, x1b_ref[r0:r0 + RC, :], x1a_ref[r0:r0 + RC, :])
            g2 = jnp.where(is_tail, g2b_ref[r0:r0 + RC, :], g2a_ref[0])
            oa_ref[r0:r0 + RC, :] = x1 + g2 * _rms(acc, gain_ref[...])

        @pl.when(is_tail)
        def _():
            ob_ref[...] = oa_ref[...]

        @pl.when(j + 1 == pl.num_programs(0))
        def _():
            wait(1 - slot)

    @pl.when(j == 0)
    def _():
        lax.fori_loop(0, T, lambda t, c: (issue(dest_ref, 0, t), c)[1], 0, unroll=2)

    @pl.when(j % 2 == 0)
    def _():
        step(0)

    @pl.when(j % 2 == 1)
    def _():
        step(1)


def _combine_call(dest, w8, hp_a, hp_b, x1_a, x1_b, gate2_a, gate2_b, gain, wsg, wsu, wsd, ys):
    T = COMBINE_TILE
    tile_rows = T * ROW_TILE_SUBLANES
    n_full = hp_a.shape[0] // tile_rows
    n_tiles = n_full + 1
    seq = x1_a.shape[0] // gate2_a.shape[0]
    tile_of = lambda j: jnp.where(j == 0, n_full, j - 1)
    full_of = lambda j: jnp.maximum(j - 1, 0)
    const = lambda shp: pl.BlockSpec(shp, lambda j: (0, 0))
    return pl.pallas_call(
        _combine_body,
        out_shape=(jax.ShapeDtypeStruct((n_full * T, D_MODEL), F32), jax.ShapeDtypeStruct((T, D_MODEL), F32)),
        grid=(n_tiles,),
        in_specs=[pl.BlockSpec((TOP_K * T,), lambda j: (tile_of(j),), memory_space=pltpu.SMEM),
                  pl.BlockSpec((TOP_K * T,), lambda j: (tile_of(jnp.minimum(j + 1, n_tiles - 1)),),
                               memory_space=pltpu.SMEM),
                  pl.BlockSpec((TOP_K, T), lambda j: (0, tile_of(j))),
                  pl.BlockSpec((tile_rows, LANES), lambda j: (full_of(j), 0)),
                  pl.BlockSpec((tile_rows, LANES), lambda j: (0, 0)),
                  pl.BlockSpec((T, D_MODEL), lambda j: (full_of(j), 0)),
                  const((T, D_MODEL)),
                  pl.BlockSpec((1, 1, D_MODEL), lambda j: (full_of(j) * T // seq, 0, 0)),
                  const((T, D_MODEL)), const((1, D_MODEL)),
                  const((D_MODEL, D_EXPERT)), const((D_MODEL, D_EXPERT)), const((D_EXPERT, D_MODEL)),
                  pl.BlockSpec(memory_space=pl.ANY)],
        out_specs=(pl.BlockSpec((T, D_MODEL), lambda j: (full_of(j), 0)), const((T, D_MODEL))),
        scratch_shapes=[pltpu.VMEM((2, TOP_K * tile_rows, LANES), I32), pltpu.SemaphoreType.DMA((2,))],
        compiler_params=_cparams(("arbitrary",)),
        name="combine",
    )(dest, dest, w8, hp_a, hp_b, x1_a, x1_b, gate2_a, gate2_b, gain.reshape(1, -1), wsg, wsu, wsd, ys)


def _rope_tables(pos):
    half = HEAD_DIM_A // 2
    inv_freq = ROPE_THETA ** (-jnp.arange(half, dtype=F32) / half)
    ang = pos.astype(F32)[:, None] * inv_freq[None, :]
    cos = jnp.cos(ang)
    sin = jnp.sin(ang)
    reps = LANES // HEAD_DIM_A
    cos_t = jnp.tile(jnp.concatenate([cos, cos], axis=1), (1, reps))
    sin_t = jnp.tile(jnp.concatenate([-sin, sin], axis=1), (1, reps))
    return cos_t, sin_t


def kernel(x_prompt, x_sample, c_prompt, c_sample, cache_a1_kv, cache_a2_kv, cache_a3_kv, state_b_wkv, state_b_shift, w_ada, b_ada, norm_pre_mix, norm_post_mix, norm_pre_ffn, norm_post_ffn, w_in, w_a_out, mu_b, w0_b, w_w2_b, a0_b, w_a2_b, w_g2_b, k_k_b, k_a_b, r_k_b, ln_x_w_b, ln_x_b_b, w_b_out, w_out, w_router, router_bias, w_e_gate, w_e_up, w_e_down, w_s_gate, w_s_up, w_s_down):
    assert DEPTH == 1
    l = 0
    nd = DEC_BATCH
    row = lambda a: a.reshape(1, -1)
    p = {'mu_b': row(mu_b[l]), 'w0_b': row(w0_b[l]), 'w_w2_b': w_w2_b[l], 'a0_b': row(a0_b[l]),
         'w_a2_b': w_a2_b[l], 'w_g2_b': w_g2_b[l], 'k_k_b': row(k_k_b[l]), 'k_a_b': row(k_a_b[l]),
         'r_k_b': row(r_k_b[l]), 'ln_x_w_b': row(ln_x_w_b[l]), 'ln_x_b_b': row(ln_x_b_b[l]),
         'norm_post_mix': norm_post_mix[l], 'norm_pre_ffn': norm_pre_ffn[l]}

    wq, wf, wg = _wsplit_call(w_in[l])
    wa = w_a_out[l].astype(BF16)
    wb = w_b_out[l].astype(BF16)
    wo = w_out[l].astype(BF16)
    wrt = jnp.concatenate([w_router[l].T, jnp.zeros((LANES - N_EXPERTS, D_MODEL), F32)], axis=0)
    rb = router_bias[l].reshape(N_EXPERTS, 1)
    wsg, wsu, wsd = w_s_gate[l].astype(BF16), w_s_up[l].astype(BF16), w_s_down[l].astype(BF16)

    n_c = BATCH + nd
    c_all = jnp.concatenate([c_prompt, c_sample, jnp.zeros((-n_c % 8, D_MODEL), F32)], axis=0)
    mod = _mod_call(c_all, w_ada[l], b_ada[l])
    mod_p = [m.reshape(BATCH, 1, D_MODEL) for m in jnp.split(mod[:BATCH], 6, axis=-1)]
    mod_s = [m.reshape(1, nd, D_MODEL) for m in jnp.split(mod[BATCH:n_c], 6, axis=-1)]

    cos_p, sin_p = _rope_tables(jnp.arange(SEQ, dtype=I32))
    cos_s, sin_s = _rope_tables(jnp.full((nd,), PAST_LEN, I32))

    keep_p = [min(w, SEQ) for w, _ in DILATED_GROUPS]
    dils = tuple(d for _, d in DILATED_GROUPS)

    q0, q1, q2, feat_p, gates_p, *tails_p = _inproj_call(
        x_prompt, norm_pre_mix[l], mod_p[1], mod_p[0], cos_p, sin_p, wq, wf, wg,
        tm=256, keeps=keep_p, mod_per_row=False, dils=dils)
    o_parts, lse_parts = [], []
    for gi, qg in enumerate((q0, q1, q2)):
        o, lse = _attn_call(qg, gi)
        o_parts.append(o)
        lse_parts.append(lse)
    ob_p, wkv_p = _wkv_call(feat_p, p)
    x1_p, hp_p, wt_p = _post_call(o_parts, lse_parts, ob_p, gates_p, x_prompt, mod_p[2], mod_p[4], mod_p[3],
                                  p, wa, wb, wo, wrt, rb, tm=512, mod_per_row=False)

    xs3 = x_sample.reshape(1, nd, D_MODEL)
    s0, s1, s2, feat_s, gates_s, *tails_s = _inproj_call(
        xs3, norm_pre_mix[l], mod_s[1], mod_s[0], cos_s, sin_s, wq, wf, wg,
        tm=nd, keeps=(nd,) * N_GROUPS_A, mod_per_row=True, dils=(1, 1, 1))
    qkv_s = jnp.stack([z.reshape(nd, 3, N_HEADS_A, HEAD_DIM_A) for z in (s0, s1, s2)], axis=2)
    qkv_s = qkv_s.reshape(nd, 3 * N_GROUPS_A * N_HEADS_A, HEAD_DIM_A).astype(F32)
    oa_s = _sattn_call(qkv_s, cache_a1_kv[l], cache_a2_kv[l], cache_a3_kv[l])
    r_s, w_s, k_s, v_s, aa_s, bb_s, g_s = _swkv_prep_call(feat_s[0], state_b_shift[l], p)
    nh = nd * N_HEADS_B
    as_row = lambda a: a.reshape(nh, 1, HEAD_DIM_B)
    s_new, y_col = _swkv_step_call(state_b_wkv[l].reshape(nh, HEAD_DIM_B, HEAD_DIM_B), as_row(aa_s), as_row(w_s),
                                   as_row(bb_s), as_row(k_s), as_row(r_s), v_s.reshape(nh, HEAD_DIM_B))
    ob_s = _swkv_fin_call(y_col.reshape(nd, D_B), r_s, k_s, v_s, g_s, p)
    x1_s, hp_s, wt_s = _post_call([oa_s.reshape(1, nd, D_GROUP_A)], None, ob_s.reshape(1, nd, D_B), gates_s, xs3,
                                  mod_s[2], mod_s[4], mod_s[3], p, wa, wb, wo, wrt, rb, tm=nd, mod_per_row=True)

    n_p = BATCH * SEQ
    n_real = n_p + nd
    n_all = -(-n_real // MOE_TILE) * MOE_TILE
    pad = n_all - n_real
    n_blocks = -(-(n_real * TOP_K) // EXPERT_BLOCK) + N_EXPERTS
    n_blocks_pad = -(-n_blocks // LANES) * LANES
    assert n_p % MOE_TILE == 0 and nd <= MOE_TILE
    hp_a = hp_p.reshape(n_p * ROW_TILE_SUBLANES, LANES)
    hp_b = jnp.concatenate([hp_s[0], jnp.zeros((pad * ROW_TILE_SUBLANES, LANES), I32)], axis=0)
    wt_all = jnp.concatenate([wt_p, wt_s, jnp.full((N_EXPERTS, pad), -1.0, F32)], axis=1)
    dest8, w8, tab, etab = _rank_call(wt_all, n_real, n_blocks, n_blocks_pad)
    dest = dest8.reshape(TOP_K, n_all // MOE_TILE, MOE_TILE).transpose(1, 0, 2).reshape(-1)
    xs = _dispatch_call(dest, etab, hp_a, hp_b, n_real, n_blocks * EXPERT_BLOCK)
    ys = _ffn_call(tab[0], tab[1, :1], xs, w_e_gate[l], w_e_up[l], w_e_down[l], n_blocks)
    n_ct = n_p // COMBINE_TILE + 1
    dest_c = dest8[:, :n_ct * COMBINE_TILE].reshape(TOP_K, n_ct, COMBINE_TILE).transpose(1, 0, 2).reshape(-1)
    pad_rows = lambda z: jnp.concatenate([z, jnp.zeros((COMBINE_TILE - nd, D_MODEL), F32)], axis=0)
    out_p, out_s = _combine_call(dest_c, w8, hp_a, hp_b, x1_p.reshape(n_p, D_MODEL), pad_rows(x1_s[0]), mod_p[5],
                                 pad_rows(mod_s[5][0]), norm_post_ffn[l], wsg, wsu, wsd, ys)
    y_prompt = out_p.reshape(BATCH, SEQ, D_MODEL)
    y_sample = out_s[:nd]

    a_p = [z.reshape(1, BATCH, kp, 2, N_HEADS_A, HEAD_DIM_A) for z, kp in zip(tails_p, keep_p)]
    a_s = [z.reshape(1, nd, DEC_SEQ, 2, N_HEADS_A, HEAD_DIM_A) for z in tails_s]
    shift_p = feat_p[:, -1][None]
    shift_s = feat_s[0][None]
    return (y_prompt, y_sample.reshape(nd, DEC_SEQ, D_MODEL), a_p[0], a_p[1], a_p[2], wkv_p[None], shift_p,
            a_s[0], a_s[1], a_s[2], s_new.reshape(1, nd, N_HEADS_B, HEAD_DIM_B, HEAD_DIM_B), shift_s)
```
